```python
import math
import jax, jax.numpy as jnp
from jax import lax
import numpy as np

D_MODEL = 1024
BATCH = 8
SEQ = 4096
DEPTH = 2

D_FF = 2816
HEAD_DIM = 64
W_A = 256
A_BLOCKS = 4
A_BLOCK_W = W_A // A_BLOCKS
LRU_C = 8.0
LRU_CONV = 4
N_Q_HEADS = 8
N_KV_HEADS = 2
W_B = N_Q_HEADS * HEAD_DIM
WINDOW = 128
BLK = 128
W_C = 256
C_GROUPS = 4
C_CONV = 31
D_MIX = W_A + W_B + W_C
OFF_LRU_X = 0
OFF_LRU_GATE = OFF_LRU_X + W_A
OFF_Q = OFF_LRU_GATE + W_A
OFF_K = OFF_Q + W_B
OFF_V = OFF_K + N_KV_HEADS * HEAD_DIM
OFF_GLU = OFF_V + N_KV_HEADS * HEAD_DIM
D_IN_PROJ = OFF_GLU + 2 * W_C
NORM_EPS = 1e-6
LN_EPS = 1e-5
NEG_BIG = -1e30

kernel_name = "hymba_style_lru_swa_conformer_macaron"


def rms_norm(x, g):
    xf = x.astype(jnp.float32)
    y = xf * lax.rsqrt(jnp.mean(xf * xf, axis=-1, keepdims=True) + NORM_EPS)
    return (y * g.astype(jnp.float32)).astype(x.dtype)


def layer_norm(x, g, b):
    xf = x.astype(jnp.float32)
    mu = jnp.mean(xf, axis=-1, keepdims=True)
    xc = xf - mu
    var = jnp.mean(xc * xc, axis=-1, keepdims=True)
    y = xc * lax.rsqrt(var + LN_EPS) * g.astype(jnp.float32) + b.astype(jnp.float32)
    return y.astype(x.dtype)


def swiglu(x, w_gu, w_down):
    g, u = jnp.split(x @ w_gu, 2, axis=-1)
    return (jax.nn.silu(g) * u) @ w_down


def causal_depthwise_conv(x, w, b):
    k = w.shape[0]
    y = lax.conv_general_dilated(
        x, w[:, None, :], window_strides=(1,), padding=[(k - 1, 0)],
        dimension_numbers=("NWC", "WIO", "NWC"), feature_group_count=x.shape[-1])
    return y + b


def rg_lru(x, w_a, b_a, w_x, b_x, lam):
    bsz, s, w = x.shape
    xb = x.reshape(bsz, s, A_BLOCKS, A_BLOCK_W)
    r = jax.nn.sigmoid(jnp.einsum("bshi,hij->bshj", xb, w_a).reshape(bsz, s, w) + b_a)
    i = jax.nn.sigmoid(jnp.einsum("bshi,hij->bshj", xb, w_x).reshape(bsz, s, w) + b_x)
    log_a = -LRU_C * r.astype(jnp.float32) * jax.nn.softplus(-lam.astype(jnp.float32))
    a = jnp.exp(log_a)
    u = jnp.sqrt(-jnp.expm1(2.0 * log_a)) * (i * x).astype(jnp.float32)

    def combine(left, right):
        a1, b1 = left
        a2, b2 = right
        return a1 * a2, a2 * b1 + b2

    _, h = lax.associative_scan(combine, (a, u), axis=1)
    return h.astype(x.dtype)


def sliding_window_attention_sinks(q, k, v, sinks):
    bsz, s, h, d = q.shape
    kvh = k.shape[2]
    grp = h // kvh
    nblk = s // BLK
    qb = q.reshape(bsz, nblk, BLK, kvh, grp, d)

    def banded(t):
        cur = t.reshape(bsz, nblk, BLK, kvh, d)
        prev = jnp.pad(t, ((0, 0), (BLK, 0), (0, 0), (0, 0)))[:, :s].reshape(bsz, nblk, BLK, kvh, d)
        return jnp.concatenate([prev, cur], axis=2)

    kw, vw = banded(k), banded(v)
    scores = jnp.einsum("bnqkgd,bnjkd->bnkgqj", qb, kw).astype(jnp.float32) * (1.0 / math.sqrt(d))
    qi = jnp.arange(BLK)[:, None]
    kj = jnp.arange(2 * BLK)[None, :]
    rel = BLK + qi - kj
    k_pos = (jnp.arange(nblk)[:, None, None] - 1) * BLK + kj[None]
    mask = (rel >= 0)[None] & (rel < WINDOW)[None] & (k_pos >= 0)
    scores = jnp.where(mask[None, :, None, None], scores, NEG_BIG)
    sink = sinks.astype(jnp.float32).reshape(1, 1, kvh, grp, 1, 1)
    m = jnp.maximum(jnp.max(scores, axis=-1, keepdims=True), sink)
    p = jnp.exp(scores - m)
    p = p / (jnp.sum(p, axis=-1, keepdims=True) + jnp.exp(sink - m))
    o = jnp.einsum("bnkgqj,bnjkd->bnqkgd", p.astype(v.dtype), vw)
    return o.reshape(bsz, s, h * d)


def conformer_conv(glu_in, w, b, ln_g, ln_b):
    a, g = jnp.split(glu_in, 2, axis=-1)
    y = a * jax.nn.sigmoid(g)
    y = causal_depthwise_conv(y, w, b)
    y = layer_norm(y, ln_g, ln_b)
    return jax.nn.silu(y)


def _fwd_setup_inputs(seed: int = 0) -> dict:
    key = jax.random.key(seed)
    ks = iter(jax.random.split(key, 40))
    L = DEPTH

    def nrm(shape, scale):
        return scale * jax.random.normal(next(ks), shape, jnp.float32)

    def gain(shape):
        return 1.0 + 0.05 * jax.random.normal(next(ks), shape, jnp.float32)

    a0 = jax.random.uniform(next(ks), (L, W_A), jnp.float32, minval=0.9, maxval=0.999)
    return {
        "x": jax.random.normal(next(ks), (BATCH, SEQ, D_MODEL), jnp.float32),
        "ffn1_pre_g": gain((L, D_MODEL)),
        "ffn1_w_gu": nrm((L, D_MODEL, 2 * D_FF), D_MODEL ** -0.5),
        "ffn1_w_down": nrm((L, D_FF, D_MODEL), D_FF ** -0.5),
        "ffn1_post_g": gain((L, D_MODEL)),
        "mix_pre_g": gain((L, D_MODEL)),
        "w_in": nrm((L, D_MODEL, D_IN_PROJ), D_MODEL ** -0.5),
        "lru_conv_w": nrm((L, LRU_CONV, W_A), LRU_CONV ** -0.5),
        "lru_conv_b": nrm((L, W_A), 0.02),
        "lru_w_a": nrm((L, A_BLOCKS, A_BLOCK_W, A_BLOCK_W), A_BLOCK_W ** -0.5),
        "lru_b_a": nrm((L, W_A), 0.02),
        "lru_w_x": nrm((L, A_BLOCKS, A_BLOCK_W, A_BLOCK_W), A_BLOCK_W ** -0.5),
        "lru_b_x": nrm((L, W_A), 0.02),
        "lru_lambda": jnp.log(a0) - jnp.log1p(-a0),
        "attn_sinks": nrm((L, N_Q_HEADS), 0.5),
        "conv_w": nrm((L, C_CONV, W_C), C_CONV ** -0.5),
        "conv_b": nrm((L, W_C), 0.02),
        "conv_ln_g": gain((L, W_C)),
        "conv_ln_b": nrm((L, W_C), 0.02),
        "group_g": gain((L, D_MIX)),
        "w_out": nrm((L, D_MIX, D_MODEL), D_MIX ** -0.5),
        "mix_post_g": gain((L, D_MODEL)),
        "ffn2_pre_g": gain((L, D_MODEL)),
        "ffn2_w_gu": nrm((L, D_MODEL, 2 * D_FF), D_MODEL ** -0.5),
        "ffn2_w_down": nrm((L, D_FF, D_MODEL), D_FF ** -0.5),
        "ffn2_post_g": gain((L, D_MODEL)),
    }


def _fwd_reference(x, ffn1_pre_g, ffn1_w_gu, ffn1_w_down, ffn1_post_g, mix_pre_g, w_in,
              lru_conv_w, lru_conv_b, lru_w_a, lru_b_a, lru_w_x, lru_b_x, lru_lambda,
              attn_sinks, conv_w, conv_b, conv_ln_g, conv_ln_b, group_g, w_out,
              mix_post_g, ffn2_pre_g, ffn2_w_gu, ffn2_w_down, ffn2_post_g):
    bsz, s, _ = x.shape
    for l in range(DEPTH):
        x = x + 0.5 * rms_norm(swiglu(rms_norm(x, ffn1_pre_g[l]), ffn1_w_gu[l], ffn1_w_down[l]), ffn1_post_g[l])

        hn = rms_norm(x, mix_pre_g[l])
        proj = hn @ w_in[l]
        lru_x = proj[..., OFF_LRU_X:OFF_LRU_GATE]
        lru_gate = proj[..., OFF_LRU_GATE:OFF_Q]
        q = proj[..., OFF_Q:OFF_K].reshape(bsz, s, N_Q_HEADS, HEAD_DIM)
        k = proj[..., OFF_K:OFF_V].reshape(bsz, s, N_KV_HEADS, HEAD_DIM)
        v = proj[..., OFF_V:OFF_GLU].reshape(bsz, s, N_KV_HEADS, HEAD_DIM)
        glu_in = proj[..., OFF_GLU:]

        y_a = jax.nn.gelu(lru_gate) * rg_lru(
            causal_depthwise_conv(lru_x, lru_conv_w[l], lru_conv_b[l]),
            lru_w_a[l], lru_b_a[l], lru_w_x[l], lru_b_x[l], lru_lambda[l])
        y_b = sliding_window_attention_sinks(q, k, v, attn_sinks[l])
        y_c = conformer_conv(glu_in, conv_w[l], conv_b[l], conv_ln_g[l], conv_ln_b[l])

        gg = group_g[l]
        y = jnp.concatenate([
            rms_norm(y_a, gg[:W_A]),
            rms_norm(y_b, gg[W_A:W_A + W_B]),
            rms_norm(y_c, gg[W_A + W_B:]),
        ], axis=-1)
        x = x + rms_norm(y @ w_out[l], mix_post_g[l])

        x = x + 0.5 * rms_norm(swiglu(rms_norm(x, ffn2_pre_g[l]), ffn2_w_gu[l], ffn2_w_down[l]), ffn2_post_g[l])
    return x


import jax as _jax
import jax.numpy as _jnp

TWIN_FORMAT = 'train_step'
FWD_PARAMS = ['x', 'ffn1_pre_g', 'ffn1_w_gu', 'ffn1_w_down', 'ffn1_post_g', 'mix_pre_g', 'w_in', 'lru_conv_w', 'lru_conv_b', 'lru_w_a', 'lru_b_a', 'lru_w_x', 'lru_b_x', 'lru_lambda', 'attn_sinks', 'conv_w', 'conv_b', 'conv_ln_g', 'conv_ln_b', 'group_g', 'w_out', 'mix_post_g', 'ffn2_pre_g', 'ffn2_w_gu', 'ffn2_w_down', 'ffn2_post_g']
TWIN_WEIGHTS = ['ffn1_pre_g', 'ffn1_w_gu', 'ffn1_w_down', 'ffn1_post_g', 'mix_pre_g', 'w_in', 'lru_conv_w', 'lru_conv_b', 'lru_w_a', 'lru_b_a', 'lru_w_x', 'lru_b_x', 'lru_lambda', 'attn_sinks', 'conv_w', 'conv_b', 'conv_ln_g', 'conv_ln_b', 'group_g', 'w_out', 'mix_post_g', 'ffn2_pre_g', 'ffn2_w_gu', 'ffn2_w_down', 'ffn2_post_g']
TWIN_DIFF_INPUT = 'x'
TWIN_INPUTS = ['x', 'ffn1_pre_g', 'ffn1_w_gu', 'ffn1_w_down', 'ffn1_post_g', 'mix_pre_g', 'w_in', 'lru_conv_w', 'lru_conv_b', 'lru_w_a', 'lru_b_a', 'lru_w_x', 'lru_b_x', 'lru_lambda', 'attn_sinks', 'conv_w', 'conv_b', 'conv_ln_g', 'conv_ln_b', 'group_g', 'w_out', 'mix_post_g', 'ffn2_pre_g', 'ffn2_w_gu', 'ffn2_w_down', 'ffn2_post_g', 'loss_target', 'm_ffn1_pre_g', 'm_ffn1_w_gu', 'm_ffn1_w_down', 'm_ffn1_post_g', 'm_mix_pre_g', 'm_w_in', 'm_lru_conv_w', 'm_lru_conv_b', 'm_lru_w_a', 'm_lru_b_a', 'm_lru_w_x', 'm_lru_b_x', 'm_lru_lambda', 'm_attn_sinks', 'm_conv_w', 'm_conv_b', 'm_conv_ln_g', 'm_conv_ln_b', 'm_group_g', 'm_w_out', 'm_mix_post_g', 'm_ffn2_pre_g', 'm_ffn2_w_gu', 'm_ffn2_w_down', 'm_ffn2_post_g', 'v_ffn1_pre_g', 'v_ffn1_w_gu', 'v_ffn1_w_down', 'v_ffn1_post_g', 'v_mix_pre_g', 'v_w_in', 'v_lru_conv_w', 'v_lru_conv_b', 'v_lru_w_a', 'v_lru_b_a', 'v_lru_w_x', 'v_lru_b_x', 'v_lru_lambda', 'v_attn_sinks', 'v_conv_w', 'v_conv_b', 'v_conv_ln_g', 'v_conv_ln_b', 'v_group_g', 'v_w_out', 'v_mix_post_g', 'v_ffn2_pre_g', 'v_ffn2_w_gu', 'v_ffn2_w_down', 'v_ffn2_post_g']
TWIN_OUTPUTS = ['loss', 'grad_x', 'grad_ffn1_pre_g', 'grad_ffn1_w_gu', 'grad_ffn1_w_down', 'grad_ffn1_post_g', 'grad_mix_pre_g', 'grad_w_in', 'grad_lru_conv_w', 'grad_lru_conv_b', 'grad_lru_w_a', 'grad_lru_b_a', 'grad_lru_w_x', 'grad_lru_b_x', 'grad_lru_lambda', 'grad_attn_sinks', 'grad_conv_w', 'grad_conv_b', 'grad_conv_ln_g', 'grad_conv_ln_b', 'grad_group_g', 'grad_w_out', 'grad_mix_post_g', 'grad_ffn2_pre_g', 'grad_ffn2_w_gu', 'grad_ffn2_w_down', 'grad_ffn2_post_g', 'delta_ffn1_pre_g', 'delta_ffn1_w_gu', 'delta_ffn1_w_down', 'delta_ffn1_post_g', 'delta_mix_pre_g', 'delta_w_in', 'delta_lru_conv_w', 'delta_lru_conv_b', 'delta_lru_w_a', 'delta_lru_b_a', 'delta_lru_w_x', 'delta_lru_b_x', 'delta_lru_lambda', 'delta_attn_sinks', 'delta_conv_w', 'delta_conv_b', 'delta_conv_ln_g', 'delta_conv_ln_b', 'delta_group_g', 'delta_w_out', 'delta_mix_post_g', 'delta_ffn2_pre_g', 'delta_ffn2_w_gu', 'delta_ffn2_w_down', 'delta_ffn2_post_g', 'new_m_ffn1_pre_g', 'new_m_ffn1_w_gu', 'new_m_ffn1_w_down', 'new_m_ffn1_post_g', 'new_m_mix_pre_g', 'new_m_w_in', 'new_m_lru_conv_w', 'new_m_lru_conv_b', 'new_m_lru_w_a', 'new_m_lru_b_a', 'new_m_lru_w_x', 'new_m_lru_b_x', 'new_m_lru_lambda', 'new_m_attn_sinks', 'new_m_conv_w', 'new_m_conv_b', 'new_m_conv_ln_g', 'new_m_conv_ln_b', 'new_m_group_g', 'new_m_w_out', 'new_m_mix_post_g', 'new_m_ffn2_pre_g', 'new_m_ffn2_w_gu', 'new_m_ffn2_w_down', 'new_m_ffn2_post_g', 'new_v_ffn1_pre_g', 'new_v_ffn1_w_gu', 'new_v_ffn1_w_down', 'new_v_ffn1_post_g', 'new_v_mix_pre_g', 'new_v_w_in', 'new_v_lru_conv_w', 'new_v_lru_conv_b', 'new_v_lru_w_a', 'new_v_lru_b_a', 'new_v_lru_w_x', 'new_v_lru_b_x', 'new_v_lru_lambda', 'new_v_attn_sinks', 'new_v_conv_w', 'new_v_conv_b', 'new_v_conv_ln_g', 'new_v_conv_ln_b', 'new_v_group_g', 'new_v_w_out', 'new_v_mix_post_g', 'new_v_ffn2_pre_g', 'new_v_ffn2_w_gu', 'new_v_ffn2_w_down', 'new_v_ffn2_post_g']
TWIN_LEAF_KINDS = {'loss': 'loss', 'grad_x': 'grad_x', 'grad_ffn1_pre_g': 'grad_w', 'grad_ffn1_w_gu': 'grad_w', 'grad_ffn1_w_down': 'grad_w', 'grad_ffn1_post_g': 'grad_w', 'grad_mix_pre_g': 'grad_w', 'grad_w_in': 'grad_w', 'grad_lru_conv_w': 'grad_w', 'grad_lru_conv_b': 'grad_w', 'grad_lru_w_a': 'grad_w', 'grad_lru_b_a': 'grad_w', 'grad_lru_w_x': 'grad_w', 'grad_lru_b_x': 'grad_w', 'grad_lru_lambda': 'grad_w', 'grad_attn_sinks': 'grad_w', 'grad_conv_w': 'grad_w', 'grad_conv_b': 'grad_w', 'grad_conv_ln_g': 'grad_w', 'grad_conv_ln_b': 'grad_w', 'grad_group_g': 'grad_w', 'grad_w_out': 'grad_w', 'grad_mix_post_g': 'grad_w', 'grad_ffn2_pre_g': 'grad_w', 'grad_ffn2_w_gu': 'grad_w', 'grad_ffn2_w_down': 'grad_w', 'grad_ffn2_post_g': 'grad_w', 'delta_ffn1_pre_g': 'delta_w', 'delta_ffn1_w_gu': 'delta_w', 'delta_ffn1_w_down': 'delta_w', 'delta_ffn1_post_g': 'delta_w', 'delta_mix_pre_g': 'delta_w', 'delta_w_in': 'delta_w', 'delta_lru_conv_w': 'delta_w', 'delta_lru_conv_b': 'delta_w', 'delta_lru_w_a': 'delta_w', 'delta_lru_b_a': 'delta_w', 'delta_lru_w_x': 'delta_w', 'delta_lru_b_x': 'delta_w', 'delta_lru_lambda': 'delta_w', 'delta_attn_sinks': 'delta_w', 'delta_conv_w': 'delta_w', 'delta_conv_b': 'delta_w', 'delta_conv_ln_g': 'delta_w', 'delta_conv_ln_b': 'delta_w', 'delta_group_g': 'delta_w', 'delta_w_out': 'delta_w', 'delta_mix_post_g': 'delta_w', 'delta_ffn2_pre_g': 'delta_w', 'delta_ffn2_w_gu': 'delta_w', 'delta_ffn2_w_down': 'delta_w', 'delta_ffn2_post_g': 'delta_w', 'new_m_ffn1_pre_g': 'new_m', 'new_m_ffn1_w_gu': 'new_m', 'new_m_ffn1_w_down': 'new_m', 'new_m_ffn1_post_g': 'new_m', 'new_m_mix_pre_g': 'new_m', 'new_m_w_in': 'new_m', 'new_m_lru_conv_w': 'new_m', 'new_m_lru_conv_b': 'new_m', 'new_m_lru_w_a': 'new_m', 'new_m_lru_b_a': 'new_m', 'new_m_lru_w_x': 'new_m', 'new_m_lru_b_x': 'new_m', 'new_m_lru_lambda': 'new_m', 'new_m_attn_sinks': 'new_m', 'new_m_conv_w': 'new_m', 'new_m_conv_b': 'new_m', 'new_m_conv_ln_g': 'new_m', 'new_m_conv_ln_b': 'new_m', 'new_m_group_g': 'new_m', 'new_m_w_out': 'new_m', 'new_m_mix_post_g': 'new_m', 'new_m_ffn2_pre_g': 'new_m', 'new_m_ffn2_w_gu': 'new_m', 'new_m_ffn2_w_down': 'new_m', 'new_m_ffn2_post_g': 'new_m', 'new_v_ffn1_pre_g': 'new_v', 'new_v_ffn1_w_gu': 'new_v', 'new_v_ffn1_w_down': 'new_v', 'new_v_ffn1_post_g': 'new_v', 'new_v_mix_pre_g': 'new_v', 'new_v_w_in': 'new_v', 'new_v_lru_conv_w': 'new_v', 'new_v_lru_conv_b': 'new_v', 'new_v_lru_w_a': 'new_v', 'new_v_lru_b_a': 'new_v', 'new_v_lru_w_x': 'new_v', 'new_v_lru_b_x': 'new_v', 'new_v_lru_lambda': 'new_v', 'new_v_attn_sinks': 'new_v', 'new_v_conv_w': 'new_v', 'new_v_conv_b': 'new_v', 'new_v_conv_ln_g': 'new_v', 'new_v_conv_ln_b': 'new_v', 'new_v_group_g': 'new_v', 'new_v_w_out': 'new_v', 'new_v_mix_post_g': 'new_v', 'new_v_ffn2_pre_g': 'new_v', 'new_v_ffn2_w_gu': 'new_v', 'new_v_ffn2_w_down': 'new_v', 'new_v_ffn2_post_g': 'new_v'}


def _forward(args):
    return _fwd_reference(*[args[k] for k in FWD_PARAMS])


def _output_shape():
    out = _jax.eval_shape(lambda: _forward(_fwd_setup_inputs(0)))
    return out.shape, out.dtype

N_MICROBATCH = 1
ADAM_LR = 0.001
ADAM_B1 = 0.9
ADAM_B2 = 0.999
ADAM_EPS = 1e-08
ADAM_WD = 0.01
ADAM_STEP = 10
PER_EXAMPLE_BATCH_AXIS = {'x': 0, 'loss_target': 0}
SHARED_INPUTS = []
_WEIGHT_DTYPES = {'ffn1_pre_g': _jnp.float32, 'ffn1_w_gu': _jnp.float32, 'ffn1_w_down': _jnp.float32, 'ffn1_post_g': _jnp.float32, 'mix_pre_g': _jnp.float32, 'w_in': _jnp.float32, 'lru_conv_w': _jnp.float32, 'lru_conv_b': _jnp.float32, 'lru_w_a': _jnp.float32, 'lru_b_a': _jnp.float32, 'lru_w_x': _jnp.float32, 'lru_b_x': _jnp.float32, 'lru_lambda': _jnp.float32, 'attn_sinks': _jnp.float32, 'conv_w': _jnp.float32, 'conv_b': _jnp.float32, 'conv_ln_g': _jnp.float32, 'conv_ln_b': _jnp.float32, 'group_g': _jnp.float32, 'w_out': _jnp.float32, 'mix_post_g': _jnp.float32, 'ffn2_pre_g': _jnp.float32, 'ffn2_w_gu': _jnp.float32, 'ffn2_w_down': _jnp.float32, 'ffn2_post_g': _jnp.float32}
MOMENT_SCALE = {'ffn1_pre_g': 1.096795e+00, 'ffn1_w_gu': 4.078287e-01, 'ffn1_w_down': 7.020059e-01, 'ffn1_post_g': 7.801161e+00, 'mix_pre_g': 1.444779e+00, 'w_in': 1.110448e+00, 'lru_conv_w': 1.363035e+00, 'lru_conv_b': 2.345027e+01, 'lru_w_a': 6.295094e-01, 'lru_b_a': 3.482262e-01, 'lru_w_x': 1.287853e+00, 'lru_b_x': 4.234821e-01, 'lru_lambda': 6.113155e-01, 'attn_sinks': 2.815872e-01, 'conv_w': 9.339333e-01, 'conv_b': 1.410046e+01, 'conv_ln_g': 5.630834e+00, 'conv_ln_b': 8.628237e+00, 'group_g': 2.073874e+00, 'w_out': 2.046213e+00, 'mix_post_g': 3.254144e+01, 'ffn2_pre_g': 5.961779e-01, 'ffn2_w_gu': 2.484720e-01, 'ffn2_w_down': 4.809340e-01, 'ffn2_post_g': 7.995985e+00}


def _to_microbatches(a, axis):
    t = _jnp.moveaxis(a, axis, 0)
    t = t.reshape((N_MICROBATCH, t.shape[0] // N_MICROBATCH) + t.shape[1:])
    return _jnp.moveaxis(t, 1, axis + 1)


def setup_inputs(seed: int = 0) -> dict:
    inp = _fwd_setup_inputs(seed)
    key = _jax.random.fold_in(_jax.random.key(seed), 7919)
    shape, _ = _output_shape()
    out = dict(inp)
    out["loss_target"] = _jax.random.normal(_jax.random.fold_in(key, 0), shape, _jnp.float32)
    for i, name in enumerate(TWIN_WEIGHTS):
        w = inp[name].astype(_jnp.float32)
        if MOMENT_SCALE is None:
            s = _jnp.sqrt(_jnp.mean(_jnp.square(w)) + 1e-30)
        else:
            s = MOMENT_SCALE[name]
        km, kv = _jax.random.split(_jax.random.fold_in(key, i + 1))
        out[name] = w
        out["m_" + name] = s * _jax.random.normal(km, w.shape, _jnp.float32)
        out["v_" + name] = (s * s) * _jax.random.uniform(kv, w.shape, _jnp.float32, 0.5, 1.5)
    if N_MICROBATCH > 1:
        for name, axis in PER_EXAMPLE_BATCH_AXIS.items():
            out[name] = _to_microbatches(out[name], axis)
    return {'x': out['x'], 'ffn1_pre_g': out['ffn1_pre_g'], 'ffn1_w_gu': out['ffn1_w_gu'], 'ffn1_w_down': out['ffn1_w_down'], 'ffn1_post_g': out['ffn1_post_g'], 'mix_pre_g': out['mix_pre_g'], 'w_in': out['w_in'], 'lru_conv_w': out['lru_conv_w'], 'lru_conv_b': out['lru_conv_b'], 'lru_w_a': out['lru_w_a'], 'lru_b_a': out['lru_b_a'], 'lru_w_x': out['lru_w_x'], 'lru_b_x': out['lru_b_x'], 'lru_lambda': out['lru_lambda'], 'attn_sinks': out['attn_sinks'], 'conv_w': out['conv_w'], 'conv_b': out['conv_b'], 'conv_ln_g': out['conv_ln_g'], 'conv_ln_b': out['conv_ln_b'], 'group_g': out['group_g'], 'w_out': out['w_out'], 'mix_post_g': out['mix_post_g'], 'ffn2_pre_g': out['ffn2_pre_g'], 'ffn2_w_gu': out['ffn2_w_gu'], 'ffn2_w_down': out['ffn2_w_down'], 'ffn2_post_g': out['ffn2_post_g'], 'loss_target': out['loss_target'], 'm_ffn1_pre_g': out['m_ffn1_pre_g'], 'm_ffn1_w_gu': out['m_ffn1_w_gu'], 'm_ffn1_w_down': out['m_ffn1_w_down'], 'm_ffn1_post_g': out['m_ffn1_post_g'], 'm_mix_pre_g': out['m_mix_pre_g'], 'm_w_in': out['m_w_in'], 'm_lru_conv_w': out['m_lru_conv_w'], 'm_lru_conv_b': out['m_lru_conv_b'], 'm_lru_w_a': out['m_lru_w_a'], 'm_lru_b_a': out['m_lru_b_a'], 'm_lru_w_x': out['m_lru_w_x'], 'm_lru_b_x': out['m_lru_b_x'], 'm_lru_lambda': out['m_lru_lambda'], 'm_attn_sinks': out['m_attn_sinks'], 'm_conv_w': out['m_conv_w'], 'm_conv_b': out['m_conv_b'], 'm_conv_ln_g': out['m_conv_ln_g'], 'm_conv_ln_b': out['m_conv_ln_b'], 'm_group_g': out['m_group_g'], 'm_w_out': out['m_w_out'], 'm_mix_post_g': out['m_mix_post_g'], 'm_ffn2_pre_g': out['m_ffn2_pre_g'], 'm_ffn2_w_gu': out['m_ffn2_w_gu'], 'm_ffn2_w_down': out['m_ffn2_w_down'], 'm_ffn2_post_g': out['m_ffn2_post_g'], 'v_ffn1_pre_g': out['v_ffn1_pre_g'], 'v_ffn1_w_gu': out['v_ffn1_w_gu'], 'v_ffn1_w_down': out['v_ffn1_w_down'], 'v_ffn1_post_g': out['v_ffn1_post_g'], 'v_mix_pre_g': out['v_mix_pre_g'], 'v_w_in': out['v_w_in'], 'v_lru_conv_w': out['v_lru_conv_w'], 'v_lru_conv_b': out['v_lru_conv_b'], 'v_lru_w_a': out['v_lru_w_a'], 'v_lru_b_a': out['v_lru_b_a'], 'v_lru_w_x': out['v_lru_w_x'], 'v_lru_b_x': out['v_lru_b_x'], 'v_lru_lambda': out['v_lru_lambda'], 'v_attn_sinks': out['v_attn_sinks'], 'v_conv_w': out['v_conv_w'], 'v_conv_b': out['v_conv_b'], 'v_conv_ln_g': out['v_conv_ln_g'], 'v_conv_ln_b': out['v_conv_ln_b'], 'v_group_g': out['v_group_g'], 'v_w_out': out['v_w_out'], 'v_mix_post_g': out['v_mix_post_g'], 'v_ffn2_pre_g': out['v_ffn2_pre_g'], 'v_ffn2_w_gu': out['v_ffn2_w_gu'], 'v_ffn2_w_down': out['v_ffn2_w_down'], 'v_ffn2_post_g': out['v_ffn2_post_g']}


def _loss(weights, diff, rest, loss_target):
    with _jax.named_scope("forward"):
        args = {**rest, TWIN_DIFF_INPUT: diff, **{k: w.astype(_WEIGHT_DTYPES[k]) for k, w in weights.items()}}
        y = _forward(args)
    with _jax.named_scope("loss_head"):
        err = _jnp.square(y.astype(_jnp.float32) - loss_target)
        return 0.5 * _jnp.sum(_jnp.mean(err, axis=-1)) if err.ndim else 0.5 * err


def _adamw(w, g, m, v):
    m = ADAM_B1 * m + (1.0 - ADAM_B1) * g
    v = ADAM_B2 * v + (1.0 - ADAM_B2) * _jnp.square(g)
    m_hat = m / (1.0 - ADAM_B1 ** ADAM_STEP)
    v_hat = v / (1.0 - ADAM_B2 ** ADAM_STEP)
    delta = -ADAM_LR * (m_hat / (_jnp.sqrt(v_hat) + ADAM_EPS) + ADAM_WD * w)
    return delta, m, v


def reference(x, ffn1_pre_g, ffn1_w_gu, ffn1_w_down, ffn1_post_g, mix_pre_g, w_in, lru_conv_w, lru_conv_b, lru_w_a, lru_b_a, lru_w_x, lru_b_x, lru_lambda, attn_sinks, conv_w, conv_b, conv_ln_g, conv_ln_b, group_g, w_out, mix_post_g, ffn2_pre_g, ffn2_w_gu, ffn2_w_down, ffn2_post_g, loss_target, m_ffn1_pre_g, m_ffn1_w_gu, m_ffn1_w_down, m_ffn1_post_g, m_mix_pre_g, m_w_in, m_lru_conv_w, m_lru_conv_b, m_lru_w_a, m_lru_b_a, m_lru_w_x, m_lru_b_x, m_lru_lambda, m_attn_sinks, m_conv_w, m_conv_b, m_conv_ln_g, m_conv_ln_b, m_group_g, m_w_out, m_mix_post_g, m_ffn2_pre_g, m_ffn2_w_gu, m_ffn2_w_down, m_ffn2_post_g, v_ffn1_pre_g, v_ffn1_w_gu, v_ffn1_w_down, v_ffn1_post_g, v_mix_pre_g, v_w_in, v_lru_conv_w, v_lru_conv_b, v_lru_w_a, v_lru_b_a, v_lru_w_x, v_lru_b_x, v_lru_lambda, v_attn_sinks, v_conv_w, v_conv_b, v_conv_ln_g, v_conv_ln_b, v_group_g, v_w_out, v_mix_post_g, v_ffn2_pre_g, v_ffn2_w_gu, v_ffn2_w_down, v_ffn2_post_g):
    given = dict(x=x, ffn1_pre_g=ffn1_pre_g, ffn1_w_gu=ffn1_w_gu, ffn1_w_down=ffn1_w_down, ffn1_post_g=ffn1_post_g, mix_pre_g=mix_pre_g, w_in=w_in, lru_conv_w=lru_conv_w, lru_conv_b=lru_conv_b, lru_w_a=lru_w_a, lru_b_a=lru_b_a, lru_w_x=lru_w_x, lru_b_x=lru_b_x, lru_lambda=lru_lambda, attn_sinks=attn_sinks, conv_w=conv_w, conv_b=conv_b, conv_ln_g=conv_ln_g, conv_ln_b=conv_ln_b, group_g=group_g, w_out=w_out, mix_post_g=mix_post_g, ffn2_pre_g=ffn2_pre_g, ffn2_w_gu=ffn2_w_gu, ffn2_w_down=ffn2_w_down, ffn2_post_g=ffn2_post_g, loss_target=loss_target, m_ffn1_pre_g=m_ffn1_pre_g, m_ffn1_w_gu=m_ffn1_w_gu, m_ffn1_w_down=m_ffn1_w_down, m_ffn1_post_g=m_ffn1_post_g, m_mix_pre_g=m_mix_pre_g, m_w_in=m_w_in, m_lru_conv_w=m_lru_conv_w, m_lru_conv_b=m_lru_conv_b, m_lru_w_a=m_lru_w_a, m_lru_b_a=m_lru_b_a, m_lru_w_x=m_lru_w_x, m_lru_b_x=m_lru_b_x, m_lru_lambda=m_lru_lambda, m_attn_sinks=m_attn_sinks, m_conv_w=m_conv_w, m_conv_b=m_conv_b, m_conv_ln_g=m_conv_ln_g, m_conv_ln_b=m_conv_ln_b, m_group_g=m_group_g, m_w_out=m_w_out, m_mix_post_g=m_mix_post_g, m_ffn2_pre_g=m_ffn2_pre_g, m_ffn2_w_gu=m_ffn2_w_gu, m_ffn2_w_down=m_ffn2_w_down, m_ffn2_post_g=m_ffn2_post_g, v_ffn1_pre_g=v_ffn1_pre_g, v_ffn1_w_gu=v_ffn1_w_gu, v_ffn1_w_down=v_ffn1_w_down, v_ffn1_post_g=v_ffn1_post_g, v_mix_pre_g=v_mix_pre_g, v_w_in=v_w_in, v_lru_conv_w=v_lru_conv_w, v_lru_conv_b=v_lru_conv_b, v_lru_w_a=v_lru_w_a, v_lru_b_a=v_lru_b_a, v_lru_w_x=v_lru_w_x, v_lru_b_x=v_lru_b_x, v_lru_lambda=v_lru_lambda, v_attn_sinks=v_attn_sinks, v_conv_w=v_conv_w, v_conv_b=v_conv_b, v_conv_ln_g=v_conv_ln_g, v_conv_ln_b=v_conv_ln_b, v_group_g=v_group_g, v_w_out=v_w_out, v_mix_post_g=v_mix_post_g, v_ffn2_pre_g=v_ffn2_pre_g, v_ffn2_w_gu=v_ffn2_w_gu, v_ffn2_w_down=v_ffn2_w_down, v_ffn2_post_g=v_ffn2_post_g)
    weights = {n: given[n] for n in TWIN_WEIGHTS}
    shared = {n: given[n] for n in SHARED_INPUTS}
    per_example = {n: given[n] for n in ['x']}
    grad_fn = _jax.value_and_grad(_loss, argnums=(0, 1))

    def one_microbatch(ex, loss_target):
        ex = dict(ex)
        diff = ex.pop(TWIN_DIFF_INPUT)
        return grad_fn(weights, diff, {**shared, **ex}, loss_target)

    if N_MICROBATCH == 1:
        loss, (grad_w, grad_x) = one_microbatch(per_example, given["loss_target"])
    else:
        def body(carry, xs):
            loss_sum, grad_sum = carry
            l_k, (gw_k, gx_k) = one_microbatch(xs[0], xs[1])
            with _jax.named_scope("update"):
                return (loss_sum + l_k, _jax.tree.map(_jnp.add, grad_sum, gw_k)), gx_k

        init = (_jnp.zeros((), _jnp.float32), _jax.tree.map(_jnp.zeros_like, weights))
        (loss, grad_w), grad_x = _jax.lax.scan(body, init, (per_example, given["loss_target"]))
    with _jax.named_scope("update"):
        delta_w, new_m, new_v = {}, {}, {}
        for n in TWIN_WEIGHTS:
            delta_w[n], new_m[n], new_v[n] = _adamw(weights[n], grad_w[n], given["m_" + n], given["v_" + n])
    return (loss, grad_x, *[grad_w[n] for n in TWIN_WEIGHTS], *[delta_w[n] for n in TWIN_WEIGHTS],
            *[new_m[n] for n in TWIN_WEIGHTS], *[new_v[n] for n in TWIN_WEIGHTS])
```

```python
import functools
import math

import jax
import jax.numpy as jnp
from jax import lax
from jax.experimental import pallas as pl
from jax.experimental.pallas import tpu as pltpu

F32 = jnp.float32
BF16 = jnp.bfloat16
N_DEV = 8
AXES = ("x", "y", "c")
MESH = pl.DeviceIdType.MESH

NORM_EPS = 1e-6
LN_EPS = 1e-5
NEG_BIG = -1e30
W_A = 256
W_B = 512
W_C = 256
HEAD_DIM = 64
N_Q_HEADS = 8
N_KV_HEADS = 2
Q_PER_KV = N_Q_HEADS // N_KV_HEADS
BLK = 128
LRU_K = 4
LRU_C = 8.0
A_BLOCKS = 4
CC_K = 31
CC_HALO = 32
LRU_HALO = 8
D_IN_PROJ = 2 * W_A + W_B + 2 * N_KV_HEADS * HEAD_DIM + 2 * W_C
ADAM_LR = 0.001
ADAM_B1 = 0.9
ADAM_B2 = 0.999
ADAM_EPS = 1e-08
ADAM_WD = 0.01
ADAM_STEP = 10
VMEM_LIMIT = 56 * 1024 * 1024

SDS = jax.ShapeDtypeStruct


def _cp(n_axes):
    return pltpu.CompilerParams(dimension_semantics=("arbitrary",) * n_axes, vmem_limit_bytes=VMEM_LIMIT)


def _time_tile(s):
    return max(BLK, s // 8)


def _mm(a, b):
    return jnp.dot(a.astype(BF16), b.astype(BF16), preferred_element_type=F32)


def _mm_nt(a, b):
    return lax.dot_general(a.astype(BF16), b.astype(BF16), (((1,), (1,)), ((), ())), preferred_element_type=F32)


def _mm_tn(a, b):
    return lax.dot_general(a.astype(BF16), b.astype(BF16), (((0,), (0,)), ((), ())), preferred_element_type=F32)


def _rms_r(x):
    return lax.rsqrt(jnp.mean(x * x, axis=-1, keepdims=True) + NORM_EPS)


def _rms_bwd(x, r, g, dy):
    gy = dy * g
    dx = r * (gy - x * (r * r) * jnp.mean(gy * x, axis=-1, keepdims=True))
    dg = jnp.sum(dy * x * r, axis=0, keepdims=True)
    return dx, dg


def _sigmoid(x):
    return 1.0 / (1.0 + jnp.exp(-x))


def _dsilu(z, sz):
    return sz * (1.0 + z * (1.0 - sz))


_GELU_C = math.sqrt(2.0 / math.pi)


def _gelu(x):
    t = jnp.tanh(_GELU_C * (x + 0.044715 * x * x * x))
    return 0.5 * x * (1.0 + t), t


def _dgelu(x, t):
    return 0.5 * (1.0 + t) + 0.5 * x * (1.0 - t * t) * _GELU_C * (1.0 + 3.0 * 0.044715 * x * x)


def _log1p(e):
    return jnp.where(e < 1e-2, e * (1.0 - e * (0.5 - e * (1.0 / 3.0))), jnp.log(1.0 + e))


def _softplus(x):
    return jnp.maximum(x, 0.0) + _log1p(jnp.exp(-jnp.abs(x)))


def _neg_expm1(x):
    small = -x * (1.0 + x * (0.5 + x * (1.0 / 6.0) * (1.0 + x * 0.25)))
    return jnp.where(x > -1e-2, small, 1.0 - jnp.exp(x))


def _shift_down(x, s):
    return x if s == 0 else pltpu.roll(x, s, 0)


def _shift_up(x, s):
    return x if s == 0 else pltpu.roll(x, x.shape[0] - s, 0)


def _ffn_fwd(x, pre_g, post_g, wgu_all, wd_all, layer, name):
    s, d = x.shape
    fc = wgu_all.shape[-1]
    ts = _time_tile(s)
    n_t, n_f = s // ts, N_DEV // 2

    def body(x_ref, pg_ref, qg_ref, wg_ref, wu_ref, wd_ref, xo_ref, h_ref, g_ref, u_ref, d_ref, h_scr, acc):
        f = pl.program_id(1)

        @pl.when(f == 0)
        def _():
            xv = x_ref[...]
            hv = (xv * _rms_r(xv) * pg_ref[...]).astype(BF16)
            h_scr[...] = hv
            h_ref[...] = hv
            acc[...] = jnp.zeros_like(acc)

        hv = h_scr[...]
        g = jnp.dot(hv, wg_ref[...], preferred_element_type=F32)
        u = jnp.dot(hv, wu_ref[...], preferred_element_type=F32)
        g_ref[...] = g.astype(BF16)
        u_ref[...] = u.astype(BF16)
        a = (g * _sigmoid(g) * u).astype(BF16)
        acc[...] += jnp.dot(a, wd_ref[...].reshape(fc, d), preferred_element_type=F32)

        @pl.when(f == n_f - 1)
        def _():
            dv = acc[...]
            d_ref[...] = dv
            xo_ref[...] = x_ref[...] + 0.5 * (dv * _rms_r(dv) * qg_ref[...])

    row = pl.BlockSpec((ts, d), lambda i, f: (i, 0))
    vec = pl.BlockSpec((1, d), lambda i, f: (0, 0))
    act = pl.BlockSpec((None, ts, fc), lambda i, f: (f, i, 0))
    return pl.pallas_call(
        body, name=name, grid=(n_t, n_f),
        in_specs=[row, vec, vec,
                  pl.BlockSpec((None, None, d, fc), lambda i, f: (f, layer, 0, 0)),
                  pl.BlockSpec((None, None, d, fc), lambda i, f: (f + n_f, layer, 0, 0)),
                  pl.BlockSpec((2, None, fc // 2, d), lambda i, f: (f, layer, 0, 0))],
        out_specs=[row, row, act, act, row],
        out_shape=[SDS((s, d), F32), SDS((s, d), BF16), SDS((n_f, s, fc), BF16), SDS((n_f, s, fc), BF16),
                   SDS((s, d), F32)],
        scratch_shapes=[pltpu.VMEM((ts, d), BF16), pltpu.VMEM((ts, d), F32)],
        compiler_params=_cp(2),
    )(x, pre_g, post_g, wgu_all, wgu_all, wd_all)


def _ffn_bwd_act(dxo, dmid, x, pre_g, post_g, g_s, u_s, wgu_all, wd_all, layer, name):
    s, d = x.shape
    fc = wgu_all.shape[-1]
    ts = _time_tile(s)
    n_t, n_f = s // ts, N_DEV // 2

    def body(dxo_ref, dm_ref, x_ref, pg_ref, qg_ref, g_ref, u_ref, wg_ref, wu_ref, wd_ref,
             dx_ref, dd_ref, dg_ref, du_ref, dpg_ref, dqg_ref, dd_scr, dh_acc):
        i, f = pl.program_id(0), pl.program_id(1)

        @pl.when((i == 0) & (f == 0))
        def _():
            dpg_ref[...] = jnp.zeros_like(dpg_ref)
            dqg_ref[...] = jnp.zeros_like(dqg_ref)

        @pl.when(f == 0)
        def _():
            dv = dm_ref[...]
            ddv, dq = _rms_bwd(dv, _rms_r(dv), qg_ref[...], 0.5 * dxo_ref[...])
            dqg_ref[...] += dq
            dd_scr[...] = ddv.astype(BF16)
            dd_ref[...] = ddv.astype(BF16)
            dh_acc[...] = jnp.zeros_like(dh_acc)

        da = _mm_nt(dd_scr[...], wd_ref[...].reshape(fc, d))
        g = g_ref[...].astype(F32)
        u = u_ref[...].astype(F32)
        sg = _sigmoid(g)
        du = (da * (g * sg)).astype(BF16)
        dg = (da * u * _dsilu(g, sg)).astype(BF16)
        dg_ref[...] = dg
        du_ref[...] = du
        dh_acc[...] += _mm_nt(dg, wg_ref[...]) + _mm_nt(du, wu_ref[...])

        @pl.when(f == n_f - 1)
        def _():
            xv = x_ref[...]
            dxv, dp = _rms_bwd(xv, _rms_r(xv), pg_ref[...], dh_acc[...])
            dpg_ref[...] += dp
            dx_ref[...] = dxo_ref[...] + dxv

    row = pl.BlockSpec((ts, d), lambda i, f: (i, 0))
    vec = pl.BlockSpec((1, d), lambda i, f: (0, 0))
    act = pl.BlockSpec((None, ts, fc), lambda i, f: (f, i, 0))
    return pl.pallas_call(
        body, name=name, grid=(n_t, n_f),
        in_specs=[row, row, row, vec, vec, act, act,
                  pl.BlockSpec((None, None, d, fc), lambda i, f: (f, layer, 0, 0)),
                  pl.BlockSpec((None, None, d, fc), lambda i, f: (f + n_f, layer, 0, 0)),
                  pl.BlockSpec((2, None, fc // 2, d), lambda i, f: (f, layer, 0, 0))],
        out_specs=[row, row, act, act, vec, vec],
        out_shape=[SDS((s, d), F32), SDS((s, d), BF16), SDS((n_f, s, fc), BF16), SDS((n_f, s, fc), BF16),
                   SDS((1, d), F32), SDS((1, d), F32)],
        scratch_shapes=[pltpu.VMEM((ts, d), BF16), pltpu.VMEM((ts, d), F32)],
        compiler_params=_cp(2),
    )(dxo, dmid, x, pre_g, post_g, g_s, u_s, wgu_all, wgu_all, wd_all)


def _ffn_bwd_w(h, dd, g_s, u_s, dg, du, name):
    s, d = h.shape
    n_f, _, fc = g_s.shape
    ts = _time_tile(s)
    n_t = s // ts

    def body(h_ref, dd_ref, g_ref, u_ref, dg_ref, du_ref, wg_ref, wu_ref, wd_ref, acc_g, acc_u, acc_d):
        i = pl.program_id(1)

        @pl.when(i == 0)
        def _():
            acc_g[...] = jnp.zeros_like(acc_g)
            acc_u[...] = jnp.zeros_like(acc_u)
            acc_d[...] = jnp.zeros_like(acc_d)

        g = g_ref[...].astype(F32)
        a = (g * _sigmoid(g) * u_ref[...].astype(F32)).astype(BF16)
        hv = h_ref[...]
        acc_g[...] += _mm_tn(hv, dg_ref[...])
        acc_u[...] += _mm_tn(hv, du_ref[...])
        acc_d[...] += _mm_tn(a, dd_ref[...])

        @pl.when(i == n_t - 1)
        def _():
            wg_ref[...] = acc_g[...].astype(BF16)
            wu_ref[...] = acc_u[...].astype(BF16)
            wd_ref[...] = acc_d[...].astype(BF16)

    row = pl.BlockSpec((ts, d), lambda f, i: (i, 0))
    act = pl.BlockSpec((None, ts, fc), lambda f, i: (f, i, 0))
    return pl.pallas_call(
        body, name=name, grid=(n_f, n_t),
        in_specs=[row, row, act, act, act, act],
        out_specs=[pl.BlockSpec((None, d, fc), lambda f, i: (f, 0, 0)),
                   pl.BlockSpec((None, d, fc), lambda f, i: (f, 0, 0)),
                   pl.BlockSpec((None, fc, d), lambda f, i: (f, 0, 0))],
        out_shape=[SDS((n_f, d, fc), BF16), SDS((n_f, d, fc), BF16), SDS((n_f, fc, d), BF16)],
        scratch_shapes=[pltpu.VMEM((d, fc), F32), pltpu.VMEM((d, fc), F32), pltpu.VMEM((fc, d), F32)],
        compiler_params=_cp(2),
    )(h, dd, g_s, u_s, dg, du)


_PROJ_WIDTHS = (W_A, W_A, W_B, N_KV_HEADS * HEAD_DIM, N_KV_HEADS * HEAD_DIM, 2 * W_C)


def _mix_in_fwd(x, pre_g, w_in, name):
    s, d = x.shape
    ts = _time_tile(s)

    def body(x_ref, pg_ref, w_ref, hn_ref, *outs):
        xv = x_ref[...]
        hn = (xv * _rms_r(xv) * pg_ref[...]).astype(BF16)
        hn_ref[...] = hn
        proj = jnp.dot(hn, w_ref[...], preferred_element_type=F32)
        off = 0
        for o_ref, w in zip(outs, _PROJ_WIDTHS):
            o_ref[...] = proj[:, off:off + w]
            off += w

    row = lambda w: pl.BlockSpec((ts, w), lambda i: (i, 0))
    return pl.pallas_call(
        body, name=name, grid=(s // ts,),
        in_specs=[row(d), pl.BlockSpec((1, d), lambda i: (0, 0)), pl.BlockSpec((d, D_IN_PROJ), lambda i: (0, 0))],
        out_specs=[row(d)] + [row(w) for w in _PROJ_WIDTHS],
        out_shape=[SDS((s, d), BF16)] + [SDS((s, w), F32) for w in _PROJ_WIDTHS],
        compiler_params=_cp(1),
    )(x, pre_g, w_in)


def _mix_in_bwd(dres, x, pre_g, hn, w_in, dlx, dlg, dq, dkc, dkp, dvc, dvp, dglu, name):
    s, d = x.shape
    ts = _time_tile(s)
    n_t = s // ts
    bpt = ts // BLK
    n_blk = s // BLK
    kvw = N_KV_HEADS * HEAD_DIM

    def body(dres_ref, x_ref, pg_ref, hn_ref, w_ref, dlx_ref, dlg_ref, dq_ref, dkc_ref, dkp_ref, dkn_ref,
             dvc_ref, dvp_ref, dvn_ref, dglu_ref, dx_ref, dw_ref, dpg_ref, acc):
        i = pl.program_id(0)

        @pl.when(i == 0)
        def _():
            acc[...] = jnp.zeros_like(acc)
            dpg_ref[...] = jnp.zeros_like(dpg_ref)

        def shifted(cur_ref, nxt_ref):
            nxt = jnp.where(i < n_t - 1, nxt_ref[...], 0.0)
            if bpt == 1:
                return nxt
            return jnp.concatenate([cur_ref[BLK:, :], nxt], axis=0)

        dk = dkc_ref[...] + shifted(dkp_ref, dkn_ref)
        dv = dvc_ref[...] + shifted(dvp_ref, dvn_ref)
        dproj = jnp.concatenate([dlx_ref[...], dlg_ref[...], dq_ref[...], dk, dv, dglu_ref[...]], axis=1).astype(BF16)
        dhn = _mm_nt(dproj, w_ref[...])
        acc[...] += _mm_tn(hn_ref[...], dproj)
        xv = x_ref[...]
        dxv, dp = _rms_bwd(xv, _rms_r(xv), pg_ref[...], dhn)
        dpg_ref[...] += dp
        dx_ref[...] = dres_ref[...] + dxv

        @pl.when(i == n_t - 1)
        def _():
            dw_ref[...] = acc[...].astype(BF16)

    row = lambda w: pl.BlockSpec((ts, w), lambda i: (i, 0))
    nxt = pl.BlockSpec((BLK, kvw), lambda i: (jnp.minimum((i + 1) * bpt, n_blk - 1), 0))
    vec = pl.BlockSpec((1, d), lambda i: (0, 0))
    full = pl.BlockSpec((d, D_IN_PROJ), lambda i: (0, 0))
    return pl.pallas_call(
        body, name=name, grid=(n_t,),
        in_specs=[row(d), row(d), vec, row(d), full, row(W_A), row(W_A), row(W_B), row(kvw), row(kvw), nxt,
                  row(kvw), row(kvw), nxt, row(2 * W_C)],
        out_specs=[row(d), full, vec],
        out_shape=[SDS((s, d), F32), SDS((d, D_IN_PROJ), BF16), SDS((1, d), F32)],
        scratch_shapes=[pltpu.VMEM((d, D_IN_PROJ), F32)],
        compiler_params=_cp(1),
    )(dres, x, pre_g, hn, w_in, dlx, dlg, dq, dkc, dkp, dkp, dvc, dvp, dvp, dglu)


def _lru_gates(xc, lru_p):
    cw_ref, cb_ref, wa_ref, ba_ref, wx_ref, bx_ref, lam_ref = lru_p
    c = cb_ref[...]
    for j in range(LRU_K):
        c = c + cw_ref[j:j + 1, :] * _shift_down(xc, LRU_K - 1 - j)[LRU_HALO:, :]
    r = _sigmoid(_mm(c, wa_ref[...]) + ba_ref[...])
    ig = _sigmoid(_mm(c, wx_ref[...]) + bx_ref[...])
    sp = _softplus(-lam_ref[...])
    log_a = -LRU_C * r * sp
    a = jnp.exp(log_a)
    m = jnp.sqrt(_neg_expm1(2.0 * log_a))
    return c, r, ig, sp, a, m


def _lru_specs(s, ts):
    n8 = ts // LRU_HALO
    small = lambda r: pl.BlockSpec((r, W_A), lambda i: (0, 0))
    params = [small(LRU_K), small(1), small(W_A), small(1), small(W_A), small(1), small(1)]
    return n8, params


def _lru_fwd(lx, lg, lru_p, name):
    s = lx.shape[0]
    ts = _time_tile(s)
    n8, pspecs = _lru_specs(s, ts)

    def body(lx_ref, lxp_ref, lg_ref, *rest):
        lru_p, (ya_ref, h_ref, carry) = rest[:7], rest[7:]
        i = pl.program_id(0)
        prev = jnp.where(i > 0, lxp_ref[...], 0.0)
        xc = jnp.concatenate([prev, lx_ref[...]], axis=0)
        c, r, ig, sp, a, m = _lru_gates(xc, lru_p)
        acc_a, acc_b = a, m * (ig * c)
        t = lax.broadcasted_iota(jnp.int32, a.shape, 0)
        k = 1
        while k < ts:
            keep = t >= k
            acc_b = jnp.where(keep, acc_a * _shift_down(acc_b, k) + acc_b, acc_b)
            acc_a = jnp.where(keep, acc_a * _shift_down(acc_a, k), acc_a)
            k *= 2
        h0 = jnp.where(i > 0, carry[...], 0.0)
        h = acc_b + acc_a * h0
        carry[...] = h[ts - 1:ts, :]
        h_ref[...] = h
        ya_ref[...] = _gelu(lg_ref[...])[0] * h

    row = pl.BlockSpec((ts, W_A), lambda i: (i, 0))
    prev8 = pl.BlockSpec((LRU_HALO, W_A), lambda i: (jnp.maximum(i * n8 - 1, 0), 0))
    return pl.pallas_call(
        body, name=name, grid=(s // ts,),
        in_specs=[row, prev8, row] + pspecs,
        out_specs=[row, row],
        out_shape=[SDS((s, W_A), F32), SDS((s, W_A), F32)],
        scratch_shapes=[pltpu.VMEM((1, W_A), F32)],
        compiler_params=_cp(1),
    )(lx, lx, lg, *lru_p)


def _lru_bwd(dya, lx, lg, h_s, lru_p, name):
    s = lx.shape[0]
    ts = _time_tile(s)
    n_t = s // ts
    n8, pspecs = _lru_specs(s, ts)

    def body(dya_ref, lx_ref, lxp_ref, lg_ref, h_ref, hp_ref, *rest):
        lru_p = rest[:7]
        (dlx_ref, dlg_ref, dcw_ref, dcb_ref, dwa_ref, dba_ref, dwx_ref, dbx_ref, dlam_ref,
         carry_a, carry_l, carry_dc) = rest[7:]
        cw_ref, _, wa_ref, _, wx_ref, _, lam_ref = lru_p
        i = pl.program_id(0)
        first_tile = i == n_t - 1
        last_tile = i == 0

        @pl.when(i == 0)
        def _():
            for ref in (dcw_ref, dcb_ref, dwa_ref, dba_ref, dwx_ref, dbx_ref, dlam_ref):
                ref[...] = jnp.zeros_like(ref)

        prev = jnp.where(first_tile, 0.0, lxp_ref[...])
        xc = jnp.concatenate([prev, lx_ref[...]], axis=0)
        c, r, ig, sp, a, m = _lru_gates(xc, lru_p)
        h = h_ref[...]
        hcat = jnp.concatenate([jnp.where(first_tile, 0.0, hp_ref[...]), h], axis=0)
        h_m1 = _shift_down(hcat, 1)[LRU_HALO:, :]
        lg = lg_ref[...]
        ge, th = _gelu(lg)
        dya = dya_ref[...]
        dlg_ref[...] = dya * h * _dgelu(lg, th)
        dh = dya * ge
        t = lax.broadcasted_iota(jnp.int32, a.shape, 0)
        a_next = jnp.where(t < ts - 1, _shift_up(a, 1), jnp.where(last_tile, 0.0, carry_a[...]))
        acc_a, acc_b = a_next, dh
        k = 1
        while k < ts:
            keep = t < ts - k
            acc_b = jnp.where(keep, acc_a * _shift_up(acc_b, k) + acc_b, acc_b)
            acc_a = jnp.where(keep, acc_a * _shift_up(acc_a, k), acc_a)
            k *= 2
        lam_beyond = jnp.where(last_tile, 0.0, carry_l[...])
        lmb = acc_b + acc_a * lam_beyond
        carry_a[...] = a[0:1, :]
        carry_l[...] = lmb[0:1, :]
        gi = ig * c
        dgi = lmb * m
        dla = lmb * h_m1 * a - (lmb * gi) * (a * a) / m
        dr = dla * (-LRU_C * sp)
        dsp = jnp.sum(dla * (-LRU_C * r), axis=0, keepdims=True)
        dlam_ref[...] += -dsp * _sigmoid(-lam_ref[...])
        dra = dr * r * (1.0 - r)
        dia = dgi * c * ig * (1.0 - ig)
        dc = dgi * ig + _mm_nt(dra, wa_ref[...]) + _mm_nt(dia, wx_ref[...])
        dwa_ref[...] += _mm_tn(c, dra)
        dwx_ref[...] += _mm_tn(c, dia)
        dba_ref[...] += jnp.sum(dra, axis=0, keepdims=True)
        dbx_ref[...] += jnp.sum(dia, axis=0, keepdims=True)
        dcb_ref[...] += jnp.sum(dc, axis=0, keepdims=True)
        dcc = jnp.concatenate([dc, jnp.where(last_tile, 0.0, carry_dc[...])], axis=0)
        carry_dc[...] = dc[0:LRU_HALO, :]
        dlx = jnp.zeros_like(dc)
        for j in range(LRU_K):
            sh = LRU_K - 1 - j
            dcw_ref[j:j + 1, :] += jnp.sum(dc * _shift_down(xc, sh)[LRU_HALO:, :], axis=0, keepdims=True)
            dlx = dlx + cw_ref[j:j + 1, :] * _shift_up(dcc, sh)[:ts, :]
        dlx_ref[...] = dlx

    rev = lambda i: (n_t - 1 - i, 0)
    row = pl.BlockSpec((ts, W_A), rev)
    prev8 = pl.BlockSpec((LRU_HALO, W_A), lambda i: (jnp.maximum((n_t - 1 - i) * n8 - 1, 0), 0))
    small = lambda r: pl.BlockSpec((r, W_A), lambda i: (0, 0))
    return pl.pallas_call(
        body, name=name, grid=(n_t,),
        in_specs=[row, row, prev8, row, row, prev8] + pspecs,
        out_specs=[row, row, small(LRU_K), small(1), small(W_A), small(1), small(W_A), small(1), small(1)],
        out_shape=[SDS((s, W_A), F32), SDS((s, W_A), F32), SDS((LRU_K, W_A), F32), SDS((1, W_A), F32),
                   SDS((W_A, W_A), F32), SDS((1, W_A), F32), SDS((W_A, W_A), F32), SDS((1, W_A), F32),
                   SDS((1, W_A), F32)],
        scratch_shapes=[pltpu.VMEM((1, W_A), F32), pltpu.VMEM((1, W_A), F32), pltpu.VMEM((LRU_HALO, W_A), F32)],
        compiler_params=_cp(1),
    )(dya, lx, lx, lg, h_s, h_s, *lru_p)


def _attn_probs(q_ref, kc_ref, kp_ref, sk_ref, n, h):
    g = h // Q_PER_KV
    qh = q_ref[:, h * HEAD_DIM:(h + 1) * HEAD_DIM]
    kc = kc_ref[:, g * HEAD_DIM:(g + 1) * HEAD_DIM]
    kp = kp_ref[:, g * HEAD_DIM:(g + 1) * HEAD_DIM]
    qi = lax.broadcasted_iota(jnp.int32, (BLK, BLK), 0)
    kj = lax.broadcasted_iota(jnp.int32, (BLK, BLK), 1)
    scale = 1.0 / math.sqrt(HEAD_DIM)
    sc = jnp.where(kj <= qi, _mm_nt(qh, kc) * scale, NEG_BIG)
    sp = jnp.where((kj > qi) & (n > 0), _mm_nt(qh, kp) * scale, NEG_BIG)
    sink = sk_ref[h]
    m = jnp.maximum(jnp.maximum(jnp.max(sc, axis=-1, keepdims=True), jnp.max(sp, axis=-1, keepdims=True)), sink)
    pc = jnp.exp(sc - m)
    pp = jnp.exp(sp - m)
    es = jnp.exp(sink - m)
    inv = 1.0 / (jnp.sum(pc, axis=-1, keepdims=True) + jnp.sum(pp, axis=-1, keepdims=True) + es)
    return qh, kc, kp, pc * inv, pp * inv, es * inv


def _attn_specs(s):
    kvw = N_KV_HEADS * HEAD_DIM
    cur = lambda w: pl.BlockSpec((BLK, w), lambda n: (n, 0))
    prv = lambda w: pl.BlockSpec((BLK, w), lambda n: (jnp.maximum(n - 1, 0), 0))
    return kvw, cur, prv


def _attn_fwd(q, k, v, sinks, name):
    s = q.shape[0]
    kvw, cur, prv = _attn_specs(s)

    def body(q_ref, kc_ref, kp_ref, vc_ref, vp_ref, sk_ref, y_ref):
        n = pl.program_id(0)
        outs = []
        for h in range(N_Q_HEADS):
            g = h // Q_PER_KV
            _, _, _, pc, pp, _ = _attn_probs(q_ref, kc_ref, kp_ref, sk_ref, n, h)
            outs.append(_mm(pc, vc_ref[:, g * HEAD_DIM:(g + 1) * HEAD_DIM])
                        + _mm(pp, vp_ref[:, g * HEAD_DIM:(g + 1) * HEAD_DIM]))
        y_ref[...] = jnp.concatenate(outs, axis=1)

    return pl.pallas_call(
        body, name=name, grid=(s // BLK,),
        in_specs=[cur(W_B), cur(kvw), prv(kvw), cur(kvw), prv(kvw), pl.BlockSpec(memory_space=pltpu.SMEM)],
        out_specs=cur(W_B),
        out_shape=SDS((s, W_B), F32),
        compiler_params=_cp(1),
    )(q, k, k, v, v, sinks)


def _attn_bwd(dy, q, k, v, sinks, name):
    s = q.shape[0]
    kvw, cur, prv = _attn_specs(s)

    def body(dy_ref, q_ref, kc_ref, kp_ref, vc_ref, vp_ref, sk_ref, dq_ref, dkc_ref, dkp_ref, dvc_ref, dvp_ref, dsk_ref):
        n = pl.program_id(0)

        @pl.when(n == 0)
        def _():
            dsk_ref[...] = jnp.zeros_like(dsk_ref)

        scale = 1.0 / math.sqrt(HEAD_DIM)
        dqs = []
        head_row = lax.broadcasted_iota(jnp.int32, (N_Q_HEADS, BLK), 0)
        dsk = jnp.zeros((N_Q_HEADS, BLK), F32)
        dkc, dkp, dvc, dvp = ([None] * N_KV_HEADS for _ in range(4))
        add = lambda old, new: new if old is None else old + new
        for h in range(N_Q_HEADS):
            g = h // Q_PER_KV
            qh, kc, kp, pc, pp, ps = _attn_probs(q_ref, kc_ref, kp_ref, sk_ref, n, h)
            do = dy_ref[:, h * HEAD_DIM:(h + 1) * HEAD_DIM]
            dpc = _mm_nt(do, vc_ref[:, g * HEAD_DIM:(g + 1) * HEAD_DIM])
            dpp = _mm_nt(do, vp_ref[:, g * HEAD_DIM:(g + 1) * HEAD_DIM])
            delta = jnp.sum(pc * dpc, axis=-1, keepdims=True) + jnp.sum(pp * dpp, axis=-1, keepdims=True)
            dsc = pc * (dpc - delta) * scale
            dsp = pp * (dpp - delta) * scale
            dqs.append(_mm(dsc, kc) + _mm(dsp, kp))
            dkc[g] = add(dkc[g], _mm_tn(dsc, qh))
            dkp[g] = add(dkp[g], _mm_tn(dsp, qh))
            dvc[g] = add(dvc[g], _mm_tn(pc, do))
            dvp[g] = add(dvp[g], _mm_tn(pp, do))
            dsk = jnp.where(head_row == h, jnp.sum(-ps * delta, axis=0, keepdims=True), dsk)
        dq_ref[...] = jnp.concatenate(dqs, axis=1)
        dkc_ref[...] = jnp.concatenate(dkc, axis=1)
        dkp_ref[...] = jnp.concatenate(dkp, axis=1)
        dvc_ref[...] = jnp.concatenate(dvc, axis=1)
        dvp_ref[...] = jnp.concatenate(dvp, axis=1)
        dsk_ref[...] += dsk

    return pl.pallas_call(
        body, name=name, grid=(s // BLK,),
        in_specs=[cur(W_B), cur(W_B), cur(kvw), prv(kvw), cur(kvw), prv(kvw), pl.BlockSpec(memory_space=pltpu.SMEM)],
        out_specs=[cur(W_B), cur(kvw), cur(kvw), cur(kvw), cur(kvw), pl.BlockSpec((N_Q_HEADS, BLK), lambda n: (0, 0))],
        out_shape=[SDS((s, W_B), F32)] + [SDS((s, kvw), F32)] * 4 + [SDS((N_Q_HEADS, BLK), F32)],
        compiler_params=_cp(1),
    )(dy, q, k, k, v, v, sinks)


def _cc_recompute(glu_ref, glup_ref, cw_ref, cb_ref, first_tile):
    prev = jnp.where(first_tile, 0.0, glup_ref[...])
    ge = jnp.concatenate([prev, glu_ref[...]], axis=0)
    y0 = ge[:, :W_C] * _sigmoid(ge[:, W_C:])
    y1 = cb_ref[...]
    for j in range(CC_K):
        y1 = y1 + cw_ref[j:j + 1, :] * _shift_down(y0, CC_K - 1 - j)[CC_HALO:, :]
    return y0, y1


def _ln_stats(y1):
    mu = jnp.mean(y1, axis=-1, keepdims=True)
    xc = y1 - mu
    rstd = lax.rsqrt(jnp.mean(xc * xc, axis=-1, keepdims=True) + LN_EPS)
    return xc * rstd, rstd


def _cc_specs(s, ts):
    n32 = ts // CC_HALO
    row = lambda w: pl.BlockSpec((ts, w), lambda i: (i, 0))
    prev = pl.BlockSpec((CC_HALO, 2 * W_C), lambda i: (jnp.maximum(i * n32 - 1, 0), 0))
    small = lambda r: pl.BlockSpec((r, W_C), lambda i: (0, 0))
    return row, prev, small


def _cc_fwd(glu, cw, cb, lng, lnb, name):
    s = glu.shape[0]
    ts = _time_tile(s)
    row, prev, small = _cc_specs(s, ts)

    def body(glu_ref, glup_ref, cw_ref, cb_ref, lng_ref, lnb_ref, y_ref):
        _, y1 = _cc_recompute(glu_ref, glup_ref, cw_ref, cb_ref, pl.program_id(0) == 0)
        xhat, _ = _ln_stats(y1)
        z = xhat * lng_ref[...] + lnb_ref[...]
        y_ref[...] = z * _sigmoid(z)

    return pl.pallas_call(
        body, name=name, grid=(s // ts,),
        in_specs=[row(2 * W_C), prev, small(CC_HALO), small(1), small(1), small(1)],
        out_specs=row(W_C), out_shape=SDS((s, W_C), F32), compiler_params=_cp(1),
    )(glu, glu, cw, cb, lng, lnb)


def _cc_bwd_conv(dy, glu, cw, cb, lng, lnb, name):
    s = glu.shape[0]
    ts = _time_tile(s)
    row, prev, small = _cc_specs(s, ts)

    def body(dy_ref, glu_ref, glup_ref, cw_ref, cb_ref, lng_ref, lnb_ref, dy1_ref, dcw_ref, dcb_ref, dlng_ref, dlnb_ref):
        i = pl.program_id(0)

        @pl.when(i == 0)
        def _():
            for ref in (dcw_ref, dcb_ref, dlng_ref, dlnb_ref):
                ref[...] = jnp.zeros_like(ref)

        y0, y1 = _cc_recompute(glu_ref, glup_ref, cw_ref, cb_ref, i == 0)
        xhat, rstd = _ln_stats(y1)
        z = xhat * lng_ref[...] + lnb_ref[...]
        dz = dy_ref[...] * _dsilu(z, _sigmoid(z))
        dlng_ref[...] += jnp.sum(dz * xhat, axis=0, keepdims=True)
        dlnb_ref[...] += jnp.sum(dz, axis=0, keepdims=True)
        dxh = dz * lng_ref[...]
        dy1 = rstd * (dxh - jnp.mean(dxh, axis=-1, keepdims=True) - xhat * jnp.mean(dxh * xhat, axis=-1, keepdims=True))
        dy1_ref[...] = dy1
        dcb_ref[...] += jnp.sum(dy1, axis=0, keepdims=True)
        for j in range(CC_K):
            dcw_ref[j:j + 1, :] += jnp.sum(dy1 * _shift_down(y0, CC_K - 1 - j)[CC_HALO:, :], axis=0, keepdims=True)

    return pl.pallas_call(
        body, name=name, grid=(s // ts,),
        in_specs=[row(W_C), row(2 * W_C), prev, small(CC_HALO), small(1), small(1), small(1)],
        out_specs=[row(W_C), small(CC_HALO), small(1), small(1), small(1)],
        out_shape=[SDS((s, W_C), F32), SDS((CC_HALO, W_C), F32)] + [SDS((1, W_C), F32)] * 3,
        compiler_params=_cp(1),
    )(dy, glu, glu, cw, cb, lng, lnb)


def _cc_bwd_glu(dy1, glu, cw, name):
    s = glu.shape[0]
    ts = _time_tile(s)
    n_t = s // ts
    n32 = ts // CC_HALO

    def body(dy1_ref, dyn_ref, glu_ref, cw_ref, dglu_ref):
        i = pl.program_id(0)
        dcat = jnp.concatenate([dy1_ref[...], jnp.where(i < n_t - 1, dyn_ref[...], 0.0)], axis=0)
        dy0 = jnp.zeros((ts, W_C), F32)
        for j in range(CC_K):
            dy0 = dy0 + cw_ref[j:j + 1, :] * _shift_up(dcat, CC_K - 1 - j)[:ts, :]
        a = glu_ref[:, :W_C]
        sg = _sigmoid(glu_ref[:, W_C:])
        dglu_ref[...] = jnp.concatenate([dy0 * sg, dy0 * a * sg * (1.0 - sg)], axis=1)

    row = lambda w: pl.BlockSpec((ts, w), lambda i: (i, 0))
    nxt = pl.BlockSpec((CC_HALO, W_C), lambda i: (jnp.minimum((i + 1) * n32, s // CC_HALO - 1), 0))
    return pl.pallas_call(
        body, name=name, grid=(n_t,),
        in_specs=[row(W_C), nxt, row(2 * W_C), pl.BlockSpec((CC_HALO, W_C), lambda i: (0, 0))],
        out_specs=row(2 * W_C), out_shape=SDS((s, 2 * W_C), F32), compiler_params=_cp(1),
    )(dy1, dy1, glu, cw)


_MIX_OFFS = ((0, W_A), (W_A, W_A + W_B), (W_A + W_B, W_A + W_B + W_C))


def _mix_out_fwd(x, ya, yb, yc, group_g, w_out, post_g, name):
    s, d = x.shape
    ts = _time_tile(s)
    dm = w_out.shape[0]

    def body(x_ref, ya_ref, yb_ref, yc_ref, gg_ref, w_ref, qg_ref, xo_ref, o_ref):
        parts = []
        for y_ref, (lo, hi) in zip((ya_ref, yb_ref, yc_ref), _MIX_OFFS):
            yv = y_ref[...]
            parts.append(yv * _rms_r(yv) * gg_ref[:, lo:hi])
        o = _mm(jnp.concatenate(parts, axis=1), w_ref[...])
        o_ref[...] = o
        xo_ref[...] = x_ref[...] + o * _rms_r(o) * qg_ref[...]

    row = lambda w: pl.BlockSpec((ts, w), lambda i: (i, 0))
    return pl.pallas_call(
        body, name=name, grid=(s // ts,),
        in_specs=[row(d), row(W_A), row(W_B), row(W_C), pl.BlockSpec((1, dm), lambda i: (0, 0)),
                  pl.BlockSpec((dm, d), lambda i: (0, 0)), pl.BlockSpec((1, d), lambda i: (0, 0))],
        out_specs=[row(d), row(d)], out_shape=[SDS((s, d), F32), SDS((s, d), F32)], compiler_params=_cp(1),
    )(x, ya, yb, yc, group_g, w_out, post_g)


def _mix_out_bwd(dxo, o, ya, yb, yc, group_g, w_out, post_g, name):
    s, d = o.shape
    ts = _time_tile(s)
    n_t = s // ts
    dm = w_out.shape[0]

    def body(dxo_ref, o_ref, ya_ref, yb_ref, yc_ref, gg_ref, w_ref, qg_ref,
             dya_ref, dyb_ref, dyc_ref, dw_ref, dqg_ref, dgg_ref, acc):
        i = pl.program_id(0)

        @pl.when(i == 0)
        def _():
            acc[...] = jnp.zeros_like(acc)
            dqg_ref[...] = jnp.zeros_like(dqg_ref)
            dgg_ref[...] = jnp.zeros_like(dgg_ref)

        ov = o_ref[...]
        do, dq = _rms_bwd(ov, _rms_r(ov), qg_ref[...], dxo_ref[...])
        dqg_ref[...] += dq
        do = do.astype(BF16)
        dyn = _mm_nt(do, w_ref[...])
        parts, dggs = [], []
        for y_ref, dy_ref, (lo, hi) in zip((ya_ref, yb_ref, yc_ref), (dya_ref, dyb_ref, dyc_ref), _MIX_OFFS):
            yv = y_ref[...]
            r = _rms_r(yv)
            gg = gg_ref[:, lo:hi]
            parts.append(yv * r * gg)
            dyv, dg = _rms_bwd(yv, r, gg, dyn[:, lo:hi])
            dy_ref[...] = dyv
            dggs.append(dg)
        dgg_ref[...] += jnp.concatenate(dggs, axis=1)
        acc[...] += _mm_tn(jnp.concatenate(parts, axis=1), do)

        @pl.when(i == n_t - 1)
        def _():
            dw_ref[...] = acc[...].astype(BF16)

    row = lambda w: pl.BlockSpec((ts, w), lambda i: (i, 0))
    full = pl.BlockSpec((dm, d), lambda i: (0, 0))
    return pl.pallas_call(
        body, name=name, grid=(n_t,),
        in_specs=[row(d), row(d), row(W_A), row(W_B), row(W_C), pl.BlockSpec((1, dm), lambda i: (0, 0)), full,
                  pl.BlockSpec((1, d), lambda i: (0, 0))],
        out_specs=[row(W_A), row(W_B), row(W_C), full, pl.BlockSpec((1, d), lambda i: (0, 0)),
                   pl.BlockSpec((1, dm), lambda i: (0, 0))],
        out_shape=[SDS((s, W_A), F32), SDS((s, W_B), F32), SDS((s, W_C), F32), SDS((dm, d), BF16),
                   SDS((1, d), F32), SDS((1, dm), F32)],
        scratch_shapes=[pltpu.VMEM((dm, d), F32)],
        compiler_params=_cp(1),
    )(dxo, o, ya, yb, yc, group_g, w_out, post_g)


def _loss_head(y, target, name):
    s, d = y.shape
    ts = _time_tile(s)

    def body(y_ref, t_ref, loss_ref, dy_ref):
        @pl.when(pl.program_id(0) == 0)
        def _():
            loss_ref[...] = jnp.zeros_like(loss_ref)

        err = y_ref[...] - t_ref[...]
        dy_ref[...] = err * (1.0 / d)
        per_tok = jnp.mean(err * err, axis=-1, keepdims=True)
        loss_ref[...] += 0.5 * jnp.sum(per_tok, axis=0, keepdims=True)

    row = pl.BlockSpec((ts, d), lambda i: (i, 0))
    return pl.pallas_call(
        body, name=name, grid=(s // ts,), in_specs=[row, row],
        out_specs=[pl.BlockSpec((1, BLK), lambda i: (0, 0)), row],
        out_shape=[SDS((1, BLK), F32), SDS((s, d), F32)], compiler_params=_cp(1),
    )(y, target)


def _my_pos():
    x, y, c = (lax.axis_index(a) for a in AXES)
    return x, y, c, 4 * x + 2 * y + c


def _dev(p):
    return (p >> 2, (p >> 1) & 1, p & 1)


def _all_gather(shards, name):
    n = len(shards)

    def body(*refs):
        ins, outs, (send_sems, recv_sems, local_sems) = refs[:n], refs[n:2 * n], refs[2 * n:]
        x, y, c, _ = _my_pos()
        me, sibling = (x, y, c), (x, y, 1 - c)
        chips = [(1 - x, y), (x, 1 - y), (1 - x, 1 - y)]

        def copy(a, k, block, to, src=None):
            dst = outs[a].at[4 * block[0] + 2 * block[1] + block[2]]
            return pltpu.make_async_remote_copy(
                src_ref=dst if src is None else src, dst_ref=dst,
                send_sem=send_sems.at[a, k], recv_sem=recv_sems.at[a, k], device_id=to, device_id_type=MESH)

        started = []
        for a in range(n):
            mine = pltpu.make_async_copy(ins[a], outs[a].at[4 * x + 2 * y + c], local_sems.at[a])
            mine.start()
            started.append(mine)
        first = []
        for a in range(n):
            first.append(copy(a, 0, me, sibling, src=ins[a]))
            first += [copy(a, 1 + j, me, (*chip, c), src=ins[a]) for j, chip in enumerate(chips)]
        for cp in first:
            cp.start()
        passed = []
        for j, chip in enumerate(chips):
            for a in range(n):
                copy(a, 1 + j, (*chip, c), me).wait_recv()
                fwd = copy(a, 4 + j, (*chip, c), sibling)
                fwd.start()
                passed.append(fwd)
        for a in range(n):
            copy(a, 0, sibling, me).wait_recv()
            for j, chip in enumerate(chips):
                copy(a, 4 + j, (*chip, 1 - c), me).wait_recv()
        for cp in first + passed:
            cp.wait_send()
        for mine in started:
            mine.wait()

    any_spec = pl.BlockSpec(memory_space=pl.ANY)
    return pl.pallas_call(
        body, name=name,
        in_specs=[any_spec] * n, out_specs=[any_spec] * n,
        out_shape=[SDS((N_DEV,) + a.shape, a.dtype) for a in shards],
        scratch_shapes=[pltpu.SemaphoreType.DMA((n, 7)), pltpu.SemaphoreType.DMA((n, 7)), pltpu.SemaphoreType.DMA((n,))],
    )(*shards)


def _grad_exchange(groups, name):
    flat, meta = [], []
    for gi, layers in enumerate(groups):
        for li, arrs in enumerate(layers):
            off = 0
            for a in arrs:
                meta.append((gi, li, off, a.shape[0], len(flat)))
                flat.append(a)
                off += a.shape[0]
            assert off == N_DEV
    n_in, n_g = len(flat), len(groups)
    n_l = len(groups[0])

    def body(*refs):
        ins, outs, (send_sems, recv_sems, local_sems) = refs[:n_in], refs[n_in:n_in + n_g], refs[n_in + n_g:]
        _, _, _, me = _my_pos()

        def slab_src(gi, li, p):
            for g2, l2, off, cnt, idx in meta:
                if g2 == gi and l2 == li and off <= p < off + cnt:
                    return ins[idx].at[p - off]
            raise AssertionError

        def rdma(gi, li, p, src_dev):
            return pltpu.make_async_remote_copy(
                src_ref=slab_src(gi, li, p), dst_ref=outs[gi].at[src_dev, li],
                send_sem=send_sems.at[gi * n_l + li, p], recv_sem=recv_sems.at[gi * n_l + li, src_dev],
                device_id=_dev(p), device_id_type=MESH)

        for p in range(N_DEV):
            @pl.when(me != p)
            def _():
                for gi in range(n_g):
                    for li in range(n_l):
                        rdma(gi, li, p, me).start()

            @pl.when(me == p)
            def _():
                for gi in range(n_g):
                    for li in range(n_l):
                        pltpu.make_async_copy(slab_src(gi, li, p), outs[gi].at[p, li], local_sems.at[gi * n_l + li]).start()

        for p in range(N_DEV):
            @pl.when(me != p)
            def _():
                for gi in range(n_g):
                    for li in range(n_l):
                        rdma(gi, li, p, p).wait_recv()
                        rdma(gi, li, p, p).wait_send()

            @pl.when(me == p)
            def _():
                for gi in range(n_g):
                    for li in range(n_l):
                        pltpu.make_async_copy(slab_src(gi, li, p), outs[gi].at[p, li], local_sems.at[gi * n_l + li]).wait()

    any_spec = pl.BlockSpec(memory_space=pl.ANY)
    out_shape = [SDS((N_DEV, n_l) + layers[0][0].shape[1:], layers[0][0].dtype) for layers in groups]
    return pl.pallas_call(
        body, name=name,
        in_specs=[any_spec] * n_in, out_specs=[any_spec] * n_g, out_shape=out_shape,
        scratch_shapes=[pltpu.SemaphoreType.DMA((n_g * n_l, N_DEV)), pltpu.SemaphoreType.DMA((n_g * n_l, N_DEV)),
                        pltpu.SemaphoreType.DMA((n_g * n_l,))],
    )(*flat)


def _all_reduce_small(part, name):
    r = part.shape[0]

    def body(in_ref, out_ref, buf, send_sems, recv_sems):
        _, _, _, me = _my_pos()

        def rdma(p, src_dev):
            return pltpu.make_async_remote_copy(
                src_ref=in_ref, dst_ref=buf.at[src_dev], send_sem=send_sems.at[p], recv_sem=recv_sems.at[src_dev],
                device_id=_dev(p), device_id_type=MESH)

        for p in range(N_DEV):
            @pl.when(me != p)
            def _():
                rdma(p, me).start()

            @pl.when(me == p)
            def _():
                buf[p] = in_ref[...]

        for p in range(N_DEV):
            @pl.when(me != p)
            def _():
                rdma(p, p).wait_recv()
                rdma(p, p).wait_send()

        total = buf[0]
        for p in range(1, N_DEV):
            total = total + buf[p]
        out_ref[...] = total

    vm = pl.BlockSpec(memory_space=pltpu.VMEM)
    return pl.pallas_call(
        body, name=name, in_specs=[vm], out_specs=vm, out_shape=SDS(part.shape, F32),
        scratch_shapes=[pltpu.VMEM((N_DEV, r, BLK), F32), pltpu.SemaphoreType.DMA((N_DEV,)),
                        pltpu.SemaphoreType.DMA((N_DEV,))],
        compiler_params=pltpu.CompilerParams(vmem_limit_bytes=VMEM_LIMIT),
    )(part)


def _adamw_math(w, g, m, v):
    m = ADAM_B1 * m + (1.0 - ADAM_B1) * g
    v = ADAM_B2 * v + (1.0 - ADAM_B2) * (g * g)
    m_hat = m / (1.0 - ADAM_B1 ** ADAM_STEP)
    v_hat = v / (1.0 - ADAM_B2 ** ADAM_STEP)
    delta = -ADAM_LR * (m_hat / (jnp.sqrt(v_hat) + ADAM_EPS) + ADAM_WD * w)
    return delta, m, v


def _row_tile(rows, cap=256):
    best = None
    for t in range(16, min(rows, cap) + 1, 16):
        if rows % t == 0:
            best = t
    return best if best is not None else rows


def _reduce_adamw(recv, w, m, v, name):
    n_l, r, c = w.shape
    tr = _row_tile(r)

    def body(recv_ref, w_ref, m_ref, v_ref, g_ref, d_ref, nm_ref, nv_ref):
        g = recv_ref[0].astype(F32)
        for p in range(1, N_DEV):
            g = g + recv_ref[p].astype(F32)
        g_ref[...] = g
        d_ref[...], nm_ref[...], nv_ref[...] = _adamw_math(w_ref[...], g, m_ref[...], v_ref[...])

    blk = pl.BlockSpec((None, tr, c), lambda l, i: (l, i, 0))
    return pl.pallas_call(
        body, name=name, grid=(n_l, r // tr),
        in_specs=[pl.BlockSpec((N_DEV, None, tr, c), lambda l, i: (0, l, i, 0)), blk, blk, blk],
        out_specs=[blk] * 4, out_shape=[SDS(w.shape, F32)] * 4, compiler_params=_cp(2),
    )(recv, w, m, v)


def _adamw_small(w, g, m, v, name):
    def body(w_ref, g_ref, m_ref, v_ref, d_ref, nm_ref, nv_ref):
        d_ref[...], nm_ref[...], nv_ref[...] = _adamw_math(w_ref[...], g_ref[...], m_ref[...], v_ref[...])

    vm = pl.BlockSpec(memory_space=pltpu.VMEM)
    return pl.pallas_call(body, name=name, in_specs=[vm] * 4, out_specs=[vm] * 3, out_shape=[SDS(w.shape, F32)] * 3)(w, g, m, v)


def _pack(arrs):
    flat = jnp.concatenate([a.reshape(-1).astype(F32) for a in arrs])
    pad = (-flat.shape[0]) % (8 * BLK)
    return jnp.pad(flat, (0, pad)).reshape(-1, BLK)


def _unpack(packed, shapes):
    flat = packed.reshape(-1)
    out, off = [], 0
    for shp in shapes:
        n = math.prod(shp)
        out.append(flat[off:off + n].reshape(shp))
        off += n
    return out


def _block_diag(w):
    nb, bw, _ = w.shape
    eye = jnp.eye(nb, dtype=w.dtype)
    return (eye[:, None, :, None] * w[:, :, None, :]).reshape(nb * bw, nb * bw)


def _diag_blocks(wd, nb):
    bw = wd.shape[0] // nb
    return jnp.stack([wd[b * bw:(b + 1) * bw, b * bw:(b + 1) * bw] for b in range(nb)])


WEIGHT_NAMES = ['ffn1_pre_g', 'ffn1_w_gu', 'ffn1_w_down', 'ffn1_post_g', 'mix_pre_g', 'w_in', 'lru_conv_w', 'lru_conv_b',
                'lru_w_a', 'lru_b_a', 'lru_w_x', 'lru_b_x', 'lru_lambda', 'attn_sinks', 'conv_w', 'conv_b', 'conv_ln_g',
                'conv_ln_b', 'group_g', 'w_out', 'mix_post_g', 'ffn2_pre_g', 'ffn2_w_gu', 'ffn2_w_down', 'ffn2_post_g']
BIG = ('ffn1_w_gu', 'ffn1_w_down', 'w_in', 'w_out', 'ffn2_w_gu', 'ffn2_w_down')
CHANNEL_SHARDED = ('lru_conv_w', 'conv_w')


def _step(x, target, w, m, v):
    n_l = w['ffn1_pre_g'].shape[0]
    s, d = x.shape[1], x.shape[2]
    x = x.reshape(s, d)
    target = target.reshape(s, d)
    _, _, _, me = _my_pos()

    conv_shard = _pack([w['lru_conv_w'], w['conv_w']])
    gathered = _all_gather([w[k].astype(BF16) for k in BIG] + [conv_shard], "all_gather_weights")
    wgu1, wd1, w_in_g, w_out_g, wgu2, wd2, conv_g = gathered
    w_in_full = jnp.transpose(w_in_g, (1, 2, 0, 3)).reshape(n_l, d, D_IN_PROJ)
    w_out_full = jnp.transpose(w_out_g, (1, 0, 2, 3)).reshape(n_l, -1, d)
    ch = W_A // N_DEV
    conv_parts = [_unpack(conv_g[p], [(n_l, LRU_K, ch), (n_l, CC_K, ch)]) for p in range(N_DEV)]
    lru_cw = jnp.concatenate([cp[0] for cp in conv_parts], axis=-1)
    cc_cw = jnp.concatenate([cp[1] for cp in conv_parts], axis=-1)
    cc_cw = jnp.pad(cc_cw, ((0, 0), (0, CC_HALO - CC_K), (0, 0)))
    vec = lambda name, l: w[name][l][None, :]

    saved = []
    h = x
    for l in range(n_l):
        sv = {'x0': h}
        x1, sv['h1'], sv['g1'], sv['u1'], sv['d1'] = _ffn_fwd(
            h, vec('ffn1_pre_g', l), vec('ffn1_post_g', l), wgu1, wd1, l, f"ffn1_fwd_l{l}")
        sv['x1'] = x1
        sv['hn'], lx, lg, q, k, vv, glu = _mix_in_fwd(x1, vec('mix_pre_g', l), w_in_full[l], f"mix_in_fwd_l{l}")
        sv.update(lx=lx, lg=lg, q=q, k=k, v=vv, glu=glu)
        lru_p = (lru_cw[l], vec('lru_conv_b', l), _block_diag(w['lru_w_a'][l]).astype(BF16), vec('lru_b_a', l),
                 _block_diag(w['lru_w_x'][l]).astype(BF16), vec('lru_b_x', l), vec('lru_lambda', l))
        cc_p = (cc_cw[l], vec('conv_b', l), vec('conv_ln_g', l), vec('conv_ln_b', l))
        sv.update(lru_p=lru_p, cc_p=cc_p)
        sv['ya'], sv['hs'] = _lru_fwd(lx, lg, lru_p, f"lru_fwd_l{l}")
        sv['yb'] = _attn_fwd(q, k, vv, w['attn_sinks'][l], f"attn_fwd_l{l}")
        sv['yc'] = _cc_fwd(glu, *cc_p, f"cconv_fwd_l{l}")
        x2, sv['o'] = _mix_out_fwd(x1, sv['ya'], sv['yb'], sv['yc'], vec('group_g', l), w_out_full[l],
                                   vec('mix_post_g', l), f"mix_out_fwd_l{l}")
        sv['x2'] = x2
        h, sv['h2'], sv['g2'], sv['u2'], sv['d2'] = _ffn_fwd(
            x2, vec('ffn2_pre_g', l), vec('ffn2_post_g', l), wgu2, wd2, l, f"ffn2_fwd_l{l}")
        saved.append(sv)

    loss_row, dh = _loss_head(h, target, "loss_head")
    loss = lax.psum(loss_row[0, 0], AXES)

    big = {k: [None] * n_l for k in BIG}
    small = [dict() for _ in range(n_l)]
    for l in reversed(range(n_l)):
        sv, sg = saved[l], small[l]
        dx2, dd, dg, du, sg['ffn2_pre_g'], sg['ffn2_post_g'] = _ffn_bwd_act(
            dh, sv['d2'], sv['x2'], vec('ffn2_pre_g', l), vec('ffn2_post_g', l), sv['g2'], sv['u2'], wgu2, wd2, l,
            f"ffn2_bwd_act_l{l}")
        dwg, dwu, dwd = _ffn_bwd_w(sv['h2'], dd, sv['g2'], sv['u2'], dg, du, f"ffn2_bwd_w_l{l}")
        big['ffn2_w_gu'][l] = [dwg, dwu]
        big['ffn2_w_down'][l] = [dwd.reshape(N_DEV, -1, d)]
        dya, dyb, dyc, dw_out, sg['mix_post_g'], sg['group_g'] = _mix_out_bwd(
            dx2, sv['o'], sv['ya'], sv['yb'], sv['yc'], vec('group_g', l), w_out_full[l], vec('mix_post_g', l),
            f"mix_out_bwd_l{l}")
        big['w_out'][l] = [dw_out.reshape(N_DEV, -1, d)]
        (dlx, dlg, sg['lru_conv_w'], sg['lru_conv_b'], dwa, sg['lru_b_a'], dwx, sg['lru_b_x'],
         sg['lru_lambda']) = _lru_bwd(dya, sv['lx'], sv['lg'], sv['hs'], sv['lru_p'], f"lru_bwd_l{l}")
        sg['lru_w_a'] = _diag_blocks(dwa, A_BLOCKS)
        sg['lru_w_x'] = _diag_blocks(dwx, A_BLOCKS)
        dq, dkc, dkp, dvc, dvp, dsk = _attn_bwd(dyb, sv['q'], sv['k'], sv['v'], w['attn_sinks'][l], f"attn_bwd_l{l}")
        sg['attn_sinks'] = dsk[:, 0]
        dy1, dcw, sg['conv_b'], sg['conv_ln_g'], sg['conv_ln_b'] = _cc_bwd_conv(dyc, sv['glu'], *sv['cc_p'],
                                                                                f"cconv_bwd_conv_l{l}")
        sg['conv_w'] = dcw[:CC_K]
        dglu = _cc_bwd_glu(dy1, sv['glu'], sv['cc_p'][0], f"cconv_bwd_glu_l{l}")
        dx1, dw_in, sg['mix_pre_g'] = _mix_in_bwd(dx2, sv['x1'], vec('mix_pre_g', l), sv['hn'], w_in_full[l],
                                                  dlx, dlg, dq, dkc, dkp, dvc, dvp, dglu, f"mix_in_bwd_l{l}")
        big['w_in'][l] = [jnp.transpose(dw_in.reshape(d, N_DEV, -1), (1, 0, 2))]
        dh, dd, dg, du, sg['ffn1_pre_g'], sg['ffn1_post_g'] = _ffn_bwd_act(
            dx1, sv['d1'], sv['x0'], vec('ffn1_pre_g', l), vec('ffn1_post_g', l), sv['g1'], sv['u1'], wgu1, wd1, l,
            f"ffn1_bwd_act_l{l}")
        dwg, dwu, dwd = _ffn_bwd_w(sv['h1'], dd, sv['g1'], sv['u1'], dg, du, f"ffn1_bwd_w_l{l}")
        big['ffn1_w_gu'][l] = [dwg, dwu]
        big['ffn1_w_down'][l] = [dwd.reshape(N_DEV, -1, d)]
    grad_x = dh.reshape(1, s, d)

    recvs = _grad_exchange([big[k] for k in BIG], "grad_exchange")
    out = {}
    for k, recv in zip(BIG, recvs):
        out[k] = _reduce_adamw(recv, w[k], m[k], v[k], f"reduce_adamw_{k}")

    small_names = [k for k in WEIGHT_NAMES if k not in BIG]
    small_shapes = [(n_l,) + tuple(small[0][k].shape) for k in small_names]
    part = _pack([jnp.stack([small[l][k] for l in range(n_l)]) for k in small_names])
    total = _all_reduce_small(part, "all_reduce_small")
    grads_small = dict(zip(small_names, _unpack(total, small_shapes)))
    for k in small_names:
        g = grads_small[k]
        if k in CHANNEL_SHARDED:
            g = lax.dynamic_slice_in_dim(g, me * ch, ch, axis=2)
        grads_small[k] = g.reshape(w[k].shape)
    packed = [_pack([src[k] for k in small_names]) for src in (w, grads_small, m, v)]
    res = _adamw_small(*packed, "adamw_small")
    shapes = [w[k].shape for k in small_names]
    for k, dlt, nm, nv in zip(small_names, *[_unpack(r, shapes) for r in res]):
        out[k] = (grads_small[k], dlt, nm, nv)

    return (loss, grad_x, *[out[k][0] for k in WEIGHT_NAMES], *[out[k][1] for k in WEIGHT_NAMES],
            *[out[k][2] for k in WEIGHT_NAMES], *[out[k][3] for k in WEIGHT_NAMES])


def kernel(x, ffn1_pre_g, ffn1_w_gu, ffn1_w_down, ffn1_post_g, mix_pre_g, w_in, lru_conv_w, lru_conv_b, lru_w_a, lru_b_a, lru_w_x, lru_b_x, lru_lambda, attn_sinks, conv_w, conv_b, conv_ln_g, conv_ln_b, group_g, w_out, mix_post_g, ffn2_pre_g, ffn2_w_gu, ffn2_w_down, ffn2_post_g, loss_target, m_ffn1_pre_g, m_ffn1_w_gu, m_ffn1_w_down, m_ffn1_post_g, m_mix_pre_g, m_w_in, m_lru_conv_w, m_lru_conv_b, m_lru_w_a, m_lru_b_a, m_lru_w_x, m_lru_b_x, m_lru_lambda, m_attn_sinks, m_conv_w, m_conv_b, m_conv_ln_g, m_conv_ln_b, m_group_g, m_w_out, m_mix_post_g, m_ffn2_pre_g, m_ffn2_w_gu, m_ffn2_w_down, m_ffn2_post_g, v_ffn1_pre_g, v_ffn1_w_gu, v_ffn1_w_down, v_ffn1_post_g, v_mix_pre_g, v_w_in, v_lru_conv_w, v_lru_conv_b, v_lru_w_a, v_lru_b_a, v_lru_w_x, v_lru_b_x, v_lru_lambda, v_attn_sinks, v_conv_w, v_conv_b, v_conv_ln_g, v_conv_ln_b, v_group_g, v_w_out, v_mix_post_g, v_ffn2_pre_g, v_ffn2_w_gu, v_ffn2_w_down, v_ffn2_post_g):
    args = (ffn1_pre_g, ffn1_w_gu, ffn1_w_down, ffn1_post_g, mix_pre_g, w_in, lru_conv_w, lru_conv_b, lru_w_a, lru_b_a, lru_w_x, lru_b_x, lru_lambda, attn_sinks, conv_w, conv_b, conv_ln_g, conv_ln_b, group_g, w_out, mix_post_g, ffn2_pre_g, ffn2_w_gu, ffn2_w_down, ffn2_post_g)
    ms = (m_ffn1_pre_g, m_ffn1_w_gu, m_ffn1_w_down, m_ffn1_post_g, m_mix_pre_g, m_w_in, m_lru_conv_w, m_lru_conv_b, m_lru_w_a, m_lru_b_a, m_lru_w_x, m_lru_b_x, m_lru_lambda, m_attn_sinks, m_conv_w, m_conv_b, m_conv_ln_g, m_conv_ln_b, m_group_g, m_w_out, m_mix_post_g, m_ffn2_pre_g, m_ffn2_w_gu, m_ffn2_w_down, m_ffn2_post_g)
    vs = (v_ffn1_pre_g, v_ffn1_w_gu, v_ffn1_w_down, v_ffn1_post_g, v_mix_pre_g, v_w_in, v_lru_conv_w, v_lru_conv_b, v_lru_w_a, v_lru_b_a, v_lru_w_x, v_lru_b_x, v_lru_lambda, v_attn_sinks, v_conv_w, v_conv_b, v_conv_ln_g, v_conv_ln_b, v_group_g, v_w_out, v_mix_post_g, v_ffn2_pre_g, v_ffn2_w_gu, v_ffn2_w_down, v_ffn2_post_g)
    return _step(x, loss_target, dict(zip(WEIGHT_NAMES, args)), dict(zip(WEIGHT_NAMES, ms)), dict(zip(WEIGHT_NAMES, vs)))
```

```python
import functools
import math
import operator

import jax
import jax.numpy as jnp
from jax import lax
from jax.experimental import pallas as pl
from jax.experimental.pallas import tpu as pltpu

F32 = jnp.float32
BF16 = jnp.bfloat16
N_DEV = 8
AXES = ("x", "y", "c")
MESH = pl.DeviceIdType.MESH

NORM_EPS = 1e-6
LN_EPS = 1e-5
NEG_BIG = -1e30
W_A = 256
W_B = 512
W_C = 256
HEAD_DIM = 64
N_Q_HEADS = 8
N_KV_HEADS = 2
Q_PER_KV = N_Q_HEADS // N_KV_HEADS
KV_W = N_KV_HEADS * HEAD_DIM
BLK = 128
LRU_K = 4
LRU_C = 8.0
A_BLOCKS = 4
CC_K = 31
CC_HALO = 32
LRU_HALO = 8
D_IN_PROJ = 2 * W_A + W_B + 2 * KV_W + 2 * W_C
ADAM_LR = 0.001
ADAM_B1 = 0.9
ADAM_B2 = 0.999
ADAM_EPS = 1e-08
ADAM_WD = 0.01
ADAM_STEP = 10
VMEM_LIMIT = 56 * 1024 * 1024

SDS = jax.ShapeDtypeStruct
ANY = pl.BlockSpec(memory_space=pl.ANY)


def _time_tile(s):
    return max(BLK, s // 8)


class _Exchange:
    def __init__(self, inputs, out_shapes, aliases, sem_shapes, start, wait):
        self.inputs, self.out_shapes, self.aliases, self.sem_shapes = inputs, out_shapes, aliases, sem_shapes
        self.start, self.wait = start, wait


def _my_pos():
    x, y, c = (lax.axis_index(a) for a in AXES)
    return x, y, c, 4 * x + 2 * y + c


def _flip(k):
    x, y, c, _ = _my_pos()
    return (1 - x if k & 4 else x, 1 - y if k & 2 else y, 1 - c if k & 1 else c)


def _slot(dev):
    return 4 * dev[0] + 2 * dev[1] + dev[2]


def _dev(p):
    return (p >> 2, (p >> 1) & 1, p & 1)


def _gather_a(items, two_level):
    rels = (1, 2, 4, 6) if two_level else tuple(range(1, N_DEV))
    n = len(items)
    src_of = lambda ins, a: ins[a] if items[a][1] is None else ins[a].at[items[a][1]]

    def shape_of(a):
        arr, l = items[a]
        return arr.shape if l is None else arr.shape[1:]

    def copies(ins, outs, sems, a):
        send, recv, _ = sems
        me = _my_pos()[3]
        return [(k, pltpu.make_async_remote_copy(
            src_ref=src_of(ins, a), dst_ref=outs[a].at[me], send_sem=send.at[a, k], recv_sem=recv.at[a, k],
            device_id=_flip(k), device_id_type=MESH)) for k in rels]

    def local(ins, outs, sems, a):
        return pltpu.make_async_copy(src_of(ins, a), outs[a].at[_my_pos()[3]], sems[2].at[a])

    def start(ins, outs, sems):
        for a in range(n):
            local(ins, outs, sems, a).start()
            for _, cp in copies(ins, outs, sems, a):
                cp.start()

    def wait(ins, outs, sems):
        send, recv, _ = sems
        for a in range(n):
            for k, cp in copies(ins, outs, sems, a):
                pltpu.make_async_remote_copy(
                    src_ref=src_of(ins, a), dst_ref=outs[a].at[_slot(_flip(k))], send_sem=send.at[a, k],
                    recv_sem=recv.at[a, k], device_id=_flip(k), device_id_type=MESH).wait_recv()
                cp.wait_send()
            local(ins, outs, sems, a).wait()

    return _Exchange([it[0] for it in items], [SDS((N_DEV,) + shape_of(a), items[a][0].dtype) for a in range(n)], {},
                     [pltpu.SemaphoreType.DMA((n, N_DEV)), pltpu.SemaphoreType.DMA((n, N_DEV)),
                      pltpu.SemaphoreType.DMA((n,))], start, wait)


def _gather_b(bufs):
    n = len(bufs)

    def copies(ins, outs, sems, a, c_of_block):
        send, recv = sems
        x, y, c, _ = _my_pos()
        res = []
        for k in (2, 4, 6):
            chip = _flip(k)
            blk = _slot((chip[0], chip[1], c if c_of_block == "mine" else 1 - c))
            res.append(pltpu.make_async_remote_copy(
                src_ref=ins[a].at[blk], dst_ref=outs[a].at[blk], send_sem=send.at[a, k], recv_sem=recv.at[a, k],
                device_id=_flip(1), device_id_type=MESH))
        return res

    def start(ins, outs, sems):
        for a in range(n):
            for cp in copies(ins, outs, sems, a, "mine"):
                cp.start()

    def wait(ins, outs, sems):
        for a in range(n):
            for cp in copies(ins, outs, sems, a, "sibling"):
                cp.wait_recv()
            for cp in copies(ins, outs, sems, a, "mine"):
                cp.wait_send()

    return _Exchange(list(bufs), [SDS(b.shape, b.dtype) for b in bufs], {a: a for a in range(n)},
                     [pltpu.SemaphoreType.DMA((n, N_DEV)), pltpu.SemaphoreType.DMA((n, N_DEV))], start, wait)


def _grad_x(items, n_l):
    n = len(items)
    inputs, first_in, recv_in, aliases, out_shapes = [], [], [], {}, []
    for a, (arrs, l, recv) in enumerate(items):
        first_in.append(len(inputs))
        inputs += list(arrs)
        assert sum(arr.shape[0] for arr in arrs) == N_DEV
        if recv is not None:
            aliases[len(inputs)] = a
            inputs.append(recv)
        out_shapes.append(SDS((N_DEV, n_l) + arrs[0].shape[1:], arrs[0].dtype))

    def slab(ins, a, p):
        off = 0
        for j, arr in enumerate(items[a][0]):
            if p < off + arr.shape[0]:
                return ins[first_in[a] + j].at[p - off]
            off += arr.shape[0]
        raise AssertionError

    def rdma(ins, outs, sems, a, p, src_dev):
        send, recv, _ = sems
        return pltpu.make_async_remote_copy(
            src_ref=slab(ins, a, p), dst_ref=outs[a].at[src_dev, items[a][1]], send_sem=send.at[a, p],
            recv_sem=recv.at[a, src_dev], device_id=_dev(p), device_id_type=MESH)

    def local(ins, outs, sems, a, p):
        return pltpu.make_async_copy(slab(ins, a, p), outs[a].at[p, items[a][1]], sems[2].at[a])

    def start(ins, outs, sems):
        me = _my_pos()[3]
        for p in range(N_DEV):
            @pl.when(me != p)
            def _():
                for a in range(n):
                    rdma(ins, outs, sems, a, p, me).start()

            @pl.when(me == p)
            def _():
                for a in range(n):
                    local(ins, outs, sems, a, p).start()

    def wait(ins, outs, sems):
        me = _my_pos()[3]
        for p in range(N_DEV):
            @pl.when(me != p)
            def _():
                for a in range(n):
                    rdma(ins, outs, sems, a, p, p).wait_recv()
                    rdma(ins, outs, sems, a, p, p).wait_send()

            @pl.when(me == p)
            def _():
                for a in range(n):
                    local(ins, outs, sems, a, p).wait()

    return _Exchange(inputs, out_shapes, aliases,
                     [pltpu.SemaphoreType.DMA((n, N_DEV)), pltpu.SemaphoreType.DMA((n, N_DEV)),
                      pltpu.SemaphoreType.DMA((n,))], start, wait)


def _pcall(body, args, *, name, grid, in_specs, out_specs, out_shape, scratch_shapes=(), carry=()):
    n_in, n_out, n_scr = len(in_specs), len(out_specs), len(scratch_shapes)
    c_in = [len(e.inputs) for e in carry]
    c_out = [len(e.out_shapes) for e in carry]
    c_sem = [len(e.sem_shapes) for e in carry]
    aliases = {}
    for j, e in enumerate(carry):
        for i_loc, o_loc in e.aliases.items():
            aliases[n_in + sum(c_in[:j]) + i_loc] = n_out + sum(c_out[:j]) + o_loc

    def wrapped(*refs):
        def take(counts, pos):
            groups = []
            for cnt in counts:
                groups.append(refs[pos:pos + cnt])
                pos += cnt
            return groups, pos

        (ins,), pos = take([n_in], 0)
        cins, pos = take(c_in, pos)
        (outs,), pos = take([n_out], pos)
        couts, pos = take(c_out, pos)
        (scr,), pos = take([n_scr], pos)
        csems, pos = take(c_sem, pos)
        if carry:
            ids = [pl.program_id(k) for k in range(len(grid))]
            first = functools.reduce(operator.and_, [i == 0 for i in ids])
            last = functools.reduce(operator.and_, [i == g - 1 for i, g in zip(ids, grid)])

            @pl.when(first)
            def _():
                for e, ci, co, cs in zip(carry, cins, couts, csems):
                    e.start(ci, co, cs)

        body(*ins, *outs, *scr)
        if carry:
            @pl.when(last)
            def _():
                for e, ci, co, cs in zip(carry, cins, couts, csems):
                    e.wait(ci, co, cs)

    res = pl.pallas_call(
        wrapped, name=name, grid=grid,
        in_specs=list(in_specs) + [ANY] * sum(c_in),
        out_specs=list(out_specs) + [ANY] * sum(c_out),
        out_shape=list(out_shape) + [s for e in carry for s in e.out_shapes],
        scratch_shapes=list(scratch_shapes) + [s for e in carry for s in e.sem_shapes],
        input_output_aliases=aliases,
        compiler_params=pltpu.CompilerParams(dimension_semantics=("arbitrary",) * len(grid),
                                             vmem_limit_bytes=VMEM_LIMIT),
    )(*args, *[a for e in carry for a in e.inputs])
    outs, pos, extra = list(res[:n_out]), n_out, []
    for cnt in c_out:
        extra.append(list(res[pos:pos + cnt]))
        pos += cnt
    return outs, extra


def _all_gather(items, name):
    n = len(items)
    shape_of = lambda a: items[a][0].shape if items[a][1] is None else items[a][0].shape[1:]

    def body(*refs):
        ins, outs, (send_sems, recv_sems, local_sems) = refs[:n], refs[n:2 * n], refs[2 * n:]
        x, y, c, me = _my_pos()
        src_of = lambda a: ins[a] if items[a][1] is None else ins[a].at[items[a][1]]

        def copy(a, k, block, to, src=None):
            dst = outs[a].at[_slot(block)]
            return pltpu.make_async_remote_copy(
                src_ref=dst if src is None else src, dst_ref=dst,
                send_sem=send_sems.at[a, k], recv_sem=recv_sems.at[a, k], device_id=to, device_id_type=MESH)

        mine = [pltpu.make_async_copy(src_of(a), outs[a].at[me], local_sems.at[a]) for a in range(n)]
        for cp in mine:
            cp.start()
        first = [copy(a, k, (x, y, c), _flip(k), src=src_of(a)) for a in range(n) for k in (1, 2, 4, 6)]
        for cp in first:
            cp.start()
        passed = []
        for k in (2, 4, 6):
            for a in range(n):
                copy(a, k, _flip(k), (x, y, c)).wait_recv()
                fwd = copy(a, k + 1, _flip(k), _flip(1))
                fwd.start()
                passed.append(fwd)
        for a in range(n):
            copy(a, 1, _flip(1), (x, y, c)).wait_recv()
            for k in (2, 4, 6):
                copy(a, k + 1, _flip(k + 1), (x, y, c)).wait_recv()
        for cp in first + passed:
            cp.wait_send()
        for cp in mine:
            cp.wait()

    return pl.pallas_call(
        body, name=name,
        in_specs=[ANY] * n, out_specs=[ANY] * n,
        out_shape=[SDS((N_DEV,) + shape_of(a), items[a][0].dtype) for a in range(n)],
        scratch_shapes=[pltpu.SemaphoreType.DMA((n, N_DEV)), pltpu.SemaphoreType.DMA((n, N_DEV)),
                        pltpu.SemaphoreType.DMA((n,))],
    )(*[it[0] for it in items])


def _mm(a, b):
    return jnp.dot(a.astype(BF16), b.astype(BF16), preferred_element_type=F32)


def _mm_nt(a, b):
    return lax.dot_general(a.astype(BF16), b.astype(BF16), (((1,), (1,)), ((), ())), preferred_element_type=F32)


def _mm_tn(a, b):
    return lax.dot_general(a.astype(BF16), b.astype(BF16), (((0,), (0,)), ((), ())), preferred_element_type=F32)


def _rms_r(x):
    return lax.rsqrt(jnp.mean(x * x, axis=-1, keepdims=True) + NORM_EPS)


def _rms_bwd(x, r, g, dy):
    gy = dy * g
    dx = r * (gy - x * (r * r) * jnp.mean(gy * x, axis=-1, keepdims=True))
    dg = jnp.sum(dy * x * r, axis=0, keepdims=True)
    return dx, dg


def _sigmoid(x):
    return 1.0 / (1.0 + jnp.exp(-x))


def _dsilu(z, sz):
    return sz * (1.0 + z * (1.0 - sz))


_GELU_C = math.sqrt(2.0 / math.pi)


def _gelu(x):
    t = jnp.tanh(_GELU_C * (x + 0.044715 * x * x * x))
    return 0.5 * x * (1.0 + t), t


def _dgelu(x, t):
    return 0.5 * (1.0 + t) + 0.5 * x * (1.0 - t * t) * _GELU_C * (1.0 + 3.0 * 0.044715 * x * x)


def _log1p(e):
    return jnp.where(e < 1e-2, e * (1.0 - e * (0.5 - e * (1.0 / 3.0))), jnp.log(1.0 + e))


def _softplus(x):
    return jnp.maximum(x, 0.0) + _log1p(jnp.exp(-jnp.abs(x)))


def _neg_expm1(x):
    small = -x * (1.0 + x * (0.5 + x * (1.0 / 6.0) * (1.0 + x * 0.25)))
    return jnp.where(x > -1e-2, small, 1.0 - jnp.exp(x))


def _shift_down(x, s):
    return x if s == 0 else pltpu.roll(x, s, 0)


def _shift_up(x, s):
    return x if s == 0 else pltpu.roll(x, x.shape[0] - s, 0)


def _ffn_wspecs(d, fc, order):
    f_of = (lambda i, f: f) if order == "tf" else (lambda f, i: f)
    n_f = N_DEV // 2
    return [pl.BlockSpec((None, fc, d), lambda *g: (f_of(*g), 0, 0)),
            pl.BlockSpec((None, fc, d), lambda *g: (f_of(*g) + n_f, 0, 0)),
            pl.BlockSpec((2, fc // 2, d), lambda *g: (f_of(*g), 0, 0))]


def _ffn_fwd(x, pre_g, post_g, wgu_t, wd, name, carry=()):
    s, d = x.shape
    fc = wgu_t.shape[1]
    ts = _time_tile(s)
    n_t, n_f = s // ts, N_DEV // 2

    def body(x_ref, pg_ref, qg_ref, wg_ref, wu_ref, wd_ref, xo_ref, h_ref, g_ref, u_ref, d_ref, h_scr, acc):
        f = pl.program_id(1)

        @pl.when(f == 0)
        def _():
            xv = x_ref[...]
            hv = (xv * _rms_r(xv) * pg_ref[...]).astype(BF16)
            h_scr[...] = hv
            h_ref[...] = hv
            acc[...] = jnp.zeros_like(acc)

        hv = h_scr[...]
        g = _mm_nt(hv, wg_ref[...])
        u = _mm_nt(hv, wu_ref[...])
        g_ref[...] = g.astype(BF16)
        u_ref[...] = u.astype(BF16)
        a = (g * _sigmoid(g) * u).astype(BF16)
        acc[...] += jnp.dot(a, wd_ref[...].reshape(fc, d), preferred_element_type=F32)

        @pl.when(f == n_f - 1)
        def _():
            dv = acc[...]
            d_ref[...] = dv
            xo_ref[...] = x_ref[...] + 0.5 * (dv * _rms_r(dv) * qg_ref[...])

    row = pl.BlockSpec((ts, d), lambda i, f: (i, 0))
    vec = pl.BlockSpec((1, d), lambda i, f: (0, 0))
    act = pl.BlockSpec((None, ts, fc), lambda i, f: (f, i, 0))
    return _pcall(
        body, (x, pre_g, post_g, wgu_t, wgu_t, wd), name=name, grid=(n_t, n_f),
        in_specs=[row, vec, vec] + _ffn_wspecs(d, fc, "tf"),
        out_specs=[row, row, act, act, row],
        out_shape=[SDS((s, d), F32), SDS((s, d), BF16), SDS((n_f, s, fc), BF16), SDS((n_f, s, fc), BF16),
                   SDS((s, d), F32)],
        scratch_shapes=[pltpu.VMEM((ts, d), BF16), pltpu.VMEM((ts, d), F32)], carry=carry)


def _ffn_bwd_act(dxo, dmid, x, pre_g, post_g, g_s, u_s, wgu_t, wd, name, carry=()):
    s, d = x.shape
    fc = wgu_t.shape[1]
    ts = _time_tile(s)
    n_t, n_f = s // ts, N_DEV // 2

    def body(dxo_ref, dm_ref, x_ref, pg_ref, qg_ref, g_ref, u_ref, wg_ref, wu_ref, wd_ref,
             dx_ref, dd_ref, dg_ref, du_ref, dpg_ref, dqg_ref, dd_scr, dh_acc):
        i, f = pl.program_id(0), pl.program_id(1)

        @pl.when((i == 0) & (f == 0))
        def _():
            dpg_ref[...] = jnp.zeros_like(dpg_ref)
            dqg_ref[...] = jnp.zeros_like(dqg_ref)

        @pl.when(f == 0)
        def _():
            dv = dm_ref[...]
            ddv, dq = _rms_bwd(dv, _rms_r(dv), qg_ref[...], 0.5 * dxo_ref[...])
            dqg_ref[...] += dq
            dd_scr[...] = ddv.astype(BF16)
            dd_ref[...] = ddv.astype(BF16)
            dh_acc[...] = jnp.zeros_like(dh_acc)

        da = _mm_nt(dd_scr[...], wd_ref[...].reshape(fc, d))
        g = g_ref[...].astype(F32)
        u = u_ref[...].astype(F32)
        sg = _sigmoid(g)
        du = (da * (g * sg)).astype(BF16)
        dg = (da * u * _dsilu(g, sg)).astype(BF16)
        dg_ref[...] = dg
        du_ref[...] = du
        dh_acc[...] += _mm(dg, wg_ref[...]) + _mm(du, wu_ref[...])

        @pl.when(f == n_f - 1)
        def _():
            xv = x_ref[...]
            dxv, dp = _rms_bwd(xv, _rms_r(xv), pg_ref[...], dh_acc[...])
            dpg_ref[...] += dp
            dx_ref[...] = dxo_ref[...] + dxv

    row = pl.BlockSpec((ts, d), lambda i, f: (i, 0))
    vec = pl.BlockSpec((1, d), lambda i, f: (0, 0))
    act = pl.BlockSpec((None, ts, fc), lambda i, f: (f, i, 0))
    return _pcall(
        body, (dxo, dmid, x, pre_g, post_g, g_s, u_s, wgu_t, wgu_t, wd), name=name, grid=(n_t, n_f),
        in_specs=[row, row, row, vec, vec, act, act] + _ffn_wspecs(d, fc, "tf"),
        out_specs=[row, row, act, act, vec, vec],
        out_shape=[SDS((s, d), F32), SDS((s, d), BF16), SDS((n_f, s, fc), BF16), SDS((n_f, s, fc), BF16),
                   SDS((1, d), F32), SDS((1, d), F32)],
        scratch_shapes=[pltpu.VMEM((ts, d), BF16), pltpu.VMEM((ts, d), F32)], carry=carry)


def _ffn_bwd_w(h, dd, g_s, u_s, dg, du, name, carry=()):
    s, d = h.shape
    n_f, _, fc = g_s.shape
    ts = _time_tile(s)
    n_t = s // ts

    def body(h_ref, dd_ref, g_ref, u_ref, dg_ref, du_ref, wg_ref, wu_ref, wd_ref, acc_g, acc_u, acc_d):
        i = pl.program_id(1)

        @pl.when(i == 0)
        def _():
            acc_g[...] = jnp.zeros_like(acc_g)
            acc_u[...] = jnp.zeros_like(acc_u)
            acc_d[...] = jnp.zeros_like(acc_d)

        g = g_ref[...].astype(F32)
        a = (g * _sigmoid(g) * u_ref[...].astype(F32)).astype(BF16)
        hv = h_ref[...]
        acc_g[...] += _mm_tn(dg_ref[...], hv)
        acc_u[...] += _mm_tn(du_ref[...], hv)
        acc_d[...] += _mm_tn(a, dd_ref[...])

        @pl.when(i == n_t - 1)
        def _():
            wg_ref[...] = acc_g[...].astype(BF16)
            wu_ref[...] = acc_u[...].astype(BF16)
            wd_ref[...] = acc_d[...].astype(BF16)

    row = pl.BlockSpec((ts, d), lambda f, i: (i, 0))
    act = pl.BlockSpec((None, ts, fc), lambda f, i: (f, i, 0))
    out = pl.BlockSpec((None, fc, d), lambda f, i: (f, 0, 0))
    return _pcall(
        body, (h, dd, g_s, u_s, dg, du), name=name, grid=(n_f, n_t),
        in_specs=[row, row, act, act, act, act], out_specs=[out, out, out],
        out_shape=[SDS((n_f, fc, d), BF16)] * 3,
        scratch_shapes=[pltpu.VMEM((fc, d), F32)] * 3, carry=carry)


_PROJ_WIDTHS = (W_A, W_A, W_B, KV_W, KV_W, 2 * W_C)


def _mix_in_fwd(x, pre_g, w_in_t, name, carry=()):
    s, d = x.shape
    ts = _time_tile(s)

    def body(x_ref, pg_ref, w_ref, hn_ref, *outs):
        xv = x_ref[...]
        hn = (xv * _rms_r(xv) * pg_ref[...]).astype(BF16)
        hn_ref[...] = hn
        proj = _mm_nt(hn, w_ref[...])
        off = 0
        for o_ref, w in zip(outs, _PROJ_WIDTHS):
            o_ref[...] = proj[:, off:off + w]
            off += w

    row = lambda w: pl.BlockSpec((ts, w), lambda i: (i, 0))
    return _pcall(
        body, (x, pre_g, w_in_t), name=name, grid=(s // ts,),
        in_specs=[row(d), pl.BlockSpec((1, d), lambda i: (0, 0)), pl.BlockSpec((D_IN_PROJ, d), lambda i: (0, 0))],
        out_specs=[row(d)] + [row(w) for w in _PROJ_WIDTHS],
        out_shape=[SDS((s, d), BF16)] + [SDS((s, w), F32) for w in _PROJ_WIDTHS], carry=carry)


def _mix_in_bwd(dres, x, pre_g, hn, w_in_t, dlx, dlg, dq, dk, dk_up, dv, dv_up, dglu, name, carry=()):
    s, d = x.shape
    ts = _time_tile(s)
    n_t = s // ts

    def body(dres_ref, x_ref, pg_ref, hn_ref, w_ref, dlx_ref, dlg_ref, dq_ref, dk_ref, dkn_ref,
             dv_ref, dvn_ref, dglu_ref, dx_ref, dw_ref, dpg_ref, acc):
        i = pl.program_id(0)

        @pl.when(i == 0)
        def _():
            acc[...] = jnp.zeros_like(acc)
            dpg_ref[...] = jnp.zeros_like(dpg_ref)

        def with_next(cur_ref, nxt_ref):
            nxt = jnp.where(i < n_t - 1, nxt_ref[...], 0.0)
            if ts == BLK:
                return cur_ref[...] + nxt
            return jnp.concatenate([cur_ref[:ts - BLK, :], cur_ref[ts - BLK:, :] + nxt], axis=0)

        dproj = jnp.concatenate([dlx_ref[...], dlg_ref[...], dq_ref[...], with_next(dk_ref, dkn_ref),
                                 with_next(dv_ref, dvn_ref), dglu_ref[...]], axis=1).astype(BF16)
        dhn = _mm(dproj, w_ref[...])
        acc[...] += _mm_tn(dproj, hn_ref[...])
        xv = x_ref[...]
        dxv, dp = _rms_bwd(xv, _rms_r(xv), pg_ref[...], dhn)
        dpg_ref[...] += dp
        dx_ref[...] = dres_ref[...] + dxv

        @pl.when(i == n_t - 1)
        def _():
            dw_ref[...] = acc[...].astype(BF16)

    row = lambda w: pl.BlockSpec((ts, w), lambda i: (i, 0))
    nxt = pl.BlockSpec((BLK, KV_W), lambda i: (jnp.minimum(i + 1, n_t - 1), 0))
    vec = pl.BlockSpec((1, d), lambda i: (0, 0))
    full = pl.BlockSpec((D_IN_PROJ, d), lambda i: (0, 0))
    return _pcall(
        body, (dres, x, pre_g, hn, w_in_t, dlx, dlg, dq, dk, dk_up, dv, dv_up, dglu), name=name, grid=(n_t,),
        in_specs=[row(d), row(d), vec, row(d), full, row(W_A), row(W_A), row(W_B), row(KV_W), nxt,
                  row(KV_W), nxt, row(2 * W_C)],
        out_specs=[row(d), full, vec],
        out_shape=[SDS((s, d), F32), SDS((D_IN_PROJ, d), BF16), SDS((1, d), F32)],
        scratch_shapes=[pltpu.VMEM((D_IN_PROJ, d), F32)], carry=carry)


def _lru_gates(xc, lru_p):
    cw_ref, cb_ref, wa_ref, ba_ref, wx_ref, bx_ref, lam_ref = lru_p
    c = cb_ref[...]
    for j in range(LRU_K):
        c = c + cw_ref[j:j + 1, :] * _shift_down(xc, LRU_K - 1 - j)[LRU_HALO:, :]
    r = _sigmoid(_mm(c, wa_ref[...]) + ba_ref[...])
    ig = _sigmoid(_mm(c, wx_ref[...]) + bx_ref[...])
    sp = _softplus(-lam_ref[...])
    log_a = -LRU_C * r * sp
    a = jnp.exp(log_a)
    m = jnp.sqrt(_neg_expm1(2.0 * log_a))
    return c, r, ig, sp, a, m


def _lru_pspecs():
    small = lambda r: pl.BlockSpec((r, W_A), lambda i: (0, 0))
    return [small(LRU_K), small(1), small(W_A), small(1), small(W_A), small(1), small(1)]


def _lru_fwd(lx, lg, lru_p, name, carry=()):
    s = lx.shape[0]
    ts = _time_tile(s)
    n8 = ts // LRU_HALO

    def body(lx_ref, lxp_ref, lg_ref, *rest):
        lru_p, (ya_ref, h_ref, hcarry) = rest[:7], rest[7:]
        i = pl.program_id(0)
        prev = jnp.where(i > 0, lxp_ref[...], 0.0)
        xc = jnp.concatenate([prev, lx_ref[...]], axis=0)
        c, r, ig, sp, a, m = _lru_gates(xc, lru_p)
        acc_a, acc_b = a, m * (ig * c)
        t = lax.broadcasted_iota(jnp.int32, a.shape, 0)
        k = 1
        while k < ts:
            keep = t >= k
            acc_b = jnp.where(keep, acc_a * _shift_down(acc_b, k) + acc_b, acc_b)
            acc_a = jnp.where(keep, acc_a * _shift_down(acc_a, k), acc_a)
            k *= 2
        h0 = jnp.where(i > 0, hcarry[...], 0.0)
        h = acc_b + acc_a * h0
        hcarry[...] = h[ts - 1:ts, :]
        h_ref[...] = h
        ya_ref[...] = _gelu(lg_ref[...])[0] * h

    row = pl.BlockSpec((ts, W_A), lambda i: (i, 0))
    prev8 = pl.BlockSpec((LRU_HALO, W_A), lambda i: (jnp.maximum(i * n8 - 1, 0), 0))
    return _pcall(
        body, (lx, lx, lg, *lru_p), name=name, grid=(s // ts,),
        in_specs=[row, prev8, row] + _lru_pspecs(), out_specs=[row, row],
        out_shape=[SDS((s, W_A), F32), SDS((s, W_A), F32)],
        scratch_shapes=[pltpu.VMEM((1, W_A), F32)], carry=carry)


def _lru_bwd(dya, lx, lg, h_s, lru_p, name, carry=()):
    s = lx.shape[0]
    ts = _time_tile(s)
    n_t = s // ts
    n8 = ts // LRU_HALO

    def body(dya_ref, lx_ref, lxp_ref, lg_ref, h_ref, hp_ref, *rest):
        lru_p = rest[:7]
        (dlx_ref, dlg_ref, dcw_ref, dcb_ref, dwa_ref, dba_ref, dwx_ref, dbx_ref, dlam_ref,
         carry_a, carry_l, carry_dc) = rest[7:]
        cw_ref, _, wa_ref, _, wx_ref, _, lam_ref = lru_p
        i = pl.program_id(0)
        first_tile = i == n_t - 1
        last_tile = i == 0

        @pl.when(i == 0)
        def _():
            for ref in (dcw_ref, dcb_ref, dwa_ref, dba_ref, dwx_ref, dbx_ref, dlam_ref):
                ref[...] = jnp.zeros_like(ref)

        prev = jnp.where(first_tile, 0.0, lxp_ref[...])
        xc = jnp.concatenate([prev, lx_ref[...]], axis=0)
        c, r, ig, sp, a, m = _lru_gates(xc, lru_p)
        h = h_ref[...]
        hcat = jnp.concatenate([jnp.where(first_tile, 0.0, hp_ref[...]), h], axis=0)
        h_m1 = _shift_down(hcat, 1)[LRU_HALO:, :]
        lg = lg_ref[...]
        ge, th = _gelu(lg)
        dya = dya_ref[...]
        dlg_ref[...] = dya * h * _dgelu(lg, th)
        dh = dya * ge
        t = lax.broadcasted_iota(jnp.int32, a.shape, 0)
        a_next = jnp.where(t < ts - 1, _shift_up(a, 1), jnp.where(last_tile, 0.0, carry_a[...]))
        acc_a, acc_b = a_next, dh
        k = 1
        while k < ts:
            keep = t < ts - k
            acc_b = jnp.where(keep, acc_a * _shift_up(acc_b, k) + acc_b, acc_b)
            acc_a = jnp.where(keep, acc_a * _shift_up(acc_a, k), acc_a)
            k *= 2
        lam_beyond = jnp.where(last_tile, 0.0, carry_l[...])
        lmb = acc_b + acc_a * lam_beyond
        carry_a[...] = a[0:1, :]
        carry_l[...] = lmb[0:1, :]
        gi = ig * c
        dgi = lmb * m
        dla = lmb * h_m1 * a - (lmb * gi) * (a * a) / m
        dr = dla * (-LRU_C * sp)
        dsp = jnp.sum(dla * (-LRU_C * r), axis=0, keepdims=True)
        dlam_ref[...] += -dsp * _sigmoid(-lam_ref[...])
        dra = dr * r * (1.0 - r)
        dia = dgi * c * ig * (1.0 - ig)
        dc = dgi * ig + _mm_nt(dra, wa_ref[...]) + _mm_nt(dia, wx_ref[...])
        dwa_ref[...] += _mm_tn(c, dra)
        dwx_ref[...] += _mm_tn(c, dia)
        dba_ref[...] += jnp.sum(dra, axis=0, keepdims=True)
        dbx_ref[...] += jnp.sum(dia, axis=0, keepdims=True)
        dcb_ref[...] += jnp.sum(dc, axis=0, keepdims=True)
        dcc = jnp.concatenate([dc, jnp.where(last_tile, 0.0, carry_dc[...])], axis=0)
        carry_dc[...] = dc[0:LRU_HALO, :]
        dlx = jnp.zeros_like(dc)
        for j in range(LRU_K):
            sh = LRU_K - 1 - j
            dcw_ref[j:j + 1, :] += jnp.sum(dc * _shift_down(xc, sh)[LRU_HALO:, :], axis=0, keepdims=True)
            dlx = dlx + cw_ref[j:j + 1, :] * _shift_up(dcc, sh)[:ts, :]
        dlx_ref[...] = dlx

    row = pl.BlockSpec((ts, W_A), lambda i: (n_t - 1 - i, 0))
    prev8 = pl.BlockSpec((LRU_HALO, W_A), lambda i: (jnp.maximum((n_t - 1 - i) * n8 - 1, 0), 0))
    small = lambda r: pl.BlockSpec((r, W_A), lambda i: (0, 0))
    return _pcall(
        body, (dya, lx, lx, lg, h_s, h_s, *lru_p), name=name, grid=(n_t,),
        in_specs=[row, row, prev8, row, row, prev8] + _lru_pspecs(),
        out_specs=[row, row, small(LRU_K), small(1), small(W_A), small(1), small(W_A), small(1), small(1)],
        out_shape=[SDS((s, W_A), F32), SDS((s, W_A), F32), SDS((LRU_K, W_A), F32), SDS((1, W_A), F32),
                   SDS((W_A, W_A), F32), SDS((1, W_A), F32), SDS((W_A, W_A), F32), SDS((1, W_A), F32),
                   SDS((1, W_A), F32)],
        scratch_shapes=[pltpu.VMEM((1, W_A), F32), pltpu.VMEM((1, W_A), F32), pltpu.VMEM((LRU_HALO, W_A), F32)],
        carry=carry)


def _attn_probs(qh, kc, kp, sink, prev_ok):
    qi = lax.broadcasted_iota(jnp.int32, (BLK, BLK), 0)
    kj = lax.broadcasted_iota(jnp.int32, (BLK, BLK), 1)
    scale = 1.0 / math.sqrt(HEAD_DIM)
    sc = jnp.where(kj <= qi, _mm_nt(qh, kc) * scale, NEG_BIG)
    sp = jnp.where((kj > qi) if prev_ok is True else ((kj > qi) & prev_ok), _mm_nt(qh, kp) * scale, NEG_BIG)
    m = jnp.maximum(jnp.maximum(jnp.max(sc, axis=-1, keepdims=True), jnp.max(sp, axis=-1, keepdims=True)), sink)
    pc = jnp.exp(sc - m)
    pp = jnp.exp(sp - m)
    es = jnp.exp(sink - m)
    inv = 1.0 / (jnp.sum(pc, axis=-1, keepdims=True) + jnp.sum(pp, axis=-1, keepdims=True) + es)
    return pc * inv, pp * inv, es * inv


def _attn_operands(q_ref, k_ref, kp_ref, v_ref, vp_ref, i, b, h):
    g = h // Q_PER_KV
    rows, cols = slice(b * BLK, (b + 1) * BLK), slice(g * HEAD_DIM, (g + 1) * HEAD_DIM)
    qh = q_ref[rows, h * HEAD_DIM:(h + 1) * HEAD_DIM]
    if b == 0:
        return qh, k_ref[rows, cols], kp_ref[:, cols], v_ref[rows, cols], vp_ref[:, cols], i > 0
    prev = slice((b - 1) * BLK, b * BLK)
    return qh, k_ref[rows, cols], k_ref[prev, cols], v_ref[rows, cols], v_ref[prev, cols], True


def _attn_specs(s, ts):
    bpt = ts // BLK
    tile = lambda w: pl.BlockSpec((ts, w), lambda i: (i, 0))
    prv = pl.BlockSpec((BLK, KV_W), lambda i: (jnp.maximum(i * bpt - 1, 0), 0))
    return bpt, tile, prv


def _attn_fwd(q, k, v, sinks, name, carry=()):
    s = q.shape[0]
    ts = _time_tile(s)
    bpt, tile, prv = _attn_specs(s, ts)

    def body(q_ref, k_ref, kp_ref, v_ref, vp_ref, sk_ref, y_ref):
        i = pl.program_id(0)
        for b in range(bpt):
            outs = []
            for h in range(N_Q_HEADS):
                qh, kc, kp, vc, vp, ok = _attn_operands(q_ref, k_ref, kp_ref, v_ref, vp_ref, i, b, h)
                pc, pp, _ = _attn_probs(qh, kc, kp, sk_ref[h], ok)
                outs.append(_mm(pc, vc) + _mm(pp, vp))
            y_ref[b * BLK:(b + 1) * BLK, :] = jnp.concatenate(outs, axis=1)

    return _pcall(
        body, (q, k, k, v, v, sinks), name=name, grid=(s // ts,),
        in_specs=[tile(W_B), tile(KV_W), prv, tile(KV_W), prv, pl.BlockSpec(memory_space=pltpu.SMEM)],
        out_specs=[tile(W_B)], out_shape=[SDS((s, W_B), F32)], carry=carry)


def _attn_bwd(dy, q, k, v, sinks, name, carry=()):
    s = q.shape[0]
    ts = _time_tile(s)
    n_t = s // ts
    bpt, tile, prv = _attn_specs(s, ts)

    def body(dy_ref, q_ref, k_ref, kp_ref, v_ref, vp_ref, sk_ref, dq_ref, dk_ref, dv_ref, dku_ref, dvu_ref, dsk_ref):
        i = pl.program_id(0)

        @pl.when(i == 0)
        def _():
            dsk_ref[...] = jnp.zeros_like(dsk_ref)

        scale = 1.0 / math.sqrt(HEAD_DIM)
        head_row = lax.broadcasted_iota(jnp.int32, (N_Q_HEADS, BLK), 0)
        dsk = jnp.zeros((N_Q_HEADS, BLK), F32)
        add = lambda old, new: new if old is None else old + new
        dk_blocks, dv_blocks = [], []
        for b in range(bpt):
            dqs = []
            dkc, dkp, dvc, dvp = ([None] * N_KV_HEADS for _ in range(4))
            for h in range(N_Q_HEADS):
                g = h // Q_PER_KV
                qh, kc, kp, vc, vp, ok = _attn_operands(q_ref, k_ref, kp_ref, v_ref, vp_ref, i, b, h)
                pc, pp, ps = _attn_probs(qh, kc, kp, sk_ref[h], ok)
                do = dy_ref[b * BLK:(b + 1) * BLK, h * HEAD_DIM:(h + 1) * HEAD_DIM]
                dpc = _mm_nt(do, vc)
                dpp = _mm_nt(do, vp)
                delta = jnp.sum(pc * dpc, axis=-1, keepdims=True) + jnp.sum(pp * dpp, axis=-1, keepdims=True)
                dsc = pc * (dpc - delta) * scale
                dsp = pp * (dpp - delta) * scale
                dqs.append(_mm(dsc, kc) + _mm(dsp, kp))
                dkc[g] = add(dkc[g], _mm_tn(dsc, qh))
                dkp[g] = add(dkp[g], _mm_tn(dsp, qh))
                dvc[g] = add(dvc[g], _mm_tn(pc, do))
                dvp[g] = add(dvp[g], _mm_tn(pp, do))
                dsk = dsk + jnp.where(head_row == h, jnp.sum(-ps * delta, axis=0, keepdims=True), 0.0)
            dq_ref[b * BLK:(b + 1) * BLK, :] = jnp.concatenate(dqs, axis=1)
            dk_blocks.append(jnp.concatenate(dkc, axis=1))
            dv_blocks.append(jnp.concatenate(dvc, axis=1))
            if b == 0:
                dku_ref[...] = jnp.concatenate(dkp, axis=1)
                dvu_ref[...] = jnp.concatenate(dvp, axis=1)
            else:
                dk_blocks[b - 1] = dk_blocks[b - 1] + jnp.concatenate(dkp, axis=1)
                dv_blocks[b - 1] = dv_blocks[b - 1] + jnp.concatenate(dvp, axis=1)
        for b in range(bpt):
            dk_ref[b * BLK:(b + 1) * BLK, :] = dk_blocks[b]
            dv_ref[b * BLK:(b + 1) * BLK, :] = dv_blocks[b]
        dsk_ref[...] += dsk

    up = pl.BlockSpec((BLK, KV_W), lambda i: (i, 0))
    return _pcall(
        body, (dy, q, k, k, v, v, sinks), name=name, grid=(n_t,),
        in_specs=[tile(W_B), tile(W_B), tile(KV_W), prv, tile(KV_W), prv, pl.BlockSpec(memory_space=pltpu.SMEM)],
        out_specs=[tile(W_B), tile(KV_W), tile(KV_W), up, up, pl.BlockSpec((N_Q_HEADS, BLK), lambda i: (0, 0))],
        out_shape=[SDS((s, W_B), F32), SDS((s, KV_W), F32), SDS((s, KV_W), F32), SDS((n_t * BLK, KV_W), F32),
                   SDS((n_t * BLK, KV_W), F32), SDS((N_Q_HEADS, BLK), F32)], carry=carry)


def _cc_recompute(glu_ref, glup_ref, cw_ref, cb_ref, first_tile):
    prev = jnp.where(first_tile, 0.0, glup_ref[...])
    ge = jnp.concatenate([prev, glu_ref[...]], axis=0)
    y0 = ge[:, :W_C] * _sigmoid(ge[:, W_C:])
    y1 = cb_ref[...]
    for j in range(CC_K):
        y1 = y1 + cw_ref[j:j + 1, :] * _shift_down(y0, CC_K - 1 - j)[CC_HALO:, :]
    return y0, y1


def _ln_stats(y1):
    mu = jnp.mean(y1, axis=-1, keepdims=True)
    xc = y1 - mu
    rstd = lax.rsqrt(jnp.mean(xc * xc, axis=-1, keepdims=True) + LN_EPS)
    return xc * rstd, rstd


def _cc_specs(s, ts):
    n32 = ts // CC_HALO
    row = lambda w: pl.BlockSpec((ts, w), lambda i: (i, 0))
    prev = pl.BlockSpec((CC_HALO, 2 * W_C), lambda i: (jnp.maximum(i * n32 - 1, 0), 0))
    small = lambda r: pl.BlockSpec((r, W_C), lambda i: (0, 0))
    return row, prev, small


def _cc_fwd(glu, cw, cb, lng, lnb, name, carry=()):
    s = glu.shape[0]
    ts = _time_tile(s)
    row, prev, small = _cc_specs(s, ts)

    def body(glu_ref, glup_ref, cw_ref, cb_ref, lng_ref, lnb_ref, y_ref):
        _, y1 = _cc_recompute(glu_ref, glup_ref, cw_ref, cb_ref, pl.program_id(0) == 0)
        xhat, _ = _ln_stats(y1)
        z = xhat * lng_ref[...] + lnb_ref[...]
        y_ref[...] = z * _sigmoid(z)

    return _pcall(
        body, (glu, glu, cw, cb, lng, lnb), name=name, grid=(s // ts,),
        in_specs=[row(2 * W_C), prev, small(CC_HALO), small(1), small(1), small(1)],
        out_specs=[row(W_C)], out_shape=[SDS((s, W_C), F32)], carry=carry)


def _cc_bwd_conv(dy, glu, cw, cb, lng, lnb, name, carry=()):
    s = glu.shape[0]
    ts = _time_tile(s)
    row, prev, small = _cc_specs(s, ts)

    def body(dy_ref, glu_ref, glup_ref, cw_ref, cb_ref, lng_ref, lnb_ref, dy1_ref, dcw_ref, dcb_ref, dlng_ref, dlnb_ref):
        i = pl.program_id(0)

        @pl.when(i == 0)
        def _():
            for ref in (dcw_ref, dcb_ref, dlng_ref, dlnb_ref):
                ref[...] = jnp.zeros_like(ref)

        y0, y1 = _cc_recompute(glu_ref, glup_ref, cw_ref, cb_ref, i == 0)
        xhat, rstd = _ln_stats(y1)
        z = xhat * lng_ref[...] + lnb_ref[...]
        dz = dy_ref[...] * _dsilu(z, _sigmoid(z))
        dlng_ref[...] += jnp.sum(dz * xhat, axis=0, keepdims=True)
        dlnb_ref[...] += jnp.sum(dz, axis=0, keepdims=True)
        dxh = dz * lng_ref[...]
        dy1 = rstd * (dxh - jnp.mean(dxh, axis=-1, keepdims=True) - xhat * jnp.mean(dxh * xhat, axis=-1, keepdims=True))
        dy1_ref[...] = dy1
        dcb_ref[...] += jnp.sum(dy1, axis=0, keepdims=True)
        for j in range(CC_K):
            dcw_ref[j:j + 1, :] += jnp.sum(dy1 * _shift_down(y0, CC_K - 1 - j)[CC_HALO:, :], axis=0, keepdims=True)

    return _pcall(
        body, (dy, glu, glu, cw, cb, lng, lnb), name=name, grid=(s // ts,),
        in_specs=[row(W_C), row(2 * W_C), prev, small(CC_HALO), small(1), small(1), small(1)],
        out_specs=[row(W_C), small(CC_HALO), small(1), small(1), small(1)],
        out_shape=[SDS((s, W_C), F32), SDS((CC_HALO, W_C), F32)] + [SDS((1, W_C), F32)] * 3, carry=carry)


def _cc_bwd_glu(dy1, glu, cw, name, carry=()):
    s = glu.shape[0]
    ts = _time_tile(s)
    n_t = s // ts
    n32 = ts // CC_HALO

    def body(dy1_ref, dyn_ref, glu_ref, cw_ref, dglu_ref):
        i = pl.program_id(0)
        dcat = jnp.concatenate([dy1_ref[...], jnp.where(i < n_t - 1, dyn_ref[...], 0.0)], axis=0)
        dy0 = jnp.zeros((ts, W_C), F32)
        for j in range(CC_K):
            dy0 = dy0 + cw_ref[j:j + 1, :] * _shift_up(dcat, CC_K - 1 - j)[:ts, :]
        a = glu_ref[:, :W_C]
        sg = _sigmoid(glu_ref[:, W_C:])
        dglu_ref[...] = jnp.concatenate([dy0 * sg, dy0 * a * sg * (1.0 - sg)], axis=1)

    row = lambda w: pl.BlockSpec((ts, w), lambda i: (i, 0))
    nxt = pl.BlockSpec((CC_HALO, W_C), lambda i: (jnp.minimum((i + 1) * n32, s // CC_HALO - 1), 0))
    return _pcall(
        body, (dy1, dy1, glu, cw), name=name, grid=(n_t,),
        in_specs=[row(W_C), nxt, row(2 * W_C), pl.BlockSpec((CC_HALO, W_C), lambda i: (0, 0))],
        out_specs=[row(2 * W_C)], out_shape=[SDS((s, 2 * W_C), F32)], carry=carry)


_MIX_OFFS = ((0, W_A), (W_A, W_A + W_B), (W_A + W_B, W_A + W_B + W_C))


def _mix_out_fwd(x, ya, yb, yc, group_g, w_out, post_g, name, carry=()):
    s, d = x.shape
    ts = _time_tile(s)
    dm = w_out.shape[0]

    def body(x_ref, ya_ref, yb_ref, yc_ref, gg_ref, w_ref, qg_ref, xo_ref, o_ref):
        parts = []
        for y_ref, (lo, hi) in zip((ya_ref, yb_ref, yc_ref), _MIX_OFFS):
            yv = y_ref[...]
            parts.append(yv * _rms_r(yv) * gg_ref[:, lo:hi])
        o = _mm(jnp.concatenate(parts, axis=1), w_ref[...])
        o_ref[...] = o
        xo_ref[...] = x_ref[...] + o * _rms_r(o) * qg_ref[...]

    row = lambda w: pl.BlockSpec((ts, w), lambda i: (i, 0))
    return _pcall(
        body, (x, ya, yb, yc, group_g, w_out, post_g), name=name, grid=(s // ts,),
        in_specs=[row(d), row(W_A), row(W_B), row(W_C), pl.BlockSpec((1, dm), lambda i: (0, 0)),
                  pl.BlockSpec((dm, d), lambda i: (0, 0)), pl.BlockSpec((1, d), lambda i: (0, 0))],
        out_specs=[row(d), row(d)], out_shape=[SDS((s, d), F32), SDS((s, d), F32)], carry=carry)


def _mix_out_bwd(dxo, o, ya, yb, yc, group_g, w_out, post_g, name, carry=()):
    s, d = o.shape
    ts = _time_tile(s)
    n_t = s // ts
    dm = w_out.shape[0]

    def body(dxo_ref, o_ref, ya_ref, yb_ref, yc_ref, gg_ref, w_ref, qg_ref,
             dya_ref, dyb_ref, dyc_ref, dw_ref, dqg_ref, dgg_ref, acc):
        i = pl.program_id(0)

        @pl.when(i == 0)
        def _():
            acc[...] = jnp.zeros_like(acc)
            dqg_ref[...] = jnp.zeros_like(dqg_ref)
            dgg_ref[...] = jnp.zeros_like(dgg_ref)

        ov = o_ref[...]
        do, dq = _rms_bwd(ov, _rms_r(ov), qg_ref[...], dxo_ref[...])
        dqg_ref[...] += dq
        do = do.astype(BF16)
        dyn = _mm_nt(do, w_ref[...])
        parts, dggs = [], []
        for y_ref, dy_ref, (lo, hi) in zip((ya_ref, yb_ref, yc_ref), (dya_ref, dyb_ref, dyc_ref), _MIX_OFFS):
            yv = y_ref[...]
            r = _rms_r(yv)
            gg = gg_ref[:, lo:hi]
            parts.append(yv * r * gg)
            dyv, dg = _rms_bwd(yv, r, gg, dyn[:, lo:hi])
            dy_ref[...] = dyv
            dggs.append(dg)
        dgg_ref[...] += jnp.concatenate(dggs, axis=1)
        acc[...] += _mm_tn(jnp.concatenate(parts, axis=1), do)

        @pl.when(i == n_t - 1)
        def _():
            dw_ref[...] = acc[...].astype(BF16)

    row = lambda w: pl.BlockSpec((ts, w), lambda i: (i, 0))
    full = pl.BlockSpec((dm, d), lambda i: (0, 0))
    return _pcall(
        body, (dxo, o, ya, yb, yc, group_g, w_out, post_g), name=name, grid=(n_t,),
        in_specs=[row(d), row(d), row(W_A), row(W_B), row(W_C), pl.BlockSpec((1, dm), lambda i: (0, 0)), full,
                  pl.BlockSpec((1, d), lambda i: (0, 0))],
        out_specs=[row(W_A), row(W_B), row(W_C), full, pl.BlockSpec((1, d), lambda i: (0, 0)),
                   pl.BlockSpec((1, dm), lambda i: (0, 0))],
        out_shape=[SDS((s, W_A), F32), SDS((s, W_B), F32), SDS((s, W_C), F32), SDS((dm, d), BF16),
                   SDS((1, d), F32), SDS((1, dm), F32)],
        scratch_shapes=[pltpu.VMEM((dm, d), F32)], carry=carry)


def _loss_head(y, target, name):
    s, d = y.shape
    ts = _time_tile(s)

    def body(y_ref, t_ref, loss_ref, dy_ref):
        @pl.when(pl.program_id(0) == 0)
        def _():
            loss_ref[...] = jnp.zeros_like(loss_ref)

        err = y_ref[...] - t_ref[...]
        dy_ref[...] = err * (1.0 / d)
        per_tok = jnp.mean(err * err, axis=-1, keepdims=True)
        loss_ref[...] += 0.5 * jnp.sum(per_tok, axis=0, keepdims=True)

    row = pl.BlockSpec((ts, d), lambda i: (i, 0))
    return _pcall(body, (y, target), name=name, grid=(s // ts,), in_specs=[row, row],
                  out_specs=[pl.BlockSpec((1, BLK), lambda i: (0, 0)), row],
                  out_shape=[SDS((1, BLK), F32), SDS((s, d), F32)])[0]


def _adamw_math(w, g, m, v):
    m = ADAM_B1 * m + (1.0 - ADAM_B1) * g
    v = ADAM_B2 * v + (1.0 - ADAM_B2) * (g * g)
    m_hat = m / (1.0 - ADAM_B1 ** ADAM_STEP)
    v_hat = v / (1.0 - ADAM_B2 ** ADAM_STEP)
    delta = -ADAM_LR * (m_hat / (jnp.sqrt(v_hat) + ADAM_EPS) + ADAM_WD * w)
    return delta, m, v


def _row_tile(rows, cap=256):
    best = None
    for t in range(16, min(rows, cap) + 1, 16):
        if rows % t == 0:
            best = t
    return best if best is not None else rows


def _reduce_adamw(recv, w, m, v, name):
    n_l, r, c = w.shape
    tr = _row_tile(r)

    def body(recv_ref, w_ref, m_ref, v_ref, g_ref, d_ref, nm_ref, nv_ref):
        g = recv_ref[0].astype(F32)
        for p in range(1, N_DEV):
            g = g + recv_ref[p].astype(F32)
        g_ref[...] = g
        d_ref[...], nm_ref[...], nv_ref[...] = _adamw_math(w_ref[...], g, m_ref[...], v_ref[...])

    blk = pl.BlockSpec((None, tr, c), lambda l, i: (l, i, 0))
    return _pcall(
        body, (recv, w, m, v), name=name, grid=(n_l, r // tr),
        in_specs=[pl.BlockSpec((N_DEV, None, tr, c), lambda l, i: (0, l, i, 0)), blk, blk, blk],
        out_specs=[blk] * 4, out_shape=[SDS(w.shape, F32)] * 4)[0]


def _reduce_adamw_small(parts, w, m, v, name):
    def body(p_ref, w_ref, m_ref, v_ref, g_ref, d_ref, nm_ref, nv_ref):
        g = p_ref[0]
        for p in range(1, N_DEV):
            g = g + p_ref[p]
        g_ref[...] = g
        d_ref[...], nm_ref[...], nv_ref[...] = _adamw_math(w_ref[...], g, m_ref[...], v_ref[...])

    vm = pl.BlockSpec(memory_space=pltpu.VMEM)
    return pl.pallas_call(body, name=name, in_specs=[vm] * 4, out_specs=[vm] * 4, out_shape=[SDS(w.shape, F32)] * 4,
                          compiler_params=pltpu.CompilerParams(vmem_limit_bytes=VMEM_LIMIT))(parts, w, m, v)


def _pack(arrs):
    flat = jnp.concatenate([a.reshape(-1).astype(F32) for a in arrs])
    pad = (-flat.shape[0]) % (8 * BLK)
    return jnp.pad(flat, (0, pad)).reshape(-1, BLK)


def _unpack(packed, shapes):
    flat = packed.reshape(-1)
    out, off = [], 0
    for shp in shapes:
        n = math.prod(shp)
        out.append(flat[off:off + n].reshape(shp))
        off += n
    return out


def _block_diag(w):
    nb, bw, _ = w.shape
    eye = jnp.eye(nb, dtype=w.dtype)
    return (eye[:, None, :, None] * w[:, :, None, :]).reshape(nb * bw, nb * bw)


def _diag_blocks(wd, nb):
    bw = wd.shape[0] // nb
    return jnp.stack([wd[b * bw:(b + 1) * bw, b * bw:(b + 1) * bw] for b in range(nb)])


WEIGHT_NAMES = ['ffn1_pre_g', 'ffn1_w_gu', 'ffn1_w_down', 'ffn1_post_g', 'mix_pre_g', 'w_in', 'lru_conv_w', 'lru_conv_b',
                'lru_w_a', 'lru_b_a', 'lru_w_x', 'lru_b_x', 'lru_lambda', 'attn_sinks', 'conv_w', 'conv_b', 'conv_ln_g',
                'conv_ln_b', 'group_g', 'w_out', 'mix_post_g', 'ffn2_pre_g', 'ffn2_w_gu', 'ffn2_w_down', 'ffn2_post_g']
BIG = ('ffn1_w_gu', 'ffn1_w_down', 'w_in', 'w_out', 'ffn2_w_gu', 'ffn2_w_down')
TRANSPOSED = ('ffn1_w_gu', 'ffn2_w_gu', 'w_in')
SMALL = tuple(k for k in WEIGHT_NAMES if k not in BIG)
CHANNEL_SHARDED = ('lru_conv_w', 'conv_w')


def _step(x, target, w, m, v):
    n_l = w['ffn1_pre_g'].shape[0]
    assert n_l == 2, "the exchange schedule below is laid out for two layers"
    s, d = x.shape[1], x.shape[2]
    x = x.reshape(s, d)
    target = target.reshape(s, d)
    me = _my_pos()[3]
    tview = lambda t, k: jnp.swapaxes(t[k], 1, 2) if k in TRANSPOSED else t[k]
    wb = {k: tview(w, k).astype(BF16) for k in BIG}
    vec = lambda name, l: w[name][l][None, :]

    conv_shard = _pack([w['lru_conv_w'], w['conv_w']])
    g0 = _all_gather([(wb['ffn1_w_gu'], 0), (wb['ffn1_w_down'], 0), (wb['w_in'], 0), (wb['w_out'], 0),
                      (conv_shard, None)], "all_gather_first")
    wts = [dict(), dict()]
    wts[0]['ffn1_w_gu'], wts[0]['ffn1_w_down'], wts[0]['w_in'], wts[0]['w_out'], conv_g = g0
    ch = W_A // N_DEV
    conv_parts = [_unpack(conv_g[p], [(n_l, LRU_K, ch), (n_l, CC_K, ch)]) for p in range(N_DEV)]
    lru_cw = jnp.concatenate([cp[0] for cp in conv_parts], axis=-1)
    cc_cw = jnp.concatenate([cp[1] for cp in conv_parts], axis=-1)
    cc_cw = jnp.pad(cc_cw, ((0, 0), (0, CC_HALO - CC_K), (0, 0)))

    def stage_a(names, l):
        return _gather_a([(wb[k], l) for k in names], two_level=True)

    def finish(names, l, bufs):
        for k, b in zip(names, bufs):
            wts[l][k] = b

    saved = []
    h = x
    pend = {}
    for l in range(n_l):
        sv = {'x0': h}
        lw = wts[l]
        if l == 0:
            carry = [stage_a(('ffn2_w_gu', 'ffn2_w_down'), 0)]
        else:
            carry = [_gather_b(pend.pop('f2wo_1'))]
        (x1, sv['h1'], sv['g1'], sv['u1'], sv['d1']), ex = _ffn_fwd(
            h, vec('ffn1_pre_g', l), vec('ffn1_post_g', l), lw['ffn1_w_gu'], lw['ffn1_w_down'], f"ffn1_fwd_l{l}", carry)
        if l == 0:
            pend['f2_0'] = ex[0]
        else:
            finish(('ffn2_w_gu', 'ffn2_w_down', 'w_out'), 1, ex[0])
        sv['x1'] = x1
        carry = [_gather_b(pend.pop('f2_0'))] if l == 0 else []
        (sv['hn'], lx, lg, q, k, vv, glu), ex = _mix_in_fwd(
            x1, vec('mix_pre_g', l), lw['w_in'].reshape(D_IN_PROJ, d), f"mix_in_fwd_l{l}", carry)
        if l == 0:
            finish(('ffn2_w_gu', 'ffn2_w_down'), 0, ex[0])
        sv.update(lx=lx, lg=lg, q=q, k=k, v=vv, glu=glu)
        lru_p = (lru_cw[l], vec('lru_conv_b', l), _block_diag(w['lru_w_a'][l]).astype(BF16), vec('lru_b_a', l),
                 _block_diag(w['lru_w_x'][l]).astype(BF16), vec('lru_b_x', l), vec('lru_lambda', l))
        cc_p = (cc_cw[l], vec('conv_b', l), vec('conv_ln_g', l), vec('conv_ln_b', l))
        sv.update(lru_p=lru_p, cc_p=cc_p)
        (sv['ya'], sv['hs']), _ = _lru_fwd(lx, lg, lru_p, f"lru_fwd_l{l}")
        carry = [stage_a(('ffn1_w_gu', 'ffn1_w_down'), 1)] if l == 0 else []
        (sv['yb'],), ex = _attn_fwd(q, k, vv, w['attn_sinks'][l], f"attn_fwd_l{l}", carry)
        if l == 0:
            pend['f1_1'] = ex[0]
        carry = [_gather_b(pend.pop('f1_1'))] if l == 0 else []
        (sv['yc'],), ex = _cc_fwd(glu, *cc_p, f"cconv_fwd_l{l}", carry)
        if l == 0:
            finish(('ffn1_w_gu', 'ffn1_w_down'), 1, ex[0])
        carry = [stage_a(('w_in',), 1)] if l == 0 else []
        (x2, sv['o']), ex = _mix_out_fwd(x1, sv['ya'], sv['yb'], sv['yc'], vec('group_g', l),
                                         lw['w_out'].reshape(-1, d), vec('mix_post_g', l), f"mix_out_fwd_l{l}", carry)
        if l == 0:
            pend['wi_1'] = ex[0]
        sv['x2'] = x2
        carry = [_gather_b(pend.pop('wi_1')), stage_a(('ffn2_w_gu', 'ffn2_w_down', 'w_out'), 1)] if l == 0 else []
        (h, sv['h2'], sv['g2'], sv['u2'], sv['d2']), ex = _ffn_fwd(
            x2, vec('ffn2_pre_g', l), vec('ffn2_post_g', l), lw['ffn2_w_gu'], lw['ffn2_w_down'], f"ffn2_fwd_l{l}", carry)
        if l == 0:
            finish(('w_in',), 1, ex[0])
            pend['f2wo_1'] = ex[1]
        saved.append(sv)

    loss_row, dh = _loss_head(h, target, "loss_head")
    loss = lax.psum(loss_row[0, 0], AXES)

    recv = {k: None for k in BIG}
    ready = {}
    small = [dict() for _ in range(n_l)]

    def exchange(keys):
        return _grad_x([(ready.pop(key), key[1], recv[key[0]]) for key in keys], n_l)

    def received(keys, bufs):
        for key, b in zip(keys, bufs):
            recv[key[0]] = b

    def run(fn, *args, keys=()):
        outs, ex = fn(*args, carry=[exchange(keys)] if keys else [])
        if keys:
            received(keys, ex[0])
        return outs

    for l in reversed(range(n_l)):
        sv, sg, lw = saved[l], small[l], wts[l]
        keys = [] if l == n_l - 1 else [('ffn1_w_gu', l + 1)]
        dx2, dd, dg, du, sg['ffn2_pre_g'], sg['ffn2_post_g'] = run(
            _ffn_bwd_act, dh, sv['d2'], sv['x2'], vec('ffn2_pre_g', l), vec('ffn2_post_g', l), sv['g2'], sv['u2'],
            lw['ffn2_w_gu'], lw['ffn2_w_down'], f"ffn2_bwd_act_l{l}", keys=keys)
        keys = [] if l == n_l - 1 else [('ffn1_w_down', l + 1)]
        dwg, dwu, dwd = run(_ffn_bwd_w, sv['h2'], dd, sv['g2'], sv['u2'], dg, du, f"ffn2_bwd_w_l{l}", keys=keys)
        ready[('ffn2_w_gu', l)] = [dwg, dwu]
        ready[('ffn2_w_down', l)] = [dwd.reshape(N_DEV, -1, d)]
        dya, dyb, dyc, dw_out, sg['mix_post_g'], sg['group_g'] = run(
            _mix_out_bwd, dx2, sv['o'], sv['ya'], sv['yb'], sv['yc'], vec('group_g', l), lw['w_out'].reshape(-1, d),
            vec('mix_post_g', l), f"mix_out_bwd_l{l}")
        ready[('w_out', l)] = [dw_out.reshape(N_DEV, -1, d)]
        (dlx, dlg, sg['lru_conv_w'], sg['lru_conv_b'], dwa, sg['lru_b_a'], dwx, sg['lru_b_x'],
         sg['lru_lambda']) = run(_lru_bwd, dya, sv['lx'], sv['lg'], sv['hs'], sv['lru_p'], f"lru_bwd_l{l}")
        sg['lru_w_a'] = _diag_blocks(dwa, A_BLOCKS)
        sg['lru_w_x'] = _diag_blocks(dwx, A_BLOCKS)
        dq, dk, dv, dk_up, dv_up, dsk = run(_attn_bwd, dyb, sv['q'], sv['k'], sv['v'], w['attn_sinks'][l],
                                            f"attn_bwd_l{l}", keys=[('ffn2_w_gu', l)])
        sg['attn_sinks'] = dsk[:, 0]
        dy1, dcw, sg['conv_b'], sg['conv_ln_g'], sg['conv_ln_b'] = run(
            _cc_bwd_conv, dyc, sv['glu'], *sv['cc_p'], f"cconv_bwd_conv_l{l}")
        sg['conv_w'] = dcw[:CC_K]
        (dglu,) = run(_cc_bwd_glu, dy1, sv['glu'], sv['cc_p'][0], f"cconv_bwd_glu_l{l}")
        dx1, dw_in, sg['mix_pre_g'] = run(
            _mix_in_bwd, dx2, sv['x1'], vec('mix_pre_g', l), sv['hn'], lw['w_in'].reshape(D_IN_PROJ, d),
            dlx, dlg, dq, dk, dk_up, dv, dv_up, dglu, f"mix_in_bwd_l{l}", keys=[('w_out', l)])
        ready[('w_in', l)] = [dw_in.reshape(N_DEV, -1, d)]
        dh, dd, dg, du, sg['ffn1_pre_g'], sg['ffn1_post_g'] = run(
            _ffn_bwd_act, dx1, sv['d1'], sv['x0'], vec('ffn1_pre_g', l), vec('ffn1_post_g', l), sv['g1'], sv['u1'],
            lw['ffn1_w_gu'], lw['ffn1_w_down'], f"ffn1_bwd_act_l{l}", keys=[('ffn2_w_down', l), ('w_in', l)])
        small_ex = []
        if l == 0:
            part = _pack([jnp.stack([small[j][k] for j in range(n_l)]) for k in SMALL])
            small_ex = [_gather_a([(part, None)], two_level=False)]
        (dwg, dwu, dwd), ex = _ffn_bwd_w(sv['h1'], dd, sv['g1'], sv['u1'], dg, du, f"ffn1_bwd_w_l{l}", small_ex)
        ready[('ffn1_w_gu', l)] = [dwg, dwu]
        ready[('ffn1_w_down', l)] = [dwd.reshape(N_DEV, -1, d)]
    small_parts = ex[0][0]
    grad_x = dh.reshape(1, s, d)

    keys = [('ffn1_w_gu', 0), ('ffn1_w_down', 0)]
    last = exchange(keys)
    def no_compute(o_ref):
        o_ref[...] = jnp.zeros_like(o_ref)

    _, ex = _pcall(no_compute, (), name="grad_exchange_last", grid=(1,), in_specs=[],
                   out_specs=[pl.BlockSpec((8, BLK), lambda i: (0, 0))], out_shape=[SDS((8, BLK), F32)], carry=[last])
    received(keys, ex[0])

    out = {}
    for k in BIG:
        res = _reduce_adamw(recv[k], tview(w, k), tview(m, k), tview(v, k), f"reduce_adamw_{k}")
        out[k] = [jnp.swapaxes(r, 1, 2) for r in res] if k in TRANSPOSED else res

    small_shapes = [(n_l,) + tuple(small[0][k].shape) for k in SMALL]

    def widen(t, k):
        if k not in CHANNEL_SHARDED:
            return t.reshape((n_l,) + tuple(small[0][k].shape))
        full = jnp.zeros((n_l,) + tuple(small[0][k].shape), F32)
        return lax.dynamic_update_slice_in_dim(full, t, me * ch, axis=2)

    packed = [_pack([widen(src[k], k) for k in SMALL]) for src in (w, m, v)]
    res = _reduce_adamw_small(small_parts, *packed, "reduce_adamw_small")
    for k, g, dlt, nm, nv in zip(SMALL, *[_unpack(r, small_shapes) for r in res]):
        vals = [g, dlt, nm, nv]
        if k in CHANNEL_SHARDED:
            vals = [lax.dynamic_slice_in_dim(t, me * ch, ch, axis=2) for t in vals]
        out[k] = [t.reshape(w[k].shape) for t in vals]

    return (loss, grad_x, *[out[k][0] for k in WEIGHT_NAMES], *[out[k][1] for k in WEIGHT_NAMES],
            *[out[k][2] for k in WEIGHT_NAMES], *[out[k][3] for k in WEIGHT_NAMES])


def kernel(x, ffn1_pre_g, ffn1_w_gu, ffn1_w_down, ffn1_post_g, mix_pre_g, w_in, lru_conv_w, lru_conv_b, lru_w_a, lru_b_a, lru_w_x, lru_b_x, lru_lambda, attn_sinks, conv_w, conv_b, conv_ln_g, conv_ln_b, group_g, w_out, mix_post_g, ffn2_pre_g, ffn2_w_gu, ffn2_w_down, ffn2_post_g, loss_target, m_ffn1_pre_g, m_ffn1_w_gu, m_ffn1_w_down, m_ffn1_post_g, m_mix_pre_g, m_w_in, m_lru_conv_w, m_lru_conv_b, m_lru_w_a, m_lru_b_a, m_lru_w_x, m_lru_b_x, m_lru_lambda, m_attn_sinks, m_conv_w, m_conv_b, m_conv_ln_g, m_conv_ln_b, m_group_g, m_w_out, m_mix_post_g, m_ffn2_pre_g, m_ffn2_w_gu, m_ffn2_w_down, m_ffn2_post_g, v_ffn1_pre_g, v_ffn1_w_gu, v_ffn1_w_down, v_ffn1_post_g, v_mix_pre_g, v_w_in, v_lru_conv_w, v_lru_conv_b, v_lru_w_a, v_lru_b_a, v_lru_w_x, v_lru_b_x, v_lru_lambda, v_attn_sinks, v_conv_w, v_conv_b, v_conv_ln_g, v_conv_ln_b, v_group_g, v_w_out, v_mix_post_g, v_ffn2_pre_g, v_ffn2_w_gu, v_ffn2_w_down, v_ffn2_post_g):
    args = (ffn1_pre_g, ffn1_w_gu, ffn1_w_down, ffn1_post_g, mix_pre_g, w_in, lru_conv_w, lru_conv_b, lru_w_a, lru_b_a, lru_w_x, lru_b_x, lru_lambda, attn_sinks, conv_w, conv_b, conv_ln_g, conv_ln_b, group_g, w_out, mix_post_g, ffn2_pre_g, ffn2_w_gu, ffn2_w_down, ffn2_post_g)
    ms = (m_ffn1_pre_g, m_ffn1_w_gu, m_ffn1_w_down, m_ffn1_post_g, m_mix_pre_g, m_w_in, m_lru_conv_w, m_lru_conv_b, m_lru_w_a, m_lru_b_a, m_lru_w_x, m_lru_b_x, m_lru_lambda, m_attn_sinks, m_conv_w, m_conv_b, m_conv_ln_g, m_conv_ln_b, m_group_g, m_w_out, m_mix_post_g, m_ffn2_pre_g, m_ffn2_w_gu, m_ffn2_w_down, m_ffn2_post_g)
    vs = (v_ffn1_pre_g, v_ffn1_w_gu, v_ffn1_w_down, v_ffn1_post_g, v_mix_pre_g, v_w_in, v_lru_conv_w, v_lru_conv_b, v_lru_w_a, v_lru_b_a, v_lru_w_x, v_lru_b_x, v_lru_lambda, v_attn_sinks, v_conv_w, v_conv_b, v_conv_ln_g, v_conv_ln_b, v_group_g, v_w_out, v_mix_post_g, v_ffn2_pre_g, v_ffn2_w_gu, v_ffn2_w_down, v_ffn2_post_g)
    return _step(x, loss_target, dict(zip(WEIGHT_NAMES, args)), dict(zip(WEIGHT_NAMES, ms)), dict(zip(WEIGHT_NAMES, vs)))
```

```python
import functools
import math
import operator

import jax
import jax.numpy as jnp
from jax import lax
from jax.experimental import pallas as pl
from jax.experimental.pallas import tpu as pltpu

F32 = jnp.float32
BF16 = jnp.bfloat16
N_DEV = 8
AXES = ("x", "y", "c")
MESH = pl.DeviceIdType.MESH

NORM_EPS = 1e-6
LN_EPS = 1e-5
NEG_BIG = -1e30
W_A = 256
W_B = 512
W_C = 256
HEAD_DIM = 64
N_Q_HEADS = 8
N_KV_HEADS = 2
Q_PER_KV = N_Q_HEADS // N_KV_HEADS
KV_W = N_KV_HEADS * HEAD_DIM
BLK = 128
LRU_K = 4
LRU_C = 8.0
A_BLOCKS = 4
CC_K = 31
CC_HALO = 32
LRU_HALO = 8
D_IN_PROJ = 2 * W_A + W_B + 2 * KV_W + 2 * W_C
ADAM_LR = 0.001
ADAM_B1 = 0.9
ADAM_B2 = 0.999
ADAM_EPS = 1e-08
ADAM_WD = 0.01
ADAM_STEP = 10
VMEM_LIMIT = 56 * 1024 * 1024

SDS = jax.ShapeDtypeStruct
ANY = pl.BlockSpec(memory_space=pl.ANY)


def _time_tile(s):
    return max(BLK, s // 8)


class _Exchange:
    def __init__(self, inputs, out_shapes, aliases, sem_shapes, start, wait):
        self.inputs, self.out_shapes, self.aliases, self.sem_shapes = inputs, out_shapes, aliases, sem_shapes
        self.start, self.wait = start, wait


def _my_pos():
    x, y, c = (lax.axis_index(a) for a in AXES)
    return x, y, c, 4 * x + 2 * y + c


def _flip(k):
    x, y, c, _ = _my_pos()
    return (1 - x if k & 4 else x, 1 - y if k & 2 else y, 1 - c if k & 1 else c)


def _slot(dev):
    return 4 * dev[0] + 2 * dev[1] + dev[2]


def _dev(p):
    return (p >> 2, (p >> 1) & 1, p & 1)


def _gather_a(items, two_level):
    rels = (1, 2, 4, 6) if two_level else tuple(range(1, N_DEV))
    n = len(items)
    src_of = lambda ins, a: ins[a] if items[a][1] is None else ins[a].at[items[a][1]]

    def shape_of(a):
        arr, l = items[a]
        return arr.shape if l is None else arr.shape[1:]

    def copies(ins, outs, sems, a):
        send, recv, _ = sems
        me = _my_pos()[3]
        return [(k, pltpu.make_async_remote_copy(
            src_ref=src_of(ins, a), dst_ref=outs[a].at[me], send_sem=send.at[a, k], recv_sem=recv.at[a, k],
            device_id=_flip(k), device_id_type=MESH)) for k in rels]

    def local(ins, outs, sems, a):
        return pltpu.make_async_copy(src_of(ins, a), outs[a].at[_my_pos()[3]], sems[2].at[a])

    def start(ins, outs, sems):
        for a in range(n):
            local(ins, outs, sems, a).start()
            for _, cp in copies(ins, outs, sems, a):
                cp.start()

    def wait(ins, outs, sems):
        send, recv, _ = sems
        for a in range(n):
            for k, cp in copies(ins, outs, sems, a):
                pltpu.make_async_remote_copy(
                    src_ref=src_of(ins, a), dst_ref=outs[a].at[_slot(_flip(k))], send_sem=send.at[a, k],
                    recv_sem=recv.at[a, k], device_id=_flip(k), device_id_type=MESH).wait_recv()
                cp.wait_send()
            local(ins, outs, sems, a).wait()

    return _Exchange([it[0] for it in items], [SDS((N_DEV,) + shape_of(a), items[a][0].dtype) for a in range(n)], {},
                     [pltpu.SemaphoreType.DMA((n, N_DEV)), pltpu.SemaphoreType.DMA((n, N_DEV)),
                      pltpu.SemaphoreType.DMA((n,))], start, wait)


def _gather_b(bufs):
    n = len(bufs)

    def copies(ins, outs, sems, a, c_of_block):
        send, recv = sems
        x, y, c, _ = _my_pos()
        res = []
        for k in (2, 4, 6):
            chip = _flip(k)
            blk = _slot((chip[0], chip[1], c if c_of_block == "mine" else 1 - c))
            res.append(pltpu.make_async_remote_copy(
                src_ref=ins[a].at[blk], dst_ref=outs[a].at[blk], send_sem=send.at[a, k], recv_sem=recv.at[a, k],
                device_id=_flip(1), device_id_type=MESH))
        return res

    def start(ins, outs, sems):
        for a in range(n):
            for cp in copies(ins, outs, sems, a, "mine"):
                cp.start()

    def wait(ins, outs, sems):
        for a in range(n):
            for cp in copies(ins, outs, sems, a, "sibling"):
                cp.wait_recv()
            for cp in copies(ins, outs, sems, a, "mine"):
                cp.wait_send()

    return _Exchange(list(bufs), [SDS(b.shape, b.dtype) for b in bufs], {a: a for a in range(n)},
                     [pltpu.SemaphoreType.DMA((n, N_DEV)), pltpu.SemaphoreType.DMA((n, N_DEV))], start, wait)


def _grad_x(items, n_l):
    n = len(items)
    inputs, first_in, recv_in, aliases, out_shapes = [], [], [], {}, []
    for a, (arrs, l, recv) in enumerate(items):
        first_in.append(len(inputs))
        inputs += list(arrs)
        assert sum(arr.shape[0] for arr in arrs) == N_DEV
        if recv is not None:
            aliases[len(inputs)] = a
            inputs.append(recv)
        out_shapes.append(SDS((N_DEV, n_l) + arrs[0].shape[1:], arrs[0].dtype))

    def slab(ins, a, p):
        off = 0
        for j, arr in enumerate(items[a][0]):
            if p < off + arr.shape[0]:
                return ins[first_in[a] + j].at[p - off]
            off += arr.shape[0]
        raise AssertionError

    def rdma(ins, outs, sems, a, p, src_dev):
        send, recv, _ = sems
        return pltpu.make_async_remote_copy(
            src_ref=slab(ins, a, p), dst_ref=outs[a].at[src_dev, items[a][1]], send_sem=send.at[a, p],
            recv_sem=recv.at[a, src_dev], device_id=_dev(p), device_id_type=MESH)

    def local(ins, outs, sems, a, p):
        return pltpu.make_async_copy(slab(ins, a, p), outs[a].at[p, items[a][1]], sems[2].at[a])

    def start(ins, outs, sems):
        me = _my_pos()[3]
        for p in range(N_DEV):
            @pl.when(me != p)
            def _():
                for a in range(n):
                    rdma(ins, outs, sems, a, p, me).start()

            @pl.when(me == p)
            def _():
                for a in range(n):
                    local(ins, outs, sems, a, p).start()

    def wait(ins, outs, sems):
        me = _my_pos()[3]
        for p in range(N_DEV):
            @pl.when(me != p)
            def _():
                for a in range(n):
                    rdma(ins, outs, sems, a, p, p).wait_recv()
                    rdma(ins, outs, sems, a, p, p).wait_send()

            @pl.when(me == p)
            def _():
                for a in range(n):
                    local(ins, outs, sems, a, p).wait()

    return _Exchange(inputs, out_shapes, aliases,
                     [pltpu.SemaphoreType.DMA((n, N_DEV)), pltpu.SemaphoreType.DMA((n, N_DEV)),
                      pltpu.SemaphoreType.DMA((n,))], start, wait)


def _pcall(body, args, *, name, grid, in_specs, out_specs, out_shape, scratch_shapes=(), carry=(), body_aliases=None):
    n_in, n_out, n_scr = len(in_specs), len(out_specs), len(scratch_shapes)
    c_in = [len(e.inputs) for e in carry]
    c_out = [len(e.out_shapes) for e in carry]
    c_sem = [len(e.sem_shapes) for e in carry]
    aliases = dict(body_aliases or {})
    for j, e in enumerate(carry):
        for i_loc, o_loc in e.aliases.items():
            aliases[n_in + sum(c_in[:j]) + i_loc] = n_out + sum(c_out[:j]) + o_loc

    def wrapped(*refs):
        def take(counts, pos):
            groups = []
            for cnt in counts:
                groups.append(refs[pos:pos + cnt])
                pos += cnt
            return groups, pos

        (ins,), pos = take([n_in], 0)
        cins, pos = take(c_in, pos)
        (outs,), pos = take([n_out], pos)
        couts, pos = take(c_out, pos)
        (scr,), pos = take([n_scr], pos)
        csems, pos = take(c_sem, pos)
        if carry:
            ids = [pl.program_id(k) for k in range(len(grid))]
            first = functools.reduce(operator.and_, [i == 0 for i in ids])
            last = functools.reduce(operator.and_, [i == g - 1 for i, g in zip(ids, grid)])

            @pl.when(first)
            def _():
                for e, ci, co, cs in zip(carry, cins, couts, csems):
                    e.start(ci, co, cs)

        body(*ins, *outs, *scr)
        if carry:
            @pl.when(last)
            def _():
                for e, ci, co, cs in zip(carry, cins, couts, csems):
                    e.wait(ci, co, cs)

    res = pl.pallas_call(
        wrapped, name=name, grid=grid,
        in_specs=list(in_specs) + [ANY] * sum(c_in),
        out_specs=list(out_specs) + [ANY] * sum(c_out),
        out_shape=list(out_shape) + [s for e in carry for s in e.out_shapes],
        scratch_shapes=list(scratch_shapes) + [s for e in carry for s in e.sem_shapes],
        input_output_aliases=aliases,
        compiler_params=pltpu.CompilerParams(dimension_semantics=("arbitrary",) * len(grid),
                                             vmem_limit_bytes=VMEM_LIMIT),
    )(*args, *[a for e in carry for a in e.inputs])
    outs, pos, extra = list(res[:n_out]), n_out, []
    for cnt in c_out:
        extra.append(list(res[pos:pos + cnt]))
        pos += cnt
    return outs, extra


def _all_gather(items, name):
    n = len(items)
    shape_of = lambda a: items[a][0].shape if items[a][1] is None else items[a][0].shape[1:]

    def body(*refs):
        ins, outs, (send_sems, recv_sems, local_sems) = refs[:n], refs[n:2 * n], refs[2 * n:]
        x, y, c, me = _my_pos()
        src_of = lambda a: ins[a] if items[a][1] is None else ins[a].at[items[a][1]]

        def copy(a, k, block, to, src=None):
            dst = outs[a].at[_slot(block)]
            return pltpu.make_async_remote_copy(
                src_ref=dst if src is None else src, dst_ref=dst,
                send_sem=send_sems.at[a, k], recv_sem=recv_sems.at[a, k], device_id=to, device_id_type=MESH)

        mine = [pltpu.make_async_copy(src_of(a), outs[a].at[me], local_sems.at[a]) for a in range(n)]
        for cp in mine:
            cp.start()
        first = [copy(a, k, (x, y, c), _flip(k), src=src_of(a)) for a in range(n) for k in (1, 2, 4, 6)]
        for cp in first:
            cp.start()
        passed = []
        for k in (2, 4, 6):
            for a in range(n):
                copy(a, k, _flip(k), (x, y, c)).wait_recv()
                fwd = copy(a, k + 1, _flip(k), _flip(1))
                fwd.start()
                passed.append(fwd)
        for a in range(n):
            copy(a, 1, _flip(1), (x, y, c)).wait_recv()
            for k in (2, 4, 6):
                copy(a, k + 1, _flip(k + 1), (x, y, c)).wait_recv()
        for cp in first + passed:
            cp.wait_send()
        for cp in mine:
            cp.wait()

    return pl.pallas_call(
        body, name=name,
        in_specs=[ANY] * n, out_specs=[ANY] * n,
        out_shape=[SDS((N_DEV,) + shape_of(a), items[a][0].dtype) for a in range(n)],
        scratch_shapes=[pltpu.SemaphoreType.DMA((n, N_DEV)), pltpu.SemaphoreType.DMA((n, N_DEV)),
                        pltpu.SemaphoreType.DMA((n,))],
    )(*[it[0] for it in items])


def _mm(a, b):
    return jnp.dot(a.astype(BF16), b.astype(BF16), preferred_element_type=F32)


def _mm_nt(a, b):
    return lax.dot_general(a.astype(BF16), b.astype(BF16), (((1,), (1,)), ((), ())), preferred_element_type=F32)


def _mm_tn(a, b):
    return lax.dot_general(a.astype(BF16), b.astype(BF16), (((0,), (0,)), ((), ())), preferred_element_type=F32)


def _rms_r(x):
    return lax.rsqrt(jnp.mean(x * x, axis=-1, keepdims=True) + NORM_EPS)


def _rms_bwd(x, r, g, dy):
    gy = dy * g
    dx = r * (gy - x * (r * r) * jnp.mean(gy * x, axis=-1, keepdims=True))
    dg = jnp.sum(dy * x * r, axis=0, keepdims=True)
    return dx, dg


def _sigmoid(x):
    return 1.0 / (1.0 + jnp.exp(-x))


def _dsilu(z, sz):
    return sz * (1.0 + z * (1.0 - sz))


_GELU_C = math.sqrt(2.0 / math.pi)


def _gelu(x):
    t = jnp.tanh(_GELU_C * (x + 0.044715 * x * x * x))
    return 0.5 * x * (1.0 + t), t


def _dgelu(x, t):
    return 0.5 * (1.0 + t) + 0.5 * x * (1.0 - t * t) * _GELU_C * (1.0 + 3.0 * 0.044715 * x * x)


def _log1p(e):
    return jnp.where(e < 1e-2, e * (1.0 - e * (0.5 - e * (1.0 / 3.0))), jnp.log(1.0 + e))


def _softplus(x):
    return jnp.maximum(x, 0.0) + _log1p(jnp.exp(-jnp.abs(x)))


def _neg_expm1(x):
    small = -x * (1.0 + x * (0.5 + x * (1.0 / 6.0) * (1.0 + x * 0.25)))
    return jnp.where(x > -1e-2, small, 1.0 - jnp.exp(x))


def _shift_down(x, s):
    return x if s == 0 else pltpu.roll(x, s, 0)


def _shift_up(x, s):
    return x if s == 0 else pltpu.roll(x, x.shape[0] - s, 0)


def _ffn_wspecs(d, fc, order):
    f_of = (lambda i, f: f) if order == "tf" else (lambda f, i: f)
    n_f = N_DEV // 2
    return [pl.BlockSpec((None, fc, d), lambda *g: (f_of(*g), 0, 0)),
            pl.BlockSpec((None, fc, d), lambda *g: (f_of(*g) + n_f, 0, 0)),
            pl.BlockSpec((2, fc // 2, d), lambda *g: (f_of(*g), 0, 0))]


def _ffn_fwd(x, pre_g, post_g, wgu_t, wd, name, carry=()):
    s, d = x.shape
    fc = wgu_t.shape[1]
    ts = _time_tile(s)
    n_t, n_f = s // ts, N_DEV // 2

    def body(x_ref, pg_ref, qg_ref, wg_ref, wu_ref, wd_ref, xo_ref, h_ref, g_ref, u_ref, d_ref, h_scr, acc):
        f = pl.program_id(1)

        @pl.when(f == 0)
        def _():
            xv = x_ref[...]
            hv = (xv * _rms_r(xv) * pg_ref[...]).astype(BF16)
            h_scr[...] = hv
            h_ref[...] = hv
            acc[...] = jnp.zeros_like(acc)

        hv = h_scr[...]
        g = _mm_nt(hv, wg_ref[...])
        u = _mm_nt(hv, wu_ref[...])
        g_ref[...] = g.astype(BF16)
        u_ref[...] = u.astype(BF16)
        a = (g * _sigmoid(g) * u).astype(BF16)
        acc[...] += jnp.dot(a, wd_ref[...].reshape(fc, d), preferred_element_type=F32)

        @pl.when(f == n_f - 1)
        def _():
            dv = acc[...]
            d_ref[...] = dv
            xo_ref[...] = x_ref[...] + 0.5 * (dv * _rms_r(dv) * qg_ref[...])

    row = pl.BlockSpec((ts, d), lambda i, f: (i, 0))
    vec = pl.BlockSpec((1, d), lambda i, f: (0, 0))
    act = pl.BlockSpec((None, ts, fc), lambda i, f: (f, i, 0))
    return _pcall(
        body, (x, pre_g, post_g, wgu_t, wgu_t, wd), name=name, grid=(n_t, n_f),
        in_specs=[row, vec, vec] + _ffn_wspecs(d, fc, "tf"),
        out_specs=[row, row, act, act, row],
        out_shape=[SDS((s, d), F32), SDS((s, d), BF16), SDS((n_f, s, fc), BF16), SDS((n_f, s, fc), BF16),
                   SDS((s, d), F32)],
        scratch_shapes=[pltpu.VMEM((ts, d), BF16), pltpu.VMEM((ts, d), F32)], carry=carry)


def _ffn_bwd_act(dxo, dmid, x, pre_g, post_g, g_s, u_s, wgu_t, wd, name, carry=()):
    s, d = x.shape
    fc = wgu_t.shape[1]
    ts = _time_tile(s)
    n_t, n_f = s // ts, N_DEV // 2

    def body(dxo_ref, dm_ref, x_ref, pg_ref, qg_ref, g_ref, u_ref, wg_ref, wu_ref, wd_ref,
             dx_ref, dd_ref, dg_ref, du_ref, dpg_ref, dqg_ref, dd_scr, dh_acc):
        i, f = pl.program_id(0), pl.program_id(1)

        @pl.when((i == 0) & (f == 0))
        def _():
            dpg_ref[...] = jnp.zeros_like(dpg_ref)
            dqg_ref[...] = jnp.zeros_like(dqg_ref)

        @pl.when(f == 0)
        def _():
            dv = dm_ref[...]
            ddv, dq = _rms_bwd(dv, _rms_r(dv), qg_ref[...], 0.5 * dxo_ref[...])
            dqg_ref[...] += dq
            dd_scr[...] = ddv.astype(BF16)
            dd_ref[...] = ddv.astype(BF16)
            dh_acc[...] = jnp.zeros_like(dh_acc)

        da = _mm_nt(dd_scr[...], wd_ref[...].reshape(fc, d))
        g = g_ref[...].astype(F32)
        u = u_ref[...].astype(F32)
        sg = _sigmoid(g)
        du = (da * (g * sg)).astype(BF16)
        dg = (da * u * _dsilu(g, sg)).astype(BF16)
        dg_ref[...] = dg
        du_ref[...] = du
        dh_acc[...] += _mm(dg, wg_ref[...]) + _mm(du, wu_ref[...])

        @pl.when(f == n_f - 1)
        def _():
            xv = x_ref[...]
            dxv, dp = _rms_bwd(xv, _rms_r(xv), pg_ref[...], dh_acc[...])
            dpg_ref[...] += dp
            dx_ref[...] = dxo_ref[...] + dxv

    row = pl.BlockSpec((ts, d), lambda i, f: (i, 0))
    vec = pl.BlockSpec((1, d), lambda i, f: (0, 0))
    act = pl.BlockSpec((None, ts, fc), lambda i, f: (f, i, 0))
    return _pcall(
        body, (dxo, dmid, x, pre_g, post_g, g_s, u_s, wgu_t, wgu_t, wd), name=name, grid=(n_t, n_f),
        in_specs=[row, row, row, vec, vec, act, act] + _ffn_wspecs(d, fc, "tf"),
        out_specs=[row, row, act, act, vec, vec],
        out_shape=[SDS((s, d), F32), SDS((s, d), BF16), SDS((n_f, s, fc), BF16), SDS((n_f, s, fc), BF16),
                   SDS((1, d), F32), SDS((1, d), F32)],
        scratch_shapes=[pltpu.VMEM((ts, d), BF16), pltpu.VMEM((ts, d), F32)], carry=carry)


def _ffn_bwd_w(h, dd, g_s, u_s, dg, du, name, carry=()):
    s, d = h.shape
    n_f, _, fc = g_s.shape
    ts = _time_tile(s)
    n_t = s // ts

    def body(h_ref, dd_ref, g_ref, u_ref, dg_ref, du_ref, wg_ref, wu_ref, wd_ref, acc_g, acc_u, acc_d):
        i = pl.program_id(1)

        @pl.when(i == 0)
        def _():
            acc_g[...] = jnp.zeros_like(acc_g)
            acc_u[...] = jnp.zeros_like(acc_u)
            acc_d[...] = jnp.zeros_like(acc_d)

        g = g_ref[...].astype(F32)
        a = (g * _sigmoid(g) * u_ref[...].astype(F32)).astype(BF16)
        hv = h_ref[...]
        acc_g[...] += _mm_tn(dg_ref[...], hv)
        acc_u[...] += _mm_tn(du_ref[...], hv)
        acc_d[...] += _mm_tn(a, dd_ref[...])

        @pl.when(i == n_t - 1)
        def _():
            wg_ref[...] = acc_g[...].astype(BF16)
            wu_ref[...] = acc_u[...].astype(BF16)
            wd_ref[...] = acc_d[...].astype(BF16)

    row = pl.BlockSpec((ts, d), lambda f, i: (i, 0))
    act = pl.BlockSpec((None, ts, fc), lambda f, i: (f, i, 0))
    out = pl.BlockSpec((None, fc, d), lambda f, i: (f, 0, 0))
    return _pcall(
        body, (h, dd, g_s, u_s, dg, du), name=name, grid=(n_f, n_t),
        in_specs=[row, row, act, act, act, act], out_specs=[out, out, out],
        out_shape=[SDS((n_f, fc, d), BF16)] * 3,
        scratch_shapes=[pltpu.VMEM((fc, d), F32)] * 3, carry=carry)


def _ffn_bwd_w_send(h, dd, g_s, u_s, dg, du, recv_gu, recv_d, layer, name, carry=()):
    s, d = h.shape
    n_f, _, fc = g_s.shape
    ts = _time_tile(s)
    n_t = s // ts
    half = fc // 2

    def body(h_ref, dd_ref, g_ref, u_ref, dg_ref, du_ref, _rgu_in, _rd_in, rgu_ref, rd_ref,
             acc_g, acc_u, acc_d, st_g, st_u, st_d, send_sems, recv_sems, local_sems):
        f, i = pl.program_id(0), pl.program_id(1)
        me = _my_pos()[3]

        @pl.when(i == 0)
        def _():
            acc_g[...] = jnp.zeros_like(acc_g)
            acc_u[...] = jnp.zeros_like(acc_u)
            acc_d[...] = jnp.zeros_like(acc_d)

        g = g_ref[...].astype(F32)
        a = (g * _sigmoid(g) * u_ref[...].astype(F32)).astype(BF16)
        hv = h_ref[...]
        acc_g[...] += _mm_tn(dg_ref[...], hv)
        acc_u[...] += _mm_tn(du_ref[...], hv)
        acc_d[...] += _mm_tn(a, dd_ref[...])

        def messages(fs):
            return [(st_g.at[fs], rgu_ref, 0, fs, 0), (st_u.at[fs], rgu_ref, 0, fs + n_f, 1),
                    (st_d.at[fs, pl.ds(0, half)], rd_ref, 1, 2 * fs, 2),
                    (st_d.at[fs, pl.ds(half, half)], rd_ref, 1, 2 * fs + 1, 3)]

        def remote(fs, msg, src_dev):
            src, buf, row, p, j = msg
            return pltpu.make_async_remote_copy(
                src_ref=src, dst_ref=buf.at[src_dev, layer], send_sem=send_sems.at[fs, j],
                recv_sem=recv_sems.at[row, src_dev], device_id=_dev(p), device_id_type=MESH)

        def local(fs, msg):
            src, buf, _, p, j = msg
            return pltpu.make_async_copy(src, buf.at[p, layer], local_sems.at[fs, j])

        for fs in range(n_f):
            @pl.when((f == fs) & (i == n_t - 1))
            def _():
                st_g[fs] = acc_g[...].astype(BF16)
                st_u[fs] = acc_u[...].astype(BF16)
                st_d[fs] = acc_d[...].astype(BF16)
                for msg in messages(fs):
                    @pl.when(me != msg[3])
                    def _():
                        remote(fs, msg, me).start()

                    @pl.when(me == msg[3])
                    def _():
                        local(fs, msg).start()

        @pl.when((f == n_f - 1) & (i == n_t - 1))
        def _():
            for fs in range(n_f):
                for msg in messages(fs):
                    @pl.when(me != msg[3])
                    def _():
                        remote(fs, msg, me).wait_send()

                    @pl.when(me == msg[3])
                    def _():
                        local(fs, msg).wait()
            for src_dev in range(N_DEV):
                @pl.when(me != src_dev)
                def _():
                    remote(0, messages(0)[0], src_dev).wait_recv()
                    remote(0, messages(0)[2], src_dev).wait_recv()

    row = pl.BlockSpec((ts, d), lambda f, i: (i, 0))
    act = pl.BlockSpec((None, ts, fc), lambda f, i: (f, i, 0))
    return _pcall(
        body, (h, dd, g_s, u_s, dg, du, recv_gu, recv_d), name=name, grid=(n_f, n_t),
        in_specs=[row, row, act, act, act, act, ANY, ANY], out_specs=[ANY, ANY],
        out_shape=[SDS(recv_gu.shape, recv_gu.dtype), SDS(recv_d.shape, recv_d.dtype)],
        scratch_shapes=[pltpu.VMEM((fc, d), F32)] * 3 + [pltpu.VMEM((n_f, fc, d), BF16)] * 3
        + [pltpu.SemaphoreType.DMA((n_f, 4)), pltpu.SemaphoreType.DMA((2, N_DEV)), pltpu.SemaphoreType.DMA((n_f, 4))],
        carry=carry, body_aliases={6: 0, 7: 1})


_PROJ_WIDTHS = (W_A, W_A, W_B, KV_W, KV_W, 2 * W_C)


def _mix_in_fwd(x, pre_g, w_in_t, name, carry=()):
    s, d = x.shape
    ts = _time_tile(s)

    def body(x_ref, pg_ref, w_ref, hn_ref, *outs):
        xv = x_ref[...]
        hn = (xv * _rms_r(xv) * pg_ref[...]).astype(BF16)
        hn_ref[...] = hn
        proj = _mm_nt(hn, w_ref[...])
        off = 0
        for o_ref, w in zip(outs, _PROJ_WIDTHS):
            o_ref[...] = proj[:, off:off + w]
            off += w

    row = lambda w: pl.BlockSpec((ts, w), lambda i: (i, 0))
    return _pcall(
        body, (x, pre_g, w_in_t), name=name, grid=(s // ts,),
        in_specs=[row(d), pl.BlockSpec((1, d), lambda i: (0, 0)), pl.BlockSpec((D_IN_PROJ, d), lambda i: (0, 0))],
        out_specs=[row(d)] + [row(w) for w in _PROJ_WIDTHS],
        out_shape=[SDS((s, d), BF16)] + [SDS((s, w), F32) for w in _PROJ_WIDTHS], carry=carry)


def _mix_in_bwd(dres, x, pre_g, hn, w_in_t, dlx, dlg, dq, dk, dk_up, dv, dv_up, dglu, name, carry=()):
    s, d = x.shape
    ts = _time_tile(s)
    n_t = s // ts

    def body(dres_ref, x_ref, pg_ref, hn_ref, w_ref, dlx_ref, dlg_ref, dq_ref, dk_ref, dkn_ref,
             dv_ref, dvn_ref, dglu_ref, dx_ref, dw_ref, dpg_ref, acc):
        i = pl.program_id(0)

        @pl.when(i == 0)
        def _():
            acc[...] = jnp.zeros_like(acc)
            dpg_ref[...] = jnp.zeros_like(dpg_ref)

        def with_next(cur_ref, nxt_ref):
            nxt = jnp.where(i < n_t - 1, nxt_ref[...], 0.0)
            if ts == BLK:
                return cur_ref[...] + nxt
            return jnp.concatenate([cur_ref[:ts - BLK, :], cur_ref[ts - BLK:, :] + nxt], axis=0)

        dproj = jnp.concatenate([dlx_ref[...], dlg_ref[...], dq_ref[...], with_next(dk_ref, dkn_ref),
                                 with_next(dv_ref, dvn_ref), dglu_ref[...]], axis=1).astype(BF16)
        dhn = _mm(dproj, w_ref[...])
        acc[...] += _mm_tn(dproj, hn_ref[...])
        xv = x_ref[...]
        dxv, dp = _rms_bwd(xv, _rms_r(xv), pg_ref[...], dhn)
        dpg_ref[...] += dp
        dx_ref[...] = dres_ref[...] + dxv

        @pl.when(i == n_t - 1)
        def _():
            dw_ref[...] = acc[...].astype(BF16)

    row = lambda w: pl.BlockSpec((ts, w), lambda i: (i, 0))
    nxt = pl.BlockSpec((BLK, KV_W), lambda i: (jnp.minimum(i + 1, n_t - 1), 0))
    vec = pl.BlockSpec((1, d), lambda i: (0, 0))
    full = pl.BlockSpec((D_IN_PROJ, d), lambda i: (0, 0))
    return _pcall(
        body, (dres, x, pre_g, hn, w_in_t, dlx, dlg, dq, dk, dk_up, dv, dv_up, dglu), name=name, grid=(n_t,),
        in_specs=[row(d), row(d), vec, row(d), full, row(W_A), row(W_A), row(W_B), row(KV_W), nxt,
                  row(KV_W), nxt, row(2 * W_C)],
        out_specs=[row(d), full, vec],
        out_shape=[SDS((s, d), F32), SDS((D_IN_PROJ, d), BF16), SDS((1, d), F32)],
        scratch_shapes=[pltpu.VMEM((D_IN_PROJ, d), F32)], carry=carry)


def _lru_gates(xc, lru_p):
    cw_ref, cb_ref, wa_ref, ba_ref, wx_ref, bx_ref, lam_ref = lru_p
    c = cb_ref[...]
    for j in range(LRU_K):
        c = c + cw_ref[j:j + 1, :] * _shift_down(xc, LRU_K - 1 - j)[LRU_HALO:, :]
    r = _sigmoid(_mm(c, wa_ref[...]) + ba_ref[...])
    ig = _sigmoid(_mm(c, wx_ref[...]) + bx_ref[...])
    sp = _softplus(-lam_ref[...])
    log_a = -LRU_C * r * sp
    a = jnp.exp(log_a)
    m = jnp.sqrt(_neg_expm1(2.0 * log_a))
    return c, r, ig, sp, a, m


def _lru_pspecs():
    small = lambda r: pl.BlockSpec((r, W_A), lambda i: (0, 0))
    return [small(LRU_K), small(1), small(W_A), small(1), small(W_A), small(1), small(1)]


def _lru_fwd(lx, lg, lru_p, name, carry=()):
    s = lx.shape[0]
    ts = _time_tile(s)
    n8 = ts // LRU_HALO

    def body(lx_ref, lxp_ref, lg_ref, *rest):
        lru_p, (ya_ref, h_ref, hcarry) = rest[:7], rest[7:]
        i = pl.program_id(0)
        prev = jnp.where(i > 0, lxp_ref[...], 0.0)
        xc = jnp.concatenate([prev, lx_ref[...]], axis=0)
        c, r, ig, sp, a, m = _lru_gates(xc, lru_p)
        acc_a, acc_b = a, m * (ig * c)
        t = lax.broadcasted_iota(jnp.int32, a.shape, 0)
        k = 1
        while k < ts:
            keep = t >= k
            acc_b = jnp.where(keep, acc_a * _shift_down(acc_b, k) + acc_b, acc_b)
            acc_a = jnp.where(keep, acc_a * _shift_down(acc_a, k), acc_a)
            k *= 2
        h0 = jnp.where(i > 0, hcarry[...], 0.0)
        h = acc_b + acc_a * h0
        hcarry[...] = h[ts - 1:ts, :]
        h_ref[...] = h
        ya_ref[...] = _gelu(lg_ref[...])[0] * h

    row = pl.BlockSpec((ts, W_A), lambda i: (i, 0))
    prev8 = pl.BlockSpec((LRU_HALO, W_A), lambda i: (jnp.maximum(i * n8 - 1, 0), 0))
    return _pcall(
        body, (lx, lx, lg, *lru_p), name=name, grid=(s // ts,),
        in_specs=[row, prev8, row] + _lru_pspecs(), out_specs=[row, row],
        out_shape=[SDS((s, W_A), F32), SDS((s, W_A), F32)],
        scratch_shapes=[pltpu.VMEM((1, W_A), F32)], carry=carry)


def _lru_bwd(dya, lx, lg, h_s, lru_p, name, carry=()):
    s = lx.shape[0]
    ts = _time_tile(s)
    n_t = s // ts
    n8 = ts // LRU_HALO

    def body(dya_ref, lx_ref, lxp_ref, lg_ref, h_ref, hp_ref, *rest):
        lru_p = rest[:7]
        (dlx_ref, dlg_ref, dcw_ref, dcb_ref, dwa_ref, dba_ref, dwx_ref, dbx_ref, dlam_ref,
         carry_a, carry_l, carry_dc) = rest[7:]
        cw_ref, _, wa_ref, _, wx_ref, _, lam_ref = lru_p
        i = pl.program_id(0)
        first_tile = i == n_t - 1
        last_tile = i == 0

        @pl.when(i == 0)
        def _():
            for ref in (dcw_ref, dcb_ref, dwa_ref, dba_ref, dwx_ref, dbx_ref, dlam_ref):
                ref[...] = jnp.zeros_like(ref)

        prev = jnp.where(first_tile, 0.0, lxp_ref[...])
        xc = jnp.concatenate([prev, lx_ref[...]], axis=0)
        c, r, ig, sp, a, m = _lru_gates(xc, lru_p)
        h = h_ref[...]
        hcat = jnp.concatenate([jnp.where(first_tile, 0.0, hp_ref[...]), h], axis=0)
        h_m1 = _shift_down(hcat, 1)[LRU_HALO:, :]
        lg = lg_ref[...]
        ge, th = _gelu(lg)
        dya = dya_ref[...]
        dlg_ref[...] = dya * h * _dgelu(lg, th)
        dh = dya * ge
        t = lax.broadcasted_iota(jnp.int32, a.shape, 0)
        a_next = jnp.where(t < ts - 1, _shift_up(a, 1), jnp.where(last_tile, 0.0, carry_a[...]))
        acc_a, acc_b = a_next, dh
        k = 1
        while k < ts:
            keep = t < ts - k
            acc_b = jnp.where(keep, acc_a * _shift_up(acc_b, k) + acc_b, acc_b)
            acc_a = jnp.where(keep, acc_a * _shift_up(acc_a, k), acc_a)
            k *= 2
        lam_beyond = jnp.where(last_tile, 0.0, carry_l[...])
        lmb = acc_b + acc_a * lam_beyond
        carry_a[...] = a[0:1, :]
        carry_l[...] = lmb[0:1, :]
        gi = ig * c
        dgi = lmb * m
        dla = lmb * h_m1 * a - (lmb * gi) * (a * a) / m
        dr = dla * (-LRU_C * sp)
        dsp = jnp.sum(dla * (-LRU_C * r), axis=0, keepdims=True)
        dlam_ref[...] += -dsp * _sigmoid(-lam_ref[...])
        dra = dr * r * (1.0 - r)
        dia = dgi * c * ig * (1.0 - ig)
        dc = dgi * ig + _mm_nt(dra, wa_ref[...]) + _mm_nt(dia, wx_ref[...])
        dwa_ref[...] += _mm_tn(c, dra)
        dwx_ref[...] += _mm_tn(c, dia)
        dba_ref[...] += jnp.sum(dra, axis=0, keepdims=True)
        dbx_ref[...] += jnp.sum(dia, axis=0, keepdims=True)
        dcb_ref[...] += jnp.sum(dc, axis=0, keepdims=True)
        dcc = jnp.concatenate([dc, jnp.where(last_tile, 0.0, carry_dc[...])], axis=0)
        carry_dc[...] = dc[0:LRU_HALO, :]
        dlx = jnp.zeros_like(dc)
        for j in range(LRU_K):
            sh = LRU_K - 1 - j
            dcw_ref[j:j + 1, :] += jnp.sum(dc * _shift_down(xc, sh)[LRU_HALO:, :], axis=0, keepdims=True)
            dlx = dlx + cw_ref[j:j + 1, :] * _shift_up(dcc, sh)[:ts, :]
        dlx_ref[...] = dlx

    row = pl.BlockSpec((ts, W_A), lambda i: (n_t - 1 - i, 0))
    prev8 = pl.BlockSpec((LRU_HALO, W_A), lambda i: (jnp.maximum((n_t - 1 - i) * n8 - 1, 0), 0))
    small = lambda r: pl.BlockSpec((r, W_A), lambda i: (0, 0))
    return _pcall(
        body, (dya, lx, lx, lg, h_s, h_s, *lru_p), name=name, grid=(n_t,),
        in_specs=[row, row, prev8, row, row, prev8] + _lru_pspecs(),
        out_specs=[row, row, small(LRU_K), small(1), small(W_A), small(1), small(W_A), small(1), small(1)],
        out_shape=[SDS((s, W_A), F32), SDS((s, W_A), F32), SDS((LRU_K, W_A), F32), SDS((1, W_A), F32),
                   SDS((W_A, W_A), F32), SDS((1, W_A), F32), SDS((W_A, W_A), F32), SDS((1, W_A), F32),
                   SDS((1, W_A), F32)],
        scratch_shapes=[pltpu.VMEM((1, W_A), F32), pltpu.VMEM((1, W_A), F32), pltpu.VMEM((LRU_HALO, W_A), F32)],
        carry=carry)


_ATT_ROWS = N_Q_HEADS * BLK
_GRP_ROWS = Q_PER_KV * BLK


def _attn_stack(ref, rows, g):
    return jnp.concatenate([ref[rows, h * HEAD_DIM:(h + 1) * HEAD_DIM]
                            for h in range(g * Q_PER_KV, (g + 1) * Q_PER_KV)], axis=0)


def _attn_unstack(parts):
    return jnp.concatenate([p[j * BLK:(j + 1) * BLK, :] for p in parts for j in range(Q_PER_KV)], axis=1)


def _grp(x, g):
    return x[g * _GRP_ROWS:(g + 1) * _GRP_ROWS, :]


def _attn_block(q_ref, k_ref, kp_ref, v_ref, vp_ref, sink_col, i, b):
    rows, prev = slice(b * BLK, (b + 1) * BLK), slice((b - 1) * BLK, b * BLK)
    qs, kcs, kps, vcs, vps = [], [], [], [], []
    for g in range(N_KV_HEADS):
        cols = slice(g * HEAD_DIM, (g + 1) * HEAD_DIM)
        qs.append(_attn_stack(q_ref, rows, g))
        kcs.append(k_ref[rows, cols])
        vcs.append(v_ref[rows, cols])
        kps.append(kp_ref[:, cols] if b == 0 else k_ref[prev, cols])
        vps.append(vp_ref[:, cols] if b == 0 else v_ref[prev, cols])
    scale = 1.0 / math.sqrt(HEAD_DIM)
    sc = jnp.concatenate([_mm_nt(qs[g], kcs[g]) for g in range(N_KV_HEADS)], axis=0) * scale
    sp = jnp.concatenate([_mm_nt(qs[g], kps[g]) for g in range(N_KV_HEADS)], axis=0) * scale
    qi = lax.broadcasted_iota(jnp.int32, (_ATT_ROWS, BLK), 0) & (BLK - 1)
    kj = lax.broadcasted_iota(jnp.int32, (_ATT_ROWS, BLK), 1)
    sc = jnp.where(kj <= qi, sc, NEG_BIG)
    sp = jnp.where((kj > qi) if b > 0 else ((kj > qi) & (i > 0)), sp, NEG_BIG)
    m = jnp.maximum(jnp.maximum(jnp.max(sc, axis=-1, keepdims=True), jnp.max(sp, axis=-1, keepdims=True)), sink_col)
    pc = jnp.exp(sc - m)
    pp = jnp.exp(sp - m)
    es = jnp.exp(sink_col - m)
    inv = 1.0 / (jnp.sum(pc, axis=-1, keepdims=True) + jnp.sum(pp, axis=-1, keepdims=True) + es)
    return qs, kcs, kps, vcs, vps, pc * inv, pp * inv, es * inv


def _attn_specs(s, ts):
    bpt = ts // BLK
    tile = lambda w: pl.BlockSpec((ts, w), lambda i: (i, 0))
    prv = pl.BlockSpec((BLK, KV_W), lambda i: (jnp.maximum(i * bpt - 1, 0), 0))
    sink = pl.BlockSpec((_ATT_ROWS, 1), lambda i: (0, 0))
    return bpt, tile, prv, sink


def _attn_fwd(q, k, v, sink_col, name, carry=()):
    s = q.shape[0]
    ts = _time_tile(s)
    bpt, tile, prv, sink = _attn_specs(s, ts)

    def body(q_ref, k_ref, kp_ref, v_ref, vp_ref, sk_ref, y_ref):
        i = pl.program_id(0)
        for b in range(bpt):
            _, _, _, vcs, vps, pc, pp, _ = _attn_block(q_ref, k_ref, kp_ref, v_ref, vp_ref, sk_ref[...], i, b)
            outs = [_mm(_grp(pc, g), vcs[g]) + _mm(_grp(pp, g), vps[g]) for g in range(N_KV_HEADS)]
            y_ref[b * BLK:(b + 1) * BLK, :] = _attn_unstack(outs)

    return _pcall(
        body, (q, k, k, v, v, sink_col), name=name, grid=(s // ts,),
        in_specs=[tile(W_B), tile(KV_W), prv, tile(KV_W), prv, sink],
        out_specs=[tile(W_B)], out_shape=[SDS((s, W_B), F32)], carry=carry)


def _attn_bwd(dy, q, k, v, sinks, name, carry=()):
    s = q.shape[0]
    ts = _time_tile(s)
    n_t = s // ts
    bpt, tile, prv, sink = _attn_specs(s, ts)

    def body(dy_ref, q_ref, k_ref, kp_ref, v_ref, vp_ref, sk_ref, dq_ref, dk_ref, dv_ref, dku_ref, dvu_ref, dsk_ref):
        i = pl.program_id(0)

        @pl.when(i == 0)
        def _():
            dsk_ref[...] = jnp.zeros_like(dsk_ref)

        scale = 1.0 / math.sqrt(HEAD_DIM)
        groups = range(N_KV_HEADS)
        head_row = lax.broadcasted_iota(jnp.int32, (N_Q_HEADS, BLK), 0)
        dsk = jnp.zeros((N_Q_HEADS, BLK), F32)
        dk_blocks, dv_blocks = [], []
        for b in range(bpt):
            rows = slice(b * BLK, (b + 1) * BLK)
            qs, kcs, kps, vcs, vps, pc, pp, ps = _attn_block(q_ref, k_ref, kp_ref, v_ref, vp_ref, sk_ref[...], i, b)
            dos = [_attn_stack(dy_ref, rows, g) for g in groups]
            dpc = jnp.concatenate([_mm_nt(dos[g], vcs[g]) for g in groups], axis=0)
            dpp = jnp.concatenate([_mm_nt(dos[g], vps[g]) for g in groups], axis=0)
            delta = jnp.sum(pc * dpc, axis=-1, keepdims=True) + jnp.sum(pp * dpp, axis=-1, keepdims=True)
            dsc = pc * (dpc - delta) * scale
            dsp = pp * (dpp - delta) * scale
            dq_ref[rows, :] = _attn_unstack([_mm(_grp(dsc, g), kcs[g]) + _mm(_grp(dsp, g), kps[g]) for g in groups])
            dk_blocks.append(jnp.concatenate([_mm_tn(_grp(dsc, g), qs[g]) for g in groups], axis=1))
            dv_blocks.append(jnp.concatenate([_mm_tn(_grp(pc, g), dos[g]) for g in groups], axis=1))
            dkp = jnp.concatenate([_mm_tn(_grp(dsp, g), qs[g]) for g in groups], axis=1)
            dvp = jnp.concatenate([_mm_tn(_grp(pp, g), dos[g]) for g in groups], axis=1)
            if b == 0:
                dku_ref[...] = dkp
                dvu_ref[...] = dvp
            else:
                dk_blocks[b - 1] = dk_blocks[b - 1] + dkp
                dv_blocks[b - 1] = dv_blocks[b - 1] + dvp
            dsink = -ps * delta
            for h in range(N_Q_HEADS):
                dsk = dsk + jnp.where(head_row == h, jnp.sum(dsink[h * BLK:(h + 1) * BLK, :], axis=0, keepdims=True), 0.0)
        for b in range(bpt):
            dk_ref[b * BLK:(b + 1) * BLK, :] = dk_blocks[b]
            dv_ref[b * BLK:(b + 1) * BLK, :] = dv_blocks[b]
        dsk_ref[...] += dsk

    up = pl.BlockSpec((BLK, KV_W), lambda i: (i, 0))
    return _pcall(
        body, (dy, q, k, k, v, v, sinks), name=name, grid=(n_t,),
        in_specs=[tile(W_B), tile(W_B), tile(KV_W), prv, tile(KV_W), prv, sink],
        out_specs=[tile(W_B), tile(KV_W), tile(KV_W), up, up, pl.BlockSpec((N_Q_HEADS, BLK), lambda i: (0, 0))],
        out_shape=[SDS((s, W_B), F32), SDS((s, KV_W), F32), SDS((s, KV_W), F32), SDS((n_t * BLK, KV_W), F32),
                   SDS((n_t * BLK, KV_W), F32), SDS((N_Q_HEADS, BLK), F32)], carry=carry)


def _cc_recompute(glu_ref, glup_ref, cw_ref, cb_ref, first_tile):
    prev = jnp.where(first_tile, 0.0, glup_ref[...])
    ge = jnp.concatenate([prev, glu_ref[...]], axis=0)
    y0 = ge[:, :W_C] * _sigmoid(ge[:, W_C:])
    y1 = cb_ref[...]
    for j in range(CC_K):
        y1 = y1 + cw_ref[j:j + 1, :] * _shift_down(y0, CC_K - 1 - j)[CC_HALO:, :]
    return y0, y1


def _ln_stats(y1):
    mu = jnp.mean(y1, axis=-1, keepdims=True)
    xc = y1 - mu
    rstd = lax.rsqrt(jnp.mean(xc * xc, axis=-1, keepdims=True) + LN_EPS)
    return xc * rstd, rstd


def _cc_specs(s, ts):
    n32 = ts // CC_HALO
    row = lambda w: pl.BlockSpec((ts, w), lambda i: (i, 0))
    prev = pl.BlockSpec((CC_HALO, 2 * W_C), lambda i: (jnp.maximum(i * n32 - 1, 0), 0))
    small = lambda r: pl.BlockSpec((r, W_C), lambda i: (0, 0))
    return row, prev, small


def _cc_fwd(glu, cw, cb, lng, lnb, name, carry=()):
    s = glu.shape[0]
    ts = _time_tile(s)
    row, prev, small = _cc_specs(s, ts)

    def body(glu_ref, glup_ref, cw_ref, cb_ref, lng_ref, lnb_ref, y_ref):
        _, y1 = _cc_recompute(glu_ref, glup_ref, cw_ref, cb_ref, pl.program_id(0) == 0)
        xhat, _ = _ln_stats(y1)
        z = xhat * lng_ref[...] + lnb_ref[...]
        y_ref[...] = z * _sigmoid(z)

    return _pcall(
        body, (glu, glu, cw, cb, lng, lnb), name=name, grid=(s // ts,),
        in_specs=[row(2 * W_C), prev, small(CC_HALO), small(1), small(1), small(1)],
        out_specs=[row(W_C)], out_shape=[SDS((s, W_C), F32)], carry=carry)


def _cc_bwd_conv(dy, glu, cw, cb, lng, lnb, name, carry=()):
    s = glu.shape[0]
    ts = _time_tile(s)
    row, prev, small = _cc_specs(s, ts)

    def body(dy_ref, glu_ref, glup_ref, cw_ref, cb_ref, lng_ref, lnb_ref, dy1_ref, dcw_ref, dcb_ref, dlng_ref, dlnb_ref):
        i = pl.program_id(0)

        @pl.when(i == 0)
        def _():
            for ref in (dcw_ref, dcb_ref, dlng_ref, dlnb_ref):
                ref[...] = jnp.zeros_like(ref)

        y0, y1 = _cc_recompute(glu_ref, glup_ref, cw_ref, cb_ref, i == 0)
        xhat, rstd = _ln_stats(y1)
        z = xhat * lng_ref[...] + lnb_ref[...]
        dz = dy_ref[...] * _dsilu(z, _sigmoid(z))
        dlng_ref[...] += jnp.sum(dz * xhat, axis=0, keepdims=True)
        dlnb_ref[...] += jnp.sum(dz, axis=0, keepdims=True)
        dxh = dz * lng_ref[...]
        dy1 = rstd * (dxh - jnp.mean(dxh, axis=-1, keepdims=True) - xhat * jnp.mean(dxh * xhat, axis=-1, keepdims=True))
        dy1_ref[...] = dy1
        dcb_ref[...] += jnp.sum(dy1, axis=0, keepdims=True)
        for j in range(CC_K):
            dcw_ref[j:j + 1, :] += jnp.sum(dy1 * _shift_down(y0, CC_K - 1 - j)[CC_HALO:, :], axis=0, keepdims=True)

    return _pcall(
        body, (dy, glu, glu, cw, cb, lng, lnb), name=name, grid=(s // ts,),
        in_specs=[row(W_C), row(2 * W_C), prev, small(CC_HALO), small(1), small(1), small(1)],
        out_specs=[row(W_C), small(CC_HALO), small(1), small(1), small(1)],
        out_shape=[SDS((s, W_C), F32), SDS((CC_HALO, W_C), F32)] + [SDS((1, W_C), F32)] * 3, carry=carry)


def _cc_bwd_glu(dy1, glu, cw, name, carry=()):
    s = glu.shape[0]
    ts = _time_tile(s)
    n_t = s // ts
    n32 = ts // CC_HALO

    def body(dy1_ref, dyn_ref, glu_ref, cw_ref, dglu_ref):
        i = pl.program_id(0)
        dcat = jnp.concatenate([dy1_ref[...], jnp.where(i < n_t - 1, dyn_ref[...], 0.0)], axis=0)
        dy0 = jnp.zeros((ts, W_C), F32)
        for j in range(CC_K):
            dy0 = dy0 + cw_ref[j:j + 1, :] * _shift_up(dcat, CC_K - 1 - j)[:ts, :]
        a = glu_ref[:, :W_C]
        sg = _sigmoid(glu_ref[:, W_C:])
        dglu_ref[...] = jnp.concatenate([dy0 * sg, dy0 * a * sg * (1.0 - sg)], axis=1)

    row = lambda w: pl.BlockSpec((ts, w), lambda i: (i, 0))
    nxt = pl.BlockSpec((CC_HALO, W_C), lambda i: (jnp.minimum((i + 1) * n32, s // CC_HALO - 1), 0))
    return _pcall(
        body, (dy1, dy1, glu, cw), name=name, grid=(n_t,),
        in_specs=[row(W_C), nxt, row(2 * W_C), pl.BlockSpec((CC_HALO, W_C), lambda i: (0, 0))],
        out_specs=[row(2 * W_C)], out_shape=[SDS((s, 2 * W_C), F32)], carry=carry)


_MIX_OFFS = ((0, W_A), (W_A, W_A + W_B), (W_A + W_B, W_A + W_B + W_C))


def _mix_out_fwd(x, ya, yb, yc, group_g, w_out, post_g, name, carry=()):
    s, d = x.shape
    ts = _time_tile(s)
    dm = w_out.shape[0]

    def body(x_ref, ya_ref, yb_ref, yc_ref, gg_ref, w_ref, qg_ref, xo_ref, o_ref):
        parts = []
        for y_ref, (lo, hi) in zip((ya_ref, yb_ref, yc_ref), _MIX_OFFS):
            yv = y_ref[...]
            parts.append(yv * _rms_r(yv) * gg_ref[:, lo:hi])
        o = _mm(jnp.concatenate(parts, axis=1), w_ref[...])
        o_ref[...] = o
        xo_ref[...] = x_ref[...] + o * _rms_r(o) * qg_ref[...]

    row = lambda w: pl.BlockSpec((ts, w), lambda i: (i, 0))
    return _pcall(
        body, (x, ya, yb, yc, group_g, w_out, post_g), name=name, grid=(s // ts,),
        in_specs=[row(d), row(W_A), row(W_B), row(W_C), pl.BlockSpec((1, dm), lambda i: (0, 0)),
                  pl.BlockSpec((dm, d), lambda i: (0, 0)), pl.BlockSpec((1, d), lambda i: (0, 0))],
        out_specs=[row(d), row(d)], out_shape=[SDS((s, d), F32), SDS((s, d), F32)], carry=carry)


def _mix_out_bwd(dxo, o, ya, yb, yc, group_g, w_out, post_g, name, carry=()):
    s, d = o.shape
    ts = _time_tile(s)
    n_t = s // ts
    dm = w_out.shape[0]

    def body(dxo_ref, o_ref, ya_ref, yb_ref, yc_ref, gg_ref, w_ref, qg_ref,
             dya_ref, dyb_ref, dyc_ref, dw_ref, dqg_ref, dgg_ref, acc):
        i = pl.program_id(0)

        @pl.when(i == 0)
        def _():
            acc[...] = jnp.zeros_like(acc)
            dqg_ref[...] = jnp.zeros_like(dqg_ref)
            dgg_ref[...] = jnp.zeros_like(dgg_ref)

        ov = o_ref[...]
        do, dq = _rms_bwd(ov, _rms_r(ov), qg_ref[...], dxo_ref[...])
        dqg_ref[...] += dq
        do = do.astype(BF16)
        dyn = _mm_nt(do, w_ref[...])
        parts, dggs = [], []
        for y_ref, dy_ref, (lo, hi) in zip((ya_ref, yb_ref, yc_ref), (dya_ref, dyb_ref, dyc_ref), _MIX_OFFS):
            yv = y_ref[...]
            r = _rms_r(yv)
            gg = gg_ref[:, lo:hi]
            parts.append(yv * r * gg)
            dyv, dg = _rms_bwd(yv, r, gg, dyn[:, lo:hi])
            dy_ref[...] = dyv
            dggs.append(dg)
        dgg_ref[...] += jnp.concatenate(dggs, axis=1)
        acc[...] += _mm_tn(jnp.concatenate(parts, axis=1), do)

        @pl.when(i == n_t - 1)
        def _():
            dw_ref[...] = acc[...].astype(BF16)

    row = lambda w: pl.BlockSpec((ts, w), lambda i: (i, 0))
    full = pl.BlockSpec((dm, d), lambda i: (0, 0))
    return _pcall(
        body, (dxo, o, ya, yb, yc, group_g, w_out, post_g), name=name, grid=(n_t,),
        in_specs=[row(d), row(d), row(W_A), row(W_B), row(W_C), pl.BlockSpec((1, dm), lambda i: (0, 0)), full,
                  pl.BlockSpec((1, d), lambda i: (0, 0))],
        out_specs=[row(W_A), row(W_B), row(W_C), full, pl.BlockSpec((1, d), lambda i: (0, 0)),
                   pl.BlockSpec((1, dm), lambda i: (0, 0))],
        out_shape=[SDS((s, W_A), F32), SDS((s, W_B), F32), SDS((s, W_C), F32), SDS((dm, d), BF16),
                   SDS((1, d), F32), SDS((1, dm), F32)],
        scratch_shapes=[pltpu.VMEM((dm, d), F32)], carry=carry)


def _loss_head(y, target, name):
    s, d = y.shape
    ts = _time_tile(s)

    def body(y_ref, t_ref, loss_ref, dy_ref):
        @pl.when(pl.program_id(0) == 0)
        def _():
            loss_ref[...] = jnp.zeros_like(loss_ref)

        err = y_ref[...] - t_ref[...]
        dy_ref[...] = err * (1.0 / d)
        per_tok = jnp.mean(err * err, axis=-1, keepdims=True)
        loss_ref[...] += 0.5 * jnp.sum(per_tok, axis=0, keepdims=True)

    row = pl.BlockSpec((ts, d), lambda i: (i, 0))
    return _pcall(body, (y, target), name=name, grid=(s // ts,), in_specs=[row, row],
                  out_specs=[pl.BlockSpec((1, BLK), lambda i: (0, 0)), row],
                  out_shape=[SDS((1, BLK), F32), SDS((s, d), F32)])[0]


def _adamw_math(w, g, m, v):
    m = ADAM_B1 * m + (1.0 - ADAM_B1) * g
    v = ADAM_B2 * v + (1.0 - ADAM_B2) * (g * g)
    m_hat = m / (1.0 - ADAM_B1 ** ADAM_STEP)
    v_hat = v / (1.0 - ADAM_B2 ** ADAM_STEP)
    delta = -ADAM_LR * (m_hat / (jnp.sqrt(v_hat) + ADAM_EPS) + ADAM_WD * w)
    return delta, m, v


def _row_tile(rows, cap=256):
    best = None
    for t in range(16, min(rows, cap) + 1, 16):
        if rows % t == 0:
            best = t
    return best if best is not None else rows


def _reduce_adamw(recv, w, m, v, name):
    n_l, r, c = w.shape
    tr = _row_tile(r)

    def body(recv_ref, w_ref, m_ref, v_ref, g_ref, d_ref, nm_ref, nv_ref):
        g = recv_ref[0].astype(F32)
        for p in range(1, N_DEV):
            g = g + recv_ref[p].astype(F32)
        g_ref[...] = g
        d_ref[...], nm_ref[...], nv_ref[...] = _adamw_math(w_ref[...], g, m_ref[...], v_ref[...])

    blk = pl.BlockSpec((None, tr, c), lambda l, i: (l, i, 0))
    return _pcall(
        body, (recv, w, m, v), name=name, grid=(n_l, r // tr),
        in_specs=[pl.BlockSpec((N_DEV, None, tr, c), lambda l, i: (0, l, i, 0)), blk, blk, blk],
        out_specs=[blk] * 4, out_shape=[SDS(w.shape, F32)] * 4)[0]


def _reduce_adamw_small(parts, w, m, v, name):
    def body(p_ref, w_ref, m_ref, v_ref, g_ref, d_ref, nm_ref, nv_ref):
        g = p_ref[0]
        for p in range(1, N_DEV):
            g = g + p_ref[p]
        g_ref[...] = g
        d_ref[...], nm_ref[...], nv_ref[...] = _adamw_math(w_ref[...], g, m_ref[...], v_ref[...])

    vm = pl.BlockSpec(memory_space=pltpu.VMEM)
    return pl.pallas_call(body, name=name, in_specs=[vm] * 4, out_specs=[vm] * 4, out_shape=[SDS(w.shape, F32)] * 4,
                          compiler_params=pltpu.CompilerParams(vmem_limit_bytes=VMEM_LIMIT))(parts, w, m, v)


def _pack(arrs):
    flat = jnp.concatenate([a.reshape(-1).astype(F32) for a in arrs])
    pad = (-flat.shape[0]) % (8 * BLK)
    return jnp.pad(flat, (0, pad)).reshape(-1, BLK)


def _unpack(packed, shapes):
    flat = packed.reshape(-1)
    out, off = [], 0
    for shp in shapes:
        n = math.prod(shp)
        out.append(flat[off:off + n].reshape(shp))
        off += n
    return out


def _block_diag(w):
    nb, bw, _ = w.shape
    eye = jnp.eye(nb, dtype=w.dtype)
    return (eye[:, None, :, None] * w[:, :, None, :]).reshape(nb * bw, nb * bw)


def _diag_blocks(wd, nb):
    bw = wd.shape[0] // nb
    return jnp.stack([wd[b * bw:(b + 1) * bw, b * bw:(b + 1) * bw] for b in range(nb)])


WEIGHT_NAMES = ['ffn1_pre_g', 'ffn1_w_gu', 'ffn1_w_down', 'ffn1_post_g', 'mix_pre_g', 'w_in', 'lru_conv_w', 'lru_conv_b',
                'lru_w_a', 'lru_b_a', 'lru_w_x', 'lru_b_x', 'lru_lambda', 'attn_sinks', 'conv_w', 'conv_b', 'conv_ln_g',
                'conv_ln_b', 'group_g', 'w_out', 'mix_post_g', 'ffn2_pre_g', 'ffn2_w_gu', 'ffn2_w_down', 'ffn2_post_g']
BIG = ('ffn1_w_gu', 'ffn1_w_down', 'w_in', 'w_out', 'ffn2_w_gu', 'ffn2_w_down')
TRANSPOSED = ('ffn1_w_gu', 'ffn2_w_gu', 'w_in')
SMALL = tuple(k for k in WEIGHT_NAMES if k not in BIG)
CHANNEL_SHARDED = ('lru_conv_w', 'conv_w')


def _step(x, target, w, m, v):
    n_l = w['ffn1_pre_g'].shape[0]
    assert n_l == 2, "the exchange schedule below is laid out for two layers"
    s, d = x.shape[1], x.shape[2]
    x = x.reshape(s, d)
    target = target.reshape(s, d)
    me = _my_pos()[3]
    tview = lambda t, k: jnp.swapaxes(t[k], 1, 2) if k in TRANSPOSED else t[k]
    wb = {k: tview(w, k).astype(BF16) for k in BIG}
    vec = lambda name, l: w[name][l][None, :]

    conv_shard = _pack([w['lru_conv_w'], w['conv_w']])
    g0 = _all_gather([(wb['ffn1_w_gu'], 0), (wb['ffn1_w_down'], 0), (wb['w_in'], 0), (wb['w_out'], 0),
                      (conv_shard, None)], "all_gather_first")
    wts = [dict(), dict()]
    wts[0]['ffn1_w_gu'], wts[0]['ffn1_w_down'], wts[0]['w_in'], wts[0]['w_out'], conv_g = g0
    ch = W_A // N_DEV
    conv_parts = [_unpack(conv_g[p], [(n_l, LRU_K, ch), (n_l, CC_K, ch)]) for p in range(N_DEV)]
    lru_cw = jnp.concatenate([cp[0] for cp in conv_parts], axis=-1)
    cc_cw = jnp.concatenate([cp[1] for cp in conv_parts], axis=-1)
    cc_cw = jnp.pad(cc_cw, ((0, 0), (0, CC_HALO - CC_K), (0, 0)))

    def stage_a(names, l):
        return _gather_a([(wb[k], l) for k in names], two_level=True)

    def finish(names, l, bufs):
        for k, b in zip(names, bufs):
            wts[l][k] = b

    saved = []
    h = x
    pend = {}
    for l in range(n_l):
        sv = {'x0': h}
        lw = wts[l]
        if l == 0:
            carry = [stage_a(('ffn2_w_gu', 'ffn2_w_down'), 0)]
        else:
            carry = [_gather_b(pend.pop('f2wo_1'))]
        (x1, sv['h1'], sv['g1'], sv['u1'], sv['d1']), ex = _ffn_fwd(
            h, vec('ffn1_pre_g', l), vec('ffn1_post_g', l), lw['ffn1_w_gu'], lw['ffn1_w_down'], f"ffn1_fwd_l{l}", carry)
        if l == 0:
            pend['f2_0'] = ex[0]
        else:
            finish(('ffn2_w_gu', 'ffn2_w_down', 'w_out'), 1, ex[0])
        sv['x1'] = x1
        carry = [_gather_b(pend.pop('f2_0'))] if l == 0 else []
        (sv['hn'], lx, lg, q, k, vv, glu), ex = _mix_in_fwd(
            x1, vec('mix_pre_g', l), lw['w_in'].reshape(D_IN_PROJ, d), f"mix_in_fwd_l{l}", carry)
        if l == 0:
            finish(('ffn2_w_gu', 'ffn2_w_down'), 0, ex[0])
        sv.update(lx=lx, lg=lg, q=q, k=k, v=vv, glu=glu)
        lru_p = (lru_cw[l], vec('lru_conv_b', l), _block_diag(w['lru_w_a'][l]).astype(BF16), vec('lru_b_a', l),
                 _block_diag(w['lru_w_x'][l]).astype(BF16), vec('lru_b_x', l), vec('lru_lambda', l))
        cc_p = (cc_cw[l], vec('conv_b', l), vec('conv_ln_g', l), vec('conv_ln_b', l))
        sv.update(lru_p=lru_p, cc_p=cc_p)
        (sv['ya'], sv['hs']), _ = _lru_fwd(lx, lg, lru_p, f"lru_fwd_l{l}")
        carry = [stage_a(('ffn1_w_gu', 'ffn1_w_down'), 1)] if l == 0 else []
        sv['sink_col'] = jnp.repeat(w['attn_sinks'][l], BLK)[:, None]
        (sv['yb'],), ex = _attn_fwd(q, k, vv, sv['sink_col'], f"attn_fwd_l{l}", carry)
        if l == 0:
            pend['f1_1'] = ex[0]
        carry = [_gather_b(pend.pop('f1_1'))] if l == 0 else []
        (sv['yc'],), ex = _cc_fwd(glu, *cc_p, f"cconv_fwd_l{l}", carry)
        if l == 0:
            finish(('ffn1_w_gu', 'ffn1_w_down'), 1, ex[0])
        carry = [stage_a(('w_in',), 1)] if l == 0 else []
        (x2, sv['o']), ex = _mix_out_fwd(x1, sv['ya'], sv['yb'], sv['yc'], vec('group_g', l),
                                         lw['w_out'].reshape(-1, d), vec('mix_post_g', l), f"mix_out_fwd_l{l}", carry)
        if l == 0:
            pend['wi_1'] = ex[0]
        sv['x2'] = x2
        carry = [_gather_b(pend.pop('wi_1')), stage_a(('ffn2_w_gu', 'ffn2_w_down', 'w_out'), 1)] if l == 0 else []
        (h, sv['h2'], sv['g2'], sv['u2'], sv['d2']), ex = _ffn_fwd(
            x2, vec('ffn2_pre_g', l), vec('ffn2_post_g', l), lw['ffn2_w_gu'], lw['ffn2_w_down'], f"ffn2_fwd_l{l}", carry)
        if l == 0:
            finish(('w_in',), 1, ex[0])
            pend['f2wo_1'] = ex[1]
        saved.append(sv)

    loss_row, dh = _loss_head(h, target, "loss_head")
    loss = lax.psum(loss_row[0, 0], AXES)

    recv = {k: None for k in BIG}
    ready = {}
    small = [dict() for _ in range(n_l)]

    def exchange(keys):
        return _grad_x([(ready.pop(key), key[1], recv[key[0]]) for key in keys], n_l)

    def received(keys, bufs):
        for key, b in zip(keys, bufs):
            recv[key[0]] = b

    def run(fn, *args, keys=()):
        outs, ex = fn(*args, carry=[exchange(keys)] if keys else [])
        if keys:
            received(keys, ex[0])
        return outs

    for l in reversed(range(n_l)):
        sv, sg, lw = saved[l], small[l], wts[l]
        keys = [] if l == n_l - 1 else [('ffn1_w_gu', l + 1)]
        dx2, dd, dg, du, sg['ffn2_pre_g'], sg['ffn2_post_g'] = run(
            _ffn_bwd_act, dh, sv['d2'], sv['x2'], vec('ffn2_pre_g', l), vec('ffn2_post_g', l), sv['g2'], sv['u2'],
            lw['ffn2_w_gu'], lw['ffn2_w_down'], f"ffn2_bwd_act_l{l}", keys=keys)
        keys = [] if l == n_l - 1 else [('ffn1_w_down', l + 1)]
        dwg, dwu, dwd = run(_ffn_bwd_w, sv['h2'], dd, sv['g2'], sv['u2'], dg, du, f"ffn2_bwd_w_l{l}", keys=keys)
        ready[('ffn2_w_gu', l)] = [dwg, dwu]
        ready[('ffn2_w_down', l)] = [dwd.reshape(N_DEV, -1, d)]
        dya, dyb, dyc, dw_out, sg['mix_post_g'], sg['group_g'] = run(
            _mix_out_bwd, dx2, sv['o'], sv['ya'], sv['yb'], sv['yc'], vec('group_g', l), lw['w_out'].reshape(-1, d),
            vec('mix_post_g', l), f"mix_out_bwd_l{l}")
        ready[('w_out', l)] = [dw_out.reshape(N_DEV, -1, d)]
        (dlx, dlg, sg['lru_conv_w'], sg['lru_conv_b'], dwa, sg['lru_b_a'], dwx, sg['lru_b_x'],
         sg['lru_lambda']) = run(_lru_bwd, dya, sv['lx'], sv['lg'], sv['hs'], sv['lru_p'], f"lru_bwd_l{l}")
        sg['lru_w_a'] = _diag_blocks(dwa, A_BLOCKS)
        sg['lru_w_x'] = _diag_blocks(dwx, A_BLOCKS)
        dq, dk, dv, dk_up, dv_up, dsk = run(_attn_bwd, dyb, sv['q'], sv['k'], sv['v'], sv['sink_col'],
                                            f"attn_bwd_l{l}", keys=[('ffn2_w_gu', l)])
        sg['attn_sinks'] = dsk[:, 0]
        dy1, dcw, sg['conv_b'], sg['conv_ln_g'], sg['conv_ln_b'] = run(
            _cc_bwd_conv, dyc, sv['glu'], *sv['cc_p'], f"cconv_bwd_conv_l{l}")
        sg['conv_w'] = dcw[:CC_K]
        (dglu,) = run(_cc_bwd_glu, dy1, sv['glu'], sv['cc_p'][0], f"cconv_bwd_glu_l{l}")
        dx1, dw_in, sg['mix_pre_g'] = run(
            _mix_in_bwd, dx2, sv['x1'], vec('mix_pre_g', l), sv['hn'], lw['w_in'].reshape(D_IN_PROJ, d),
            dlx, dlg, dq, dk, dk_up, dv, dv_up, dglu, f"mix_in_bwd_l{l}", keys=[('w_out', l)])
        ready[('w_in', l)] = [dw_in.reshape(N_DEV, -1, d)]
        dh, dd, dg, du, sg['ffn1_pre_g'], sg['ffn1_post_g'] = run(
            _ffn_bwd_act, dx1, sv['d1'], sv['x0'], vec('ffn1_pre_g', l), vec('ffn1_post_g', l), sv['g1'], sv['u1'],
            lw['ffn1_w_gu'], lw['ffn1_w_down'], f"ffn1_bwd_act_l{l}", keys=[('ffn2_w_down', l), ('w_in', l)])
        if l > 0:
            dwg, dwu, dwd = run(_ffn_bwd_w, sv['h1'], dd, sv['g1'], sv['u1'], dg, du, f"ffn1_bwd_w_l{l}")
            ready[('ffn1_w_gu', l)] = [dwg, dwu]
            ready[('ffn1_w_down', l)] = [dwd.reshape(N_DEV, -1, d)]
        else:
            part = _pack([jnp.stack([small[j][k] for j in range(n_l)]) for k in SMALL])
            (recv['ffn1_w_gu'], recv['ffn1_w_down']), ex = _ffn_bwd_w_send(
                sv['h1'], dd, sv['g1'], sv['u1'], dg, du, recv['ffn1_w_gu'], recv['ffn1_w_down'], 0, "ffn1_bwd_w_send_l0",
                [_gather_a([(part, None)], two_level=False)])
            small_parts = ex[0][0]
    grad_x = dh.reshape(1, s, d)

    out = {}
    for k in BIG:
        res = _reduce_adamw(recv[k], tview(w, k), tview(m, k), tview(v, k), f"reduce_adamw_{k}")
        out[k] = [jnp.swapaxes(r, 1, 2) for r in res] if k in TRANSPOSED else res

    small_shapes = [(n_l,) + tuple(small[0][k].shape) for k in SMALL]

    def widen(t, k):
        if k not in CHANNEL_SHARDED:
            return t.reshape((n_l,) + tuple(small[0][k].shape))
        full = jnp.zeros((n_l,) + tuple(small[0][k].shape), F32)
        return lax.dynamic_update_slice_in_dim(full, t, me * ch, axis=2)

    packed = [_pack([widen(src[k], k) for k in SMALL]) for src in (w, m, v)]
    res = _reduce_adamw_small(small_parts, *packed, "reduce_adamw_small")
    for k, g, dlt, nm, nv in zip(SMALL, *[_unpack(r, small_shapes) for r in res]):
        vals = [g, dlt, nm, nv]
        if k in CHANNEL_SHARDED:
            vals = [lax.dynamic_slice_in_dim(t, me * ch, ch, axis=2) for t in vals]
        out[k] = [t.reshape(w[k].shape) for t in vals]

    return (loss, grad_x, *[out[k][0] for k in WEIGHT_NAMES], *[out[k][1] for k in WEIGHT_NAMES],
            *[out[k][2] for k in WEIGHT_NAMES], *[out[k][3] for k in WEIGHT_NAMES])


def kernel(x, ffn1_pre_g, ffn1_w_gu, ffn1_w_down, ffn1_post_g, mix_pre_g, w_in, lru_conv_w, lru_conv_b, lru_w_a, lru_b_a, lru_w_x, lru_b_x, lru_lambda, attn_sinks, conv_w, conv_b, conv_ln_g, conv_ln_b, group_g, w_out, mix_post_g, ffn2_pre_g, ffn2_w_gu, ffn2_w_down, ffn2_post_g, loss_target, m_ffn1_pre_g, m_ffn1_w_gu, m_ffn1_w_down, m_ffn1_post_g, m_mix_pre_g, m_w_in, m_lru_conv_w, m_lru_conv_b, m_lru_w_a, m_lru_b_a, m_lru_w_x, m_lru_b_x, m_lru_lambda, m_attn_sinks, m_conv_w, m_conv_b, m_conv_ln_g, m_conv_ln_b, m_group_g, m_w_out, m_mix_post_g, m_ffn2_pre_g, m_ffn2_w_gu, m_ffn2_w_down, m_ffn2_post_g, v_ffn1_pre_g, v_ffn1_w_gu, v_ffn1_w_down, v_ffn1_post_g, v_mix_pre_g, v_w_in, v_lru_conv_w, v_lru_conv_b, v_lru_w_a, v_lru_b_a, v_lru_w_x, v_lru_b_x, v_lru_lambda, v_attn_sinks, v_conv_w, v_conv_b, v_conv_ln_g, v_conv_ln_b, v_group_g, v_w_out, v_mix_post_g, v_ffn2_pre_g, v_ffn2_w_gu, v_ffn2_w_down, v_ffn2_post_g):
    args = (ffn1_pre_g, ffn1_w_gu, ffn1_w_down, ffn1_post_g, mix_pre_g, w_in, lru_conv_w, lru_conv_b, lru_w_a, lru_b_a, lru_w_x, lru_b_x, lru_lambda, attn_sinks, conv_w, conv_b, conv_ln_g, conv_ln_b, group_g, w_out, mix_post_g, ffn2_pre_g, ffn2_w_gu, ffn2_w_down, ffn2_post_g)
    ms = (m_ffn1_pre_g, m_ffn1_w_gu, m_ffn1_w_down, m_ffn1_post_g, m_mix_pre_g, m_w_in, m_lru_conv_w, m_lru_conv_b, m_lru_w_a, m_lru_b_a, m_lru_w_x, m_lru_b_x, m_lru_lambda, m_attn_sinks, m_conv_w, m_conv_b, m_conv_ln_g, m_conv_ln_b, m_group_g, m_w_out, m_mix_post_g, m_ffn2_pre_g, m_ffn2_w_gu, m_ffn2_w_down, m_ffn2_post_g)
    vs = (v_ffn1_pre_g, v_ffn1_w_gu, v_ffn1_w_down, v_ffn1_post_g, v_mix_pre_g, v_w_in, v_lru_conv_w, v_lru_conv_b, v_lru_w_a, v_lru_b_a, v_lru_w_x, v_lru_b_x, v_lru_lambda, v_attn_sinks, v_conv_w, v_conv_b, v_conv_ln_g, v_conv_ln_b, v_group_g, v_w_out, v_mix_post_g, v_ffn2_pre_g, v_ffn2_w_gu, v_ffn2_w_down, v_ffn2_post_g)
    return _step(x, loss_target, dict(zip(WEIGHT_NAMES, args)), dict(zip(WEIGHT_NAMES, ms)), dict(zip(WEIGHT_NAMES, vs)))
```

```python
import functools
import math
import operator

import jax
import jax.numpy as jnp
from jax import lax
from jax.experimental import pallas as pl
from jax.experimental.pallas import tpu as pltpu

F32 = jnp.float32
BF16 = jnp.bfloat16
N_DEV = 8
AXES = ("x", "y", "c")
MESH = pl.DeviceIdType.MESH

NORM_EPS = 1e-6
LN_EPS = 1e-5
NEG_BIG = -1e30
W_A = 256
W_B = 512
W_C = 256
HEAD_DIM = 64
N_Q_HEADS = 8
N_KV_HEADS = 2
Q_PER_KV = N_Q_HEADS // N_KV_HEADS
KV_W = N_KV_HEADS * HEAD_DIM
BLK = 128
LRU_K = 4
LRU_C = 8.0
A_BLOCKS = 4
CC_K = 31
CC_HALO = 32
LRU_HALO = 8
D_IN_PROJ = 2 * W_A + W_B + 2 * KV_W + 2 * W_C
ADAM_LR = 0.001
ADAM_B1 = 0.9
ADAM_B2 = 0.999
ADAM_EPS = 1e-08
ADAM_WD = 0.01
ADAM_STEP = 10
VMEM_LIMIT = 56 * 1024 * 1024

SDS = jax.ShapeDtypeStruct
ANY = pl.BlockSpec(memory_space=pl.ANY)


def _time_tile(s):
    return max(BLK, s // 8)


class _Exchange:
    def __init__(self, inputs, out_shapes, aliases, sem_shapes, start, wait):
        self.inputs, self.out_shapes, self.aliases, self.sem_shapes = inputs, out_shapes, aliases, sem_shapes
        self.start, self.wait = start, wait


def _my_pos():
    x, y, c = (lax.axis_index(a) for a in AXES)
    return x, y, c, 4 * x + 2 * y + c


def _flip(k):
    x, y, c, _ = _my_pos()
    return (1 - x if k & 4 else x, 1 - y if k & 2 else y, 1 - c if k & 1 else c)


def _slot(dev):
    return 4 * dev[0] + 2 * dev[1] + dev[2]


def _dev(p):
    return (p >> 2, (p >> 1) & 1, p & 1)


def _gather_a(items, two_level):
    rels = (1, 2, 4, 6) if two_level else tuple(range(1, N_DEV))
    n = len(items)
    src_of = lambda ins, a: ins[a] if items[a][1] is None else ins[a].at[items[a][1]]

    def shape_of(a):
        arr, l = items[a]
        return arr.shape if l is None else arr.shape[1:]

    def copies(ins, outs, sems, a):
        send, recv, _ = sems
        me = _my_pos()[3]
        return [(k, pltpu.make_async_remote_copy(
            src_ref=src_of(ins, a), dst_ref=outs[a].at[me], send_sem=send.at[a, k], recv_sem=recv.at[a, k],
            device_id=_flip(k), device_id_type=MESH)) for k in rels]

    def local(ins, outs, sems, a):
        return pltpu.make_async_copy(src_of(ins, a), outs[a].at[_my_pos()[3]], sems[2].at[a])

    def start(ins, outs, sems):
        for a in range(n):
            local(ins, outs, sems, a).start()
            for _, cp in copies(ins, outs, sems, a):
                cp.start()

    def wait(ins, outs, sems):
        send, recv, _ = sems
        for a in range(n):
            for k, cp in copies(ins, outs, sems, a):
                pltpu.make_async_remote_copy(
                    src_ref=src_of(ins, a), dst_ref=outs[a].at[_slot(_flip(k))], send_sem=send.at[a, k],
                    recv_sem=recv.at[a, k], device_id=_flip(k), device_id_type=MESH).wait_recv()
                cp.wait_send()
            local(ins, outs, sems, a).wait()

    return _Exchange([it[0] for it in items], [SDS((N_DEV,) + shape_of(a), items[a][0].dtype) for a in range(n)], {},
                     [pltpu.SemaphoreType.DMA((n, N_DEV)), pltpu.SemaphoreType.DMA((n, N_DEV)),
                      pltpu.SemaphoreType.DMA((n,))], start, wait)


def _gather_b(bufs):
    n = len(bufs)

    def copies(ins, outs, sems, a, c_of_block):
        send, recv = sems
        x, y, c, _ = _my_pos()
        res = []
        for k in (2, 4, 6):
            chip = _flip(k)
            blk = _slot((chip[0], chip[1], c if c_of_block == "mine" else 1 - c))
            res.append(pltpu.make_async_remote_copy(
                src_ref=ins[a].at[blk], dst_ref=outs[a].at[blk], send_sem=send.at[a, k], recv_sem=recv.at[a, k],
                device_id=_flip(1), device_id_type=MESH))
        return res

    def start(ins, outs, sems):
        for a in range(n):
            for cp in copies(ins, outs, sems, a, "mine"):
                cp.start()

    def wait(ins, outs, sems):
        for a in range(n):
            for cp in copies(ins, outs, sems, a, "sibling"):
                cp.wait_recv()
            for cp in copies(ins, outs, sems, a, "mine"):
                cp.wait_send()

    return _Exchange(list(bufs), [SDS(b.shape, b.dtype) for b in bufs], {a: a for a in range(n)},
                     [pltpu.SemaphoreType.DMA((n, N_DEV)), pltpu.SemaphoreType.DMA((n, N_DEV))], start, wait)


def _grad_x(items, n_l):
    n = len(items)
    inputs, first_in, recv_in, aliases, out_shapes = [], [], [], {}, []
    for a, (arrs, l, recv) in enumerate(items):
        first_in.append(len(inputs))
        inputs += list(arrs)
        assert sum(arr.shape[0] for arr in arrs) == N_DEV
        if recv is not None:
            aliases[len(inputs)] = a
            inputs.append(recv)
        out_shapes.append(SDS((N_DEV, n_l) + arrs[0].shape[1:], arrs[0].dtype))

    def slab(ins, a, p):
        off = 0
        for j, arr in enumerate(items[a][0]):
            if p < off + arr.shape[0]:
                return ins[first_in[a] + j].at[p - off]
            off += arr.shape[0]
        raise AssertionError

    def rdma(ins, outs, sems, a, p, src_dev):
        send, recv, _ = sems
        return pltpu.make_async_remote_copy(
            src_ref=slab(ins, a, p), dst_ref=outs[a].at[src_dev, items[a][1]], send_sem=send.at[a, p],
            recv_sem=recv.at[a, src_dev], device_id=_dev(p), device_id_type=MESH)

    def local(ins, outs, sems, a, p):
        return pltpu.make_async_copy(slab(ins, a, p), outs[a].at[p, items[a][1]], sems[2].at[a])

    def start(ins, outs, sems):
        me = _my_pos()[3]
        for p in range(N_DEV):
            @pl.when(me != p)
            def _():
                for a in range(n):
                    rdma(ins, outs, sems, a, p, me).start()

            @pl.when(me == p)
            def _():
                for a in range(n):
                    local(ins, outs, sems, a, p).start()

    def wait(ins, outs, sems):
        me = _my_pos()[3]
        for p in range(N_DEV):
            @pl.when(me != p)
            def _():
                for a in range(n):
                    rdma(ins, outs, sems, a, p, p).wait_recv()
                    rdma(ins, outs, sems, a, p, p).wait_send()

            @pl.when(me == p)
            def _():
                for a in range(n):
                    local(ins, outs, sems, a, p).wait()

    return _Exchange(inputs, out_shapes, aliases,
                     [pltpu.SemaphoreType.DMA((n, N_DEV)), pltpu.SemaphoreType.DMA((n, N_DEV)),
                      pltpu.SemaphoreType.DMA((n,))], start, wait)


def _pcall(body, args, *, name, grid, in_specs, out_specs, out_shape, scratch_shapes=(), carry=(), body_aliases=None):
    n_in, n_out, n_scr = len(in_specs), len(out_specs), len(scratch_shapes)
    c_in = [len(e.inputs) for e in carry]
    c_out = [len(e.out_shapes) for e in carry]
    c_sem = [len(e.sem_shapes) for e in carry]
    aliases = dict(body_aliases or {})
    for j, e in enumerate(carry):
        for i_loc, o_loc in e.aliases.items():
            aliases[n_in + sum(c_in[:j]) + i_loc] = n_out + sum(c_out[:j]) + o_loc

    def wrapped(*refs):
        def take(counts, pos):
            groups = []
            for cnt in counts:
                groups.append(refs[pos:pos + cnt])
                pos += cnt
            return groups, pos

        (ins,), pos = take([n_in], 0)
        cins, pos = take(c_in, pos)
        (outs,), pos = take([n_out], pos)
        couts, pos = take(c_out, pos)
        (scr,), pos = take([n_scr], pos)
        csems, pos = take(c_sem, pos)
        if carry:
            ids = [pl.program_id(k) for k in range(len(grid))]
            first = functools.reduce(operator.and_, [i == 0 for i in ids])
            last = functools.reduce(operator.and_, [i == g - 1 for i, g in zip(ids, grid)])

            @pl.when(first)
            def _():
                for e, ci, co, cs in zip(carry, cins, couts, csems):
                    e.start(ci, co, cs)

        body(*ins, *outs, *scr)
        if carry:
            @pl.when(last)
            def _():
                for e, ci, co, cs in zip(carry, cins, couts, csems):
                    e.wait(ci, co, cs)

    res = pl.pallas_call(
        wrapped, name=name, grid=grid,
        in_specs=list(in_specs) + [ANY] * sum(c_in),
        out_specs=list(out_specs) + [ANY] * sum(c_out),
        out_shape=list(out_shape) + [s for e in carry for s in e.out_shapes],
        scratch_shapes=list(scratch_shapes) + [s for e in carry for s in e.sem_shapes],
        input_output_aliases=aliases,
        compiler_params=pltpu.CompilerParams(dimension_semantics=("arbitrary",) * len(grid),
                                             vmem_limit_bytes=VMEM_LIMIT),
    )(*args, *[a for e in carry for a in e.inputs])
    outs, pos, extra = list(res[:n_out]), n_out, []
    for cnt in c_out:
        extra.append(list(res[pos:pos + cnt]))
        pos += cnt
    return outs, extra


def _all_gather(items, name):
    n = len(items)
    shape_of = lambda a: items[a][0].shape if items[a][1] is None else items[a][0].shape[1:]

    def body(*refs):
        ins, outs, (send_sems, recv_sems, local_sems) = refs[:n], refs[n:2 * n], refs[2 * n:]
        x, y, c, me = _my_pos()
        src_of = lambda a: ins[a] if items[a][1] is None else ins[a].at[items[a][1]]

        def copy(a, k, block, to, src=None):
            dst = outs[a].at[_slot(block)]
            return pltpu.make_async_remote_copy(
                src_ref=dst if src is None else src, dst_ref=dst,
                send_sem=send_sems.at[a, k], recv_sem=recv_sems.at[a, k], device_id=to, device_id_type=MESH)

        mine = [pltpu.make_async_copy(src_of(a), outs[a].at[me], local_sems.at[a]) for a in range(n)]
        for cp in mine:
            cp.start()
        first = [copy(a, k, (x, y, c), _flip(k), src=src_of(a)) for a in range(n) for k in (1, 2, 4, 6)]
        for cp in first:
            cp.start()
        passed = []
        for k in (2, 4, 6):
            for a in range(n):
                copy(a, k, _flip(k), (x, y, c)).wait_recv()
                fwd = copy(a, k + 1, _flip(k), _flip(1))
                fwd.start()
                passed.append(fwd)
        for a in range(n):
            copy(a, 1, _flip(1), (x, y, c)).wait_recv()
            for k in (2, 4, 6):
                copy(a, k + 1, _flip(k + 1), (x, y, c)).wait_recv()
        for cp in first + passed:
            cp.wait_send()
        for cp in mine:
            cp.wait()

    return pl.pallas_call(
        body, name=name,
        in_specs=[ANY] * n, out_specs=[ANY] * n,
        out_shape=[SDS((N_DEV,) + shape_of(a), items[a][0].dtype) for a in range(n)],
        scratch_shapes=[pltpu.SemaphoreType.DMA((n, N_DEV)), pltpu.SemaphoreType.DMA((n, N_DEV)),
                        pltpu.SemaphoreType.DMA((n,))],
    )(*[it[0] for it in items])


def _mm(a, b):
    return jnp.dot(a.astype(BF16), b.astype(BF16), preferred_element_type=F32)


def _mm_nt(a, b):
    return lax.dot_general(a.astype(BF16), b.astype(BF16), (((1,), (1,)), ((), ())), preferred_element_type=F32)


def _mm_tn(a, b):
    return lax.dot_general(a.astype(BF16), b.astype(BF16), (((0,), (0,)), ((), ())), preferred_element_type=F32)


def _rms_r(x):
    return lax.rsqrt(jnp.mean(x * x, axis=-1, keepdims=True) + NORM_EPS)


def _rms_bwd(x, r, g, dy):
    gy = dy * g
    dx = r * (gy - x * (r * r) * jnp.mean(gy * x, axis=-1, keepdims=True))
    dg = jnp.sum(dy * x * r, axis=0, keepdims=True)
    return dx, dg


def _sigmoid(x):
    return 1.0 / (1.0 + jnp.exp(-x))


def _dsilu(z, sz):
    return sz * (1.0 + z * (1.0 - sz))


_GELU_C = math.sqrt(2.0 / math.pi)


def _gelu(x):
    t = jnp.tanh(_GELU_C * (x + 0.044715 * x * x * x))
    return 0.5 * x * (1.0 + t), t


def _dgelu(x, t):
    return 0.5 * (1.0 + t) + 0.5 * x * (1.0 - t * t) * _GELU_C * (1.0 + 3.0 * 0.044715 * x * x)


def _log1p(e):
    return jnp.where(e < 1e-2, e * (1.0 - e * (0.5 - e * (1.0 / 3.0))), jnp.log(1.0 + e))


def _softplus(x):
    return jnp.maximum(x, 0.0) + _log1p(jnp.exp(-jnp.abs(x)))


def _neg_expm1(x):
    small = -x * (1.0 + x * (0.5 + x * (1.0 / 6.0) * (1.0 + x * 0.25)))
    return jnp.where(x > -1e-2, small, 1.0 - jnp.exp(x))


def _shift_down(x, s):
    return x if s == 0 else pltpu.roll(x, s, 0)


def _shift_up(x, s):
    return x if s == 0 else pltpu.roll(x, x.shape[0] - s, 0)


def _ffn_wspecs(d, fc, order):
    f_of = (lambda i, f: f) if order == "tf" else (lambda f, i: f)
    n_f = N_DEV // 2
    return [pl.BlockSpec((None, fc, d), lambda *g: (f_of(*g), 0, 0)),
            pl.BlockSpec((None, fc, d), lambda *g: (f_of(*g) + n_f, 0, 0)),
            pl.BlockSpec((2, fc // 2, d), lambda *g: (f_of(*g), 0, 0))]


def _ffn_fwd(x, pre_g, post_g, wgu_t, wd, name, carry=()):
    s, d = x.shape
    fc = wgu_t.shape[1]
    ts = 2 * _time_tile(s)
    n_t, n_f = s // ts, N_DEV // 2

    def body(x_ref, pg_ref, qg_ref, wg_ref, wu_ref, wd_ref, xo_ref, h_ref, g_ref, u_ref, d_ref, h_scr, acc):
        f = pl.program_id(1)

        @pl.when(f == 0)
        def _():
            xv = x_ref[...]
            hv = (xv * _rms_r(xv) * pg_ref[...]).astype(BF16)
            h_scr[...] = hv
            h_ref[...] = hv
            acc[...] = jnp.zeros_like(acc)

        hv = h_scr[...]
        g = _mm_nt(hv, wg_ref[...])
        u = _mm_nt(hv, wu_ref[...])
        g_ref[...] = g.astype(BF16)
        u_ref[...] = u.astype(BF16)
        a = (g * _sigmoid(g) * u).astype(BF16)
        acc[...] += jnp.dot(a, wd_ref[...].reshape(fc, d), preferred_element_type=F32)

        @pl.when(f == n_f - 1)
        def _():
            dv = acc[...]
            d_ref[...] = dv.astype(BF16)
            xo_ref[...] = x_ref[...] + 0.5 * (dv * _rms_r(dv) * qg_ref[...])

    row = pl.BlockSpec((ts, d), lambda i, f: (i, 0))
    vec = pl.BlockSpec((1, d), lambda i, f: (0, 0))
    act = pl.BlockSpec((None, ts, fc), lambda i, f: (f, i, 0))
    return _pcall(
        body, (x, pre_g, post_g, wgu_t, wgu_t, wd), name=name, grid=(n_t, n_f),
        in_specs=[row, vec, vec] + _ffn_wspecs(d, fc, "tf"),
        out_specs=[row, row, act, act, row],
        out_shape=[SDS((s, d), F32), SDS((s, d), BF16), SDS((n_f, s, fc), BF16), SDS((n_f, s, fc), BF16),
                   SDS((s, d), BF16)],
        scratch_shapes=[pltpu.VMEM((ts, d), BF16), pltpu.VMEM((ts, d), F32)], carry=carry)


def _ffn_bwd_act(dxo, dmid, x, pre_g, post_g, g_s, u_s, wgu_t, wd, name, carry=()):
    s, d = x.shape
    fc = wgu_t.shape[1]
    ts = _time_tile(s)
    n_t, n_f = s // ts, N_DEV // 2

    def body(dxo_ref, dm_ref, x_ref, pg_ref, qg_ref, g_ref, u_ref, wg_ref, wu_ref, wd_ref,
             dx_ref, dd_ref, dg_ref, du_ref, dpg_ref, dqg_ref, dd_scr, dh_acc):
        i, f = pl.program_id(0), pl.program_id(1)

        @pl.when((i == 0) & (f == 0))
        def _():
            dpg_ref[...] = jnp.zeros_like(dpg_ref)
            dqg_ref[...] = jnp.zeros_like(dqg_ref)

        @pl.when(f == 0)
        def _():
            dv = dm_ref[...].astype(F32)
            ddv, dq = _rms_bwd(dv, _rms_r(dv), qg_ref[...], 0.5 * dxo_ref[...])
            dqg_ref[...] += dq
            dd_scr[...] = ddv.astype(BF16)
            dd_ref[...] = ddv.astype(BF16)
            dh_acc[...] = jnp.zeros_like(dh_acc)

        da = _mm_nt(dd_scr[...], wd_ref[...].reshape(fc, d))
        g = g_ref[...].astype(F32)
        u = u_ref[...].astype(F32)
        sg = _sigmoid(g)
        du = (da * (g * sg)).astype(BF16)
        dg = (da * u * _dsilu(g, sg)).astype(BF16)
        dg_ref[...] = dg
        du_ref[...] = du
        dh_acc[...] += _mm(dg, wg_ref[...]) + _mm(du, wu_ref[...])

        @pl.when(f == n_f - 1)
        def _():
            xv = x_ref[...]
            dxv, dp = _rms_bwd(xv, _rms_r(xv), pg_ref[...], dh_acc[...])
            dpg_ref[...] += dp
            dx_ref[...] = dxo_ref[...] + dxv

    row = pl.BlockSpec((ts, d), lambda i, f: (i, 0))
    vec = pl.BlockSpec((1, d), lambda i, f: (0, 0))
    act = pl.BlockSpec((None, ts, fc), lambda i, f: (f, i, 0))
    return _pcall(
        body, (dxo, dmid, x, pre_g, post_g, g_s, u_s, wgu_t, wgu_t, wd), name=name, grid=(n_t, n_f),
        in_specs=[row, row, row, vec, vec, act, act] + _ffn_wspecs(d, fc, "tf"),
        out_specs=[row, row, act, act, vec, vec],
        out_shape=[SDS((s, d), F32), SDS((s, d), BF16), SDS((n_f, s, fc), BF16), SDS((n_f, s, fc), BF16),
                   SDS((1, d), F32), SDS((1, d), F32)],
        scratch_shapes=[pltpu.VMEM((ts, d), BF16), pltpu.VMEM((ts, d), F32)], carry=carry)


def _ffn_bwd_w(h, dd, g_s, u_s, dg, du, name, carry=()):
    s, d = h.shape
    n_f, _, fc = g_s.shape
    ts = _time_tile(s)
    n_t = s // ts

    def body(h_ref, dd_ref, g_ref, u_ref, dg_ref, du_ref, wg_ref, wu_ref, wd_ref, acc_g, acc_u, acc_d):
        i = pl.program_id(1)

        @pl.when(i == 0)
        def _():
            acc_g[...] = jnp.zeros_like(acc_g)
            acc_u[...] = jnp.zeros_like(acc_u)
            acc_d[...] = jnp.zeros_like(acc_d)

        g = g_ref[...].astype(F32)
        a = (g * _sigmoid(g) * u_ref[...].astype(F32)).astype(BF16)
        hv = h_ref[...]
        acc_g[...] += _mm_tn(dg_ref[...], hv)
        acc_u[...] += _mm_tn(du_ref[...], hv)
        acc_d[...] += _mm_tn(a, dd_ref[...])

        @pl.when(i == n_t - 1)
        def _():
            wg_ref[...] = acc_g[...].astype(BF16)
            wu_ref[...] = acc_u[...].astype(BF16)
            wd_ref[...] = acc_d[...].astype(BF16)

    row = pl.BlockSpec((ts, d), lambda f, i: (i, 0))
    act = pl.BlockSpec((None, ts, fc), lambda f, i: (f, i, 0))
    out = pl.BlockSpec((None, fc, d), lambda f, i: (f, 0, 0))
    return _pcall(
        body, (h, dd, g_s, u_s, dg, du), name=name, grid=(n_f, n_t),
        in_specs=[row, row, act, act, act, act], out_specs=[out, out, out],
        out_shape=[SDS((n_f, fc, d), BF16)] * 3,
        scratch_shapes=[pltpu.VMEM((fc, d), F32)] * 3, carry=carry)


def _ffn_bwd_w_send(h, dd, g_s, u_s, dg, du, recv_gu, recv_d, layer, name, carry=()):
    s, d = h.shape
    n_f, _, fc = g_s.shape
    ts = _time_tile(s)
    n_t = s // ts
    half = fc // 2

    def body(h_ref, dd_ref, g_ref, u_ref, dg_ref, du_ref, _rgu_in, _rd_in, rgu_ref, rd_ref,
             acc_g, acc_u, acc_d, st_g, st_u, st_d, send_sems, recv_sems, local_sems):
        f, i = pl.program_id(0), pl.program_id(1)
        me = _my_pos()[3]

        @pl.when(i == 0)
        def _():
            acc_g[...] = jnp.zeros_like(acc_g)
            acc_u[...] = jnp.zeros_like(acc_u)
            acc_d[...] = jnp.zeros_like(acc_d)

        g = g_ref[...].astype(F32)
        a = (g * _sigmoid(g) * u_ref[...].astype(F32)).astype(BF16)
        hv = h_ref[...]
        acc_g[...] += _mm_tn(dg_ref[...], hv)
        acc_u[...] += _mm_tn(du_ref[...], hv)
        acc_d[...] += _mm_tn(a, dd_ref[...])

        def messages(fs):
            return [(st_g.at[fs], rgu_ref, 0, fs, 0), (st_u.at[fs], rgu_ref, 0, fs + n_f, 1),
                    (st_d.at[fs, pl.ds(0, half)], rd_ref, 1, 2 * fs, 2),
                    (st_d.at[fs, pl.ds(half, half)], rd_ref, 1, 2 * fs + 1, 3)]

        def remote(fs, msg, src_dev):
            src, buf, row, p, j = msg
            return pltpu.make_async_remote_copy(
                src_ref=src, dst_ref=buf.at[src_dev, layer], send_sem=send_sems.at[fs, j],
                recv_sem=recv_sems.at[row, src_dev], device_id=_dev(p), device_id_type=MESH)

        def local(fs, msg):
            src, buf, _, p, j = msg
            return pltpu.make_async_copy(src, buf.at[p, layer], local_sems.at[fs, j])

        for fs in range(n_f):
            @pl.when((f == fs) & (i == n_t - 1))
            def _():
                st_g[fs] = acc_g[...].astype(BF16)
                st_u[fs] = acc_u[...].astype(BF16)
                st_d[fs] = acc_d[...].astype(BF16)
                for msg in messages(fs):
                    @pl.when(me != msg[3])
                    def _():
                        remote(fs, msg, me).start()

                    @pl.when(me == msg[3])
                    def _():
                        local(fs, msg).start()

        @pl.when((f == n_f - 1) & (i == n_t - 1))
        def _():
            for fs in range(n_f):
                for msg in messages(fs):
                    @pl.when(me != msg[3])
                    def _():
                        remote(fs, msg, me).wait_send()

                    @pl.when(me == msg[3])
                    def _():
                        local(fs, msg).wait()
            for src_dev in range(N_DEV):
                @pl.when(me != src_dev)
                def _():
                    remote(0, messages(0)[0], src_dev).wait_recv()
                    remote(0, messages(0)[2], src_dev).wait_recv()

    row = pl.BlockSpec((ts, d), lambda f, i: (i, 0))
    act = pl.BlockSpec((None, ts, fc), lambda f, i: (f, i, 0))
    return _pcall(
        body, (h, dd, g_s, u_s, dg, du, recv_gu, recv_d), name=name, grid=(n_f, n_t),
        in_specs=[row, row, act, act, act, act, ANY, ANY], out_specs=[ANY, ANY],
        out_shape=[SDS(recv_gu.shape, recv_gu.dtype), SDS(recv_d.shape, recv_d.dtype)],
        scratch_shapes=[pltpu.VMEM((fc, d), F32)] * 3 + [pltpu.VMEM((n_f, fc, d), BF16)] * 3
        + [pltpu.SemaphoreType.DMA((n_f, 4)), pltpu.SemaphoreType.DMA((2, N_DEV)), pltpu.SemaphoreType.DMA((n_f, 4))],
        carry=carry, body_aliases={6: 0, 7: 1})


_PROJ_WIDTHS = (W_A, W_A, W_B, KV_W, KV_W, 2 * W_C)


def _mix_in_fwd(x, pre_g, w_in_t, name, carry=()):
    s, d = x.shape
    ts = _time_tile(s)

    def body(x_ref, pg_ref, w_ref, hn_ref, *outs):
        xv = x_ref[...]
        hn = (xv * _rms_r(xv) * pg_ref[...]).astype(BF16)
        hn_ref[...] = hn
        proj = _mm_nt(hn, w_ref[...])
        off = 0
        for o_ref, w in zip(outs, _PROJ_WIDTHS):
            o_ref[...] = proj[:, off:off + w]
            off += w

    row = lambda w: pl.BlockSpec((ts, w), lambda i: (i, 0))
    return _pcall(
        body, (x, pre_g, w_in_t), name=name, grid=(s // ts,),
        in_specs=[row(d), pl.BlockSpec((1, d), lambda i: (0, 0)), pl.BlockSpec((D_IN_PROJ, d), lambda i: (0, 0))],
        out_specs=[row(d)] + [row(w) for w in _PROJ_WIDTHS],
        out_shape=[SDS((s, d), BF16)] + [SDS((s, w), F32) for w in _PROJ_WIDTHS], carry=carry)


def _mix_in_bwd(dres, x, pre_g, hn, w_in_t, dlx, dlg, dq, dk, dk_up, dv, dv_up, dglu, name, carry=()):
    s, d = x.shape
    ts = _time_tile(s)
    n_t = s // ts

    def body(dres_ref, x_ref, pg_ref, hn_ref, w_ref, dlx_ref, dlg_ref, dq_ref, dk_ref, dkn_ref,
             dv_ref, dvn_ref, dglu_ref, dx_ref, dw_ref, dpg_ref, acc):
        i = pl.program_id(0)

        @pl.when(i == 0)
        def _():
            acc[...] = jnp.zeros_like(acc)
            dpg_ref[...] = jnp.zeros_like(dpg_ref)

        def with_next(cur_ref, nxt_ref):
            nxt = jnp.where(i < n_t - 1, nxt_ref[...], 0.0)
            if ts == BLK:
                return cur_ref[...] + nxt
            return jnp.concatenate([cur_ref[:ts - BLK, :], cur_ref[ts - BLK:, :] + nxt], axis=0)

        dproj = jnp.concatenate([dlx_ref[...], dlg_ref[...], dq_ref[...], with_next(dk_ref, dkn_ref),
                                 with_next(dv_ref, dvn_ref), dglu_ref[...]], axis=1).astype(BF16)
        dhn = _mm(dproj, w_ref[...])
        acc[...] += _mm_tn(dproj, hn_ref[...])
        xv = x_ref[...]
        dxv, dp = _rms_bwd(xv, _rms_r(xv), pg_ref[...], dhn)
        dpg_ref[...] += dp
        dx_ref[...] = dres_ref[...] + dxv

        @pl.when(i == n_t - 1)
        def _():
            dw_ref[...] = acc[...].astype(BF16)

    row = lambda w: pl.BlockSpec((ts, w), lambda i: (i, 0))
    nxt = pl.BlockSpec((BLK, KV_W), lambda i: (jnp.minimum(i + 1, n_t - 1), 0))
    vec = pl.BlockSpec((1, d), lambda i: (0, 0))
    full = pl.BlockSpec((D_IN_PROJ, d), lambda i: (0, 0))
    return _pcall(
        body, (dres, x, pre_g, hn, w_in_t, dlx, dlg, dq, dk, dk_up, dv, dv_up, dglu), name=name, grid=(n_t,),
        in_specs=[row(d), row(d), vec, row(d), full, row(W_A), row(W_A), row(W_B), row(KV_W), nxt,
                  row(KV_W), nxt, row(2 * W_C)],
        out_specs=[row(d), full, vec],
        out_shape=[SDS((s, d), F32), SDS((D_IN_PROJ, d), BF16), SDS((1, d), F32)],
        scratch_shapes=[pltpu.VMEM((D_IN_PROJ, d), F32)], carry=carry)


def _lru_gates(xc, lru_p):
    cw_ref, cb_ref, wa_ref, ba_ref, wx_ref, bx_ref, lam_ref = lru_p
    c = cb_ref[...]
    for j in range(LRU_K):
        c = c + cw_ref[j:j + 1, :] * _shift_down(xc, LRU_K - 1 - j)[LRU_HALO:, :]
    r = _sigmoid(_mm(c, wa_ref[...]) + ba_ref[...])
    ig = _sigmoid(_mm(c, wx_ref[...]) + bx_ref[...])
    sp = _softplus(-lam_ref[...])
    log_a = -LRU_C * r * sp
    a = jnp.exp(log_a)
    m = jnp.sqrt(_neg_expm1(2.0 * log_a))
    return c, r, ig, sp, a, m


def _lru_pspecs():
    small = lambda r: pl.BlockSpec((r, W_A), lambda i: (0, 0))
    return [small(LRU_K), small(1), small(W_A), small(1), small(W_A), small(1), small(1)]


def _lru_fwd(lx, lg, lru_p, name, carry=()):
    s = lx.shape[0]
    ts = _time_tile(s)
    n8 = ts // LRU_HALO

    def body(lx_ref, lxp_ref, lg_ref, *rest):
        lru_p, (ya_ref, h_ref, hcarry) = rest[:7], rest[7:]
        i = pl.program_id(0)
        prev = jnp.where(i > 0, lxp_ref[...], 0.0)
        xc = jnp.concatenate([prev, lx_ref[...]], axis=0)
        c, r, ig, sp, a, m = _lru_gates(xc, lru_p)
        acc_a, acc_b = a, m * (ig * c)
        t = lax.broadcasted_iota(jnp.int32, a.shape, 0)
        k = 1
        while k < ts:
            keep = t >= k
            acc_b = jnp.where(keep, acc_a * _shift_down(acc_b, k) + acc_b, acc_b)
            acc_a = jnp.where(keep, acc_a * _shift_down(acc_a, k), acc_a)
            k *= 2
        h0 = jnp.where(i > 0, hcarry[...], 0.0)
        h = acc_b + acc_a * h0
        hcarry[...] = h[ts - 1:ts, :]
        h_ref[...] = h
        ya_ref[...] = _gelu(lg_ref[...])[0] * h

    row = pl.BlockSpec((ts, W_A), lambda i: (i, 0))
    prev8 = pl.BlockSpec((LRU_HALO, W_A), lambda i: (jnp.maximum(i * n8 - 1, 0), 0))
    return _pcall(
        body, (lx, lx, lg, *lru_p), name=name, grid=(s // ts,),
        in_specs=[row, prev8, row] + _lru_pspecs(), out_specs=[row, row],
        out_shape=[SDS((s, W_A), F32), SDS((s, W_A), F32)],
        scratch_shapes=[pltpu.VMEM((1, W_A), F32)], carry=carry)


def _lru_bwd(dya, lx, lg, h_s, lru_p, name, carry=()):
    s = lx.shape[0]
    ts = _time_tile(s)
    n_t = s // ts
    n8 = ts // LRU_HALO

    def body(dya_ref, lx_ref, lxp_ref, lg_ref, h_ref, hp_ref, *rest):
        lru_p = rest[:7]
        (dlx_ref, dlg_ref, dcw_ref, dcb_ref, dwa_ref, dba_ref, dwx_ref, dbx_ref, dlam_ref,
         carry_a, carry_l, carry_dc) = rest[7:]
        cw_ref, _, wa_ref, _, wx_ref, _, lam_ref = lru_p
        i = pl.program_id(0)
        first_tile = i == n_t - 1
        last_tile = i == 0

        @pl.when(i == 0)
        def _():
            for ref in (dcw_ref, dcb_ref, dwa_ref, dba_ref, dwx_ref, dbx_ref, dlam_ref):
                ref[...] = jnp.zeros_like(ref)

        prev = jnp.where(first_tile, 0.0, lxp_ref[...])
        xc = jnp.concatenate([prev, lx_ref[...]], axis=0)
        c, r, ig, sp, a, m = _lru_gates(xc, lru_p)
        h = h_ref[...]
        hcat = jnp.concatenate([jnp.where(first_tile, 0.0, hp_ref[...]), h], axis=0)
        h_m1 = _shift_down(hcat, 1)[LRU_HALO:, :]
        lg = lg_ref[...]
        ge, th = _gelu(lg)
        dya = dya_ref[...]
        dlg_ref[...] = dya * h * _dgelu(lg, th)
        dh = dya * ge
        t = lax.broadcasted_iota(jnp.int32, a.shape, 0)
        a_next = jnp.where(t < ts - 1, _shift_up(a, 1), jnp.where(last_tile, 0.0, carry_a[...]))
        acc_a, acc_b = a_next, dh
        k = 1
        while k < ts:
            keep = t < ts - k
            acc_b = jnp.where(keep, acc_a * _shift_up(acc_b, k) + acc_b, acc_b)
            acc_a = jnp.where(keep, acc_a * _shift_up(acc_a, k), acc_a)
            k *= 2
        lam_beyond = jnp.where(last_tile, 0.0, carry_l[...])
        lmb = acc_b + acc_a * lam_beyond
        carry_a[...] = a[0:1, :]
        carry_l[...] = lmb[0:1, :]
        gi = ig * c
        dgi = lmb * m
        dla = lmb * h_m1 * a - (lmb * gi) * (a * a) / m
        dr = dla * (-LRU_C * sp)
        dsp = jnp.sum(dla * (-LRU_C * r), axis=0, keepdims=True)
        dlam_ref[...] += -dsp * _sigmoid(-lam_ref[...])
        dra = dr * r * (1.0 - r)
        dia = dgi * c * ig * (1.0 - ig)
        dc = dgi * ig + _mm_nt(dra, wa_ref[...]) + _mm_nt(dia, wx_ref[...])
        dwa_ref[...] += _mm_tn(c, dra)
        dwx_ref[...] += _mm_tn(c, dia)
        dba_ref[...] += jnp.sum(dra, axis=0, keepdims=True)
        dbx_ref[...] += jnp.sum(dia, axis=0, keepdims=True)
        dcb_ref[...] += jnp.sum(dc, axis=0, keepdims=True)
        dcc = jnp.concatenate([dc, jnp.where(last_tile, 0.0, carry_dc[...])], axis=0)
        carry_dc[...] = dc[0:LRU_HALO, :]
        dlx = jnp.zeros_like(dc)
        for j in range(LRU_K):
            sh = LRU_K - 1 - j
            dcw_ref[j:j + 1, :] += jnp.sum(dc * _shift_down(xc, sh)[LRU_HALO:, :], axis=0, keepdims=True)
            dlx = dlx + cw_ref[j:j + 1, :] * _shift_up(dcc, sh)[:ts, :]
        dlx_ref[...] = dlx

    row = pl.BlockSpec((ts, W_A), lambda i: (n_t - 1 - i, 0))
    prev8 = pl.BlockSpec((LRU_HALO, W_A), lambda i: (jnp.maximum((n_t - 1 - i) * n8 - 1, 0), 0))
    small = lambda r: pl.BlockSpec((r, W_A), lambda i: (0, 0))
    return _pcall(
        body, (dya, lx, lx, lg, h_s, h_s, *lru_p), name=name, grid=(n_t,),
        in_specs=[row, row, prev8, row, row, prev8] + _lru_pspecs(),
        out_specs=[row, row, small(LRU_K), small(1), small(W_A), small(1), small(W_A), small(1), small(1)],
        out_shape=[SDS((s, W_A), F32), SDS((s, W_A), F32), SDS((LRU_K, W_A), F32), SDS((1, W_A), F32),
                   SDS((W_A, W_A), F32), SDS((1, W_A), F32), SDS((W_A, W_A), F32), SDS((1, W_A), F32),
                   SDS((1, W_A), F32)],
        scratch_shapes=[pltpu.VMEM((1, W_A), F32), pltpu.VMEM((1, W_A), F32), pltpu.VMEM((LRU_HALO, W_A), F32)],
        carry=carry)


_ATT_ROWS = N_Q_HEADS * BLK
_GRP_ROWS = Q_PER_KV * BLK


def _attn_stack(ref, rows, g):
    return jnp.concatenate([ref[rows, h * HEAD_DIM:(h + 1) * HEAD_DIM]
                            for h in range(g * Q_PER_KV, (g + 1) * Q_PER_KV)], axis=0)


def _attn_unstack(parts):
    return jnp.concatenate([p[j * BLK:(j + 1) * BLK, :] for p in parts for j in range(Q_PER_KV)], axis=1)


def _grp(x, g):
    return x[g * _GRP_ROWS:(g + 1) * _GRP_ROWS, :]


def _attn_block(q_ref, k_ref, kp_ref, v_ref, vp_ref, sink_col, i, b):
    rows, prev = slice(b * BLK, (b + 1) * BLK), slice((b - 1) * BLK, b * BLK)
    qs, kcs, kps, vcs, vps = [], [], [], [], []
    for g in range(N_KV_HEADS):
        cols = slice(g * HEAD_DIM, (g + 1) * HEAD_DIM)
        qs.append(_attn_stack(q_ref, rows, g))
        kcs.append(k_ref[rows, cols])
        vcs.append(v_ref[rows, cols])
        kps.append(kp_ref[:, cols] if b == 0 else k_ref[prev, cols])
        vps.append(vp_ref[:, cols] if b == 0 else v_ref[prev, cols])
    scale = 1.0 / math.sqrt(HEAD_DIM)
    sc = jnp.concatenate([_mm_nt(qs[g], kcs[g]) for g in range(N_KV_HEADS)], axis=0) * scale
    sp = jnp.concatenate([_mm_nt(qs[g], kps[g]) for g in range(N_KV_HEADS)], axis=0) * scale
    qi = lax.broadcasted_iota(jnp.int32, (_ATT_ROWS, BLK), 0) & (BLK - 1)
    kj = lax.broadcasted_iota(jnp.int32, (_ATT_ROWS, BLK), 1)
    sc = jnp.where(kj <= qi, sc, NEG_BIG)
    sp = jnp.where((kj > qi) if b > 0 else ((kj > qi) & (i > 0)), sp, NEG_BIG)
    m = jnp.maximum(jnp.maximum(jnp.max(sc, axis=-1, keepdims=True), jnp.max(sp, axis=-1, keepdims=True)), sink_col)
    pc = jnp.exp(sc - m)
    pp = jnp.exp(sp - m)
    es = jnp.exp(sink_col - m)
    inv = 1.0 / (jnp.sum(pc, axis=-1, keepdims=True) + jnp.sum(pp, axis=-1, keepdims=True) + es)
    return qs, kcs, kps, vcs, vps, pc * inv, pp * inv, es * inv


def _attn_specs(s, ts):
    bpt = ts // BLK
    tile = lambda w: pl.BlockSpec((ts, w), lambda i: (i, 0))
    prv = pl.BlockSpec((BLK, KV_W), lambda i: (jnp.maximum(i * bpt - 1, 0), 0))
    sink = pl.BlockSpec((_ATT_ROWS, 1), lambda i: (0, 0))
    return bpt, tile, prv, sink


def _attn_fwd(q, k, v, sink_col, name, carry=()):
    s = q.shape[0]
    ts = _time_tile(s)
    bpt, tile, prv, sink = _attn_specs(s, ts)

    def body(q_ref, k_ref, kp_ref, v_ref, vp_ref, sk_ref, y_ref):
        i = pl.program_id(0)
        for b in range(bpt):
            _, _, _, vcs, vps, pc, pp, _ = _attn_block(q_ref, k_ref, kp_ref, v_ref, vp_ref, sk_ref[...], i, b)
            outs = [_mm(_grp(pc, g), vcs[g]) + _mm(_grp(pp, g), vps[g]) for g in range(N_KV_HEADS)]
            y_ref[b * BLK:(b + 1) * BLK, :] = _attn_unstack(outs)

    return _pcall(
        body, (q, k, k, v, v, sink_col), name=name, grid=(s // ts,),
        in_specs=[tile(W_B), tile(KV_W), prv, tile(KV_W), prv, sink],
        out_specs=[tile(W_B)], out_shape=[SDS((s, W_B), F32)], carry=carry)


def _attn_bwd(dy, q, k, v, sinks, name, carry=()):
    s = q.shape[0]
    ts = _time_tile(s)
    n_t = s // ts
    bpt, tile, prv, sink = _attn_specs(s, ts)

    def body(dy_ref, q_ref, k_ref, kp_ref, v_ref, vp_ref, sk_ref, dq_ref, dk_ref, dv_ref, dku_ref, dvu_ref, dsk_ref):
        i = pl.program_id(0)

        @pl.when(i == 0)
        def _():
            dsk_ref[...] = jnp.zeros_like(dsk_ref)

        scale = 1.0 / math.sqrt(HEAD_DIM)
        groups = range(N_KV_HEADS)
        head_row = lax.broadcasted_iota(jnp.int32, (N_Q_HEADS, BLK), 0)
        dsk = jnp.zeros((N_Q_HEADS, BLK), F32)
        dk_blocks, dv_blocks = [], []
        for b in range(bpt):
            rows = slice(b * BLK, (b + 1) * BLK)
            qs, kcs, kps, vcs, vps, pc, pp, ps = _attn_block(q_ref, k_ref, kp_ref, v_ref, vp_ref, sk_ref[...], i, b)
            dos = [_attn_stack(dy_ref, rows, g) for g in groups]
            dpc = jnp.concatenate([_mm_nt(dos[g], vcs[g]) for g in groups], axis=0)
            dpp = jnp.concatenate([_mm_nt(dos[g], vps[g]) for g in groups], axis=0)
            delta = jnp.sum(pc * dpc, axis=-1, keepdims=True) + jnp.sum(pp * dpp, axis=-1, keepdims=True)
            dsc = pc * (dpc - delta) * scale
            dsp = pp * (dpp - delta) * scale
            dq_ref[rows, :] = _attn_unstack([_mm(_grp(dsc, g), kcs[g]) + _mm(_grp(dsp, g), kps[g]) for g in groups])
            dk_blocks.append(jnp.concatenate([_mm_tn(_grp(dsc, g), qs[g]) for g in groups], axis=1))
            dv_blocks.append(jnp.concatenate([_mm_tn(_grp(pc, g), dos[g]) for g in groups], axis=1))
            dkp = jnp.concatenate([_mm_tn(_grp(dsp, g), qs[g]) for g in groups], axis=1)
            dvp = jnp.concatenate([_mm_tn(_grp(pp, g), dos[g]) for g in groups], axis=1)
            if b == 0:
                dku_ref[...] = dkp
                dvu_ref[...] = dvp
            else:
                dk_blocks[b - 1] = dk_blocks[b - 1] + dkp
                dv_blocks[b - 1] = dv_blocks[b - 1] + dvp
            dsink = -ps * delta
            for h in range(N_Q_HEADS):
                dsk = dsk + jnp.where(head_row == h, jnp.sum(dsink[h * BLK:(h + 1) * BLK, :], axis=0, keepdims=True), 0.0)
        for b in range(bpt):
            dk_ref[b * BLK:(b + 1) * BLK, :] = dk_blocks[b]
            dv_ref[b * BLK:(b + 1) * BLK, :] = dv_blocks[b]
        dsk_ref[...] += dsk

    up = pl.BlockSpec((BLK, KV_W), lambda i: (i, 0))
    return _pcall(
        body, (dy, q, k, k, v, v, sinks), name=name, grid=(n_t,),
        in_specs=[tile(W_B), tile(W_B), tile(KV_W), prv, tile(KV_W), prv, sink],
        out_specs=[tile(W_B), tile(KV_W), tile(KV_W), up, up, pl.BlockSpec((N_Q_HEADS, BLK), lambda i: (0, 0))],
        out_shape=[SDS((s, W_B), F32), SDS((s, KV_W), F32), SDS((s, KV_W), F32), SDS((n_t * BLK, KV_W), F32),
                   SDS((n_t * BLK, KV_W), F32), SDS((N_Q_HEADS, BLK), F32)], carry=carry)


def _cc_recompute(glu_ref, glup_ref, cw_ref, cb_ref, first_tile):
    prev = jnp.where(first_tile, 0.0, glup_ref[...])
    ge = jnp.concatenate([prev, glu_ref[...]], axis=0)
    y0 = ge[:, :W_C] * _sigmoid(ge[:, W_C:])
    y1 = cb_ref[...]
    for j in range(CC_K):
        y1 = y1 + cw_ref[j:j + 1, :] * _shift_down(y0, CC_K - 1 - j)[CC_HALO:, :]
    return y0, y1


def _ln_stats(y1):
    mu = jnp.mean(y1, axis=-1, keepdims=True)
    xc = y1 - mu
    rstd = lax.rsqrt(jnp.mean(xc * xc, axis=-1, keepdims=True) + LN_EPS)
    return xc * rstd, rstd


def _cc_specs(s, ts):
    n32 = ts // CC_HALO
    row = lambda w: pl.BlockSpec((ts, w), lambda i: (i, 0))
    prev = pl.BlockSpec((CC_HALO, 2 * W_C), lambda i: (jnp.maximum(i * n32 - 1, 0), 0))
    small = lambda r: pl.BlockSpec((r, W_C), lambda i: (0, 0))
    return row, prev, small


def _cc_fwd(glu, cw, cb, lng, lnb, name, carry=()):
    s = glu.shape[0]
    ts = _time_tile(s)
    row, prev, small = _cc_specs(s, ts)

    def body(glu_ref, glup_ref, cw_ref, cb_ref, lng_ref, lnb_ref, y_ref):
        _, y1 = _cc_recompute(glu_ref, glup_ref, cw_ref, cb_ref, pl.program_id(0) == 0)
        xhat, _ = _ln_stats(y1)
        z = xhat * lng_ref[...] + lnb_ref[...]
        y_ref[...] = z * _sigmoid(z)

    return _pcall(
        body, (glu, glu, cw, cb, lng, lnb), name=name, grid=(s // ts,),
        in_specs=[row(2 * W_C), prev, small(CC_HALO), small(1), small(1), small(1)],
        out_specs=[row(W_C)], out_shape=[SDS((s, W_C), F32)], carry=carry)


def _cc_bwd_conv(dy, glu, cw, cb, lng, lnb, name, carry=()):
    s = glu.shape[0]
    ts = _time_tile(s)
    row, prev, small = _cc_specs(s, ts)

    def body(dy_ref, glu_ref, glup_ref, cw_ref, cb_ref, lng_ref, lnb_ref, dy1_ref, dcw_ref, dcb_ref, dlng_ref, dlnb_ref):
        i = pl.program_id(0)

        @pl.when(i == 0)
        def _():
            for ref in (dcw_ref, dcb_ref, dlng_ref, dlnb_ref):
                ref[...] = jnp.zeros_like(ref)

        y0, y1 = _cc_recompute(glu_ref, glup_ref, cw_ref, cb_ref, i == 0)
        xhat, rstd = _ln_stats(y1)
        z = xhat * lng_ref[...] + lnb_ref[...]
        dz = dy_ref[...] * _dsilu(z, _sigmoid(z))
        dlng_ref[...] += jnp.sum(dz * xhat, axis=0, keepdims=True)
        dlnb_ref[...] += jnp.sum(dz, axis=0, keepdims=True)
        dxh = dz * lng_ref[...]
        dy1 = rstd * (dxh - jnp.mean(dxh, axis=-1, keepdims=True) - xhat * jnp.mean(dxh * xhat, axis=-1, keepdims=True))
        dy1_ref[...] = dy1
        dcb_ref[...] += jnp.sum(dy1, axis=0, keepdims=True)
        for j in range(CC_K):
            dcw_ref[j:j + 1, :] += jnp.sum(dy1 * _shift_down(y0, CC_K - 1 - j)[CC_HALO:, :], axis=0, keepdims=True)

    return _pcall(
        body, (dy, glu, glu, cw, cb, lng, lnb), name=name, grid=(s // ts,),
        in_specs=[row(W_C), row(2 * W_C), prev, small(CC_HALO), small(1), small(1), small(1)],
        out_specs=[row(W_C), small(CC_HALO), small(1), small(1), small(1)],
        out_shape=[SDS((s, W_C), F32), SDS((CC_HALO, W_C), F32)] + [SDS((1, W_C), F32)] * 3, carry=carry)


def _cc_bwd_glu(dy1, glu, cw, name, carry=()):
    s = glu.shape[0]
    ts = _time_tile(s)
    n_t = s // ts
    n32 = ts // CC_HALO

    def body(dy1_ref, dyn_ref, glu_ref, cw_ref, dglu_ref):
        i = pl.program_id(0)
        dcat = jnp.concatenate([dy1_ref[...], jnp.where(i < n_t - 1, dyn_ref[...], 0.0)], axis=0)
        dy0 = jnp.zeros((ts, W_C), F32)
        for j in range(CC_K):
            dy0 = dy0 + cw_ref[j:j + 1, :] * _shift_up(dcat, CC_K - 1 - j)[:ts, :]
        a = glu_ref[:, :W_C]
        sg = _sigmoid(glu_ref[:, W_C:])
        dglu_ref[...] = jnp.concatenate([dy0 * sg, dy0 * a * sg * (1.0 - sg)], axis=1)

    row = lambda w: pl.BlockSpec((ts, w), lambda i: (i, 0))
    nxt = pl.BlockSpec((CC_HALO, W_C), lambda i: (jnp.minimum((i + 1) * n32, s // CC_HALO - 1), 0))
    return _pcall(
        body, (dy1, dy1, glu, cw), name=name, grid=(n_t,),
        in_specs=[row(W_C), nxt, row(2 * W_C), pl.BlockSpec((CC_HALO, W_C), lambda i: (0, 0))],
        out_specs=[row(2 * W_C)], out_shape=[SDS((s, 2 * W_C), F32)], carry=carry)


_MIX_OFFS = ((0, W_A), (W_A, W_A + W_B), (W_A + W_B, W_A + W_B + W_C))


def _mix_out_fwd(x, ya, yb, yc, group_g, w_out, post_g, name, carry=()):
    s, d = x.shape
    ts = _time_tile(s)
    dm = w_out.shape[0]

    def body(x_ref, ya_ref, yb_ref, yc_ref, gg_ref, w_ref, qg_ref, xo_ref, o_ref):
        parts = []
        for y_ref, (lo, hi) in zip((ya_ref, yb_ref, yc_ref), _MIX_OFFS):
            yv = y_ref[...]
            parts.append(yv * _rms_r(yv) * gg_ref[:, lo:hi])
        o = _mm(jnp.concatenate(parts, axis=1), w_ref[...])
        o_ref[...] = o
        xo_ref[...] = x_ref[...] + o * _rms_r(o) * qg_ref[...]

    row = lambda w: pl.BlockSpec((ts, w), lambda i: (i, 0))
    return _pcall(
        body, (x, ya, yb, yc, group_g, w_out, post_g), name=name, grid=(s // ts,),
        in_specs=[row(d), row(W_A), row(W_B), row(W_C), pl.BlockSpec((1, dm), lambda i: (0, 0)),
                  pl.BlockSpec((dm, d), lambda i: (0, 0)), pl.BlockSpec((1, d), lambda i: (0, 0))],
        out_specs=[row(d), row(d)], out_shape=[SDS((s, d), F32), SDS((s, d), F32)], carry=carry)


def _mix_out_bwd(dxo, o, ya, yb, yc, group_g, w_out, post_g, name, carry=()):
    s, d = o.shape
    ts = _time_tile(s)
    n_t = s // ts
    dm = w_out.shape[0]

    def body(dxo_ref, o_ref, ya_ref, yb_ref, yc_ref, gg_ref, w_ref, qg_ref,
             dya_ref, dyb_ref, dyc_ref, dw_ref, dqg_ref, dgg_ref, acc):
        i = pl.program_id(0)

        @pl.when(i == 0)
        def _():
            acc[...] = jnp.zeros_like(acc)
            dqg_ref[...] = jnp.zeros_like(dqg_ref)
            dgg_ref[...] = jnp.zeros_like(dgg_ref)

        ov = o_ref[...]
        do, dq = _rms_bwd(ov, _rms_r(ov), qg_ref[...], dxo_ref[...])
        dqg_ref[...] += dq
        do = do.astype(BF16)
        dyn = _mm_nt(do, w_ref[...])
        parts, dggs = [], []
        for y_ref, dy_ref, (lo, hi) in zip((ya_ref, yb_ref, yc_ref), (dya_ref, dyb_ref, dyc_ref), _MIX_OFFS):
            yv = y_ref[...]
            r = _rms_r(yv)
            gg = gg_ref[:, lo:hi]
            parts.append(yv * r * gg)
            dyv, dg = _rms_bwd(yv, r, gg, dyn[:, lo:hi])
            dy_ref[...] = dyv
            dggs.append(dg)
        dgg_ref[...] += jnp.concatenate(dggs, axis=1)
        acc[...] += _mm_tn(jnp.concatenate(parts, axis=1), do)

        @pl.when(i == n_t - 1)
        def _():
            dw_ref[...] = acc[...].astype(BF16)

    row = lambda w: pl.BlockSpec((ts, w), lambda i: (i, 0))
    full = pl.BlockSpec((dm, d), lambda i: (0, 0))
    return _pcall(
        body, (dxo, o, ya, yb, yc, group_g, w_out, post_g), name=name, grid=(n_t,),
        in_specs=[row(d), row(d), row(W_A), row(W_B), row(W_C), pl.BlockSpec((1, dm), lambda i: (0, 0)), full,
                  pl.BlockSpec((1, d), lambda i: (0, 0))],
        out_specs=[row(W_A), row(W_B), row(W_C), full, pl.BlockSpec((1, d), lambda i: (0, 0)),
                   pl.BlockSpec((1, dm), lambda i: (0, 0))],
        out_shape=[SDS((s, W_A), F32), SDS((s, W_B), F32), SDS((s, W_C), F32), SDS((dm, d), BF16),
                   SDS((1, d), F32), SDS((1, dm), F32)],
        scratch_shapes=[pltpu.VMEM((dm, d), F32)], carry=carry)


def _loss_head(y, target, name):
    s, d = y.shape
    ts = _time_tile(s)

    def body(y_ref, t_ref, loss_ref, dy_ref):
        @pl.when(pl.program_id(0) == 0)
        def _():
            loss_ref[...] = jnp.zeros_like(loss_ref)

        err = y_ref[...] - t_ref[...]
        dy_ref[...] = err * (1.0 / d)
        per_tok = jnp.mean(err * err, axis=-1, keepdims=True)
        loss_ref[...] += 0.5 * jnp.sum(per_tok, axis=0, keepdims=True)

    row = pl.BlockSpec((ts, d), lambda i: (i, 0))
    return _pcall(body, (y, target), name=name, grid=(s // ts,), in_specs=[row, row],
                  out_specs=[pl.BlockSpec((1, BLK), lambda i: (0, 0)), row],
                  out_shape=[SDS((1, BLK), F32), SDS((s, d), F32)])[0]


def _adamw_math(w, g, m, v):
    m = ADAM_B1 * m + (1.0 - ADAM_B1) * g
    v = ADAM_B2 * v + (1.0 - ADAM_B2) * (g * g)
    m_hat = m / (1.0 - ADAM_B1 ** ADAM_STEP)
    v_hat = v / (1.0 - ADAM_B2 ** ADAM_STEP)
    delta = -ADAM_LR * (m_hat / (jnp.sqrt(v_hat) + ADAM_EPS) + ADAM_WD * w)
    return delta, m, v


def _row_tile(rows, cap=256):
    best = None
    for t in range(16, min(rows, cap) + 1, 16):
        if rows % t == 0:
            best = t
    return best if best is not None else rows


def _reduce_adamw(recv, w, m, v, name):
    n_l, r, c = w.shape
    tr = _row_tile(r)

    def body(recv_ref, w_ref, m_ref, v_ref, g_ref, d_ref, nm_ref, nv_ref):
        g = recv_ref[0].astype(F32)
        for p in range(1, N_DEV):
            g = g + recv_ref[p].astype(F32)
        g_ref[...] = g
        d_ref[...], nm_ref[...], nv_ref[...] = _adamw_math(w_ref[...], g, m_ref[...], v_ref[...])

    blk = pl.BlockSpec((None, tr, c), lambda l, i: (l, i, 0))
    return _pcall(
        body, (recv, w, m, v), name=name, grid=(n_l, r // tr),
        in_specs=[pl.BlockSpec((N_DEV, None, tr, c), lambda l, i: (0, l, i, 0)), blk, blk, blk],
        out_specs=[blk] * 4, out_shape=[SDS(w.shape, F32)] * 4)[0]


def _reduce_adamw_small(parts, w, m, v, name):
    def body(p_ref, w_ref, m_ref, v_ref, g_ref, d_ref, nm_ref, nv_ref):
        g = p_ref[0]
        for p in range(1, N_DEV):
            g = g + p_ref[p]
        g_ref[...] = g
        d_ref[...], nm_ref[...], nv_ref[...] = _adamw_math(w_ref[...], g, m_ref[...], v_ref[...])

    vm = pl.BlockSpec(memory_space=pltpu.VMEM)
    return pl.pallas_call(body, name=name, in_specs=[vm] * 4, out_specs=[vm] * 4, out_shape=[SDS(w.shape, F32)] * 4,
                          compiler_params=pltpu.CompilerParams(vmem_limit_bytes=VMEM_LIMIT))(parts, w, m, v)


def _pack(arrs):
    flat = jnp.concatenate([a.reshape(-1).astype(F32) for a in arrs])
    pad = (-flat.shape[0]) % (8 * BLK)
    return jnp.pad(flat, (0, pad)).reshape(-1, BLK)


def _unpack(packed, shapes):
    flat = packed.reshape(-1)
    out, off = [], 0
    for shp in shapes:
        n = math.prod(shp)
        out.append(flat[off:off + n].reshape(shp))
        off += n
    return out


def _block_diag(w):
    nb, bw, _ = w.shape
    eye = jnp.eye(nb, dtype=w.dtype)
    return (eye[:, None, :, None] * w[:, :, None, :]).reshape(nb * bw, nb * bw)


def _diag_blocks(wd, nb):
    bw = wd.shape[0] // nb
    return jnp.stack([wd[b * bw:(b + 1) * bw, b * bw:(b + 1) * bw] for b in range(nb)])


WEIGHT_NAMES = ['ffn1_pre_g', 'ffn1_w_gu', 'ffn1_w_down', 'ffn1_post_g', 'mix_pre_g', 'w_in', 'lru_conv_w', 'lru_conv_b',
                'lru_w_a', 'lru_b_a', 'lru_w_x', 'lru_b_x', 'lru_lambda', 'attn_sinks', 'conv_w', 'conv_b', 'conv_ln_g',
                'conv_ln_b', 'group_g', 'w_out', 'mix_post_g', 'ffn2_pre_g', 'ffn2_w_gu', 'ffn2_w_down', 'ffn2_post_g']
BIG = ('ffn1_w_gu', 'ffn1_w_down', 'w_in', 'w_out', 'ffn2_w_gu', 'ffn2_w_down')
TRANSPOSED = ('ffn1_w_gu', 'ffn2_w_gu', 'w_in')
SMALL = tuple(k for k in WEIGHT_NAMES if k not in BIG)
CHANNEL_SHARDED = ('lru_conv_w', 'conv_w')


def _step(x, target, w, m, v):
    n_l = w['ffn1_pre_g'].shape[0]
    assert n_l == 2, "the exchange schedule below is laid out for two layers"
    s, d = x.shape[1], x.shape[2]
    x = x.reshape(s, d)
    target = target.reshape(s, d)
    me = _my_pos()[3]
    tview = lambda t, k: jnp.swapaxes(t[k], 1, 2) if k in TRANSPOSED else t[k]
    wb = {k: tview(w, k).astype(BF16) for k in BIG}
    vec = lambda name, l: w[name][l][None, :]

    conv_shard = _pack([w['lru_conv_w'], w['conv_w']])
    g0 = _all_gather([(wb['ffn1_w_gu'], 0), (wb['ffn1_w_down'], 0), (wb['w_in'], 0), (wb['w_out'], 0),
                      (conv_shard, None)], "all_gather_first")
    wts = [dict(), dict()]
    wts[0]['ffn1_w_gu'], wts[0]['ffn1_w_down'], wts[0]['w_in'], wts[0]['w_out'], conv_g = g0
    ch = W_A // N_DEV
    conv_parts = [_unpack(conv_g[p], [(n_l, LRU_K, ch), (n_l, CC_K, ch)]) for p in range(N_DEV)]
    lru_cw = jnp.concatenate([cp[0] for cp in conv_parts], axis=-1)
    cc_cw = jnp.concatenate([cp[1] for cp in conv_parts], axis=-1)
    cc_cw = jnp.pad(cc_cw, ((0, 0), (0, CC_HALO - CC_K), (0, 0)))

    gather_plan = {
        ('ffn1', 0): [('A', 'f2_0', ('ffn2_w_gu', 'ffn2_w_down'), 0)],
        ('mix_in', 0): [('B', 'f2_0')],
        ('lru', 0): [('A', 'wi_1', ('w_in',), 1)],
        ('attn', 0): [('A', 'g1_1', ('ffn1_w_gu',), 1), ('B', 'wi_1')],
        ('cconv', 0): [('B', 'g1_1')],
        ('mix_out', 0): [('A', 'd1_1', ('ffn1_w_down',), 1)],
        ('ffn2', 0): [('B', 'd1_1'), ('A', 'wo_1', ('w_out',), 1)],
        ('ffn1', 1): [('A', 'f2_1', ('ffn2_w_gu', 'ffn2_w_down'), 1), ('B', 'wo_1')],
        ('mix_in', 1): [('B', 'f2_1')],
    }
    pend = {}

    def fwd(kernel_name, l, fn, *args):
        plan = gather_plan.get((kernel_name, l), [])
        carry = []
        for st in plan:
            if st[0] == 'A':
                carry.append(_gather_a([(wb[k], st[3]) for k in st[2]], two_level=True))
            else:
                carry.append(_gather_b(pend[st[1]][2]))
        outs, ex = fn(*args, f"{kernel_name}_fwd_l{l}", carry)
        for st, bufs in zip(plan, ex):
            if st[0] == 'A':
                pend[st[1]] = (st[2], st[3], bufs)
            else:
                names, wl, _ = pend.pop(st[1])
                for k, b in zip(names, bufs):
                    wts[wl][k] = b
        return outs

    saved = []
    h = x
    for l in range(n_l):
        sv = {'x0': h}
        lw = wts[l]
        x1, sv['h1'], sv['g1'], sv['u1'], sv['d1'] = fwd(
            'ffn1', l, _ffn_fwd, h, vec('ffn1_pre_g', l), vec('ffn1_post_g', l), lw['ffn1_w_gu'], lw['ffn1_w_down'])
        sv['x1'] = x1
        sv['hn'], lx, lg, q, k, vv, glu = fwd('mix_in', l, _mix_in_fwd, x1, vec('mix_pre_g', l),
                                              lw['w_in'].reshape(D_IN_PROJ, d))
        sv.update(lx=lx, lg=lg, q=q, k=k, v=vv, glu=glu)
        lru_p = (lru_cw[l], vec('lru_conv_b', l), _block_diag(w['lru_w_a'][l]).astype(BF16), vec('lru_b_a', l),
                 _block_diag(w['lru_w_x'][l]).astype(BF16), vec('lru_b_x', l), vec('lru_lambda', l))
        cc_p = (cc_cw[l], vec('conv_b', l), vec('conv_ln_g', l), vec('conv_ln_b', l))
        sv.update(lru_p=lru_p, cc_p=cc_p)
        sv['ya'], sv['hs'] = fwd('lru', l, _lru_fwd, lx, lg, lru_p)
        sv['sink_col'] = jnp.repeat(w['attn_sinks'][l], BLK)[:, None]
        (sv['yb'],) = fwd('attn', l, _attn_fwd, q, k, vv, sv['sink_col'])
        (sv['yc'],) = fwd('cconv', l, _cc_fwd, glu, *cc_p)
        x2, sv['o'] = fwd('mix_out', l, _mix_out_fwd, x1, sv['ya'], sv['yb'], sv['yc'], vec('group_g', l),
                          lw['w_out'].reshape(-1, d), vec('mix_post_g', l))
        sv['x2'] = x2
        h, sv['h2'], sv['g2'], sv['u2'], sv['d2'] = fwd(
            'ffn2', l, _ffn_fwd, x2, vec('ffn2_pre_g', l), vec('ffn2_post_g', l), lw['ffn2_w_gu'], lw['ffn2_w_down'])
        saved.append(sv)

    loss_row, dh = _loss_head(h, target, "loss_head")
    loss = lax.psum(loss_row[0, 0], AXES)

    recv = {k: None for k in BIG}
    ready = {}
    small = [dict() for _ in range(n_l)]

    def exchange(keys):
        return _grad_x([(ready.pop(key), key[1], recv[key[0]]) for key in keys], n_l)

    def received(keys, bufs):
        for key, b in zip(keys, bufs):
            recv[key[0]] = b

    def run(fn, *args, keys=()):
        outs, ex = fn(*args, carry=[exchange(keys)] if keys else [])
        if keys:
            received(keys, ex[0])
        return outs

    for l in reversed(range(n_l)):
        sv, sg, lw = saved[l], small[l], wts[l]
        keys = [] if l == n_l - 1 else [('ffn1_w_gu', l + 1)]
        dx2, dd, dg, du, sg['ffn2_pre_g'], sg['ffn2_post_g'] = run(
            _ffn_bwd_act, dh, sv['d2'], sv['x2'], vec('ffn2_pre_g', l), vec('ffn2_post_g', l), sv['g2'], sv['u2'],
            lw['ffn2_w_gu'], lw['ffn2_w_down'], f"ffn2_bwd_act_l{l}", keys=keys)
        keys = [] if l == n_l - 1 else [('ffn1_w_down', l + 1)]
        dwg, dwu, dwd = run(_ffn_bwd_w, sv['h2'], dd, sv['g2'], sv['u2'], dg, du, f"ffn2_bwd_w_l{l}", keys=keys)
        ready[('ffn2_w_gu', l)] = [dwg, dwu]
        ready[('ffn2_w_down', l)] = [dwd.reshape(N_DEV, -1, d)]
        dya, dyb, dyc, dw_out, sg['mix_post_g'], sg['group_g'] = run(
            _mix_out_bwd, dx2, sv['o'], sv['ya'], sv['yb'], sv['yc'], vec('group_g', l), lw['w_out'].reshape(-1, d),
            vec('mix_post_g', l), f"mix_out_bwd_l{l}")
        ready[('w_out', l)] = [dw_out.reshape(N_DEV, -1, d)]
        (dlx, dlg, sg['lru_conv_w'], sg['lru_conv_b'], dwa, sg['lru_b_a'], dwx, sg['lru_b_x'],
         sg['lru_lambda']) = run(_lru_bwd, dya, sv['lx'], sv['lg'], sv['hs'], sv['lru_p'], f"lru_bwd_l{l}")
        sg['lru_w_a'] = _diag_blocks(dwa, A_BLOCKS)
        sg['lru_w_x'] = _diag_blocks(dwx, A_BLOCKS)
        dq, dk, dv, dk_up, dv_up, dsk = run(_attn_bwd, dyb, sv['q'], sv['k'], sv['v'], sv['sink_col'],
                                            f"attn_bwd_l{l}", keys=[('ffn2_w_down', l)])
        sg['attn_sinks'] = dsk[:, 0]
        dy1, dcw, sg['conv_b'], sg['conv_ln_g'], sg['conv_ln_b'] = run(
            _cc_bwd_conv, dyc, sv['glu'], *sv['cc_p'], f"cconv_bwd_conv_l{l}")
        sg['conv_w'] = dcw[:CC_K]
        (dglu,) = run(_cc_bwd_glu, dy1, sv['glu'], sv['cc_p'][0], f"cconv_bwd_glu_l{l}")
        dx1, dw_in, sg['mix_pre_g'] = run(
            _mix_in_bwd, dx2, sv['x1'], vec('mix_pre_g', l), sv['hn'], lw['w_in'].reshape(D_IN_PROJ, d),
            dlx, dlg, dq, dk, dk_up, dv, dv_up, dglu, f"mix_in_bwd_l{l}", keys=[('w_out', l)])
        ready[('w_in', l)] = [dw_in.reshape(N_DEV, -1, d)]
        dh, dd, dg, du, sg['ffn1_pre_g'], sg['ffn1_post_g'] = run(
            _ffn_bwd_act, dx1, sv['d1'], sv['x0'], vec('ffn1_pre_g', l), vec('ffn1_post_g', l), sv['g1'], sv['u1'],
            lw['ffn1_w_gu'], lw['ffn1_w_down'], f"ffn1_bwd_act_l{l}", keys=[('ffn2_w_gu', l), ('w_in', l)])
        if l > 0:
            dwg, dwu, dwd = run(_ffn_bwd_w, sv['h1'], dd, sv['g1'], sv['u1'], dg, du, f"ffn1_bwd_w_l{l}")
            ready[('ffn1_w_gu', l)] = [dwg, dwu]
            ready[('ffn1_w_down', l)] = [dwd.reshape(N_DEV, -1, d)]
        else:
            part = _pack([jnp.stack([small[j][k] for j in range(n_l)]) for k in SMALL])
            (recv['ffn1_w_gu'], recv['ffn1_w_down']), ex = _ffn_bwd_w_send(
                sv['h1'], dd, sv['g1'], sv['u1'], dg, du, recv['ffn1_w_gu'], recv['ffn1_w_down'], 0, "ffn1_bwd_w_send_l0",
                [_gather_a([(part, None)], two_level=False)])
            small_parts = ex[0][0]
    grad_x = dh.reshape(1, s, d)

    out = {}
    for k in BIG:
        res = _reduce_adamw(recv[k], tview(w, k), tview(m, k), tview(v, k), f"reduce_adamw_{k}")
        out[k] = [jnp.swapaxes(r, 1, 2) for r in res] if k in TRANSPOSED else res

    small_shapes = [(n_l,) + tuple(small[0][k].shape) for k in SMALL]

    def widen(t, k):
        if k not in CHANNEL_SHARDED:
            return t.reshape((n_l,) + tuple(small[0][k].shape))
        full = jnp.zeros((n_l,) + tuple(small[0][k].shape), F32)
        return lax.dynamic_update_slice_in_dim(full, t, me * ch, axis=2)

    packed = [_pack([widen(src[k], k) for k in SMALL]) for src in (w, m, v)]
    res = _reduce_adamw_small(small_parts, *packed, "reduce_adamw_small")
    for k, g, dlt, nm, nv in zip(SMALL, *[_unpack(r, small_shapes) for r in res]):
        vals = [g, dlt, nm, nv]
        if k in CHANNEL_SHARDED:
            vals = [lax.dynamic_slice_in_dim(t, me * ch, ch, axis=2) for t in vals]
        out[k] = [t.reshape(w[k].shape) for t in vals]

    return (loss, grad_x, *[out[k][0] for k in WEIGHT_NAMES], *[out[k][1] for k in WEIGHT_NAMES],
            *[out[k][2] for k in WEIGHT_NAMES], *[out[k][3] for k in WEIGHT_NAMES])


def kernel(x, ffn1_pre_g, ffn1_w_gu, ffn1_w_down, ffn1_post_g, mix_pre_g, w_in, lru_conv_w, lru_conv_b, lru_w_a, lru_b_a, lru_w_x, lru_b_x, lru_lambda, attn_sinks, conv_w, conv_b, conv_ln_g, conv_ln_b, group_g, w_out, mix_post_g, ffn2_pre_g, ffn2_w_gu, ffn2_w_down, ffn2_post_g, loss_target, m_ffn1_pre_g, m_ffn1_w_gu, m_ffn1_w_down, m_ffn1_post_g, m_mix_pre_g, m_w_in, m_lru_conv_w, m_lru_conv_b, m_lru_w_a, m_lru_b_a, m_lru_w_x, m_lru_b_x, m_lru_lambda, m_attn_sinks, m_conv_w, m_conv_b, m_conv_ln_g, m_conv_ln_b, m_group_g, m_w_out, m_mix_post_g, m_ffn2_pre_g, m_ffn2_w_gu, m_ffn2_w_down, m_ffn2_post_g, v_ffn1_pre_g, v_ffn1_w_gu, v_ffn1_w_down, v_ffn1_post_g, v_mix_pre_g, v_w_in, v_lru_conv_w, v_lru_conv_b, v_lru_w_a, v_lru_b_a, v_lru_w_x, v_lru_b_x, v_lru_lambda, v_attn_sinks, v_conv_w, v_conv_b, v_conv_ln_g, v_conv_ln_b, v_group_g, v_w_out, v_mix_post_g, v_ffn2_pre_g, v_ffn2_w_gu, v_ffn2_w_down, v_ffn2_post_g):
    args = (ffn1_pre_g, ffn1_w_gu, ffn1_w_down, ffn1_post_g, mix_pre_g, w_in, lru_conv_w, lru_conv_b, lru_w_a, lru_b_a, lru_w_x, lru_b_x, lru_lambda, attn_sinks, conv_w, conv_b, conv_ln_g, conv_ln_b, group_g, w_out, mix_post_g, ffn2_pre_g, ffn2_w_gu, ffn2_w_down, ffn2_post_g)
    ms = (m_ffn1_pre_g, m_ffn1_w_gu, m_ffn1_w_down, m_ffn1_post_g, m_mix_pre_g, m_w_in, m_lru_conv_w, m_lru_conv_b, m_lru_w_a, m_lru_b_a, m_lru_w_x, m_lru_b_x, m_lru_lambda, m_attn_sinks, m_conv_w, m_conv_b, m_conv_ln_g, m_conv_ln_b, m_group_g, m_w_out, m_mix_post_g, m_ffn2_pre_g, m_ffn2_w_gu, m_ffn2_w_down, m_ffn2_post_g)
    vs = (v_ffn1_pre_g, v_ffn1_w_gu, v_ffn1_w_down, v_ffn1_post_g, v_mix_pre_g, v_w_in, v_lru_conv_w, v_lru_conv_b, v_lru_w_a, v_lru_b_a, v_lru_w_x, v_lru_b_x, v_lru_lambda, v_attn_sinks, v_conv_w, v_conv_b, v_conv_ln_g, v_conv_ln_b, v_group_g, v_w_out, v_mix_post_g, v_ffn2_pre_g, v_ffn2_w_gu, v_ffn2_w_down, v_ffn2_post_g)
    return _step(x, loss_target, dict(zip(WEIGHT_NAMES, args)), dict(zip(WEIGHT_NAMES, ms)), dict(zip(WEIGHT_NAMES, vs)))
```

```python
import functools
import math
import operator

import jax
import jax.numpy as jnp
from jax import lax
from jax.experimental import pallas as pl
from jax.experimental.pallas import tpu as pltpu

F32 = jnp.float32
BF16 = jnp.bfloat16
N_DEV = 8
AXES = ("x", "y", "c")
MESH = pl.DeviceIdType.MESH

NORM_EPS = 1e-6
LN_EPS = 1e-5
NEG_BIG = -1e30
W_A = 256
W_B = 512
W_C = 256
HEAD_DIM = 64
N_Q_HEADS = 8
N_KV_HEADS = 2
Q_PER_KV = N_Q_HEADS // N_KV_HEADS
KV_W = N_KV_HEADS * HEAD_DIM
BLK = 128
LRU_K = 4
LRU_C = 8.0
A_BLOCKS = 4
CC_K = 31
CC_HALO = 32
LRU_HALO = 8
D_IN_PROJ = 2 * W_A + W_B + 2 * KV_W + 2 * W_C
ADAM_LR = 0.001
ADAM_B1 = 0.9
ADAM_B2 = 0.999
ADAM_EPS = 1e-08
ADAM_WD = 0.01
ADAM_STEP = 10
VMEM_LIMIT = 56 * 1024 * 1024

SDS = jax.ShapeDtypeStruct
ANY = pl.BlockSpec(memory_space=pl.ANY)


def _time_tile(s):
    return max(BLK, s // 8)


class _Exchange:
    def __init__(self, inputs, out_shapes, aliases, sem_shapes, start, wait):
        self.inputs, self.out_shapes, self.aliases, self.sem_shapes = inputs, out_shapes, aliases, sem_shapes
        self.start, self.wait = start, wait


def _my_pos():
    x, y, c = (lax.axis_index(a) for a in AXES)
    return x, y, c, 4 * x + 2 * y + c


def _flip(k):
    x, y, c, _ = _my_pos()
    return (1 - x if k & 4 else x, 1 - y if k & 2 else y, 1 - c if k & 1 else c)


def _slot(dev):
    return 4 * dev[0] + 2 * dev[1] + dev[2]


def _dev(p):
    return (p >> 2, (p >> 1) & 1, p & 1)


def _gather_a(items, two_level):
    rels = (1, 2, 4, 6) if two_level else tuple(range(1, N_DEV))
    n = len(items)
    src_of = lambda ins, a: ins[a] if items[a][1] is None else ins[a].at[items[a][1]]

    def shape_of(a):
        arr, l = items[a]
        return arr.shape if l is None else arr.shape[1:]

    def copies(ins, outs, sems, a):
        send, recv, _ = sems
        me = _my_pos()[3]
        return [(k, pltpu.make_async_remote_copy(
            src_ref=src_of(ins, a), dst_ref=outs[a].at[me], send_sem=send.at[a, k], recv_sem=recv.at[a, k],
            device_id=_flip(k), device_id_type=MESH)) for k in rels]

    def local(ins, outs, sems, a):
        return pltpu.make_async_copy(src_of(ins, a), outs[a].at[_my_pos()[3]], sems[2].at[a])

    def start(ins, outs, sems):
        for a in range(n):
            local(ins, outs, sems, a).start()
            for _, cp in copies(ins, outs, sems, a):
                cp.start()

    def wait(ins, outs, sems):
        send, recv, _ = sems
        for a in range(n):
            for k, cp in copies(ins, outs, sems, a):
                pltpu.make_async_remote_copy(
                    src_ref=src_of(ins, a), dst_ref=outs[a].at[_slot(_flip(k))], send_sem=send.at[a, k],
                    recv_sem=recv.at[a, k], device_id=_flip(k), device_id_type=MESH).wait_recv()
                cp.wait_send()
            local(ins, outs, sems, a).wait()

    return _Exchange([it[0] for it in items], [SDS((N_DEV,) + shape_of(a), items[a][0].dtype) for a in range(n)], {},
                     [pltpu.SemaphoreType.DMA((n, N_DEV)), pltpu.SemaphoreType.DMA((n, N_DEV)),
                      pltpu.SemaphoreType.DMA((n,))], start, wait)


def _gather_b(bufs):
    n = len(bufs)

    def copies(ins, outs, sems, a, c_of_block):
        send, recv = sems
        x, y, c, _ = _my_pos()
        res = []
        for k in (2, 4, 6):
            chip = _flip(k)
            blk = _slot((chip[0], chip[1], c if c_of_block == "mine" else 1 - c))
            res.append(pltpu.make_async_remote_copy(
                src_ref=ins[a].at[blk], dst_ref=outs[a].at[blk], send_sem=send.at[a, k], recv_sem=recv.at[a, k],
                device_id=_flip(1), device_id_type=MESH))
        return res

    def start(ins, outs, sems):
        for a in range(n):
            for cp in copies(ins, outs, sems, a, "mine"):
                cp.start()

    def wait(ins, outs, sems):
        for a in range(n):
            for cp in copies(ins, outs, sems, a, "sibling"):
                cp.wait_recv()
            for cp in copies(ins, outs, sems, a, "mine"):
                cp.wait_send()

    return _Exchange(list(bufs), [SDS(b.shape, b.dtype) for b in bufs], {a: a for a in range(n)},
                     [pltpu.SemaphoreType.DMA((n, N_DEV)), pltpu.SemaphoreType.DMA((n, N_DEV))], start, wait)


def _grad_x(items, n_l):
    n = len(items)
    inputs, first_in, recv_in, aliases, out_shapes = [], [], [], {}, []
    for a, (arrs, l, recv) in enumerate(items):
        first_in.append(len(inputs))
        inputs += list(arrs)
        assert sum(arr.shape[0] for arr in arrs) == N_DEV
        if recv is not None:
            aliases[len(inputs)] = a
            inputs.append(recv)
        out_shapes.append(SDS((N_DEV, n_l) + arrs[0].shape[1:], arrs[0].dtype))

    def slab(ins, a, p):
        off = 0
        for j, arr in enumerate(items[a][0]):
            if p < off + arr.shape[0]:
                return ins[first_in[a] + j].at[p - off]
            off += arr.shape[0]
        raise AssertionError

    def rdma(ins, outs, sems, a, p, src_dev):
        send, recv, _ = sems
        return pltpu.make_async_remote_copy(
            src_ref=slab(ins, a, p), dst_ref=outs[a].at[src_dev, items[a][1]], send_sem=send.at[a, p],
            recv_sem=recv.at[a, src_dev], device_id=_dev(p), device_id_type=MESH)

    def local(ins, outs, sems, a, p):
        return pltpu.make_async_copy(slab(ins, a, p), outs[a].at[p, items[a][1]], sems[2].at[a])

    def start(ins, outs, sems):
        me = _my_pos()[3]
        for p in range(N_DEV):
            @pl.when(me != p)
            def _():
                for a in range(n):
                    rdma(ins, outs, sems, a, p, me).start()

            @pl.when(me == p)
            def _():
                for a in range(n):
                    local(ins, outs, sems, a, p).start()

    def wait(ins, outs, sems):
        me = _my_pos()[3]
        for p in range(N_DEV):
            @pl.when(me != p)
            def _():
                for a in range(n):
                    rdma(ins, outs, sems, a, p, p).wait_recv()
                    rdma(ins, outs, sems, a, p, p).wait_send()

            @pl.when(me == p)
            def _():
                for a in range(n):
                    local(ins, outs, sems, a, p).wait()

    return _Exchange(inputs, out_shapes, aliases,
                     [pltpu.SemaphoreType.DMA((n, N_DEV)), pltpu.SemaphoreType.DMA((n, N_DEV)),
                      pltpu.SemaphoreType.DMA((n,))], start, wait)


def _pcall(body, args, *, name, grid, in_specs, out_specs, out_shape, scratch_shapes=(), carry=(), body_aliases=None):
    n_in, n_out, n_scr = len(in_specs), len(out_specs), len(scratch_shapes)
    c_in = [len(e.inputs) for e in carry]
    c_out = [len(e.out_shapes) for e in carry]
    c_sem = [len(e.sem_shapes) for e in carry]
    aliases = dict(body_aliases or {})
    for j, e in enumerate(carry):
        for i_loc, o_loc in e.aliases.items():
            aliases[n_in + sum(c_in[:j]) + i_loc] = n_out + sum(c_out[:j]) + o_loc

    def wrapped(*refs):
        def take(counts, pos):
            groups = []
            for cnt in counts:
                groups.append(refs[pos:pos + cnt])
                pos += cnt
            return groups, pos

        (ins,), pos = take([n_in], 0)
        cins, pos = take(c_in, pos)
        (outs,), pos = take([n_out], pos)
        couts, pos = take(c_out, pos)
        (scr,), pos = take([n_scr], pos)
        csems, pos = take(c_sem, pos)
        if carry:
            ids = [pl.program_id(k) for k in range(len(grid))]
            first = functools.reduce(operator.and_, [i == 0 for i in ids])
            last = functools.reduce(operator.and_, [i == g - 1 for i, g in zip(ids, grid)])

            @pl.when(first)
            def _():
                for e, ci, co, cs in zip(carry, cins, couts, csems):
                    e.start(ci, co, cs)

        body(*ins, *outs, *scr)
        if carry:
            @pl.when(last)
            def _():
                for e, ci, co, cs in zip(carry, cins, couts, csems):
                    e.wait(ci, co, cs)

    res = pl.pallas_call(
        wrapped, name=name, grid=grid,
        in_specs=list(in_specs) + [ANY] * sum(c_in),
        out_specs=list(out_specs) + [ANY] * sum(c_out),
        out_shape=list(out_shape) + [s for e in carry for s in e.out_shapes],
        scratch_shapes=list(scratch_shapes) + [s for e in carry for s in e.sem_shapes],
        input_output_aliases=aliases,
        compiler_params=pltpu.CompilerParams(dimension_semantics=("arbitrary",) * len(grid),
                                             vmem_limit_bytes=VMEM_LIMIT),
    )(*args, *[a for e in carry for a in e.inputs])
    outs, pos, extra = list(res[:n_out]), n_out, []
    for cnt in c_out:
        extra.append(list(res[pos:pos + cnt]))
        pos += cnt
    return outs, extra


def _all_gather(items, name):
    n = len(items)
    shape_of = lambda a: items[a][0].shape if items[a][1] is None else items[a][0].shape[1:]

    def body(*refs):
        ins, outs, (send_sems, recv_sems, local_sems) = refs[:n], refs[n:2 * n], refs[2 * n:]
        x, y, c, me = _my_pos()
        src_of = lambda a: ins[a] if items[a][1] is None else ins[a].at[items[a][1]]

        def copy(a, k, block, to, src=None):
            dst = outs[a].at[_slot(block)]
            return pltpu.make_async_remote_copy(
                src_ref=dst if src is None else src, dst_ref=dst,
                send_sem=send_sems.at[a, k], recv_sem=recv_sems.at[a, k], device_id=to, device_id_type=MESH)

        mine = [pltpu.make_async_copy(src_of(a), outs[a].at[me], local_sems.at[a]) for a in range(n)]
        for cp in mine:
            cp.start()
        first = [copy(a, k, (x, y, c), _flip(k), src=src_of(a)) for a in range(n) for k in (1, 2, 4, 6)]
        for cp in first:
            cp.start()
        passed = []
        for k in (2, 4, 6):
            for a in range(n):
                copy(a, k, _flip(k), (x, y, c)).wait_recv()
                fwd = copy(a, k + 1, _flip(k), _flip(1))
                fwd.start()
                passed.append(fwd)
        for a in range(n):
            copy(a, 1, _flip(1), (x, y, c)).wait_recv()
            for k in (2, 4, 6):
                copy(a, k + 1, _flip(k + 1), (x, y, c)).wait_recv()
        for cp in first + passed:
            cp.wait_send()
        for cp in mine:
            cp.wait()

    return pl.pallas_call(
        body, name=name,
        in_specs=[ANY] * n, out_specs=[ANY] * n,
        out_shape=[SDS((N_DEV,) + shape_of(a), items[a][0].dtype) for a in range(n)],
        scratch_shapes=[pltpu.SemaphoreType.DMA((n, N_DEV)), pltpu.SemaphoreType.DMA((n, N_DEV)),
                        pltpu.SemaphoreType.DMA((n,))],
    )(*[it[0] for it in items])


def _mm(a, b):
    return jnp.dot(a.astype(BF16), b.astype(BF16), preferred_element_type=F32)


def _mm_nt(a, b):
    return lax.dot_general(a.astype(BF16), b.astype(BF16), (((1,), (1,)), ((), ())), preferred_element_type=F32)


def _mm_tn(a, b):
    return lax.dot_general(a.astype(BF16), b.astype(BF16), (((0,), (0,)), ((), ())), preferred_element_type=F32)


def _rms_r(x):
    return lax.rsqrt(jnp.mean(x * x, axis=-1, keepdims=True) + NORM_EPS)


def _rms_bwd(x, r, g, dy):
    gy = dy * g
    dx = r * (gy - x * (r * r) * jnp.mean(gy * x, axis=-1, keepdims=True))
    dg = jnp.sum(dy * x * r, axis=0, keepdims=True)
    return dx, dg


def _sigmoid(x):
    return 1.0 / (1.0 + jnp.exp(-x))


def _dsilu(z, sz):
    return sz * (1.0 + z * (1.0 - sz))


def _swiglu_bf16(g, u):
    sg = 0.5 * jnp.tanh(0.5 * g) + 0.5
    silu = g * sg
    return silu * u, silu, sg + silu * (1.0 - sg)


_GELU_C = math.sqrt(2.0 / math.pi)


def _gelu(x):
    t = jnp.tanh(_GELU_C * (x + 0.044715 * x * x * x))
    return 0.5 * x * (1.0 + t), t


def _dgelu(x, t):
    return 0.5 * (1.0 + t) + 0.5 * x * (1.0 - t * t) * _GELU_C * (1.0 + 3.0 * 0.044715 * x * x)


def _log1p(e):
    return jnp.where(e < 1e-2, e * (1.0 - e * (0.5 - e * (1.0 / 3.0))), jnp.log(1.0 + e))


def _softplus(x):
    return jnp.maximum(x, 0.0) + _log1p(jnp.exp(-jnp.abs(x)))


def _neg_expm1(x):
    small = -x * (1.0 + x * (0.5 + x * (1.0 / 6.0) * (1.0 + x * 0.25)))
    return jnp.where(x > -1e-2, small, 1.0 - jnp.exp(x))


def _shift_down(x, s):
    return x if s == 0 else pltpu.roll(x, s, 0)


def _shift_up(x, s):
    return x if s == 0 else pltpu.roll(x, x.shape[0] - s, 0)


def _ffn_wspecs(d, fc, order):
    f_of = (lambda i, f: f) if order == "tf" else (lambda f, i: f)
    n_f = N_DEV // 2
    return [pl.BlockSpec((None, fc, d), lambda *g: (f_of(*g), 0, 0)),
            pl.BlockSpec((None, fc, d), lambda *g: (f_of(*g) + n_f, 0, 0)),
            pl.BlockSpec((2, fc // 2, d), lambda *g: (f_of(*g), 0, 0))]


def _ffn_fwd(x, pre_g, post_g, wgu_t, wd, name, carry=()):
    s, d = x.shape
    fc = wgu_t.shape[1]
    ts = 2 * _time_tile(s)
    n_t, n_f = s // ts, N_DEV // 2

    def body(x_ref, pg_ref, qg_ref, wg_ref, wu_ref, wd_ref, xo_ref, h_ref, g_ref, u_ref, d_ref, h_scr, acc):
        f = pl.program_id(1)

        @pl.when(f == 0)
        def _():
            xv = x_ref[...]
            hv = (xv * _rms_r(xv) * pg_ref[...]).astype(BF16)
            h_scr[...] = hv
            h_ref[...] = hv
            acc[...] = jnp.zeros_like(acc)

        hv = h_scr[...]
        g = _mm_nt(hv, wg_ref[...])
        u = _mm_nt(hv, wu_ref[...])
        g = g.astype(BF16)
        u = u.astype(BF16)
        g_ref[...] = g
        u_ref[...] = u
        acc[...] += jnp.dot(_swiglu_bf16(g, u)[0], wd_ref[...].reshape(fc, d), preferred_element_type=F32)

        @pl.when(f == n_f - 1)
        def _():
            dv = acc[...]
            d_ref[...] = dv.astype(BF16)
            xo_ref[...] = x_ref[...] + 0.5 * (dv * _rms_r(dv) * qg_ref[...])

    row = pl.BlockSpec((ts, d), lambda i, f: (i, 0))
    vec = pl.BlockSpec((1, d), lambda i, f: (0, 0))
    act = pl.BlockSpec((None, ts, fc), lambda i, f: (f, i, 0))
    return _pcall(
        body, (x, pre_g, post_g, wgu_t, wgu_t, wd), name=name, grid=(n_t, n_f),
        in_specs=[row, vec, vec] + _ffn_wspecs(d, fc, "tf"),
        out_specs=[row, row, act, act, row],
        out_shape=[SDS((s, d), F32), SDS((s, d), BF16), SDS((n_f, s, fc), BF16), SDS((n_f, s, fc), BF16),
                   SDS((s, d), BF16)],
        scratch_shapes=[pltpu.VMEM((ts, d), BF16), pltpu.VMEM((ts, d), F32)], carry=carry)


def _ffn_bwd_act(dxo, dmid, x, pre_g, post_g, g_s, u_s, wgu_t, wd, name, carry=()):
    s, d = x.shape
    fc = wgu_t.shape[1]
    ts = _time_tile(s)
    n_t, n_f = s // ts, N_DEV // 2

    def body(dxo_ref, dm_ref, x_ref, pg_ref, qg_ref, g_ref, u_ref, wg_ref, wu_ref, wd_ref,
             dx_ref, dd_ref, dg_ref, du_ref, dpg_ref, dqg_ref, dd_scr, dh_acc):
        i, f = pl.program_id(0), pl.program_id(1)

        @pl.when((i == 0) & (f == 0))
        def _():
            dpg_ref[...] = jnp.zeros_like(dpg_ref)
            dqg_ref[...] = jnp.zeros_like(dqg_ref)

        @pl.when(f == 0)
        def _():
            dv = dm_ref[...].astype(F32)
            ddv, dq = _rms_bwd(dv, _rms_r(dv), qg_ref[...], 0.5 * dxo_ref[...])
            dqg_ref[...] += dq
            dd_scr[...] = ddv.astype(BF16)
            dd_ref[...] = ddv.astype(BF16)
            dh_acc[...] = jnp.zeros_like(dh_acc)

        da = _mm_nt(dd_scr[...], wd_ref[...].reshape(fc, d)).astype(BF16)
        u = u_ref[...]
        _, silu, dsilu = _swiglu_bf16(g_ref[...], u)
        du = da * silu
        dg = da * u * dsilu
        dg_ref[...] = dg
        du_ref[...] = du
        dh_acc[...] += _mm(dg, wg_ref[...]) + _mm(du, wu_ref[...])

        @pl.when(f == n_f - 1)
        def _():
            xv = x_ref[...]
            dxv, dp = _rms_bwd(xv, _rms_r(xv), pg_ref[...], dh_acc[...])
            dpg_ref[...] += dp
            dx_ref[...] = dxo_ref[...] + dxv

    row = pl.BlockSpec((ts, d), lambda i, f: (i, 0))
    vec = pl.BlockSpec((1, d), lambda i, f: (0, 0))
    act = pl.BlockSpec((None, ts, fc), lambda i, f: (f, i, 0))
    return _pcall(
        body, (dxo, dmid, x, pre_g, post_g, g_s, u_s, wgu_t, wgu_t, wd), name=name, grid=(n_t, n_f),
        in_specs=[row, row, row, vec, vec, act, act] + _ffn_wspecs(d, fc, "tf"),
        out_specs=[row, row, act, act, vec, vec],
        out_shape=[SDS((s, d), F32), SDS((s, d), BF16), SDS((n_f, s, fc), BF16), SDS((n_f, s, fc), BF16),
                   SDS((1, d), F32), SDS((1, d), F32)],
        scratch_shapes=[pltpu.VMEM((ts, d), BF16), pltpu.VMEM((ts, d), F32)], carry=carry)


def _ffn_bwd_w(h, dd, g_s, u_s, dg, du, name, carry=()):
    s, d = h.shape
    n_f, _, fc = g_s.shape
    ts = _time_tile(s)
    n_t = s // ts

    def body(h_ref, dd_ref, g_ref, u_ref, dg_ref, du_ref, wg_ref, wu_ref, wd_ref, acc_g, acc_u, acc_d):
        i = pl.program_id(1)

        @pl.when(i == 0)
        def _():
            acc_g[...] = jnp.zeros_like(acc_g)
            acc_u[...] = jnp.zeros_like(acc_u)
            acc_d[...] = jnp.zeros_like(acc_d)

        a = _swiglu_bf16(g_ref[...], u_ref[...])[0]
        hv = h_ref[...]
        acc_g[...] += _mm_tn(dg_ref[...], hv)
        acc_u[...] += _mm_tn(du_ref[...], hv)
        acc_d[...] += _mm_tn(a, dd_ref[...])

        @pl.when(i == n_t - 1)
        def _():
            wg_ref[...] = acc_g[...].astype(BF16)
            wu_ref[...] = acc_u[...].astype(BF16)
            wd_ref[...] = acc_d[...].astype(BF16)

    row = pl.BlockSpec((ts, d), lambda f, i: (i, 0))
    act = pl.BlockSpec((None, ts, fc), lambda f, i: (f, i, 0))
    out = pl.BlockSpec((None, fc, d), lambda f, i: (f, 0, 0))
    return _pcall(
        body, (h, dd, g_s, u_s, dg, du), name=name, grid=(n_f, n_t),
        in_specs=[row, row, act, act, act, act], out_specs=[out, out, out],
        out_shape=[SDS((n_f, fc, d), BF16)] * 3,
        scratch_shapes=[pltpu.VMEM((fc, d), F32)] * 3, carry=carry)


def _ffn_bwd_w_send(h, dd, g_s, u_s, dg, du, recv_gu, recv_d, layer, name, carry=()):
    s, d = h.shape
    n_f, _, fc = g_s.shape
    ts = _time_tile(s)
    n_t = s // ts
    half = fc // 2

    def body(h_ref, dd_ref, g_ref, u_ref, dg_ref, du_ref, _rgu_in, _rd_in, rgu_ref, rd_ref,
             acc_g, acc_u, acc_d, st_g, st_u, st_d, send_sems, recv_sems, local_sems):
        f, i = pl.program_id(0), pl.program_id(1)
        me = _my_pos()[3]

        @pl.when(i == 0)
        def _():
            acc_g[...] = jnp.zeros_like(acc_g)
            acc_u[...] = jnp.zeros_like(acc_u)
            acc_d[...] = jnp.zeros_like(acc_d)

        a = _swiglu_bf16(g_ref[...], u_ref[...])[0]
        hv = h_ref[...]
        acc_g[...] += _mm_tn(dg_ref[...], hv)
        acc_u[...] += _mm_tn(du_ref[...], hv)
        acc_d[...] += _mm_tn(a, dd_ref[...])

        def messages(fs):
            return [(st_g.at[fs], rgu_ref, 0, fs, 0), (st_u.at[fs], rgu_ref, 0, fs + n_f, 1),
                    (st_d.at[fs, pl.ds(0, half)], rd_ref, 1, 2 * fs, 2),
                    (st_d.at[fs, pl.ds(half, half)], rd_ref, 1, 2 * fs + 1, 3)]

        def remote(fs, msg, src_dev):
            src, buf, row, p, j = msg
            return pltpu.make_async_remote_copy(
                src_ref=src, dst_ref=buf.at[src_dev, layer], send_sem=send_sems.at[fs, j],
                recv_sem=recv_sems.at[row, src_dev], device_id=_dev(p), device_id_type=MESH)

        def local(fs, msg):
            src, buf, _, p, j = msg
            return pltpu.make_async_copy(src, buf.at[p, layer], local_sems.at[fs, j])

        for fs in range(n_f):
            @pl.when((f == fs) & (i == n_t - 1))
            def _():
                st_g[fs] = acc_g[...].astype(BF16)
                st_u[fs] = acc_u[...].astype(BF16)
                st_d[fs] = acc_d[...].astype(BF16)
                for msg in messages(fs):
                    @pl.when(me != msg[3])
                    def _():
                        remote(fs, msg, me).start()

                    @pl.when(me == msg[3])
                    def _():
                        local(fs, msg).start()

        @pl.when((f == n_f - 1) & (i == n_t - 1))
        def _():
            for fs in range(n_f):
                for msg in messages(fs):
                    @pl.when(me != msg[3])
                    def _():
                        remote(fs, msg, me).wait_send()

                    @pl.when(me == msg[3])
                    def _():
                        local(fs, msg).wait()
            for src_dev in range(N_DEV):
                @pl.when(me != src_dev)
                def _():
                    remote(0, messages(0)[0], src_dev).wait_recv()
                    remote(0, messages(0)[2], src_dev).wait_recv()

    row = pl.BlockSpec((ts, d), lambda f, i: (i, 0))
    act = pl.BlockSpec((None, ts, fc), lambda f, i: (f, i, 0))
    return _pcall(
        body, (h, dd, g_s, u_s, dg, du, recv_gu, recv_d), name=name, grid=(n_f, n_t),
        in_specs=[row, row, act, act, act, act, ANY, ANY], out_specs=[ANY, ANY],
        out_shape=[SDS(recv_gu.shape, recv_gu.dtype), SDS(recv_d.shape, recv_d.dtype)],
        scratch_shapes=[pltpu.VMEM((fc, d), F32)] * 3 + [pltpu.VMEM((n_f, fc, d), BF16)] * 3
        + [pltpu.SemaphoreType.DMA((n_f, 4)), pltpu.SemaphoreType.DMA((2, N_DEV)), pltpu.SemaphoreType.DMA((n_f, 4))],
        carry=carry, body_aliases={6: 0, 7: 1})


_PROJ_WIDTHS = (W_A, W_A, W_B, KV_W, KV_W, 2 * W_C)


def _mix_in_fwd(x, pre_g, w_in_t, name, carry=()):
    s, d = x.shape
    ts = _time_tile(s)

    def body(x_ref, pg_ref, w_ref, hn_ref, *outs):
        xv = x_ref[...]
        hn = (xv * _rms_r(xv) * pg_ref[...]).astype(BF16)
        hn_ref[...] = hn
        proj = _mm_nt(hn, w_ref[...])
        off = 0
        for o_ref, w in zip(outs, _PROJ_WIDTHS):
            o_ref[...] = proj[:, off:off + w]
            off += w

    row = lambda w: pl.BlockSpec((ts, w), lambda i: (i, 0))
    return _pcall(
        body, (x, pre_g, w_in_t), name=name, grid=(s // ts,),
        in_specs=[row(d), pl.BlockSpec((1, d), lambda i: (0, 0)), pl.BlockSpec((D_IN_PROJ, d), lambda i: (0, 0))],
        out_specs=[row(d)] + [row(w) for w in _PROJ_WIDTHS],
        out_shape=[SDS((s, d), BF16)] + [SDS((s, w), F32) for w in _PROJ_WIDTHS], carry=carry)


def _mix_in_bwd(dres, x, pre_g, hn, w_in_t, dlx, dlg, dq, dk, dk_up, dv, dv_up, dglu, name, carry=()):
    s, d = x.shape
    ts = _time_tile(s)
    n_t = s // ts

    def body(dres_ref, x_ref, pg_ref, hn_ref, w_ref, dlx_ref, dlg_ref, dq_ref, dk_ref, dkn_ref,
             dv_ref, dvn_ref, dglu_ref, dx_ref, dw_ref, dpg_ref, acc):
        i = pl.program_id(0)

        @pl.when(i == 0)
        def _():
            acc[...] = jnp.zeros_like(acc)
            dpg_ref[...] = jnp.zeros_like(dpg_ref)

        def with_next(cur_ref, nxt_ref):
            nxt = jnp.where(i < n_t - 1, nxt_ref[...], 0.0)
            if ts == BLK:
                return cur_ref[...] + nxt
            return jnp.concatenate([cur_ref[:ts - BLK, :], cur_ref[ts - BLK:, :] + nxt], axis=0)

        dproj = jnp.concatenate([dlx_ref[...], dlg_ref[...], dq_ref[...], with_next(dk_ref, dkn_ref),
                                 with_next(dv_ref, dvn_ref), dglu_ref[...]], axis=1).astype(BF16)
        dhn = _mm(dproj, w_ref[...])
        acc[...] += _mm_tn(dproj, hn_ref[...])
        xv = x_ref[...]
        dxv, dp = _rms_bwd(xv, _rms_r(xv), pg_ref[...], dhn)
        dpg_ref[...] += dp
        dx_ref[...] = dres_ref[...] + dxv

        @pl.when(i == n_t - 1)
        def _():
            dw_ref[...] = acc[...].astype(BF16)

    row = lambda w: pl.BlockSpec((ts, w), lambda i: (i, 0))
    nxt = pl.BlockSpec((BLK, KV_W), lambda i: (jnp.minimum(i + 1, n_t - 1), 0))
    vec = pl.BlockSpec((1, d), lambda i: (0, 0))
    full = pl.BlockSpec((D_IN_PROJ, d), lambda i: (0, 0))
    return _pcall(
        body, (dres, x, pre_g, hn, w_in_t, dlx, dlg, dq, dk, dk_up, dv, dv_up, dglu), name=name, grid=(n_t,),
        in_specs=[row(d), row(d), vec, row(d), full, row(W_A), row(W_A), row(W_B), row(KV_W), nxt,
                  row(KV_W), nxt, row(2 * W_C)],
        out_specs=[row(d), full, vec],
        out_shape=[SDS((s, d), F32), SDS((D_IN_PROJ, d), BF16), SDS((1, d), F32)],
        scratch_shapes=[pltpu.VMEM((D_IN_PROJ, d), F32)], carry=carry)


def _lru_gates(xc, lru_p):
    cw_ref, cb_ref, wa_ref, ba_ref, wx_ref, bx_ref, lam_ref = lru_p
    c = cb_ref[...]
    for j in range(LRU_K):
        c = c + cw_ref[j:j + 1, :] * _shift_down(xc, LRU_K - 1 - j)[LRU_HALO:, :]
    r = _sigmoid(_mm(c, wa_ref[...]) + ba_ref[...])
    ig = _sigmoid(_mm(c, wx_ref[...]) + bx_ref[...])
    sp = _softplus(-lam_ref[...])
    log_a = -LRU_C * r * sp
    a = jnp.exp(log_a)
    m = jnp.sqrt(_neg_expm1(2.0 * log_a))
    return c, r, ig, sp, a, m


def _lru_pspecs():
    small = lambda r: pl.BlockSpec((r, W_A), lambda i: (0, 0))
    return [small(LRU_K), small(1), small(W_A), small(1), small(W_A), small(1), small(1)]


def _lru_fwd(lx, lg, lru_p, name, carry=()):
    s = lx.shape[0]
    ts = _time_tile(s)
    n8 = ts // LRU_HALO

    def body(lx_ref, lxp_ref, lg_ref, *rest):
        lru_p, (ya_ref, h_ref, hcarry) = rest[:7], rest[7:]
        i = pl.program_id(0)
        prev = jnp.where(i > 0, lxp_ref[...], 0.0)
        xc = jnp.concatenate([prev, lx_ref[...]], axis=0)
        c, r, ig, sp, a, m = _lru_gates(xc, lru_p)
        acc_a, acc_b = a, m * (ig * c)
        t = lax.broadcasted_iota(jnp.int32, a.shape, 0)
        k = 1
        while k < ts:
            keep = t >= k
            acc_b = jnp.where(keep, acc_a * _shift_down(acc_b, k) + acc_b, acc_b)
            acc_a = jnp.where(keep, acc_a * _shift_down(acc_a, k), acc_a)
            k *= 2
        h0 = jnp.where(i > 0, hcarry[...], 0.0)
        h = acc_b + acc_a * h0
        hcarry[...] = h[ts - 1:ts, :]
        h_ref[...] = h
        ya_ref[...] = _gelu(lg_ref[...])[0] * h

    row = pl.BlockSpec((ts, W_A), lambda i: (i, 0))
    prev8 = pl.BlockSpec((LRU_HALO, W_A), lambda i: (jnp.maximum(i * n8 - 1, 0), 0))
    return _pcall(
        body, (lx, lx, lg, *lru_p), name=name, grid=(s // ts,),
        in_specs=[row, prev8, row] + _lru_pspecs(), out_specs=[row, row],
        out_shape=[SDS((s, W_A), F32), SDS((s, W_A), F32)],
        scratch_shapes=[pltpu.VMEM((1, W_A), F32)], carry=carry)


def _lru_bwd(dya, lx, lg, h_s, lru_p, name, carry=()):
    s = lx.shape[0]
    ts = _time_tile(s)
    n_t = s // ts
    n8 = ts // LRU_HALO

    def body(dya_ref, lx_ref, lxp_ref, lg_ref, h_ref, hp_ref, *rest):
        lru_p = rest[:7]
        (dlx_ref, dlg_ref, dcw_ref, dcb_ref, dwa_ref, dba_ref, dwx_ref, dbx_ref, dlam_ref,
         carry_a, carry_l, carry_dc) = rest[7:]
        cw_ref, _, wa_ref, _, wx_ref, _, lam_ref = lru_p
        i = pl.program_id(0)
        first_tile = i == n_t - 1
        last_tile = i == 0

        @pl.when(i == 0)
        def _():
            for ref in (dcw_ref, dcb_ref, dwa_ref, dba_ref, dwx_ref, dbx_ref, dlam_ref):
                ref[...] = jnp.zeros_like(ref)

        prev = jnp.where(first_tile, 0.0, lxp_ref[...])
        xc = jnp.concatenate([prev, lx_ref[...]], axis=0)
        c, r, ig, sp, a, m = _lru_gates(xc, lru_p)
        h = h_ref[...]
        hcat = jnp.concatenate([jnp.where(first_tile, 0.0, hp_ref[...]), h], axis=0)
        h_m1 = _shift_down(hcat, 1)[LRU_HALO:, :]
        lg = lg_ref[...]
        ge, th = _gelu(lg)
        dya = dya_ref[...]
        dlg_ref[...] = dya * h * _dgelu(lg, th)
        dh = dya * ge
        t = lax.broadcasted_iota(jnp.int32, a.shape, 0)
        a_next = jnp.where(t < ts - 1, _shift_up(a, 1), jnp.where(last_tile, 0.0, carry_a[...]))
        acc_a, acc_b = a_next, dh
        k = 1
        while k < ts:
            keep = t < ts - k
            acc_b = jnp.where(keep, acc_a * _shift_up(acc_b, k) + acc_b, acc_b)
            acc_a = jnp.where(keep, acc_a * _shift_up(acc_a, k), acc_a)
            k *= 2
        lam_beyond = jnp.where(last_tile, 0.0, carry_l[...])
        lmb = acc_b + acc_a * lam_beyond
        carry_a[...] = a[0:1, :]
        carry_l[...] = lmb[0:1, :]
        gi = ig * c
        dgi = lmb * m
        dla = lmb * h_m1 * a - (lmb * gi) * (a * a) / m
        dr = dla * (-LRU_C * sp)
        dsp = jnp.sum(dla * (-LRU_C * r), axis=0, keepdims=True)
        dlam_ref[...] += -dsp * _sigmoid(-lam_ref[...])
        dra = dr * r * (1.0 - r)
        dia = dgi * c * ig * (1.0 - ig)
        dc = dgi * ig + _mm_nt(dra, wa_ref[...]) + _mm_nt(dia, wx_ref[...])
        dwa_ref[...] += _mm_tn(c, dra)
        dwx_ref[...] += _mm_tn(c, dia)
        dba_ref[...] += jnp.sum(dra, axis=0, keepdims=True)
        dbx_ref[...] += jnp.sum(dia, axis=0, keepdims=True)
        dcb_ref[...] += jnp.sum(dc, axis=0, keepdims=True)
        dcc = jnp.concatenate([dc, jnp.where(last_tile, 0.0, carry_dc[...])], axis=0)
        carry_dc[...] = dc[0:LRU_HALO, :]
        dlx = jnp.zeros_like(dc)
        for j in range(LRU_K):
            sh = LRU_K - 1 - j
            dcw_ref[j:j + 1, :] += jnp.sum(dc * _shift_down(xc, sh)[LRU_HALO:, :], axis=0, keepdims=True)
            dlx = dlx + cw_ref[j:j + 1, :] * _shift_up(dcc, sh)[:ts, :]
        dlx_ref[...] = dlx

    row = pl.BlockSpec((ts, W_A), lambda i: (n_t - 1 - i, 0))
    prev8 = pl.BlockSpec((LRU_HALO, W_A), lambda i: (jnp.maximum((n_t - 1 - i) * n8 - 1, 0), 0))
    small = lambda r: pl.BlockSpec((r, W_A), lambda i: (0, 0))
    return _pcall(
        body, (dya, lx, lx, lg, h_s, h_s, *lru_p), name=name, grid=(n_t,),
        in_specs=[row, row, prev8, row, row, prev8] + _lru_pspecs(),
        out_specs=[row, row, small(LRU_K), small(1), small(W_A), small(1), small(W_A), small(1), small(1)],
        out_shape=[SDS((s, W_A), F32), SDS((s, W_A), F32), SDS((LRU_K, W_A), F32), SDS((1, W_A), F32),
                   SDS((W_A, W_A), F32), SDS((1, W_A), F32), SDS((W_A, W_A), F32), SDS((1, W_A), F32),
                   SDS((1, W_A), F32)],
        scratch_shapes=[pltpu.VMEM((1, W_A), F32), pltpu.VMEM((1, W_A), F32), pltpu.VMEM((LRU_HALO, W_A), F32)],
        carry=carry)


_ATT_ROWS = N_Q_HEADS * BLK
_GRP_ROWS = Q_PER_KV * BLK


def _attn_stack(ref, rows, g):
    return jnp.concatenate([ref[rows, h * HEAD_DIM:(h + 1) * HEAD_DIM]
                            for h in range(g * Q_PER_KV, (g + 1) * Q_PER_KV)], axis=0)


def _attn_unstack(parts):
    return jnp.concatenate([p[j * BLK:(j + 1) * BLK, :] for p in parts for j in range(Q_PER_KV)], axis=1)


def _grp(x, g):
    return x[g * _GRP_ROWS:(g + 1) * _GRP_ROWS, :]


def _attn_block(q_ref, k_ref, kp_ref, v_ref, vp_ref, sink_col, i, b):
    rows, prev = slice(b * BLK, (b + 1) * BLK), slice((b - 1) * BLK, b * BLK)
    qs, kcs, kps, vcs, vps = [], [], [], [], []
    for g in range(N_KV_HEADS):
        cols = slice(g * HEAD_DIM, (g + 1) * HEAD_DIM)
        qs.append(_attn_stack(q_ref, rows, g))
        kcs.append(k_ref[rows, cols])
        vcs.append(v_ref[rows, cols])
        kps.append(kp_ref[:, cols] if b == 0 else k_ref[prev, cols])
        vps.append(vp_ref[:, cols] if b == 0 else v_ref[prev, cols])
    scale = 1.0 / math.sqrt(HEAD_DIM)
    sc = jnp.concatenate([_mm_nt(qs[g], kcs[g]) for g in range(N_KV_HEADS)], axis=0) * scale
    sp = jnp.concatenate([_mm_nt(qs[g], kps[g]) for g in range(N_KV_HEADS)], axis=0) * scale
    qi = lax.broadcasted_iota(jnp.int32, (_ATT_ROWS, BLK), 0) & (BLK - 1)
    kj = lax.broadcasted_iota(jnp.int32, (_ATT_ROWS, BLK), 1)
    sc = jnp.where(kj <= qi, sc, NEG_BIG)
    sp = jnp.where((kj > qi) if b > 0 else ((kj > qi) & (i > 0)), sp, NEG_BIG)
    m = jnp.maximum(jnp.maximum(jnp.max(sc, axis=-1, keepdims=True), jnp.max(sp, axis=-1, keepdims=True)), sink_col)
    pc = jnp.exp(sc - m)
    pp = jnp.exp(sp - m)
    es = jnp.exp(sink_col - m)
    inv = 1.0 / (jnp.sum(pc, axis=-1, keepdims=True) + jnp.sum(pp, axis=-1, keepdims=True) + es)
    return qs, kcs, kps, vcs, vps, pc * inv, pp * inv, es * inv


def _attn_specs(s, ts):
    bpt = ts // BLK
    tile = lambda w: pl.BlockSpec((ts, w), lambda i: (i, 0))
    prv = pl.BlockSpec((BLK, KV_W), lambda i: (jnp.maximum(i * bpt - 1, 0), 0))
    sink = pl.BlockSpec((_ATT_ROWS, 1), lambda i: (0, 0))
    return bpt, tile, prv, sink


def _attn_fwd(q, k, v, sink_col, name, carry=()):
    s = q.shape[0]
    ts = _time_tile(s)
    bpt, tile, prv, sink = _attn_specs(s, ts)

    def body(q_ref, k_ref, kp_ref, v_ref, vp_ref, sk_ref, y_ref):
        i = pl.program_id(0)
        for b in range(bpt):
            _, _, _, vcs, vps, pc, pp, _ = _attn_block(q_ref, k_ref, kp_ref, v_ref, vp_ref, sk_ref[...], i, b)
            outs = [_mm(_grp(pc, g), vcs[g]) + _mm(_grp(pp, g), vps[g]) for g in range(N_KV_HEADS)]
            y_ref[b * BLK:(b + 1) * BLK, :] = _attn_unstack(outs)

    return _pcall(
        body, (q, k, k, v, v, sink_col), name=name, grid=(s // ts,),
        in_specs=[tile(W_B), tile(KV_W), prv, tile(KV_W), prv, sink],
        out_specs=[tile(W_B)], out_shape=[SDS((s, W_B), F32)], carry=carry)


def _attn_bwd(dy, q, k, v, sinks, name, carry=()):
    s = q.shape[0]
    ts = _time_tile(s)
    n_t = s // ts
    bpt, tile, prv, sink = _attn_specs(s, ts)

    def body(dy_ref, q_ref, k_ref, kp_ref, v_ref, vp_ref, sk_ref, dq_ref, dk_ref, dv_ref, dku_ref, dvu_ref, dsk_ref):
        i = pl.program_id(0)

        @pl.when(i == 0)
        def _():
            dsk_ref[...] = jnp.zeros_like(dsk_ref)

        scale = 1.0 / math.sqrt(HEAD_DIM)
        groups = range(N_KV_HEADS)
        head_row = lax.broadcasted_iota(jnp.int32, (N_Q_HEADS, BLK), 0)
        dsk = jnp.zeros((N_Q_HEADS, BLK), F32)
        dk_blocks, dv_blocks = [], []
        for b in range(bpt):
            rows = slice(b * BLK, (b + 1) * BLK)
            qs, kcs, kps, vcs, vps, pc, pp, ps = _attn_block(q_ref, k_ref, kp_ref, v_ref, vp_ref, sk_ref[...], i, b)
            dos = [_attn_stack(dy_ref, rows, g) for g in groups]
            dpc = jnp.concatenate([_mm_nt(dos[g], vcs[g]) for g in groups], axis=0)
            dpp = jnp.concatenate([_mm_nt(dos[g], vps[g]) for g in groups], axis=0)
            delta = jnp.sum(pc * dpc, axis=-1, keepdims=True) + jnp.sum(pp * dpp, axis=-1, keepdims=True)
            dsc = pc * (dpc - delta) * scale
            dsp = pp * (dpp - delta) * scale
            dq_ref[rows, :] = _attn_unstack([_mm(_grp(dsc, g), kcs[g]) + _mm(_grp(dsp, g), kps[g]) for g in groups])
            dk_blocks.append(jnp.concatenate([_mm_tn(_grp(dsc, g), qs[g]) for g in groups], axis=1))
            dv_blocks.append(jnp.concatenate([_mm_tn(_grp(pc, g), dos[g]) for g in groups], axis=1))
            dkp = jnp.concatenate([_mm_tn(_grp(dsp, g), qs[g]) for g in groups], axis=1)
            dvp = jnp.concatenate([_mm_tn(_grp(pp, g), dos[g]) for g in groups], axis=1)
            if b == 0:
                dku_ref[...] = dkp
                dvu_ref[...] = dvp
            else:
                dk_blocks[b - 1] = dk_blocks[b - 1] + dkp
                dv_blocks[b - 1] = dv_blocks[b - 1] + dvp
            dsink = -ps * delta
            for h in range(N_Q_HEADS):
                dsk = dsk + jnp.where(head_row == h, jnp.sum(dsink[h * BLK:(h + 1) * BLK, :], axis=0, keepdims=True), 0.0)
        for b in range(bpt):
            dk_ref[b * BLK:(b + 1) * BLK, :] = dk_blocks[b]
            dv_ref[b * BLK:(b + 1) * BLK, :] = dv_blocks[b]
        dsk_ref[...] += dsk

    up = pl.BlockSpec((BLK, KV_W), lambda i: (i, 0))
    return _pcall(
        body, (dy, q, k, k, v, v, sinks), name=name, grid=(n_t,),
        in_specs=[tile(W_B), tile(W_B), tile(KV_W), prv, tile(KV_W), prv, sink],
        out_specs=[tile(W_B), tile(KV_W), tile(KV_W), up, up, pl.BlockSpec((N_Q_HEADS, BLK), lambda i: (0, 0))],
        out_shape=[SDS((s, W_B), F32), SDS((s, KV_W), F32), SDS((s, KV_W), F32), SDS((n_t * BLK, KV_W), F32),
                   SDS((n_t * BLK, KV_W), F32), SDS((N_Q_HEADS, BLK), F32)], carry=carry)


def _cc_recompute(glu_ref, glup_ref, cw_ref, cb_ref, first_tile):
    prev = jnp.where(first_tile, 0.0, glup_ref[...])
    ge = jnp.concatenate([prev, glu_ref[...]], axis=0)
    y0 = ge[:, :W_C] * _sigmoid(ge[:, W_C:])
    y1 = cb_ref[...]
    for j in range(CC_K):
        y1 = y1 + cw_ref[j:j + 1, :] * _shift_down(y0, CC_K - 1 - j)[CC_HALO:, :]
    return y0, y1


def _ln_stats(y1):
    mu = jnp.mean(y1, axis=-1, keepdims=True)
    xc = y1 - mu
    rstd = lax.rsqrt(jnp.mean(xc * xc, axis=-1, keepdims=True) + LN_EPS)
    return xc * rstd, rstd


def _cc_specs(s, ts):
    n32 = ts // CC_HALO
    row = lambda w: pl.BlockSpec((ts, w), lambda i: (i, 0))
    prev = pl.BlockSpec((CC_HALO, 2 * W_C), lambda i: (jnp.maximum(i * n32 - 1, 0), 0))
    small = lambda r: pl.BlockSpec((r, W_C), lambda i: (0, 0))
    return row, prev, small


def _cc_fwd(glu, cw, cb, lng, lnb, name, carry=()):
    s = glu.shape[0]
    ts = _time_tile(s)
    row, prev, small = _cc_specs(s, ts)

    def body(glu_ref, glup_ref, cw_ref, cb_ref, lng_ref, lnb_ref, y_ref):
        _, y1 = _cc_recompute(glu_ref, glup_ref, cw_ref, cb_ref, pl.program_id(0) == 0)
        xhat, _ = _ln_stats(y1)
        z = xhat * lng_ref[...] + lnb_ref[...]
        y_ref[...] = z * _sigmoid(z)

    return _pcall(
        body, (glu, glu, cw, cb, lng, lnb), name=name, grid=(s // ts,),
        in_specs=[row(2 * W_C), prev, small(CC_HALO), small(1), small(1), small(1)],
        out_specs=[row(W_C)], out_shape=[SDS((s, W_C), F32)], carry=carry)


def _cc_bwd_conv(dy, glu, cw, cb, lng, lnb, name, carry=()):
    s = glu.shape[0]
    ts = _time_tile(s)
    row, prev, small = _cc_specs(s, ts)

    def body(dy_ref, glu_ref, glup_ref, cw_ref, cb_ref, lng_ref, lnb_ref, dy1_ref, dcw_ref, dcb_ref, dlng_ref, dlnb_ref):
        i = pl.program_id(0)

        @pl.when(i == 0)
        def _():
            for ref in (dcw_ref, dcb_ref, dlng_ref, dlnb_ref):
                ref[...] = jnp.zeros_like(ref)

        y0, y1 = _cc_recompute(glu_ref, glup_ref, cw_ref, cb_ref, i == 0)
        xhat, rstd = _ln_stats(y1)
        z = xhat * lng_ref[...] + lnb_ref[...]
        dz = dy_ref[...] * _dsilu(z, _sigmoid(z))
        dlng_ref[...] += jnp.sum(dz * xhat, axis=0, keepdims=True)
        dlnb_ref[...] += jnp.sum(dz, axis=0, keepdims=True)
        dxh = dz * lng_ref[...]
        dy1 = rstd * (dxh - jnp.mean(dxh, axis=-1, keepdims=True) - xhat * jnp.mean(dxh * xhat, axis=-1, keepdims=True))
        dy1_ref[...] = dy1
        dcb_ref[...] += jnp.sum(dy1, axis=0, keepdims=True)
        for j in range(CC_K):
            dcw_ref[j:j + 1, :] += jnp.sum(dy1 * _shift_down(y0, CC_K - 1 - j)[CC_HALO:, :], axis=0, keepdims=True)

    return _pcall(
        body, (dy, glu, glu, cw, cb, lng, lnb), name=name, grid=(s // ts,),
        in_specs=[row(W_C), row(2 * W_C), prev, small(CC_HALO), small(1), small(1), small(1)],
        out_specs=[row(W_C), small(CC_HALO), small(1), small(1), small(1)],
        out_shape=[SDS((s, W_C), F32), SDS((CC_HALO, W_C), F32)] + [SDS((1, W_C), F32)] * 3, carry=carry)


def _cc_bwd_glu(dy1, glu, cw, name, carry=()):
    s = glu.shape[0]
    ts = _time_tile(s)
    n_t = s // ts
    n32 = ts // CC_HALO

    def body(dy1_ref, dyn_ref, glu_ref, cw_ref, dglu_ref):
        i = pl.program_id(0)
        dcat = jnp.concatenate([dy1_ref[...], jnp.where(i < n_t - 1, dyn_ref[...], 0.0)], axis=0)
        dy0 = jnp.zeros((ts, W_C), F32)
        for j in range(CC_K):
            dy0 = dy0 + cw_ref[j:j + 1, :] * _shift_up(dcat, CC_K - 1 - j)[:ts, :]
        a = glu_ref[:, :W_C]
        sg = _sigmoid(glu_ref[:, W_C:])
        dglu_ref[...] = jnp.concatenate([dy0 * sg, dy0 * a * sg * (1.0 - sg)], axis=1)

    row = lambda w: pl.BlockSpec((ts, w), lambda i: (i, 0))
    nxt = pl.BlockSpec((CC_HALO, W_C), lambda i: (jnp.minimum((i + 1) * n32, s // CC_HALO - 1), 0))
    return _pcall(
        body, (dy1, dy1, glu, cw), name=name, grid=(n_t,),
        in_specs=[row(W_C), nxt, row(2 * W_C), pl.BlockSpec((CC_HALO, W_C), lambda i: (0, 0))],
        out_specs=[row(2 * W_C)], out_shape=[SDS((s, 2 * W_C), F32)], carry=carry)


_MIX_OFFS = ((0, W_A), (W_A, W_A + W_B), (W_A + W_B, W_A + W_B + W_C))


def _mix_out_fwd(x, ya, yb, yc, group_g, w_out, post_g, name, carry=()):
    s, d = x.shape
    ts = _time_tile(s)
    dm = w_out.shape[0]

    def body(x_ref, ya_ref, yb_ref, yc_ref, gg_ref, w_ref, qg_ref, xo_ref, o_ref):
        parts = []
        for y_ref, (lo, hi) in zip((ya_ref, yb_ref, yc_ref), _MIX_OFFS):
            yv = y_ref[...]
            parts.append(yv * _rms_r(yv) * gg_ref[:, lo:hi])
        o = _mm(jnp.concatenate(parts, axis=1), w_ref[...])
        o_ref[...] = o
        xo_ref[...] = x_ref[...] + o * _rms_r(o) * qg_ref[...]

    row = lambda w: pl.BlockSpec((ts, w), lambda i: (i, 0))
    return _pcall(
        body, (x, ya, yb, yc, group_g, w_out, post_g), name=name, grid=(s // ts,),
        in_specs=[row(d), row(W_A), row(W_B), row(W_C), pl.BlockSpec((1, dm), lambda i: (0, 0)),
                  pl.BlockSpec((dm, d), lambda i: (0, 0)), pl.BlockSpec((1, d), lambda i: (0, 0))],
        out_specs=[row(d), row(d)], out_shape=[SDS((s, d), F32), SDS((s, d), F32)], carry=carry)


def _mix_out_bwd(dxo, o, ya, yb, yc, group_g, w_out, post_g, name, carry=()):
    s, d = o.shape
    ts = _time_tile(s)
    n_t = s // ts
    dm = w_out.shape[0]

    def body(dxo_ref, o_ref, ya_ref, yb_ref, yc_ref, gg_ref, w_ref, qg_ref,
             dya_ref, dyb_ref, dyc_ref, dw_ref, dqg_ref, dgg_ref, acc):
        i = pl.program_id(0)

        @pl.when(i == 0)
        def _():
            acc[...] = jnp.zeros_like(acc)
            dqg_ref[...] = jnp.zeros_like(dqg_ref)
            dgg_ref[...] = jnp.zeros_like(dgg_ref)

        ov = o_ref[...]
        do, dq = _rms_bwd(ov, _rms_r(ov), qg_ref[...], dxo_ref[...])
        dqg_ref[...] += dq
        do = do.astype(BF16)
        dyn = _mm_nt(do, w_ref[...])
        parts, dggs = [], []
        for y_ref, dy_ref, (lo, hi) in zip((ya_ref, yb_ref, yc_ref), (dya_ref, dyb_ref, dyc_ref), _MIX_OFFS):
            yv = y_ref[...]
            r = _rms_r(yv)
            gg = gg_ref[:, lo:hi]
            parts.append(yv * r * gg)
            dyv, dg = _rms_bwd(yv, r, gg, dyn[:, lo:hi])
            dy_ref[...] = dyv
            dggs.append(dg)
        dgg_ref[...] += jnp.concatenate(dggs, axis=1)
        acc[...] += _mm_tn(jnp.concatenate(parts, axis=1), do)

        @pl.when(i == n_t - 1)
        def _():
            dw_ref[...] = acc[...].astype(BF16)

    row = lambda w: pl.BlockSpec((ts, w), lambda i: (i, 0))
    full = pl.BlockSpec((dm, d), lambda i: (0, 0))
    return _pcall(
        body, (dxo, o, ya, yb, yc, group_g, w_out, post_g), name=name, grid=(n_t,),
        in_specs=[row(d), row(d), row(W_A), row(W_B), row(W_C), pl.BlockSpec((1, dm), lambda i: (0, 0)), full,
                  pl.BlockSpec((1, d), lambda i: (0, 0))],
        out_specs=[row(W_A), row(W_B), row(W_C), full, pl.BlockSpec((1, d), lambda i: (0, 0)),
                   pl.BlockSpec((1, dm), lambda i: (0, 0))],
        out_shape=[SDS((s, W_A), F32), SDS((s, W_B), F32), SDS((s, W_C), F32), SDS((dm, d), BF16),
                   SDS((1, d), F32), SDS((1, dm), F32)],
        scratch_shapes=[pltpu.VMEM((dm, d), F32)], carry=carry)


def _loss_head(y, target, name):
    s, d = y.shape
    ts = _time_tile(s)

    def body(y_ref, t_ref, loss_ref, dy_ref):
        @pl.when(pl.program_id(0) == 0)
        def _():
            loss_ref[...] = jnp.zeros_like(loss_ref)

        err = y_ref[...] - t_ref[...]
        dy_ref[...] = err * (1.0 / d)
        per_tok = jnp.mean(err * err, axis=-1, keepdims=True)
        loss_ref[...] += 0.5 * jnp.sum(per_tok, axis=0, keepdims=True)

    row = pl.BlockSpec((ts, d), lambda i: (i, 0))
    return _pcall(body, (y, target), name=name, grid=(s // ts,), in_specs=[row, row],
                  out_specs=[pl.BlockSpec((1, BLK), lambda i: (0, 0)), row],
                  out_shape=[SDS((1, BLK), F32), SDS((s, d), F32)])[0]


def _adamw_math(w, g, m, v):
    m = ADAM_B1 * m + (1.0 - ADAM_B1) * g
    v = ADAM_B2 * v + (1.0 - ADAM_B2) * (g * g)
    m_hat = m / (1.0 - ADAM_B1 ** ADAM_STEP)
    v_hat = v / (1.0 - ADAM_B2 ** ADAM_STEP)
    delta = -ADAM_LR * (m_hat / (jnp.sqrt(v_hat) + ADAM_EPS) + ADAM_WD * w)
    return delta, m, v


def _row_tile(rows, cap=256):
    best = None
    for t in range(16, min(rows, cap) + 1, 16):
        if rows % t == 0:
            best = t
    return best if best is not None else rows


def _reduce_adamw(recv, w, m, v, name):
    n_l, r, c = w.shape
    tr = _row_tile(r)

    def body(recv_ref, w_ref, m_ref, v_ref, g_ref, d_ref, nm_ref, nv_ref):
        g = recv_ref[0].astype(F32)
        for p in range(1, N_DEV):
            g = g + recv_ref[p].astype(F32)
        g_ref[...] = g
        d_ref[...], nm_ref[...], nv_ref[...] = _adamw_math(w_ref[...], g, m_ref[...], v_ref[...])

    blk = pl.BlockSpec((None, tr, c), lambda l, i: (l, i, 0))
    return _pcall(
        body, (recv, w, m, v), name=name, grid=(n_l, r // tr),
        in_specs=[pl.BlockSpec((N_DEV, None, tr, c), lambda l, i: (0, l, i, 0)), blk, blk, blk],
        out_specs=[blk] * 4, out_shape=[SDS(w.shape, F32)] * 4)[0]


def _reduce_adamw_small(parts, w, m, v, name):
    def body(p_ref, w_ref, m_ref, v_ref, g_ref, d_ref, nm_ref, nv_ref):
        g = p_ref[0]
        for p in range(1, N_DEV):
            g = g + p_ref[p]
        g_ref[...] = g
        d_ref[...], nm_ref[...], nv_ref[...] = _adamw_math(w_ref[...], g, m_ref[...], v_ref[...])

    vm = pl.BlockSpec(memory_space=pltpu.VMEM)
    return pl.pallas_call(body, name=name, in_specs=[vm] * 4, out_specs=[vm] * 4, out_shape=[SDS(w.shape, F32)] * 4,
                          compiler_params=pltpu.CompilerParams(vmem_limit_bytes=VMEM_LIMIT))(parts, w, m, v)


def _rows_of(shape):
    return -(-math.prod(shape) // (8 * BLK)) * 8


def _pack(arrs):
    rows = []
    for a in arrs:
        flat = a.reshape(-1).astype(F32)
        rows.append(jnp.pad(flat, (0, _rows_of(a.shape) * BLK - flat.shape[0])).reshape(-1, BLK))
    return jnp.concatenate(rows, axis=0)


def _unpack(packed, shapes):
    out, row = [], 0
    for shp in shapes:
        n, r = math.prod(shp), _rows_of(shp)
        out.append(packed[row:row + r].reshape(-1)[:n].reshape(shp))
        row += r
    return out


def _block_diag(w):
    nb, bw, _ = w.shape
    eye = jnp.eye(nb, dtype=w.dtype)
    return (eye[:, None, :, None] * w[:, :, None, :]).reshape(nb * bw, nb * bw)


def _diag_blocks(wd, nb):
    bw = wd.shape[0] // nb
    return jnp.stack([wd[b * bw:(b + 1) * bw, b * bw:(b + 1) * bw] for b in range(nb)])


WEIGHT_NAMES = ['ffn1_pre_g', 'ffn1_w_gu', 'ffn1_w_down', 'ffn1_post_g', 'mix_pre_g', 'w_in', 'lru_conv_w', 'lru_conv_b',
                'lru_w_a', 'lru_b_a', 'lru_w_x', 'lru_b_x', 'lru_lambda', 'attn_sinks', 'conv_w', 'conv_b', 'conv_ln_g',
                'conv_ln_b', 'group_g', 'w_out', 'mix_post_g', 'ffn2_pre_g', 'ffn2_w_gu', 'ffn2_w_down', 'ffn2_post_g']
BIG = ('ffn1_w_gu', 'ffn1_w_down', 'w_in', 'w_out', 'ffn2_w_gu', 'ffn2_w_down')
TRANSPOSED = ('ffn1_w_gu', 'ffn2_w_gu', 'w_in')
SMALL = tuple(k for k in WEIGHT_NAMES if k not in BIG)
CHANNEL_SHARDED = ('lru_conv_w', 'conv_w')


def _step(x, target, w, m, v):
    n_l = w['ffn1_pre_g'].shape[0]
    assert n_l == 2, "the exchange schedule below is laid out for two layers"
    s, d = x.shape[1], x.shape[2]
    x = x.reshape(s, d)
    target = target.reshape(s, d)
    me = _my_pos()[3]
    tview = lambda t, k: jnp.swapaxes(t[k], 1, 2) if k in TRANSPOSED else t[k]
    wb = {k: tview(w, k).astype(BF16) for k in BIG}
    vec = lambda name, l: w[name][l][None, :]

    conv_shard = _pack([w['lru_conv_w'], w['conv_w']])
    g0 = _all_gather([(wb['ffn1_w_gu'], 0), (wb['ffn1_w_down'], 0), (wb['w_in'], 0), (wb['w_out'], 0),
                      (conv_shard, None)], "all_gather_first")
    wts = [dict(), dict()]
    wts[0]['ffn1_w_gu'], wts[0]['ffn1_w_down'], wts[0]['w_in'], wts[0]['w_out'], conv_g = g0
    ch = W_A // N_DEV
    conv_parts = [_unpack(conv_g[p], [(n_l, LRU_K, ch), (n_l, CC_K, ch)]) for p in range(N_DEV)]
    lru_cw = jnp.concatenate([cp[0] for cp in conv_parts], axis=-1)
    cc_cw = jnp.concatenate([cp[1] for cp in conv_parts], axis=-1)
    cc_cw = jnp.pad(cc_cw, ((0, 0), (0, CC_HALO - CC_K), (0, 0)))

    gather_plan = {
        ('ffn1', 0): [('A', 'f2_0', ('ffn2_w_gu', 'ffn2_w_down'), 0)],
        ('mix_in', 0): [('B', 'f2_0')],
        ('attn', 0): [('A', 'g1_1', ('ffn1_w_gu',), 1)],
        ('cconv', 0): [('B', 'g1_1')],
        ('ffn2', 0): [('D', None, ('ffn1_w_down',), 1), ('A', 'wi_1', ('w_in',), 1), ('A', 'wo_1', ('w_out',), 1)],
        ('ffn1', 1): [('A', 'f2_1', ('ffn2_w_gu', 'ffn2_w_down'), 1), ('B', 'wi_1'), ('B', 'wo_1')],
        ('mix_in', 1): [('B', 'f2_1')],
    }
    pend = {}

    def fwd(kernel_name, l, fn, *args):
        plan = gather_plan.get((kernel_name, l), [])
        carry = []
        for st in plan:
            if st[0] == 'B':
                carry.append(_gather_b(pend[st[1]][2]))
            else:
                carry.append(_gather_a([(wb[k], st[3]) for k in st[2]], two_level=st[0] == 'A'))
        outs, ex = fn(*args, f"{kernel_name}_fwd_l{l}", carry)
        for st, bufs in zip(plan, ex):
            if st[0] == 'A':
                pend[st[1]] = (st[2], st[3], bufs)
            else:
                names, wl = (st[2], st[3]) if st[0] == 'D' else pend.pop(st[1])[:2]
                for k, b in zip(names, bufs):
                    wts[wl][k] = b
        return outs

    saved = []
    h = x
    for l in range(n_l):
        sv = {'x0': h}
        lw = wts[l]
        x1, sv['h1'], sv['g1'], sv['u1'], sv['d1'] = fwd(
            'ffn1', l, _ffn_fwd, h, vec('ffn1_pre_g', l), vec('ffn1_post_g', l), lw['ffn1_w_gu'], lw['ffn1_w_down'])
        sv['x1'] = x1
        sv['hn'], lx, lg, q, k, vv, glu = fwd('mix_in', l, _mix_in_fwd, x1, vec('mix_pre_g', l),
                                              lw['w_in'].reshape(D_IN_PROJ, d))
        sv.update(lx=lx, lg=lg, q=q, k=k, v=vv, glu=glu)
        lru_p = (lru_cw[l], vec('lru_conv_b', l), _block_diag(w['lru_w_a'][l]).astype(BF16), vec('lru_b_a', l),
                 _block_diag(w['lru_w_x'][l]).astype(BF16), vec('lru_b_x', l), vec('lru_lambda', l))
        cc_p = (cc_cw[l], vec('conv_b', l), vec('conv_ln_g', l), vec('conv_ln_b', l))
        sv.update(lru_p=lru_p, cc_p=cc_p)
        sv['ya'], sv['hs'] = fwd('lru', l, _lru_fwd, lx, lg, lru_p)
        sv['sink_col'] = jnp.repeat(w['attn_sinks'][l], BLK)[:, None]
        (sv['yb'],) = fwd('attn', l, _attn_fwd, q, k, vv, sv['sink_col'])
        (sv['yc'],) = fwd('cconv', l, _cc_fwd, glu, *cc_p)
        x2, sv['o'] = fwd('mix_out', l, _mix_out_fwd, x1, sv['ya'], sv['yb'], sv['yc'], vec('group_g', l),
                          lw['w_out'].reshape(-1, d), vec('mix_post_g', l))
        sv['x2'] = x2
        h, sv['h2'], sv['g2'], sv['u2'], sv['d2'] = fwd(
            'ffn2', l, _ffn_fwd, x2, vec('ffn2_pre_g', l), vec('ffn2_post_g', l), lw['ffn2_w_gu'], lw['ffn2_w_down'])
        saved.append(sv)

    loss_row, dh = _loss_head(h, target, "loss_head")

    recv = {k: None for k in BIG}
    ready = {}
    small = [dict() for _ in range(n_l)]

    def exchange(keys):
        return _grad_x([(ready.pop(key), key[1], recv[key[0]]) for key in keys], n_l)

    def received(keys, bufs):
        for key, b in zip(keys, bufs):
            recv[key[0]] = b

    def run(fn, *args, keys=()):
        outs, ex = fn(*args, carry=[exchange(keys)] if keys else [])
        if keys:
            received(keys, ex[0])
        return outs

    for l in reversed(range(n_l)):
        sv, sg, lw = saved[l], small[l], wts[l]
        keys = [] if l == n_l - 1 else [('ffn1_w_gu', l + 1)]
        dx2, dd, dg, du, sg['ffn2_pre_g'], sg['ffn2_post_g'] = run(
            _ffn_bwd_act, dh, sv['d2'], sv['x2'], vec('ffn2_pre_g', l), vec('ffn2_post_g', l), sv['g2'], sv['u2'],
            lw['ffn2_w_gu'], lw['ffn2_w_down'], f"ffn2_bwd_act_l{l}", keys=keys)
        keys = [] if l == n_l - 1 else [('ffn1_w_down', l + 1)]
        dwg, dwu, dwd = run(_ffn_bwd_w, sv['h2'], dd, sv['g2'], sv['u2'], dg, du, f"ffn2_bwd_w_l{l}", keys=keys)
        ready[('ffn2_w_gu', l)] = [dwg, dwu]
        ready[('ffn2_w_down', l)] = [dwd.reshape(N_DEV, -1, d)]
        dya, dyb, dyc, dw_out, sg['mix_post_g'], sg['group_g'] = run(
            _mix_out_bwd, dx2, sv['o'], sv['ya'], sv['yb'], sv['yc'], vec('group_g', l), lw['w_out'].reshape(-1, d),
            vec('mix_post_g', l), f"mix_out_bwd_l{l}")
        ready[('w_out', l)] = [dw_out.reshape(N_DEV, -1, d)]
        (dlx, dlg, sg['lru_conv_w'], sg['lru_conv_b'], dwa, sg['lru_b_a'], dwx, sg['lru_b_x'],
         sg['lru_lambda']) = run(_lru_bwd, dya, sv['lx'], sv['lg'], sv['hs'], sv['lru_p'], f"lru_bwd_l{l}")
        sg['lru_w_a'] = _diag_blocks(dwa, A_BLOCKS)
        sg['lru_w_x'] = _diag_blocks(dwx, A_BLOCKS)
        dq, dk, dv, dk_up, dv_up, dsk = run(_attn_bwd, dyb, sv['q'], sv['k'], sv['v'], sv['sink_col'],
                                            f"attn_bwd_l{l}", keys=[('ffn2_w_down', l)])
        sg['attn_sinks'] = dsk[:, 0]
        dy1, dcw, sg['conv_b'], sg['conv_ln_g'], sg['conv_ln_b'] = run(
            _cc_bwd_conv, dyc, sv['glu'], *sv['cc_p'], f"cconv_bwd_conv_l{l}")
        sg['conv_w'] = dcw[:CC_K]
        (dglu,) = run(_cc_bwd_glu, dy1, sv['glu'], sv['cc_p'][0], f"cconv_bwd_glu_l{l}")
        dx1, dw_in, sg['mix_pre_g'] = run(
            _mix_in_bwd, dx2, sv['x1'], vec('mix_pre_g', l), sv['hn'], lw['w_in'].reshape(D_IN_PROJ, d),
            dlx, dlg, dq, dk, dk_up, dv, dv_up, dglu, f"mix_in_bwd_l{l}", keys=[('w_out', l)])
        ready[('w_in', l)] = [dw_in.reshape(N_DEV, -1, d)]
        dh, dd, dg, du, sg['ffn1_pre_g'], sg['ffn1_post_g'] = run(
            _ffn_bwd_act, dx1, sv['d1'], sv['x0'], vec('ffn1_pre_g', l), vec('ffn1_post_g', l), sv['g1'], sv['u1'],
            lw['ffn1_w_gu'], lw['ffn1_w_down'], f"ffn1_bwd_act_l{l}", keys=[('ffn2_w_gu', l), ('w_in', l)])
        if l > 0:
            dwg, dwu, dwd = run(_ffn_bwd_w, sv['h1'], dd, sv['g1'], sv['u1'], dg, du, f"ffn1_bwd_w_l{l}")
            ready[('ffn1_w_gu', l)] = [dwg, dwu]
            ready[('ffn1_w_down', l)] = [dwd.reshape(N_DEV, -1, d)]
        else:
            part = _pack([jnp.stack([small[j][k] for j in range(n_l)]) for k in SMALL] + [loss_row])
            (recv['ffn1_w_gu'], recv['ffn1_w_down']), ex = _ffn_bwd_w_send(
                sv['h1'], dd, sv['g1'], sv['u1'], dg, du, recv['ffn1_w_gu'], recv['ffn1_w_down'], 0, "ffn1_bwd_w_send_l0",
                [_gather_a([(part, None)], two_level=False)])
            small_parts = ex[0][0]
    grad_x = dh.reshape(1, s, d)

    out = {}
    for k in BIG:
        res = _reduce_adamw(recv[k], tview(w, k), tview(m, k), tview(v, k), f"reduce_adamw_{k}")
        out[k] = [jnp.swapaxes(r, 1, 2) for r in res] if k in TRANSPOSED else res

    small_shapes = [(n_l,) + tuple(small[0][k].shape) for k in SMALL]

    def widen(t, k):
        if k not in CHANNEL_SHARDED:
            return t.reshape((n_l,) + tuple(small[0][k].shape))
        full = jnp.zeros((n_l,) + tuple(small[0][k].shape), F32)
        return lax.dynamic_update_slice_in_dim(full, t, me * ch, axis=2)

    no_w = jnp.zeros(loss_row.shape, F32)
    packed = [_pack([widen(src[k], k) for k in SMALL] + [no_w]) for src in (w, m, v)]
    res = _reduce_adamw_small(small_parts, *packed, "reduce_adamw_small")
    loss = _unpack(res[0], small_shapes + [loss_row.shape])[-1][0, 0]
    for k, g, dlt, nm, nv in zip(SMALL, *[_unpack(r, small_shapes) for r in res]):
        vals = [g, dlt, nm, nv]
        if k in CHANNEL_SHARDED:
            vals = [lax.dynamic_slice_in_dim(t, me * ch, ch, axis=2) for t in vals]
        out[k] = [t.reshape(w[k].shape) for t in vals]

    return (loss, grad_x, *[out[k][0] for k in WEIGHT_NAMES], *[out[k][1] for k in WEIGHT_NAMES],
            *[out[k][2] for k in WEIGHT_NAMES], *[out[k][3] for k in WEIGHT_NAMES])


def kernel(x, ffn1_pre_g, ffn1_w_gu, ffn1_w_down, ffn1_post_g, mix_pre_g, w_in, lru_conv_w, lru_conv_b, lru_w_a, lru_b_a, lru_w_x, lru_b_x, lru_lambda, attn_sinks, conv_w, conv_b, conv_ln_g, conv_ln_b, group_g, w_out, mix_post_g, ffn2_pre_g, ffn2_w_gu, ffn2_w_down, ffn2_post_g, loss_target, m_ffn1_pre_g, m_ffn1_w_gu, m_ffn1_w_down, m_ffn1_post_g, m_mix_pre_g, m_w_in, m_lru_conv_w, m_lru_conv_b, m_lru_w_a, m_lru_b_a, m_lru_w_x, m_lru_b_x, m_lru_lambda, m_attn_sinks, m_conv_w, m_conv_b, m_conv_ln_g, m_conv_ln_b, m_group_g, m_w_out, m_mix_post_g, m_ffn2_pre_g, m_ffn2_w_gu, m_ffn2_w_down, m_ffn2_post_g, v_ffn1_pre_g, v_ffn1_w_gu, v_ffn1_w_down, v_ffn1_post_g, v_mix_pre_g, v_w_in, v_lru_conv_w, v_lru_conv_b, v_lru_w_a, v_lru_b_a, v_lru_w_x, v_lru_b_x, v_lru_lambda, v_attn_sinks, v_conv_w, v_conv_b, v_conv_ln_g, v_conv_ln_b, v_group_g, v_w_out, v_mix_post_g, v_ffn2_pre_g, v_ffn2_w_gu, v_ffn2_w_down, v_ffn2_post_g):
    args = (ffn1_pre_g, ffn1_w_gu, ffn1_w_down, ffn1_post_g, mix_pre_g, w_in, lru_conv_w, lru_conv_b, lru_w_a, lru_b_a, lru_w_x, lru_b_x, lru_lambda, attn_sinks, conv_w, conv_b, conv_ln_g, conv_ln_b, group_g, w_out, mix_post_g, ffn2_pre_g, ffn2_w_gu, ffn2_w_down, ffn2_post_g)
    ms = (m_ffn1_pre_g, m_ffn1_w_gu, m_ffn1_w_down, m_ffn1_post_g, m_mix_pre_g, m_w_in, m_lru_conv_w, m_lru_conv_b, m_lru_w_a, m_lru_b_a, m_lru_w_x, m_lru_b_x, m_lru_lambda, m_attn_sinks, m_conv_w, m_conv_b, m_conv_ln_g, m_conv_ln_b, m_group_g, m_w_out, m_mix_post_g, m_ffn2_pre_g, m_ffn2_w_gu, m_ffn2_w_down, m_ffn2_post_g)
    vs = (v_ffn1_pre_g, v_ffn1_w_gu, v_ffn1_w_down, v_ffn1_post_g, v_mix_pre_g, v_w_in, v_lru_conv_w, v_lru_conv_b, v_lru_w_a, v_lru_b_a, v_lru_w_x, v_lru_b_x, v_lru_lambda, v_attn_sinks, v_conv_w, v_conv_b, v_conv_ln_g, v_conv_ln_b, v_group_g, v_w_out, v_mix_post_g, v_ffn2_pre_g, v_ffn2_w_gu, v_ffn2_w_down, v_ffn2_post_g)
    return _step(x, loss_target, dict(zip(WEIGHT_NAMES, args)), dict(zip(WEIGHT_NAMES, ms)), dict(zip(WEIGHT_NAMES, vs)))
```

```python
import functools
import math
import operator

import jax
import jax.numpy as jnp
from jax import lax
from jax.experimental import pallas as pl
from jax.experimental.pallas import tpu as pltpu

F32 = jnp.float32
BF16 = jnp.bfloat16
N_DEV = 8
AXES = ("x", "y", "c")
MESH = pl.DeviceIdType.MESH

NORM_EPS = 1e-6
LN_EPS = 1e-5
NEG_BIG = -1e30
W_A = 256
W_B = 512
W_C = 256
HEAD_DIM = 64
N_Q_HEADS = 8
N_KV_HEADS = 2
Q_PER_KV = N_Q_HEADS // N_KV_HEADS
KV_W = N_KV_HEADS * HEAD_DIM
BLK = 128
LRU_K = 4
LRU_C = 8.0
A_BLOCKS = 4
CC_K = 31
CC_HALO = 32
LRU_HALO = 8
D_IN_PROJ = 2 * W_A + W_B + 2 * KV_W + 2 * W_C
ADAM_LR = 0.001
ADAM_B1 = 0.9
ADAM_B2 = 0.999
ADAM_EPS = 1e-08
ADAM_WD = 0.01
ADAM_STEP = 10
VMEM_LIMIT = 56 * 1024 * 1024

SDS = jax.ShapeDtypeStruct
ANY = pl.BlockSpec(memory_space=pl.ANY)


def _time_tile(s):
    return max(BLK, s // 8)


class _Exchange:
    def __init__(self, inputs, out_shapes, aliases, sem_shapes, start, wait):
        self.inputs, self.out_shapes, self.aliases, self.sem_shapes = inputs, out_shapes, aliases, sem_shapes
        self.start, self.wait = start, wait


def _my_pos():
    x, y, c = (lax.axis_index(a) for a in AXES)
    return x, y, c, 4 * x + 2 * y + c


def _flip(k):
    x, y, c, _ = _my_pos()
    return (1 - x if k & 4 else x, 1 - y if k & 2 else y, 1 - c if k & 1 else c)


def _slot(dev):
    return 4 * dev[0] + 2 * dev[1] + dev[2]


def _dev(p):
    return (p >> 2, (p >> 1) & 1, p & 1)


def _gather_a(items, two_level):
    rels = (1, 2, 4, 6) if two_level else tuple(range(1, N_DEV))
    n = len(items)
    src_of = lambda ins, a: ins[a] if items[a][1] is None else ins[a].at[items[a][1]]

    def shape_of(a):
        arr, l = items[a]
        return arr.shape if l is None else arr.shape[1:]

    def copies(ins, outs, sems, a):
        send, recv, _ = sems
        me = _my_pos()[3]
        return [(k, pltpu.make_async_remote_copy(
            src_ref=src_of(ins, a), dst_ref=outs[a].at[me], send_sem=send.at[a, k], recv_sem=recv.at[a, k],
            device_id=_flip(k), device_id_type=MESH)) for k in rels]

    def local(ins, outs, sems, a):
        return pltpu.make_async_copy(src_of(ins, a), outs[a].at[_my_pos()[3]], sems[2].at[a])

    def start(ins, outs, sems):
        for a in range(n):
            local(ins, outs, sems, a).start()
            for _, cp in copies(ins, outs, sems, a):
                cp.start()

    def wait(ins, outs, sems):
        send, recv, _ = sems
        for a in range(n):
            for k, cp in copies(ins, outs, sems, a):
                pltpu.make_async_remote_copy(
                    src_ref=src_of(ins, a), dst_ref=outs[a].at[_slot(_flip(k))], send_sem=send.at[a, k],
                    recv_sem=recv.at[a, k], device_id=_flip(k), device_id_type=MESH).wait_recv()
                cp.wait_send()
            local(ins, outs, sems, a).wait()

    return _Exchange([it[0] for it in items], [SDS((N_DEV,) + shape_of(a), items[a][0].dtype) for a in range(n)], {},
                     [pltpu.SemaphoreType.DMA((n, N_DEV)), pltpu.SemaphoreType.DMA((n, N_DEV)),
                      pltpu.SemaphoreType.DMA((n,))], start, wait)


def _gather_b(bufs):
    n = len(bufs)

    def copies(ins, outs, sems, a, c_of_block):
        send, recv = sems
        x, y, c, _ = _my_pos()
        res = []
        for k in (2, 4, 6):
            chip = _flip(k)
            blk = _slot((chip[0], chip[1], c if c_of_block == "mine" else 1 - c))
            res.append(pltpu.make_async_remote_copy(
                src_ref=ins[a].at[blk], dst_ref=outs[a].at[blk], send_sem=send.at[a, k], recv_sem=recv.at[a, k],
                device_id=_flip(1), device_id_type=MESH))
        return res

    def start(ins, outs, sems):
        for a in range(n):
            for cp in copies(ins, outs, sems, a, "mine"):
                cp.start()

    def wait(ins, outs, sems):
        for a in range(n):
            for cp in copies(ins, outs, sems, a, "sibling"):
                cp.wait_recv()
            for cp in copies(ins, outs, sems, a, "mine"):
                cp.wait_send()

    return _Exchange(list(bufs), [SDS(b.shape, b.dtype) for b in bufs], {a: a for a in range(n)},
                     [pltpu.SemaphoreType.DMA((n, N_DEV)), pltpu.SemaphoreType.DMA((n, N_DEV))], start, wait)


def _grad_x(items, n_l):
    n = len(items)
    inputs, first_in, recv_in, aliases, out_shapes = [], [], [], {}, []
    for a, (arrs, l, recv) in enumerate(items):
        first_in.append(len(inputs))
        inputs += list(arrs)
        assert sum(arr.shape[0] for arr in arrs) == N_DEV
        if recv is not None:
            aliases[len(inputs)] = a
            inputs.append(recv)
        out_shapes.append(SDS((N_DEV, n_l) + arrs[0].shape[1:], arrs[0].dtype))

    def slab(ins, a, p):
        off = 0
        for j, arr in enumerate(items[a][0]):
            if p < off + arr.shape[0]:
                return ins[first_in[a] + j].at[p - off]
            off += arr.shape[0]
        raise AssertionError

    def rdma(ins, outs, sems, a, p, src_dev):
        send, recv, _ = sems
        return pltpu.make_async_remote_copy(
            src_ref=slab(ins, a, p), dst_ref=outs[a].at[src_dev, items[a][1]], send_sem=send.at[a, p],
            recv_sem=recv.at[a, src_dev], device_id=_dev(p), device_id_type=MESH)

    def local(ins, outs, sems, a, p):
        return pltpu.make_async_copy(slab(ins, a, p), outs[a].at[p, items[a][1]], sems[2].at[a])

    def start(ins, outs, sems):
        me = _my_pos()[3]
        for p in range(N_DEV):
            @pl.when(me != p)
            def _():
                for a in range(n):
                    rdma(ins, outs, sems, a, p, me).start()

            @pl.when(me == p)
            def _():
                for a in range(n):
                    local(ins, outs, sems, a, p).start()

    def wait(ins, outs, sems):
        me = _my_pos()[3]
        for p in range(N_DEV):
            @pl.when(me != p)
            def _():
                for a in range(n):
                    rdma(ins, outs, sems, a, p, p).wait_recv()
                    rdma(ins, outs, sems, a, p, p).wait_send()

            @pl.when(me == p)
            def _():
                for a in range(n):
                    local(ins, outs, sems, a, p).wait()

    return _Exchange(inputs, out_shapes, aliases,
                     [pltpu.SemaphoreType.DMA((n, N_DEV)), pltpu.SemaphoreType.DMA((n, N_DEV)),
                      pltpu.SemaphoreType.DMA((n,))], start, wait)


def _pcall(body, args, *, name, grid, in_specs, out_specs, out_shape, scratch_shapes=(), carry=(), body_aliases=None):
    n_in, n_out, n_scr = len(in_specs), len(out_specs), len(scratch_shapes)
    c_in = [len(e.inputs) for e in carry]
    c_out = [len(e.out_shapes) for e in carry]
    c_sem = [len(e.sem_shapes) for e in carry]
    aliases = dict(body_aliases or {})
    for j, e in enumerate(carry):
        for i_loc, o_loc in e.aliases.items():
            aliases[n_in + sum(c_in[:j]) + i_loc] = n_out + sum(c_out[:j]) + o_loc

    def wrapped(*refs):
        def take(counts, pos):
            groups = []
            for cnt in counts:
                groups.append(refs[pos:pos + cnt])
                pos += cnt
            return groups, pos

        (ins,), pos = take([n_in], 0)
        cins, pos = take(c_in, pos)
        (outs,), pos = take([n_out], pos)
        couts, pos = take(c_out, pos)
        (scr,), pos = take([n_scr], pos)
        csems, pos = take(c_sem, pos)
        if carry:
            ids = [pl.program_id(k) for k in range(len(grid))]
            first = functools.reduce(operator.and_, [i == 0 for i in ids])
            last = functools.reduce(operator.and_, [i == g - 1 for i, g in zip(ids, grid)])

            @pl.when(first)
            def _():
                for e, ci, co, cs in zip(carry, cins, couts, csems):
                    e.start(ci, co, cs)

        body(*ins, *outs, *scr)
        if carry:
            @pl.when(last)
            def _():
                for e, ci, co, cs in zip(carry, cins, couts, csems):
                    e.wait(ci, co, cs)

    res = pl.pallas_call(
        wrapped, name=name, grid=grid,
        in_specs=list(in_specs) + [ANY] * sum(c_in),
        out_specs=list(out_specs) + [ANY] * sum(c_out),
        out_shape=list(out_shape) + [s for e in carry for s in e.out_shapes],
        scratch_shapes=list(scratch_shapes) + [s for e in carry for s in e.sem_shapes],
        input_output_aliases=aliases,
        compiler_params=pltpu.CompilerParams(dimension_semantics=("arbitrary",) * len(grid),
                                             vmem_limit_bytes=VMEM_LIMIT),
    )(*args, *[a for e in carry for a in e.inputs])
    outs, pos, extra = list(res[:n_out]), n_out, []
    for cnt in c_out:
        extra.append(list(res[pos:pos + cnt]))
        pos += cnt
    return outs, extra


def _all_gather(items, name):
    n = len(items)
    shape_of = lambda a: items[a][0].shape if items[a][1] is None else items[a][0].shape[1:]

    def body(*refs):
        ins, outs, (send_sems, recv_sems, local_sems) = refs[:n], refs[n:2 * n], refs[2 * n:]
        x, y, c, me = _my_pos()
        src_of = lambda a: ins[a] if items[a][1] is None else ins[a].at[items[a][1]]

        def copy(a, k, block, to, src=None):
            dst = outs[a].at[_slot(block)]
            return pltpu.make_async_remote_copy(
                src_ref=dst if src is None else src, dst_ref=dst,
                send_sem=send_sems.at[a, k], recv_sem=recv_sems.at[a, k], device_id=to, device_id_type=MESH)

        mine = [pltpu.make_async_copy(src_of(a), outs[a].at[me], local_sems.at[a]) for a in range(n)]
        for cp in mine:
            cp.start()
        first = [copy(a, k, (x, y, c), _flip(k), src=src_of(a)) for a in range(n) for k in (1, 2, 4, 6)]
        for cp in first:
            cp.start()
        passed = []
        for k in (2, 4, 6):
            for a in range(n):
                copy(a, k, _flip(k), (x, y, c)).wait_recv()
                fwd = copy(a, k + 1, _flip(k), _flip(1))
                fwd.start()
                passed.append(fwd)
        for a in range(n):
            copy(a, 1, _flip(1), (x, y, c)).wait_recv()
            for k in (2, 4, 6):
                copy(a, k + 1, _flip(k + 1), (x, y, c)).wait_recv()
        for cp in first + passed:
            cp.wait_send()
        for cp in mine:
            cp.wait()

    return pl.pallas_call(
        body, name=name,
        in_specs=[ANY] * n, out_specs=[ANY] * n,
        out_shape=[SDS((N_DEV,) + shape_of(a), items[a][0].dtype) for a in range(n)],
        scratch_shapes=[pltpu.SemaphoreType.DMA((n, N_DEV)), pltpu.SemaphoreType.DMA((n, N_DEV)),
                        pltpu.SemaphoreType.DMA((n,))],
    )(*[it[0] for it in items])


def _mm(a, b):
    return jnp.dot(a.astype(BF16), b.astype(BF16), preferred_element_type=F32)


def _mm_nt(a, b):
    return lax.dot_general(a.astype(BF16), b.astype(BF16), (((1,), (1,)), ((), ())), preferred_element_type=F32)


def _mm_tn(a, b):
    return lax.dot_general(a.astype(BF16), b.astype(BF16), (((0,), (0,)), ((), ())), preferred_element_type=F32)


def _rms_r(x):
    return lax.rsqrt(jnp.mean(x * x, axis=-1, keepdims=True) + NORM_EPS)


def _rms_bwd(x, r, g, dy):
    gy = dy * g
    dx = r * (gy - x * (r * r) * jnp.mean(gy * x, axis=-1, keepdims=True))
    dg = jnp.sum(dy * x * r, axis=0, keepdims=True)
    return dx, dg


def _sigmoid(x):
    return 1.0 / (1.0 + jnp.exp(-x))


def _dsilu(z, sz):
    return sz * (1.0 + z * (1.0 - sz))


def _swiglu_bf16(g, u):
    sg = 0.5 * jnp.tanh(0.5 * g) + 0.5
    silu = g * sg
    return silu * u, silu, sg + silu * (1.0 - sg)


_GELU_C = math.sqrt(2.0 / math.pi)


def _gelu(x):
    t = jnp.tanh(_GELU_C * (x + 0.044715 * x * x * x))
    return 0.5 * x * (1.0 + t), t


def _dgelu(x, t):
    return 0.5 * (1.0 + t) + 0.5 * x * (1.0 - t * t) * _GELU_C * (1.0 + 3.0 * 0.044715 * x * x)


def _log1p(e):
    return jnp.where(e < 1e-2, e * (1.0 - e * (0.5 - e * (1.0 / 3.0))), jnp.log(1.0 + e))


def _softplus(x):
    return jnp.maximum(x, 0.0) + _log1p(jnp.exp(-jnp.abs(x)))


def _neg_expm1(x):
    small = -x * (1.0 + x * (0.5 + x * (1.0 / 6.0) * (1.0 + x * 0.25)))
    return jnp.where(x > -1e-2, small, 1.0 - jnp.exp(x))


def _shift_down(x, s):
    return x if s == 0 else pltpu.roll(x, s, 0)


def _shift_up(x, s):
    return x if s == 0 else pltpu.roll(x, x.shape[0] - s, 0)


def _ffn_wspecs(d, fc, order):
    f_of = (lambda i, f: f) if order == "tf" else (lambda f, i: f)
    n_f = N_DEV // 2
    return [pl.BlockSpec((None, fc, d), lambda *g: (f_of(*g), 0, 0)),
            pl.BlockSpec((None, fc, d), lambda *g: (f_of(*g) + n_f, 0, 0)),
            pl.BlockSpec((2, fc // 2, d), lambda *g: (f_of(*g), 0, 0))]


def _ffn_fwd(x, pre_g, post_g, wgu_t, wd, name, carry=()):
    s, d = x.shape
    fc = wgu_t.shape[1]
    ts = 2 * _time_tile(s)
    n_t, n_f = s // ts, N_DEV // 2

    def body(x_ref, pg_ref, qg_ref, wg_ref, wu_ref, wd_ref, xo_ref, h_ref, g_ref, u_ref, d_ref, h_scr, acc):
        f = pl.program_id(1)

        @pl.when(f == 0)
        def _():
            xv = x_ref[...]
            hv = (xv * _rms_r(xv) * pg_ref[...]).astype(BF16)
            h_scr[...] = hv
            h_ref[...] = hv
            acc[...] = jnp.zeros_like(acc)

        hv = h_scr[...]
        g = _mm_nt(hv, wg_ref[...])
        u = _mm_nt(hv, wu_ref[...])
        g = g.astype(BF16)
        u = u.astype(BF16)
        g_ref[...] = g
        u_ref[...] = u
        acc[...] += jnp.dot(_swiglu_bf16(g, u)[0], wd_ref[...].reshape(fc, d), preferred_element_type=F32)

        @pl.when(f == n_f - 1)
        def _():
            dv = acc[...]
            d_ref[...] = dv.astype(BF16)
            xo_ref[...] = x_ref[...] + 0.5 * (dv * _rms_r(dv) * qg_ref[...])

    row = pl.BlockSpec((ts, d), lambda i, f: (i, 0))
    vec = pl.BlockSpec((1, d), lambda i, f: (0, 0))
    act = pl.BlockSpec((None, ts, fc), lambda i, f: (f, i, 0))
    return _pcall(
        body, (x, pre_g, post_g, wgu_t, wgu_t, wd), name=name, grid=(n_t, n_f),
        in_specs=[row, vec, vec] + _ffn_wspecs(d, fc, "tf"),
        out_specs=[row, row, act, act, row],
        out_shape=[SDS((s, d), F32), SDS((s, d), BF16), SDS((n_f, s, fc), BF16), SDS((n_f, s, fc), BF16),
                   SDS((s, d), BF16)],
        scratch_shapes=[pltpu.VMEM((ts, d), BF16), pltpu.VMEM((ts, d), F32)], carry=carry)


def _ffn_bwd_act(dxo, dmid, x, pre_g, post_g, g_s, u_s, wgu_t, wd, name, carry=()):
    s, d = x.shape
    fc = wgu_t.shape[1]
    ts = _time_tile(s)
    n_t, n_f = s // ts, N_DEV // 2

    def body(dxo_ref, dm_ref, x_ref, pg_ref, qg_ref, g_ref, u_ref, wg_ref, wu_ref, wd_ref,
             dx_ref, dd_ref, dg_ref, du_ref, dpg_ref, dqg_ref, dd_scr, dh_acc):
        i, f = pl.program_id(0), pl.program_id(1)

        @pl.when((i == 0) & (f == 0))
        def _():
            dpg_ref[...] = jnp.zeros_like(dpg_ref)
            dqg_ref[...] = jnp.zeros_like(dqg_ref)

        @pl.when(f == 0)
        def _():
            dv = dm_ref[...].astype(F32)
            ddv, dq = _rms_bwd(dv, _rms_r(dv), qg_ref[...], 0.5 * dxo_ref[...])
            dqg_ref[...] += dq
            dd_scr[...] = ddv.astype(BF16)
            dd_ref[...] = ddv.astype(BF16)
            dh_acc[...] = jnp.zeros_like(dh_acc)

        da = _mm_nt(dd_scr[...], wd_ref[...].reshape(fc, d)).astype(BF16)
        u = u_ref[...]
        _, silu, dsilu = _swiglu_bf16(g_ref[...], u)
        du = da * silu
        dg = da * u * dsilu
        dg_ref[...] = dg
        du_ref[...] = du
        dh_acc[...] += _mm(dg, wg_ref[...]) + _mm(du, wu_ref[...])

        @pl.when(f == n_f - 1)
        def _():
            xv = x_ref[...]
            dxv, dp = _rms_bwd(xv, _rms_r(xv), pg_ref[...], dh_acc[...])
            dpg_ref[...] += dp
            dx_ref[...] = dxo_ref[...] + dxv

    row = pl.BlockSpec((ts, d), lambda i, f: (i, 0))
    vec = pl.BlockSpec((1, d), lambda i, f: (0, 0))
    act = pl.BlockSpec((None, ts, fc), lambda i, f: (f, i, 0))
    return _pcall(
        body, (dxo, dmid, x, pre_g, post_g, g_s, u_s, wgu_t, wgu_t, wd), name=name, grid=(n_t, n_f),
        in_specs=[row, row, row, vec, vec, act, act] + _ffn_wspecs(d, fc, "tf"),
        out_specs=[row, row, act, act, vec, vec],
        out_shape=[SDS((s, d), F32), SDS((s, d), BF16), SDS((n_f, s, fc), BF16), SDS((n_f, s, fc), BF16),
                   SDS((1, d), F32), SDS((1, d), F32)],
        scratch_shapes=[pltpu.VMEM((ts, d), BF16), pltpu.VMEM((ts, d), F32)], carry=carry)


def _ffn_bwd_w(h, dd, g_s, u_s, dg, du, name, carry=()):
    s, d = h.shape
    n_f, _, fc = g_s.shape
    ts = _time_tile(s)
    n_t = s // ts

    def body(h_ref, dd_ref, g_ref, u_ref, dg_ref, du_ref, wg_ref, wu_ref, wd_ref, acc_g, acc_u, acc_d):
        i = pl.program_id(1)

        @pl.when(i == 0)
        def _():
            acc_g[...] = jnp.zeros_like(acc_g)
            acc_u[...] = jnp.zeros_like(acc_u)
            acc_d[...] = jnp.zeros_like(acc_d)

        a = _swiglu_bf16(g_ref[...], u_ref[...])[0]
        hv = h_ref[...]
        acc_g[...] += _mm_tn(dg_ref[...], hv)
        acc_u[...] += _mm_tn(du_ref[...], hv)
        acc_d[...] += _mm_tn(a, dd_ref[...])

        @pl.when(i == n_t - 1)
        def _():
            wg_ref[...] = acc_g[...].astype(BF16)
            wu_ref[...] = acc_u[...].astype(BF16)
            wd_ref[...] = acc_d[...].astype(BF16)

    row = pl.BlockSpec((ts, d), lambda f, i: (i, 0))
    act = pl.BlockSpec((None, ts, fc), lambda f, i: (f, i, 0))
    out = pl.BlockSpec((None, fc, d), lambda f, i: (f, 0, 0))
    return _pcall(
        body, (h, dd, g_s, u_s, dg, du), name=name, grid=(n_f, n_t),
        in_specs=[row, row, act, act, act, act], out_specs=[out, out, out],
        out_shape=[SDS((n_f, fc, d), BF16)] * 3,
        scratch_shapes=[pltpu.VMEM((fc, d), F32)] * 3, carry=carry)


def _ffn_bwd_w_send(h, dd, g_s, u_s, dg, du, recv_gu, recv_d, layer, name, carry=()):
    s, d = h.shape
    n_f, _, fc = g_s.shape
    ts = _time_tile(s)
    n_t = s // ts
    half = fc // 2

    def body(h_ref, dd_ref, g_ref, u_ref, dg_ref, du_ref, _rgu_in, _rd_in, rgu_ref, rd_ref,
             acc_g, acc_u, acc_d, st_g, st_u, st_d, send_sems, recv_sems, local_sems):
        f, i = pl.program_id(0), pl.program_id(1)
        me = _my_pos()[3]

        @pl.when(i == 0)
        def _():
            acc_g[...] = jnp.zeros_like(acc_g)
            acc_u[...] = jnp.zeros_like(acc_u)
            acc_d[...] = jnp.zeros_like(acc_d)

        a = _swiglu_bf16(g_ref[...], u_ref[...])[0]
        hv = h_ref[...]
        acc_g[...] += _mm_tn(dg_ref[...], hv)
        acc_u[...] += _mm_tn(du_ref[...], hv)
        acc_d[...] += _mm_tn(a, dd_ref[...])

        def messages(fs):
            return [(st_g.at[fs], rgu_ref, 0, fs, 0), (st_u.at[fs], rgu_ref, 0, fs + n_f, 1),
                    (st_d.at[fs, pl.ds(0, half)], rd_ref, 1, 2 * fs, 2),
                    (st_d.at[fs, pl.ds(half, half)], rd_ref, 1, 2 * fs + 1, 3)]

        def remote(fs, msg, src_dev):
            src, buf, row, p, j = msg
            return pltpu.make_async_remote_copy(
                src_ref=src, dst_ref=buf.at[src_dev, layer], send_sem=send_sems.at[fs, j],
                recv_sem=recv_sems.at[row, src_dev], device_id=_dev(p), device_id_type=MESH)

        def local(fs, msg):
            src, buf, _, p, j = msg
            return pltpu.make_async_copy(src, buf.at[p, layer], local_sems.at[fs, j])

        for fs in range(n_f):
            @pl.when((f == fs) & (i == n_t - 1))
            def _():
                st_g[fs] = acc_g[...].astype(BF16)
                st_u[fs] = acc_u[...].astype(BF16)
                st_d[fs] = acc_d[...].astype(BF16)
                for msg in messages(fs):
                    @pl.when(me != msg[3])
                    def _():
                        remote(fs, msg, me).start()

                    @pl.when(me == msg[3])
                    def _():
                        local(fs, msg).start()

        @pl.when((f == n_f - 1) & (i == n_t - 1))
        def _():
            for fs in range(n_f):
                for msg in messages(fs):
                    @pl.when(me != msg[3])
                    def _():
                        remote(fs, msg, me).wait_send()

                    @pl.when(me == msg[3])
                    def _():
                        local(fs, msg).wait()
            for src_dev in range(N_DEV):
                @pl.when(me != src_dev)
                def _():
                    remote(0, messages(0)[0], src_dev).wait_recv()
                    remote(0, messages(0)[2], src_dev).wait_recv()

    row = pl.BlockSpec((ts, d), lambda f, i: (i, 0))
    act = pl.BlockSpec((None, ts, fc), lambda f, i: (f, i, 0))
    return _pcall(
        body, (h, dd, g_s, u_s, dg, du, recv_gu, recv_d), name=name, grid=(n_f, n_t),
        in_specs=[row, row, act, act, act, act, ANY, ANY], out_specs=[ANY, ANY],
        out_shape=[SDS(recv_gu.shape, recv_gu.dtype), SDS(recv_d.shape, recv_d.dtype)],
        scratch_shapes=[pltpu.VMEM((fc, d), F32)] * 3 + [pltpu.VMEM((n_f, fc, d), BF16)] * 3
        + [pltpu.SemaphoreType.DMA((n_f, 4)), pltpu.SemaphoreType.DMA((2, N_DEV)), pltpu.SemaphoreType.DMA((n_f, 4))],
        carry=carry, body_aliases={6: 0, 7: 1})


_PROJ_WIDTHS = (W_A, W_A, W_B, KV_W, KV_W, 2 * W_C)


def _mix_in_fwd(x, pre_g, w_in_t, name, carry=()):
    s, d = x.shape
    ts = _time_tile(s)

    def body(x_ref, pg_ref, w_ref, hn_ref, *outs):
        xv = x_ref[...]
        hn = (xv * _rms_r(xv) * pg_ref[...]).astype(BF16)
        hn_ref[...] = hn
        proj = _mm_nt(hn, w_ref[...])
        off = 0
        for o_ref, w in zip(outs, _PROJ_WIDTHS):
            o_ref[...] = proj[:, off:off + w]
            off += w

    row = lambda w: pl.BlockSpec((ts, w), lambda i: (i, 0))
    return _pcall(
        body, (x, pre_g, w_in_t), name=name, grid=(s // ts,),
        in_specs=[row(d), pl.BlockSpec((1, d), lambda i: (0, 0)), pl.BlockSpec((D_IN_PROJ, d), lambda i: (0, 0))],
        out_specs=[row(d)] + [row(w) for w in _PROJ_WIDTHS],
        out_shape=[SDS((s, d), BF16)] + [SDS((s, w), F32) for w in _PROJ_WIDTHS], carry=carry)


def _mix_in_bwd(dres, x, pre_g, hn, w_in_t, dlx, dlg, dq, dk, dk_up, dv, dv_up, dglu, name, carry=()):
    s, d = x.shape
    ts = _time_tile(s)
    n_t = s // ts

    def body(dres_ref, x_ref, pg_ref, hn_ref, w_ref, dlx_ref, dlg_ref, dq_ref, dk_ref, dkn_ref,
             dv_ref, dvn_ref, dglu_ref, dx_ref, dw_ref, dpg_ref, acc):
        i = pl.program_id(0)

        @pl.when(i == 0)
        def _():
            acc[...] = jnp.zeros_like(acc)
            dpg_ref[...] = jnp.zeros_like(dpg_ref)

        def with_next(cur_ref, nxt_ref):
            nxt = jnp.where(i < n_t - 1, nxt_ref[...], 0.0)
            if ts == BLK:
                return cur_ref[...] + nxt
            return jnp.concatenate([cur_ref[:ts - BLK, :], cur_ref[ts - BLK:, :] + nxt], axis=0)

        dproj = jnp.concatenate([dlx_ref[...], dlg_ref[...], dq_ref[...], with_next(dk_ref, dkn_ref),
                                 with_next(dv_ref, dvn_ref), dglu_ref[...]], axis=1).astype(BF16)
        dhn = _mm(dproj, w_ref[...])
        acc[...] += _mm_tn(dproj, hn_ref[...])
        xv = x_ref[...]
        dxv, dp = _rms_bwd(xv, _rms_r(xv), pg_ref[...], dhn)
        dpg_ref[...] += dp
        dx_ref[...] = dres_ref[...] + dxv

        @pl.when(i == n_t - 1)
        def _():
            dw_ref[...] = acc[...].astype(BF16)

    row = lambda w: pl.BlockSpec((ts, w), lambda i: (i, 0))
    nxt = pl.BlockSpec((BLK, KV_W), lambda i: (jnp.minimum(i + 1, n_t - 1), 0))
    vec = pl.BlockSpec((1, d), lambda i: (0, 0))
    full = pl.BlockSpec((D_IN_PROJ, d), lambda i: (0, 0))
    return _pcall(
        body, (dres, x, pre_g, hn, w_in_t, dlx, dlg, dq, dk, dk_up, dv, dv_up, dglu), name=name, grid=(n_t,),
        in_specs=[row(d), row(d), vec, row(d), full, row(W_A), row(W_A), row(W_B), row(KV_W), nxt,
                  row(KV_W), nxt, row(2 * W_C)],
        out_specs=[row(d), full, vec],
        out_shape=[SDS((s, d), F32), SDS((D_IN_PROJ, d), BF16), SDS((1, d), F32)],
        scratch_shapes=[pltpu.VMEM((D_IN_PROJ, d), F32)], carry=carry)


def _lru_gates(xc, lru_p):
    cw_ref, cb_ref, wa_ref, ba_ref, wx_ref, bx_ref, lam_ref = lru_p
    c = cb_ref[...]
    for j in range(LRU_K):
        c = c + cw_ref[j:j + 1, :] * _shift_down(xc, LRU_K - 1 - j)[LRU_HALO:, :]
    r = _sigmoid(_mm(c, wa_ref[...]) + ba_ref[...])
    ig = _sigmoid(_mm(c, wx_ref[...]) + bx_ref[...])
    sp = _softplus(-lam_ref[...])
    log_a = -LRU_C * r * sp
    a = jnp.exp(log_a)
    m = jnp.sqrt(_neg_expm1(2.0 * log_a))
    return c, r, ig, sp, a, m


def _lru_pspecs():
    small = lambda r: pl.BlockSpec((r, W_A), lambda i: (0, 0))
    return [small(LRU_K), small(1), small(W_A), small(1), small(W_A), small(1), small(1)]


def _lru_fwd(lx, lg, lru_p, name, carry=()):
    s = lx.shape[0]
    ts = _time_tile(s)
    n8 = ts // LRU_HALO

    def body(lx_ref, lxp_ref, lg_ref, *rest):
        lru_p, (ya_ref, h_ref, hcarry) = rest[:7], rest[7:]
        i = pl.program_id(0)
        prev = jnp.where(i > 0, lxp_ref[...], 0.0)
        xc = jnp.concatenate([prev, lx_ref[...]], axis=0)
        c, r, ig, sp, a, m = _lru_gates(xc, lru_p)
        acc_a, acc_b = a, m * (ig * c)
        t = lax.broadcasted_iota(jnp.int32, a.shape, 0)
        k = 1
        while k < ts:
            keep = t >= k
            acc_b = jnp.where(keep, acc_a * _shift_down(acc_b, k) + acc_b, acc_b)
            acc_a = jnp.where(keep, acc_a * _shift_down(acc_a, k), acc_a)
            k *= 2
        h0 = jnp.where(i > 0, hcarry[...], 0.0)
        h = acc_b + acc_a * h0
        hcarry[...] = h[ts - 1:ts, :]
        h_ref[...] = h
        ya_ref[...] = _gelu(lg_ref[...])[0] * h

    row = pl.BlockSpec((ts, W_A), lambda i: (i, 0))
    prev8 = pl.BlockSpec((LRU_HALO, W_A), lambda i: (jnp.maximum(i * n8 - 1, 0), 0))
    return _pcall(
        body, (lx, lx, lg, *lru_p), name=name, grid=(s // ts,),
        in_specs=[row, prev8, row] + _lru_pspecs(), out_specs=[row, row],
        out_shape=[SDS((s, W_A), F32), SDS((s, W_A), F32)],
        scratch_shapes=[pltpu.VMEM((1, W_A), F32)], carry=carry)


def _lru_bwd(dya, lx, lg, h_s, lru_p, name, carry=()):
    s = lx.shape[0]
    ts = _time_tile(s)
    n_t = s // ts
    n8 = ts // LRU_HALO

    def body(dya_ref, lx_ref, lxp_ref, lg_ref, h_ref, hp_ref, *rest):
        lru_p = rest[:7]
        (dlx_ref, dlg_ref, dcw_ref, dcb_ref, dwa_ref, dba_ref, dwx_ref, dbx_ref, dlam_ref,
         carry_a, carry_l, carry_dc) = rest[7:]
        cw_ref, _, wa_ref, _, wx_ref, _, lam_ref = lru_p
        i = pl.program_id(0)
        first_tile = i == n_t - 1
        last_tile = i == 0

        @pl.when(i == 0)
        def _():
            for ref in (dcw_ref, dcb_ref, dwa_ref, dba_ref, dwx_ref, dbx_ref, dlam_ref):
                ref[...] = jnp.zeros_like(ref)

        prev = jnp.where(first_tile, 0.0, lxp_ref[...])
        xc = jnp.concatenate([prev, lx_ref[...]], axis=0)
        c, r, ig, sp, a, m = _lru_gates(xc, lru_p)
        h = h_ref[...]
        hcat = jnp.concatenate([jnp.where(first_tile, 0.0, hp_ref[...]), h], axis=0)
        h_m1 = _shift_down(hcat, 1)[LRU_HALO:, :]
        lg = lg_ref[...]
        ge, th = _gelu(lg)
        dya = dya_ref[...]
        dlg_ref[...] = dya * h * _dgelu(lg, th)
        dh = dya * ge
        t = lax.broadcasted_iota(jnp.int32, a.shape, 0)
        a_next = jnp.where(t < ts - 1, _shift_up(a, 1), jnp.where(last_tile, 0.0, carry_a[...]))
        acc_a, acc_b = a_next, dh
        k = 1
        while k < ts:
            keep = t < ts - k
            acc_b = jnp.where(keep, acc_a * _shift_up(acc_b, k) + acc_b, acc_b)
            acc_a = jnp.where(keep, acc_a * _shift_up(acc_a, k), acc_a)
            k *= 2
        lam_beyond = jnp.where(last_tile, 0.0, carry_l[...])
        lmb = acc_b + acc_a * lam_beyond
        carry_a[...] = a[0:1, :]
        carry_l[...] = lmb[0:1, :]
        gi = ig * c
        dgi = lmb * m
        dla = lmb * h_m1 * a - (lmb * gi) * (a * a) / m
        dr = dla * (-LRU_C * sp)
        dsp = jnp.sum(dla * (-LRU_C * r), axis=0, keepdims=True)
        dlam_ref[...] += -dsp * _sigmoid(-lam_ref[...])
        dra = dr * r * (1.0 - r)
        dia = dgi * c * ig * (1.0 - ig)
        dc = dgi * ig + _mm_nt(dra, wa_ref[...]) + _mm_nt(dia, wx_ref[...])
        dwa_ref[...] += _mm_tn(c, dra)
        dwx_ref[...] += _mm_tn(c, dia)
        dba_ref[...] += jnp.sum(dra, axis=0, keepdims=True)
        dbx_ref[...] += jnp.sum(dia, axis=0, keepdims=True)
        dcb_ref[...] += jnp.sum(dc, axis=0, keepdims=True)
        dcc = jnp.concatenate([dc, jnp.where(last_tile, 0.0, carry_dc[...])], axis=0)
        carry_dc[...] = dc[0:LRU_HALO, :]
        dlx = jnp.zeros_like(dc)
        for j in range(LRU_K):
            sh = LRU_K - 1 - j
            dcw_ref[j:j + 1, :] += jnp.sum(dc * _shift_down(xc, sh)[LRU_HALO:, :], axis=0, keepdims=True)
            dlx = dlx + cw_ref[j:j + 1, :] * _shift_up(dcc, sh)[:ts, :]
        dlx_ref[...] = dlx

    row = pl.BlockSpec((ts, W_A), lambda i: (n_t - 1 - i, 0))
    prev8 = pl.BlockSpec((LRU_HALO, W_A), lambda i: (jnp.maximum((n_t - 1 - i) * n8 - 1, 0), 0))
    small = lambda r: pl.BlockSpec((r, W_A), lambda i: (0, 0))
    return _pcall(
        body, (dya, lx, lx, lg, h_s, h_s, *lru_p), name=name, grid=(n_t,),
        in_specs=[row, row, prev8, row, row, prev8] + _lru_pspecs(),
        out_specs=[row, row, small(LRU_K), small(1), small(W_A), small(1), small(W_A), small(1), small(1)],
        out_shape=[SDS((s, W_A), F32), SDS((s, W_A), F32), SDS((LRU_K, W_A), F32), SDS((1, W_A), F32),
                   SDS((W_A, W_A), F32), SDS((1, W_A), F32), SDS((W_A, W_A), F32), SDS((1, W_A), F32),
                   SDS((1, W_A), F32)],
        scratch_shapes=[pltpu.VMEM((1, W_A), F32), pltpu.VMEM((1, W_A), F32), pltpu.VMEM((LRU_HALO, W_A), F32)],
        carry=carry)


_ATT_ROWS = N_Q_HEADS * BLK
_GRP_ROWS = Q_PER_KV * BLK


def _attn_stack(ref, rows, g):
    return jnp.concatenate([ref[rows, h * HEAD_DIM:(h + 1) * HEAD_DIM]
                            for h in range(g * Q_PER_KV, (g + 1) * Q_PER_KV)], axis=0)


def _attn_unstack(parts):
    return jnp.concatenate([p[j * BLK:(j + 1) * BLK, :] for p in parts for j in range(Q_PER_KV)], axis=1)


def _grp(x, g):
    return x[:, g * _GRP_ROWS:(g + 1) * _GRP_ROWS]


def _attn_block(q_ref, k_ref, kp_ref, v_ref, vp_ref, sink_row, i, b):
    rows, prev = slice(b * BLK, (b + 1) * BLK), slice((b - 1) * BLK, b * BLK)
    qs, kcs, kps, vcs, vps = [], [], [], [], []
    for g in range(N_KV_HEADS):
        cols = slice(g * HEAD_DIM, (g + 1) * HEAD_DIM)
        qs.append(_attn_stack(q_ref, rows, g))
        kcs.append(k_ref[rows, cols])
        vcs.append(v_ref[rows, cols])
        kps.append(kp_ref[:, cols] if b == 0 else k_ref[prev, cols])
        vps.append(vp_ref[:, cols] if b == 0 else v_ref[prev, cols])
    scale = 1.0 / math.sqrt(HEAD_DIM)
    sc = jnp.concatenate([_mm_nt(kcs[g], qs[g]) for g in range(N_KV_HEADS)], axis=1) * scale
    sp = jnp.concatenate([_mm_nt(kps[g], qs[g]) for g in range(N_KV_HEADS)], axis=1) * scale
    kj = lax.broadcasted_iota(jnp.int32, (BLK, _ATT_ROWS), 0)
    qi = lax.broadcasted_iota(jnp.int32, (BLK, _ATT_ROWS), 1) & (BLK - 1)
    sc = jnp.where(kj <= qi, sc, NEG_BIG)
    sp = jnp.where((kj > qi) if b > 0 else ((kj > qi) & (i > 0)), sp, NEG_BIG)
    m = jnp.maximum(jnp.maximum(jnp.max(sc, axis=0, keepdims=True), jnp.max(sp, axis=0, keepdims=True)), sink_row)
    pc = jnp.exp(sc - m)
    pp = jnp.exp(sp - m)
    es = jnp.exp(sink_row - m)
    inv = 1.0 / (jnp.sum(pc, axis=0, keepdims=True) + jnp.sum(pp, axis=0, keepdims=True) + es)
    return qs, kcs, kps, vcs, vps, pc * inv, pp * inv, es * inv


def _attn_specs(s, ts):
    bpt = ts // BLK
    tile = lambda w: pl.BlockSpec((ts, w), lambda i: (i, 0))
    prv = pl.BlockSpec((BLK, KV_W), lambda i: (jnp.maximum(i * bpt - 1, 0), 0))
    sink = pl.BlockSpec((1, _ATT_ROWS), lambda i: (0, 0))
    return bpt, tile, prv, sink


def _attn_fwd(q, k, v, sink_row, name, carry=()):
    s = q.shape[0]
    ts = _time_tile(s)
    bpt, tile, prv, sink = _attn_specs(s, ts)

    def body(q_ref, k_ref, kp_ref, v_ref, vp_ref, sk_ref, y_ref):
        i = pl.program_id(0)
        for b in range(bpt):
            _, _, _, vcs, vps, pc, pp, _ = _attn_block(q_ref, k_ref, kp_ref, v_ref, vp_ref, sk_ref[...], i, b)
            outs = [_mm_tn(_grp(pc, g), vcs[g]) + _mm_tn(_grp(pp, g), vps[g]) for g in range(N_KV_HEADS)]
            y_ref[b * BLK:(b + 1) * BLK, :] = _attn_unstack(outs)

    return _pcall(
        body, (q, k, k, v, v, sink_row), name=name, grid=(s // ts,),
        in_specs=[tile(W_B), tile(KV_W), prv, tile(KV_W), prv, sink],
        out_specs=[tile(W_B)], out_shape=[SDS((s, W_B), F32)], carry=carry)


def _attn_bwd(dy, q, k, v, sinks, name, carry=()):
    s = q.shape[0]
    ts = _time_tile(s)
    n_t = s // ts
    bpt, tile, prv, sink = _attn_specs(s, ts)

    def body(dy_ref, q_ref, k_ref, kp_ref, v_ref, vp_ref, sk_ref, dq_ref, dk_ref, dv_ref, dku_ref, dvu_ref, dsk_ref):
        i = pl.program_id(0)

        @pl.when(i == 0)
        def _():
            dsk_ref[...] = jnp.zeros_like(dsk_ref)

        scale = 1.0 / math.sqrt(HEAD_DIM)
        groups = range(N_KV_HEADS)
        head_row = lax.broadcasted_iota(jnp.int32, (N_Q_HEADS, BLK), 0)
        dsk = jnp.zeros((N_Q_HEADS, BLK), F32)
        dk_blocks, dv_blocks = [], []
        for b in range(bpt):
            rows = slice(b * BLK, (b + 1) * BLK)
            qs, kcs, kps, vcs, vps, pc, pp, ps = _attn_block(q_ref, k_ref, kp_ref, v_ref, vp_ref, sk_ref[...], i, b)
            dos = [_attn_stack(dy_ref, rows, g) for g in groups]
            dpc = jnp.concatenate([_mm_nt(vcs[g], dos[g]) for g in groups], axis=1)
            dpp = jnp.concatenate([_mm_nt(vps[g], dos[g]) for g in groups], axis=1)
            delta = jnp.sum(pc * dpc, axis=0, keepdims=True) + jnp.sum(pp * dpp, axis=0, keepdims=True)
            dsc = pc * (dpc - delta) * scale
            dsp = pp * (dpp - delta) * scale
            dq_ref[rows, :] = _attn_unstack([_mm_tn(_grp(dsc, g), kcs[g]) + _mm_tn(_grp(dsp, g), kps[g])
                                             for g in groups])
            dk_blocks.append(jnp.concatenate([_mm(_grp(dsc, g), qs[g]) for g in groups], axis=1))
            dv_blocks.append(jnp.concatenate([_mm(_grp(pc, g), dos[g]) for g in groups], axis=1))
            dkp = jnp.concatenate([_mm(_grp(dsp, g), qs[g]) for g in groups], axis=1)
            dvp = jnp.concatenate([_mm(_grp(pp, g), dos[g]) for g in groups], axis=1)
            if b == 0:
                dku_ref[...] = dkp
                dvu_ref[...] = dvp
            else:
                dk_blocks[b - 1] = dk_blocks[b - 1] + dkp
                dv_blocks[b - 1] = dv_blocks[b - 1] + dvp
            dsink = -ps * delta
            for h in range(N_Q_HEADS):
                dsk = dsk + jnp.where(head_row == h, jnp.sum(dsink[:, h * BLK:(h + 1) * BLK], axis=1, keepdims=True), 0.0)
        for b in range(bpt):
            dk_ref[b * BLK:(b + 1) * BLK, :] = dk_blocks[b]
            dv_ref[b * BLK:(b + 1) * BLK, :] = dv_blocks[b]
        dsk_ref[...] += dsk

    up = pl.BlockSpec((BLK, KV_W), lambda i: (i, 0))
    return _pcall(
        body, (dy, q, k, k, v, v, sinks), name=name, grid=(n_t,),
        in_specs=[tile(W_B), tile(W_B), tile(KV_W), prv, tile(KV_W), prv, sink],
        out_specs=[tile(W_B), tile(KV_W), tile(KV_W), up, up, pl.BlockSpec((N_Q_HEADS, BLK), lambda i: (0, 0))],
        out_shape=[SDS((s, W_B), F32), SDS((s, KV_W), F32), SDS((s, KV_W), F32), SDS((n_t * BLK, KV_W), F32),
                   SDS((n_t * BLK, KV_W), F32), SDS((N_Q_HEADS, BLK), F32)], carry=carry)


def _cc_recompute(glu_ref, glup_ref, cw_ref, cb_ref, first_tile):
    prev = jnp.where(first_tile, 0.0, glup_ref[...])
    ge = jnp.concatenate([prev, glu_ref[...]], axis=0)
    y0 = ge[:, :W_C] * _sigmoid(ge[:, W_C:])
    y1 = cb_ref[...]
    for j in range(CC_K):
        y1 = y1 + cw_ref[j:j + 1, :] * _shift_down(y0, CC_K - 1 - j)[CC_HALO:, :]
    return y0, y1


def _ln_stats(y1):
    mu = jnp.mean(y1, axis=-1, keepdims=True)
    xc = y1 - mu
    rstd = lax.rsqrt(jnp.mean(xc * xc, axis=-1, keepdims=True) + LN_EPS)
    return xc * rstd, rstd


def _cc_specs(s, ts):
    n32 = ts // CC_HALO
    row = lambda w: pl.BlockSpec((ts, w), lambda i: (i, 0))
    prev = pl.BlockSpec((CC_HALO, 2 * W_C), lambda i: (jnp.maximum(i * n32 - 1, 0), 0))
    small = lambda r: pl.BlockSpec((r, W_C), lambda i: (0, 0))
    return row, prev, small


def _cc_fwd(glu, cw, cb, lng, lnb, name, carry=()):
    s = glu.shape[0]
    ts = _time_tile(s)
    row, prev, small = _cc_specs(s, ts)

    def body(glu_ref, glup_ref, cw_ref, cb_ref, lng_ref, lnb_ref, y_ref):
        _, y1 = _cc_recompute(glu_ref, glup_ref, cw_ref, cb_ref, pl.program_id(0) == 0)
        xhat, _ = _ln_stats(y1)
        z = xhat * lng_ref[...] + lnb_ref[...]
        y_ref[...] = z * _sigmoid(z)

    return _pcall(
        body, (glu, glu, cw, cb, lng, lnb), name=name, grid=(s // ts,),
        in_specs=[row(2 * W_C), prev, small(CC_HALO), small(1), small(1), small(1)],
        out_specs=[row(W_C)], out_shape=[SDS((s, W_C), F32)], carry=carry)


def _cc_bwd_conv(dy, glu, cw, cb, lng, lnb, name, carry=()):
    s = glu.shape[0]
    ts = _time_tile(s)
    row, prev, small = _cc_specs(s, ts)

    def body(dy_ref, glu_ref, glup_ref, cw_ref, cb_ref, lng_ref, lnb_ref, dy1_ref, dcw_ref, dcb_ref, dlng_ref, dlnb_ref):
        i = pl.program_id(0)

        @pl.when(i == 0)
        def _():
            for ref in (dcw_ref, dcb_ref, dlng_ref, dlnb_ref):
                ref[...] = jnp.zeros_like(ref)

        y0, y1 = _cc_recompute(glu_ref, glup_ref, cw_ref, cb_ref, i == 0)
        xhat, rstd = _ln_stats(y1)
        z = xhat * lng_ref[...] + lnb_ref[...]
        dz = dy_ref[...] * _dsilu(z, _sigmoid(z))
        dlng_ref[...] += jnp.sum(dz * xhat, axis=0, keepdims=True)
        dlnb_ref[...] += jnp.sum(dz, axis=0, keepdims=True)
        dxh = dz * lng_ref[...]
        dy1 = rstd * (dxh - jnp.mean(dxh, axis=-1, keepdims=True) - xhat * jnp.mean(dxh * xhat, axis=-1, keepdims=True))
        dy1_ref[...] = dy1
        dcb_ref[...] += jnp.sum(dy1, axis=0, keepdims=True)
        for j in range(CC_K):
            dcw_ref[j:j + 1, :] += jnp.sum(dy1 * _shift_down(y0, CC_K - 1 - j)[CC_HALO:, :], axis=0, keepdims=True)

    return _pcall(
        body, (dy, glu, glu, cw, cb, lng, lnb), name=name, grid=(s // ts,),
        in_specs=[row(W_C), row(2 * W_C), prev, small(CC_HALO), small(1), small(1), small(1)],
        out_specs=[row(W_C), small(CC_HALO), small(1), small(1), small(1)],
        out_shape=[SDS((s, W_C), F32), SDS((CC_HALO, W_C), F32)] + [SDS((1, W_C), F32)] * 3, carry=carry)


def _cc_bwd_glu(dy1, glu, cw, name, carry=()):
    s = glu.shape[0]
    ts = _time_tile(s)
    n_t = s // ts
    n32 = ts // CC_HALO

    def body(dy1_ref, dyn_ref, glu_ref, cw_ref, dglu_ref):
        i = pl.program_id(0)
        dcat = jnp.concatenate([dy1_ref[...], jnp.where(i < n_t - 1, dyn_ref[...], 0.0)], axis=0)
        dy0 = jnp.zeros((ts, W_C), F32)
        for j in range(CC_K):
            dy0 = dy0 + cw_ref[j:j + 1, :] * _shift_up(dcat, CC_K - 1 - j)[:ts, :]
        a = glu_ref[:, :W_C]
        sg = _sigmoid(glu_ref[:, W_C:])
        dglu_ref[...] = jnp.concatenate([dy0 * sg, dy0 * a * sg * (1.0 - sg)], axis=1)

    row = lambda w: pl.BlockSpec((ts, w), lambda i: (i, 0))
    nxt = pl.BlockSpec((CC_HALO, W_C), lambda i: (jnp.minimum((i + 1) * n32, s // CC_HALO - 1), 0))
    return _pcall(
        body, (dy1, dy1, glu, cw), name=name, grid=(n_t,),
        in_specs=[row(W_C), nxt, row(2 * W_C), pl.BlockSpec((CC_HALO, W_C), lambda i: (0, 0))],
        out_specs=[row(2 * W_C)], out_shape=[SDS((s, 2 * W_C), F32)], carry=carry)


_MIX_OFFS = ((0, W_A), (W_A, W_A + W_B), (W_A + W_B, W_A + W_B + W_C))


def _mix_out_fwd(x, ya, yb, yc, group_g, w_out, post_g, name, carry=()):
    s, d = x.shape
    ts = _time_tile(s)
    dm = w_out.shape[0]

    def body(x_ref, ya_ref, yb_ref, yc_ref, gg_ref, w_ref, qg_ref, xo_ref, o_ref):
        parts = []
        for y_ref, (lo, hi) in zip((ya_ref, yb_ref, yc_ref), _MIX_OFFS):
            yv = y_ref[...]
            parts.append(yv * _rms_r(yv) * gg_ref[:, lo:hi])
        o = _mm(jnp.concatenate(parts, axis=1), w_ref[...])
        o_ref[...] = o
        xo_ref[...] = x_ref[...] + o * _rms_r(o) * qg_ref[...]

    row = lambda w: pl.BlockSpec((ts, w), lambda i: (i, 0))
    return _pcall(
        body, (x, ya, yb, yc, group_g, w_out, post_g), name=name, grid=(s // ts,),
        in_specs=[row(d), row(W_A), row(W_B), row(W_C), pl.BlockSpec((1, dm), lambda i: (0, 0)),
                  pl.BlockSpec((dm, d), lambda i: (0, 0)), pl.BlockSpec((1, d), lambda i: (0, 0))],
        out_specs=[row(d), row(d)], out_shape=[SDS((s, d), F32), SDS((s, d), F32)], carry=carry)


def _mix_out_bwd(dxo, o, ya, yb, yc, group_g, w_out, post_g, name, carry=()):
    s, d = o.shape
    ts = _time_tile(s)
    n_t = s // ts
    dm = w_out.shape[0]

    def body(dxo_ref, o_ref, ya_ref, yb_ref, yc_ref, gg_ref, w_ref, qg_ref,
             dya_ref, dyb_ref, dyc_ref, dw_ref, dqg_ref, dgg_ref, acc):
        i = pl.program_id(0)

        @pl.when(i == 0)
        def _():
            acc[...] = jnp.zeros_like(acc)
            dqg_ref[...] = jnp.zeros_like(dqg_ref)
            dgg_ref[...] = jnp.zeros_like(dgg_ref)

        ov = o_ref[...]
        do, dq = _rms_bwd(ov, _rms_r(ov), qg_ref[...], dxo_ref[...])
        dqg_ref[...] += dq
        do = do.astype(BF16)
        dyn = _mm_nt(do, w_ref[...])
        parts, dggs = [], []
        for y_ref, dy_ref, (lo, hi) in zip((ya_ref, yb_ref, yc_ref), (dya_ref, dyb_ref, dyc_ref), _MIX_OFFS):
            yv = y_ref[...]
            r = _rms_r(yv)
            gg = gg_ref[:, lo:hi]
            parts.append(yv * r * gg)
            dyv, dg = _rms_bwd(yv, r, gg, dyn[:, lo:hi])
            dy_ref[...] = dyv
            dggs.append(dg)
        dgg_ref[...] += jnp.concatenate(dggs, axis=1)
        acc[...] += _mm_tn(jnp.concatenate(parts, axis=1), do)

        @pl.when(i == n_t - 1)
        def _():
            dw_ref[...] = acc[...].astype(BF16)

    row = lambda w: pl.BlockSpec((ts, w), lambda i: (i, 0))
    full = pl.BlockSpec((dm, d), lambda i: (0, 0))
    return _pcall(
        body, (dxo, o, ya, yb, yc, group_g, w_out, post_g), name=name, grid=(n_t,),
        in_specs=[row(d), row(d), row(W_A), row(W_B), row(W_C), pl.BlockSpec((1, dm), lambda i: (0, 0)), full,
                  pl.BlockSpec((1, d), lambda i: (0, 0))],
        out_specs=[row(W_A), row(W_B), row(W_C), full, pl.BlockSpec((1, d), lambda i: (0, 0)),
                   pl.BlockSpec((1, dm), lambda i: (0, 0))],
        out_shape=[SDS((s, W_A), F32), SDS((s, W_B), F32), SDS((s, W_C), F32), SDS((dm, d), BF16),
                   SDS((1, d), F32), SDS((1, dm), F32)],
        scratch_shapes=[pltpu.VMEM((dm, d), F32)], carry=carry)


def _loss_head(y, target, name):
    s, d = y.shape
    ts = _time_tile(s)

    def body(y_ref, t_ref, loss_ref, dy_ref):
        @pl.when(pl.program_id(0) == 0)
        def _():
            loss_ref[...] = jnp.zeros_like(loss_ref)

        err = y_ref[...] - t_ref[...]
        dy_ref[...] = err * (1.0 / d)
        per_tok = jnp.mean(err * err, axis=-1, keepdims=True)
        loss_ref[...] += 0.5 * jnp.sum(per_tok, axis=0, keepdims=True)

    row = pl.BlockSpec((ts, d), lambda i: (i, 0))
    return _pcall(body, (y, target), name=name, grid=(s // ts,), in_specs=[row, row],
                  out_specs=[pl.BlockSpec((1, BLK), lambda i: (0, 0)), row],
                  out_shape=[SDS((1, BLK), F32), SDS((s, d), F32)])[0]


def _adamw_math(w, g, m, v):
    m = ADAM_B1 * m + (1.0 - ADAM_B1) * g
    v = ADAM_B2 * v + (1.0 - ADAM_B2) * (g * g)
    m_hat = m / (1.0 - ADAM_B1 ** ADAM_STEP)
    v_hat = v / (1.0 - ADAM_B2 ** ADAM_STEP)
    delta = -ADAM_LR * (m_hat / (jnp.sqrt(v_hat) + ADAM_EPS) + ADAM_WD * w)
    return delta, m, v


def _row_tile(rows, cap=256):
    best = None
    for t in range(16, min(rows, cap) + 1, 16):
        if rows % t == 0:
            best = t
    return best if best is not None else rows


def _reduce_adamw(recv, w, m, v, name):
    n_l, r, c = w.shape
    tr = _row_tile(r)

    def body(recv_ref, w_ref, m_ref, v_ref, g_ref, d_ref, nm_ref, nv_ref):
        g = recv_ref[0].astype(F32)
        for p in range(1, N_DEV):
            g = g + recv_ref[p].astype(F32)
        g_ref[...] = g
        d_ref[...], nm_ref[...], nv_ref[...] = _adamw_math(w_ref[...], g, m_ref[...], v_ref[...])

    blk = pl.BlockSpec((None, tr, c), lambda l, i: (l, i, 0))
    return _pcall(
        body, (recv, w, m, v), name=name, grid=(n_l, r // tr),
        in_specs=[pl.BlockSpec((N_DEV, None, tr, c), lambda l, i: (0, l, i, 0)), blk, blk, blk],
        out_specs=[blk] * 4, out_shape=[SDS(w.shape, F32)] * 4)[0]


def _reduce_adamw_small(parts, w, m, v, name):
    def body(p_ref, w_ref, m_ref, v_ref, g_ref, d_ref, nm_ref, nv_ref):
        g = p_ref[0]
        for p in range(1, N_DEV):
            g = g + p_ref[p]
        g_ref[...] = g
        d_ref[...], nm_ref[...], nv_ref[...] = _adamw_math(w_ref[...], g, m_ref[...], v_ref[...])

    vm = pl.BlockSpec(memory_space=pltpu.VMEM)
    return pl.pallas_call(body, name=name, in_specs=[vm] * 4, out_specs=[vm] * 4, out_shape=[SDS(w.shape, F32)] * 4,
                          compiler_params=pltpu.CompilerParams(vmem_limit_bytes=VMEM_LIMIT))(parts, w, m, v)


def _rows_of(shape):
    return -(-math.prod(shape) // (8 * BLK)) * 8


def _pack(arrs):
    rows = []
    for a in arrs:
        flat = a.reshape(-1).astype(F32)
        rows.append(jnp.pad(flat, (0, _rows_of(a.shape) * BLK - flat.shape[0])).reshape(-1, BLK))
    return jnp.concatenate(rows, axis=0)


def _unpack(packed, shapes):
    out, row = [], 0
    for shp in shapes:
        n, r = math.prod(shp), _rows_of(shp)
        out.append(packed[row:row + r].reshape(-1)[:n].reshape(shp))
        row += r
    return out


def _block_diag(w):
    nb, bw, _ = w.shape
    eye = jnp.eye(nb, dtype=w.dtype)
    return (eye[:, None, :, None] * w[:, :, None, :]).reshape(nb * bw, nb * bw)


def _diag_blocks(wd, nb):
    bw = wd.shape[0] // nb
    return jnp.stack([wd[b * bw:(b + 1) * bw, b * bw:(b + 1) * bw] for b in range(nb)])


WEIGHT_NAMES = ['ffn1_pre_g', 'ffn1_w_gu', 'ffn1_w_down', 'ffn1_post_g', 'mix_pre_g', 'w_in', 'lru_conv_w', 'lru_conv_b',
                'lru_w_a', 'lru_b_a', 'lru_w_x', 'lru_b_x', 'lru_lambda', 'attn_sinks', 'conv_w', 'conv_b', 'conv_ln_g',
                'conv_ln_b', 'group_g', 'w_out', 'mix_post_g', 'ffn2_pre_g', 'ffn2_w_gu', 'ffn2_w_down', 'ffn2_post_g']
BIG = ('ffn1_w_gu', 'ffn1_w_down', 'w_in', 'w_out', 'ffn2_w_gu', 'ffn2_w_down')
TRANSPOSED = ('ffn1_w_gu', 'ffn2_w_gu', 'w_in')
SMALL = tuple(k for k in WEIGHT_NAMES if k not in BIG)
CHANNEL_SHARDED = ('lru_conv_w', 'conv_w')


def _step(x, target, w, m, v):
    n_l = w['ffn1_pre_g'].shape[0]
    assert n_l == 2, "the exchange schedule below is laid out for two layers"
    s, d = x.shape[1], x.shape[2]
    x = x.reshape(s, d)
    target = target.reshape(s, d)
    me = _my_pos()[3]
    tview = lambda t, k: jnp.swapaxes(t[k], 1, 2) if k in TRANSPOSED else t[k]
    wb = {k: tview(w, k).astype(BF16) for k in BIG}
    vec = lambda name, l: w[name][l][None, :]

    conv_shard = _pack([w['lru_conv_w'], w['conv_w']])
    g0 = _all_gather([(wb['ffn1_w_gu'], 0), (wb['ffn1_w_down'], 0), (wb['w_in'], 0), (wb['w_out'], 0),
                      (conv_shard, None)], "all_gather_first")
    wts = [dict(), dict()]
    wts[0]['ffn1_w_gu'], wts[0]['ffn1_w_down'], wts[0]['w_in'], wts[0]['w_out'], conv_g = g0
    ch = W_A // N_DEV
    conv_parts = [_unpack(conv_g[p], [(n_l, LRU_K, ch), (n_l, CC_K, ch)]) for p in range(N_DEV)]
    lru_cw = jnp.concatenate([cp[0] for cp in conv_parts], axis=-1)
    cc_cw = jnp.concatenate([cp[1] for cp in conv_parts], axis=-1)
    cc_cw = jnp.pad(cc_cw, ((0, 0), (0, CC_HALO - CC_K), (0, 0)))

    gather_plan = {
        ('ffn1', 0): [('A', 'f2_0', ('ffn2_w_gu', 'ffn2_w_down'), 0)],
        ('mix_in', 0): [('B', 'f2_0')],
        ('attn', 0): [('A', 'g1_1', ('ffn1_w_gu',), 1)],
        ('cconv', 0): [('B', 'g1_1')],
        ('ffn2', 0): [('D', None, ('ffn1_w_down',), 1), ('A', 'wi_1', ('w_in',), 1), ('A', 'wo_1', ('w_out',), 1)],
        ('ffn1', 1): [('A', 'f2_1', ('ffn2_w_gu', 'ffn2_w_down'), 1), ('B', 'wi_1'), ('B', 'wo_1')],
        ('mix_in', 1): [('B', 'f2_1')],
    }
    pend = {}

    def fwd(kernel_name, l, fn, *args):
        plan = gather_plan.get((kernel_name, l), [])
        carry = []
        for st in plan:
            if st[0] == 'B':
                carry.append(_gather_b(pend[st[1]][2]))
            else:
                carry.append(_gather_a([(wb[k], st[3]) for k in st[2]], two_level=st[0] == 'A'))
        outs, ex = fn(*args, f"{kernel_name}_fwd_l{l}", carry)
        for st, bufs in zip(plan, ex):
            if st[0] == 'A':
                pend[st[1]] = (st[2], st[3], bufs)
            else:
                names, wl = (st[2], st[3]) if st[0] == 'D' else pend.pop(st[1])[:2]
                for k, b in zip(names, bufs):
                    wts[wl][k] = b
        return outs

    saved = []
    h = x
    for l in range(n_l):
        sv = {'x0': h}
        lw = wts[l]
        x1, sv['h1'], sv['g1'], sv['u1'], sv['d1'] = fwd(
            'ffn1', l, _ffn_fwd, h, vec('ffn1_pre_g', l), vec('ffn1_post_g', l), lw['ffn1_w_gu'], lw['ffn1_w_down'])
        sv['x1'] = x1
        sv['hn'], lx, lg, q, k, vv, glu = fwd('mix_in', l, _mix_in_fwd, x1, vec('mix_pre_g', l),
                                              lw['w_in'].reshape(D_IN_PROJ, d))
        sv.update(lx=lx, lg=lg, q=q, k=k, v=vv, glu=glu)
        lru_p = (lru_cw[l], vec('lru_conv_b', l), _block_diag(w['lru_w_a'][l]).astype(BF16), vec('lru_b_a', l),
                 _block_diag(w['lru_w_x'][l]).astype(BF16), vec('lru_b_x', l), vec('lru_lambda', l))
        cc_p = (cc_cw[l], vec('conv_b', l), vec('conv_ln_g', l), vec('conv_ln_b', l))
        sv.update(lru_p=lru_p, cc_p=cc_p)
        sv['ya'], sv['hs'] = fwd('lru', l, _lru_fwd, lx, lg, lru_p)
        sv['sink_row'] = jnp.repeat(w['attn_sinks'][l], BLK)[None, :]
        (sv['yb'],) = fwd('attn', l, _attn_fwd, q, k, vv, sv['sink_row'])
        (sv['yc'],) = fwd('cconv', l, _cc_fwd, glu, *cc_p)
        x2, sv['o'] = fwd('mix_out', l, _mix_out_fwd, x1, sv['ya'], sv['yb'], sv['yc'], vec('group_g', l),
                          lw['w_out'].reshape(-1, d), vec('mix_post_g', l))
        sv['x2'] = x2
        h, sv['h2'], sv['g2'], sv['u2'], sv['d2'] = fwd(
            'ffn2', l, _ffn_fwd, x2, vec('ffn2_pre_g', l), vec('ffn2_post_g', l), lw['ffn2_w_gu'], lw['ffn2_w_down'])
        saved.append(sv)

    loss_row, dh = _loss_head(h, target, "loss_head")

    recv = {k: None for k in BIG}
    ready = {}
    small = [dict() for _ in range(n_l)]

    def exchange(keys):
        return _grad_x([(ready.pop(key), key[1], recv[key[0]]) for key in keys], n_l)

    def received(keys, bufs):
        for key, b in zip(keys, bufs):
            recv[key[0]] = b

    def run(fn, *args, keys=()):
        outs, ex = fn(*args, carry=[exchange(keys)] if keys else [])
        if keys:
            received(keys, ex[0])
        return outs

    for l in reversed(range(n_l)):
        sv, sg, lw = saved[l], small[l], wts[l]
        keys = [] if l == n_l - 1 else [('ffn1_w_gu', l + 1)]
        dx2, dd, dg, du, sg['ffn2_pre_g'], sg['ffn2_post_g'] = run(
            _ffn_bwd_act, dh, sv['d2'], sv['x2'], vec('ffn2_pre_g', l), vec('ffn2_post_g', l), sv['g2'], sv['u2'],
            lw['ffn2_w_gu'], lw['ffn2_w_down'], f"ffn2_bwd_act_l{l}", keys=keys)
        keys = [] if l == n_l - 1 else [('ffn1_w_down', l + 1)]
        dwg, dwu, dwd = run(_ffn_bwd_w, sv['h2'], dd, sv['g2'], sv['u2'], dg, du, f"ffn2_bwd_w_l{l}", keys=keys)
        ready[('ffn2_w_gu', l)] = [dwg, dwu]
        ready[('ffn2_w_down', l)] = [dwd.reshape(N_DEV, -1, d)]
        dya, dyb, dyc, dw_out, sg['mix_post_g'], sg['group_g'] = run(
            _mix_out_bwd, dx2, sv['o'], sv['ya'], sv['yb'], sv['yc'], vec('group_g', l), lw['w_out'].reshape(-1, d),
            vec('mix_post_g', l), f"mix_out_bwd_l{l}")
        ready[('w_out', l)] = [dw_out.reshape(N_DEV, -1, d)]
        (dlx, dlg, sg['lru_conv_w'], sg['lru_conv_b'], dwa, sg['lru_b_a'], dwx, sg['lru_b_x'],
         sg['lru_lambda']) = run(_lru_bwd, dya, sv['lx'], sv['lg'], sv['hs'], sv['lru_p'], f"lru_bwd_l{l}")
        sg['lru_w_a'] = _diag_blocks(dwa, A_BLOCKS)
        sg['lru_w_x'] = _diag_blocks(dwx, A_BLOCKS)
        dq, dk, dv, dk_up, dv_up, dsk = run(_attn_bwd, dyb, sv['q'], sv['k'], sv['v'], sv['sink_row'],
                                            f"attn_bwd_l{l}", keys=[('ffn2_w_down', l)] if l == 0 else [])
        sg['attn_sinks'] = dsk[:, 0]
        dy1, dcw, sg['conv_b'], sg['conv_ln_g'], sg['conv_ln_b'] = run(
            _cc_bwd_conv, dyc, sv['glu'], *sv['cc_p'], f"cconv_bwd_conv_l{l}")
        sg['conv_w'] = dcw[:CC_K]
        (dglu,) = run(_cc_bwd_glu, dy1, sv['glu'], sv['cc_p'][0], f"cconv_bwd_glu_l{l}")
        dx1, dw_in, sg['mix_pre_g'] = run(
            _mix_in_bwd, dx2, sv['x1'], vec('mix_pre_g', l), sv['hn'], lw['w_in'].reshape(D_IN_PROJ, d),
            dlx, dlg, dq, dk, dk_up, dv, dv_up, dglu, f"mix_in_bwd_l{l}", keys=[('w_out', l)])
        ready[('w_in', l)] = [dw_in.reshape(N_DEV, -1, d)]
        dh, dd, dg, du, sg['ffn1_pre_g'], sg['ffn1_post_g'] = run(
            _ffn_bwd_act, dx1, sv['d1'], sv['x0'], vec('ffn1_pre_g', l), vec('ffn1_post_g', l), sv['g1'], sv['u1'],
            lw['ffn1_w_gu'], lw['ffn1_w_down'], f"ffn1_bwd_act_l{l}", keys=[('ffn2_w_gu', l), ('w_in', l)])
        if l > 0:
            dwg, dwu, dwd = run(_ffn_bwd_w, sv['h1'], dd, sv['g1'], sv['u1'], dg, du, f"ffn1_bwd_w_l{l}",
                                keys=[('ffn2_w_down', l)])
            ready[('ffn1_w_gu', l)] = [dwg, dwu]
            ready[('ffn1_w_down', l)] = [dwd.reshape(N_DEV, -1, d)]
        else:
            part = _pack([jnp.stack([small[j][k] for j in range(n_l)]) for k in SMALL] + [loss_row])
            (recv['ffn1_w_gu'], recv['ffn1_w_down']), ex = _ffn_bwd_w_send(
                sv['h1'], dd, sv['g1'], sv['u1'], dg, du, recv['ffn1_w_gu'], recv['ffn1_w_down'], 0, "ffn1_bwd_w_send_l0",
                [_gather_a([(part, None)], two_level=False)])
            small_parts = ex[0][0]
    grad_x = dh.reshape(1, s, d)

    out = {}
    for k in BIG:
        res = _reduce_adamw(recv[k], tview(w, k), tview(m, k), tview(v, k), f"reduce_adamw_{k}")
        out[k] = [jnp.swapaxes(r, 1, 2) for r in res] if k in TRANSPOSED else res

    small_shapes = [(n_l,) + tuple(small[0][k].shape) for k in SMALL]

    def widen(t, k):
        if k not in CHANNEL_SHARDED:
            return t.reshape((n_l,) + tuple(small[0][k].shape))
        full = jnp.zeros((n_l,) + tuple(small[0][k].shape), F32)
        return lax.dynamic_update_slice_in_dim(full, t, me * ch, axis=2)

    no_w = jnp.zeros(loss_row.shape, F32)
    packed = [_pack([widen(src[k], k) for k in SMALL] + [no_w]) for src in (w, m, v)]
    res = _reduce_adamw_small(small_parts, *packed, "reduce_adamw_small")
    loss = _unpack(res[0], small_shapes + [loss_row.shape])[-1][0, 0]
    for k, g, dlt, nm, nv in zip(SMALL, *[_unpack(r, small_shapes) for r in res]):
        vals = [g, dlt, nm, nv]
        if k in CHANNEL_SHARDED:
            vals = [lax.dynamic_slice_in_dim(t, me * ch, ch, axis=2) for t in vals]
        out[k] = [t.reshape(w[k].shape) for t in vals]

    return (loss, grad_x, *[out[k][0] for k in WEIGHT_NAMES], *[out[k][1] for k in WEIGHT_NAMES],
            *[out[k][2] for k in WEIGHT_NAMES], *[out[k][3] for k in WEIGHT_NAMES])


def kernel(x, ffn1_pre_g, ffn1_w_gu, ffn1_w_down, ffn1_post_g, mix_pre_g, w_in, lru_conv_w, lru_conv_b, lru_w_a, lru_b_a, lru_w_x, lru_b_x, lru_lambda, attn_sinks, conv_w, conv_b, conv_ln_g, conv_ln_b, group_g, w_out, mix_post_g, ffn2_pre_g, ffn2_w_gu, ffn2_w_down, ffn2_post_g, loss_target, m_ffn1_pre_g, m_ffn1_w_gu, m_ffn1_w_down, m_ffn1_post_g, m_mix_pre_g, m_w_in, m_lru_conv_w, m_lru_conv_b, m_lru_w_a, m_lru_b_a, m_lru_w_x, m_lru_b_x, m_lru_lambda, m_attn_sinks, m_conv_w, m_conv_b, m_conv_ln_g, m_conv_ln_b, m_group_g, m_w_out, m_mix_post_g, m_ffn2_pre_g, m_ffn2_w_gu, m_ffn2_w_down, m_ffn2_post_g, v_ffn1_pre_g, v_ffn1_w_gu, v_ffn1_w_down, v_ffn1_post_g, v_mix_pre_g, v_w_in, v_lru_conv_w, v_lru_conv_b, v_lru_w_a, v_lru_b_a, v_lru_w_x, v_lru_b_x, v_lru_lambda, v_attn_sinks, v_conv_w, v_conv_b, v_conv_ln_g, v_conv_ln_b, v_group_g, v_w_out, v_mix_post_g, v_ffn2_pre_g, v_ffn2_w_gu, v_ffn2_w_down, v_ffn2_post_g):
    args = (ffn1_pre_g, ffn1_w_gu, ffn1_w_down, ffn1_post_g, mix_pre_g, w_in, lru_conv_w, lru_conv_b, lru_w_a, lru_b_a, lru_w_x, lru_b_x, lru_lambda, attn_sinks, conv_w, conv_b, conv_ln_g, conv_ln_b, group_g, w_out, mix_post_g, ffn2_pre_g, ffn2_w_gu, ffn2_w_down, ffn2_post_g)
    ms = (m_ffn1_pre_g, m_ffn1_w_gu, m_ffn1_w_down, m_ffn1_post_g, m_mix_pre_g, m_w_in, m_lru_conv_w, m_lru_conv_b, m_lru_w_a, m_lru_b_a, m_lru_w_x, m_lru_b_x, m_lru_lambda, m_attn_sinks, m_conv_w, m_conv_b, m_conv_ln_g, m_conv_ln_b, m_group_g, m_w_out, m_mix_post_g, m_ffn2_pre_g, m_ffn2_w_gu, m_ffn2_w_down, m_ffn2_post_g)
    vs = (v_ffn1_pre_g, v_ffn1_w_gu, v_ffn1_w_down, v_ffn1_post_g, v_mix_pre_g, v_w_in, v_lru_conv_w, v_lru_conv_b, v_lru_w_a, v_lru_b_a, v_lru_w_x, v_lru_b_x, v_lru_lambda, v_attn_sinks, v_conv_w, v_conv_b, v_conv_ln_g, v_conv_ln_b, v_group_g, v_w_out, v_mix_post_g, v_ffn2_pre_g, v_ffn2_w_gu, v_ffn2_w_down, v_ffn2_post_g)
    return _step(x, loss_target, dict(zip(WEIGHT_NAMES, args)), dict(zip(WEIGHT_NAMES, ms)), dict(zip(WEIGHT_NAMES, vs)))
```

```python
import functools
import math
import operator

import jax
import jax.numpy as jnp
from jax import lax
from jax.experimental import pallas as pl
from jax.experimental.pallas import tpu as pltpu

F32 = jnp.float32
BF16 = jnp.bfloat16
N_DEV = 8
AXES = ("x", "y", "c")
MESH = pl.DeviceIdType.MESH

NORM_EPS = 1e-6
LN_EPS = 1e-5
NEG_BIG = -1e30
W_A = 256
W_B = 512
W_C = 256
HEAD_DIM = 64
N_Q_HEADS = 8
N_KV_HEADS = 2
Q_PER_KV = N_Q_HEADS // N_KV_HEADS
KV_W = N_KV_HEADS * HEAD_DIM
BLK = 128
LRU_K = 4
LRU_C = 8.0
A_BLOCKS = 4
CC_K = 31
CC_HALO = 32
LRU_HALO = 8
D_IN_PROJ = 2 * W_A + W_B + 2 * KV_W + 2 * W_C
ADAM_LR = 0.001
ADAM_B1 = 0.9
ADAM_B2 = 0.999
ADAM_EPS = 1e-08
ADAM_WD = 0.01
ADAM_STEP = 10
VMEM_LIMIT = 56 * 1024 * 1024

SDS = jax.ShapeDtypeStruct
ANY = pl.BlockSpec(memory_space=pl.ANY)


def _time_tile(s):
    return max(BLK, s // 8)


class _Exchange:
    def __init__(self, inputs, out_shapes, aliases, sem_shapes, start, wait):
        self.inputs, self.out_shapes, self.aliases, self.sem_shapes = inputs, out_shapes, aliases, sem_shapes
        self.start, self.wait = start, wait


def _my_pos():
    x, y, c = (lax.axis_index(a) for a in AXES)
    return x, y, c, 4 * x + 2 * y + c


def _flip(k):
    x, y, c, _ = _my_pos()
    return (1 - x if k & 4 else x, 1 - y if k & 2 else y, 1 - c if k & 1 else c)


def _slot(dev):
    return 4 * dev[0] + 2 * dev[1] + dev[2]


def _dev(p):
    return (p >> 2, (p >> 1) & 1, p & 1)


def _gather_a(items, two_level):
    rels = (1, 2, 4, 6) if two_level else tuple(range(1, N_DEV))
    n = len(items)
    src_of = lambda ins, a: ins[a] if items[a][1] is None else ins[a].at[items[a][1]]

    def shape_of(a):
        arr, l = items[a]
        return arr.shape if l is None else arr.shape[1:]

    def copies(ins, outs, sems, a):
        send, recv, _ = sems
        me = _my_pos()[3]
        return [(k, pltpu.make_async_remote_copy(
            src_ref=src_of(ins, a), dst_ref=outs[a].at[me], send_sem=send.at[a, k], recv_sem=recv.at[a, k],
            device_id=_flip(k), device_id_type=MESH)) for k in rels]

    def local(ins, outs, sems, a):
        return pltpu.make_async_copy(src_of(ins, a), outs[a].at[_my_pos()[3]], sems[2].at[a])

    def start(ins, outs, sems):
        for a in range(n):
            local(ins, outs, sems, a).start()
            for _, cp in copies(ins, outs, sems, a):
                cp.start()

    def wait(ins, outs, sems):
        send, recv, _ = sems
        for a in range(n):
            for k, cp in copies(ins, outs, sems, a):
                pltpu.make_async_remote_copy(
                    src_ref=src_of(ins, a), dst_ref=outs[a].at[_slot(_flip(k))], send_sem=send.at[a, k],
                    recv_sem=recv.at[a, k], device_id=_flip(k), device_id_type=MESH).wait_recv()
                cp.wait_send()
            local(ins, outs, sems, a).wait()

    return _Exchange([it[0] for it in items], [SDS((N_DEV,) + shape_of(a), items[a][0].dtype) for a in range(n)], {},
                     [pltpu.SemaphoreType.DMA((n, N_DEV)), pltpu.SemaphoreType.DMA((n, N_DEV)),
                      pltpu.SemaphoreType.DMA((n,))], start, wait)


def _gather_b(bufs):
    n = len(bufs)

    def copies(ins, outs, sems, a, c_of_block):
        send, recv = sems
        x, y, c, _ = _my_pos()
        res = []
        for k in (2, 4, 6):
            chip = _flip(k)
            blk = _slot((chip[0], chip[1], c if c_of_block == "mine" else 1 - c))
            res.append(pltpu.make_async_remote_copy(
                src_ref=ins[a].at[blk], dst_ref=outs[a].at[blk], send_sem=send.at[a, k], recv_sem=recv.at[a, k],
                device_id=_flip(1), device_id_type=MESH))
        return res

    def start(ins, outs, sems):
        for a in range(n):
            for cp in copies(ins, outs, sems, a, "mine"):
                cp.start()

    def wait(ins, outs, sems):
        for a in range(n):
            for cp in copies(ins, outs, sems, a, "sibling"):
                cp.wait_recv()
            for cp in copies(ins, outs, sems, a, "mine"):
                cp.wait_send()

    return _Exchange(list(bufs), [SDS(b.shape, b.dtype) for b in bufs], {a: a for a in range(n)},
                     [pltpu.SemaphoreType.DMA((n, N_DEV)), pltpu.SemaphoreType.DMA((n, N_DEV))], start, wait)


def _grad_x(items, n_l):
    n = len(items)
    inputs, first_in, recv_in, aliases, out_shapes = [], [], [], {}, []
    for a, (arrs, l, recv) in enumerate(items):
        first_in.append(len(inputs))
        inputs += list(arrs)
        assert sum(arr.shape[0] for arr in arrs) == N_DEV
        if recv is not None:
            aliases[len(inputs)] = a
            inputs.append(recv)
        out_shapes.append(SDS((N_DEV, n_l) + arrs[0].shape[1:], arrs[0].dtype))

    def slab(ins, a, p):
        off = 0
        for j, arr in enumerate(items[a][0]):
            if p < off + arr.shape[0]:
                return ins[first_in[a] + j].at[p - off]
            off += arr.shape[0]
        raise AssertionError

    def rdma(ins, outs, sems, a, p, src_dev):
        send, recv, _ = sems
        return pltpu.make_async_remote_copy(
            src_ref=slab(ins, a, p), dst_ref=outs[a].at[src_dev, items[a][1]], send_sem=send.at[a, p],
            recv_sem=recv.at[a, src_dev], device_id=_dev(p), device_id_type=MESH)

    def local(ins, outs, sems, a, p):
        return pltpu.make_async_copy(slab(ins, a, p), outs[a].at[p, items[a][1]], sems[2].at[a])

    def start(ins, outs, sems):
        me = _my_pos()[3]
        for p in range(N_DEV):
            @pl.when(me != p)
            def _():
                for a in range(n):
                    rdma(ins, outs, sems, a, p, me).start()

            @pl.when(me == p)
            def _():
                for a in range(n):
                    local(ins, outs, sems, a, p).start()

    def wait(ins, outs, sems):
        me = _my_pos()[3]
        for p in range(N_DEV):
            @pl.when(me != p)
            def _():
                for a in range(n):
                    rdma(ins, outs, sems, a, p, p).wait_recv()
                    rdma(ins, outs, sems, a, p, p).wait_send()

            @pl.when(me == p)
            def _():
                for a in range(n):
                    local(ins, outs, sems, a, p).wait()

    return _Exchange(inputs, out_shapes, aliases,
                     [pltpu.SemaphoreType.DMA((n, N_DEV)), pltpu.SemaphoreType.DMA((n, N_DEV)),
                      pltpu.SemaphoreType.DMA((n,))], start, wait)


def _pcall(body, args, *, name, grid, in_specs, out_specs, out_shape, scratch_shapes=(), carry=(), body_aliases=None):
    n_in, n_out, n_scr = len(in_specs), len(out_specs), len(scratch_shapes)
    c_in = [len(e.inputs) for e in carry]
    c_out = [len(e.out_shapes) for e in carry]
    c_sem = [len(e.sem_shapes) for e in carry]
    aliases = dict(body_aliases or {})
    for j, e in enumerate(carry):
        for i_loc, o_loc in e.aliases.items():
            aliases[n_in + sum(c_in[:j]) + i_loc] = n_out + sum(c_out[:j]) + o_loc

    def wrapped(*refs):
        def take(counts, pos):
            groups = []
            for cnt in counts:
                groups.append(refs[pos:pos + cnt])
                pos += cnt
            return groups, pos

        (ins,), pos = take([n_in], 0)
        cins, pos = take(c_in, pos)
        (outs,), pos = take([n_out], pos)
        couts, pos = take(c_out, pos)
        (scr,), pos = take([n_scr], pos)
        csems, pos = take(c_sem, pos)
        if carry:
            ids = [pl.program_id(k) for k in range(len(grid))]
            first = functools.reduce(operator.and_, [i == 0 for i in ids])
            last = functools.reduce(operator.and_, [i == g - 1 for i, g in zip(ids, grid)])

            @pl.when(first)
            def _():
                for e, ci, co, cs in zip(carry, cins, couts, csems):
                    e.start(ci, co, cs)

        body(*ins, *outs, *scr)
        if carry:
            @pl.when(last)
            def _():
                for e, ci, co, cs in zip(carry, cins, couts, csems):
                    e.wait(ci, co, cs)

    res = pl.pallas_call(
        wrapped, name=name, grid=grid,
        in_specs=list(in_specs) + [ANY] * sum(c_in),
        out_specs=list(out_specs) + [ANY] * sum(c_out),
        out_shape=list(out_shape) + [s for e in carry for s in e.out_shapes],
        scratch_shapes=list(scratch_shapes) + [s for e in carry for s in e.sem_shapes],
        input_output_aliases=aliases,
        compiler_params=pltpu.CompilerParams(dimension_semantics=("arbitrary",) * len(grid),
                                             vmem_limit_bytes=VMEM_LIMIT),
    )(*args, *[a for e in carry for a in e.inputs])
    outs, pos, extra = list(res[:n_out]), n_out, []
    for cnt in c_out:
        extra.append(list(res[pos:pos + cnt]))
        pos += cnt
    return outs, extra


def _all_gather(items, name):
    n = len(items)
    shape_of = lambda a: items[a][0].shape if items[a][1] is None else items[a][0].shape[1:]

    def body(*refs):
        ins, outs, (send_sems, recv_sems, local_sems) = refs[:n], refs[n:2 * n], refs[2 * n:]
        x, y, c, me = _my_pos()
        src_of = lambda a: ins[a] if items[a][1] is None else ins[a].at[items[a][1]]

        def copy(a, k, block, to, src=None):
            dst = outs[a].at[_slot(block)]
            return pltpu.make_async_remote_copy(
                src_ref=dst if src is None else src, dst_ref=dst,
                send_sem=send_sems.at[a, k], recv_sem=recv_sems.at[a, k], device_id=to, device_id_type=MESH)

        mine = [pltpu.make_async_copy(src_of(a), outs[a].at[me], local_sems.at[a]) for a in range(n)]
        for cp in mine:
            cp.start()
        first = [copy(a, k, (x, y, c), _flip(k), src=src_of(a)) for a in range(n) for k in (1, 2, 4, 6)]
        for cp in first:
            cp.start()
        passed = []
        for k in (2, 4, 6):
            for a in range(n):
                copy(a, k, _flip(k), (x, y, c)).wait_recv()
                fwd = copy(a, k + 1, _flip(k), _flip(1))
                fwd.start()
                passed.append(fwd)
        for a in range(n):
            copy(a, 1, _flip(1), (x, y, c)).wait_recv()
            for k in (2, 4, 6):
                copy(a, k + 1, _flip(k + 1), (x, y, c)).wait_recv()
        for cp in first + passed:
            cp.wait_send()
        for cp in mine:
            cp.wait()

    return pl.pallas_call(
        body, name=name,
        in_specs=[ANY] * n, out_specs=[ANY] * n,
        out_shape=[SDS((N_DEV,) + shape_of(a), items[a][0].dtype) for a in range(n)],
        scratch_shapes=[pltpu.SemaphoreType.DMA((n, N_DEV)), pltpu.SemaphoreType.DMA((n, N_DEV)),
                        pltpu.SemaphoreType.DMA((n,))],
    )(*[it[0] for it in items])


def _mm(a, b):
    return jnp.dot(a.astype(BF16), b.astype(BF16), preferred_element_type=F32)


def _mm_nt(a, b):
    return lax.dot_general(a.astype(BF16), b.astype(BF16), (((1,), (1,)), ((), ())), preferred_element_type=F32)


def _mm_tn(a, b):
    return lax.dot_general(a.astype(BF16), b.astype(BF16), (((0,), (0,)), ((), ())), preferred_element_type=F32)


def _rms_r(x):
    return lax.rsqrt(jnp.mean(x * x, axis=-1, keepdims=True) + NORM_EPS)


def _rms_bwd(x, r, g, dy):
    gy = dy * g
    dx = r * (gy - x * (r * r) * jnp.mean(gy * x, axis=-1, keepdims=True))
    dg = jnp.sum(dy * x * r, axis=0, keepdims=True)
    return dx, dg


def _sigmoid(x):
    return 1.0 / (1.0 + jnp.exp(-x))


def _dsilu(z, sz):
    return sz * (1.0 + z * (1.0 - sz))


def _swiglu_bf16(g, u):
    sg = 0.5 * jnp.tanh(0.5 * g) + 0.5
    silu = g * sg
    return silu * u, silu, sg + silu * (1.0 - sg)


_GELU_C = math.sqrt(2.0 / math.pi)


def _gelu(x):
    t = jnp.tanh(_GELU_C * (x + 0.044715 * x * x * x))
    return 0.5 * x * (1.0 + t), t


def _dgelu(x, t):
    return 0.5 * (1.0 + t) + 0.5 * x * (1.0 - t * t) * _GELU_C * (1.0 + 3.0 * 0.044715 * x * x)


def _log1p(e):
    return jnp.where(e < 1e-2, e * (1.0 - e * (0.5 - e * (1.0 / 3.0))), jnp.log(1.0 + e))


def _softplus(x):
    return jnp.maximum(x, 0.0) + _log1p(jnp.exp(-jnp.abs(x)))


def _neg_expm1(x):
    small = -x * (1.0 + x * (0.5 + x * (1.0 / 6.0) * (1.0 + x * 0.25)))
    return jnp.where(x > -1e-2, small, 1.0 - jnp.exp(x))


def _shift_down(x, s):
    return x if s == 0 else pltpu.roll(x, s, 0)


def _shift_up(x, s):
    return x if s == 0 else pltpu.roll(x, x.shape[0] - s, 0)


def _ffn_wspecs(d, fc, order):
    f_of = (lambda i, f: f) if order == "tf" else (lambda f, i: f)
    n_f = N_DEV // 2
    return [pl.BlockSpec((None, fc, d), lambda *g: (f_of(*g), 0, 0)),
            pl.BlockSpec((None, fc, d), lambda *g: (f_of(*g) + n_f, 0, 0)),
            pl.BlockSpec((2, fc // 2, d), lambda *g: (f_of(*g), 0, 0))]


def _ffn_fwd(x, pre_g, post_g, wgu_t, wd, name, carry=()):
    s, d = x.shape
    fc = wgu_t.shape[1]
    ts = 2 * _time_tile(s)
    n_t, n_f = s // ts, N_DEV // 2

    def body(x_ref, pg_ref, qg_ref, wg_ref, wu_ref, wd_ref, xo_ref, h_ref, g_ref, u_ref, d_ref, h_scr, acc):
        f = pl.program_id(1)

        @pl.when(f == 0)
        def _():
            xv = x_ref[...]
            hv = (xv * _rms_r(xv) * pg_ref[...]).astype(BF16)
            h_scr[...] = hv
            h_ref[...] = hv
            acc[...] = jnp.zeros_like(acc)

        hv = h_scr[...]
        g = _mm_nt(hv, wg_ref[...])
        u = _mm_nt(hv, wu_ref[...])
        g = g.astype(BF16)
        u = u.astype(BF16)
        g_ref[...] = g
        u_ref[...] = u
        acc[...] += jnp.dot(_swiglu_bf16(g, u)[0], wd_ref[...].reshape(fc, d), preferred_element_type=F32)

        @pl.when(f == n_f - 1)
        def _():
            dv = acc[...]
            d_ref[...] = dv.astype(BF16)
            xo_ref[...] = x_ref[...] + 0.5 * (dv * _rms_r(dv) * qg_ref[...])

    row = pl.BlockSpec((ts, d), lambda i, f: (i, 0))
    vec = pl.BlockSpec((1, d), lambda i, f: (0, 0))
    act = pl.BlockSpec((None, ts, fc), lambda i, f: (f, i, 0))
    return _pcall(
        body, (x, pre_g, post_g, wgu_t, wgu_t, wd), name=name, grid=(n_t, n_f),
        in_specs=[row, vec, vec] + _ffn_wspecs(d, fc, "tf"),
        out_specs=[row, row, act, act, row],
        out_shape=[SDS((s, d), F32), SDS((s, d), BF16), SDS((n_f, s, fc), BF16), SDS((n_f, s, fc), BF16),
                   SDS((s, d), BF16)],
        scratch_shapes=[pltpu.VMEM((ts, d), BF16), pltpu.VMEM((ts, d), F32)], carry=carry)


def _ffn_bwd_act(dxo, dmid, x, pre_g, post_g, g_s, u_s, wgu_t, wd, name, carry=()):
    s, d = x.shape
    fc = wgu_t.shape[1]
    ts = _time_tile(s)
    n_t, n_f = s // ts, N_DEV // 2

    def body(dxo_ref, dm_ref, x_ref, pg_ref, qg_ref, g_ref, u_ref, wg_ref, wu_ref, wd_ref,
             dx_ref, dd_ref, dg_ref, du_ref, dpg_ref, dqg_ref, dd_scr, dh_acc):
        i, f = pl.program_id(0), pl.program_id(1)

        @pl.when((i == 0) & (f == 0))
        def _():
            dpg_ref[...] = jnp.zeros_like(dpg_ref)
            dqg_ref[...] = jnp.zeros_like(dqg_ref)

        @pl.when(f == 0)
        def _():
            dv = dm_ref[...].astype(F32)
            ddv, dq = _rms_bwd(dv, _rms_r(dv), qg_ref[...], 0.5 * dxo_ref[...])
            dqg_ref[...] += dq
            dd_scr[...] = ddv.astype(BF16)
            dd_ref[...] = ddv.astype(BF16)
            dh_acc[...] = jnp.zeros_like(dh_acc)

        da = _mm_nt(dd_scr[...], wd_ref[...].reshape(fc, d)).astype(BF16)
        u = u_ref[...]
        _, silu, dsilu = _swiglu_bf16(g_ref[...], u)
        du = da * silu
        dg = da * u * dsilu
        dg_ref[...] = dg
        du_ref[...] = du
        dh_acc[...] += _mm(dg, wg_ref[...]) + _mm(du, wu_ref[...])

        @pl.when(f == n_f - 1)
        def _():
            xv = x_ref[...]
            dxv, dp = _rms_bwd(xv, _rms_r(xv), pg_ref[...], dh_acc[...])
            dpg_ref[...] += dp
            dx_ref[...] = dxo_ref[...] + dxv

    row = pl.BlockSpec((ts, d), lambda i, f: (i, 0))
    vec = pl.BlockSpec((1, d), lambda i, f: (0, 0))
    act = pl.BlockSpec((None, ts, fc), lambda i, f: (f, i, 0))
    return _pcall(
        body, (dxo, dmid, x, pre_g, post_g, g_s, u_s, wgu_t, wgu_t, wd), name=name, grid=(n_t, n_f),
        in_specs=[row, row, row, vec, vec, act, act] + _ffn_wspecs(d, fc, "tf"),
        out_specs=[row, row, act, act, vec, vec],
        out_shape=[SDS((s, d), F32), SDS((s, d), BF16), SDS((n_f, s, fc), BF16), SDS((n_f, s, fc), BF16),
                   SDS((1, d), F32), SDS((1, d), F32)],
        scratch_shapes=[pltpu.VMEM((ts, d), BF16), pltpu.VMEM((ts, d), F32)], carry=carry)


def _ffn_bwd_w(h, dd, g_s, u_s, dg, du, name, carry=()):
    s, d = h.shape
    n_f, _, fc = g_s.shape
    ts = _time_tile(s)
    n_t = s // ts

    def body(h_ref, dd_ref, g_ref, u_ref, dg_ref, du_ref, wg_ref, wu_ref, wd_ref, acc_g, acc_u, acc_d):
        i = pl.program_id(1)

        @pl.when(i == 0)
        def _():
            acc_g[...] = jnp.zeros_like(acc_g)
            acc_u[...] = jnp.zeros_like(acc_u)
            acc_d[...] = jnp.zeros_like(acc_d)

        a = _swiglu_bf16(g_ref[...], u_ref[...])[0]
        hv = h_ref[...]
        acc_g[...] += _mm_tn(dg_ref[...], hv)
        acc_u[...] += _mm_tn(du_ref[...], hv)
        acc_d[...] += _mm_tn(a, dd_ref[...])

        @pl.when(i == n_t - 1)
        def _():
            wg_ref[...] = acc_g[...].astype(BF16)
            wu_ref[...] = acc_u[...].astype(BF16)
            wd_ref[...] = acc_d[...].astype(BF16)

    row = pl.BlockSpec((ts, d), lambda f, i: (i, 0))
    act = pl.BlockSpec((None, ts, fc), lambda f, i: (f, i, 0))
    out = pl.BlockSpec((None, fc, d), lambda f, i: (f, 0, 0))
    return _pcall(
        body, (h, dd, g_s, u_s, dg, du), name=name, grid=(n_f, n_t),
        in_specs=[row, row, act, act, act, act], out_specs=[out, out, out],
        out_shape=[SDS((n_f, fc, d), BF16)] * 3,
        scratch_shapes=[pltpu.VMEM((fc, d), F32)] * 3, carry=carry)


def _ffn_bwd_w_send(h, dd, g_s, u_s, dg, du, recv_gu, recv_d, layer, name, carry=()):
    s, d = h.shape
    n_f, _, fc = g_s.shape
    ts = _time_tile(s)
    n_t = s // ts
    half = fc // 2

    def chunk_of(step):
        return (step + 2 * lax.axis_index("x") + lax.axis_index("y")) % n_f

    def body(h_ref, dd_ref, g_ref, u_ref, dg_ref, du_ref, _rgu_in, _rd_in, rgu_ref, rd_ref,
             acc_g, acc_u, acc_d, st_g, st_u, st_d, send_sems, recv_sems, local_sems):
        f, i = pl.program_id(0), pl.program_id(1)
        me = _my_pos()[3]

        @pl.when(i == 0)
        def _():
            acc_g[...] = jnp.zeros_like(acc_g)
            acc_u[...] = jnp.zeros_like(acc_u)
            acc_d[...] = jnp.zeros_like(acc_d)

        a = _swiglu_bf16(g_ref[...], u_ref[...])[0]
        hv = h_ref[...]
        acc_g[...] += _mm_tn(dg_ref[...], hv)
        acc_u[...] += _mm_tn(du_ref[...], hv)
        acc_d[...] += _mm_tn(a, dd_ref[...])

        def messages(fs):
            c = chunk_of(fs)
            return [(st_g.at[fs], rgu_ref, 0, c, 0), (st_u.at[fs], rgu_ref, 0, c + n_f, 1),
                    (st_d.at[fs, pl.ds(0, half)], rd_ref, 1, 2 * c, 2),
                    (st_d.at[fs, pl.ds(half, half)], rd_ref, 1, 2 * c + 1, 3)]

        def remote(fs, msg, src_dev):
            src, buf, row, p, j = msg
            return pltpu.make_async_remote_copy(
                src_ref=src, dst_ref=buf.at[src_dev, layer], send_sem=send_sems.at[fs, j],
                recv_sem=recv_sems.at[row, src_dev], device_id=_dev(p), device_id_type=MESH)

        def local(fs, msg):
            src, buf, _, p, j = msg
            return pltpu.make_async_copy(src, buf.at[p, layer], local_sems.at[fs, j])

        for fs in range(n_f):
            @pl.when((f == fs) & (i == n_t - 1))
            def _():
                st_g[fs] = acc_g[...].astype(BF16)
                st_u[fs] = acc_u[...].astype(BF16)
                st_d[fs] = acc_d[...].astype(BF16)
                for msg in messages(fs):
                    @pl.when(me != msg[3])
                    def _():
                        remote(fs, msg, me).start()

                    @pl.when(me == msg[3])
                    def _():
                        local(fs, msg).start()

        @pl.when((f == n_f - 1) & (i == n_t - 1))
        def _():
            for fs in range(n_f):
                for msg in messages(fs):
                    @pl.when(me != msg[3])
                    def _():
                        remote(fs, msg, me).wait_send()

                    @pl.when(me == msg[3])
                    def _():
                        local(fs, msg).wait()
            for src_dev in range(N_DEV):
                @pl.when(me != src_dev)
                def _():
                    remote(0, messages(0)[0], src_dev).wait_recv()
                    remote(0, messages(0)[2], src_dev).wait_recv()

    row = pl.BlockSpec((ts, d), lambda f, i: (i, 0))
    act = pl.BlockSpec((None, ts, fc), lambda f, i: (chunk_of(f), i, 0))
    return _pcall(
        body, (h, dd, g_s, u_s, dg, du, recv_gu, recv_d), name=name, grid=(n_f, n_t),
        in_specs=[row, row, act, act, act, act, ANY, ANY], out_specs=[ANY, ANY],
        out_shape=[SDS(recv_gu.shape, recv_gu.dtype), SDS(recv_d.shape, recv_d.dtype)],
        scratch_shapes=[pltpu.VMEM((fc, d), F32)] * 3 + [pltpu.VMEM((n_f, fc, d), BF16)] * 3
        + [pltpu.SemaphoreType.DMA((n_f, 4)), pltpu.SemaphoreType.DMA((2, N_DEV)), pltpu.SemaphoreType.DMA((n_f, 4))],
        carry=carry, body_aliases={6: 0, 7: 1})


_PROJ_WIDTHS = (W_A, W_A, W_B, KV_W, KV_W, 2 * W_C)


def _mix_in_fwd(x, pre_g, w_in_t, name, carry=()):
    s, d = x.shape
    ts = _time_tile(s)

    def body(x_ref, pg_ref, w_ref, hn_ref, *outs):
        xv = x_ref[...]
        hn = (xv * _rms_r(xv) * pg_ref[...]).astype(BF16)
        hn_ref[...] = hn
        proj = _mm_nt(hn, w_ref[...])
        off = 0
        for o_ref, w in zip(outs, _PROJ_WIDTHS):
            o_ref[...] = proj[:, off:off + w]
            off += w

    row = lambda w: pl.BlockSpec((ts, w), lambda i: (i, 0))
    return _pcall(
        body, (x, pre_g, w_in_t), name=name, grid=(s // ts,),
        in_specs=[row(d), pl.BlockSpec((1, d), lambda i: (0, 0)), pl.BlockSpec((D_IN_PROJ, d), lambda i: (0, 0))],
        out_specs=[row(d)] + [row(w) for w in _PROJ_WIDTHS],
        out_shape=[SDS((s, d), BF16)] + [SDS((s, w), F32) for w in _PROJ_WIDTHS], carry=carry)


def _mix_in_bwd(dres, x, pre_g, hn, w_in_t, dlx, dlg, dq, dk, dk_up, dv, dv_up, dglu, name, carry=()):
    s, d = x.shape
    ts = _time_tile(s)
    n_t = s // ts

    def body(dres_ref, x_ref, pg_ref, hn_ref, w_ref, dlx_ref, dlg_ref, dq_ref, dk_ref, dkn_ref,
             dv_ref, dvn_ref, dglu_ref, dx_ref, dw_ref, dpg_ref, acc):
        i = pl.program_id(0)

        @pl.when(i == 0)
        def _():
            acc[...] = jnp.zeros_like(acc)
            dpg_ref[...] = jnp.zeros_like(dpg_ref)

        def with_next(cur_ref, nxt_ref):
            nxt = jnp.where(i < n_t - 1, nxt_ref[...], 0.0)
            if ts == BLK:
                return cur_ref[...] + nxt
            return jnp.concatenate([cur_ref[:ts - BLK, :], cur_ref[ts - BLK:, :] + nxt], axis=0)

        dproj = jnp.concatenate([dlx_ref[...], dlg_ref[...], dq_ref[...], with_next(dk_ref, dkn_ref),
                                 with_next(dv_ref, dvn_ref), dglu_ref[...]], axis=1).astype(BF16)
        dhn = _mm(dproj, w_ref[...])
        acc[...] += _mm_tn(dproj, hn_ref[...])
        xv = x_ref[...]
        dxv, dp = _rms_bwd(xv, _rms_r(xv), pg_ref[...], dhn)
        dpg_ref[...] += dp
        dx_ref[...] = dres_ref[...] + dxv

        @pl.when(i == n_t - 1)
        def _():
            dw_ref[...] = acc[...].astype(BF16)

    row = lambda w: pl.BlockSpec((ts, w), lambda i: (i, 0))
    nxt = pl.BlockSpec((BLK, KV_W), lambda i: (jnp.minimum(i + 1, n_t - 1), 0))
    vec = pl.BlockSpec((1, d), lambda i: (0, 0))
    full = pl.BlockSpec((D_IN_PROJ, d), lambda i: (0, 0))
    return _pcall(
        body, (dres, x, pre_g, hn, w_in_t, dlx, dlg, dq, dk, dk_up, dv, dv_up, dglu), name=name, grid=(n_t,),
        in_specs=[row(d), row(d), vec, row(d), full, row(W_A), row(W_A), row(W_B), row(KV_W), nxt,
                  row(KV_W), nxt, row(2 * W_C)],
        out_specs=[row(d), full, vec],
        out_shape=[SDS((s, d), F32), SDS((D_IN_PROJ, d), BF16), SDS((1, d), F32)],
        scratch_shapes=[pltpu.VMEM((D_IN_PROJ, d), F32)], carry=carry)


def _lru_gates(xc, lru_p):
    cw_ref, cb_ref, wa_ref, ba_ref, wx_ref, bx_ref, lam_ref = lru_p
    c = cb_ref[...]
    for j in range(LRU_K):
        c = c + cw_ref[j:j + 1, :] * _shift_down(xc, LRU_K - 1 - j)[LRU_HALO:, :]
    r = _sigmoid(_mm(c, wa_ref[...]) + ba_ref[...])
    ig = _sigmoid(_mm(c, wx_ref[...]) + bx_ref[...])
    sp = _softplus(-lam_ref[...])
    log_a = -LRU_C * r * sp
    a = jnp.exp(log_a)
    m = jnp.sqrt(_neg_expm1(2.0 * log_a))
    return c, r, ig, sp, a, m


def _lru_pspecs():
    small = lambda r: pl.BlockSpec((r, W_A), lambda i: (0, 0))
    return [small(LRU_K), small(1), small(W_A), small(1), small(W_A), small(1), small(1)]


def _lru_fwd(lx, lg, lru_p, name, carry=()):
    s = lx.shape[0]
    ts = _time_tile(s)
    n8 = ts // LRU_HALO

    def body(lx_ref, lxp_ref, lg_ref, *rest):
        lru_p, (ya_ref, h_ref, hcarry) = rest[:7], rest[7:]
        i = pl.program_id(0)
        prev = jnp.where(i > 0, lxp_ref[...], 0.0)
        xc = jnp.concatenate([prev, lx_ref[...]], axis=0)
        c, r, ig, sp, a, m = _lru_gates(xc, lru_p)
        acc_a, acc_b = a, m * (ig * c)
        t = lax.broadcasted_iota(jnp.int32, a.shape, 0)
        k = 1
        while k < ts:
            keep = t >= k
            acc_b = jnp.where(keep, acc_a * _shift_down(acc_b, k) + acc_b, acc_b)
            acc_a = jnp.where(keep, acc_a * _shift_down(acc_a, k), acc_a)
            k *= 2
        h0 = jnp.where(i > 0, hcarry[...], 0.0)
        h = acc_b + acc_a * h0
        hcarry[...] = h[ts - 1:ts, :]
        h_ref[...] = h
        ya_ref[...] = _gelu(lg_ref[...])[0] * h

    row = pl.BlockSpec((ts, W_A), lambda i: (i, 0))
    prev8 = pl.BlockSpec((LRU_HALO, W_A), lambda i: (jnp.maximum(i * n8 - 1, 0), 0))
    return _pcall(
        body, (lx, lx, lg, *lru_p), name=name, grid=(s // ts,),
        in_specs=[row, prev8, row] + _lru_pspecs(), out_specs=[row, row],
        out_shape=[SDS((s, W_A), F32), SDS((s, W_A), F32)],
        scratch_shapes=[pltpu.VMEM((1, W_A), F32)], carry=carry)


def _lru_bwd(dya, lx, lg, h_s, lru_p, name, carry=()):
    s = lx.shape[0]
    ts = _time_tile(s)
    n_t = s // ts
    n8 = ts // LRU_HALO

    def body(dya_ref, lx_ref, lxp_ref, lg_ref, h_ref, hp_ref, *rest):
        lru_p = rest[:7]
        (dlx_ref, dlg_ref, dcw_ref, dcb_ref, dwa_ref, dba_ref, dwx_ref, dbx_ref, dlam_ref,
         carry_a, carry_l, carry_dc) = rest[7:]
        cw_ref, _, wa_ref, _, wx_ref, _, lam_ref = lru_p
        i = pl.program_id(0)
        first_tile = i == n_t - 1
        last_tile = i == 0

        @pl.when(i == 0)
        def _():
            for ref in (dcw_ref, dcb_ref, dwa_ref, dba_ref, dwx_ref, dbx_ref, dlam_ref):
                ref[...] = jnp.zeros_like(ref)

        prev = jnp.where(first_tile, 0.0, lxp_ref[...])
        xc = jnp.concatenate([prev, lx_ref[...]], axis=0)
        c, r, ig, sp, a, m = _lru_gates(xc, lru_p)
        h = h_ref[...]
        hcat = jnp.concatenate([jnp.where(first_tile, 0.0, hp_ref[...]), h], axis=0)
        h_m1 = _shift_down(hcat, 1)[LRU_HALO:, :]
        lg = lg_ref[...]
        ge, th = _gelu(lg)
        dya = dya_ref[...]
        dlg_ref[...] = dya * h * _dgelu(lg, th)
        dh = dya * ge
        t = lax.broadcasted_iota(jnp.int32, a.shape, 0)
        a_next = jnp.where(t < ts - 1, _shift_up(a, 1), jnp.where(last_tile, 0.0, carry_a[...]))
        acc_a, acc_b = a_next, dh
        k = 1
        while k < ts:
            keep = t < ts - k
            acc_b = jnp.where(keep, acc_a * _shift_up(acc_b, k) + acc_b, acc_b)
            acc_a = jnp.where(keep, acc_a * _shift_up(acc_a, k), acc_a)
            k *= 2
        lam_beyond = jnp.where(last_tile, 0.0, carry_l[...])
        lmb = acc_b + acc_a * lam_beyond
        carry_a[...] = a[0:1, :]
        carry_l[...] = lmb[0:1, :]
        gi = ig * c
        dgi = lmb * m
        dla = lmb * h_m1 * a - (lmb * gi) * (a * a) / m
        dr = dla * (-LRU_C * sp)
        dsp = jnp.sum(dla * (-LRU_C * r), axis=0, keepdims=True)
        dlam_ref[...] += -dsp * _sigmoid(-lam_ref[...])
        dra = dr * r * (1.0 - r)
        dia = dgi * c * ig * (1.0 - ig)
        dc = dgi * ig + _mm_nt(dra, wa_ref[...]) + _mm_nt(dia, wx_ref[...])
        dwa_ref[...] += _mm_tn(c, dra)
        dwx_ref[...] += _mm_tn(c, dia)
        dba_ref[...] += jnp.sum(dra, axis=0, keepdims=True)
        dbx_ref[...] += jnp.sum(dia, axis=0, keepdims=True)
        dcb_ref[...] += jnp.sum(dc, axis=0, keepdims=True)
        dcc = jnp.concatenate([dc, jnp.where(last_tile, 0.0, carry_dc[...])], axis=0)
        carry_dc[...] = dc[0:LRU_HALO, :]
        dlx = jnp.zeros_like(dc)
        for j in range(LRU_K):
            sh = LRU_K - 1 - j
            dcw_ref[j:j + 1, :] += jnp.sum(dc * _shift_down(xc, sh)[LRU_HALO:, :], axis=0, keepdims=True)
            dlx = dlx + cw_ref[j:j + 1, :] * _shift_up(dcc, sh)[:ts, :]
        dlx_ref[...] = dlx

    row = pl.BlockSpec((ts, W_A), lambda i: (n_t - 1 - i, 0))
    prev8 = pl.BlockSpec((LRU_HALO, W_A), lambda i: (jnp.maximum((n_t - 1 - i) * n8 - 1, 0), 0))
    small = lambda r: pl.BlockSpec((r, W_A), lambda i: (0, 0))
    return _pcall(
        body, (dya, lx, lx, lg, h_s, h_s, *lru_p), name=name, grid=(n_t,),
        in_specs=[row, row, prev8, row, row, prev8] + _lru_pspecs(),
        out_specs=[row, row, small(LRU_K), small(1), small(W_A), small(1), small(W_A), small(1), small(1)],
        out_shape=[SDS((s, W_A), F32), SDS((s, W_A), F32), SDS((LRU_K, W_A), F32), SDS((1, W_A), F32),
                   SDS((W_A, W_A), F32), SDS((1, W_A), F32), SDS((W_A, W_A), F32), SDS((1, W_A), F32),
                   SDS((1, W_A), F32)],
        scratch_shapes=[pltpu.VMEM((1, W_A), F32), pltpu.VMEM((1, W_A), F32), pltpu.VMEM((LRU_HALO, W_A), F32)],
        carry=carry)


_ATT_ROWS = N_Q_HEADS * BLK
_GRP_ROWS = Q_PER_KV * BLK


def _attn_stack(ref, rows, g):
    return jnp.concatenate([ref[rows, h * HEAD_DIM:(h + 1) * HEAD_DIM]
                            for h in range(g * Q_PER_KV, (g + 1) * Q_PER_KV)], axis=0)


def _attn_unstack(parts):
    return jnp.concatenate([p[j * BLK:(j + 1) * BLK, :] for p in parts for j in range(Q_PER_KV)], axis=1)


def _grp(x, g):
    return x[:, g * _GRP_ROWS:(g + 1) * _GRP_ROWS]


def _attn_block(q_ref, k_ref, kp_ref, v_ref, vp_ref, sink_row, i, b):
    rows, prev = slice(b * BLK, (b + 1) * BLK), slice((b - 1) * BLK, b * BLK)
    qs, kcs, kps, vcs, vps = [], [], [], [], []
    for g in range(N_KV_HEADS):
        cols = slice(g * HEAD_DIM, (g + 1) * HEAD_DIM)
        qs.append(_attn_stack(q_ref, rows, g))
        kcs.append(k_ref[rows, cols])
        vcs.append(v_ref[rows, cols])
        kps.append(kp_ref[:, cols] if b == 0 else k_ref[prev, cols])
        vps.append(vp_ref[:, cols] if b == 0 else v_ref[prev, cols])
    scale = 1.0 / math.sqrt(HEAD_DIM)
    sc = jnp.concatenate([_mm_nt(kcs[g], qs[g]) for g in range(N_KV_HEADS)], axis=1) * scale
    sp = jnp.concatenate([_mm_nt(kps[g], qs[g]) for g in range(N_KV_HEADS)], axis=1) * scale
    kj = lax.broadcasted_iota(jnp.int32, (BLK, _ATT_ROWS), 0)
    qi = lax.broadcasted_iota(jnp.int32, (BLK, _ATT_ROWS), 1) & (BLK - 1)
    sc = jnp.where(kj <= qi, sc, NEG_BIG)
    sp = jnp.where((kj > qi) if b > 0 else ((kj > qi) & (i > 0)), sp, NEG_BIG)
    m = jnp.maximum(jnp.maximum(jnp.max(sc, axis=0, keepdims=True), jnp.max(sp, axis=0, keepdims=True)), sink_row)
    pc = jnp.exp(sc - m)
    pp = jnp.exp(sp - m)
    es = jnp.exp(sink_row - m)
    inv = 1.0 / (jnp.sum(pc, axis=0, keepdims=True) + jnp.sum(pp, axis=0, keepdims=True) + es)
    return qs, kcs, kps, vcs, vps, pc * inv, pp * inv, es * inv


def _attn_specs(s, ts):
    bpt = ts // BLK
    tile = lambda w: pl.BlockSpec((ts, w), lambda i: (i, 0))
    prv = pl.BlockSpec((BLK, KV_W), lambda i: (jnp.maximum(i * bpt - 1, 0), 0))
    sink = pl.BlockSpec((1, _ATT_ROWS), lambda i: (0, 0))
    return bpt, tile, prv, sink


def _attn_fwd(q, k, v, sink_row, name, carry=()):
    s = q.shape[0]
    ts = _time_tile(s)
    bpt, tile, prv, sink = _attn_specs(s, ts)

    def body(q_ref, k_ref, kp_ref, v_ref, vp_ref, sk_ref, y_ref):
        i = pl.program_id(0)
        for b in range(bpt):
            _, _, _, vcs, vps, pc, pp, _ = _attn_block(q_ref, k_ref, kp_ref, v_ref, vp_ref, sk_ref[...], i, b)
            outs = [_mm_tn(_grp(pc, g), vcs[g]) + _mm_tn(_grp(pp, g), vps[g]) for g in range(N_KV_HEADS)]
            y_ref[b * BLK:(b + 1) * BLK, :] = _attn_unstack(outs)

    return _pcall(
        body, (q, k, k, v, v, sink_row), name=name, grid=(s // ts,),
        in_specs=[tile(W_B), tile(KV_W), prv, tile(KV_W), prv, sink],
        out_specs=[tile(W_B)], out_shape=[SDS((s, W_B), F32)], carry=carry)


def _attn_bwd(dy, q, k, v, sinks, name, carry=()):
    s = q.shape[0]
    ts = _time_tile(s)
    n_t = s // ts
    bpt, tile, prv, sink = _attn_specs(s, ts)

    def body(dy_ref, q_ref, k_ref, kp_ref, v_ref, vp_ref, sk_ref, dq_ref, dk_ref, dv_ref, dku_ref, dvu_ref, dsk_ref):
        i = pl.program_id(0)

        @pl.when(i == 0)
        def _():
            dsk_ref[...] = jnp.zeros_like(dsk_ref)

        scale = 1.0 / math.sqrt(HEAD_DIM)
        groups = range(N_KV_HEADS)
        head_row = lax.broadcasted_iota(jnp.int32, (N_Q_HEADS, BLK), 0)
        dsk = jnp.zeros((N_Q_HEADS, BLK), F32)
        dk_blocks, dv_blocks = [], []
        for b in range(bpt):
            rows = slice(b * BLK, (b + 1) * BLK)
            qs, kcs, kps, vcs, vps, pc, pp, ps = _attn_block(q_ref, k_ref, kp_ref, v_ref, vp_ref, sk_ref[...], i, b)
            dos = [_attn_stack(dy_ref, rows, g) for g in groups]
            dpc = jnp.concatenate([_mm_nt(vcs[g], dos[g]) for g in groups], axis=1)
            dpp = jnp.concatenate([_mm_nt(vps[g], dos[g]) for g in groups], axis=1)
            delta = jnp.sum(pc * dpc, axis=0, keepdims=True) + jnp.sum(pp * dpp, axis=0, keepdims=True)
            dsc = pc * (dpc - delta) * scale
            dsp = pp * (dpp - delta) * scale
            dq_ref[rows, :] = _attn_unstack([_mm_tn(_grp(dsc, g), kcs[g]) + _mm_tn(_grp(dsp, g), kps[g])
                                             for g in groups])
            dk_blocks.append(jnp.concatenate([_mm(_grp(dsc, g), qs[g]) for g in groups], axis=1))
            dv_blocks.append(jnp.concatenate([_mm(_grp(pc, g), dos[g]) for g in groups], axis=1))
            dkp = jnp.concatenate([_mm(_grp(dsp, g), qs[g]) for g in groups], axis=1)
            dvp = jnp.concatenate([_mm(_grp(pp, g), dos[g]) for g in groups], axis=1)
            if b == 0:
                dku_ref[...] = dkp
                dvu_ref[...] = dvp
            else:
                dk_blocks[b - 1] = dk_blocks[b - 1] + dkp
                dv_blocks[b - 1] = dv_blocks[b - 1] + dvp
            dsink = -ps * delta
            for h in range(N_Q_HEADS):
                dsk = dsk + jnp.where(head_row == h, jnp.sum(dsink[:, h * BLK:(h + 1) * BLK], axis=1, keepdims=True), 0.0)
        for b in range(bpt):
            dk_ref[b * BLK:(b + 1) * BLK, :] = dk_blocks[b]
            dv_ref[b * BLK:(b + 1) * BLK, :] = dv_blocks[b]
        dsk_ref[...] += dsk

    up = pl.BlockSpec((BLK, KV_W), lambda i: (i, 0))
    return _pcall(
        body, (dy, q, k, k, v, v, sinks), name=name, grid=(n_t,),
        in_specs=[tile(W_B), tile(W_B), tile(KV_W), prv, tile(KV_W), prv, sink],
        out_specs=[tile(W_B), tile(KV_W), tile(KV_W), up, up, pl.BlockSpec((N_Q_HEADS, BLK), lambda i: (0, 0))],
        out_shape=[SDS((s, W_B), F32), SDS((s, KV_W), F32), SDS((s, KV_W), F32), SDS((n_t * BLK, KV_W), F32),
                   SDS((n_t * BLK, KV_W), F32), SDS((N_Q_HEADS, BLK), F32)], carry=carry)


def _cc_recompute(glu_ref, glup_ref, cw_ref, cb_ref, first_tile):
    prev = jnp.where(first_tile, 0.0, glup_ref[...])
    ge = jnp.concatenate([prev, glu_ref[...]], axis=0)
    y0 = ge[:, :W_C] * _sigmoid(ge[:, W_C:])
    y1 = cb_ref[...]
    for j in range(CC_K):
        y1 = y1 + cw_ref[j:j + 1, :] * _shift_down(y0, CC_K - 1 - j)[CC_HALO:, :]
    return y0, y1


def _ln_stats(y1):
    mu = jnp.mean(y1, axis=-1, keepdims=True)
    xc = y1 - mu
    rstd = lax.rsqrt(jnp.mean(xc * xc, axis=-1, keepdims=True) + LN_EPS)
    return xc * rstd, rstd


def _cc_specs(s, ts):
    n32 = ts // CC_HALO
    row = lambda w: pl.BlockSpec((ts, w), lambda i: (i, 0))
    prev = pl.BlockSpec((CC_HALO, 2 * W_C), lambda i: (jnp.maximum(i * n32 - 1, 0), 0))
    small = lambda r: pl.BlockSpec((r, W_C), lambda i: (0, 0))
    return row, prev, small


def _cc_fwd(glu, cw, cb, lng, lnb, name, carry=()):
    s = glu.shape[0]
    ts = _time_tile(s)
    row, prev, small = _cc_specs(s, ts)

    def body(glu_ref, glup_ref, cw_ref, cb_ref, lng_ref, lnb_ref, y_ref):
        _, y1 = _cc_recompute(glu_ref, glup_ref, cw_ref, cb_ref, pl.program_id(0) == 0)
        xhat, _ = _ln_stats(y1)
        z = xhat * lng_ref[...] + lnb_ref[...]
        y_ref[...] = z * _sigmoid(z)

    return _pcall(
        body, (glu, glu, cw, cb, lng, lnb), name=name, grid=(s // ts,),
        in_specs=[row(2 * W_C), prev, small(CC_HALO), small(1), small(1), small(1)],
        out_specs=[row(W_C)], out_shape=[SDS((s, W_C), F32)], carry=carry)


def _cc_bwd_conv(dy, glu, cw, cb, lng, lnb, name, carry=()):
    s = glu.shape[0]
    ts = _time_tile(s)
    row, prev, small = _cc_specs(s, ts)

    def body(dy_ref, glu_ref, glup_ref, cw_ref, cb_ref, lng_ref, lnb_ref, dy1_ref, dcw_ref, dcb_ref, dlng_ref, dlnb_ref):
        i = pl.program_id(0)

        @pl.when(i == 0)
        def _():
            for ref in (dcw_ref, dcb_ref, dlng_ref, dlnb_ref):
                ref[...] = jnp.zeros_like(ref)

        y0, y1 = _cc_recompute(glu_ref, glup_ref, cw_ref, cb_ref, i == 0)
        xhat, rstd = _ln_stats(y1)
        z = xhat * lng_ref[...] + lnb_ref[...]
        dz = dy_ref[...] * _dsilu(z, _sigmoid(z))
        dlng_ref[...] += jnp.sum(dz * xhat, axis=0, keepdims=True)
        dlnb_ref[...] += jnp.sum(dz, axis=0, keepdims=True)
        dxh = dz * lng_ref[...]
        dy1 = rstd * (dxh - jnp.mean(dxh, axis=-1, keepdims=True) - xhat * jnp.mean(dxh * xhat, axis=-1, keepdims=True))
        dy1_ref[...] = dy1
        dcb_ref[...] += jnp.sum(dy1, axis=0, keepdims=True)
        for j in range(CC_K):
            dcw_ref[j:j + 1, :] += jnp.sum(dy1 * _shift_down(y0, CC_K - 1 - j)[CC_HALO:, :], axis=0, keepdims=True)

    return _pcall(
        body, (dy, glu, glu, cw, cb, lng, lnb), name=name, grid=(s // ts,),
        in_specs=[row(W_C), row(2 * W_C), prev, small(CC_HALO), small(1), small(1), small(1)],
        out_specs=[row(W_C), small(CC_HALO), small(1), small(1), small(1)],
        out_shape=[SDS((s, W_C), F32), SDS((CC_HALO, W_C), F32)] + [SDS((1, W_C), F32)] * 3, carry=carry)


def _cc_bwd_glu(dy1, glu, cw, name, carry=()):
    s = glu.shape[0]
    ts = _time_tile(s)
    n_t = s // ts
    n32 = ts // CC_HALO

    def body(dy1_ref, dyn_ref, glu_ref, cw_ref, dglu_ref):
        i = pl.program_id(0)
        dcat = jnp.concatenate([dy1_ref[...], jnp.where(i < n_t - 1, dyn_ref[...], 0.0)], axis=0)
        dy0 = jnp.zeros((ts, W_C), F32)
        for j in range(CC_K):
            dy0 = dy0 + cw_ref[j:j + 1, :] * _shift_up(dcat, CC_K - 1 - j)[:ts, :]
        a = glu_ref[:, :W_C]
        sg = _sigmoid(glu_ref[:, W_C:])
        dglu_ref[...] = jnp.concatenate([dy0 * sg, dy0 * a * sg * (1.0 - sg)], axis=1)

    row = lambda w: pl.BlockSpec((ts, w), lambda i: (i, 0))
    nxt = pl.BlockSpec((CC_HALO, W_C), lambda i: (jnp.minimum((i + 1) * n32, s // CC_HALO - 1), 0))
    return _pcall(
        body, (dy1, dy1, glu, cw), name=name, grid=(n_t,),
        in_specs=[row(W_C), nxt, row(2 * W_C), pl.BlockSpec((CC_HALO, W_C), lambda i: (0, 0))],
        out_specs=[row(2 * W_C)], out_shape=[SDS((s, 2 * W_C), F32)], carry=carry)


_MIX_OFFS = ((0, W_A), (W_A, W_A + W_B), (W_A + W_B, W_A + W_B + W_C))


def _mix_out_fwd(x, ya, yb, yc, group_g, w_out, post_g, name, carry=()):
    s, d = x.shape
    ts = _time_tile(s)
    dm = w_out.shape[0]

    def body(x_ref, ya_ref, yb_ref, yc_ref, gg_ref, w_ref, qg_ref, xo_ref, o_ref):
        parts = []
        for y_ref, (lo, hi) in zip((ya_ref, yb_ref, yc_ref), _MIX_OFFS):
            yv = y_ref[...]
            parts.append(yv * _rms_r(yv) * gg_ref[:, lo:hi])
        o = _mm(jnp.concatenate(parts, axis=1), w_ref[...])
        o_ref[...] = o
        xo_ref[...] = x_ref[...] + o * _rms_r(o) * qg_ref[...]

    row = lambda w: pl.BlockSpec((ts, w), lambda i: (i, 0))
    return _pcall(
        body, (x, ya, yb, yc, group_g, w_out, post_g), name=name, grid=(s // ts,),
        in_specs=[row(d), row(W_A), row(W_B), row(W_C), pl.BlockSpec((1, dm), lambda i: (0, 0)),
                  pl.BlockSpec((dm, d), lambda i: (0, 0)), pl.BlockSpec((1, d), lambda i: (0, 0))],
        out_specs=[row(d), row(d)], out_shape=[SDS((s, d), F32), SDS((s, d), F32)], carry=carry)


def _mix_out_bwd(dxo, o, ya, yb, yc, group_g, w_out, post_g, name, carry=()):
    s, d = o.shape
    ts = _time_tile(s)
    n_t = s // ts
    dm = w_out.shape[0]

    def body(dxo_ref, o_ref, ya_ref, yb_ref, yc_ref, gg_ref, w_ref, qg_ref,
             dya_ref, dyb_ref, dyc_ref, dw_ref, dqg_ref, dgg_ref, acc):
        i = pl.program_id(0)

        @pl.when(i == 0)
        def _():
            acc[...] = jnp.zeros_like(acc)
            dqg_ref[...] = jnp.zeros_like(dqg_ref)
            dgg_ref[...] = jnp.zeros_like(dgg_ref)

        ov = o_ref[...]
        do, dq = _rms_bwd(ov, _rms_r(ov), qg_ref[...], dxo_ref[...])
        dqg_ref[...] += dq
        do = do.astype(BF16)
        dyn = _mm_nt(do, w_ref[...])
        parts, dggs = [], []
        for y_ref, dy_ref, (lo, hi) in zip((ya_ref, yb_ref, yc_ref), (dya_ref, dyb_ref, dyc_ref), _MIX_OFFS):
            yv = y_ref[...]
            r = _rms_r(yv)
            gg = gg_ref[:, lo:hi]
            parts.append(yv * r * gg)
            dyv, dg = _rms_bwd(yv, r, gg, dyn[:, lo:hi])
            dy_ref[...] = dyv
            dggs.append(dg)
        dgg_ref[...] += jnp.concatenate(dggs, axis=1)
        acc[...] += _mm_tn(jnp.concatenate(parts, axis=1), do)

        @pl.when(i == n_t - 1)
        def _():
            dw_ref[...] = acc[...].astype(BF16)

    row = lambda w: pl.BlockSpec((ts, w), lambda i: (i, 0))
    full = pl.BlockSpec((dm, d), lambda i: (0, 0))
    return _pcall(
        body, (dxo, o, ya, yb, yc, group_g, w_out, post_g), name=name, grid=(n_t,),
        in_specs=[row(d), row(d), row(W_A), row(W_B), row(W_C), pl.BlockSpec((1, dm), lambda i: (0, 0)), full,
                  pl.BlockSpec((1, d), lambda i: (0, 0))],
        out_specs=[row(W_A), row(W_B), row(W_C), full, pl.BlockSpec((1, d), lambda i: (0, 0)),
                   pl.BlockSpec((1, dm), lambda i: (0, 0))],
        out_shape=[SDS((s, W_A), F32), SDS((s, W_B), F32), SDS((s, W_C), F32), SDS((dm, d), BF16),
                   SDS((1, d), F32), SDS((1, dm), F32)],
        scratch_shapes=[pltpu.VMEM((dm, d), F32)], carry=carry)


def _loss_head(y, target, name):
    s, d = y.shape
    ts = _time_tile(s)

    def body(y_ref, t_ref, loss_ref, dy_ref):
        @pl.when(pl.program_id(0) == 0)
        def _():
            loss_ref[...] = jnp.zeros_like(loss_ref)

        err = y_ref[...] - t_ref[...]
        dy_ref[...] = err * (1.0 / d)
        per_tok = jnp.mean(err * err, axis=-1, keepdims=True)
        loss_ref[...] += 0.5 * jnp.sum(per_tok, axis=0, keepdims=True)

    row = pl.BlockSpec((ts, d), lambda i: (i, 0))
    return _pcall(body, (y, target), name=name, grid=(s // ts,), in_specs=[row, row],
                  out_specs=[pl.BlockSpec((1, BLK), lambda i: (0, 0)), row],
                  out_shape=[SDS((1, BLK), F32), SDS((s, d), F32)])[0]


def _adamw_math(w, g, m, v):
    m = ADAM_B1 * m + (1.0 - ADAM_B1) * g
    v = ADAM_B2 * v + (1.0 - ADAM_B2) * (g * g)
    m_hat = m / (1.0 - ADAM_B1 ** ADAM_STEP)
    v_hat = v / (1.0 - ADAM_B2 ** ADAM_STEP)
    delta = -ADAM_LR * (m_hat / (jnp.sqrt(v_hat) + ADAM_EPS) + ADAM_WD * w)
    return delta, m, v


def _row_tile(rows, cap=256):
    best = None
    for t in range(16, min(rows, cap) + 1, 16):
        if rows % t == 0:
            best = t
    return best if best is not None else rows


def _reduce_adamw(recv, w, m, v, name):
    n_l, r, c = w.shape
    tr = _row_tile(r)

    def body(recv_ref, w_ref, m_ref, v_ref, g_ref, d_ref, nm_ref, nv_ref):
        g = recv_ref[0].astype(F32)
        for p in range(1, N_DEV):
            g = g + recv_ref[p].astype(F32)
        g_ref[...] = g
        d_ref[...], nm_ref[...], nv_ref[...] = _adamw_math(w_ref[...], g, m_ref[...], v_ref[...])

    blk = pl.BlockSpec((None, tr, c), lambda l, i: (l, i, 0))
    return _pcall(
        body, (recv, w, m, v), name=name, grid=(n_l, r // tr),
        in_specs=[pl.BlockSpec((N_DEV, None, tr, c), lambda l, i: (0, l, i, 0)), blk, blk, blk],
        out_specs=[blk] * 4, out_shape=[SDS(w.shape, F32)] * 4)[0]


def _reduce_adamw_small(parts, w, m, v, name):
    def body(p_ref, w_ref, m_ref, v_ref, g_ref, d_ref, nm_ref, nv_ref):
        g = p_ref[0]
        for p in range(1, N_DEV):
            g = g + p_ref[p]
        g_ref[...] = g
        d_ref[...], nm_ref[...], nv_ref[...] = _adamw_math(w_ref[...], g, m_ref[...], v_ref[...])

    vm = pl.BlockSpec(memory_space=pltpu.VMEM)
    return pl.pallas_call(body, name=name, in_specs=[vm] * 4, out_specs=[vm] * 4, out_shape=[SDS(w.shape, F32)] * 4,
                          compiler_params=pltpu.CompilerParams(vmem_limit_bytes=VMEM_LIMIT))(parts, w, m, v)


def _rows_of(shape):
    return -(-math.prod(shape) // (8 * BLK)) * 8


def _pack(arrs):
    rows = []
    for a in arrs:
        flat = a.reshape(-1).astype(F32)
        rows.append(jnp.pad(flat, (0, _rows_of(a.shape) * BLK - flat.shape[0])).reshape(-1, BLK))
    return jnp.concatenate(rows, axis=0)


def _unpack(packed, shapes):
    out, row = [], 0
    for shp in shapes:
        n, r = math.prod(shp), _rows_of(shp)
        out.append(packed[row:row + r].reshape(-1)[:n].reshape(shp))
        row += r
    return out


def _block_diag(w):
    nb, bw, _ = w.shape
    eye = jnp.eye(nb, dtype=w.dtype)
    return (eye[:, None, :, None] * w[:, :, None, :]).reshape(nb * bw, nb * bw)


def _diag_blocks(wd, nb):
    bw = wd.shape[0] // nb
    return jnp.stack([wd[b * bw:(b + 1) * bw, b * bw:(b + 1) * bw] for b in range(nb)])


WEIGHT_NAMES = ['ffn1_pre_g', 'ffn1_w_gu', 'ffn1_w_down', 'ffn1_post_g', 'mix_pre_g', 'w_in', 'lru_conv_w', 'lru_conv_b',
                'lru_w_a', 'lru_b_a', 'lru_w_x', 'lru_b_x', 'lru_lambda', 'attn_sinks', 'conv_w', 'conv_b', 'conv_ln_g',
                'conv_ln_b', 'group_g', 'w_out', 'mix_post_g', 'ffn2_pre_g', 'ffn2_w_gu', 'ffn2_w_down', 'ffn2_post_g']
BIG = ('ffn1_w_gu', 'ffn1_w_down', 'w_in', 'w_out', 'ffn2_w_gu', 'ffn2_w_down')
TRANSPOSED = ('ffn1_w_gu', 'ffn2_w_gu', 'w_in')
SMALL = tuple(k for k in WEIGHT_NAMES if k not in BIG)
CHANNEL_SHARDED = ('lru_conv_w', 'conv_w')


def _step(x, target, w, m, v):
    n_l = w['ffn1_pre_g'].shape[0]
    assert n_l == 2, "the exchange schedule below is laid out for two layers"
    s, d = x.shape[1], x.shape[2]
    x = x.reshape(s, d)
    target = target.reshape(s, d)
    me = _my_pos()[3]
    tview = lambda t, k: jnp.swapaxes(t[k], 1, 2) if k in TRANSPOSED else t[k]
    wb = {k: tview(w, k).astype(BF16) for k in BIG}
    vec = lambda name, l: w[name][l][None, :]

    conv_shard = _pack([w['lru_conv_w'], w['conv_w']])
    g0 = _all_gather([(wb['ffn1_w_gu'], 0), (wb['ffn1_w_down'], 0), (wb['w_in'], 0), (wb['w_out'], 0),
                      (conv_shard, None)], "all_gather_first")
    wts = [dict(), dict()]
    wts[0]['ffn1_w_gu'], wts[0]['ffn1_w_down'], wts[0]['w_in'], wts[0]['w_out'], conv_g = g0
    ch = W_A // N_DEV
    conv_parts = [_unpack(conv_g[p], [(n_l, LRU_K, ch), (n_l, CC_K, ch)]) for p in range(N_DEV)]
    lru_cw = jnp.concatenate([cp[0] for cp in conv_parts], axis=-1)
    cc_cw = jnp.concatenate([cp[1] for cp in conv_parts], axis=-1)
    cc_cw = jnp.pad(cc_cw, ((0, 0), (0, CC_HALO - CC_K), (0, 0)))

    gather_plan = {
        ('ffn1', 0): [('A', 'f2_0', ('ffn2_w_gu', 'ffn2_w_down'), 0)],
        ('mix_in', 0): [('B', 'f2_0')],
        ('attn', 0): [('A', 'g1_1', ('ffn1_w_gu',), 1)],
        ('cconv', 0): [('B', 'g1_1')],
        ('ffn2', 0): [('D', None, ('ffn1_w_down',), 1), ('A', 'wi_1', ('w_in',), 1), ('A', 'wo_1', ('w_out',), 1)],
        ('ffn1', 1): [('A', 'f2_1', ('ffn2_w_gu', 'ffn2_w_down'), 1), ('B', 'wi_1'), ('B', 'wo_1')],
        ('mix_in', 1): [('B', 'f2_1')],
    }
    pend = {}

    def fwd(kernel_name, l, fn, *args):
        plan = gather_plan.get((kernel_name, l), [])
        carry = []
        for st in plan:
            if st[0] == 'B':
                carry.append(_gather_b(pend[st[1]][2]))
            else:
                carry.append(_gather_a([(wb[k], st[3]) for k in st[2]], two_level=st[0] == 'A'))
        outs, ex = fn(*args, f"{kernel_name}_fwd_l{l}", carry)
        for st, bufs in zip(plan, ex):
            if st[0] == 'A':
                pend[st[1]] = (st[2], st[3], bufs)
            else:
                names, wl = (st[2], st[3]) if st[0] == 'D' else pend.pop(st[1])[:2]
                for k, b in zip(names, bufs):
                    wts[wl][k] = b
        return outs

    saved = []
    h = x
    for l in range(n_l):
        sv = {'x0': h}
        lw = wts[l]
        x1, sv['h1'], sv['g1'], sv['u1'], sv['d1'] = fwd(
            'ffn1', l, _ffn_fwd, h, vec('ffn1_pre_g', l), vec('ffn1_post_g', l), lw['ffn1_w_gu'], lw['ffn1_w_down'])
        sv['x1'] = x1
        sv['hn'], lx, lg, q, k, vv, glu = fwd('mix_in', l, _mix_in_fwd, x1, vec('mix_pre_g', l),
                                              lw['w_in'].reshape(D_IN_PROJ, d))
        sv.update(lx=lx, lg=lg, q=q, k=k, v=vv, glu=glu)
        lru_p = (lru_cw[l], vec('lru_conv_b', l), _block_diag(w['lru_w_a'][l]).astype(BF16), vec('lru_b_a', l),
                 _block_diag(w['lru_w_x'][l]).astype(BF16), vec('lru_b_x', l), vec('lru_lambda', l))
        cc_p = (cc_cw[l], vec('conv_b', l), vec('conv_ln_g', l), vec('conv_ln_b', l))
        sv.update(lru_p=lru_p, cc_p=cc_p)
        sv['ya'], sv['hs'] = fwd('lru', l, _lru_fwd, lx, lg, lru_p)
        sv['sink_row'] = jnp.repeat(w['attn_sinks'][l], BLK)[None, :]
        (sv['yb'],) = fwd('attn', l, _attn_fwd, q, k, vv, sv['sink_row'])
        (sv['yc'],) = fwd('cconv', l, _cc_fwd, glu, *cc_p)
        x2, sv['o'] = fwd('mix_out', l, _mix_out_fwd, x1, sv['ya'], sv['yb'], sv['yc'], vec('group_g', l),
                          lw['w_out'].reshape(-1, d), vec('mix_post_g', l))
        sv['x2'] = x2
        h, sv['h2'], sv['g2'], sv['u2'], sv['d2'] = fwd(
            'ffn2', l, _ffn_fwd, x2, vec('ffn2_pre_g', l), vec('ffn2_post_g', l), lw['ffn2_w_gu'], lw['ffn2_w_down'])
        saved.append(sv)

    loss_row, dh = _loss_head(h, target, "loss_head")

    recv = {k: None for k in BIG}
    ready = {}
    small = [dict() for _ in range(n_l)]

    def exchange(keys):
        return _grad_x([(ready.pop(key), key[1], recv[key[0]]) for key in keys], n_l)

    def received(keys, bufs):
        for key, b in zip(keys, bufs):
            recv[key[0]] = b

    def run(fn, *args, keys=()):
        outs, ex = fn(*args, carry=[exchange(keys)] if keys else [])
        if keys:
            received(keys, ex[0])
        return outs

    for l in reversed(range(n_l)):
        sv, sg, lw = saved[l], small[l], wts[l]
        keys = [] if l == n_l - 1 else [('ffn1_w_gu', l + 1)]
        dx2, dd, dg, du, sg['ffn2_pre_g'], sg['ffn2_post_g'] = run(
            _ffn_bwd_act, dh, sv['d2'], sv['x2'], vec('ffn2_pre_g', l), vec('ffn2_post_g', l), sv['g2'], sv['u2'],
            lw['ffn2_w_gu'], lw['ffn2_w_down'], f"ffn2_bwd_act_l{l}", keys=keys)
        keys = [] if l == n_l - 1 else [('ffn1_w_down', l + 1)]
        dwg, dwu, dwd = run(_ffn_bwd_w, sv['h2'], dd, sv['g2'], sv['u2'], dg, du, f"ffn2_bwd_w_l{l}", keys=keys)
        ready[('ffn2_w_gu', l)] = [dwg, dwu]
        ready[('ffn2_w_down', l)] = [dwd.reshape(N_DEV, -1, d)]
        dya, dyb, dyc, dw_out, sg['mix_post_g'], sg['group_g'] = run(
            _mix_out_bwd, dx2, sv['o'], sv['ya'], sv['yb'], sv['yc'], vec('group_g', l), lw['w_out'].reshape(-1, d),
            vec('mix_post_g', l), f"mix_out_bwd_l{l}")
        ready[('w_out', l)] = [dw_out.reshape(N_DEV, -1, d)]
        (dlx, dlg, sg['lru_conv_w'], sg['lru_conv_b'], dwa, sg['lru_b_a'], dwx, sg['lru_b_x'],
         sg['lru_lambda']) = run(_lru_bwd, dya, sv['lx'], sv['lg'], sv['hs'], sv['lru_p'], f"lru_bwd_l{l}")
        sg['lru_w_a'] = _diag_blocks(dwa, A_BLOCKS)
        sg['lru_w_x'] = _diag_blocks(dwx, A_BLOCKS)
        dq, dk, dv, dk_up, dv_up, dsk = run(_attn_bwd, dyb, sv['q'], sv['k'], sv['v'], sv['sink_row'],
                                            f"attn_bwd_l{l}", keys=[('ffn2_w_down', l)] if l == 0 else [])
        sg['attn_sinks'] = dsk[:, 0]
        dy1, dcw, sg['conv_b'], sg['conv_ln_g'], sg['conv_ln_b'] = run(
            _cc_bwd_conv, dyc, sv['glu'], *sv['cc_p'], f"cconv_bwd_conv_l{l}")
        sg['conv_w'] = dcw[:CC_K]
        (dglu,) = run(_cc_bwd_glu, dy1, sv['glu'], sv['cc_p'][0], f"cconv_bwd_glu_l{l}")
        dx1, dw_in, sg['mix_pre_g'] = run(
            _mix_in_bwd, dx2, sv['x1'], vec('mix_pre_g', l), sv['hn'], lw['w_in'].reshape(D_IN_PROJ, d),
            dlx, dlg, dq, dk, dk_up, dv, dv_up, dglu, f"mix_in_bwd_l{l}", keys=[('w_out', l)])
        ready[('w_in', l)] = [dw_in.reshape(N_DEV, -1, d)]
        dh, dd, dg, du, sg['ffn1_pre_g'], sg['ffn1_post_g'] = run(
            _ffn_bwd_act, dx1, sv['d1'], sv['x0'], vec('ffn1_pre_g', l), vec('ffn1_post_g', l), sv['g1'], sv['u1'],
            lw['ffn1_w_gu'], lw['ffn1_w_down'], f"ffn1_bwd_act_l{l}", keys=[('ffn2_w_gu', l), ('w_in', l)])
        if l > 0:
            dwg, dwu, dwd = run(_ffn_bwd_w, sv['h1'], dd, sv['g1'], sv['u1'], dg, du, f"ffn1_bwd_w_l{l}",
                                keys=[('ffn2_w_down', l)])
            ready[('ffn1_w_gu', l)] = [dwg, dwu]
            ready[('ffn1_w_down', l)] = [dwd.reshape(N_DEV, -1, d)]
        else:
            part = _pack([jnp.stack([small[j][k] for j in range(n_l)]) for k in SMALL] + [loss_row])
            (recv['ffn1_w_gu'], recv['ffn1_w_down']), ex = _ffn_bwd_w_send(
                sv['h1'], dd, sv['g1'], sv['u1'], dg, du, recv['ffn1_w_gu'], recv['ffn1_w_down'], 0, "ffn1_bwd_w_send_l0",
                [_gather_a([(part, None)], two_level=False)])
            small_parts = ex[0][0]
    grad_x = dh.reshape(1, s, d)

    out = {}
    for k in BIG:
        res = _reduce_adamw(recv[k], tview(w, k), tview(m, k), tview(v, k), f"reduce_adamw_{k}")
        out[k] = [jnp.swapaxes(r, 1, 2) for r in res] if k in TRANSPOSED else res

    small_shapes = [(n_l,) + tuple(small[0][k].shape) for k in SMALL]

    def widen(t, k):
        if k not in CHANNEL_SHARDED:
            return t.reshape((n_l,) + tuple(small[0][k].shape))
        full = jnp.zeros((n_l,) + tuple(small[0][k].shape), F32)
        return lax.dynamic_update_slice_in_dim(full, t, me * ch, axis=2)

    no_w = jnp.zeros(loss_row.shape, F32)
    packed = [_pack([widen(src[k], k) for k in SMALL] + [no_w]) for src in (w, m, v)]
    res = _reduce_adamw_small(small_parts, *packed, "reduce_adamw_small")
    loss = _unpack(res[0], small_shapes + [loss_row.shape])[-1][0, 0]
    for k, g, dlt, nm, nv in zip(SMALL, *[_unpack(r, small_shapes) for r in res]):
        vals = [g, dlt, nm, nv]
        if k in CHANNEL_SHARDED:
            vals = [lax.dynamic_slice_in_dim(t, me * ch, ch, axis=2) for t in vals]
        out[k] = [t.reshape(w[k].shape) for t in vals]

    return (loss, grad_x, *[out[k][0] for k in WEIGHT_NAMES], *[out[k][1] for k in WEIGHT_NAMES],
            *[out[k][2] for k in WEIGHT_NAMES], *[out[k][3] for k in WEIGHT_NAMES])


def kernel(x, ffn1_pre_g, ffn1_w_gu, ffn1_w_down, ffn1_post_g, mix_pre_g, w_in, lru_conv_w, lru_conv_b, lru_w_a, lru_b_a, lru_w_x, lru_b_x, lru_lambda, attn_sinks, conv_w, conv_b, conv_ln_g, conv_ln_b, group_g, w_out, mix_post_g, ffn2_pre_g, ffn2_w_gu, ffn2_w_down, ffn2_post_g, loss_target, m_ffn1_pre_g, m_ffn1_w_gu, m_ffn1_w_down, m_ffn1_post_g, m_mix_pre_g, m_w_in, m_lru_conv_w, m_lru_conv_b, m_lru_w_a, m_lru_b_a, m_lru_w_x, m_lru_b_x, m_lru_lambda, m_attn_sinks, m_conv_w, m_conv_b, m_conv_ln_g, m_conv_ln_b, m_group_g, m_w_out, m_mix_post_g, m_ffn2_pre_g, m_ffn2_w_gu, m_ffn2_w_down, m_ffn2_post_g, v_ffn1_pre_g, v_ffn1_w_gu, v_ffn1_w_down, v_ffn1_post_g, v_mix_pre_g, v_w_in, v_lru_conv_w, v_lru_conv_b, v_lru_w_a, v_lru_b_a, v_lru_w_x, v_lru_b_x, v_lru_lambda, v_attn_sinks, v_conv_w, v_conv_b, v_conv_ln_g, v_conv_ln_b, v_group_g, v_w_out, v_mix_post_g, v_ffn2_pre_g, v_ffn2_w_gu, v_ffn2_w_down, v_ffn2_post_g):
    args = (ffn1_pre_g, ffn1_w_gu, ffn1_w_down, ffn1_post_g, mix_pre_g, w_in, lru_conv_w, lru_conv_b, lru_w_a, lru_b_a, lru_w_x, lru_b_x, lru_lambda, attn_sinks, conv_w, conv_b, conv_ln_g, conv_ln_b, group_g, w_out, mix_post_g, ffn2_pre_g, ffn2_w_gu, ffn2_w_down, ffn2_post_g)
    ms = (m_ffn1_pre_g, m_ffn1_w_gu, m_ffn1_w_down, m_ffn1_post_g, m_mix_pre_g, m_w_in, m_lru_conv_w, m_lru_conv_b, m_lru_w_a, m_lru_b_a, m_lru_w_x, m_lru_b_x, m_lru_lambda, m_attn_sinks, m_conv_w, m_conv_b, m_conv_ln_g, m_conv_ln_b, m_group_g, m_w_out, m_mix_post_g, m_ffn2_pre_g, m_ffn2_w_gu, m_ffn2_w_down, m_ffn2_post_g)
    vs = (v_ffn1_pre_g, v_ffn1_w_gu, v_ffn1_w_down, v_ffn1_post_g, v_mix_pre_g, v_w_in, v_lru_conv_w, v_lru_conv_b, v_lru_w_a, v_lru_b_a, v_lru_w_x, v_lru_b_x, v_lru_lambda, v_attn_sinks, v_conv_w, v_conv_b, v_conv_ln_g, v_conv_ln_b, v_group_g, v_w_out, v_mix_post_g, v_ffn2_pre_g, v_ffn2_w_gu, v_ffn2_w_down, v_ffn2_post_g)
    return _step(x, loss_target, dict(zip(WEIGHT_NAMES, args)), dict(zip(WEIGHT_NAMES, ms)), dict(zip(WEIGHT_NAMES, vs)))
```

```python
import functools
import math
import operator

import jax
import jax.numpy as jnp
from jax import lax
from jax.experimental import pallas as pl
from jax.experimental.pallas import tpu as pltpu

F32 = jnp.float32
BF16 = jnp.bfloat16
N_DEV = 8
AXES = ("x", "y", "c")
MESH = pl.DeviceIdType.MESH

NORM_EPS = 1e-6
LN_EPS = 1e-5
NEG_BIG = -1e30
W_A = 256
W_B = 512
W_C = 256
HEAD_DIM = 64
N_Q_HEADS = 8
N_KV_HEADS = 2
Q_PER_KV = N_Q_HEADS // N_KV_HEADS
KV_W = N_KV_HEADS * HEAD_DIM
BLK = 128
LRU_K = 4
LRU_C = 8.0
A_BLOCKS = 4
CC_K = 31
CC_HALO = 32
LRU_HALO = 8
D_IN_PROJ = 2 * W_A + W_B + 2 * KV_W + 2 * W_C
ADAM_LR = 0.001
ADAM_B1 = 0.9
ADAM_B2 = 0.999
ADAM_EPS = 1e-08
ADAM_WD = 0.01
ADAM_STEP = 10
VMEM_LIMIT = 56 * 1024 * 1024

SDS = jax.ShapeDtypeStruct
ANY = pl.BlockSpec(memory_space=pl.ANY)


def _time_tile(s):
    return max(BLK, s // 8)


class _Exchange:
    def __init__(self, inputs, out_shapes, aliases, sem_shapes, start, wait):
        self.inputs, self.out_shapes, self.aliases, self.sem_shapes = inputs, out_shapes, aliases, sem_shapes
        self.start, self.wait = start, wait


def _my_pos():
    x, y, c = (lax.axis_index(a) for a in AXES)
    return x, y, c, 4 * x + 2 * y + c


def _flip(k):
    x, y, c, _ = _my_pos()
    return (1 - x if k & 4 else x, 1 - y if k & 2 else y, 1 - c if k & 1 else c)


def _slot(dev):
    return 4 * dev[0] + 2 * dev[1] + dev[2]


def _dev(p):
    return (p >> 2, (p >> 1) & 1, p & 1)


def _gather_a(items, two_level):
    rels = (1, 2, 4, 6) if two_level else tuple(range(1, N_DEV))
    n = len(items)
    src_of = lambda ins, a: ins[a] if items[a][1] is None else ins[a].at[items[a][1]]

    def shape_of(a):
        arr, l = items[a]
        return arr.shape if l is None else arr.shape[1:]

    def copies(ins, outs, sems, a):
        send, recv, _ = sems
        me = _my_pos()[3]
        return [(k, pltpu.make_async_remote_copy(
            src_ref=src_of(ins, a), dst_ref=outs[a].at[me], send_sem=send.at[a, k], recv_sem=recv.at[a, k],
            device_id=_flip(k), device_id_type=MESH)) for k in rels]

    def local(ins, outs, sems, a):
        return pltpu.make_async_copy(src_of(ins, a), outs[a].at[_my_pos()[3]], sems[2].at[a])

    def start(ins, outs, sems):
        for a in range(n):
            local(ins, outs, sems, a).start()
            for _, cp in copies(ins, outs, sems, a):
                cp.start()

    def wait(ins, outs, sems):
        send, recv, _ = sems
        for a in range(n):
            for k, cp in copies(ins, outs, sems, a):
                pltpu.make_async_remote_copy(
                    src_ref=src_of(ins, a), dst_ref=outs[a].at[_slot(_flip(k))], send_sem=send.at[a, k],
                    recv_sem=recv.at[a, k], device_id=_flip(k), device_id_type=MESH).wait_recv()
                cp.wait_send()
            local(ins, outs, sems, a).wait()

    return _Exchange([it[0] for it in items], [SDS((N_DEV,) + shape_of(a), items[a][0].dtype) for a in range(n)], {},
                     [pltpu.SemaphoreType.DMA((n, N_DEV)), pltpu.SemaphoreType.DMA((n, N_DEV)),
                      pltpu.SemaphoreType.DMA((n,))], start, wait)


def _gather_b(bufs):
    n = len(bufs)

    def copies(ins, outs, sems, a, c_of_block):
        send, recv = sems
        x, y, c, _ = _my_pos()
        res = []
        for k in (2, 4, 6):
            chip = _flip(k)
            blk = _slot((chip[0], chip[1], c if c_of_block == "mine" else 1 - c))
            res.append(pltpu.make_async_remote_copy(
                src_ref=ins[a].at[blk], dst_ref=outs[a].at[blk], send_sem=send.at[a, k], recv_sem=recv.at[a, k],
                device_id=_flip(1), device_id_type=MESH))
        return res

    def start(ins, outs, sems):
        for a in range(n):
            for cp in copies(ins, outs, sems, a, "mine"):
                cp.start()

    def wait(ins, outs, sems):
        for a in range(n):
            for cp in copies(ins, outs, sems, a, "sibling"):
                cp.wait_recv()
            for cp in copies(ins, outs, sems, a, "mine"):
                cp.wait_send()

    return _Exchange(list(bufs), [SDS(b.shape, b.dtype) for b in bufs], {a: a for a in range(n)},
                     [pltpu.SemaphoreType.DMA((n, N_DEV)), pltpu.SemaphoreType.DMA((n, N_DEV))], start, wait)


def _grad_x(items, n_l):
    n = len(items)
    inputs, first_in, recv_in, aliases, out_shapes = [], [], [], {}, []
    for a, (arrs, l, recv) in enumerate(items):
        first_in.append(len(inputs))
        inputs += list(arrs)
        assert sum(arr.shape[0] for arr in arrs) == N_DEV
        if recv is not None:
            aliases[len(inputs)] = a
            inputs.append(recv)
        out_shapes.append(SDS((N_DEV, n_l) + arrs[0].shape[1:], arrs[0].dtype))

    def slab(ins, a, p):
        off = 0
        for j, arr in enumerate(items[a][0]):
            if p < off + arr.shape[0]:
                return ins[first_in[a] + j].at[p - off]
            off += arr.shape[0]
        raise AssertionError

    def rdma(ins, outs, sems, a, p, src_dev):
        send, recv, _ = sems
        return pltpu.make_async_remote_copy(
            src_ref=slab(ins, a, p), dst_ref=outs[a].at[src_dev, items[a][1]], send_sem=send.at[a, p],
            recv_sem=recv.at[a, src_dev], device_id=_dev(p), device_id_type=MESH)

    def local(ins, outs, sems, a, p):
        return pltpu.make_async_copy(slab(ins, a, p), outs[a].at[p, items[a][1]], sems[2].at[a])

    def start(ins, outs, sems):
        me = _my_pos()[3]
        for k in range(1, N_DEV):
            for p in range(N_DEV):
                @pl.when((me ^ k) == p)
                def _():
                    for a in range(n):
                        rdma(ins, outs, sems, a, p, me).start()

        for p in range(N_DEV):
            @pl.when(me == p)
            def _():
                for a in range(n):
                    local(ins, outs, sems, a, p).start()

    def wait(ins, outs, sems):
        me = _my_pos()[3]
        for p in range(N_DEV):
            @pl.when(me != p)
            def _():
                for a in range(n):
                    rdma(ins, outs, sems, a, p, p).wait_recv()
                    rdma(ins, outs, sems, a, p, p).wait_send()

            @pl.when(me == p)
            def _():
                for a in range(n):
                    local(ins, outs, sems, a, p).wait()

    return _Exchange(inputs, out_shapes, aliases,
                     [pltpu.SemaphoreType.DMA((n, N_DEV)), pltpu.SemaphoreType.DMA((n, N_DEV)),
                      pltpu.SemaphoreType.DMA((n,))], start, wait)


def _pcall(body, args, *, name, grid, in_specs, out_specs, out_shape, scratch_shapes=(), carry=(), body_aliases=None):
    n_in, n_out, n_scr = len(in_specs), len(out_specs), len(scratch_shapes)
    c_in = [len(e.inputs) for e in carry]
    c_out = [len(e.out_shapes) for e in carry]
    c_sem = [len(e.sem_shapes) for e in carry]
    aliases = dict(body_aliases or {})
    for j, e in enumerate(carry):
        for i_loc, o_loc in e.aliases.items():
            aliases[n_in + sum(c_in[:j]) + i_loc] = n_out + sum(c_out[:j]) + o_loc

    def wrapped(*refs):
        def take(counts, pos):
            groups = []
            for cnt in counts:
                groups.append(refs[pos:pos + cnt])
                pos += cnt
            return groups, pos

        (ins,), pos = take([n_in], 0)
        cins, pos = take(c_in, pos)
        (outs,), pos = take([n_out], pos)
        couts, pos = take(c_out, pos)
        (scr,), pos = take([n_scr], pos)
        csems, pos = take(c_sem, pos)
        if carry:
            ids = [pl.program_id(k) for k in range(len(grid))]
            first = functools.reduce(operator.and_, [i == 0 for i in ids])
            last = functools.reduce(operator.and_, [i == g - 1 for i, g in zip(ids, grid)])

            @pl.when(first)
            def _():
                for e, ci, co, cs in zip(carry, cins, couts, csems):
                    e.start(ci, co, cs)

        body(*ins, *outs, *scr)
        if carry:
            @pl.when(last)
            def _():
                for e, ci, co, cs in zip(carry, cins, couts, csems):
                    e.wait(ci, co, cs)

    res = pl.pallas_call(
        wrapped, name=name, grid=grid,
        in_specs=list(in_specs) + [ANY] * sum(c_in),
        out_specs=list(out_specs) + [ANY] * sum(c_out),
        out_shape=list(out_shape) + [s for e in carry for s in e.out_shapes],
        scratch_shapes=list(scratch_shapes) + [s for e in carry for s in e.sem_shapes],
        input_output_aliases=aliases,
        compiler_params=pltpu.CompilerParams(dimension_semantics=("arbitrary",) * len(grid),
                                             vmem_limit_bytes=VMEM_LIMIT),
    )(*args, *[a for e in carry for a in e.inputs])
    outs, pos, extra = list(res[:n_out]), n_out, []
    for cnt in c_out:
        extra.append(list(res[pos:pos + cnt]))
        pos += cnt
    return outs, extra


def _all_gather(items, name):
    n = len(items)
    shape_of = lambda a: items[a][0].shape if items[a][1] is None else items[a][0].shape[1:]

    def body(*refs):
        ins, outs, (send_sems, recv_sems, local_sems) = refs[:n], refs[n:2 * n], refs[2 * n:]
        x, y, c, me = _my_pos()
        src_of = lambda a: ins[a] if items[a][1] is None else ins[a].at[items[a][1]]

        def copy(a, k, block, to, src=None):
            dst = outs[a].at[_slot(block)]
            return pltpu.make_async_remote_copy(
                src_ref=dst if src is None else src, dst_ref=dst,
                send_sem=send_sems.at[a, k], recv_sem=recv_sems.at[a, k], device_id=to, device_id_type=MESH)

        mine = [pltpu.make_async_copy(src_of(a), outs[a].at[me], local_sems.at[a]) for a in range(n)]
        for cp in mine:
            cp.start()
        first = [copy(a, k, (x, y, c), _flip(k), src=src_of(a)) for a in range(n) for k in (1, 2, 4, 6)]
        for cp in first:
            cp.start()
        passed = []
        for k in (2, 4, 6):
            for a in range(n):
                copy(a, k, _flip(k), (x, y, c)).wait_recv()
                fwd = copy(a, k + 1, _flip(k), _flip(1))
                fwd.start()
                passed.append(fwd)
        for a in range(n):
            copy(a, 1, _flip(1), (x, y, c)).wait_recv()
            for k in (2, 4, 6):
                copy(a, k + 1, _flip(k + 1), (x, y, c)).wait_recv()
        for cp in first + passed:
            cp.wait_send()
        for cp in mine:
            cp.wait()

    return pl.pallas_call(
        body, name=name,
        in_specs=[ANY] * n, out_specs=[ANY] * n,
        out_shape=[SDS((N_DEV,) + shape_of(a), items[a][0].dtype) for a in range(n)],
        scratch_shapes=[pltpu.SemaphoreType.DMA((n, N_DEV)), pltpu.SemaphoreType.DMA((n, N_DEV)),
                        pltpu.SemaphoreType.DMA((n,))],
    )(*[it[0] for it in items])


def _mm(a, b):
    return jnp.dot(a.astype(BF16), b.astype(BF16), preferred_element_type=F32)


def _mm_nt(a, b):
    return lax.dot_general(a.astype(BF16), b.astype(BF16), (((1,), (1,)), ((), ())), preferred_element_type=F32)


def _mm_tn(a, b):
    return lax.dot_general(a.astype(BF16), b.astype(BF16), (((0,), (0,)), ((), ())), preferred_element_type=F32)


def _rms_r(x):
    return lax.rsqrt(jnp.mean(x * x, axis=-1, keepdims=True) + NORM_EPS)


def _rms_bwd(x, r, g, dy):
    gy = dy * g
    dx = r * (gy - x * (r * r) * jnp.mean(gy * x, axis=-1, keepdims=True))
    dg = jnp.sum(dy * x * r, axis=0, keepdims=True)
    return dx, dg


def _sigmoid(x):
    return 1.0 / (1.0 + jnp.exp(-x))


def _dsilu(z, sz):
    return sz * (1.0 + z * (1.0 - sz))


def _swiglu_bf16(g, u):
    sg = 0.5 * jnp.tanh(0.5 * g) + 0.5
    silu = g * sg
    return silu * u, silu, sg + silu * (1.0 - sg)


_GELU_C = math.sqrt(2.0 / math.pi)


def _gelu(x):
    t = jnp.tanh(_GELU_C * (x + 0.044715 * x * x * x))
    return 0.5 * x * (1.0 + t), t


def _dgelu(x, t):
    return 0.5 * (1.0 + t) + 0.5 * x * (1.0 - t * t) * _GELU_C * (1.0 + 3.0 * 0.044715 * x * x)


def _log1p(e):
    return jnp.where(e < 1e-2, e * (1.0 - e * (0.5 - e * (1.0 / 3.0))), jnp.log(1.0 + e))


def _softplus(x):
    return jnp.maximum(x, 0.0) + _log1p(jnp.exp(-jnp.abs(x)))


def _neg_expm1(x):
    small = -x * (1.0 + x * (0.5 + x * (1.0 / 6.0) * (1.0 + x * 0.25)))
    return jnp.where(x > -1e-2, small, 1.0 - jnp.exp(x))


def _shift_down(x, s):
    return x if s == 0 else pltpu.roll(x, s, 0)


def _shift_up(x, s):
    return x if s == 0 else pltpu.roll(x, x.shape[0] - s, 0)


def _ffn_wspecs(d, fc, order):
    f_of = (lambda i, f: f) if order == "tf" else (lambda f, i: f)
    n_f = N_DEV // 2
    return [pl.BlockSpec((None, fc, d), lambda *g: (f_of(*g), 0, 0)),
            pl.BlockSpec((None, fc, d), lambda *g: (f_of(*g) + n_f, 0, 0)),
            pl.BlockSpec((2, fc // 2, d), lambda *g: (f_of(*g), 0, 0))]


def _ffn_fwd(x, pre_g, post_g, wgu_t, wd, name, carry=()):
    s, d = x.shape
    fc = wgu_t.shape[1]
    ts = 2 * _time_tile(s)
    n_t, n_f = s // ts, N_DEV // 2

    def body(x_ref, pg_ref, qg_ref, wg_ref, wu_ref, wd_ref, xo_ref, h_ref, g_ref, u_ref, d_ref, h_scr, acc):
        f = pl.program_id(1)

        @pl.when(f == 0)
        def _():
            xv = x_ref[...]
            hv = (xv * _rms_r(xv) * pg_ref[...]).astype(BF16)
            h_scr[...] = hv
            h_ref[...] = hv
            acc[...] = jnp.zeros_like(acc)

        hv = h_scr[...]
        g = _mm_nt(hv, wg_ref[...])
        u = _mm_nt(hv, wu_ref[...])
        g = g.astype(BF16)
        u = u.astype(BF16)
        g_ref[...] = g
        u_ref[...] = u
        acc[...] += jnp.dot(_swiglu_bf16(g, u)[0], wd_ref[...].reshape(fc, d), preferred_element_type=F32)

        @pl.when(f == n_f - 1)
        def _():
            dv = acc[...]
            d_ref[...] = dv.astype(BF16)
            xo_ref[...] = x_ref[...] + 0.5 * (dv * _rms_r(dv) * qg_ref[...])

    row = pl.BlockSpec((ts, d), lambda i, f: (i, 0))
    vec = pl.BlockSpec((1, d), lambda i, f: (0, 0))
    act = pl.BlockSpec((None, ts, fc), lambda i, f: (f, i, 0))
    return _pcall(
        body, (x, pre_g, post_g, wgu_t, wgu_t, wd), name=name, grid=(n_t, n_f),
        in_specs=[row, vec, vec] + _ffn_wspecs(d, fc, "tf"),
        out_specs=[row, row, act, act, row],
        out_shape=[SDS((s, d), F32), SDS((s, d), BF16), SDS((n_f, s, fc), BF16), SDS((n_f, s, fc), BF16),
                   SDS((s, d), BF16)],
        scratch_shapes=[pltpu.VMEM((ts, d), BF16), pltpu.VMEM((ts, d), F32)], carry=carry)


def _ffn_bwd_act(dxo, dmid, x, pre_g, post_g, g_s, u_s, wgu_t, wd, name, carry=()):
    s, d = x.shape
    fc = wgu_t.shape[1]
    ts = _time_tile(s)
    n_t, n_f = s // ts, N_DEV // 2

    def body(dxo_ref, dm_ref, x_ref, pg_ref, qg_ref, g_ref, u_ref, wg_ref, wu_ref, wd_ref,
             dx_ref, dd_ref, dg_ref, du_ref, dpg_ref, dqg_ref, dd_scr, dh_acc):
        i, f = pl.program_id(0), pl.program_id(1)

        @pl.when((i == 0) & (f == 0))
        def _():
            dpg_ref[...] = jnp.zeros_like(dpg_ref)
            dqg_ref[...] = jnp.zeros_like(dqg_ref)

        @pl.when(f == 0)
        def _():
            dv = dm_ref[...].astype(F32)
            ddv, dq = _rms_bwd(dv, _rms_r(dv), qg_ref[...], 0.5 * dxo_ref[...])
            dqg_ref[...] += dq
            dd_scr[...] = ddv.astype(BF16)
            dd_ref[...] = ddv.astype(BF16)
            dh_acc[...] = jnp.zeros_like(dh_acc)

        da = _mm_nt(dd_scr[...], wd_ref[...].reshape(fc, d)).astype(BF16)
        u = u_ref[...]
        _, silu, dsilu = _swiglu_bf16(g_ref[...], u)
        du = da * silu
        dg = da * u * dsilu
        dg_ref[...] = dg
        du_ref[...] = du
        dh_acc[...] += _mm(dg, wg_ref[...]) + _mm(du, wu_ref[...])

        @pl.when(f == n_f - 1)
        def _():
            xv = x_ref[...]
            dxv, dp = _rms_bwd(xv, _rms_r(xv), pg_ref[...], dh_acc[...])
            dpg_ref[...] += dp
            dx_ref[...] = dxo_ref[...] + dxv

    row = pl.BlockSpec((ts, d), lambda i, f: (i, 0))
    vec = pl.BlockSpec((1, d), lambda i, f: (0, 0))
    act = pl.BlockSpec((None, ts, fc), lambda i, f: (f, i, 0))
    return _pcall(
        body, (dxo, dmid, x, pre_g, post_g, g_s, u_s, wgu_t, wgu_t, wd), name=name, grid=(n_t, n_f),
        in_specs=[row, row, row, vec, vec, act, act] + _ffn_wspecs(d, fc, "tf"),
        out_specs=[row, row, act, act, vec, vec],
        out_shape=[SDS((s, d), F32), SDS((s, d), BF16), SDS((n_f, s, fc), BF16), SDS((n_f, s, fc), BF16),
                   SDS((1, d), F32), SDS((1, d), F32)],
        scratch_shapes=[pltpu.VMEM((ts, d), BF16), pltpu.VMEM((ts, d), F32)], carry=carry)


def _ffn_bwd_w(h, dd, g_s, u_s, dg, du, name, carry=()):
    s, d = h.shape
    n_f, _, fc = g_s.shape
    ts = _time_tile(s)
    n_t = s // ts

    def body(h_ref, dd_ref, g_ref, u_ref, dg_ref, du_ref, wg_ref, wu_ref, wd_ref, acc_g, acc_u, acc_d):
        i = pl.program_id(1)

        @pl.when(i == 0)
        def _():
            acc_g[...] = jnp.zeros_like(acc_g)
            acc_u[...] = jnp.zeros_like(acc_u)
            acc_d[...] = jnp.zeros_like(acc_d)

        a = _swiglu_bf16(g_ref[...], u_ref[...])[0]
        hv = h_ref[...]
        acc_g[...] += _mm_tn(dg_ref[...], hv)
        acc_u[...] += _mm_tn(du_ref[...], hv)
        acc_d[...] += _mm_tn(a, dd_ref[...])

        @pl.when(i == n_t - 1)
        def _():
            wg_ref[...] = acc_g[...].astype(BF16)
            wu_ref[...] = acc_u[...].astype(BF16)
            wd_ref[...] = acc_d[...].astype(BF16)

    row = pl.BlockSpec((ts, d), lambda f, i: (i, 0))
    act = pl.BlockSpec((None, ts, fc), lambda f, i: (f, i, 0))
    out = pl.BlockSpec((None, fc, d), lambda f, i: (f, 0, 0))
    return _pcall(
        body, (h, dd, g_s, u_s, dg, du), name=name, grid=(n_f, n_t),
        in_specs=[row, row, act, act, act, act], out_specs=[out, out, out],
        out_shape=[SDS((n_f, fc, d), BF16)] * 3,
        scratch_shapes=[pltpu.VMEM((fc, d), F32)] * 3, carry=carry)


def _ffn_bwd_w_send(h, dd, g_s, u_s, dg, du, recv_gu, recv_d, layer, name, carry=()):
    s, d = h.shape
    n_f, _, fc = g_s.shape
    ts = _time_tile(s)
    n_t = s // ts
    half = fc // 2

    def chunk_of(step):
        return (step + 2 * lax.axis_index("x") + lax.axis_index("y")) % n_f

    def body(h_ref, dd_ref, g_ref, u_ref, dg_ref, du_ref, _rgu_in, _rd_in, rgu_ref, rd_ref,
             acc_g, acc_u, acc_d, st_g, st_u, st_d, send_sems, recv_sems, local_sems):
        f, i = pl.program_id(0), pl.program_id(1)
        me = _my_pos()[3]

        @pl.when(i == 0)
        def _():
            acc_g[...] = jnp.zeros_like(acc_g)
            acc_u[...] = jnp.zeros_like(acc_u)
            acc_d[...] = jnp.zeros_like(acc_d)

        a = _swiglu_bf16(g_ref[...], u_ref[...])[0]
        hv = h_ref[...]
        acc_g[...] += _mm_tn(dg_ref[...], hv)
        acc_u[...] += _mm_tn(du_ref[...], hv)
        acc_d[...] += _mm_tn(a, dd_ref[...])

        def messages(fs):
            c = chunk_of(fs)
            return [(st_g.at[fs], rgu_ref, 0, c, 0), (st_u.at[fs], rgu_ref, 0, c + n_f, 1),
                    (st_d.at[fs, pl.ds(0, half)], rd_ref, 1, 2 * c, 2),
                    (st_d.at[fs, pl.ds(half, half)], rd_ref, 1, 2 * c + 1, 3)]

        def remote(fs, msg, src_dev):
            src, buf, row, p, j = msg
            return pltpu.make_async_remote_copy(
                src_ref=src, dst_ref=buf.at[src_dev, layer], send_sem=send_sems.at[fs, j],
                recv_sem=recv_sems.at[row, src_dev], device_id=_dev(p), device_id_type=MESH)

        def local(fs, msg):
            src, buf, _, p, j = msg
            return pltpu.make_async_copy(src, buf.at[p, layer], local_sems.at[fs, j])

        for fs in range(n_f):
            @pl.when((f == fs) & (i == n_t - 1))
            def _():
                st_g[fs] = acc_g[...].astype(BF16)
                st_u[fs] = acc_u[...].astype(BF16)
                st_d[fs] = acc_d[...].astype(BF16)
                for msg in messages(fs):
                    @pl.when(me != msg[3])
                    def _():
                        remote(fs, msg, me).start()

                    @pl.when(me == msg[3])
                    def _():
                        local(fs, msg).start()

        @pl.when((f == n_f - 1) & (i == n_t - 1))
        def _():
            for fs in range(n_f):
                for msg in messages(fs):
                    @pl.when(me != msg[3])
                    def _():
                        remote(fs, msg, me).wait_send()

                    @pl.when(me == msg[3])
                    def _():
                        local(fs, msg).wait()
            for src_dev in range(N_DEV):
                @pl.when(me != src_dev)
                def _():
                    remote(0, messages(0)[0], src_dev).wait_recv()
                    remote(0, messages(0)[2], src_dev).wait_recv()

    row = pl.BlockSpec((ts, d), lambda f, i: (i, 0))
    act = pl.BlockSpec((None, ts, fc), lambda f, i: (chunk_of(f), i, 0))
    return _pcall(
        body, (h, dd, g_s, u_s, dg, du, recv_gu, recv_d), name=name, grid=(n_f, n_t),
        in_specs=[row, row, act, act, act, act, ANY, ANY], out_specs=[ANY, ANY],
        out_shape=[SDS(recv_gu.shape, recv_gu.dtype), SDS(recv_d.shape, recv_d.dtype)],
        scratch_shapes=[pltpu.VMEM((fc, d), F32)] * 3 + [pltpu.VMEM((n_f, fc, d), BF16)] * 3
        + [pltpu.SemaphoreType.DMA((n_f, 4)), pltpu.SemaphoreType.DMA((2, N_DEV)), pltpu.SemaphoreType.DMA((n_f, 4))],
        carry=carry, body_aliases={6: 0, 7: 1})


_PROJ_WIDTHS = (W_A, W_A, W_B, KV_W, KV_W, 2 * W_C)


def _mix_in_fwd(x, pre_g, w_in_t, name, carry=()):
    s, d = x.shape
    ts = _time_tile(s)

    def body(x_ref, pg_ref, w_ref, hn_ref, *outs):
        xv = x_ref[...]
        hn = (xv * _rms_r(xv) * pg_ref[...]).astype(BF16)
        hn_ref[...] = hn
        proj = _mm_nt(hn, w_ref[...])
        off = 0
        for o_ref, w in zip(outs, _PROJ_WIDTHS):
            o_ref[...] = proj[:, off:off + w]
            off += w

    row = lambda w: pl.BlockSpec((ts, w), lambda i: (i, 0))
    return _pcall(
        body, (x, pre_g, w_in_t), name=name, grid=(s // ts,),
        in_specs=[row(d), pl.BlockSpec((1, d), lambda i: (0, 0)), pl.BlockSpec((D_IN_PROJ, d), lambda i: (0, 0))],
        out_specs=[row(d)] + [row(w) for w in _PROJ_WIDTHS],
        out_shape=[SDS((s, d), BF16)] + [SDS((s, w), F32) for w in _PROJ_WIDTHS], carry=carry)


def _mix_in_bwd(dres, x, pre_g, hn, w_in_t, dlx, dlg, dq, dk, dk_up, dv, dv_up, dglu, name, carry=()):
    s, d = x.shape
    ts = _time_tile(s)
    n_t = s // ts

    def body(dres_ref, x_ref, pg_ref, hn_ref, w_ref, dlx_ref, dlg_ref, dq_ref, dk_ref, dkn_ref,
             dv_ref, dvn_ref, dglu_ref, dx_ref, dw_ref, dpg_ref, acc):
        i = pl.program_id(0)

        @pl.when(i == 0)
        def _():
            acc[...] = jnp.zeros_like(acc)
            dpg_ref[...] = jnp.zeros_like(dpg_ref)

        def with_next(cur_ref, nxt_ref):
            nxt = jnp.where(i < n_t - 1, nxt_ref[...], 0.0)
            if ts == BLK:
                return cur_ref[...] + nxt
            return jnp.concatenate([cur_ref[:ts - BLK, :], cur_ref[ts - BLK:, :] + nxt], axis=0)

        dproj = jnp.concatenate([dlx_ref[...], dlg_ref[...], dq_ref[...], with_next(dk_ref, dkn_ref),
                                 with_next(dv_ref, dvn_ref), dglu_ref[...]], axis=1).astype(BF16)
        dhn = _mm(dproj, w_ref[...])
        acc[...] += _mm_tn(dproj, hn_ref[...])
        xv = x_ref[...]
        dxv, dp = _rms_bwd(xv, _rms_r(xv), pg_ref[...], dhn)
        dpg_ref[...] += dp
        dx_ref[...] = dres_ref[...] + dxv

        @pl.when(i == n_t - 1)
        def _():
            dw_ref[...] = acc[...].astype(BF16)

    row = lambda w: pl.BlockSpec((ts, w), lambda i: (i, 0))
    nxt = pl.BlockSpec((BLK, KV_W), lambda i: (jnp.minimum(i + 1, n_t - 1), 0))
    vec = pl.BlockSpec((1, d), lambda i: (0, 0))
    full = pl.BlockSpec((D_IN_PROJ, d), lambda i: (0, 0))
    return _pcall(
        body, (dres, x, pre_g, hn, w_in_t, dlx, dlg, dq, dk, dk_up, dv, dv_up, dglu), name=name, grid=(n_t,),
        in_specs=[row(d), row(d), vec, row(d), full, row(W_A), row(W_A), row(W_B), row(KV_W), nxt,
                  row(KV_W), nxt, row(2 * W_C)],
        out_specs=[row(d), full, vec],
        out_shape=[SDS((s, d), F32), SDS((D_IN_PROJ, d), BF16), SDS((1, d), F32)],
        scratch_shapes=[pltpu.VMEM((D_IN_PROJ, d), F32)], carry=carry)


def _lru_gates(xc, lru_p):
    cw_ref, cb_ref, wa_ref, ba_ref, wx_ref, bx_ref, lam_ref = lru_p
    c = cb_ref[...]
    for j in range(LRU_K):
        c = c + cw_ref[j:j + 1, :] * _shift_down(xc, LRU_K - 1 - j)[LRU_HALO:, :]
    r = _sigmoid(_mm(c, wa_ref[...]) + ba_ref[...])
    ig = _sigmoid(_mm(c, wx_ref[...]) + bx_ref[...])
    sp = _softplus(-lam_ref[...])
    log_a = -LRU_C * r * sp
    a = jnp.exp(log_a)
    m = jnp.sqrt(_neg_expm1(2.0 * log_a))
    return c, r, ig, sp, a, m


def _lru_pspecs():
    small = lambda r: pl.BlockSpec((r, W_A), lambda i: (0, 0))
    return [small(LRU_K), small(1), small(W_A), small(1), small(W_A), small(1), small(1)]


def _lru_fwd(lx, lg, lru_p, name, carry=()):
    s = lx.shape[0]
    ts = _time_tile(s)
    n8 = ts // LRU_HALO

    def body(lx_ref, lxp_ref, lg_ref, *rest):
        lru_p, (ya_ref, h_ref, hcarry) = rest[:7], rest[7:]
        i = pl.program_id(0)
        prev = jnp.where(i > 0, lxp_ref[...], 0.0)
        xc = jnp.concatenate([prev, lx_ref[...]], axis=0)
        c, r, ig, sp, a, m = _lru_gates(xc, lru_p)
        acc_a, acc_b = a, m * (ig * c)
        t = lax.broadcasted_iota(jnp.int32, a.shape, 0)
        k = 1
        while k < ts:
            keep = t >= k
            acc_b = jnp.where(keep, acc_a * _shift_down(acc_b, k) + acc_b, acc_b)
            acc_a = jnp.where(keep, acc_a * _shift_down(acc_a, k), acc_a)
            k *= 2
        h0 = jnp.where(i > 0, hcarry[...], 0.0)
        h = acc_b + acc_a * h0
        hcarry[...] = h[ts - 1:ts, :]
        h_ref[...] = h
        ya_ref[...] = _gelu(lg_ref[...])[0] * h

    row = pl.BlockSpec((ts, W_A), lambda i: (i, 0))
    prev8 = pl.BlockSpec((LRU_HALO, W_A), lambda i: (jnp.maximum(i * n8 - 1, 0), 0))
    return _pcall(
        body, (lx, lx, lg, *lru_p), name=name, grid=(s // ts,),
        in_specs=[row, prev8, row] + _lru_pspecs(), out_specs=[row, row],
        out_shape=[SDS((s, W_A), F32), SDS((s, W_A), F32)],
        scratch_shapes=[pltpu.VMEM((1, W_A), F32)], carry=carry)


def _lru_bwd(dya, lx, lg, h_s, lru_p, name, carry=()):
    s = lx.shape[0]
    ts = _time_tile(s)
    n_t = s // ts
    n8 = ts // LRU_HALO

    def body(dya_ref, lx_ref, lxp_ref, lg_ref, h_ref, hp_ref, *rest):
        lru_p = rest[:7]
        (dlx_ref, dlg_ref, dcw_ref, dcb_ref, dwa_ref, dba_ref, dwx_ref, dbx_ref, dlam_ref,
         carry_a, carry_l, carry_dc) = rest[7:]
        cw_ref, _, wa_ref, _, wx_ref, _, lam_ref = lru_p
        i = pl.program_id(0)
        first_tile = i == n_t - 1
        last_tile = i == 0

        @pl.when(i == 0)
        def _():
            for ref in (dcw_ref, dcb_ref, dwa_ref, dba_ref, dwx_ref, dbx_ref, dlam_ref):
                ref[...] = jnp.zeros_like(ref)

        prev = jnp.where(first_tile, 0.0, lxp_ref[...])
        xc = jnp.concatenate([prev, lx_ref[...]], axis=0)
        c, r, ig, sp, a, m = _lru_gates(xc, lru_p)
        h = h_ref[...]
        hcat = jnp.concatenate([jnp.where(first_tile, 0.0, hp_ref[...]), h], axis=0)
        h_m1 = _shift_down(hcat, 1)[LRU_HALO:, :]
        lg = lg_ref[...]
        ge, th = _gelu(lg)
        dya = dya_ref[...]
        dlg_ref[...] = dya * h * _dgelu(lg, th)
        dh = dya * ge
        t = lax.broadcasted_iota(jnp.int32, a.shape, 0)
        a_next = jnp.where(t < ts - 1, _shift_up(a, 1), jnp.where(last_tile, 0.0, carry_a[...]))
        acc_a, acc_b = a_next, dh
        k = 1
        while k < ts:
            keep = t < ts - k
            acc_b = jnp.where(keep, acc_a * _shift_up(acc_b, k) + acc_b, acc_b)
            acc_a = jnp.where(keep, acc_a * _shift_up(acc_a, k), acc_a)
            k *= 2
        lam_beyond = jnp.where(last_tile, 0.0, carry_l[...])
        lmb = acc_b + acc_a * lam_beyond
        carry_a[...] = a[0:1, :]
        carry_l[...] = lmb[0:1, :]
        gi = ig * c
        dgi = lmb * m
        dla = lmb * h_m1 * a - (lmb * gi) * (a * a) / m
        dr = dla * (-LRU_C * sp)
        dsp = jnp.sum(dla * (-LRU_C * r), axis=0, keepdims=True)
        dlam_ref[...] += -dsp * _sigmoid(-lam_ref[...])
        dra = dr * r * (1.0 - r)
        dia = dgi * c * ig * (1.0 - ig)
        dc = dgi * ig + _mm_nt(dra, wa_ref[...]) + _mm_nt(dia, wx_ref[...])
        dwa_ref[...] += _mm_tn(c, dra)
        dwx_ref[...] += _mm_tn(c, dia)
        dba_ref[...] += jnp.sum(dra, axis=0, keepdims=True)
        dbx_ref[...] += jnp.sum(dia, axis=0, keepdims=True)
        dcb_ref[...] += jnp.sum(dc, axis=0, keepdims=True)
        dcc = jnp.concatenate([dc, jnp.where(last_tile, 0.0, carry_dc[...])], axis=0)
        carry_dc[...] = dc[0:LRU_HALO, :]
        dlx = jnp.zeros_like(dc)
        for j in range(LRU_K):
            sh = LRU_K - 1 - j
            dcw_ref[j:j + 1, :] += jnp.sum(dc * _shift_down(xc, sh)[LRU_HALO:, :], axis=0, keepdims=True)
            dlx = dlx + cw_ref[j:j + 1, :] * _shift_up(dcc, sh)[:ts, :]
        dlx_ref[...] = dlx

    row = pl.BlockSpec((ts, W_A), lambda i: (n_t - 1 - i, 0))
    prev8 = pl.BlockSpec((LRU_HALO, W_A), lambda i: (jnp.maximum((n_t - 1 - i) * n8 - 1, 0), 0))
    small = lambda r: pl.BlockSpec((r, W_A), lambda i: (0, 0))
    return _pcall(
        body, (dya, lx, lx, lg, h_s, h_s, *lru_p), name=name, grid=(n_t,),
        in_specs=[row, row, prev8, row, row, prev8] + _lru_pspecs(),
        out_specs=[row, row, small(LRU_K), small(1), small(W_A), small(1), small(W_A), small(1), small(1)],
        out_shape=[SDS((s, W_A), F32), SDS((s, W_A), F32), SDS((LRU_K, W_A), F32), SDS((1, W_A), F32),
                   SDS((W_A, W_A), F32), SDS((1, W_A), F32), SDS((W_A, W_A), F32), SDS((1, W_A), F32),
                   SDS((1, W_A), F32)],
        scratch_shapes=[pltpu.VMEM((1, W_A), F32), pltpu.VMEM((1, W_A), F32), pltpu.VMEM((LRU_HALO, W_A), F32)],
        carry=carry)


_ATT_ROWS = N_Q_HEADS * BLK
_GRP_ROWS = Q_PER_KV * BLK


def _attn_stack(ref, rows, g):
    return jnp.concatenate([ref[rows, h * HEAD_DIM:(h + 1) * HEAD_DIM]
                            for h in range(g * Q_PER_KV, (g + 1) * Q_PER_KV)], axis=0)


def _attn_unstack(parts):
    return jnp.concatenate([p[j * BLK:(j + 1) * BLK, :] for p in parts for j in range(Q_PER_KV)], axis=1)


def _grp(x, g):
    return x[:, g * _GRP_ROWS:(g + 1) * _GRP_ROWS]


def _attn_block(q_ref, k_ref, kp_ref, v_ref, vp_ref, sink_row, i, b):
    rows, prev = slice(b * BLK, (b + 1) * BLK), slice((b - 1) * BLK, b * BLK)
    qs, kcs, kps, vcs, vps = [], [], [], [], []
    for g in range(N_KV_HEADS):
        cols = slice(g * HEAD_DIM, (g + 1) * HEAD_DIM)
        qs.append(_attn_stack(q_ref, rows, g))
        kcs.append(k_ref[rows, cols])
        vcs.append(v_ref[rows, cols])
        kps.append(kp_ref[:, cols] if b == 0 else k_ref[prev, cols])
        vps.append(vp_ref[:, cols] if b == 0 else v_ref[prev, cols])
    scale = 1.0 / math.sqrt(HEAD_DIM)
    sc = jnp.concatenate([_mm_nt(kcs[g], qs[g]) for g in range(N_KV_HEADS)], axis=1) * scale
    sp = jnp.concatenate([_mm_nt(kps[g], qs[g]) for g in range(N_KV_HEADS)], axis=1) * scale
    kj = lax.broadcasted_iota(jnp.int32, (BLK, _ATT_ROWS), 0)
    qi = lax.broadcasted_iota(jnp.int32, (BLK, _ATT_ROWS), 1) & (BLK - 1)
    sc = jnp.where(kj <= qi, sc, NEG_BIG)
    sp = jnp.where((kj > qi) if b > 0 else ((kj > qi) & (i > 0)), sp, NEG_BIG)
    m = jnp.maximum(jnp.maximum(jnp.max(sc, axis=0, keepdims=True), jnp.max(sp, axis=0, keepdims=True)), sink_row)
    pc = jnp.exp(sc - m)
    pp = jnp.exp(sp - m)
    es = jnp.exp(sink_row - m)
    inv = 1.0 / (jnp.sum(pc, axis=0, keepdims=True) + jnp.sum(pp, axis=0, keepdims=True) + es)
    return qs, kcs, kps, vcs, vps, pc * inv, pp * inv, es * inv


def _attn_specs(s, ts):
    bpt = ts // BLK
    tile = lambda w: pl.BlockSpec((ts, w), lambda i: (i, 0))
    prv = pl.BlockSpec((BLK, KV_W), lambda i: (jnp.maximum(i * bpt - 1, 0), 0))
    sink = pl.BlockSpec((1, _ATT_ROWS), lambda i: (0, 0))
    return bpt, tile, prv, sink


def _attn_fwd(q, k, v, sink_row, name, carry=()):
    s = q.shape[0]
    ts = _time_tile(s)
    bpt, tile, prv, sink = _attn_specs(s, ts)

    def body(q_ref, k_ref, kp_ref, v_ref, vp_ref, sk_ref, y_ref):
        i = pl.program_id(0)
        for b in range(bpt):
            _, _, _, vcs, vps, pc, pp, _ = _attn_block(q_ref, k_ref, kp_ref, v_ref, vp_ref, sk_ref[...], i, b)
            outs = [_mm_tn(_grp(pc, g), vcs[g]) + _mm_tn(_grp(pp, g), vps[g]) for g in range(N_KV_HEADS)]
            y_ref[b * BLK:(b + 1) * BLK, :] = _attn_unstack(outs)

    return _pcall(
        body, (q, k, k, v, v, sink_row), name=name, grid=(s // ts,),
        in_specs=[tile(W_B), tile(KV_W), prv, tile(KV_W), prv, sink],
        out_specs=[tile(W_B)], out_shape=[SDS((s, W_B), F32)], carry=carry)


def _attn_bwd(dy, q, k, v, sinks, name, carry=()):
    s = q.shape[0]
    ts = _time_tile(s)
    n_t = s // ts
    bpt, tile, prv, sink = _attn_specs(s, ts)

    def body(dy_ref, q_ref, k_ref, kp_ref, v_ref, vp_ref, sk_ref, dq_ref, dk_ref, dv_ref, dku_ref, dvu_ref, dsk_ref):
        i = pl.program_id(0)

        @pl.when(i == 0)
        def _():
            dsk_ref[...] = jnp.zeros_like(dsk_ref)

        scale = 1.0 / math.sqrt(HEAD_DIM)
        groups = range(N_KV_HEADS)
        head_row = lax.broadcasted_iota(jnp.int32, (N_Q_HEADS, BLK), 0)
        dsk = jnp.zeros((N_Q_HEADS, BLK), F32)
        dk_blocks, dv_blocks = [], []
        for b in range(bpt):
            rows = slice(b * BLK, (b + 1) * BLK)
            qs, kcs, kps, vcs, vps, pc, pp, ps = _attn_block(q_ref, k_ref, kp_ref, v_ref, vp_ref, sk_ref[...], i, b)
            dos = [_attn_stack(dy_ref, rows, g) for g in groups]
            dpc = jnp.concatenate([_mm_nt(vcs[g], dos[g]) for g in groups], axis=1)
            dpp = jnp.concatenate([_mm_nt(vps[g], dos[g]) for g in groups], axis=1)
            delta = jnp.sum(pc * dpc, axis=0, keepdims=True) + jnp.sum(pp * dpp, axis=0, keepdims=True)
            dsc = pc * (dpc - delta) * scale
            dsp = pp * (dpp - delta) * scale
            dq_ref[rows, :] = _attn_unstack([_mm_tn(_grp(dsc, g), kcs[g]) + _mm_tn(_grp(dsp, g), kps[g])
                                             for g in groups])
            dk_blocks.append(jnp.concatenate([_mm(_grp(dsc, g), qs[g]) for g in groups], axis=1))
            dv_blocks.append(jnp.concatenate([_mm(_grp(pc, g), dos[g]) for g in groups], axis=1))
            dkp = jnp.concatenate([_mm(_grp(dsp, g), qs[g]) for g in groups], axis=1)
            dvp = jnp.concatenate([_mm(_grp(pp, g), dos[g]) for g in groups], axis=1)
            if b == 0:
                dku_ref[...] = dkp
                dvu_ref[...] = dvp
            else:
                dk_blocks[b - 1] = dk_blocks[b - 1] + dkp
                dv_blocks[b - 1] = dv_blocks[b - 1] + dvp
            dsink = -ps * delta
            for h in range(N_Q_HEADS):
                dsk = dsk + jnp.where(head_row == h, jnp.sum(dsink[:, h * BLK:(h + 1) * BLK], axis=1, keepdims=True), 0.0)
        for b in range(bpt):
            dk_ref[b * BLK:(b + 1) * BLK, :] = dk_blocks[b]
            dv_ref[b * BLK:(b + 1) * BLK, :] = dv_blocks[b]
        dsk_ref[...] += dsk

    up = pl.BlockSpec((BLK, KV_W), lambda i: (i, 0))
    return _pcall(
        body, (dy, q, k, k, v, v, sinks), name=name, grid=(n_t,),
        in_specs=[tile(W_B), tile(W_B), tile(KV_W), prv, tile(KV_W), prv, sink],
        out_specs=[tile(W_B), tile(KV_W), tile(KV_W), up, up, pl.BlockSpec((N_Q_HEADS, BLK), lambda i: (0, 0))],
        out_shape=[SDS((s, W_B), F32), SDS((s, KV_W), F32), SDS((s, KV_W), F32), SDS((n_t * BLK, KV_W), F32),
                   SDS((n_t * BLK, KV_W), F32), SDS((N_Q_HEADS, BLK), F32)], carry=carry)


def _cc_recompute(glu_ref, glup_ref, cw_ref, cb_ref, first_tile):
    prev = jnp.where(first_tile, 0.0, glup_ref[...])
    ge = jnp.concatenate([prev, glu_ref[...]], axis=0)
    y0 = ge[:, :W_C] * _sigmoid(ge[:, W_C:])
    y1 = cb_ref[...]
    for j in range(CC_K):
        y1 = y1 + cw_ref[j:j + 1, :] * _shift_down(y0, CC_K - 1 - j)[CC_HALO:, :]
    return y0, y1


def _ln_stats(y1):
    mu = jnp.mean(y1, axis=-1, keepdims=True)
    xc = y1 - mu
    rstd = lax.rsqrt(jnp.mean(xc * xc, axis=-1, keepdims=True) + LN_EPS)
    return xc * rstd, rstd


def _cc_specs(s, ts):
    n32 = ts // CC_HALO
    row = lambda w: pl.BlockSpec((ts, w), lambda i: (i, 0))
    prev = pl.BlockSpec((CC_HALO, 2 * W_C), lambda i: (jnp.maximum(i * n32 - 1, 0), 0))
    small = lambda r: pl.BlockSpec((r, W_C), lambda i: (0, 0))
    return row, prev, small


def _cc_fwd(glu, cw, cb, lng, lnb, name, carry=()):
    s = glu.shape[0]
    ts = _time_tile(s)
    row, prev, small = _cc_specs(s, ts)

    def body(glu_ref, glup_ref, cw_ref, cb_ref, lng_ref, lnb_ref, y_ref):
        _, y1 = _cc_recompute(glu_ref, glup_ref, cw_ref, cb_ref, pl.program_id(0) == 0)
        xhat, _ = _ln_stats(y1)
        z = xhat * lng_ref[...] + lnb_ref[...]
        y_ref[...] = z * _sigmoid(z)

    return _pcall(
        body, (glu, glu, cw, cb, lng, lnb), name=name, grid=(s // ts,),
        in_specs=[row(2 * W_C), prev, small(CC_HALO), small(1), small(1), small(1)],
        out_specs=[row(W_C)], out_shape=[SDS((s, W_C), F32)], carry=carry)


def _cc_bwd_conv(dy, glu, cw, cb, lng, lnb, name, carry=()):
    s = glu.shape[0]
    ts = _time_tile(s)
    row, prev, small = _cc_specs(s, ts)

    def body(dy_ref, glu_ref, glup_ref, cw_ref, cb_ref, lng_ref, lnb_ref, dy1_ref, dcw_ref, dcb_ref, dlng_ref, dlnb_ref):
        i = pl.program_id(0)

        @pl.when(i == 0)
        def _():
            for ref in (dcw_ref, dcb_ref, dlng_ref, dlnb_ref):
                ref[...] = jnp.zeros_like(ref)

        y0, y1 = _cc_recompute(glu_ref, glup_ref, cw_ref, cb_ref, i == 0)
        xhat, rstd = _ln_stats(y1)
        z = xhat * lng_ref[...] + lnb_ref[...]
        dz = dy_ref[...] * _dsilu(z, _sigmoid(z))
        dlng_ref[...] += jnp.sum(dz * xhat, axis=0, keepdims=True)
        dlnb_ref[...] += jnp.sum(dz, axis=0, keepdims=True)
        dxh = dz * lng_ref[...]
        dy1 = rstd * (dxh - jnp.mean(dxh, axis=-1, keepdims=True) - xhat * jnp.mean(dxh * xhat, axis=-1, keepdims=True))
        dy1_ref[...] = dy1
        dcb_ref[...] += jnp.sum(dy1, axis=0, keepdims=True)
        for j in range(CC_K):
            dcw_ref[j:j + 1, :] += jnp.sum(dy1 * _shift_down(y0, CC_K - 1 - j)[CC_HALO:, :], axis=0, keepdims=True)

    return _pcall(
        body, (dy, glu, glu, cw, cb, lng, lnb), name=name, grid=(s // ts,),
        in_specs=[row(W_C), row(2 * W_C), prev, small(CC_HALO), small(1), small(1), small(1)],
        out_specs=[row(W_C), small(CC_HALO), small(1), small(1), small(1)],
        out_shape=[SDS((s, W_C), F32), SDS((CC_HALO, W_C), F32)] + [SDS((1, W_C), F32)] * 3, carry=carry)


def _cc_bwd_glu(dy1, glu, cw, name, carry=()):
    s = glu.shape[0]
    ts = _time_tile(s)
    n_t = s // ts
    n32 = ts // CC_HALO

    def body(dy1_ref, dyn_ref, glu_ref, cw_ref, dglu_ref):
        i = pl.program_id(0)
        dcat = jnp.concatenate([dy1_ref[...], jnp.where(i < n_t - 1, dyn_ref[...], 0.0)], axis=0)
        dy0 = jnp.zeros((ts, W_C), F32)
        for j in range(CC_K):
            dy0 = dy0 + cw_ref[j:j + 1, :] * _shift_up(dcat, CC_K - 1 - j)[:ts, :]
        a = glu_ref[:, :W_C]
        sg = _sigmoid(glu_ref[:, W_C:])
        dglu_ref[...] = jnp.concatenate([dy0 * sg, dy0 * a * sg * (1.0 - sg)], axis=1)

    row = lambda w: pl.BlockSpec((ts, w), lambda i: (i, 0))
    nxt = pl.BlockSpec((CC_HALO, W_C), lambda i: (jnp.minimum((i + 1) * n32, s // CC_HALO - 1), 0))
    return _pcall(
        body, (dy1, dy1, glu, cw), name=name, grid=(n_t,),
        in_specs=[row(W_C), nxt, row(2 * W_C), pl.BlockSpec((CC_HALO, W_C), lambda i: (0, 0))],
        out_specs=[row(2 * W_C)], out_shape=[SDS((s, 2 * W_C), F32)], carry=carry)


_MIX_OFFS = ((0, W_A), (W_A, W_A + W_B), (W_A + W_B, W_A + W_B + W_C))


def _mix_out_fwd(x, ya, yb, yc, group_g, w_out, post_g, name, carry=()):
    s, d = x.shape
    ts = _time_tile(s)
    dm = w_out.shape[0]

    def body(x_ref, ya_ref, yb_ref, yc_ref, gg_ref, w_ref, qg_ref, xo_ref, o_ref):
        parts = []
        for y_ref, (lo, hi) in zip((ya_ref, yb_ref, yc_ref), _MIX_OFFS):
            yv = y_ref[...]
            parts.append(yv * _rms_r(yv) * gg_ref[:, lo:hi])
        o = _mm(jnp.concatenate(parts, axis=1), w_ref[...])
        o_ref[...] = o
        xo_ref[...] = x_ref[...] + o * _rms_r(o) * qg_ref[...]

    row = lambda w: pl.BlockSpec((ts, w), lambda i: (i, 0))
    return _pcall(
        body, (x, ya, yb, yc, group_g, w_out, post_g), name=name, grid=(s // ts,),
        in_specs=[row(d), row(W_A), row(W_B), row(W_C), pl.BlockSpec((1, dm), lambda i: (0, 0)),
                  pl.BlockSpec((dm, d), lambda i: (0, 0)), pl.BlockSpec((1, d), lambda i: (0, 0))],
        out_specs=[row(d), row(d)], out_shape=[SDS((s, d), F32), SDS((s, d), F32)], carry=carry)


def _mix_out_bwd(dxo, o, ya, yb, yc, group_g, w_out, post_g, name, carry=()):
    s, d = o.shape
    ts = _time_tile(s)
    n_t = s // ts
    dm = w_out.shape[0]

    def body(dxo_ref, o_ref, ya_ref, yb_ref, yc_ref, gg_ref, w_ref, qg_ref,
             dya_ref, dyb_ref, dyc_ref, dw_ref, dqg_ref, dgg_ref, acc):
        i = pl.program_id(0)

        @pl.when(i == 0)
        def _():
            acc[...] = jnp.zeros_like(acc)
            dqg_ref[...] = jnp.zeros_like(dqg_ref)
            dgg_ref[...] = jnp.zeros_like(dgg_ref)

        ov = o_ref[...]
        do, dq = _rms_bwd(ov, _rms_r(ov), qg_ref[...], dxo_ref[...])
        dqg_ref[...] += dq
        do = do.astype(BF16)
        dyn = _mm_nt(do, w_ref[...])
        parts, dggs = [], []
        for y_ref, dy_ref, (lo, hi) in zip((ya_ref, yb_ref, yc_ref), (dya_ref, dyb_ref, dyc_ref), _MIX_OFFS):
            yv = y_ref[...]
            r = _rms_r(yv)
            gg = gg_ref[:, lo:hi]
            parts.append(yv * r * gg)
            dyv, dg = _rms_bwd(yv, r, gg, dyn[:, lo:hi])
            dy_ref[...] = dyv
            dggs.append(dg)
        dgg_ref[...] += jnp.concatenate(dggs, axis=1)
        acc[...] += _mm_tn(jnp.concatenate(parts, axis=1), do)

        @pl.when(i == n_t - 1)
        def _():
            dw_ref[...] = acc[...].astype(BF16)

    row = lambda w: pl.BlockSpec((ts, w), lambda i: (i, 0))
    full = pl.BlockSpec((dm, d), lambda i: (0, 0))
    return _pcall(
        body, (dxo, o, ya, yb, yc, group_g, w_out, post_g), name=name, grid=(n_t,),
        in_specs=[row(d), row(d), row(W_A), row(W_B), row(W_C), pl.BlockSpec((1, dm), lambda i: (0, 0)), full,
                  pl.BlockSpec((1, d), lambda i: (0, 0))],
        out_specs=[row(W_A), row(W_B), row(W_C), full, pl.BlockSpec((1, d), lambda i: (0, 0)),
                   pl.BlockSpec((1, dm), lambda i: (0, 0))],
        out_shape=[SDS((s, W_A), F32), SDS((s, W_B), F32), SDS((s, W_C), F32), SDS((dm, d), BF16),
                   SDS((1, d), F32), SDS((1, dm), F32)],
        scratch_shapes=[pltpu.VMEM((dm, d), F32)], carry=carry)


def _loss_head(y, target, name):
    s, d = y.shape
    ts = _time_tile(s)

    def body(y_ref, t_ref, loss_ref, dy_ref):
        @pl.when(pl.program_id(0) == 0)
        def _():
            loss_ref[...] = jnp.zeros_like(loss_ref)

        err = y_ref[...] - t_ref[...]
        dy_ref[...] = err * (1.0 / d)
        per_tok = jnp.mean(err * err, axis=-1, keepdims=True)
        loss_ref[...] += 0.5 * jnp.sum(per_tok, axis=0, keepdims=True)

    row = pl.BlockSpec((ts, d), lambda i: (i, 0))
    return _pcall(body, (y, target), name=name, grid=(s // ts,), in_specs=[row, row],
                  out_specs=[pl.BlockSpec((1, BLK), lambda i: (0, 0)), row],
                  out_shape=[SDS((1, BLK), F32), SDS((s, d), F32)])[0]


def _adamw_math(w, g, m, v):
    m = ADAM_B1 * m + (1.0 - ADAM_B1) * g
    v = ADAM_B2 * v + (1.0 - ADAM_B2) * (g * g)
    m_hat = m / (1.0 - ADAM_B1 ** ADAM_STEP)
    v_hat = v / (1.0 - ADAM_B2 ** ADAM_STEP)
    delta = -ADAM_LR * (m_hat / (jnp.sqrt(v_hat) + ADAM_EPS) + ADAM_WD * w)
    return delta, m, v


def _row_tile(rows, cap=256):
    best = None
    for t in range(16, min(rows, cap) + 1, 16):
        if rows % t == 0:
            best = t
    return best if best is not None else rows


def _reduce_adamw(recv, w, m, v, name):
    n_l, r, c = w.shape
    tr = _row_tile(r)

    def body(recv_ref, w_ref, m_ref, v_ref, g_ref, d_ref, nm_ref, nv_ref):
        g = recv_ref[0].astype(F32)
        for p in range(1, N_DEV):
            g = g + recv_ref[p].astype(F32)
        g_ref[...] = g
        d_ref[...], nm_ref[...], nv_ref[...] = _adamw_math(w_ref[...], g, m_ref[...], v_ref[...])

    blk = pl.BlockSpec((None, tr, c), lambda l, i: (l, i, 0))
    return _pcall(
        body, (recv, w, m, v), name=name, grid=(n_l, r // tr),
        in_specs=[pl.BlockSpec((N_DEV, None, tr, c), lambda l, i: (0, l, i, 0)), blk, blk, blk],
        out_specs=[blk] * 4, out_shape=[SDS(w.shape, F32)] * 4)[0]


def _reduce_adamw_small(parts, w, m, v, name):
    def body(p_ref, w_ref, m_ref, v_ref, g_ref, d_ref, nm_ref, nv_ref):
        g = p_ref[0]
        for p in range(1, N_DEV):
            g = g + p_ref[p]
        g_ref[...] = g
        d_ref[...], nm_ref[...], nv_ref[...] = _adamw_math(w_ref[...], g, m_ref[...], v_ref[...])

    vm = pl.BlockSpec(memory_space=pltpu.VMEM)
    return pl.pallas_call(body, name=name, in_specs=[vm] * 4, out_specs=[vm] * 4, out_shape=[SDS(w.shape, F32)] * 4,
                          compiler_params=pltpu.CompilerParams(vmem_limit_bytes=VMEM_LIMIT))(parts, w, m, v)


def _rows_of(shape):
    return -(-math.prod(shape) // (8 * BLK)) * 8


def _pack(arrs):
    rows = []
    for a in arrs:
        n, r = math.prod(a.shape), _rows_of(a.shape)
        if n % BLK == 0:
            part = a.reshape(n // BLK, BLK)
            rows.append(part if n // BLK == r else jnp.pad(part, ((0, r - n // BLK), (0, 0))))
        else:
            rows.append(jnp.pad(a.reshape(-1), (0, r * BLK - n)).reshape(r, BLK))
    return jnp.concatenate(rows, axis=0)


def _unpack(packed, shapes):
    out, row = [], 0
    for shp in shapes:
        n, r = math.prod(shp), _rows_of(shp)
        if n % BLK == 0:
            out.append(packed[row:row + n // BLK].reshape(shp))
        else:
            out.append(packed[row:row + r].reshape(-1)[:n].reshape(shp))
        row += r
    return out


def _block_diag(w):
    nb, bw, _ = w.shape
    eye = jnp.eye(nb, dtype=w.dtype)
    return (eye[:, None, :, None] * w[:, :, None, :]).reshape(nb * bw, nb * bw)


def _diag_blocks(wd, nb):
    bw = wd.shape[0] // nb
    return jnp.stack([wd[b * bw:(b + 1) * bw, b * bw:(b + 1) * bw] for b in range(nb)])


WEIGHT_NAMES = ['ffn1_pre_g', 'ffn1_w_gu', 'ffn1_w_down', 'ffn1_post_g', 'mix_pre_g', 'w_in', 'lru_conv_w', 'lru_conv_b',
                'lru_w_a', 'lru_b_a', 'lru_w_x', 'lru_b_x', 'lru_lambda', 'attn_sinks', 'conv_w', 'conv_b', 'conv_ln_g',
                'conv_ln_b', 'group_g', 'w_out', 'mix_post_g', 'ffn2_pre_g', 'ffn2_w_gu', 'ffn2_w_down', 'ffn2_post_g']
BIG = ('ffn1_w_gu', 'ffn1_w_down', 'w_in', 'w_out', 'ffn2_w_gu', 'ffn2_w_down')
TRANSPOSED = ('ffn1_w_gu', 'ffn2_w_gu', 'w_in')
SMALL = tuple(k for k in WEIGHT_NAMES if k not in BIG)
CHANNEL_SHARDED = ('lru_conv_w', 'conv_w')


def _step(x, target, w, m, v):
    n_l = w['ffn1_pre_g'].shape[0]
    assert n_l == 2, "the exchange schedule below is laid out for two layers"
    s, d = x.shape[1], x.shape[2]
    x = x.reshape(s, d)
    target = target.reshape(s, d)
    me = _my_pos()[3]
    tview = lambda t, k: jnp.swapaxes(t[k], 1, 2) if k in TRANSPOSED else t[k]
    wb = {k: tview(w, k).astype(BF16) for k in BIG}
    vec = lambda name, l: w[name][l][None, :]

    conv_shard = _pack([w['lru_conv_w'], w['conv_w']])
    g0 = _all_gather([(wb['ffn1_w_gu'], 0), (wb['ffn1_w_down'], 0), (wb['w_in'], 0), (wb['w_out'], 0),
                      (conv_shard, None)], "all_gather_first")
    wts = [dict(), dict()]
    wts[0]['ffn1_w_gu'], wts[0]['ffn1_w_down'], wts[0]['w_in'], wts[0]['w_out'], conv_g = g0
    ch = W_A // N_DEV
    conv_parts = [_unpack(conv_g[p], [(n_l, LRU_K, ch), (n_l, CC_K, ch)]) for p in range(N_DEV)]
    lru_cw = jnp.concatenate([cp[0] for cp in conv_parts], axis=-1)
    cc_cw = jnp.concatenate([cp[1] for cp in conv_parts], axis=-1)
    cc_cw = jnp.pad(cc_cw, ((0, 0), (0, CC_HALO - CC_K), (0, 0)))

    gather_plan = {
        ('ffn1', 0): [('A', 'f2_0', ('ffn2_w_gu', 'ffn2_w_down'), 0)],
        ('mix_in', 0): [('B', 'f2_0')],
        ('attn', 0): [('A', 'g1_1', ('ffn1_w_gu',), 1)],
        ('cconv', 0): [('B', 'g1_1')],
        ('ffn2', 0): [('D', None, ('ffn1_w_down',), 1), ('A', 'wi_1', ('w_in',), 1), ('A', 'wo_1', ('w_out',), 1)],
        ('ffn1', 1): [('A', 'f2_1', ('ffn2_w_gu', 'ffn2_w_down'), 1), ('B', 'wi_1'), ('B', 'wo_1')],
        ('mix_in', 1): [('B', 'f2_1')],
    }
    pend = {}

    def fwd(kernel_name, l, fn, *args):
        plan = gather_plan.get((kernel_name, l), [])
        carry = []
        for st in plan:
            if st[0] == 'B':
                carry.append(_gather_b(pend[st[1]][2]))
            else:
                carry.append(_gather_a([(wb[k], st[3]) for k in st[2]], two_level=st[0] == 'A'))
        outs, ex = fn(*args, f"{kernel_name}_fwd_l{l}", carry)
        for st, bufs in zip(plan, ex):
            if st[0] == 'A':
                pend[st[1]] = (st[2], st[3], bufs)
            else:
                names, wl = (st[2], st[3]) if st[0] == 'D' else pend.pop(st[1])[:2]
                for k, b in zip(names, bufs):
                    wts[wl][k] = b
        return outs

    saved = []
    h = x
    for l in range(n_l):
        sv = {'x0': h}
        lw = wts[l]
        x1, sv['h1'], sv['g1'], sv['u1'], sv['d1'] = fwd(
            'ffn1', l, _ffn_fwd, h, vec('ffn1_pre_g', l), vec('ffn1_post_g', l), lw['ffn1_w_gu'], lw['ffn1_w_down'])
        sv['x1'] = x1
        sv['hn'], lx, lg, q, k, vv, glu = fwd('mix_in', l, _mix_in_fwd, x1, vec('mix_pre_g', l),
                                              lw['w_in'].reshape(D_IN_PROJ, d))
        sv.update(lx=lx, lg=lg, q=q, k=k, v=vv, glu=glu)
        lru_p = (lru_cw[l], vec('lru_conv_b', l), _block_diag(w['lru_w_a'][l]).astype(BF16), vec('lru_b_a', l),
                 _block_diag(w['lru_w_x'][l]).astype(BF16), vec('lru_b_x', l), vec('lru_lambda', l))
        cc_p = (cc_cw[l], vec('conv_b', l), vec('conv_ln_g', l), vec('conv_ln_b', l))
        sv.update(lru_p=lru_p, cc_p=cc_p)
        sv['ya'], sv['hs'] = fwd('lru', l, _lru_fwd, lx, lg, lru_p)
        sv['sink_row'] = jnp.repeat(w['attn_sinks'][l], BLK)[None, :]
        (sv['yb'],) = fwd('attn', l, _attn_fwd, q, k, vv, sv['sink_row'])
        (sv['yc'],) = fwd('cconv', l, _cc_fwd, glu, *cc_p)
        x2, sv['o'] = fwd('mix_out', l, _mix_out_fwd, x1, sv['ya'], sv['yb'], sv['yc'], vec('group_g', l),
                          lw['w_out'].reshape(-1, d), vec('mix_post_g', l))
        sv['x2'] = x2
        h, sv['h2'], sv['g2'], sv['u2'], sv['d2'] = fwd(
            'ffn2', l, _ffn_fwd, x2, vec('ffn2_pre_g', l), vec('ffn2_post_g', l), lw['ffn2_w_gu'], lw['ffn2_w_down'])
        saved.append(sv)

    loss_row, dh = _loss_head(h, target, "loss_head")

    recv = {k: None for k in BIG}
    ready = {}
    small = [dict() for _ in range(n_l)]

    def exchange(keys):
        return _grad_x([(ready.pop(key), key[1], recv[key[0]]) for key in keys], n_l)

    def received(keys, bufs):
        for key, b in zip(keys, bufs):
            recv[key[0]] = b

    def run(fn, *args, keys=()):
        outs, ex = fn(*args, carry=[exchange(keys)] if keys else [])
        if keys:
            received(keys, ex[0])
        return outs

    for l in reversed(range(n_l)):
        sv, sg, lw = saved[l], small[l], wts[l]
        keys = [] if l == n_l - 1 else [('ffn1_w_gu', l + 1)]
        dx2, dd, dg, du, sg['ffn2_pre_g'], sg['ffn2_post_g'] = run(
            _ffn_bwd_act, dh, sv['d2'], sv['x2'], vec('ffn2_pre_g', l), vec('ffn2_post_g', l), sv['g2'], sv['u2'],
            lw['ffn2_w_gu'], lw['ffn2_w_down'], f"ffn2_bwd_act_l{l}", keys=keys)
        keys = [] if l == n_l - 1 else [('ffn1_w_down', l + 1)]
        dwg, dwu, dwd = run(_ffn_bwd_w, sv['h2'], dd, sv['g2'], sv['u2'], dg, du, f"ffn2_bwd_w_l{l}", keys=keys)
        ready[('ffn2_w_gu', l)] = [dwg, dwu]
        ready[('ffn2_w_down', l)] = [dwd.reshape(N_DEV, -1, d)]
        dya, dyb, dyc, dw_out, sg['mix_post_g'], sg['group_g'] = run(
            _mix_out_bwd, dx2, sv['o'], sv['ya'], sv['yb'], sv['yc'], vec('group_g', l), lw['w_out'].reshape(-1, d),
            vec('mix_post_g', l), f"mix_out_bwd_l{l}")
        ready[('w_out', l)] = [dw_out.reshape(N_DEV, -1, d)]
        (dlx, dlg, sg['lru_conv_w'], sg['lru_conv_b'], dwa, sg['lru_b_a'], dwx, sg['lru_b_x'],
         sg['lru_lambda']) = run(_lru_bwd, dya, sv['lx'], sv['lg'], sv['hs'], sv['lru_p'], f"lru_bwd_l{l}")
        sg['lru_w_a'] = _diag_blocks(dwa, A_BLOCKS)
        sg['lru_w_x'] = _diag_blocks(dwx, A_BLOCKS)
        dq, dk, dv, dk_up, dv_up, dsk = run(_attn_bwd, dyb, sv['q'], sv['k'], sv['v'], sv['sink_row'],
                                            f"attn_bwd_l{l}", keys=[('ffn2_w_down', l)] if l == 0 else [])
        sg['attn_sinks'] = dsk[:, 0]
        dy1, dcw, sg['conv_b'], sg['conv_ln_g'], sg['conv_ln_b'] = run(
            _cc_bwd_conv, dyc, sv['glu'], *sv['cc_p'], f"cconv_bwd_conv_l{l}")
        sg['conv_w'] = dcw[:CC_K]
        (dglu,) = run(_cc_bwd_glu, dy1, sv['glu'], sv['cc_p'][0], f"cconv_bwd_glu_l{l}")
        dx1, dw_in, sg['mix_pre_g'] = run(
            _mix_in_bwd, dx2, sv['x1'], vec('mix_pre_g', l), sv['hn'], lw['w_in'].reshape(D_IN_PROJ, d),
            dlx, dlg, dq, dk, dk_up, dv, dv_up, dglu, f"mix_in_bwd_l{l}", keys=[('w_out', l)])
        ready[('w_in', l)] = [dw_in.reshape(N_DEV, -1, d)]
        dh, dd, dg, du, sg['ffn1_pre_g'], sg['ffn1_post_g'] = run(
            _ffn_bwd_act, dx1, sv['d1'], sv['x0'], vec('ffn1_pre_g', l), vec('ffn1_post_g', l), sv['g1'], sv['u1'],
            lw['ffn1_w_gu'], lw['ffn1_w_down'], f"ffn1_bwd_act_l{l}", keys=[('ffn2_w_gu', l), ('w_in', l)])
        if l > 0:
            dwg, dwu, dwd = run(_ffn_bwd_w, sv['h1'], dd, sv['g1'], sv['u1'], dg, du, f"ffn1_bwd_w_l{l}",
                                keys=[('ffn2_w_down', l)])
            ready[('ffn1_w_gu', l)] = [dwg, dwu]
            ready[('ffn1_w_down', l)] = [dwd.reshape(N_DEV, -1, d)]
        else:
            part = _pack([jnp.stack([small[j][k] for j in range(n_l)]) for k in SMALL] + [loss_row])
            (recv['ffn1_w_gu'], recv['ffn1_w_down']), ex = _ffn_bwd_w_send(
                sv['h1'], dd, sv['g1'], sv['u1'], dg, du, recv['ffn1_w_gu'], recv['ffn1_w_down'], 0, "ffn1_bwd_w_send_l0",
                [_gather_a([(part, None)], two_level=False)])
            small_parts = ex[0][0]
    grad_x = dh.reshape(1, s, d)

    out = {}
    for k in BIG:
        res = _reduce_adamw(recv[k], tview(w, k), tview(m, k), tview(v, k), f"reduce_adamw_{k}")
        out[k] = [jnp.swapaxes(r, 1, 2) for r in res] if k in TRANSPOSED else res

    small_shapes = [(n_l,) + tuple(small[0][k].shape) for k in SMALL]

    def widen(t, k):
        if k not in CHANNEL_SHARDED:
            return t.reshape((n_l,) + tuple(small[0][k].shape))
        full = jnp.zeros((n_l,) + tuple(small[0][k].shape), F32)
        return lax.dynamic_update_slice_in_dim(full, t, me * ch, axis=2)

    no_w = jnp.zeros(loss_row.shape, F32)
    packed = [_pack([widen(src[k], k) for k in SMALL] + [no_w]) for src in (w, m, v)]
    res = _reduce_adamw_small(small_parts, *packed, "reduce_adamw_small")
    loss = _unpack(res[0], small_shapes + [loss_row.shape])[-1][0, 0]
    for k, g, dlt, nm, nv in zip(SMALL, *[_unpack(r, small_shapes) for r in res]):
        vals = [g, dlt, nm, nv]
        if k in CHANNEL_SHARDED:
            vals = [lax.dynamic_slice_in_dim(t, me * ch, ch, axis=2) for t in vals]
        out[k] = [t.reshape(w[k].shape) for t in vals]

    return (loss, grad_x, *[out[k][0] for k in WEIGHT_NAMES], *[out[k][1] for k in WEIGHT_NAMES],
            *[out[k][2] for k in WEIGHT_NAMES], *[out[k][3] for k in WEIGHT_NAMES])


def kernel(x, ffn1_pre_g, ffn1_w_gu, ffn1_w_down, ffn1_post_g, mix_pre_g, w_in, lru_conv_w, lru_conv_b, lru_w_a, lru_b_a, lru_w_x, lru_b_x, lru_lambda, attn_sinks, conv_w, conv_b, conv_ln_g, conv_ln_b, group_g, w_out, mix_post_g, ffn2_pre_g, ffn2_w_gu, ffn2_w_down, ffn2_post_g, loss_target, m_ffn1_pre_g, m_ffn1_w_gu, m_ffn1_w_down, m_ffn1_post_g, m_mix_pre_g, m_w_in, m_lru_conv_w, m_lru_conv_b, m_lru_w_a, m_lru_b_a, m_lru_w_x, m_lru_b_x, m_lru_lambda, m_attn_sinks, m_conv_w, m_conv_b, m_conv_ln_g, m_conv_ln_b, m_group_g, m_w_out, m_mix_post_g, m_ffn2_pre_g, m_ffn2_w_gu, m_ffn2_w_down, m_ffn2_post_g, v_ffn1_pre_g, v_ffn1_w_gu, v_ffn1_w_down, v_ffn1_post_g, v_mix_pre_g, v_w_in, v_lru_conv_w, v_lru_conv_b, v_lru_w_a, v_lru_b_a, v_lru_w_x, v_lru_b_x, v_lru_lambda, v_attn_sinks, v_conv_w, v_conv_b, v_conv_ln_g, v_conv_ln_b, v_group_g, v_w_out, v_mix_post_g, v_ffn2_pre_g, v_ffn2_w_gu, v_ffn2_w_down, v_ffn2_post_g):
    args = (ffn1_pre_g, ffn1_w_gu, ffn1_w_down, ffn1_post_g, mix_pre_g, w_in, lru_conv_w, lru_conv_b, lru_w_a, lru_b_a, lru_w_x, lru_b_x, lru_lambda, attn_sinks, conv_w, conv_b, conv_ln_g, conv_ln_b, group_g, w_out, mix_post_g, ffn2_pre_g, ffn2_w_gu, ffn2_w_down, ffn2_post_g)
    ms = (m_ffn1_pre_g, m_ffn1_w_gu, m_ffn1_w_down, m_ffn1_post_g, m_mix_pre_g, m_w_in, m_lru_conv_w, m_lru_conv_b, m_lru_w_a, m_lru_b_a, m_lru_w_x, m_lru_b_x, m_lru_lambda, m_attn_sinks, m_conv_w, m_conv_b, m_conv_ln_g, m_conv_ln_b, m_group_g, m_w_out, m_mix_post_g, m_ffn2_pre_g, m_ffn2_w_gu, m_ffn2_w_down, m_ffn2_post_g)
    vs = (v_ffn1_pre_g, v_ffn1_w_gu, v_ffn1_w_down, v_ffn1_post_g, v_mix_pre_g, v_w_in, v_lru_conv_w, v_lru_conv_b, v_lru_w_a, v_lru_b_a, v_lru_w_x, v_lru_b_x, v_lru_lambda, v_attn_sinks, v_conv_w, v_conv_b, v_conv_ln_g, v_conv_ln_b, v_group_g, v_w_out, v_mix_post_g, v_ffn2_pre_g, v_ffn2_w_gu, v_ffn2_w_down, v_ffn2_post_g)
    return _step(x, loss_target, dict(zip(WEIGHT_NAMES, args)), dict(zip(WEIGHT_NAMES, ms)), dict(zip(WEIGHT_NAMES, vs)))
```

```python
import functools
import math
import operator

import jax
import jax.numpy as jnp
from jax import lax
from jax.experimental import pallas as pl
from jax.experimental.pallas import tpu as pltpu

F32 = jnp.float32
BF16 = jnp.bfloat16
N_DEV = 8
AXES = ("x", "y", "c")
MESH = pl.DeviceIdType.MESH

NORM_EPS = 1e-6
LN_EPS = 1e-5
NEG_BIG = -1e30
W_A = 256
W_B = 512
W_C = 256
HEAD_DIM = 64
N_Q_HEADS = 8
N_KV_HEADS = 2
Q_PER_KV = N_Q_HEADS // N_KV_HEADS
KV_W = N_KV_HEADS * HEAD_DIM
BLK = 128
LRU_K = 4
LRU_C = 8.0
A_BLOCKS = 4
CC_K = 31
CC_HALO = 32
LRU_HALO = 8
D_IN_PROJ = 2 * W_A + W_B + 2 * KV_W + 2 * W_C
ADAM_LR = 0.001
ADAM_B1 = 0.9
ADAM_B2 = 0.999
ADAM_EPS = 1e-08
ADAM_WD = 0.01
ADAM_STEP = 10
VMEM_LIMIT = 56 * 1024 * 1024

SDS = jax.ShapeDtypeStruct
ANY = pl.BlockSpec(memory_space=pl.ANY)


def _time_tile(s):
    return max(BLK, s // 8)


class _Exchange:
    def __init__(self, inputs, out_shapes, aliases, sem_shapes, start, wait):
        self.inputs, self.out_shapes, self.aliases, self.sem_shapes = inputs, out_shapes, aliases, sem_shapes
        self.start, self.wait = start, wait


def _my_pos():
    x, y, c = (lax.axis_index(a) for a in AXES)
    return x, y, c, 4 * x + 2 * y + c


def _flip(k):
    x, y, c, _ = _my_pos()
    return (1 - x if k & 4 else x, 1 - y if k & 2 else y, 1 - c if k & 1 else c)


def _slot(dev):
    return 4 * dev[0] + 2 * dev[1] + dev[2]


def _dev(p):
    return (p >> 2, (p >> 1) & 1, p & 1)


def _gather_a(items, two_level):
    rels = (1, 2, 4, 6) if two_level else tuple(range(1, N_DEV))
    n = len(items)
    src_of = lambda ins, a: ins[a] if items[a][1] is None else ins[a].at[items[a][1]]

    def shape_of(a):
        arr, l = items[a]
        return arr.shape if l is None else arr.shape[1:]

    def copies(ins, outs, sems, a):
        send, recv, _ = sems
        me = _my_pos()[3]
        return [(k, pltpu.make_async_remote_copy(
            src_ref=src_of(ins, a), dst_ref=outs[a].at[me], send_sem=send.at[a, k], recv_sem=recv.at[a, k],
            device_id=_flip(k), device_id_type=MESH)) for k in rels]

    def local(ins, outs, sems, a):
        return pltpu.make_async_copy(src_of(ins, a), outs[a].at[_my_pos()[3]], sems[2].at[a])

    def start(ins, outs, sems):
        for a in range(n):
            local(ins, outs, sems, a).start()
            for _, cp in copies(ins, outs, sems, a):
                cp.start()

    def wait(ins, outs, sems):
        send, recv, _ = sems
        for a in range(n):
            for k, cp in copies(ins, outs, sems, a):
                pltpu.make_async_remote_copy(
                    src_ref=src_of(ins, a), dst_ref=outs[a].at[_slot(_flip(k))], send_sem=send.at[a, k],
                    recv_sem=recv.at[a, k], device_id=_flip(k), device_id_type=MESH).wait_recv()
                cp.wait_send()
            local(ins, outs, sems, a).wait()

    return _Exchange([it[0] for it in items], [SDS((N_DEV,) + shape_of(a), items[a][0].dtype) for a in range(n)], {},
                     [pltpu.SemaphoreType.DMA((n, N_DEV)), pltpu.SemaphoreType.DMA((n, N_DEV)),
                      pltpu.SemaphoreType.DMA((n,))], start, wait)


def _gather_b(bufs):
    n = len(bufs)

    def copies(ins, outs, sems, a, c_of_block):
        send, recv = sems
        x, y, c, _ = _my_pos()
        res = []
        for k in (2, 4, 6):
            chip = _flip(k)
            blk = _slot((chip[0], chip[1], c if c_of_block == "mine" else 1 - c))
            res.append(pltpu.make_async_remote_copy(
                src_ref=ins[a].at[blk], dst_ref=outs[a].at[blk], send_sem=send.at[a, k], recv_sem=recv.at[a, k],
                device_id=_flip(1), device_id_type=MESH))
        return res

    def start(ins, outs, sems):
        for a in range(n):
            for cp in copies(ins, outs, sems, a, "mine"):
                cp.start()

    def wait(ins, outs, sems):
        for a in range(n):
            for cp in copies(ins, outs, sems, a, "sibling"):
                cp.wait_recv()
            for cp in copies(ins, outs, sems, a, "mine"):
                cp.wait_send()

    return _Exchange(list(bufs), [SDS(b.shape, b.dtype) for b in bufs], {a: a for a in range(n)},
                     [pltpu.SemaphoreType.DMA((n, N_DEV)), pltpu.SemaphoreType.DMA((n, N_DEV))], start, wait)


def _grad_x(items, n_l):
    n = len(items)
    inputs, first_in, recv_in, aliases, out_shapes = [], [], [], {}, []
    for a, (arrs, l, recv) in enumerate(items):
        first_in.append(len(inputs))
        inputs += list(arrs)
        assert sum(arr.shape[0] for arr in arrs) == N_DEV
        if recv is not None:
            aliases[len(inputs)] = a
            inputs.append(recv)
        out_shapes.append(SDS((N_DEV, n_l) + arrs[0].shape[1:], arrs[0].dtype))

    def slab(ins, a, p):
        off = 0
        for j, arr in enumerate(items[a][0]):
            if p < off + arr.shape[0]:
                return ins[first_in[a] + j].at[p - off]
            off += arr.shape[0]
        raise AssertionError

    def rdma(ins, outs, sems, a, p, src_dev):
        send, recv, _ = sems
        return pltpu.make_async_remote_copy(
            src_ref=slab(ins, a, p), dst_ref=outs[a].at[src_dev, items[a][1]], send_sem=send.at[a, p],
            recv_sem=recv.at[a, src_dev], device_id=_dev(p), device_id_type=MESH)

    def local(ins, outs, sems, a, p):
        return pltpu.make_async_copy(slab(ins, a, p), outs[a].at[p, items[a][1]], sems[2].at[a])

    def start(ins, outs, sems):
        me = _my_pos()[3]
        for k in range(1, N_DEV):
            for p in range(N_DEV):
                @pl.when((me ^ k) == p)
                def _():
                    for a in range(n):
                        rdma(ins, outs, sems, a, p, me).start()

        for p in range(N_DEV):
            @pl.when(me == p)
            def _():
                for a in range(n):
                    local(ins, outs, sems, a, p).start()

    def wait(ins, outs, sems):
        me = _my_pos()[3]
        for p in range(N_DEV):
            @pl.when(me != p)
            def _():
                for a in range(n):
                    rdma(ins, outs, sems, a, p, p).wait_recv()
                    rdma(ins, outs, sems, a, p, p).wait_send()

            @pl.when(me == p)
            def _():
                for a in range(n):
                    local(ins, outs, sems, a, p).wait()

    return _Exchange(inputs, out_shapes, aliases,
                     [pltpu.SemaphoreType.DMA((n, N_DEV)), pltpu.SemaphoreType.DMA((n, N_DEV)),
                      pltpu.SemaphoreType.DMA((n,))], start, wait)


def _pcall(body, args, *, name, grid, in_specs, out_specs, out_shape, scratch_shapes=(), carry=(), body_aliases=None):
    n_in, n_out, n_scr = len(in_specs), len(out_specs), len(scratch_shapes)
    c_in = [len(e.inputs) for e in carry]
    c_out = [len(e.out_shapes) for e in carry]
    c_sem = [len(e.sem_shapes) for e in carry]
    aliases = dict(body_aliases or {})
    for j, e in enumerate(carry):
        for i_loc, o_loc in e.aliases.items():
            aliases[n_in + sum(c_in[:j]) + i_loc] = n_out + sum(c_out[:j]) + o_loc

    def wrapped(*refs):
        def take(counts, pos):
            groups = []
            for cnt in counts:
                groups.append(refs[pos:pos + cnt])
                pos += cnt
            return groups, pos

        (ins,), pos = take([n_in], 0)
        cins, pos = take(c_in, pos)
        (outs,), pos = take([n_out], pos)
        couts, pos = take(c_out, pos)
        (scr,), pos = take([n_scr], pos)
        csems, pos = take(c_sem, pos)
        if carry:
            ids = [pl.program_id(k) for k in range(len(grid))]
            first = functools.reduce(operator.and_, [i == 0 for i in ids])
            last = functools.reduce(operator.and_, [i == g - 1 for i, g in zip(ids, grid)])

            @pl.when(first)
            def _():
                for e, ci, co, cs in zip(carry, cins, couts, csems):
                    e.start(ci, co, cs)

        body(*ins, *outs, *scr)
        if carry:
            @pl.when(last)
            def _():
                for e, ci, co, cs in zip(carry, cins, couts, csems):
                    e.wait(ci, co, cs)

    res = pl.pallas_call(
        wrapped, name=name, grid=grid,
        in_specs=list(in_specs) + [ANY] * sum(c_in),
        out_specs=list(out_specs) + [ANY] * sum(c_out),
        out_shape=list(out_shape) + [s for e in carry for s in e.out_shapes],
        scratch_shapes=list(scratch_shapes) + [s for e in carry for s in e.sem_shapes],
        input_output_aliases=aliases,
        compiler_params=pltpu.CompilerParams(dimension_semantics=("arbitrary",) * len(grid),
                                             vmem_limit_bytes=VMEM_LIMIT),
    )(*args, *[a for e in carry for a in e.inputs])
    outs, pos, extra = list(res[:n_out]), n_out, []
    for cnt in c_out:
        extra.append(list(res[pos:pos + cnt]))
        pos += cnt
    return outs, extra


def _all_gather(items, name):
    n = len(items)
    shape_of = lambda a: items[a][0].shape if items[a][1] is None else items[a][0].shape[1:]

    def body(*refs):
        ins, outs, (send_sems, recv_sems, local_sems) = refs[:n], refs[n:2 * n], refs[2 * n:]
        x, y, c, me = _my_pos()
        src_of = lambda a: ins[a] if items[a][1] is None else ins[a].at[items[a][1]]

        def copy(a, k, block, to, src=None):
            dst = outs[a].at[_slot(block)]
            return pltpu.make_async_remote_copy(
                src_ref=dst if src is None else src, dst_ref=dst,
                send_sem=send_sems.at[a, k], recv_sem=recv_sems.at[a, k], device_id=to, device_id_type=MESH)

        mine = [pltpu.make_async_copy(src_of(a), outs[a].at[me], local_sems.at[a]) for a in range(n)]
        for cp in mine:
            cp.start()
        first = [copy(a, k, (x, y, c), _flip(k), src=src_of(a)) for a in range(n) for k in (1, 2, 4, 6)]
        for cp in first:
            cp.start()
        passed = []
        for k in (2, 4, 6):
            for a in range(n):
                copy(a, k, _flip(k), (x, y, c)).wait_recv()
                fwd = copy(a, k + 1, _flip(k), _flip(1))
                fwd.start()
                passed.append(fwd)
        for a in range(n):
            copy(a, 1, _flip(1), (x, y, c)).wait_recv()
            for k in (2, 4, 6):
                copy(a, k + 1, _flip(k + 1), (x, y, c)).wait_recv()
        for cp in first + passed:
            cp.wait_send()
        for cp in mine:
            cp.wait()

    return pl.pallas_call(
        body, name=name,
        in_specs=[ANY] * n, out_specs=[ANY] * n,
        out_shape=[SDS((N_DEV,) + shape_of(a), items[a][0].dtype) for a in range(n)],
        scratch_shapes=[pltpu.SemaphoreType.DMA((n, N_DEV)), pltpu.SemaphoreType.DMA((n, N_DEV)),
                        pltpu.SemaphoreType.DMA((n,))],
    )(*[it[0] for it in items])


def _mm(a, b):
    return jnp.dot(a.astype(BF16), b.astype(BF16), preferred_element_type=F32)


def _mm_nt(a, b):
    return lax.dot_general(a.astype(BF16), b.astype(BF16), (((1,), (1,)), ((), ())), preferred_element_type=F32)


def _mm_tn(a, b):
    return lax.dot_general(a.astype(BF16), b.astype(BF16), (((0,), (0,)), ((), ())), preferred_element_type=F32)


def _rms_r(x):
    return lax.rsqrt(jnp.mean(x * x, axis=-1, keepdims=True) + NORM_EPS)


def _rms_bwd(x, r, g, dy):
    gy = dy * g
    dx = r * (gy - x * (r * r) * jnp.mean(gy * x, axis=-1, keepdims=True))
    dg = jnp.sum(dy * x * r, axis=0, keepdims=True)
    return dx, dg


def _sigmoid(x):
    return 1.0 / (1.0 + jnp.exp(-x))


def _dsilu(z, sz):
    return sz * (1.0 + z * (1.0 - sz))


def _swiglu_bf16(g, u):
    sg = 0.5 * jnp.tanh(0.5 * g) + 0.5
    silu = g * sg
    return silu * u, silu, sg + silu * (1.0 - sg)


_GELU_C = math.sqrt(2.0 / math.pi)


def _gelu(x):
    t = jnp.tanh(_GELU_C * (x + 0.044715 * x * x * x))
    return 0.5 * x * (1.0 + t), t


def _dgelu(x, t):
    return 0.5 * (1.0 + t) + 0.5 * x * (1.0 - t * t) * _GELU_C * (1.0 + 3.0 * 0.044715 * x * x)


def _log1p(e):
    return jnp.where(e < 1e-2, e * (1.0 - e * (0.5 - e * (1.0 / 3.0))), jnp.log(1.0 + e))


def _softplus(x):
    return jnp.maximum(x, 0.0) + _log1p(jnp.exp(-jnp.abs(x)))


def _neg_expm1(x):
    small = -x * (1.0 + x * (0.5 + x * (1.0 / 6.0) * (1.0 + x * 0.25)))
    return jnp.where(x > -1e-2, small, 1.0 - jnp.exp(x))


def _shift_down(x, s):
    return x if s == 0 else pltpu.roll(x, s, 0)


def _shift_up(x, s):
    return x if s == 0 else pltpu.roll(x, x.shape[0] - s, 0)


def _ffn_wspecs(d, fc, order):
    f_of = (lambda i, f: f) if order == "tf" else (lambda f, i: f)
    n_f = N_DEV // 2
    return [pl.BlockSpec((None, fc, d), lambda *g: (f_of(*g), 0, 0)),
            pl.BlockSpec((None, fc, d), lambda *g: (f_of(*g) + n_f, 0, 0)),
            pl.BlockSpec((2, fc // 2, d), lambda *g: (f_of(*g), 0, 0))]


def _ffn_fwd(x, pre_g, post_g, wgu_t, wd, name, carry=()):
    s, d = x.shape
    fc = wgu_t.shape[1]
    ts = 2 * _time_tile(s)
    n_t, n_f = s // ts, N_DEV // 2

    def body(x_ref, pg_ref, qg_ref, wg_ref, wu_ref, wd_ref, xo_ref, h_ref, g_ref, u_ref, d_ref, h_scr, acc):
        f = pl.program_id(1)

        @pl.when(f == 0)
        def _():
            xv = x_ref[...]
            hv = (xv * _rms_r(xv) * pg_ref[...]).astype(BF16)
            h_scr[...] = hv
            h_ref[...] = hv
            acc[...] = jnp.zeros_like(acc)

        hv = h_scr[...]
        g = _mm_nt(hv, wg_ref[...])
        u = _mm_nt(hv, wu_ref[...])
        g = g.astype(BF16)
        u = u.astype(BF16)
        g_ref[...] = g
        u_ref[...] = u
        acc[...] += jnp.dot(_swiglu_bf16(g, u)[0], wd_ref[...].reshape(fc, d), preferred_element_type=F32)

        @pl.when(f == n_f - 1)
        def _():
            dv = acc[...]
            d_ref[...] = dv.astype(BF16)
            xo_ref[...] = x_ref[...] + 0.5 * (dv * _rms_r(dv) * qg_ref[...])

    row = pl.BlockSpec((ts, d), lambda i, f: (i, 0))
    vec = pl.BlockSpec((1, d), lambda i, f: (0, 0))
    act = pl.BlockSpec((None, ts, fc), lambda i, f: (f, i, 0))
    return _pcall(
        body, (x, pre_g, post_g, wgu_t, wgu_t, wd), name=name, grid=(n_t, n_f),
        in_specs=[row, vec, vec] + _ffn_wspecs(d, fc, "tf"),
        out_specs=[row, row, act, act, row],
        out_shape=[SDS((s, d), F32), SDS((s, d), BF16), SDS((n_f, s, fc), BF16), SDS((n_f, s, fc), BF16),
                   SDS((s, d), BF16)],
        scratch_shapes=[pltpu.VMEM((ts, d), BF16), pltpu.VMEM((ts, d), F32)], carry=carry)


def _ffn_bwd_act(dxo, dmid, x, pre_g, post_g, g_s, u_s, wgu_t, wd, name, carry=()):
    s, d = x.shape
    fc = wgu_t.shape[1]
    ts = _time_tile(s)
    n_t, n_f = s // ts, N_DEV // 2

    def body(dxo_ref, dm_ref, x_ref, pg_ref, qg_ref, g_ref, u_ref, wg_ref, wu_ref, wd_ref,
             dx_ref, dd_ref, dg_ref, du_ref, dpg_ref, dqg_ref, dd_scr, dh_acc):
        i, f = pl.program_id(0), pl.program_id(1)

        @pl.when((i == 0) & (f == 0))
        def _():
            dpg_ref[...] = jnp.zeros_like(dpg_ref)
            dqg_ref[...] = jnp.zeros_like(dqg_ref)

        @pl.when(f == 0)
        def _():
            dv = dm_ref[...].astype(F32)
            ddv, dq = _rms_bwd(dv, _rms_r(dv), qg_ref[...], 0.5 * dxo_ref[...])
            dqg_ref[...] += dq
            dd_scr[...] = ddv.astype(BF16)
            dd_ref[...] = ddv.astype(BF16)
            dh_acc[...] = jnp.zeros_like(dh_acc)

        da = _mm_nt(dd_scr[...], wd_ref[...].reshape(fc, d)).astype(BF16)
        u = u_ref[...]
        _, silu, dsilu = _swiglu_bf16(g_ref[...], u)
        du = da * silu
        dg = da * u * dsilu
        dg_ref[...] = dg
        du_ref[...] = du
        dh_acc[...] += _mm(dg, wg_ref[...]) + _mm(du, wu_ref[...])

        @pl.when(f == n_f - 1)
        def _():
            xv = x_ref[...]
            dxv, dp = _rms_bwd(xv, _rms_r(xv), pg_ref[...], dh_acc[...])
            dpg_ref[...] += dp
            dx_ref[...] = dxo_ref[...] + dxv

    row = pl.BlockSpec((ts, d), lambda i, f: (i, 0))
    vec = pl.BlockSpec((1, d), lambda i, f: (0, 0))
    act = pl.BlockSpec((None, ts, fc), lambda i, f: (f, i, 0))
    return _pcall(
        body, (dxo, dmid, x, pre_g, post_g, g_s, u_s, wgu_t, wgu_t, wd), name=name, grid=(n_t, n_f),
        in_specs=[row, row, row, vec, vec, act, act] + _ffn_wspecs(d, fc, "tf"),
        out_specs=[row, row, act, act, vec, vec],
        out_shape=[SDS((s, d), F32), SDS((s, d), BF16), SDS((n_f, s, fc), BF16), SDS((n_f, s, fc), BF16),
                   SDS((1, d), F32), SDS((1, d), F32)],
        scratch_shapes=[pltpu.VMEM((ts, d), BF16), pltpu.VMEM((ts, d), F32)], carry=carry)


def _ffn_bwd_w(h, dd, g_s, u_s, dg, du, name, carry=()):
    s, d = h.shape
    n_f, _, fc = g_s.shape
    ts = _time_tile(s)
    n_t = s // ts

    def body(h_ref, dd_ref, g_ref, u_ref, dg_ref, du_ref, wg_ref, wu_ref, wd_ref, acc_g, acc_u, acc_d):
        i = pl.program_id(1)

        @pl.when(i == 0)
        def _():
            acc_g[...] = jnp.zeros_like(acc_g)
            acc_u[...] = jnp.zeros_like(acc_u)
            acc_d[...] = jnp.zeros_like(acc_d)

        a = _swiglu_bf16(g_ref[...], u_ref[...])[0]
        hv = h_ref[...]
        acc_g[...] += _mm_tn(dg_ref[...], hv)
        acc_u[...] += _mm_tn(du_ref[...], hv)
        acc_d[...] += _mm_tn(a, dd_ref[...])

        @pl.when(i == n_t - 1)
        def _():
            wg_ref[...] = acc_g[...].astype(BF16)
            wu_ref[...] = acc_u[...].astype(BF16)
            wd_ref[...] = acc_d[...].astype(BF16)

    row = pl.BlockSpec((ts, d), lambda f, i: (i, 0))
    act = pl.BlockSpec((None, ts, fc), lambda f, i: (f, i, 0))
    out = pl.BlockSpec((None, fc, d), lambda f, i: (f, 0, 0))
    return _pcall(
        body, (h, dd, g_s, u_s, dg, du), name=name, grid=(n_f, n_t),
        in_specs=[row, row, act, act, act, act], out_specs=[out, out, out],
        out_shape=[SDS((n_f, fc, d), BF16)] * 3,
        scratch_shapes=[pltpu.VMEM((fc, d), F32)] * 3, carry=carry)


def _ffn_bwd_w_send(h, dd, g_s, u_s, dg, du, recv_gu, recv_d, layer, name, carry=()):
    s, d = h.shape
    n_f, _, fc = g_s.shape
    ts = _time_tile(s)
    n_t = s // ts
    half = fc // 2

    def chunk_of(step):
        return (step + 2 * lax.axis_index("x") + lax.axis_index("y")) % n_f

    def body(h_ref, dd_ref, g_ref, u_ref, dg_ref, du_ref, _rgu_in, _rd_in, rgu_ref, rd_ref,
             acc_g, acc_u, acc_d, st_g, st_u, st_d, pair_gu, pair_d, zeros,
             send_sems, recv_sems, local_sems, pair_sems, zero_sems):
        f, i = pl.program_id(0), pl.program_id(1)
        x, y, c_me, me = _my_pos()
        sibling = me ^ 1

        @pl.when(i == 0)
        def _():
            acc_g[...] = jnp.zeros_like(acc_g)
            acc_u[...] = jnp.zeros_like(acc_u)
            acc_d[...] = jnp.zeros_like(acc_d)

        def zero_fills():
            res = []
            for n_k, k in enumerate((2, 4, 6)):
                other = _flip(k)
                slot = _slot((other[0], other[1], 1 - c_me))
                res += [pltpu.make_async_copy(zeros, rgu_ref.at[slot, layer, pl.ds(0, half)], zero_sems.at[n_k, 0]),
                        pltpu.make_async_copy(zeros, rgu_ref.at[slot, layer, pl.ds(half, half)], zero_sems.at[n_k, 1]),
                        pltpu.make_async_copy(zeros, rd_ref.at[slot, layer], zero_sems.at[n_k, 2])]
            return res

        @pl.when((f == 0) & (i == 0))
        def _():
            zeros[...] = jnp.zeros_like(zeros)
            for cp in zero_fills():
                cp.start()

        a = _swiglu_bf16(g_ref[...], u_ref[...])[0]
        hv = h_ref[...]
        acc_g[...] += _mm_tn(dg_ref[...], hv)
        acc_u[...] += _mm_tn(du_ref[...], hv)
        acc_d[...] += _mm_tn(a, dd_ref[...])

        def messages(fs):
            c = chunk_of(fs)
            lo, hi = pl.ds(0, half), pl.ds(half, half)
            return [(st_g.at[fs], pair_gu.at[fs // 2, 0], rgu_ref, 0, c, 0),
                    (st_u.at[fs], pair_gu.at[fs // 2, 1], rgu_ref, 0, c + n_f, 1),
                    (st_d.at[fs, lo], pair_d.at[fs], rd_ref, 1, 2 * c, 2),
                    (st_d.at[fs, hi], pair_d.at[fs], rd_ref, 1, 2 * c + 1, 3)]

        def roles(p):
            same_chip = (p >> 1) == (me >> 1)
            same_c = (p & 1) == c_me
            return p == me, p == sibling, (~same_chip) & same_c, (~same_chip) & (~same_c)

        def to_owner(fs, msg, src_dev):
            src, _, buf, row, p, j = msg
            return pltpu.make_async_remote_copy(
                src_ref=src, dst_ref=buf.at[src_dev, layer], send_sem=send_sems.at[fs, j],
                recv_sem=recv_sems.at[row, src_dev], device_id=_dev(p), device_id_type=MESH)

        def to_pair(fs, msg):
            src, pair, _, _, _, j = msg
            return pltpu.make_async_remote_copy(
                src_ref=src, dst_ref=pair, send_sem=send_sems.at[fs, j], recv_sem=pair_sems.at[fs, j],
                device_id=_dev(sibling), device_id_type=MESH)

        def local(fs, msg):
            src, _, buf, _, p, j = msg
            return pltpu.make_async_copy(src, buf.at[p, layer], local_sems.at[fs, j])

        for fs in range(n_f):
            @pl.when((f == fs) & (i == n_t - 1))
            def _():
                st_g[fs] = acc_g[...].astype(BF16)
                st_u[fs] = acc_u[...].astype(BF16)
                st_d[fs] = acc_d[...].astype(BF16)
                msgs = messages(fs)
                for msg in msgs:
                    mine, sib, _, hand_over = roles(msg[4])

                    @pl.when(mine)
                    def _():
                        local(fs, msg).start()

                    @pl.when(sib)
                    def _():
                        to_owner(fs, msg, me).start()

                    @pl.when(hand_over)
                    def _():
                        to_pair(fs, msg).start()
                for msg in msgs:
                    @pl.when(roles(msg[4])[2])
                    def _():
                        src, pair = msg[0], msg[1]
                        to_pair(fs, msg).wait_recv()
                        src[...] = (src[...].astype(F32) + pair[...].astype(F32)).astype(BF16)
                        to_owner(fs, msg, me).start()

        @pl.when((f == n_f - 1) & (i == n_t - 1))
        def _():
            for fs in range(n_f):
                for msg in messages(fs):
                    mine = roles(msg[4])[0]

                    @pl.when(mine)
                    def _():
                        local(fs, msg).wait()

                    @pl.when(~mine)
                    def _():
                        to_owner(fs, msg, me).wait_send()
            for k in (1, 2, 4, 6):
                src_dev = _slot(_flip(k))
                to_owner(0, messages(0)[0], src_dev).wait_recv()
                to_owner(0, messages(0)[2], src_dev).wait_recv()
            for cp in zero_fills():
                cp.wait()

    row = pl.BlockSpec((ts, d), lambda f, i: (i, 0))
    act = pl.BlockSpec((None, ts, fc), lambda f, i: (chunk_of(f), i, 0))
    return _pcall(
        body, (h, dd, g_s, u_s, dg, du, recv_gu, recv_d), name=name, grid=(n_f, n_t),
        in_specs=[row, row, act, act, act, act, ANY, ANY], out_specs=[ANY, ANY],
        out_shape=[SDS(recv_gu.shape, recv_gu.dtype), SDS(recv_d.shape, recv_d.dtype)],
        scratch_shapes=[pltpu.VMEM((fc, d), F32)] * 3 + [pltpu.VMEM((n_f, fc, d), BF16)] * 3
        + [pltpu.VMEM((n_f // 2, 2, fc, d), BF16), pltpu.VMEM((n_f, half, d), BF16), pltpu.VMEM((half, d), BF16)]
        + [pltpu.SemaphoreType.DMA((n_f, 4)), pltpu.SemaphoreType.DMA((2, N_DEV)), pltpu.SemaphoreType.DMA((n_f, 4)),
           pltpu.SemaphoreType.DMA((n_f, 4)), pltpu.SemaphoreType.DMA((3, 3))],
        carry=carry, body_aliases={6: 0, 7: 1})


_PROJ_WIDTHS = (W_A, W_A, W_B, KV_W, KV_W, 2 * W_C)


def _mix_in_fwd(x, pre_g, w_in_t, name, carry=()):
    s, d = x.shape
    ts = _time_tile(s)

    def body(x_ref, pg_ref, w_ref, hn_ref, *outs):
        xv = x_ref[...]
        hn = (xv * _rms_r(xv) * pg_ref[...]).astype(BF16)
        hn_ref[...] = hn
        proj = _mm_nt(hn, w_ref[...])
        off = 0
        for o_ref, w in zip(outs, _PROJ_WIDTHS):
            o_ref[...] = proj[:, off:off + w]
            off += w

    row = lambda w: pl.BlockSpec((ts, w), lambda i: (i, 0))
    return _pcall(
        body, (x, pre_g, w_in_t), name=name, grid=(s // ts,),
        in_specs=[row(d), pl.BlockSpec((1, d), lambda i: (0, 0)), pl.BlockSpec((D_IN_PROJ, d), lambda i: (0, 0))],
        out_specs=[row(d)] + [row(w) for w in _PROJ_WIDTHS],
        out_shape=[SDS((s, d), BF16)] + [SDS((s, w), F32) for w in _PROJ_WIDTHS], carry=carry)


def _mix_in_bwd(dres, x, pre_g, hn, w_in_t, dlx, dlg, dq, dk, dk_up, dv, dv_up, dglu, name, carry=()):
    s, d = x.shape
    ts = _time_tile(s)
    n_t = s // ts

    def body(dres_ref, x_ref, pg_ref, hn_ref, w_ref, dlx_ref, dlg_ref, dq_ref, dk_ref, dkn_ref,
             dv_ref, dvn_ref, dglu_ref, dx_ref, dw_ref, dpg_ref, acc):
        i = pl.program_id(0)

        @pl.when(i == 0)
        def _():
            acc[...] = jnp.zeros_like(acc)
            dpg_ref[...] = jnp.zeros_like(dpg_ref)

        def with_next(cur_ref, nxt_ref):
            nxt = jnp.where(i < n_t - 1, nxt_ref[...], 0.0)
            if ts == BLK:
                return cur_ref[...] + nxt
            return jnp.concatenate([cur_ref[:ts - BLK, :], cur_ref[ts - BLK:, :] + nxt], axis=0)

        dproj = jnp.concatenate([dlx_ref[...], dlg_ref[...], dq_ref[...], with_next(dk_ref, dkn_ref),
                                 with_next(dv_ref, dvn_ref), dglu_ref[...]], axis=1).astype(BF16)
        dhn = _mm(dproj, w_ref[...])
        acc[...] += _mm_tn(dproj, hn_ref[...])
        xv = x_ref[...]
        dxv, dp = _rms_bwd(xv, _rms_r(xv), pg_ref[...], dhn)
        dpg_ref[...] += dp
        dx_ref[...] = dres_ref[...] + dxv

        @pl.when(i == n_t - 1)
        def _():
            dw_ref[...] = acc[...].astype(BF16)

    row = lambda w: pl.BlockSpec((ts, w), lambda i: (i, 0))
    nxt = pl.BlockSpec((BLK, KV_W), lambda i: (jnp.minimum(i + 1, n_t - 1), 0))
    vec = pl.BlockSpec((1, d), lambda i: (0, 0))
    full = pl.BlockSpec((D_IN_PROJ, d), lambda i: (0, 0))
    return _pcall(
        body, (dres, x, pre_g, hn, w_in_t, dlx, dlg, dq, dk, dk_up, dv, dv_up, dglu), name=name, grid=(n_t,),
        in_specs=[row(d), row(d), vec, row(d), full, row(W_A), row(W_A), row(W_B), row(KV_W), nxt,
                  row(KV_W), nxt, row(2 * W_C)],
        out_specs=[row(d), full, vec],
        out_shape=[SDS((s, d), F32), SDS((D_IN_PROJ, d), BF16), SDS((1, d), F32)],
        scratch_shapes=[pltpu.VMEM((D_IN_PROJ, d), F32)], carry=carry)


def _lru_gates(xc, lru_p):
    cw_ref, cb_ref, wa_ref, ba_ref, wx_ref, bx_ref, lam_ref = lru_p
    c = cb_ref[...]
    for j in range(LRU_K):
        c = c + cw_ref[j:j + 1, :] * _shift_down(xc, LRU_K - 1 - j)[LRU_HALO:, :]
    r = _sigmoid(_mm(c, wa_ref[...]) + ba_ref[...])
    ig = _sigmoid(_mm(c, wx_ref[...]) + bx_ref[...])
    sp = _softplus(-lam_ref[...])
    log_a = -LRU_C * r * sp
    a = jnp.exp(log_a)
    m = jnp.sqrt(_neg_expm1(2.0 * log_a))
    return c, r, ig, sp, a, m


def _lru_pspecs():
    small = lambda r: pl.BlockSpec((r, W_A), lambda i: (0, 0))
    return [small(LRU_K), small(1), small(W_A), small(1), small(W_A), small(1), small(1)]


def _lru_fwd(lx, lg, lru_p, name, carry=()):
    s = lx.shape[0]
    ts = _time_tile(s)
    n8 = ts // LRU_HALO

    def body(lx_ref, lxp_ref, lg_ref, *rest):
        lru_p, (ya_ref, h_ref, hcarry) = rest[:7], rest[7:]
        i = pl.program_id(0)
        prev = jnp.where(i > 0, lxp_ref[...], 0.0)
        xc = jnp.concatenate([prev, lx_ref[...]], axis=0)
        c, r, ig, sp, a, m = _lru_gates(xc, lru_p)
        acc_a, acc_b = a, m * (ig * c)
        t = lax.broadcasted_iota(jnp.int32, a.shape, 0)
        k = 1
        while k < ts:
            keep = t >= k
            acc_b = jnp.where(keep, acc_a * _shift_down(acc_b, k) + acc_b, acc_b)
            acc_a = jnp.where(keep, acc_a * _shift_down(acc_a, k), acc_a)
            k *= 2
        h0 = jnp.where(i > 0, hcarry[...], 0.0)
        h = acc_b + acc_a * h0
        hcarry[...] = h[ts - 1:ts, :]
        h_ref[...] = h
        ya_ref[...] = _gelu(lg_ref[...])[0] * h

    row = pl.BlockSpec((ts, W_A), lambda i: (i, 0))
    prev8 = pl.BlockSpec((LRU_HALO, W_A), lambda i: (jnp.maximum(i * n8 - 1, 0), 0))
    return _pcall(
        body, (lx, lx, lg, *lru_p), name=name, grid=(s // ts,),
        in_specs=[row, prev8, row] + _lru_pspecs(), out_specs=[row, row],
        out_shape=[SDS((s, W_A), F32), SDS((s, W_A), F32)],
        scratch_shapes=[pltpu.VMEM((1, W_A), F32)], carry=carry)


def _lru_bwd(dya, lx, lg, h_s, lru_p, name, carry=()):
    s = lx.shape[0]
    ts = _time_tile(s)
    n_t = s // ts
    n8 = ts // LRU_HALO

    def body(dya_ref, lx_ref, lxp_ref, lg_ref, h_ref, hp_ref, *rest):
        lru_p = rest[:7]
        (dlx_ref, dlg_ref, dcw_ref, dcb_ref, dwa_ref, dba_ref, dwx_ref, dbx_ref, dlam_ref,
         carry_a, carry_l, carry_dc) = rest[7:]
        cw_ref, _, wa_ref, _, wx_ref, _, lam_ref = lru_p
        i = pl.program_id(0)
        first_tile = i == n_t - 1
        last_tile = i == 0

        @pl.when(i == 0)
        def _():
            for ref in (dcw_ref, dcb_ref, dwa_ref, dba_ref, dwx_ref, dbx_ref, dlam_ref):
                ref[...] = jnp.zeros_like(ref)

        prev = jnp.where(first_tile, 0.0, lxp_ref[...])
        xc = jnp.concatenate([prev, lx_ref[...]], axis=0)
        c, r, ig, sp, a, m = _lru_gates(xc, lru_p)
        h = h_ref[...]
        hcat = jnp.concatenate([jnp.where(first_tile, 0.0, hp_ref[...]), h], axis=0)
        h_m1 = _shift_down(hcat, 1)[LRU_HALO:, :]
        lg = lg_ref[...]
        ge, th = _gelu(lg)
        dya = dya_ref[...]
        dlg_ref[...] = dya * h * _dgelu(lg, th)
        dh = dya * ge
        t = lax.broadcasted_iota(jnp.int32, a.shape, 0)
        a_next = jnp.where(t < ts - 1, _shift_up(a, 1), jnp.where(last_tile, 0.0, carry_a[...]))
        acc_a, acc_b = a_next, dh
        k = 1
        while k < ts:
            keep = t < ts - k
            acc_b = jnp.where(keep, acc_a * _shift_up(acc_b, k) + acc_b, acc_b)
            acc_a = jnp.where(keep, acc_a * _shift_up(acc_a, k), acc_a)
            k *= 2
        lam_beyond = jnp.where(last_tile, 0.0, carry_l[...])
        lmb = acc_b + acc_a * lam_beyond
        carry_a[...] = a[0:1, :]
        carry_l[...] = lmb[0:1, :]
        gi = ig * c
        dgi = lmb * m
        dla = lmb * h_m1 * a - (lmb * gi) * (a * a) / m
        dr = dla * (-LRU_C * sp)
        dsp = jnp.sum(dla * (-LRU_C * r), axis=0, keepdims=True)
        dlam_ref[...] += -dsp * _sigmoid(-lam_ref[...])
        dra = dr * r * (1.0 - r)
        dia = dgi * c * ig * (1.0 - ig)
        dc = dgi * ig + _mm_nt(dra, wa_ref[...]) + _mm_nt(dia, wx_ref[...])
        dwa_ref[...] += _mm_tn(c, dra)
        dwx_ref[...] += _mm_tn(c, dia)
        dba_ref[...] += jnp.sum(dra, axis=0, keepdims=True)
        dbx_ref[...] += jnp.sum(dia, axis=0, keepdims=True)
        dcb_ref[...] += jnp.sum(dc, axis=0, keepdims=True)
        dcc = jnp.concatenate([dc, jnp.where(last_tile, 0.0, carry_dc[...])], axis=0)
        carry_dc[...] = dc[0:LRU_HALO, :]
        dlx = jnp.zeros_like(dc)
        for j in range(LRU_K):
            sh = LRU_K - 1 - j
            dcw_ref[j:j + 1, :] += jnp.sum(dc * _shift_down(xc, sh)[LRU_HALO:, :], axis=0, keepdims=True)
            dlx = dlx + cw_ref[j:j + 1, :] * _shift_up(dcc, sh)[:ts, :]
        dlx_ref[...] = dlx

    row = pl.BlockSpec((ts, W_A), lambda i: (n_t - 1 - i, 0))
    prev8 = pl.BlockSpec((LRU_HALO, W_A), lambda i: (jnp.maximum((n_t - 1 - i) * n8 - 1, 0), 0))
    small = lambda r: pl.BlockSpec((r, W_A), lambda i: (0, 0))
    return _pcall(
        body, (dya, lx, lx, lg, h_s, h_s, *lru_p), name=name, grid=(n_t,),
        in_specs=[row, row, prev8, row, row, prev8] + _lru_pspecs(),
        out_specs=[row, row, small(LRU_K), small(1), small(W_A), small(1), small(W_A), small(1), small(1)],
        out_shape=[SDS((s, W_A), F32), SDS((s, W_A), F32), SDS((LRU_K, W_A), F32), SDS((1, W_A), F32),
                   SDS((W_A, W_A), F32), SDS((1, W_A), F32), SDS((W_A, W_A), F32), SDS((1, W_A), F32),
                   SDS((1, W_A), F32)],
        scratch_shapes=[pltpu.VMEM((1, W_A), F32), pltpu.VMEM((1, W_A), F32), pltpu.VMEM((LRU_HALO, W_A), F32)],
        carry=carry)


_ATT_ROWS = N_Q_HEADS * BLK
_GRP_ROWS = Q_PER_KV * BLK


def _attn_stack(ref, rows, g):
    return jnp.concatenate([ref[rows, h * HEAD_DIM:(h + 1) * HEAD_DIM]
                            for h in range(g * Q_PER_KV, (g + 1) * Q_PER_KV)], axis=0)


def _attn_unstack(parts):
    return jnp.concatenate([p[j * BLK:(j + 1) * BLK, :] for p in parts for j in range(Q_PER_KV)], axis=1)


def _grp(x, g):
    return x[:, g * _GRP_ROWS:(g + 1) * _GRP_ROWS]


def _attn_block(q_ref, k_ref, kp_ref, v_ref, vp_ref, sink_row, i, b):
    rows, prev = slice(b * BLK, (b + 1) * BLK), slice((b - 1) * BLK, b * BLK)
    qs, kcs, kps, vcs, vps = [], [], [], [], []
    for g in range(N_KV_HEADS):
        cols = slice(g * HEAD_DIM, (g + 1) * HEAD_DIM)
        qs.append(_attn_stack(q_ref, rows, g))
        kcs.append(k_ref[rows, cols])
        vcs.append(v_ref[rows, cols])
        kps.append(kp_ref[:, cols] if b == 0 else k_ref[prev, cols])
        vps.append(vp_ref[:, cols] if b == 0 else v_ref[prev, cols])
    scale = 1.0 / math.sqrt(HEAD_DIM)
    sc = jnp.concatenate([_mm_nt(kcs[g], qs[g]) for g in range(N_KV_HEADS)], axis=1) * scale
    sp = jnp.concatenate([_mm_nt(kps[g], qs[g]) for g in range(N_KV_HEADS)], axis=1) * scale
    kj = lax.broadcasted_iota(jnp.int32, (BLK, _ATT_ROWS), 0)
    qi = lax.broadcasted_iota(jnp.int32, (BLK, _ATT_ROWS), 1) & (BLK - 1)
    sc = jnp.where(kj <= qi, sc, NEG_BIG)
    sp = jnp.where((kj > qi) if b > 0 else ((kj > qi) & (i > 0)), sp, NEG_BIG)
    m = jnp.maximum(jnp.maximum(jnp.max(sc, axis=0, keepdims=True), jnp.max(sp, axis=0, keepdims=True)), sink_row)
    pc = jnp.exp(sc - m)
    pp = jnp.exp(sp - m)
    es = jnp.exp(sink_row - m)
    inv = 1.0 / (jnp.sum(pc, axis=0, keepdims=True) + jnp.sum(pp, axis=0, keepdims=True) + es)
    return qs, kcs, kps, vcs, vps, pc * inv, pp * inv, es * inv


def _attn_specs(s, ts):
    bpt = ts // BLK
    tile = lambda w: pl.BlockSpec((ts, w), lambda i: (i, 0))
    prv = pl.BlockSpec((BLK, KV_W), lambda i: (jnp.maximum(i * bpt - 1, 0), 0))
    sink = pl.BlockSpec((1, _ATT_ROWS), lambda i: (0, 0))
    return bpt, tile, prv, sink


def _attn_fwd(q, k, v, sink_row, name, carry=()):
    s = q.shape[0]
    ts = _time_tile(s)
    bpt, tile, prv, sink = _attn_specs(s, ts)

    def body(q_ref, k_ref, kp_ref, v_ref, vp_ref, sk_ref, y_ref):
        i = pl.program_id(0)
        for b in range(bpt):
            _, _, _, vcs, vps, pc, pp, _ = _attn_block(q_ref, k_ref, kp_ref, v_ref, vp_ref, sk_ref[...], i, b)
            outs = [_mm_tn(_grp(pc, g), vcs[g]) + _mm_tn(_grp(pp, g), vps[g]) for g in range(N_KV_HEADS)]
            y_ref[b * BLK:(b + 1) * BLK, :] = _attn_unstack(outs)

    return _pcall(
        body, (q, k, k, v, v, sink_row), name=name, grid=(s // ts,),
        in_specs=[tile(W_B), tile(KV_W), prv, tile(KV_W), prv, sink],
        out_specs=[tile(W_B)], out_shape=[SDS((s, W_B), F32)], carry=carry)


def _attn_bwd(dy, q, k, v, sinks, name, carry=()):
    s = q.shape[0]
    ts = _time_tile(s)
    n_t = s // ts
    bpt, tile, prv, sink = _attn_specs(s, ts)

    def body(dy_ref, q_ref, k_ref, kp_ref, v_ref, vp_ref, sk_ref, dq_ref, dk_ref, dv_ref, dku_ref, dvu_ref, dsk_ref):
        i = pl.program_id(0)

        @pl.when(i == 0)
        def _():
            dsk_ref[...] = jnp.zeros_like(dsk_ref)

        scale = 1.0 / math.sqrt(HEAD_DIM)
        groups = range(N_KV_HEADS)
        head_row = lax.broadcasted_iota(jnp.int32, (N_Q_HEADS, BLK), 0)
        dsk = jnp.zeros((N_Q_HEADS, BLK), F32)
        dk_blocks, dv_blocks = [], []
        for b in range(bpt):
            rows = slice(b * BLK, (b + 1) * BLK)
            qs, kcs, kps, vcs, vps, pc, pp, ps = _attn_block(q_ref, k_ref, kp_ref, v_ref, vp_ref, sk_ref[...], i, b)
            dos = [_attn_stack(dy_ref, rows, g) for g in groups]
            dpc = jnp.concatenate([_mm_nt(vcs[g], dos[g]) for g in groups], axis=1)
            dpp = jnp.concatenate([_mm_nt(vps[g], dos[g]) for g in groups], axis=1)
            delta = jnp.sum(pc * dpc, axis=0, keepdims=True) + jnp.sum(pp * dpp, axis=0, keepdims=True)
            dsc = pc * (dpc - delta) * scale
            dsp = pp * (dpp - delta) * scale
            dq_ref[rows, :] = _attn_unstack([_mm_tn(_grp(dsc, g), kcs[g]) + _mm_tn(_grp(dsp, g), kps[g])
                                             for g in groups])
            dk_blocks.append(jnp.concatenate([_mm(_grp(dsc, g), qs[g]) for g in groups], axis=1))
            dv_blocks.append(jnp.concatenate([_mm(_grp(pc, g), dos[g]) for g in groups], axis=1))
            dkp = jnp.concatenate([_mm(_grp(dsp, g), qs[g]) for g in groups], axis=1)
            dvp = jnp.concatenate([_mm(_grp(pp, g), dos[g]) for g in groups], axis=1)
            if b == 0:
                dku_ref[...] = dkp
                dvu_ref[...] = dvp
            else:
                dk_blocks[b - 1] = dk_blocks[b - 1] + dkp
                dv_blocks[b - 1] = dv_blocks[b - 1] + dvp
            dsink = -ps * delta
            for h in range(N_Q_HEADS):
                dsk = dsk + jnp.where(head_row == h, jnp.sum(dsink[:, h * BLK:(h + 1) * BLK], axis=1, keepdims=True), 0.0)
        for b in range(bpt):
            dk_ref[b * BLK:(b + 1) * BLK, :] = dk_blocks[b]
            dv_ref[b * BLK:(b + 1) * BLK, :] = dv_blocks[b]
        dsk_ref[...] += dsk

    up = pl.BlockSpec((BLK, KV_W), lambda i: (i, 0))
    return _pcall(
        body, (dy, q, k, k, v, v, sinks), name=name, grid=(n_t,),
        in_specs=[tile(W_B), tile(W_B), tile(KV_W), prv, tile(KV_W), prv, sink],
        out_specs=[tile(W_B), tile(KV_W), tile(KV_W), up, up, pl.BlockSpec((N_Q_HEADS, BLK), lambda i: (0, 0))],
        out_shape=[SDS((s, W_B), F32), SDS((s, KV_W), F32), SDS((s, KV_W), F32), SDS((n_t * BLK, KV_W), F32),
                   SDS((n_t * BLK, KV_W), F32), SDS((N_Q_HEADS, BLK), F32)], carry=carry)


def _cc_recompute(glu_ref, glup_ref, cw_ref, cb_ref, first_tile):
    prev = jnp.where(first_tile, 0.0, glup_ref[...])
    ge = jnp.concatenate([prev, glu_ref[...]], axis=0)
    y0 = ge[:, :W_C] * _sigmoid(ge[:, W_C:])
    y1 = cb_ref[...]
    for j in range(CC_K):
        y1 = y1 + cw_ref[j:j + 1, :] * _shift_down(y0, CC_K - 1 - j)[CC_HALO:, :]
    return y0, y1


def _ln_stats(y1):
    mu = jnp.mean(y1, axis=-1, keepdims=True)
    xc = y1 - mu
    rstd = lax.rsqrt(jnp.mean(xc * xc, axis=-1, keepdims=True) + LN_EPS)
    return xc * rstd, rstd


def _cc_specs(s, ts):
    n32 = ts // CC_HALO
    row = lambda w: pl.BlockSpec((ts, w), lambda i: (i, 0))
    prev = pl.BlockSpec((CC_HALO, 2 * W_C), lambda i: (jnp.maximum(i * n32 - 1, 0), 0))
    small = lambda r: pl.BlockSpec((r, W_C), lambda i: (0, 0))
    return row, prev, small


def _cc_fwd(glu, cw, cb, lng, lnb, name, carry=()):
    s = glu.shape[0]
    ts = _time_tile(s)
    row, prev, small = _cc_specs(s, ts)

    def body(glu_ref, glup_ref, cw_ref, cb_ref, lng_ref, lnb_ref, y_ref):
        _, y1 = _cc_recompute(glu_ref, glup_ref, cw_ref, cb_ref, pl.program_id(0) == 0)
        xhat, _ = _ln_stats(y1)
        z = xhat * lng_ref[...] + lnb_ref[...]
        y_ref[...] = z * _sigmoid(z)

    return _pcall(
        body, (glu, glu, cw, cb, lng, lnb), name=name, grid=(s // ts,),
        in_specs=[row(2 * W_C), prev, small(CC_HALO), small(1), small(1), small(1)],
        out_specs=[row(W_C)], out_shape=[SDS((s, W_C), F32)], carry=carry)


def _cc_bwd_conv(dy, glu, cw, cb, lng, lnb, name, carry=()):
    s = glu.shape[0]
    ts = _time_tile(s)
    row, prev, small = _cc_specs(s, ts)

    def body(dy_ref, glu_ref, glup_ref, cw_ref, cb_ref, lng_ref, lnb_ref, dy1_ref, dcw_ref, dcb_ref, dlng_ref, dlnb_ref):
        i = pl.program_id(0)

        @pl.when(i == 0)
        def _():
            for ref in (dcw_ref, dcb_ref, dlng_ref, dlnb_ref):
                ref[...] = jnp.zeros_like(ref)

        y0, y1 = _cc_recompute(glu_ref, glup_ref, cw_ref, cb_ref, i == 0)
        xhat, rstd = _ln_stats(y1)
        z = xhat * lng_ref[...] + lnb_ref[...]
        dz = dy_ref[...] * _dsilu(z, _sigmoid(z))
        dlng_ref[...] += jnp.sum(dz * xhat, axis=0, keepdims=True)
        dlnb_ref[...] += jnp.sum(dz, axis=0, keepdims=True)
        dxh = dz * lng_ref[...]
        dy1 = rstd * (dxh - jnp.mean(dxh, axis=-1, keepdims=True) - xhat * jnp.mean(dxh * xhat, axis=-1, keepdims=True))
        dy1_ref[...] = dy1
        dcb_ref[...] += jnp.sum(dy1, axis=0, keepdims=True)
        for j in range(CC_K):
            dcw_ref[j:j + 1, :] += jnp.sum(dy1 * _shift_down(y0, CC_K - 1 - j)[CC_HALO:, :], axis=0, keepdims=True)

    return _pcall(
        body, (dy, glu, glu, cw, cb, lng, lnb), name=name, grid=(s // ts,),
        in_specs=[row(W_C), row(2 * W_C), prev, small(CC_HALO), small(1), small(1), small(1)],
        out_specs=[row(W_C), small(CC_HALO), small(1), small(1), small(1)],
        out_shape=[SDS((s, W_C), F32), SDS((CC_HALO, W_C), F32)] + [SDS((1, W_C), F32)] * 3, carry=carry)


def _cc_bwd_glu(dy1, glu, cw, name, carry=()):
    s = glu.shape[0]
    ts = _time_tile(s)
    n_t = s // ts
    n32 = ts // CC_HALO

    def body(dy1_ref, dyn_ref, glu_ref, cw_ref, dglu_ref):
        i = pl.program_id(0)
        dcat = jnp.concatenate([dy1_ref[...], jnp.where(i < n_t - 1, dyn_ref[...], 0.0)], axis=0)
        dy0 = jnp.zeros((ts, W_C), F32)
        for j in range(CC_K):
            dy0 = dy0 + cw_ref[j:j + 1, :] * _shift_up(dcat, CC_K - 1 - j)[:ts, :]
        a = glu_ref[:, :W_C]
        sg = _sigmoid(glu_ref[:, W_C:])
        dglu_ref[...] = jnp.concatenate([dy0 * sg, dy0 * a * sg * (1.0 - sg)], axis=1)

    row = lambda w: pl.BlockSpec((ts, w), lambda i: (i, 0))
    nxt = pl.BlockSpec((CC_HALO, W_C), lambda i: (jnp.minimum((i + 1) * n32, s // CC_HALO - 1), 0))
    return _pcall(
        body, (dy1, dy1, glu, cw), name=name, grid=(n_t,),
        in_specs=[row(W_C), nxt, row(2 * W_C), pl.BlockSpec((CC_HALO, W_C), lambda i: (0, 0))],
        out_specs=[row(2 * W_C)], out_shape=[SDS((s, 2 * W_C), F32)], carry=carry)


_MIX_OFFS = ((0, W_A), (W_A, W_A + W_B), (W_A + W_B, W_A + W_B + W_C))


def _mix_out_fwd(x, ya, yb, yc, group_g, w_out, post_g, name, carry=()):
    s, d = x.shape
    ts = _time_tile(s)
    dm = w_out.shape[0]

    def body(x_ref, ya_ref, yb_ref, yc_ref, gg_ref, w_ref, qg_ref, xo_ref, o_ref):
        parts = []
        for y_ref, (lo, hi) in zip((ya_ref, yb_ref, yc_ref), _MIX_OFFS):
            yv = y_ref[...]
            parts.append(yv * _rms_r(yv) * gg_ref[:, lo:hi])
        o = _mm(jnp.concatenate(parts, axis=1), w_ref[...])
        o_ref[...] = o
        xo_ref[...] = x_ref[...] + o * _rms_r(o) * qg_ref[...]

    row = lambda w: pl.BlockSpec((ts, w), lambda i: (i, 0))
    return _pcall(
        body, (x, ya, yb, yc, group_g, w_out, post_g), name=name, grid=(s // ts,),
        in_specs=[row(d), row(W_A), row(W_B), row(W_C), pl.BlockSpec((1, dm), lambda i: (0, 0)),
                  pl.BlockSpec((dm, d), lambda i: (0, 0)), pl.BlockSpec((1, d), lambda i: (0, 0))],
        out_specs=[row(d), row(d)], out_shape=[SDS((s, d), F32), SDS((s, d), F32)], carry=carry)


def _mix_out_bwd(dxo, o, ya, yb, yc, group_g, w_out, post_g, name, carry=()):
    s, d = o.shape
    ts = _time_tile(s)
    n_t = s // ts
    dm = w_out.shape[0]

    def body(dxo_ref, o_ref, ya_ref, yb_ref, yc_ref, gg_ref, w_ref, qg_ref,
             dya_ref, dyb_ref, dyc_ref, dw_ref, dqg_ref, dgg_ref, acc):
        i = pl.program_id(0)

        @pl.when(i == 0)
        def _():
            acc[...] = jnp.zeros_like(acc)
            dqg_ref[...] = jnp.zeros_like(dqg_ref)
            dgg_ref[...] = jnp.zeros_like(dgg_ref)

        ov = o_ref[...]
        do, dq = _rms_bwd(ov, _rms_r(ov), qg_ref[...], dxo_ref[...])
        dqg_ref[...] += dq
        do = do.astype(BF16)
        dyn = _mm_nt(do, w_ref[...])
        parts, dggs = [], []
        for y_ref, dy_ref, (lo, hi) in zip((ya_ref, yb_ref, yc_ref), (dya_ref, dyb_ref, dyc_ref), _MIX_OFFS):
            yv = y_ref[...]
            r = _rms_r(yv)
            gg = gg_ref[:, lo:hi]
            parts.append(yv * r * gg)
            dyv, dg = _rms_bwd(yv, r, gg, dyn[:, lo:hi])
            dy_ref[...] = dyv
            dggs.append(dg)
        dgg_ref[...] += jnp.concatenate(dggs, axis=1)
        acc[...] += _mm_tn(jnp.concatenate(parts, axis=1), do)

        @pl.when(i == n_t - 1)
        def _():
            dw_ref[...] = acc[...].astype(BF16)

    row = lambda w: pl.BlockSpec((ts, w), lambda i: (i, 0))
    full = pl.BlockSpec((dm, d), lambda i: (0, 0))
    return _pcall(
        body, (dxo, o, ya, yb, yc, group_g, w_out, post_g), name=name, grid=(n_t,),
        in_specs=[row(d), row(d), row(W_A), row(W_B), row(W_C), pl.BlockSpec((1, dm), lambda i: (0, 0)), full,
                  pl.BlockSpec((1, d), lambda i: (0, 0))],
        out_specs=[row(W_A), row(W_B), row(W_C), full, pl.BlockSpec((1, d), lambda i: (0, 0)),
                   pl.BlockSpec((1, dm), lambda i: (0, 0))],
        out_shape=[SDS((s, W_A), F32), SDS((s, W_B), F32), SDS((s, W_C), F32), SDS((dm, d), BF16),
                   SDS((1, d), F32), SDS((1, dm), F32)],
        scratch_shapes=[pltpu.VMEM((dm, d), F32)], carry=carry)


def _loss_head(y, target, name):
    s, d = y.shape
    ts = _time_tile(s)

    def body(y_ref, t_ref, loss_ref, dy_ref):
        @pl.when(pl.program_id(0) == 0)
        def _():
            loss_ref[...] = jnp.zeros_like(loss_ref)

        err = y_ref[...] - t_ref[...]
        dy_ref[...] = err * (1.0 / d)
        per_tok = jnp.mean(err * err, axis=-1, keepdims=True)
        loss_ref[...] += 0.5 * jnp.sum(per_tok, axis=0, keepdims=True)

    row = pl.BlockSpec((ts, d), lambda i: (i, 0))
    return _pcall(body, (y, target), name=name, grid=(s // ts,), in_specs=[row, row],
                  out_specs=[pl.BlockSpec((1, BLK), lambda i: (0, 0)), row],
                  out_shape=[SDS((1, BLK), F32), SDS((s, d), F32)])[0]


def _adamw_math(w, g, m, v):
    m = ADAM_B1 * m + (1.0 - ADAM_B1) * g
    v = ADAM_B2 * v + (1.0 - ADAM_B2) * (g * g)
    m_hat = m / (1.0 - ADAM_B1 ** ADAM_STEP)
    v_hat = v / (1.0 - ADAM_B2 ** ADAM_STEP)
    delta = -ADAM_LR * (m_hat / (jnp.sqrt(v_hat) + ADAM_EPS) + ADAM_WD * w)
    return delta, m, v


def _row_tile(rows, cap=256):
    best = None
    for t in range(16, min(rows, cap) + 1, 16):
        if rows % t == 0:
            best = t
    return best if best is not None else rows


def _reduce_adamw(recv, w, m, v, name):
    n_l, r, c = w.shape
    tr = _row_tile(r)

    def body(recv_ref, w_ref, m_ref, v_ref, g_ref, d_ref, nm_ref, nv_ref):
        g = recv_ref[0].astype(F32)
        for p in range(1, N_DEV):
            g = g + recv_ref[p].astype(F32)
        g_ref[...] = g
        d_ref[...], nm_ref[...], nv_ref[...] = _adamw_math(w_ref[...], g, m_ref[...], v_ref[...])

    blk = pl.BlockSpec((None, tr, c), lambda l, i: (l, i, 0))
    return _pcall(
        body, (recv, w, m, v), name=name, grid=(n_l, r // tr),
        in_specs=[pl.BlockSpec((N_DEV, None, tr, c), lambda l, i: (0, l, i, 0)), blk, blk, blk],
        out_specs=[blk] * 4, out_shape=[SDS(w.shape, F32)] * 4)[0]


def _reduce_adamw_small(parts, w, m, v, name):
    def body(p_ref, w_ref, m_ref, v_ref, g_ref, d_ref, nm_ref, nv_ref):
        g = p_ref[0]
        for p in range(1, N_DEV):
            g = g + p_ref[p]
        g_ref[...] = g
        d_ref[...], nm_ref[...], nv_ref[...] = _adamw_math(w_ref[...], g, m_ref[...], v_ref[...])

    vm = pl.BlockSpec(memory_space=pltpu.VMEM)
    return pl.pallas_call(body, name=name, in_specs=[vm] * 4, out_specs=[vm] * 4, out_shape=[SDS(w.shape, F32)] * 4,
                          compiler_params=pltpu.CompilerParams(vmem_limit_bytes=VMEM_LIMIT))(parts, w, m, v)


def _rows_of(shape):
    return -(-math.prod(shape) // (8 * BLK)) * 8


def _pack(arrs):
    rows = []
    for a in arrs:
        n, r = math.prod(a.shape), _rows_of(a.shape)
        if n % BLK == 0:
            part = a.reshape(n // BLK, BLK)
            rows.append(part if n // BLK == r else jnp.pad(part, ((0, r - n // BLK), (0, 0))))
        else:
            rows.append(jnp.pad(a.reshape(-1), (0, r * BLK - n)).reshape(r, BLK))
    return jnp.concatenate(rows, axis=0)


def _unpack(packed, shapes):
    out, row = [], 0
    for shp in shapes:
        n, r = math.prod(shp), _rows_of(shp)
        if n % BLK == 0:
            out.append(packed[row:row + n // BLK].reshape(shp))
        else:
            out.append(packed[row:row + r].reshape(-1)[:n].reshape(shp))
        row += r
    return out


def _block_diag(w):
    nb, bw, _ = w.shape
    eye = jnp.eye(nb, dtype=w.dtype)
    return (eye[:, None, :, None] * w[:, :, None, :]).reshape(nb * bw, nb * bw)


def _diag_blocks(wd, nb):
    bw = wd.shape[0] // nb
    return jnp.stack([wd[b * bw:(b + 1) * bw, b * bw:(b + 1) * bw] for b in range(nb)])


WEIGHT_NAMES = ['ffn1_pre_g', 'ffn1_w_gu', 'ffn1_w_down', 'ffn1_post_g', 'mix_pre_g', 'w_in', 'lru_conv_w', 'lru_conv_b',
                'lru_w_a', 'lru_b_a', 'lru_w_x', 'lru_b_x', 'lru_lambda', 'attn_sinks', 'conv_w', 'conv_b', 'conv_ln_g',
                'conv_ln_b', 'group_g', 'w_out', 'mix_post_g', 'ffn2_pre_g', 'ffn2_w_gu', 'ffn2_w_down', 'ffn2_post_g']
BIG = ('ffn1_w_gu', 'ffn1_w_down', 'w_in', 'w_out', 'ffn2_w_gu', 'ffn2_w_down')
TRANSPOSED = ('ffn1_w_gu', 'ffn2_w_gu', 'w_in')
SMALL = tuple(k for k in WEIGHT_NAMES if k not in BIG)
CHANNEL_SHARDED = ('lru_conv_w', 'conv_w')


def _step(x, target, w, m, v):
    n_l = w['ffn1_pre_g'].shape[0]
    assert n_l == 2, "the exchange schedule below is laid out for two layers"
    s, d = x.shape[1], x.shape[2]
    x = x.reshape(s, d)
    target = target.reshape(s, d)
    me = _my_pos()[3]
    tview = lambda t, k: jnp.swapaxes(t[k], 1, 2) if k in TRANSPOSED else t[k]
    wb = {k: tview(w, k).astype(BF16) for k in BIG}
    vec = lambda name, l: w[name][l][None, :]

    conv_shard = _pack([w['lru_conv_w'], w['conv_w']])
    g0 = _all_gather([(wb['ffn1_w_gu'], 0), (wb['ffn1_w_down'], 0), (wb['w_in'], 0), (wb['w_out'], 0),
                      (conv_shard, None)], "all_gather_first")
    wts = [dict(), dict()]
    wts[0]['ffn1_w_gu'], wts[0]['ffn1_w_down'], wts[0]['w_in'], wts[0]['w_out'], conv_g = g0
    ch = W_A // N_DEV
    conv_parts = [_unpack(conv_g[p], [(n_l, LRU_K, ch), (n_l, CC_K, ch)]) for p in range(N_DEV)]
    lru_cw = jnp.concatenate([cp[0] for cp in conv_parts], axis=-1)
    cc_cw = jnp.concatenate([cp[1] for cp in conv_parts], axis=-1)
    cc_cw = jnp.pad(cc_cw, ((0, 0), (0, CC_HALO - CC_K), (0, 0)))

    gather_plan = {
        ('ffn1', 0): [('A', 'g2_0', ('ffn2_w_gu',), 0), ('A', 'g1_1', ('ffn1_w_gu',), 1)],
        ('mix_in', 0): [('B', 'g2_0'), ('B', 'g1_1')],
        ('attn', 0): [('A', 'd2_0', ('ffn2_w_down',), 0)],
        ('cconv', 0): [('B', 'd2_0')],
        ('ffn2', 0): [('D', None, ('ffn1_w_down',), 1), ('A', 'wi_1', ('w_in',), 1), ('A', 'wo_1', ('w_out',), 1)],
        ('ffn1', 1): [('A', 'f2_1', ('ffn2_w_gu', 'ffn2_w_down'), 1), ('B', 'wi_1'), ('B', 'wo_1')],
        ('mix_in', 1): [('B', 'f2_1')],
    }
    pend = {}

    def fwd(kernel_name, l, fn, *args):
        plan = gather_plan.get((kernel_name, l), [])
        carry = []
        for st in plan:
            if st[0] == 'B':
                carry.append(_gather_b(pend[st[1]][2]))
            else:
                carry.append(_gather_a([(wb[k], st[3]) for k in st[2]], two_level=st[0] == 'A'))
        outs, ex = fn(*args, f"{kernel_name}_fwd_l{l}", carry)
        for st, bufs in zip(plan, ex):
            if st[0] == 'A':
                pend[st[1]] = (st[2], st[3], bufs)
            else:
                names, wl = (st[2], st[3]) if st[0] == 'D' else pend.pop(st[1])[:2]
                for k, b in zip(names, bufs):
                    wts[wl][k] = b
        return outs

    saved = []
    h = x
    for l in range(n_l):
        sv = {'x0': h}
        lw = wts[l]
        x1, sv['h1'], sv['g1'], sv['u1'], sv['d1'] = fwd(
            'ffn1', l, _ffn_fwd, h, vec('ffn1_pre_g', l), vec('ffn1_post_g', l), lw['ffn1_w_gu'], lw['ffn1_w_down'])
        sv['x1'] = x1
        sv['hn'], lx, lg, q, k, vv, glu = fwd('mix_in', l, _mix_in_fwd, x1, vec('mix_pre_g', l),
                                              lw['w_in'].reshape(D_IN_PROJ, d))
        sv.update(lx=lx, lg=lg, q=q, k=k, v=vv, glu=glu)
        lru_p = (lru_cw[l], vec('lru_conv_b', l), _block_diag(w['lru_w_a'][l]).astype(BF16), vec('lru_b_a', l),
                 _block_diag(w['lru_w_x'][l]).astype(BF16), vec('lru_b_x', l), vec('lru_lambda', l))
        cc_p = (cc_cw[l], vec('conv_b', l), vec('conv_ln_g', l), vec('conv_ln_b', l))
        sv.update(lru_p=lru_p, cc_p=cc_p)
        sv['ya'], sv['hs'] = fwd('lru', l, _lru_fwd, lx, lg, lru_p)
        sv['sink_row'] = jnp.repeat(w['attn_sinks'][l], BLK)[None, :]
        (sv['yb'],) = fwd('attn', l, _attn_fwd, q, k, vv, sv['sink_row'])
        (sv['yc'],) = fwd('cconv', l, _cc_fwd, glu, *cc_p)
        x2, sv['o'] = fwd('mix_out', l, _mix_out_fwd, x1, sv['ya'], sv['yb'], sv['yc'], vec('group_g', l),
                          lw['w_out'].reshape(-1, d), vec('mix_post_g', l))
        sv['x2'] = x2
        h, sv['h2'], sv['g2'], sv['u2'], sv['d2'] = fwd(
            'ffn2', l, _ffn_fwd, x2, vec('ffn2_pre_g', l), vec('ffn2_post_g', l), lw['ffn2_w_gu'], lw['ffn2_w_down'])
        saved.append(sv)

    loss_row, dh = _loss_head(h, target, "loss_head")

    recv = {k: None for k in BIG}
    ready = {}
    small = [dict() for _ in range(n_l)]

    def exchange(keys):
        return _grad_x([(ready.pop(key), key[1], recv[key[0]]) for key in keys], n_l)

    def received(keys, bufs):
        for key, b in zip(keys, bufs):
            recv[key[0]] = b

    def run(fn, *args, keys=()):
        outs, ex = fn(*args, carry=[exchange(keys)] if keys else [])
        if keys:
            received(keys, ex[0])
        return outs

    for l in reversed(range(n_l)):
        sv, sg, lw = saved[l], small[l], wts[l]
        keys = [] if l == n_l - 1 else [('ffn1_w_gu', l + 1)]
        dx2, dd, dg, du, sg['ffn2_pre_g'], sg['ffn2_post_g'] = run(
            _ffn_bwd_act, dh, sv['d2'], sv['x2'], vec('ffn2_pre_g', l), vec('ffn2_post_g', l), sv['g2'], sv['u2'],
            lw['ffn2_w_gu'], lw['ffn2_w_down'], f"ffn2_bwd_act_l{l}", keys=keys)
        keys = [] if l == n_l - 1 else [('ffn1_w_down', l + 1)]
        dwg, dwu, dwd = run(_ffn_bwd_w, sv['h2'], dd, sv['g2'], sv['u2'], dg, du, f"ffn2_bwd_w_l{l}", keys=keys)
        ready[('ffn2_w_gu', l)] = [dwg, dwu]
        ready[('ffn2_w_down', l)] = [dwd.reshape(N_DEV, -1, d)]
        dya, dyb, dyc, dw_out, sg['mix_post_g'], sg['group_g'] = run(
            _mix_out_bwd, dx2, sv['o'], sv['ya'], sv['yb'], sv['yc'], vec('group_g', l), lw['w_out'].reshape(-1, d),
            vec('mix_post_g', l), f"mix_out_bwd_l{l}")
        ready[('w_out', l)] = [dw_out.reshape(N_DEV, -1, d)]
        (dlx, dlg, sg['lru_conv_w'], sg['lru_conv_b'], dwa, sg['lru_b_a'], dwx, sg['lru_b_x'],
         sg['lru_lambda']) = run(_lru_bwd, dya, sv['lx'], sv['lg'], sv['hs'], sv['lru_p'], f"lru_bwd_l{l}")
        sg['lru_w_a'] = _diag_blocks(dwa, A_BLOCKS)
        sg['lru_w_x'] = _diag_blocks(dwx, A_BLOCKS)
        dq, dk, dv, dk_up, dv_up, dsk = run(_attn_bwd, dyb, sv['q'], sv['k'], sv['v'], sv['sink_row'],
                                            f"attn_bwd_l{l}", keys=[('w_out', l)] if l == 0 else [])
        sg['attn_sinks'] = dsk[:, 0]
        dy1, dcw, sg['conv_b'], sg['conv_ln_g'], sg['conv_ln_b'] = run(
            _cc_bwd_conv, dyc, sv['glu'], *sv['cc_p'], f"cconv_bwd_conv_l{l}")
        sg['conv_w'] = dcw[:CC_K]
        (dglu,) = run(_cc_bwd_glu, dy1, sv['glu'], sv['cc_p'][0], f"cconv_bwd_glu_l{l}")
        dx1, dw_in, sg['mix_pre_g'] = run(
            _mix_in_bwd, dx2, sv['x1'], vec('mix_pre_g', l), sv['hn'], lw['w_in'].reshape(D_IN_PROJ, d),
            dlx, dlg, dq, dk, dk_up, dv, dv_up, dglu, f"mix_in_bwd_l{l}",
            keys=[('ffn2_w_down', l)] if l == 0 else [('w_out', l)])
        ready[('w_in', l)] = [dw_in.reshape(N_DEV, -1, d)]
        dh, dd, dg, du, sg['ffn1_pre_g'], sg['ffn1_post_g'] = run(
            _ffn_bwd_act, dx1, sv['d1'], sv['x0'], vec('ffn1_pre_g', l), vec('ffn1_post_g', l), sv['g1'], sv['u1'],
            lw['ffn1_w_gu'], lw['ffn1_w_down'], f"ffn1_bwd_act_l{l}", keys=[('ffn2_w_gu', l), ('w_in', l)])
        if l > 0:
            dwg, dwu, dwd = run(_ffn_bwd_w, sv['h1'], dd, sv['g1'], sv['u1'], dg, du, f"ffn1_bwd_w_l{l}",
                                keys=[('ffn2_w_down', l)])
            ready[('ffn1_w_gu', l)] = [dwg, dwu]
            ready[('ffn1_w_down', l)] = [dwd.reshape(N_DEV, -1, d)]
        else:
            part = _pack([jnp.stack([small[j][k] for j in range(n_l)]) for k in SMALL] + [loss_row])
            (recv['ffn1_w_gu'], recv['ffn1_w_down']), ex = _ffn_bwd_w_send(
                sv['h1'], dd, sv['g1'], sv['u1'], dg, du, recv['ffn1_w_gu'], recv['ffn1_w_down'], 0, "ffn1_bwd_w_send_l0",
                [_gather_a([(part, None)], two_level=False)])
            small_parts = ex[0][0]
    grad_x = dh.reshape(1, s, d)

    out = {}
    for k in BIG:
        res = _reduce_adamw(recv[k], tview(w, k), tview(m, k), tview(v, k), f"reduce_adamw_{k}")
        out[k] = [jnp.swapaxes(r, 1, 2) for r in res] if k in TRANSPOSED else res

    small_shapes = [(n_l,) + tuple(small[0][k].shape) for k in SMALL]

    def widen(t, k):
        if k not in CHANNEL_SHARDED:
            return t.reshape((n_l,) + tuple(small[0][k].shape))
        full = jnp.zeros((n_l,) + tuple(small[0][k].shape), F32)
        return lax.dynamic_update_slice_in_dim(full, t, me * ch, axis=2)

    no_w = jnp.zeros(loss_row.shape, F32)
    packed = [_pack([widen(src[k], k) for k in SMALL] + [no_w]) for src in (w, m, v)]
    res = _reduce_adamw_small(small_parts, *packed, "reduce_adamw_small")
    loss = _unpack(res[0], small_shapes + [loss_row.shape])[-1][0, 0]
    for k, g, dlt, nm, nv in zip(SMALL, *[_unpack(r, small_shapes) for r in res]):
        vals = [g, dlt, nm, nv]
        if k in CHANNEL_SHARDED:
            vals = [lax.dynamic_slice_in_dim(t, me * ch, ch, axis=2) for t in vals]
        out[k] = [t.reshape(w[k].shape) for t in vals]

    return (loss, grad_x, *[out[k][0] for k in WEIGHT_NAMES], *[out[k][1] for k in WEIGHT_NAMES],
            *[out[k][2] for k in WEIGHT_NAMES], *[out[k][3] for k in WEIGHT_NAMES])


def kernel(x, ffn1_pre_g, ffn1_w_gu, ffn1_w_down, ffn1_post_g, mix_pre_g, w_in, lru_conv_w, lru_conv_b, lru_w_a, lru_b_a, lru_w_x, lru_b_x, lru_lambda, attn_sinks, conv_w, conv_b, conv_ln_g, conv_ln_b, group_g, w_out, mix_post_g, ffn2_pre_g, ffn2_w_gu, ffn2_w_down, ffn2_post_g, loss_target, m_ffn1_pre_g, m_ffn1_w_gu, m_ffn1_w_down, m_ffn1_post_g, m_mix_pre_g, m_w_in, m_lru_conv_w, m_lru_conv_b, m_lru_w_a, m_lru_b_a, m_lru_w_x, m_lru_b_x, m_lru_lambda, m_attn_sinks, m_conv_w, m_conv_b, m_conv_ln_g, m_conv_ln_b, m_group_g, m_w_out, m_mix_post_g, m_ffn2_pre_g, m_ffn2_w_gu, m_ffn2_w_down, m_ffn2_post_g, v_ffn1_pre_g, v_ffn1_w_gu, v_ffn1_w_down, v_ffn1_post_g, v_mix_pre_g, v_w_in, v_lru_conv_w, v_lru_conv_b, v_lru_w_a, v_lru_b_a, v_lru_w_x, v_lru_b_x, v_lru_lambda, v_attn_sinks, v_conv_w, v_conv_b, v_conv_ln_g, v_conv_ln_b, v_group_g, v_w_out, v_mix_post_g, v_ffn2_pre_g, v_ffn2_w_gu, v_ffn2_w_down, v_ffn2_post_g):
    args = (ffn1_pre_g, ffn1_w_gu, ffn1_w_down, ffn1_post_g, mix_pre_g, w_in, lru_conv_w, lru_conv_b, lru_w_a, lru_b_a, lru_w_x, lru_b_x, lru_lambda, attn_sinks, conv_w, conv_b, conv_ln_g, conv_ln_b, group_g, w_out, mix_post_g, ffn2_pre_g, ffn2_w_gu, ffn2_w_down, ffn2_post_g)
    ms = (m_ffn1_pre_g, m_ffn1_w_gu, m_ffn1_w_down, m_ffn1_post_g, m_mix_pre_g, m_w_in, m_lru_conv_w, m_lru_conv_b, m_lru_w_a, m_lru_b_a, m_lru_w_x, m_lru_b_x, m_lru_lambda, m_attn_sinks, m_conv_w, m_conv_b, m_conv_ln_g, m_conv_ln_b, m_group_g, m_w_out, m_mix_post_g, m_ffn2_pre_g, m_ffn2_w_gu, m_ffn2_w_down, m_ffn2_post_g)
    vs = (v_ffn1_pre_g, v_ffn1_w_gu, v_ffn1_w_down, v_ffn1_post_g, v_mix_pre_g, v_w_in, v_lru_conv_w, v_lru_conv_b, v_lru_w_a, v_lru_b_a, v_lru_w_x, v_lru_b_x, v_lru_lambda, v_attn_sinks, v_conv_w, v_conv_b, v_conv_ln_g, v_conv_ln_b, v_group_g, v_w_out, v_mix_post_g, v_ffn2_pre_g, v_ffn2_w_gu, v_ffn2_w_down, v_ffn2_post_g)
    return _step(x, loss_target, dict(zip(WEIGHT_NAMES, args)), dict(zip(WEIGHT_NAMES, ms)), dict(zip(WEIGHT_NAMES, vs)))
```

```python
import functools
import math
import operator

import jax
import jax.numpy as jnp
from jax import lax
from jax.experimental import pallas as pl
from jax.experimental.pallas import tpu as pltpu

F32 = jnp.float32
BF16 = jnp.bfloat16
N_DEV = 8
AXES = ("x", "y", "c")
MESH = pl.DeviceIdType.MESH

NORM_EPS = 1e-6
LN_EPS = 1e-5
NEG_BIG = -1e30
W_A = 256
W_B = 512
W_C = 256
HEAD_DIM = 64
N_Q_HEADS = 8
N_KV_HEADS = 2
Q_PER_KV = N_Q_HEADS // N_KV_HEADS
KV_W = N_KV_HEADS * HEAD_DIM
BLK = 128
LRU_K = 4
LRU_C = 8.0
A_BLOCKS = 4
CC_K = 31
CC_HALO = 32
LRU_HALO = 8
D_IN_PROJ = 2 * W_A + W_B + 2 * KV_W + 2 * W_C
ADAM_LR = 0.001
ADAM_B1 = 0.9
ADAM_B2 = 0.999
ADAM_EPS = 1e-08
ADAM_WD = 0.01
ADAM_STEP = 10
VMEM_LIMIT = 56 * 1024 * 1024

SDS = jax.ShapeDtypeStruct
ANY = pl.BlockSpec(memory_space=pl.ANY)


def _time_tile(s):
    return max(BLK, s // 8)


class _Exchange:
    def __init__(self, inputs, out_shapes, aliases, sem_shapes, start, wait):
        self.inputs, self.out_shapes, self.aliases, self.sem_shapes = inputs, out_shapes, aliases, sem_shapes
        self.start, self.wait = start, wait


def _my_pos():
    x, y, c = (lax.axis_index(a) for a in AXES)
    return x, y, c, 4 * x + 2 * y + c


def _flip(k):
    x, y, c, _ = _my_pos()
    return (1 - x if k & 4 else x, 1 - y if k & 2 else y, 1 - c if k & 1 else c)


def _slot(dev):
    return 4 * dev[0] + 2 * dev[1] + dev[2]


def _dev(p):
    return (p >> 2, (p >> 1) & 1, p & 1)


def _gather_a(items, two_level):
    rels = (1, 2, 4, 6) if two_level else tuple(range(1, N_DEV))
    items = [tuple(it) + (None,) * (4 - len(it)) for it in items]
    n = len(items)
    with_buf = [a for a in range(n) if items[a][3] is not None]

    def rows_of(ref, a):
        return ref if items[a][2] is None else ref.at[pl.ds(*items[a][2])]

    def src_of(ins, a):
        return rows_of(ins[a] if items[a][1] is None else ins[a].at[items[a][1]], a)

    def dst_of(outs, a, slot):
        return rows_of(outs[a].at[slot], a)

    def shape_of(a):
        arr, l = items[a][:2]
        return arr.shape if l is None else arr.shape[1:]

    def copies(ins, outs, sems, a):
        send, recv, _ = sems
        me = _my_pos()[3]
        return [(k, pltpu.make_async_remote_copy(
            src_ref=src_of(ins, a), dst_ref=dst_of(outs, a, me), send_sem=send.at[a, k], recv_sem=recv.at[a, k],
            device_id=_flip(k), device_id_type=MESH)) for k in rels]

    def local(ins, outs, sems, a):
        return pltpu.make_async_copy(src_of(ins, a), dst_of(outs, a, _my_pos()[3]), sems[2].at[a])

    def start(ins, outs, sems):
        for a in range(n):
            local(ins, outs, sems, a).start()
            for _, cp in copies(ins, outs, sems, a):
                cp.start()

    def wait(ins, outs, sems):
        send, recv, _ = sems
        for a in range(n):
            for k, cp in copies(ins, outs, sems, a):
                pltpu.make_async_remote_copy(
                    src_ref=src_of(ins, a), dst_ref=dst_of(outs, a, _slot(_flip(k))), send_sem=send.at[a, k],
                    recv_sem=recv.at[a, k], device_id=_flip(k), device_id_type=MESH).wait_recv()
                cp.wait_send()
            local(ins, outs, sems, a).wait()

    return _Exchange([it[0] for it in items] + [items[a][3] for a in with_buf],
                     [SDS((N_DEV,) + shape_of(a), items[a][0].dtype) for a in range(n)],
                     {n + j: a for j, a in enumerate(with_buf)},
                     [pltpu.SemaphoreType.DMA((n, N_DEV)), pltpu.SemaphoreType.DMA((n, N_DEV)),
                      pltpu.SemaphoreType.DMA((n,))], start, wait)


def _gather_b(bufs):
    n = len(bufs)

    def copies(ins, outs, sems, a, c_of_block):
        send, recv = sems
        x, y, c, _ = _my_pos()
        res = []
        for k in (2, 4, 6):
            chip = _flip(k)
            blk = _slot((chip[0], chip[1], c if c_of_block == "mine" else 1 - c))
            res.append(pltpu.make_async_remote_copy(
                src_ref=ins[a].at[blk], dst_ref=outs[a].at[blk], send_sem=send.at[a, k], recv_sem=recv.at[a, k],
                device_id=_flip(1), device_id_type=MESH))
        return res

    def start(ins, outs, sems):
        for a in range(n):
            for cp in copies(ins, outs, sems, a, "mine"):
                cp.start()

    def wait(ins, outs, sems):
        for a in range(n):
            for cp in copies(ins, outs, sems, a, "sibling"):
                cp.wait_recv()
            for cp in copies(ins, outs, sems, a, "mine"):
                cp.wait_send()

    return _Exchange(list(bufs), [SDS(b.shape, b.dtype) for b in bufs], {a: a for a in range(n)},
                     [pltpu.SemaphoreType.DMA((n, N_DEV)), pltpu.SemaphoreType.DMA((n, N_DEV))], start, wait)


def _grad_x(items, n_l):
    items = [tuple(it) + (None,) * (4 - len(it)) for it in items]
    n = len(items)
    owners = [tuple(range(N_DEV)) if it[3] is None else tuple(it[3]) for it in items]
    inputs, first_in, aliases, out_shapes = [], [], {}, []
    for a, (arrs, l, recv, _) in enumerate(items):
        first_in.append(len(inputs))
        inputs += list(arrs)
        assert sum(arr.shape[0] for arr in arrs) == N_DEV
        if recv is not None:
            aliases[len(inputs)] = a
            inputs.append(recv)
        out_shapes.append(SDS((N_DEV, n_l) + arrs[0].shape[1:], arrs[0].dtype))

    def slab(ins, a, p):
        off = 0
        for j, arr in enumerate(items[a][0]):
            if p < off + arr.shape[0]:
                return ins[first_in[a] + j].at[p - off]
            off += arr.shape[0]
        raise AssertionError

    def rdma(ins, outs, sems, a, p, src_dev):
        send, recv, _ = sems
        return pltpu.make_async_remote_copy(
            src_ref=slab(ins, a, p), dst_ref=outs[a].at[src_dev, items[a][1]], send_sem=send.at[a, p],
            recv_sem=recv.at[a, src_dev], device_id=_dev(p), device_id_type=MESH)

    def local(ins, outs, sems, a, p):
        return pltpu.make_async_copy(slab(ins, a, p), outs[a].at[p, items[a][1]], sems[2].at[a])

    def start(ins, outs, sems):
        me = _my_pos()[3]
        for p in range(N_DEV):
            mine = [a for a in range(n) if p in owners[a]]

            @pl.when(me != p)
            def _():
                for a in mine:
                    rdma(ins, outs, sems, a, p, me).start()

            @pl.when(me == p)
            def _():
                for a in mine:
                    local(ins, outs, sems, a, p).start()

    def wait(ins, outs, sems):
        me = _my_pos()[3]
        for a in range(n):
            i_own = functools.reduce(operator.or_, [me == q for q in owners[a]])
            for p in range(N_DEV):
                @pl.when((me != p) & i_own)
                def _():
                    rdma(ins, outs, sems, a, p, p).wait_recv()

                if p in owners[a]:
                    @pl.when(me != p)
                    def _():
                        rdma(ins, outs, sems, a, p, p).wait_send()

                    @pl.when(me == p)
                    def _():
                        local(ins, outs, sems, a, p).wait()

    return _Exchange(inputs, out_shapes, aliases,
                     [pltpu.SemaphoreType.DMA((n, N_DEV)), pltpu.SemaphoreType.DMA((n, N_DEV)),
                      pltpu.SemaphoreType.DMA((n,))], start, wait)


def _pcall(body, args, *, name, grid, in_specs, out_specs, out_shape, scratch_shapes=(), carry=(), body_aliases=None):
    n_in, n_out, n_scr = len(in_specs), len(out_specs), len(scratch_shapes)
    c_in = [len(e.inputs) for e in carry]
    c_out = [len(e.out_shapes) for e in carry]
    c_sem = [len(e.sem_shapes) for e in carry]
    aliases = dict(body_aliases or {})
    for j, e in enumerate(carry):
        for i_loc, o_loc in e.aliases.items():
            aliases[n_in + sum(c_in[:j]) + i_loc] = n_out + sum(c_out[:j]) + o_loc

    def wrapped(*refs):
        def take(counts, pos):
            groups = []
            for cnt in counts:
                groups.append(refs[pos:pos + cnt])
                pos += cnt
            return groups, pos

        (ins,), pos = take([n_in], 0)
        cins, pos = take(c_in, pos)
        (outs,), pos = take([n_out], pos)
        couts, pos = take(c_out, pos)
        (scr,), pos = take([n_scr], pos)
        csems, pos = take(c_sem, pos)
        if carry:
            ids = [pl.program_id(k) for k in range(len(grid))]
            first = functools.reduce(operator.and_, [i == 0 for i in ids])
            last = functools.reduce(operator.and_, [i == g - 1 for i, g in zip(ids, grid)])

            @pl.when(first)
            def _():
                for e, ci, co, cs in zip(carry, cins, couts, csems):
                    e.start(ci, co, cs)

        body(*ins, *outs, *scr)
        if carry:
            @pl.when(last)
            def _():
                for e, ci, co, cs in zip(carry, cins, couts, csems):
                    e.wait(ci, co, cs)

    res = pl.pallas_call(
        wrapped, name=name, grid=grid,
        in_specs=list(in_specs) + [ANY] * sum(c_in),
        out_specs=list(out_specs) + [ANY] * sum(c_out),
        out_shape=list(out_shape) + [s for e in carry for s in e.out_shapes],
        scratch_shapes=list(scratch_shapes) + [s for e in carry for s in e.sem_shapes],
        input_output_aliases=aliases,
        compiler_params=pltpu.CompilerParams(dimension_semantics=("arbitrary",) * len(grid),
                                             vmem_limit_bytes=VMEM_LIMIT),
    )(*args, *[a for e in carry for a in e.inputs])
    outs, pos, extra = list(res[:n_out]), n_out, []
    for cnt in c_out:
        extra.append(list(res[pos:pos + cnt]))
        pos += cnt
    return outs, extra


def _all_gather(items, name):
    n = len(items)
    shape_of = lambda a: items[a][0].shape if items[a][1] is None else items[a][0].shape[1:]

    def body(*refs):
        ins, outs, (send_sems, recv_sems, local_sems) = refs[:n], refs[n:2 * n], refs[2 * n:]
        x, y, c, me = _my_pos()
        src_of = lambda a: ins[a] if items[a][1] is None else ins[a].at[items[a][1]]

        def copy(a, k, block, to, src=None):
            dst = outs[a].at[_slot(block)]
            return pltpu.make_async_remote_copy(
                src_ref=dst if src is None else src, dst_ref=dst,
                send_sem=send_sems.at[a, k], recv_sem=recv_sems.at[a, k], device_id=to, device_id_type=MESH)

        mine = [pltpu.make_async_copy(src_of(a), outs[a].at[me], local_sems.at[a]) for a in range(n)]
        for cp in mine:
            cp.start()
        first = [copy(a, k, (x, y, c), _flip(k), src=src_of(a)) for a in range(n) for k in (1, 2, 4, 6)]
        for cp in first:
            cp.start()
        passed = []
        for k in (2, 4, 6):
            for a in range(n):
                copy(a, k, _flip(k), (x, y, c)).wait_recv()
                fwd = copy(a, k + 1, _flip(k), _flip(1))
                fwd.start()
                passed.append(fwd)
        for a in range(n):
            copy(a, 1, _flip(1), (x, y, c)).wait_recv()
            for k in (2, 4, 6):
                copy(a, k + 1, _flip(k + 1), (x, y, c)).wait_recv()
        for cp in first + passed:
            cp.wait_send()
        for cp in mine:
            cp.wait()

    return pl.pallas_call(
        body, name=name,
        in_specs=[ANY] * n, out_specs=[ANY] * n,
        out_shape=[SDS((N_DEV,) + shape_of(a), items[a][0].dtype) for a in range(n)],
        scratch_shapes=[pltpu.SemaphoreType.DMA((n, N_DEV)), pltpu.SemaphoreType.DMA((n, N_DEV)),
                        pltpu.SemaphoreType.DMA((n,))],
    )(*[it[0] for it in items])


def _mm(a, b):
    return jnp.dot(a.astype(BF16), b.astype(BF16), preferred_element_type=F32)


def _mm_nt(a, b):
    return lax.dot_general(a.astype(BF16), b.astype(BF16), (((1,), (1,)), ((), ())), preferred_element_type=F32)


def _mm_tn(a, b):
    return lax.dot_general(a.astype(BF16), b.astype(BF16), (((0,), (0,)), ((), ())), preferred_element_type=F32)


def _rms_r(x):
    return lax.rsqrt(jnp.mean(x * x, axis=-1, keepdims=True) + NORM_EPS)


def _rms_bwd(x, r, g, dy):
    gy = dy * g
    dx = r * (gy - x * (r * r) * jnp.mean(gy * x, axis=-1, keepdims=True))
    dg = jnp.sum(dy * x * r, axis=0, keepdims=True)
    return dx, dg


def _sigmoid(x):
    return 1.0 / (1.0 + jnp.exp(-x))


def _dsilu(z, sz):
    return sz * (1.0 + z * (1.0 - sz))


def _swiglu_bf16(g, u):
    sg = 0.5 * jnp.tanh(0.5 * g) + 0.5
    silu = g * sg
    return silu * u, silu, sg + silu * (1.0 - sg)


_GELU_C = math.sqrt(2.0 / math.pi)


def _gelu(x):
    t = jnp.tanh(_GELU_C * (x + 0.044715 * x * x * x))
    return 0.5 * x * (1.0 + t), t


def _dgelu(x, t):
    return 0.5 * (1.0 + t) + 0.5 * x * (1.0 - t * t) * _GELU_C * (1.0 + 3.0 * 0.044715 * x * x)


def _log1p(e):
    return jnp.where(e < 1e-2, e * (1.0 - e * (0.5 - e * (1.0 / 3.0))), jnp.log(1.0 + e))


def _softplus(x):
    return jnp.maximum(x, 0.0) + _log1p(jnp.exp(-jnp.abs(x)))


def _neg_expm1(x):
    small = -x * (1.0 + x * (0.5 + x * (1.0 / 6.0) * (1.0 + x * 0.25)))
    return jnp.where(x > -1e-2, small, 1.0 - jnp.exp(x))


def _shift_down(x, s):
    return x if s == 0 else pltpu.roll(x, s, 0)


def _shift_up(x, s):
    return x if s == 0 else pltpu.roll(x, x.shape[0] - s, 0)


def _ffn_wspecs(d, fc, order):
    f_of = (lambda i, f: f) if order == "tf" else (lambda f, i: f)
    n_f = N_DEV // 2
    return [pl.BlockSpec((None, fc, d), lambda *g: (f_of(*g), 0, 0)),
            pl.BlockSpec((None, fc, d), lambda *g: (f_of(*g) + n_f, 0, 0)),
            pl.BlockSpec((2, fc // 2, d), lambda *g: (f_of(*g), 0, 0))]


def _ffn_fwd(x, pre_g, post_g, wgu_t, wd, name, carry=()):
    s, d = x.shape
    fc = wgu_t.shape[1]
    ts = 2 * _time_tile(s)
    n_t, n_f = s // ts, N_DEV // 2

    def body(x_ref, pg_ref, qg_ref, wg_ref, wu_ref, wd_ref, xo_ref, h_ref, g_ref, u_ref, d_ref, h_scr, acc):
        f = pl.program_id(1)

        @pl.when(f == 0)
        def _():
            xv = x_ref[...]
            hv = (xv * _rms_r(xv) * pg_ref[...]).astype(BF16)
            h_scr[...] = hv
            h_ref[...] = hv
            acc[...] = jnp.zeros_like(acc)

        hv = h_scr[...]
        g = _mm_nt(hv, wg_ref[...])
        u = _mm_nt(hv, wu_ref[...])
        g = g.astype(BF16)
        u = u.astype(BF16)
        g_ref[...] = g
        u_ref[...] = u
        acc[...] += jnp.dot(_swiglu_bf16(g, u)[0], wd_ref[...].reshape(fc, d), preferred_element_type=F32)

        @pl.when(f == n_f - 1)
        def _():
            dv = acc[...]
            d_ref[...] = dv.astype(BF16)
            xo_ref[...] = x_ref[...] + 0.5 * (dv * _rms_r(dv) * qg_ref[...])

    row = pl.BlockSpec((ts, d), lambda i, f: (i, 0))
    vec = pl.BlockSpec((1, d), lambda i, f: (0, 0))
    act = pl.BlockSpec((None, ts, fc), lambda i, f: (f, i, 0))
    return _pcall(
        body, (x, pre_g, post_g, wgu_t, wgu_t, wd), name=name, grid=(n_t, n_f),
        in_specs=[row, vec, vec] + _ffn_wspecs(d, fc, "tf"),
        out_specs=[row, row, act, act, row],
        out_shape=[SDS((s, d), F32), SDS((s, d), BF16), SDS((n_f, s, fc), BF16), SDS((n_f, s, fc), BF16),
                   SDS((s, d), BF16)],
        scratch_shapes=[pltpu.VMEM((ts, d), BF16), pltpu.VMEM((ts, d), F32)], carry=carry)


def _ffn_bwd_act(dxo, dmid, x, pre_g, post_g, g_s, u_s, wgu_t, wd, name, carry=()):
    s, d = x.shape
    fc = wgu_t.shape[1]
    ts = _time_tile(s)
    n_t, n_f = s // ts, N_DEV // 2

    def body(dxo_ref, dm_ref, x_ref, pg_ref, qg_ref, g_ref, u_ref, wg_ref, wu_ref, wd_ref,
             dx_ref, dd_ref, dg_ref, du_ref, dpg_ref, dqg_ref, dd_scr, dh_acc):
        i, f = pl.program_id(0), pl.program_id(1)

        @pl.when((i == 0) & (f == 0))
        def _():
            dpg_ref[...] = jnp.zeros_like(dpg_ref)
            dqg_ref[...] = jnp.zeros_like(dqg_ref)

        @pl.when(f == 0)
        def _():
            dv = dm_ref[...].astype(F32)
            ddv, dq = _rms_bwd(dv, _rms_r(dv), qg_ref[...], 0.5 * dxo_ref[...])
            dqg_ref[...] += dq
            dd_scr[...] = ddv.astype(BF16)
            dd_ref[...] = ddv.astype(BF16)
            dh_acc[...] = jnp.zeros_like(dh_acc)

        da = _mm_nt(dd_scr[...], wd_ref[...].reshape(fc, d)).astype(BF16)
        u = u_ref[...]
        _, silu, dsilu = _swiglu_bf16(g_ref[...], u)
        du = da * silu
        dg = da * u * dsilu
        dg_ref[...] = dg
        du_ref[...] = du
        dh_acc[...] += _mm(dg, wg_ref[...]) + _mm(du, wu_ref[...])

        @pl.when(f == n_f - 1)
        def _():
            xv = x_ref[...]
            dxv, dp = _rms_bwd(xv, _rms_r(xv), pg_ref[...], dh_acc[...])
            dpg_ref[...] += dp
            dx_ref[...] = dxo_ref[...] + dxv

    row = pl.BlockSpec((ts, d), lambda i, f: (i, 0))
    vec = pl.BlockSpec((1, d), lambda i, f: (0, 0))
    act = pl.BlockSpec((None, ts, fc), lambda i, f: (f, i, 0))
    return _pcall(
        body, (dxo, dmid, x, pre_g, post_g, g_s, u_s, wgu_t, wgu_t, wd), name=name, grid=(n_t, n_f),
        in_specs=[row, row, row, vec, vec, act, act] + _ffn_wspecs(d, fc, "tf"),
        out_specs=[row, row, act, act, vec, vec],
        out_shape=[SDS((s, d), F32), SDS((s, d), BF16), SDS((n_f, s, fc), BF16), SDS((n_f, s, fc), BF16),
                   SDS((1, d), F32), SDS((1, d), F32)],
        scratch_shapes=[pltpu.VMEM((ts, d), BF16), pltpu.VMEM((ts, d), F32)], carry=carry)


def _ffn_bwd_w(h, dd, g_s, u_s, dg, du, name, carry=()):
    s, d = h.shape
    n_f, _, fc = g_s.shape
    ts = _time_tile(s)
    n_t = s // ts

    def body(h_ref, dd_ref, g_ref, u_ref, dg_ref, du_ref, wg_ref, wu_ref, wd_ref, acc_g, acc_u, acc_d):
        i = pl.program_id(1)

        @pl.when(i == 0)
        def _():
            acc_g[...] = jnp.zeros_like(acc_g)
            acc_u[...] = jnp.zeros_like(acc_u)
            acc_d[...] = jnp.zeros_like(acc_d)

        a = _swiglu_bf16(g_ref[...], u_ref[...])[0]
        hv = h_ref[...]
        acc_g[...] += _mm_tn(dg_ref[...], hv)
        acc_u[...] += _mm_tn(du_ref[...], hv)
        acc_d[...] += _mm_tn(a, dd_ref[...])

        @pl.when(i == n_t - 1)
        def _():
            wg_ref[...] = acc_g[...].astype(BF16)
            wu_ref[...] = acc_u[...].astype(BF16)
            wd_ref[...] = acc_d[...].astype(BF16)

    row = pl.BlockSpec((ts, d), lambda f, i: (i, 0))
    act = pl.BlockSpec((None, ts, fc), lambda f, i: (f, i, 0))
    out = pl.BlockSpec((None, fc, d), lambda f, i: (f, 0, 0))
    return _pcall(
        body, (h, dd, g_s, u_s, dg, du), name=name, grid=(n_f, n_t),
        in_specs=[row, row, act, act, act, act], out_specs=[out, out, out],
        out_shape=[SDS((n_f, fc, d), BF16)] * 3,
        scratch_shapes=[pltpu.VMEM((fc, d), F32)] * 3, carry=carry)


def _ffn_bwd_w_send(h, dd, g_s, u_s, dg, du, recv_gu, recv_d, layer, name, carry=()):
    s, d = h.shape
    n_f, _, fc = g_s.shape
    ts = _time_tile(s)
    n_t = s // ts
    half = fc // 2

    def chunk_of(step):
        return (step + 2 * lax.axis_index("x") + lax.axis_index("y")) % n_f

    def body(h_ref, dd_ref, g_ref, u_ref, dg_ref, du_ref, _rgu_in, _rd_in, rgu_ref, rd_ref,
             acc_g, acc_u, acc_d, st_g, st_u, st_d, pair_gu, pair_d, zeros,
             send_sems, recv_sems, local_sems, pair_sems, zero_sems):
        f, i = pl.program_id(0), pl.program_id(1)
        x, y, c_me, me = _my_pos()
        sibling = me ^ 1

        @pl.when(i == 0)
        def _():
            acc_g[...] = jnp.zeros_like(acc_g)
            acc_u[...] = jnp.zeros_like(acc_u)
            acc_d[...] = jnp.zeros_like(acc_d)

        def zero_fills():
            res = []
            for n_k, k in enumerate((2, 4, 6)):
                other = _flip(k)
                slot = _slot((other[0], other[1], 1 - c_me))
                res += [pltpu.make_async_copy(zeros, rgu_ref.at[slot, layer, pl.ds(0, half)], zero_sems.at[n_k, 0]),
                        pltpu.make_async_copy(zeros, rgu_ref.at[slot, layer, pl.ds(half, half)], zero_sems.at[n_k, 1]),
                        pltpu.make_async_copy(zeros, rd_ref.at[slot, layer], zero_sems.at[n_k, 2])]
            return res

        @pl.when((f == 0) & (i == 0))
        def _():
            zeros[...] = jnp.zeros_like(zeros)
            for cp in zero_fills():
                cp.start()

        a = _swiglu_bf16(g_ref[...], u_ref[...])[0]
        hv = h_ref[...]
        acc_g[...] += _mm_tn(dg_ref[...], hv)
        acc_u[...] += _mm_tn(du_ref[...], hv)
        acc_d[...] += _mm_tn(a, dd_ref[...])

        def messages(fs):
            c = chunk_of(fs)
            lo, hi = pl.ds(0, half), pl.ds(half, half)
            return [(st_g.at[fs], pair_gu.at[fs // 2, 0], rgu_ref, 0, c, 0),
                    (st_u.at[fs], pair_gu.at[fs // 2, 1], rgu_ref, 0, c + n_f, 1),
                    (st_d.at[fs, lo], pair_d.at[fs], rd_ref, 1, 2 * c, 2),
                    (st_d.at[fs, hi], pair_d.at[fs], rd_ref, 1, 2 * c + 1, 3)]

        def roles(p):
            same_chip = (p >> 1) == (me >> 1)
            same_c = (p & 1) == c_me
            return p == me, p == sibling, (~same_chip) & same_c, (~same_chip) & (~same_c)

        def to_owner(fs, msg, src_dev):
            src, _, buf, row, p, j = msg
            return pltpu.make_async_remote_copy(
                src_ref=src, dst_ref=buf.at[src_dev, layer], send_sem=send_sems.at[fs, j],
                recv_sem=recv_sems.at[row, src_dev], device_id=_dev(p), device_id_type=MESH)

        def to_pair(fs, msg):
            src, pair, _, _, _, j = msg
            return pltpu.make_async_remote_copy(
                src_ref=src, dst_ref=pair, send_sem=send_sems.at[fs, j], recv_sem=pair_sems.at[fs, j],
                device_id=_dev(sibling), device_id_type=MESH)

        def local(fs, msg):
            src, _, buf, _, p, j = msg
            return pltpu.make_async_copy(src, buf.at[p, layer], local_sems.at[fs, j])

        for fs in range(n_f):
            @pl.when((f == fs) & (i == n_t - 1))
            def _():
                st_g[fs] = acc_g[...].astype(BF16)
                st_u[fs] = acc_u[...].astype(BF16)
                st_d[fs] = acc_d[...].astype(BF16)
                msgs = messages(fs)
                for msg in msgs:
                    mine, sib, _, hand_over = roles(msg[4])

                    @pl.when(mine)
                    def _():
                        local(fs, msg).start()

                    @pl.when(sib)
                    def _():
                        to_owner(fs, msg, me).start()

                    @pl.when(hand_over)
                    def _():
                        to_pair(fs, msg).start()
                for msg in msgs:
                    @pl.when(roles(msg[4])[2])
                    def _():
                        src, pair = msg[0], msg[1]
                        to_pair(fs, msg).wait_recv()
                        src[...] = (src[...].astype(F32) + pair[...].astype(F32)).astype(BF16)
                        to_owner(fs, msg, me).start()

        @pl.when((f == n_f - 1) & (i == n_t - 1))
        def _():
            for fs in range(n_f):
                for msg in messages(fs):
                    mine = roles(msg[4])[0]

                    @pl.when(mine)
                    def _():
                        local(fs, msg).wait()

                    @pl.when(~mine)
                    def _():
                        to_owner(fs, msg, me).wait_send()
            for k in (1, 2, 4, 6):
                src_dev = _slot(_flip(k))
                to_owner(0, messages(0)[0], src_dev).wait_recv()
                to_owner(0, messages(0)[2], src_dev).wait_recv()
            for cp in zero_fills():
                cp.wait()

    row = pl.BlockSpec((ts, d), lambda f, i: (i, 0))
    act = pl.BlockSpec((None, ts, fc), lambda f, i: (chunk_of(f), i, 0))
    return _pcall(
        body, (h, dd, g_s, u_s, dg, du, recv_gu, recv_d), name=name, grid=(n_f, n_t),
        in_specs=[row, row, act, act, act, act, ANY, ANY], out_specs=[ANY, ANY],
        out_shape=[SDS(recv_gu.shape, recv_gu.dtype), SDS(recv_d.shape, recv_d.dtype)],
        scratch_shapes=[pltpu.VMEM((fc, d), F32)] * 3 + [pltpu.VMEM((n_f, fc, d), BF16)] * 3
        + [pltpu.VMEM((n_f // 2, 2, fc, d), BF16), pltpu.VMEM((n_f, half, d), BF16), pltpu.VMEM((half, d), BF16)]
        + [pltpu.SemaphoreType.DMA((n_f, 4)), pltpu.SemaphoreType.DMA((2, N_DEV)), pltpu.SemaphoreType.DMA((n_f, 4)),
           pltpu.SemaphoreType.DMA((n_f, 4)), pltpu.SemaphoreType.DMA((3, 3))],
        carry=carry, body_aliases={6: 0, 7: 1})


_PROJ_WIDTHS = (W_A, W_A, W_B, KV_W, KV_W, 2 * W_C)


def _mix_in_fwd(x, pre_g, w_in_t, name, carry=()):
    s, d = x.shape
    ts = _time_tile(s)

    def body(x_ref, pg_ref, w_ref, hn_ref, *outs):
        xv = x_ref[...]
        hn = (xv * _rms_r(xv) * pg_ref[...]).astype(BF16)
        hn_ref[...] = hn
        proj = _mm_nt(hn, w_ref[...])
        off = 0
        for o_ref, w in zip(outs, _PROJ_WIDTHS):
            o_ref[...] = proj[:, off:off + w]
            off += w

    row = lambda w: pl.BlockSpec((ts, w), lambda i: (i, 0))
    return _pcall(
        body, (x, pre_g, w_in_t), name=name, grid=(s // ts,),
        in_specs=[row(d), pl.BlockSpec((1, d), lambda i: (0, 0)), pl.BlockSpec((D_IN_PROJ, d), lambda i: (0, 0))],
        out_specs=[row(d)] + [row(w) for w in _PROJ_WIDTHS],
        out_shape=[SDS((s, d), BF16)] + [SDS((s, w), F32) for w in _PROJ_WIDTHS], carry=carry)


def _mix_in_bwd(dres, x, pre_g, hn, w_in_t, dlx, dlg, dq, dk, dk_up, dv, dv_up, dglu, name, carry=()):
    s, d = x.shape
    ts = _time_tile(s)
    n_t = s // ts

    def body(dres_ref, x_ref, pg_ref, hn_ref, w_ref, dlx_ref, dlg_ref, dq_ref, dk_ref, dkn_ref,
             dv_ref, dvn_ref, dglu_ref, dx_ref, dw_ref, dpg_ref, acc):
        i = pl.program_id(0)

        @pl.when(i == 0)
        def _():
            acc[...] = jnp.zeros_like(acc)
            dpg_ref[...] = jnp.zeros_like(dpg_ref)

        def with_next(cur_ref, nxt_ref):
            nxt = jnp.where(i < n_t - 1, nxt_ref[...], 0.0)
            if ts == BLK:
                return cur_ref[...] + nxt
            return jnp.concatenate([cur_ref[:ts - BLK, :], cur_ref[ts - BLK:, :] + nxt], axis=0)

        dproj = jnp.concatenate([dlx_ref[...], dlg_ref[...], dq_ref[...], with_next(dk_ref, dkn_ref),
                                 with_next(dv_ref, dvn_ref), dglu_ref[...]], axis=1).astype(BF16)
        dhn = _mm(dproj, w_ref[...])
        acc[...] += _mm_tn(dproj, hn_ref[...])
        xv = x_ref[...]
        dxv, dp = _rms_bwd(xv, _rms_r(xv), pg_ref[...], dhn)
        dpg_ref[...] += dp
        dx_ref[...] = dres_ref[...] + dxv

        @pl.when(i == n_t - 1)
        def _():
            dw_ref[...] = acc[...].astype(BF16)

    row = lambda w: pl.BlockSpec((ts, w), lambda i: (i, 0))
    nxt = pl.BlockSpec((BLK, KV_W), lambda i: (jnp.minimum(i + 1, n_t - 1), 0))
    vec = pl.BlockSpec((1, d), lambda i: (0, 0))
    full = pl.BlockSpec((D_IN_PROJ, d), lambda i: (0, 0))
    return _pcall(
        body, (dres, x, pre_g, hn, w_in_t, dlx, dlg, dq, dk, dk_up, dv, dv_up, dglu), name=name, grid=(n_t,),
        in_specs=[row(d), row(d), vec, row(d), full, row(W_A), row(W_A), row(W_B), row(KV_W), nxt,
                  row(KV_W), nxt, row(2 * W_C)],
        out_specs=[row(d), full, vec],
        out_shape=[SDS((s, d), F32), SDS((D_IN_PROJ, d), BF16), SDS((1, d), F32)],
        scratch_shapes=[pltpu.VMEM((D_IN_PROJ, d), F32)], carry=carry)


def _lru_gates(xc, lru_p):
    cw_ref, cb_ref, wa_ref, ba_ref, wx_ref, bx_ref, lam_ref = lru_p
    c = cb_ref[...]
    for j in range(LRU_K):
        c = c + cw_ref[j:j + 1, :] * _shift_down(xc, LRU_K - 1 - j)[LRU_HALO:, :]
    r = _sigmoid(_mm(c, wa_ref[...]) + ba_ref[...])
    ig = _sigmoid(_mm(c, wx_ref[...]) + bx_ref[...])
    sp = _softplus(-lam_ref[...])
    log_a = -LRU_C * r * sp
    a = jnp.exp(log_a)
    m = jnp.sqrt(_neg_expm1(2.0 * log_a))
    return c, r, ig, sp, a, m


def _lru_pspecs():
    small = lambda r: pl.BlockSpec((r, W_A), lambda i: (0, 0))
    return [small(LRU_K), small(1), small(W_A), small(1), small(W_A), small(1), small(1)]


def _lru_fwd(lx, lg, lru_p, name, carry=()):
    s = lx.shape[0]
    ts = _time_tile(s)
    n8 = ts // LRU_HALO

    def body(lx_ref, lxp_ref, lg_ref, *rest):
        lru_p, (ya_ref, h_ref, hcarry) = rest[:7], rest[7:]
        i = pl.program_id(0)
        prev = jnp.where(i > 0, lxp_ref[...], 0.0)
        xc = jnp.concatenate([prev, lx_ref[...]], axis=0)
        c, r, ig, sp, a, m = _lru_gates(xc, lru_p)
        acc_a, acc_b = a, m * (ig * c)
        t = lax.broadcasted_iota(jnp.int32, a.shape, 0)
        k = 1
        while k < ts:
            keep = t >= k
            acc_b = jnp.where(keep, acc_a * _shift_down(acc_b, k) + acc_b, acc_b)
            acc_a = jnp.where(keep, acc_a * _shift_down(acc_a, k), acc_a)
            k *= 2
        h0 = jnp.where(i > 0, hcarry[...], 0.0)
        h = acc_b + acc_a * h0
        hcarry[...] = h[ts - 1:ts, :]
        h_ref[...] = h
        ya_ref[...] = _gelu(lg_ref[...])[0] * h

    row = pl.BlockSpec((ts, W_A), lambda i: (i, 0))
    prev8 = pl.BlockSpec((LRU_HALO, W_A), lambda i: (jnp.maximum(i * n8 - 1, 0), 0))
    return _pcall(
        body, (lx, lx, lg, *lru_p), name=name, grid=(s // ts,),
        in_specs=[row, prev8, row] + _lru_pspecs(), out_specs=[row, row],
        out_shape=[SDS((s, W_A), F32), SDS((s, W_A), F32)],
        scratch_shapes=[pltpu.VMEM((1, W_A), F32)], carry=carry)


def _lru_bwd(dya, lx, lg, h_s, lru_p, name, carry=()):
    s = lx.shape[0]
    ts = _time_tile(s)
    n_t = s // ts
    n8 = ts // LRU_HALO

    def body(dya_ref, lx_ref, lxp_ref, lg_ref, h_ref, hp_ref, *rest):
        lru_p = rest[:7]
        (dlx_ref, dlg_ref, dcw_ref, dcb_ref, dwa_ref, dba_ref, dwx_ref, dbx_ref, dlam_ref,
         carry_a, carry_l, carry_dc) = rest[7:]
        cw_ref, _, wa_ref, _, wx_ref, _, lam_ref = lru_p
        i = pl.program_id(0)
        first_tile = i == n_t - 1
        last_tile = i == 0

        @pl.when(i == 0)
        def _():
            for ref in (dcw_ref, dcb_ref, dwa_ref, dba_ref, dwx_ref, dbx_ref, dlam_ref):
                ref[...] = jnp.zeros_like(ref)

        prev = jnp.where(first_tile, 0.0, lxp_ref[...])
        xc = jnp.concatenate([prev, lx_ref[...]], axis=0)
        c, r, ig, sp, a, m = _lru_gates(xc, lru_p)
        h = h_ref[...]
        hcat = jnp.concatenate([jnp.where(first_tile, 0.0, hp_ref[...]), h], axis=0)
        h_m1 = _shift_down(hcat, 1)[LRU_HALO:, :]
        lg = lg_ref[...]
        ge, th = _gelu(lg)
        dya = dya_ref[...]
        dlg_ref[...] = dya * h * _dgelu(lg, th)
        dh = dya * ge
        t = lax.broadcasted_iota(jnp.int32, a.shape, 0)
        a_next = jnp.where(t < ts - 1, _shift_up(a, 1), jnp.where(last_tile, 0.0, carry_a[...]))
        acc_a, acc_b = a_next, dh
        k = 1
        while k < ts:
            keep = t < ts - k
            acc_b = jnp.where(keep, acc_a * _shift_up(acc_b, k) + acc_b, acc_b)
            acc_a = jnp.where(keep, acc_a * _shift_up(acc_a, k), acc_a)
            k *= 2
        lam_beyond = jnp.where(last_tile, 0.0, carry_l[...])
        lmb = acc_b + acc_a * lam_beyond
        carry_a[...] = a[0:1, :]
        carry_l[...] = lmb[0:1, :]
        gi = ig * c
        dgi = lmb * m
        dla = lmb * h_m1 * a - (lmb * gi) * (a * a) / m
        dr = dla * (-LRU_C * sp)
        dsp = jnp.sum(dla * (-LRU_C * r), axis=0, keepdims=True)
        dlam_ref[...] += -dsp * _sigmoid(-lam_ref[...])
        dra = dr * r * (1.0 - r)
        dia = dgi * c * ig * (1.0 - ig)
        dc = dgi * ig + _mm_nt(dra, wa_ref[...]) + _mm_nt(dia, wx_ref[...])
        dwa_ref[...] += _mm_tn(c, dra)
        dwx_ref[...] += _mm_tn(c, dia)
        dba_ref[...] += jnp.sum(dra, axis=0, keepdims=True)
        dbx_ref[...] += jnp.sum(dia, axis=0, keepdims=True)
        dcb_ref[...] += jnp.sum(dc, axis=0, keepdims=True)
        dcc = jnp.concatenate([dc, jnp.where(last_tile, 0.0, carry_dc[...])], axis=0)
        carry_dc[...] = dc[0:LRU_HALO, :]
        dlx = jnp.zeros_like(dc)
        for j in range(LRU_K):
            sh = LRU_K - 1 - j
            dcw_ref[j:j + 1, :] += jnp.sum(dc * _shift_down(xc, sh)[LRU_HALO:, :], axis=0, keepdims=True)
            dlx = dlx + cw_ref[j:j + 1, :] * _shift_up(dcc, sh)[:ts, :]
        dlx_ref[...] = dlx

    row = pl.BlockSpec((ts, W_A), lambda i: (n_t - 1 - i, 0))
    prev8 = pl.BlockSpec((LRU_HALO, W_A), lambda i: (jnp.maximum((n_t - 1 - i) * n8 - 1, 0), 0))
    small = lambda r: pl.BlockSpec((r, W_A), lambda i: (0, 0))
    return _pcall(
        body, (dya, lx, lx, lg, h_s, h_s, *lru_p), name=name, grid=(n_t,),
        in_specs=[row, row, prev8, row, row, prev8] + _lru_pspecs(),
        out_specs=[row, row, small(LRU_K), small(1), small(W_A), small(1), small(W_A), small(1), small(1)],
        out_shape=[SDS((s, W_A), F32), SDS((s, W_A), F32), SDS((LRU_K, W_A), F32), SDS((1, W_A), F32),
                   SDS((W_A, W_A), F32), SDS((1, W_A), F32), SDS((W_A, W_A), F32), SDS((1, W_A), F32),
                   SDS((1, W_A), F32)],
        scratch_shapes=[pltpu.VMEM((1, W_A), F32), pltpu.VMEM((1, W_A), F32), pltpu.VMEM((LRU_HALO, W_A), F32)],
        carry=carry)


_ATT_ROWS = N_Q_HEADS * BLK
_GRP_ROWS = Q_PER_KV * BLK


def _attn_stack(ref, rows, g):
    return jnp.concatenate([ref[rows, h * HEAD_DIM:(h + 1) * HEAD_DIM]
                            for h in range(g * Q_PER_KV, (g + 1) * Q_PER_KV)], axis=0)


def _attn_unstack(parts):
    return jnp.concatenate([p[j * BLK:(j + 1) * BLK, :] for p in parts for j in range(Q_PER_KV)], axis=1)


def _grp(x, g):
    return x[:, g * _GRP_ROWS:(g + 1) * _GRP_ROWS]


def _attn_block(q_ref, k_ref, kp_ref, v_ref, vp_ref, sink_row, i, b):
    rows, prev = slice(b * BLK, (b + 1) * BLK), slice((b - 1) * BLK, b * BLK)
    qs, kcs, kps, vcs, vps = [], [], [], [], []
    for g in range(N_KV_HEADS):
        cols = slice(g * HEAD_DIM, (g + 1) * HEAD_DIM)
        qs.append(_attn_stack(q_ref, rows, g))
        kcs.append(k_ref[rows, cols])
        vcs.append(v_ref[rows, cols])
        kps.append(kp_ref[:, cols] if b == 0 else k_ref[prev, cols])
        vps.append(vp_ref[:, cols] if b == 0 else v_ref[prev, cols])
    scale = 1.0 / math.sqrt(HEAD_DIM)
    sc = jnp.concatenate([_mm_nt(kcs[g], qs[g]) for g in range(N_KV_HEADS)], axis=1) * scale
    sp = jnp.concatenate([_mm_nt(kps[g], qs[g]) for g in range(N_KV_HEADS)], axis=1) * scale
    kj = lax.broadcasted_iota(jnp.int32, (BLK, _ATT_ROWS), 0)
    qi = lax.broadcasted_iota(jnp.int32, (BLK, _ATT_ROWS), 1) & (BLK - 1)
    sc = jnp.where(kj <= qi, sc, NEG_BIG)
    sp = jnp.where((kj > qi) if b > 0 else ((kj > qi) & (i > 0)), sp, NEG_BIG)
    m = jnp.maximum(jnp.maximum(jnp.max(sc, axis=0, keepdims=True), jnp.max(sp, axis=0, keepdims=True)), sink_row)
    pc = jnp.exp(sc - m)
    pp = jnp.exp(sp - m)
    es = jnp.exp(sink_row - m)
    inv = 1.0 / (jnp.sum(pc, axis=0, keepdims=True) + jnp.sum(pp, axis=0, keepdims=True) + es)
    return qs, kcs, kps, vcs, vps, pc * inv, pp * inv, es * inv


def _attn_specs(s, ts):
    bpt = ts // BLK
    tile = lambda w: pl.BlockSpec((ts, w), lambda i: (i, 0))
    prv = pl.BlockSpec((BLK, KV_W), lambda i: (jnp.maximum(i * bpt - 1, 0), 0))
    sink = pl.BlockSpec((1, _ATT_ROWS), lambda i: (0, 0))
    return bpt, tile, prv, sink


def _attn_fwd(q, k, v, sink_row, name, carry=()):
    s = q.shape[0]
    ts = _time_tile(s)
    bpt, tile, prv, sink = _attn_specs(s, ts)

    def body(q_ref, k_ref, kp_ref, v_ref, vp_ref, sk_ref, y_ref):
        i = pl.program_id(0)
        for b in range(bpt):
            _, _, _, vcs, vps, pc, pp, _ = _attn_block(q_ref, k_ref, kp_ref, v_ref, vp_ref, sk_ref[...], i, b)
            outs = [_mm_tn(_grp(pc, g), vcs[g]) + _mm_tn(_grp(pp, g), vps[g]) for g in range(N_KV_HEADS)]
            y_ref[b * BLK:(b + 1) * BLK, :] = _attn_unstack(outs)

    return _pcall(
        body, (q, k, k, v, v, sink_row), name=name, grid=(s // ts,),
        in_specs=[tile(W_B), tile(KV_W), prv, tile(KV_W), prv, sink],
        out_specs=[tile(W_B)], out_shape=[SDS((s, W_B), F32)], carry=carry)


def _attn_bwd(dy, q, k, v, sinks, name, carry=()):
    s = q.shape[0]
    ts = _time_tile(s)
    n_t = s // ts
    bpt, tile, prv, sink = _attn_specs(s, ts)

    def body(dy_ref, q_ref, k_ref, kp_ref, v_ref, vp_ref, sk_ref, dq_ref, dk_ref, dv_ref, dku_ref, dvu_ref, dsk_ref):
        i = pl.program_id(0)

        @pl.when(i == 0)
        def _():
            dsk_ref[...] = jnp.zeros_like(dsk_ref)

        scale = 1.0 / math.sqrt(HEAD_DIM)
        groups = range(N_KV_HEADS)
        head_row = lax.broadcasted_iota(jnp.int32, (N_Q_HEADS, BLK), 0)
        dsk = jnp.zeros((N_Q_HEADS, BLK), F32)
        dk_blocks, dv_blocks = [], []
        for b in range(bpt):
            rows = slice(b * BLK, (b + 1) * BLK)
            qs, kcs, kps, vcs, vps, pc, pp, ps = _attn_block(q_ref, k_ref, kp_ref, v_ref, vp_ref, sk_ref[...], i, b)
            dos = [_attn_stack(dy_ref, rows, g) for g in groups]
            dpc = jnp.concatenate([_mm_nt(vcs[g], dos[g]) for g in groups], axis=1)
            dpp = jnp.concatenate([_mm_nt(vps[g], dos[g]) for g in groups], axis=1)
            delta = jnp.sum(pc * dpc, axis=0, keepdims=True) + jnp.sum(pp * dpp, axis=0, keepdims=True)
            dsc = pc * (dpc - delta) * scale
            dsp = pp * (dpp - delta) * scale
            dq_ref[rows, :] = _attn_unstack([_mm_tn(_grp(dsc, g), kcs[g]) + _mm_tn(_grp(dsp, g), kps[g])
                                             for g in groups])
            dk_blocks.append(jnp.concatenate([_mm(_grp(dsc, g), qs[g]) for g in groups], axis=1))
            dv_blocks.append(jnp.concatenate([_mm(_grp(pc, g), dos[g]) for g in groups], axis=1))
            dkp = jnp.concatenate([_mm(_grp(dsp, g), qs[g]) for g in groups], axis=1)
            dvp = jnp.concatenate([_mm(_grp(pp, g), dos[g]) for g in groups], axis=1)
            if b == 0:
                dku_ref[...] = dkp
                dvu_ref[...] = dvp
            else:
                dk_blocks[b - 1] = dk_blocks[b - 1] + dkp
                dv_blocks[b - 1] = dv_blocks[b - 1] + dvp
            dsink = -ps * delta
            for h in range(N_Q_HEADS):
                dsk = dsk + jnp.where(head_row == h, jnp.sum(dsink[:, h * BLK:(h + 1) * BLK], axis=1, keepdims=True), 0.0)
        for b in range(bpt):
            dk_ref[b * BLK:(b + 1) * BLK, :] = dk_blocks[b]
            dv_ref[b * BLK:(b + 1) * BLK, :] = dv_blocks[b]
        dsk_ref[...] += dsk

    up = pl.BlockSpec((BLK, KV_W), lambda i: (i, 0))
    return _pcall(
        body, (dy, q, k, k, v, v, sinks), name=name, grid=(n_t,),
        in_specs=[tile(W_B), tile(W_B), tile(KV_W), prv, tile(KV_W), prv, sink],
        out_specs=[tile(W_B), tile(KV_W), tile(KV_W), up, up, pl.BlockSpec((N_Q_HEADS, BLK), lambda i: (0, 0))],
        out_shape=[SDS((s, W_B), F32), SDS((s, KV_W), F32), SDS((s, KV_W), F32), SDS((n_t * BLK, KV_W), F32),
                   SDS((n_t * BLK, KV_W), F32), SDS((N_Q_HEADS, BLK), F32)], carry=carry)


def _cc_recompute(glu_ref, glup_ref, cw_ref, cb_ref, first_tile):
    prev = jnp.where(first_tile, 0.0, glup_ref[...])
    ge = jnp.concatenate([prev, glu_ref[...]], axis=0)
    y0 = ge[:, :W_C] * _sigmoid(ge[:, W_C:])
    y1 = cb_ref[...]
    for j in range(CC_K):
        y1 = y1 + cw_ref[j:j + 1, :] * _shift_down(y0, CC_K - 1 - j)[CC_HALO:, :]
    return y0, y1


def _ln_stats(y1):
    mu = jnp.mean(y1, axis=-1, keepdims=True)
    xc = y1 - mu
    rstd = lax.rsqrt(jnp.mean(xc * xc, axis=-1, keepdims=True) + LN_EPS)
    return xc * rstd, rstd


def _cc_specs(s, ts):
    n32 = ts // CC_HALO
    row = lambda w: pl.BlockSpec((ts, w), lambda i: (i, 0))
    prev = pl.BlockSpec((CC_HALO, 2 * W_C), lambda i: (jnp.maximum(i * n32 - 1, 0), 0))
    small = lambda r: pl.BlockSpec((r, W_C), lambda i: (0, 0))
    return row, prev, small


def _cc_fwd(glu, cw, cb, lng, lnb, name, carry=()):
    s = glu.shape[0]
    ts = _time_tile(s)
    row, prev, small = _cc_specs(s, ts)

    def body(glu_ref, glup_ref, cw_ref, cb_ref, lng_ref, lnb_ref, y_ref):
        _, y1 = _cc_recompute(glu_ref, glup_ref, cw_ref, cb_ref, pl.program_id(0) == 0)
        xhat, _ = _ln_stats(y1)
        z = xhat * lng_ref[...] + lnb_ref[...]
        y_ref[...] = z * _sigmoid(z)

    return _pcall(
        body, (glu, glu, cw, cb, lng, lnb), name=name, grid=(s // ts,),
        in_specs=[row(2 * W_C), prev, small(CC_HALO), small(1), small(1), small(1)],
        out_specs=[row(W_C)], out_shape=[SDS((s, W_C), F32)], carry=carry)


def _cc_bwd_conv(dy, glu, cw, cb, lng, lnb, name, carry=()):
    s = glu.shape[0]
    ts = _time_tile(s)
    row, prev, small = _cc_specs(s, ts)

    def body(dy_ref, glu_ref, glup_ref, cw_ref, cb_ref, lng_ref, lnb_ref, dy1_ref, dcw_ref, dcb_ref, dlng_ref, dlnb_ref):
        i = pl.program_id(0)

        @pl.when(i == 0)
        def _():
            for ref in (dcw_ref, dcb_ref, dlng_ref, dlnb_ref):
                ref[...] = jnp.zeros_like(ref)

        y0, y1 = _cc_recompute(glu_ref, glup_ref, cw_ref, cb_ref, i == 0)
        xhat, rstd = _ln_stats(y1)
        z = xhat * lng_ref[...] + lnb_ref[...]
        dz = dy_ref[...] * _dsilu(z, _sigmoid(z))
        dlng_ref[...] += jnp.sum(dz * xhat, axis=0, keepdims=True)
        dlnb_ref[...] += jnp.sum(dz, axis=0, keepdims=True)
        dxh = dz * lng_ref[...]
        dy1 = rstd * (dxh - jnp.mean(dxh, axis=-1, keepdims=True) - xhat * jnp.mean(dxh * xhat, axis=-1, keepdims=True))
        dy1_ref[...] = dy1
        dcb_ref[...] += jnp.sum(dy1, axis=0, keepdims=True)
        for j in range(CC_K):
            dcw_ref[j:j + 1, :] += jnp.sum(dy1 * _shift_down(y0, CC_K - 1 - j)[CC_HALO:, :], axis=0, keepdims=True)

    return _pcall(
        body, (dy, glu, glu, cw, cb, lng, lnb), name=name, grid=(s // ts,),
        in_specs=[row(W_C), row(2 * W_C), prev, small(CC_HALO), small(1), small(1), small(1)],
        out_specs=[row(W_C), small(CC_HALO), small(1), small(1), small(1)],
        out_shape=[SDS((s, W_C), F32), SDS((CC_HALO, W_C), F32)] + [SDS((1, W_C), F32)] * 3, carry=carry)


def _cc_bwd_glu(dy1, glu, cw, name, carry=()):
    s = glu.shape[0]
    ts = _time_tile(s)
    n_t = s // ts
    n32 = ts // CC_HALO

    def body(dy1_ref, dyn_ref, glu_ref, cw_ref, dglu_ref):
        i = pl.program_id(0)
        dcat = jnp.concatenate([dy1_ref[...], jnp.where(i < n_t - 1, dyn_ref[...], 0.0)], axis=0)
        dy0 = jnp.zeros((ts, W_C), F32)
        for j in range(CC_K):
            dy0 = dy0 + cw_ref[j:j + 1, :] * _shift_up(dcat, CC_K - 1 - j)[:ts, :]
        a = glu_ref[:, :W_C]
        sg = _sigmoid(glu_ref[:, W_C:])
        dglu_ref[...] = jnp.concatenate([dy0 * sg, dy0 * a * sg * (1.0 - sg)], axis=1)

    row = lambda w: pl.BlockSpec((ts, w), lambda i: (i, 0))
    nxt = pl.BlockSpec((CC_HALO, W_C), lambda i: (jnp.minimum((i + 1) * n32, s // CC_HALO - 1), 0))
    return _pcall(
        body, (dy1, dy1, glu, cw), name=name, grid=(n_t,),
        in_specs=[row(W_C), nxt, row(2 * W_C), pl.BlockSpec((CC_HALO, W_C), lambda i: (0, 0))],
        out_specs=[row(2 * W_C)], out_shape=[SDS((s, 2 * W_C), F32)], carry=carry)


_MIX_OFFS = ((0, W_A), (W_A, W_A + W_B), (W_A + W_B, W_A + W_B + W_C))


def _mix_out_fwd(x, ya, yb, yc, group_g, w_out, post_g, name, carry=()):
    s, d = x.shape
    ts = _time_tile(s)
    dm = w_out.shape[0]

    def body(x_ref, ya_ref, yb_ref, yc_ref, gg_ref, w_ref, qg_ref, xo_ref, o_ref):
        parts = []
        for y_ref, (lo, hi) in zip((ya_ref, yb_ref, yc_ref), _MIX_OFFS):
            yv = y_ref[...]
            parts.append(yv * _rms_r(yv) * gg_ref[:, lo:hi])
        o = _mm(jnp.concatenate(parts, axis=1), w_ref[...])
        o_ref[...] = o
        xo_ref[...] = x_ref[...] + o * _rms_r(o) * qg_ref[...]

    row = lambda w: pl.BlockSpec((ts, w), lambda i: (i, 0))
    return _pcall(
        body, (x, ya, yb, yc, group_g, w_out, post_g), name=name, grid=(s // ts,),
        in_specs=[row(d), row(W_A), row(W_B), row(W_C), pl.BlockSpec((1, dm), lambda i: (0, 0)),
                  pl.BlockSpec((dm, d), lambda i: (0, 0)), pl.BlockSpec((1, d), lambda i: (0, 0))],
        out_specs=[row(d), row(d)], out_shape=[SDS((s, d), F32), SDS((s, d), F32)], carry=carry)


def _mix_out_bwd(dxo, o, ya, yb, yc, group_g, w_out, post_g, name, carry=()):
    s, d = o.shape
    ts = _time_tile(s)
    n_t = s // ts
    dm = w_out.shape[0]

    def body(dxo_ref, o_ref, ya_ref, yb_ref, yc_ref, gg_ref, w_ref, qg_ref,
             dya_ref, dyb_ref, dyc_ref, dw_ref, dqg_ref, dgg_ref, acc):
        i = pl.program_id(0)

        @pl.when(i == 0)
        def _():
            acc[...] = jnp.zeros_like(acc)
            dqg_ref[...] = jnp.zeros_like(dqg_ref)
            dgg_ref[...] = jnp.zeros_like(dgg_ref)

        ov = o_ref[...]
        do, dq = _rms_bwd(ov, _rms_r(ov), qg_ref[...], dxo_ref[...])
        dqg_ref[...] += dq
        do = do.astype(BF16)
        dyn = _mm_nt(do, w_ref[...])
        parts, dggs = [], []
        for y_ref, dy_ref, (lo, hi) in zip((ya_ref, yb_ref, yc_ref), (dya_ref, dyb_ref, dyc_ref), _MIX_OFFS):
            yv = y_ref[...]
            r = _rms_r(yv)
            gg = gg_ref[:, lo:hi]
            parts.append(yv * r * gg)
            dyv, dg = _rms_bwd(yv, r, gg, dyn[:, lo:hi])
            dy_ref[...] = dyv
            dggs.append(dg)
        dgg_ref[...] += jnp.concatenate(dggs, axis=1)
        acc[...] += _mm_tn(jnp.concatenate(parts, axis=1), do)

        @pl.when(i == n_t - 1)
        def _():
            dw_ref[...] = acc[...].astype(BF16)

    row = lambda w: pl.BlockSpec((ts, w), lambda i: (i, 0))
    full = pl.BlockSpec((dm, d), lambda i: (0, 0))
    return _pcall(
        body, (dxo, o, ya, yb, yc, group_g, w_out, post_g), name=name, grid=(n_t,),
        in_specs=[row(d), row(d), row(W_A), row(W_B), row(W_C), pl.BlockSpec((1, dm), lambda i: (0, 0)), full,
                  pl.BlockSpec((1, d), lambda i: (0, 0))],
        out_specs=[row(W_A), row(W_B), row(W_C), full, pl.BlockSpec((1, d), lambda i: (0, 0)),
                   pl.BlockSpec((1, dm), lambda i: (0, 0))],
        out_shape=[SDS((s, W_A), F32), SDS((s, W_B), F32), SDS((s, W_C), F32), SDS((dm, d), BF16),
                   SDS((1, d), F32), SDS((1, dm), F32)],
        scratch_shapes=[pltpu.VMEM((dm, d), F32)], carry=carry)


def _loss_head(y, target, name):
    s, d = y.shape
    ts = _time_tile(s)

    def body(y_ref, t_ref, loss_ref, dy_ref):
        @pl.when(pl.program_id(0) == 0)
        def _():
            loss_ref[...] = jnp.zeros_like(loss_ref)

        err = y_ref[...] - t_ref[...]
        dy_ref[...] = err * (1.0 / d)
        per_tok = jnp.mean(err * err, axis=-1, keepdims=True)
        loss_ref[...] += 0.5 * jnp.sum(per_tok, axis=0, keepdims=True)

    row = pl.BlockSpec((ts, d), lambda i: (i, 0))
    return _pcall(body, (y, target), name=name, grid=(s // ts,), in_specs=[row, row],
                  out_specs=[pl.BlockSpec((1, BLK), lambda i: (0, 0)), row],
                  out_shape=[SDS((1, BLK), F32), SDS((s, d), F32)])[0]


def _adamw_math(w, g, m, v):
    m = ADAM_B1 * m + (1.0 - ADAM_B1) * g
    v = ADAM_B2 * v + (1.0 - ADAM_B2) * (g * g)
    m_hat = m / (1.0 - ADAM_B1 ** ADAM_STEP)
    v_hat = v / (1.0 - ADAM_B2 ** ADAM_STEP)
    delta = -ADAM_LR * (m_hat / (jnp.sqrt(v_hat) + ADAM_EPS) + ADAM_WD * w)
    return delta, m, v


def _row_tile(rows, cap=256):
    best = None
    for t in range(16, min(rows, cap) + 1, 16):
        if rows % t == 0:
            best = t
    return best if best is not None else rows


def _reduce_adamw(recv, w, m, v, name):
    n_l, r, c = w.shape
    tr = _row_tile(r)

    def body(recv_ref, w_ref, m_ref, v_ref, g_ref, d_ref, nm_ref, nv_ref):
        g = recv_ref[0].astype(F32)
        for p in range(1, N_DEV):
            g = g + recv_ref[p].astype(F32)
        g_ref[...] = g
        d_ref[...], nm_ref[...], nv_ref[...] = _adamw_math(w_ref[...], g, m_ref[...], v_ref[...])

    blk = pl.BlockSpec((None, tr, c), lambda l, i: (l, i, 0))
    return _pcall(
        body, (recv, w, m, v), name=name, grid=(n_l, r // tr),
        in_specs=[pl.BlockSpec((N_DEV, None, tr, c), lambda l, i: (0, l, i, 0)), blk, blk, blk],
        out_specs=[blk] * 4, out_shape=[SDS(w.shape, F32)] * 4)[0]


def _reduce_adamw_small(parts, w, m, v, name):
    def body(p_ref, w_ref, m_ref, v_ref, g_ref, d_ref, nm_ref, nv_ref):
        g = p_ref[0]
        for p in range(1, N_DEV):
            g = g + p_ref[p]
        g_ref[...] = g
        d_ref[...], nm_ref[...], nv_ref[...] = _adamw_math(w_ref[...], g, m_ref[...], v_ref[...])

    vm = pl.BlockSpec(memory_space=pltpu.VMEM)
    return pl.pallas_call(body, name=name, in_specs=[vm] * 4, out_specs=[vm] * 4, out_shape=[SDS(w.shape, F32)] * 4,
                          compiler_params=pltpu.CompilerParams(vmem_limit_bytes=VMEM_LIMIT))(parts, w, m, v)


def _rows_of(shape):
    return -(-math.prod(shape) // (8 * BLK)) * 8


def _pack(arrs):
    rows = []
    for a in arrs:
        n, r = math.prod(a.shape), _rows_of(a.shape)
        if n % BLK == 0:
            part = a.reshape(n // BLK, BLK)
            rows.append(part if n // BLK == r else jnp.pad(part, ((0, r - n // BLK), (0, 0))))
        else:
            rows.append(jnp.pad(a.reshape(-1), (0, r * BLK - n)).reshape(r, BLK))
    return jnp.concatenate(rows, axis=0)


def _unpack(packed, shapes):
    out, row = [], 0
    for shp in shapes:
        n, r = math.prod(shp), _rows_of(shp)
        if n % BLK == 0:
            out.append(packed[row:row + n // BLK].reshape(shp))
        else:
            out.append(packed[row:row + r].reshape(-1)[:n].reshape(shp))
        row += r
    return out


def _block_diag(w):
    nb, bw, _ = w.shape
    eye = jnp.eye(nb, dtype=w.dtype)
    return (eye[:, None, :, None] * w[:, :, None, :]).reshape(nb * bw, nb * bw)


def _diag_blocks(wd, nb):
    bw = wd.shape[0] // nb
    return jnp.stack([wd[b * bw:(b + 1) * bw, b * bw:(b + 1) * bw] for b in range(nb)])


WEIGHT_NAMES = ['ffn1_pre_g', 'ffn1_w_gu', 'ffn1_w_down', 'ffn1_post_g', 'mix_pre_g', 'w_in', 'lru_conv_w', 'lru_conv_b',
                'lru_w_a', 'lru_b_a', 'lru_w_x', 'lru_b_x', 'lru_lambda', 'attn_sinks', 'conv_w', 'conv_b', 'conv_ln_g',
                'conv_ln_b', 'group_g', 'w_out', 'mix_post_g', 'ffn2_pre_g', 'ffn2_w_gu', 'ffn2_w_down', 'ffn2_post_g']
BIG = ('ffn1_w_gu', 'ffn1_w_down', 'w_in', 'w_out', 'ffn2_w_gu', 'ffn2_w_down')
TRANSPOSED = ('ffn1_w_gu', 'ffn2_w_gu', 'w_in')
SMALL = tuple(k for k in WEIGHT_NAMES if k not in BIG)
CHANNEL_SHARDED = ('lru_conv_w', 'conv_w')


def _step(x, target, w, m, v):
    n_l = w['ffn1_pre_g'].shape[0]
    assert n_l == 2, "the exchange schedule below is laid out for two layers"
    s, d = x.shape[1], x.shape[2]
    x = x.reshape(s, d)
    target = target.reshape(s, d)
    me = _my_pos()[3]
    tview = lambda t, k: jnp.swapaxes(t[k], 1, 2) if k in TRANSPOSED else t[k]
    wb = {k: tview(w, k).astype(BF16) for k in BIG}
    vec = lambda name, l: w[name][l][None, :]

    conv_shard = _pack([w['lru_conv_w'], w['conv_w']])
    g0 = _all_gather([(wb['ffn1_w_gu'], 0), (wb['ffn1_w_down'], 0), (wb['w_in'], 0), (wb['w_out'], 0),
                      (conv_shard, None)], "all_gather_first")
    wts = [dict(), dict()]
    wts[0]['ffn1_w_gu'], wts[0]['ffn1_w_down'], wts[0]['w_in'], wts[0]['w_out'], conv_g = g0
    ch = W_A // N_DEV
    conv_parts = [_unpack(conv_g[p], [(n_l, LRU_K, ch), (n_l, CC_K, ch)]) for p in range(N_DEV)]
    lru_cw = jnp.concatenate([cp[0] for cp in conv_parts], axis=-1)
    cc_cw = jnp.concatenate([cp[1] for cp in conv_parts], axis=-1)
    cc_cw = jnp.pad(cc_cw, ((0, 0), (0, CC_HALO - CC_K), (0, 0)))

    fc = wb['ffn1_w_gu'].shape[1]
    cut1, cut2 = (fc * 4 // 11 + 15) // 16 * 16, (fc * 27 // 44 + 15) // 16 * 16
    gather_plan = {
        ('ffn1', 0): [('A', 'f2_0', ('ffn2_w_gu', 'ffn2_w_down'), 0)],
        ('mix_in', 0): [('B', 'f2_0'), ('A', 'g1_1a', ('ffn1_w_gu',), 1, (0, cut1))],
        ('lru', 0): [('A', 'g1_1b', ('ffn1_w_gu',), 1, (cut1, cut2 - cut1), 'g1_1a')],
        ('attn', 0): [('A', 'g1_1', ('ffn1_w_gu',), 1, (cut2, fc - cut2), 'g1_1b')],
        ('cconv', 0): [('B', 'g1_1')],
        ('ffn2', 0): [('D', None, ('ffn1_w_down',), 1), ('A', 'wi_1', ('w_in',), 1), ('A', 'wo_1', ('w_out',), 1)],
        ('ffn1', 1): [('A', 'f2_1', ('ffn2_w_gu', 'ffn2_w_down'), 1), ('B', 'wi_1'), ('B', 'wo_1')],
        ('mix_in', 1): [('B', 'f2_1')],
    }
    pend = {}

    def fwd(kernel_name, l, fn, *args):
        plan = gather_plan.get((kernel_name, l), [])
        carry = []
        for st in plan:
            if st[0] == 'B':
                carry.append(_gather_b(pend[st[1]][2]))
            else:
                rows = st[4] if len(st) > 4 else None
                into = pend.pop(st[5])[2] if len(st) > 5 else [None] * len(st[2])
                carry.append(_gather_a([(wb[k], st[3], rows, buf) for k, buf in zip(st[2], into)],
                                       two_level=st[0] == 'A'))
        outs, ex = fn(*args, f"{kernel_name}_fwd_l{l}", carry)
        for st, bufs in zip(plan, ex):
            if st[0] == 'A':
                pend[st[1]] = (st[2], st[3], bufs)
            else:
                names, wl = (st[2], st[3]) if st[0] == 'D' else pend.pop(st[1])[:2]
                for k, b in zip(names, bufs):
                    wts[wl][k] = b
        return outs

    saved = []
    h = x
    for l in range(n_l):
        sv = {'x0': h}
        lw = wts[l]
        x1, sv['h1'], sv['g1'], sv['u1'], sv['d1'] = fwd(
            'ffn1', l, _ffn_fwd, h, vec('ffn1_pre_g', l), vec('ffn1_post_g', l), lw['ffn1_w_gu'], lw['ffn1_w_down'])
        sv['x1'] = x1
        sv['hn'], lx, lg, q, k, vv, glu = fwd('mix_in', l, _mix_in_fwd, x1, vec('mix_pre_g', l),
                                              lw['w_in'].reshape(D_IN_PROJ, d))
        sv.update(lx=lx, lg=lg, q=q, k=k, v=vv, glu=glu)
        lru_p = (lru_cw[l], vec('lru_conv_b', l), _block_diag(w['lru_w_a'][l]).astype(BF16), vec('lru_b_a', l),
                 _block_diag(w['lru_w_x'][l]).astype(BF16), vec('lru_b_x', l), vec('lru_lambda', l))
        cc_p = (cc_cw[l], vec('conv_b', l), vec('conv_ln_g', l), vec('conv_ln_b', l))
        sv.update(lru_p=lru_p, cc_p=cc_p)
        sv['ya'], sv['hs'] = fwd('lru', l, _lru_fwd, lx, lg, lru_p)
        sv['sink_row'] = jnp.repeat(w['attn_sinks'][l], BLK)[None, :]
        (sv['yb'],) = fwd('attn', l, _attn_fwd, q, k, vv, sv['sink_row'])
        (sv['yc'],) = fwd('cconv', l, _cc_fwd, glu, *cc_p)
        x2, sv['o'] = fwd('mix_out', l, _mix_out_fwd, x1, sv['ya'], sv['yb'], sv['yc'], vec('group_g', l),
                          lw['w_out'].reshape(-1, d), vec('mix_post_g', l))
        sv['x2'] = x2
        h, sv['h2'], sv['g2'], sv['u2'], sv['d2'] = fwd(
            'ffn2', l, _ffn_fwd, x2, vec('ffn2_pre_g', l), vec('ffn2_post_g', l), lw['ffn2_w_gu'], lw['ffn2_w_down'])
        saved.append(sv)

    loss_row, dh = _loss_head(h, target, "loss_head")

    recv = {k: None for k in BIG}
    ready = {}
    small = [dict() for _ in range(n_l)]

    c_even, c_odd = tuple(range(0, N_DEV, 2)), tuple(range(1, N_DEV, 2))

    def exchange(keys):
        return _grad_x([(ready[key[:2]], key[1], recv[key[0]]) + tuple(key[2:]) for key in keys], n_l)

    def received(keys, bufs):
        for key, b in zip(keys, bufs):
            recv[key[0]] = b

    def run(fn, *args, keys=()):
        outs, ex = fn(*args, carry=[exchange(keys)] if keys else [])
        if keys:
            received(keys, ex[0])
        return outs

    for l in reversed(range(n_l)):
        sv, sg, lw = saved[l], small[l], wts[l]
        keys = [] if l == n_l - 1 else [('ffn1_w_gu', l + 1)]
        dx2, dd, dg, du, sg['ffn2_pre_g'], sg['ffn2_post_g'] = run(
            _ffn_bwd_act, dh, sv['d2'], sv['x2'], vec('ffn2_pre_g', l), vec('ffn2_post_g', l), sv['g2'], sv['u2'],
            lw['ffn2_w_gu'], lw['ffn2_w_down'], f"ffn2_bwd_act_l{l}", keys=keys)
        keys = [] if l == n_l - 1 else [('ffn1_w_down', l + 1), ('w_in', l + 1, c_odd)]
        dwg, dwu, dwd = run(_ffn_bwd_w, sv['h2'], dd, sv['g2'], sv['u2'], dg, du, f"ffn2_bwd_w_l{l}", keys=keys)
        ready[('ffn2_w_gu', l)] = [dwg, dwu]
        ready[('ffn2_w_down', l)] = [dwd.reshape(N_DEV, -1, d)]
        dya, dyb, dyc, dw_out, sg['mix_post_g'], sg['group_g'] = run(
            _mix_out_bwd, dx2, sv['o'], sv['ya'], sv['yb'], sv['yc'], vec('group_g', l), lw['w_out'].reshape(-1, d),
            vec('mix_post_g', l), f"mix_out_bwd_l{l}")
        ready[('w_out', l)] = [dw_out.reshape(N_DEV, -1, d)]
        (dlx, dlg, sg['lru_conv_w'], sg['lru_conv_b'], dwa, sg['lru_b_a'], dwx, sg['lru_b_x'],
         sg['lru_lambda']) = run(_lru_bwd, dya, sv['lx'], sv['lg'], sv['hs'], sv['lru_p'], f"lru_bwd_l{l}")
        sg['lru_w_a'] = _diag_blocks(dwa, A_BLOCKS)
        sg['lru_w_x'] = _diag_blocks(dwx, A_BLOCKS)
        dq, dk, dv, dk_up, dv_up, dsk = run(_attn_bwd, dyb, sv['q'], sv['k'], sv['v'], sv['sink_row'],
                                            f"attn_bwd_l{l}", keys=[('ffn2_w_down', l, c_even)] if l == 0 else [])
        sg['attn_sinks'] = dsk[:, 0]
        dy1, dcw, sg['conv_b'], sg['conv_ln_g'], sg['conv_ln_b'] = run(
            _cc_bwd_conv, dyc, sv['glu'], *sv['cc_p'], f"cconv_bwd_conv_l{l}", keys=[('w_out', l)] if l == 0 else [])
        sg['conv_w'] = dcw[:CC_K]
        (dglu,) = run(_cc_bwd_glu, dy1, sv['glu'], sv['cc_p'][0], f"cconv_bwd_glu_l{l}")
        dx1, dw_in, sg['mix_pre_g'] = run(
            _mix_in_bwd, dx2, sv['x1'], vec('mix_pre_g', l), sv['hn'], lw['w_in'].reshape(D_IN_PROJ, d),
            dlx, dlg, dq, dk, dk_up, dv, dv_up, dglu, f"mix_in_bwd_l{l}",
            keys=[('ffn2_w_down', l, c_odd)] if l == 0 else [('w_out', l)])
        ready[('w_in', l)] = [dw_in.reshape(N_DEV, -1, d)]
        dh, dd, dg, du, sg['ffn1_pre_g'], sg['ffn1_post_g'] = run(
            _ffn_bwd_act, dx1, sv['d1'], sv['x0'], vec('ffn1_pre_g', l), vec('ffn1_post_g', l), sv['g1'], sv['u1'],
            lw['ffn1_w_gu'], lw['ffn1_w_down'], f"ffn1_bwd_act_l{l}",
            keys=[('ffn2_w_gu', l), ('w_in', l)] if l == 0 else [('ffn2_w_gu', l)])
        if l > 0:
            dwg, dwu, dwd = run(_ffn_bwd_w, sv['h1'], dd, sv['g1'], sv['u1'], dg, du, f"ffn1_bwd_w_l{l}",
                                keys=[('ffn2_w_down', l), ('w_in', l, c_even)])
            ready[('ffn1_w_gu', l)] = [dwg, dwu]
            ready[('ffn1_w_down', l)] = [dwd.reshape(N_DEV, -1, d)]
        else:
            part = _pack([jnp.stack([small[j][k] for j in range(n_l)]) for k in SMALL] + [loss_row])
            (recv['ffn1_w_gu'], recv['ffn1_w_down']), ex = _ffn_bwd_w_send(
                sv['h1'], dd, sv['g1'], sv['u1'], dg, du, recv['ffn1_w_gu'], recv['ffn1_w_down'], 0, "ffn1_bwd_w_send_l0",
                [_gather_a([(part, None)], two_level=False)])
            small_parts = ex[0][0]
    grad_x = dh.reshape(1, s, d)

    out = {}
    for k in BIG:
        res = _reduce_adamw(recv[k], tview(w, k), tview(m, k), tview(v, k), f"reduce_adamw_{k}")
        out[k] = [jnp.swapaxes(r, 1, 2) for r in res] if k in TRANSPOSED else res

    small_shapes = [(n_l,) + tuple(small[0][k].shape) for k in SMALL]

    def widen(t, k):
        if k not in CHANNEL_SHARDED:
            return t.reshape((n_l,) + tuple(small[0][k].shape))
        full = jnp.zeros((n_l,) + tuple(small[0][k].shape), F32)
        return lax.dynamic_update_slice_in_dim(full, t, me * ch, axis=2)

    no_w = jnp.zeros(loss_row.shape, F32)
    packed = [_pack([widen(src[k], k) for k in SMALL] + [no_w]) for src in (w, m, v)]
    res = _reduce_adamw_small(small_parts, *packed, "reduce_adamw_small")
    loss = _unpack(res[0], small_shapes + [loss_row.shape])[-1][0, 0]
    for k, g, dlt, nm, nv in zip(SMALL, *[_unpack(r, small_shapes) for r in res]):
        vals = [g, dlt, nm, nv]
        if k in CHANNEL_SHARDED:
            vals = [lax.dynamic_slice_in_dim(t, me * ch, ch, axis=2) for t in vals]
        out[k] = [t.reshape(w[k].shape) for t in vals]

    return (loss, grad_x, *[out[k][0] for k in WEIGHT_NAMES], *[out[k][1] for k in WEIGHT_NAMES],
            *[out[k][2] for k in WEIGHT_NAMES], *[out[k][3] for k in WEIGHT_NAMES])


def kernel(x, ffn1_pre_g, ffn1_w_gu, ffn1_w_down, ffn1_post_g, mix_pre_g, w_in, lru_conv_w, lru_conv_b, lru_w_a, lru_b_a, lru_w_x, lru_b_x, lru_lambda, attn_sinks, conv_w, conv_b, conv_ln_g, conv_ln_b, group_g, w_out, mix_post_g, ffn2_pre_g, ffn2_w_gu, ffn2_w_down, ffn2_post_g, loss_target, m_ffn1_pre_g, m_ffn1_w_gu, m_ffn1_w_down, m_ffn1_post_g, m_mix_pre_g, m_w_in, m_lru_conv_w, m_lru_conv_b, m_lru_w_a, m_lru_b_a, m_lru_w_x, m_lru_b_x, m_lru_lambda, m_attn_sinks, m_conv_w, m_conv_b, m_conv_ln_g, m_conv_ln_b, m_group_g, m_w_out, m_mix_post_g, m_ffn2_pre_g, m_ffn2_w_gu, m_ffn2_w_down, m_ffn2_post_g, v_ffn1_pre_g, v_ffn1_w_gu, v_ffn1_w_down, v_ffn1_post_g, v_mix_pre_g, v_w_in, v_lru_conv_w, v_lru_conv_b, v_lru_w_a, v_lru_b_a, v_lru_w_x, v_lru_b_x, v_lru_lambda, v_attn_sinks, v_conv_w, v_conv_b, v_conv_ln_g, v_conv_ln_b, v_group_g, v_w_out, v_mix_post_g, v_ffn2_pre_g, v_ffn2_w_gu, v_ffn2_w_down, v_ffn2_post_g):
    args = (ffn1_pre_g, ffn1_w_gu, ffn1_w_down, ffn1_post_g, mix_pre_g, w_in, lru_conv_w, lru_conv_b, lru_w_a, lru_b_a, lru_w_x, lru_b_x, lru_lambda, attn_sinks, conv_w, conv_b, conv_ln_g, conv_ln_b, group_g, w_out, mix_post_g, ffn2_pre_g, ffn2_w_gu, ffn2_w_down, ffn2_post_g)
    ms = (m_ffn1_pre_g, m_ffn1_w_gu, m_ffn1_w_down, m_ffn1_post_g, m_mix_pre_g, m_w_in, m_lru_conv_w, m_lru_conv_b, m_lru_w_a, m_lru_b_a, m_lru_w_x, m_lru_b_x, m_lru_lambda, m_attn_sinks, m_conv_w, m_conv_b, m_conv_ln_g, m_conv_ln_b, m_group_g, m_w_out, m_mix_post_g, m_ffn2_pre_g, m_ffn2_w_gu, m_ffn2_w_down, m_ffn2_post_g)
    vs = (v_ffn1_pre_g, v_ffn1_w_gu, v_ffn1_w_down, v_ffn1_post_g, v_mix_pre_g, v_w_in, v_lru_conv_w, v_lru_conv_b, v_lru_w_a, v_lru_b_a, v_lru_w_x, v_lru_b_x, v_lru_lambda, v_attn_sinks, v_conv_w, v_conv_b, v_conv_ln_g, v_conv_ln_b, v_group_g, v_w_out, v_mix_post_g, v_ffn2_pre_g, v_ffn2_w_gu, v_ffn2_w_down, v_ffn2_post_g)
    return _step(x, loss_target, dict(zip(WEIGHT_NAMES, args)), dict(zip(WEIGHT_NAMES, ms)), dict(zip(WEIGHT_NAMES, vs)))
```

```python
import functools
import math
import operator

import jax
import jax.numpy as jnp
from jax import lax
from jax.experimental import pallas as pl
from jax.experimental.pallas import tpu as pltpu

F32 = jnp.float32
BF16 = jnp.bfloat16
N_DEV = 8
AXES = ("x", "y", "c")
MESH = pl.DeviceIdType.MESH

NORM_EPS = 1e-6
LN_EPS = 1e-5
NEG_BIG = -1e30
W_A = 256
W_B = 512
W_C = 256
HEAD_DIM = 64
N_Q_HEADS = 8
N_KV_HEADS = 2
Q_PER_KV = N_Q_HEADS // N_KV_HEADS
KV_W = N_KV_HEADS * HEAD_DIM
BLK = 128
LRU_K = 4
LRU_C = 8.0
A_BLOCKS = 4
CC_K = 31
CC_HALO = 32
LRU_HALO = 8
D_IN_PROJ = 2 * W_A + W_B + 2 * KV_W + 2 * W_C
ADAM_LR = 0.001
ADAM_B1 = 0.9
ADAM_B2 = 0.999
ADAM_EPS = 1e-08
ADAM_WD = 0.01
ADAM_STEP = 10
VMEM_LIMIT = 56 * 1024 * 1024

SDS = jax.ShapeDtypeStruct
ANY = pl.BlockSpec(memory_space=pl.ANY)


def _time_tile(s):
    return max(BLK, s // 8)


class _Exchange:
    def __init__(self, inputs, out_shapes, aliases, sem_shapes, start, wait):
        self.inputs, self.out_shapes, self.aliases, self.sem_shapes = inputs, out_shapes, aliases, sem_shapes
        self.start, self.wait = start, wait


def _my_pos():
    x, y, c = (lax.axis_index(a) for a in AXES)
    return x, y, c, 4 * x + 2 * y + c


def _flip(k):
    x, y, c, _ = _my_pos()
    return (1 - x if k & 4 else x, 1 - y if k & 2 else y, 1 - c if k & 1 else c)


def _slot(dev):
    return 4 * dev[0] + 2 * dev[1] + dev[2]


def _dev(p):
    return (p >> 2, (p >> 1) & 1, p & 1)


def _gather_a(items, two_level):
    rels = (1, 2, 4, 6) if two_level else tuple(range(1, N_DEV))
    items = [tuple(it) + (None,) * (4 - len(it)) for it in items]
    n = len(items)
    with_buf = [a for a in range(n) if items[a][3] is not None]

    def rows_of(ref, a):
        return ref if items[a][2] is None else ref.at[pl.ds(*items[a][2])]

    def src_of(ins, a):
        return rows_of(ins[a] if items[a][1] is None else ins[a].at[items[a][1]], a)

    def dst_of(outs, a, slot):
        return rows_of(outs[a].at[slot], a)

    def shape_of(a):
        arr, l = items[a][:2]
        return arr.shape if l is None else arr.shape[1:]

    def copies(ins, outs, sems, a):
        send, recv, _ = sems
        me = _my_pos()[3]
        return [(k, pltpu.make_async_remote_copy(
            src_ref=src_of(ins, a), dst_ref=dst_of(outs, a, me), send_sem=send.at[a, k], recv_sem=recv.at[a, k],
            device_id=_flip(k), device_id_type=MESH)) for k in rels]

    def local(ins, outs, sems, a):
        return pltpu.make_async_copy(src_of(ins, a), dst_of(outs, a, _my_pos()[3]), sems[2].at[a])

    def start(ins, outs, sems):
        for a in range(n):
            local(ins, outs, sems, a).start()
            for _, cp in copies(ins, outs, sems, a):
                cp.start()

    def wait(ins, outs, sems):
        send, recv, _ = sems
        for a in range(n):
            for k, cp in copies(ins, outs, sems, a):
                pltpu.make_async_remote_copy(
                    src_ref=src_of(ins, a), dst_ref=dst_of(outs, a, _slot(_flip(k))), send_sem=send.at[a, k],
                    recv_sem=recv.at[a, k], device_id=_flip(k), device_id_type=MESH).wait_recv()
                cp.wait_send()
            local(ins, outs, sems, a).wait()

    return _Exchange([it[0] for it in items] + [items[a][3] for a in with_buf],
                     [SDS((N_DEV,) + shape_of(a), items[a][0].dtype) for a in range(n)],
                     {n + j: a for j, a in enumerate(with_buf)},
                     [pltpu.SemaphoreType.DMA((n, N_DEV)), pltpu.SemaphoreType.DMA((n, N_DEV)),
                      pltpu.SemaphoreType.DMA((n,))], start, wait)


def _gather_b(bufs):
    n = len(bufs)

    def copies(ins, outs, sems, a, c_of_block):
        send, recv = sems
        x, y, c, _ = _my_pos()
        res = []
        for k in (2, 4, 6):
            chip = _flip(k)
            blk = _slot((chip[0], chip[1], c if c_of_block == "mine" else 1 - c))
            res.append(pltpu.make_async_remote_copy(
                src_ref=ins[a].at[blk], dst_ref=outs[a].at[blk], send_sem=send.at[a, k], recv_sem=recv.at[a, k],
                device_id=_flip(1), device_id_type=MESH))
        return res

    def start(ins, outs, sems):
        for a in range(n):
            for cp in copies(ins, outs, sems, a, "mine"):
                cp.start()

    def wait(ins, outs, sems):
        for a in range(n):
            for cp in copies(ins, outs, sems, a, "sibling"):
                cp.wait_recv()
            for cp in copies(ins, outs, sems, a, "mine"):
                cp.wait_send()

    return _Exchange(list(bufs), [SDS(b.shape, b.dtype) for b in bufs], {a: a for a in range(n)},
                     [pltpu.SemaphoreType.DMA((n, N_DEV)), pltpu.SemaphoreType.DMA((n, N_DEV))], start, wait)


def _grad_x(items, n_l):
    items = [tuple(it) + (None,) * (4 - len(it)) for it in items]
    n = len(items)
    owners = [tuple(range(N_DEV)) if it[3] is None else tuple(it[3]) for it in items]
    inputs, first_in, aliases, out_shapes = [], [], {}, []
    for a, (arrs, l, recv, _) in enumerate(items):
        first_in.append(len(inputs))
        inputs += list(arrs)
        assert sum(arr.shape[0] for arr in arrs) == N_DEV
        if recv is not None:
            aliases[len(inputs)] = a
            inputs.append(recv)
        out_shapes.append(SDS((N_DEV, n_l) + arrs[0].shape[1:], arrs[0].dtype))

    def slab(ins, a, p):
        off = 0
        for j, arr in enumerate(items[a][0]):
            if p < off + arr.shape[0]:
                return ins[first_in[a] + j].at[p - off]
            off += arr.shape[0]
        raise AssertionError

    def rdma(ins, outs, sems, a, p, src_dev):
        send, recv, _ = sems
        return pltpu.make_async_remote_copy(
            src_ref=slab(ins, a, p), dst_ref=outs[a].at[src_dev, items[a][1]], send_sem=send.at[a, p],
            recv_sem=recv.at[a, src_dev], device_id=_dev(p), device_id_type=MESH)

    def local(ins, outs, sems, a, p):
        return pltpu.make_async_copy(slab(ins, a, p), outs[a].at[p, items[a][1]], sems[2].at[a])

    def start(ins, outs, sems):
        me = _my_pos()[3]
        for p in range(N_DEV):
            mine = [a for a in range(n) if p in owners[a]]

            @pl.when(me != p)
            def _():
                for a in mine:
                    rdma(ins, outs, sems, a, p, me).start()

            @pl.when(me == p)
            def _():
                for a in mine:
                    local(ins, outs, sems, a, p).start()

    def wait(ins, outs, sems):
        me = _my_pos()[3]
        for a in range(n):
            i_own = functools.reduce(operator.or_, [me == q for q in owners[a]])
            for p in range(N_DEV):
                @pl.when((me != p) & i_own)
                def _():
                    rdma(ins, outs, sems, a, p, p).wait_recv()

                if p in owners[a]:
                    @pl.when(me != p)
                    def _():
                        rdma(ins, outs, sems, a, p, p).wait_send()

                    @pl.when(me == p)
                    def _():
                        local(ins, outs, sems, a, p).wait()

    return _Exchange(inputs, out_shapes, aliases,
                     [pltpu.SemaphoreType.DMA((n, N_DEV)), pltpu.SemaphoreType.DMA((n, N_DEV)),
                      pltpu.SemaphoreType.DMA((n,))], start, wait)


def _pcall(body, args, *, name, grid, in_specs, out_specs, out_shape, scratch_shapes=(), carry=(), body_aliases=None):
    n_in, n_out, n_scr = len(in_specs), len(out_specs), len(scratch_shapes)
    c_in = [len(e.inputs) for e in carry]
    c_out = [len(e.out_shapes) for e in carry]
    c_sem = [len(e.sem_shapes) for e in carry]
    aliases = dict(body_aliases or {})
    for j, e in enumerate(carry):
        for i_loc, o_loc in e.aliases.items():
            aliases[n_in + sum(c_in[:j]) + i_loc] = n_out + sum(c_out[:j]) + o_loc

    def wrapped(*refs):
        def take(counts, pos):
            groups = []
            for cnt in counts:
                groups.append(refs[pos:pos + cnt])
                pos += cnt
            return groups, pos

        (ins,), pos = take([n_in], 0)
        cins, pos = take(c_in, pos)
        (outs,), pos = take([n_out], pos)
        couts, pos = take(c_out, pos)
        (scr,), pos = take([n_scr], pos)
        csems, pos = take(c_sem, pos)
        if carry:
            ids = [pl.program_id(k) for k in range(len(grid))]
            first = functools.reduce(operator.and_, [i == 0 for i in ids])
            last = functools.reduce(operator.and_, [i == g - 1 for i, g in zip(ids, grid)])

            @pl.when(first)
            def _():
                for e, ci, co, cs in zip(carry, cins, couts, csems):
                    e.start(ci, co, cs)

        body(*ins, *outs, *scr)
        if carry:
            @pl.when(last)
            def _():
                for e, ci, co, cs in zip(carry, cins, couts, csems):
                    e.wait(ci, co, cs)

    res = pl.pallas_call(
        wrapped, name=name, grid=grid,
        in_specs=list(in_specs) + [ANY] * sum(c_in),
        out_specs=list(out_specs) + [ANY] * sum(c_out),
        out_shape=list(out_shape) + [s for e in carry for s in e.out_shapes],
        scratch_shapes=list(scratch_shapes) + [s for e in carry for s in e.sem_shapes],
        input_output_aliases=aliases,
        compiler_params=pltpu.CompilerParams(dimension_semantics=("arbitrary",) * len(grid),
                                             vmem_limit_bytes=VMEM_LIMIT),
    )(*args, *[a for e in carry for a in e.inputs])
    outs, pos, extra = list(res[:n_out]), n_out, []
    for cnt in c_out:
        extra.append(list(res[pos:pos + cnt]))
        pos += cnt
    return outs, extra


def _all_gather(items, name):
    n = len(items)
    shape_of = lambda a: items[a][0].shape if items[a][1] is None else items[a][0].shape[1:]

    def body(*refs):
        ins, outs, (send_sems, recv_sems, local_sems) = refs[:n], refs[n:2 * n], refs[2 * n:]
        x, y, c, me = _my_pos()
        src_of = lambda a: ins[a] if items[a][1] is None else ins[a].at[items[a][1]]

        def copy(a, k, block, to, src=None):
            dst = outs[a].at[_slot(block)]
            return pltpu.make_async_remote_copy(
                src_ref=dst if src is None else src, dst_ref=dst,
                send_sem=send_sems.at[a, k], recv_sem=recv_sems.at[a, k], device_id=to, device_id_type=MESH)

        mine = [pltpu.make_async_copy(src_of(a), outs[a].at[me], local_sems.at[a]) for a in range(n)]
        for cp in mine:
            cp.start()
        first = [copy(a, k, (x, y, c), _flip(k), src=src_of(a)) for a in range(n) for k in (1, 2, 4, 6)]
        for cp in first:
            cp.start()
        passed = []
        for k in (2, 4, 6):
            for a in range(n):
                copy(a, k, _flip(k), (x, y, c)).wait_recv()
                fwd = copy(a, k + 1, _flip(k), _flip(1))
                fwd.start()
                passed.append(fwd)
        for a in range(n):
            copy(a, 1, _flip(1), (x, y, c)).wait_recv()
            for k in (2, 4, 6):
                copy(a, k + 1, _flip(k + 1), (x, y, c)).wait_recv()
        for cp in first + passed:
            cp.wait_send()
        for cp in mine:
            cp.wait()

    return pl.pallas_call(
        body, name=name,
        in_specs=[ANY] * n, out_specs=[ANY] * n,
        out_shape=[SDS((N_DEV,) + shape_of(a), items[a][0].dtype) for a in range(n)],
        scratch_shapes=[pltpu.SemaphoreType.DMA((n, N_DEV)), pltpu.SemaphoreType.DMA((n, N_DEV)),
                        pltpu.SemaphoreType.DMA((n,))],
    )(*[it[0] for it in items])


def _mm(a, b):
    return jnp.dot(a.astype(BF16), b.astype(BF16), preferred_element_type=F32)


def _mm_nt(a, b):
    return lax.dot_general(a.astype(BF16), b.astype(BF16), (((1,), (1,)), ((), ())), preferred_element_type=F32)


def _mm_tn(a, b):
    return lax.dot_general(a.astype(BF16), b.astype(BF16), (((0,), (0,)), ((), ())), preferred_element_type=F32)


def _rms_r(x):
    return lax.rsqrt(jnp.mean(x * x, axis=-1, keepdims=True) + NORM_EPS)


def _rms_bwd(x, r, g, dy):
    gy = dy * g
    dx = r * (gy - x * (r * r) * jnp.mean(gy * x, axis=-1, keepdims=True))
    dg = jnp.sum(dy * x * r, axis=0, keepdims=True)
    return dx, dg


def _sigmoid(x):
    return 1.0 / (1.0 + jnp.exp(-x))


def _dsilu(z, sz):
    return sz * (1.0 + z * (1.0 - sz))


def _swiglu_bf16(g, u):
    sg = 0.5 * jnp.tanh(0.5 * g) + 0.5
    silu = g * sg
    return silu * u, silu, sg + silu * (1.0 - sg)


_GELU_C = math.sqrt(2.0 / math.pi)


def _gelu(x):
    t = jnp.tanh(_GELU_C * (x + 0.044715 * x * x * x))
    return 0.5 * x * (1.0 + t), t


def _dgelu(x, t):
    return 0.5 * (1.0 + t) + 0.5 * x * (1.0 - t * t) * _GELU_C * (1.0 + 3.0 * 0.044715 * x * x)


def _log1p(e):
    return jnp.where(e < 1e-2, e * (1.0 - e * (0.5 - e * (1.0 / 3.0))), jnp.log(1.0 + e))


def _softplus(x):
    return jnp.maximum(x, 0.0) + _log1p(jnp.exp(-jnp.abs(x)))


def _neg_expm1(x):
    small = -x * (1.0 + x * (0.5 + x * (1.0 / 6.0) * (1.0 + x * 0.25)))
    return jnp.where(x > -1e-2, small, 1.0 - jnp.exp(x))


def _shift_down(x, s):
    return x if s == 0 else pltpu.roll(x, s, 0)


def _shift_up(x, s):
    return x if s == 0 else pltpu.roll(x, x.shape[0] - s, 0)


def _ffn_wspecs(d, fc, order):
    f_of = (lambda i, f: f) if order == "tf" else (lambda f, i: f)
    n_f = N_DEV // 2
    return [pl.BlockSpec((None, fc, d), lambda *g: (f_of(*g), 0, 0)),
            pl.BlockSpec((None, fc, d), lambda *g: (f_of(*g) + n_f, 0, 0)),
            pl.BlockSpec((2, fc // 2, d), lambda *g: (f_of(*g), 0, 0))]


def _ffn_fwd(x, pre_g, post_g, wgu_t, wd, name, carry=()):
    s, d = x.shape
    fc = wgu_t.shape[1]
    ts = 2 * _time_tile(s)
    n_t, n_f = s // ts, N_DEV // 2

    def body(x_ref, pg_ref, qg_ref, wg_ref, wu_ref, wd_ref, xo_ref, h_ref, g_ref, u_ref, d_ref, h_scr, acc):
        f = pl.program_id(1)

        @pl.when(f == 0)
        def _():
            xv = x_ref[...]
            hv = (xv * _rms_r(xv) * pg_ref[...]).astype(BF16)
            h_scr[...] = hv
            h_ref[...] = hv
            acc[...] = jnp.zeros_like(acc)

        hv = h_scr[...]
        g = _mm_nt(hv, wg_ref[...])
        u = _mm_nt(hv, wu_ref[...])
        g = g.astype(BF16)
        u = u.astype(BF16)
        g_ref[...] = g
        u_ref[...] = u
        acc[...] += jnp.dot(_swiglu_bf16(g, u)[0], wd_ref[...].reshape(fc, d), preferred_element_type=F32)

        @pl.when(f == n_f - 1)
        def _():
            dv = acc[...]
            d_ref[...] = dv.astype(BF16)
            xo_ref[...] = x_ref[...] + 0.5 * (dv * _rms_r(dv) * qg_ref[...])

    row = pl.BlockSpec((ts, d), lambda i, f: (i, 0))
    vec = pl.BlockSpec((1, d), lambda i, f: (0, 0))
    act = pl.BlockSpec((None, ts, fc), lambda i, f: (f, i, 0))
    return _pcall(
        body, (x, pre_g, post_g, wgu_t, wgu_t, wd), name=name, grid=(n_t, n_f),
        in_specs=[row, vec, vec] + _ffn_wspecs(d, fc, "tf"),
        out_specs=[row, row, act, act, row],
        out_shape=[SDS((s, d), F32), SDS((s, d), BF16), SDS((n_f, s, fc), BF16), SDS((n_f, s, fc), BF16),
                   SDS((s, d), BF16)],
        scratch_shapes=[pltpu.VMEM((ts, d), BF16), pltpu.VMEM((ts, d), F32)], carry=carry)


def _ffn_bwd_act(dxo, dmid, x, pre_g, post_g, g_s, u_s, wgu_t, wd, name, carry=()):
    s, d = x.shape
    fc = wgu_t.shape[1]
    ts = _time_tile(s)
    n_t, n_f = s // ts, N_DEV // 2

    def body(dxo_ref, dm_ref, x_ref, pg_ref, qg_ref, g_ref, u_ref, wg_ref, wu_ref, wd_ref,
             dx_ref, dd_ref, dg_ref, du_ref, dpg_ref, dqg_ref, dd_scr, dh_acc):
        i, f = pl.program_id(0), pl.program_id(1)

        @pl.when((i == 0) & (f == 0))
        def _():
            dpg_ref[...] = jnp.zeros_like(dpg_ref)
            dqg_ref[...] = jnp.zeros_like(dqg_ref)

        @pl.when(f == 0)
        def _():
            dv = dm_ref[...].astype(F32)
            ddv, dq = _rms_bwd(dv, _rms_r(dv), qg_ref[...], 0.5 * dxo_ref[...])
            dqg_ref[...] += dq
            dd_scr[...] = ddv.astype(BF16)
            dd_ref[...] = ddv.astype(BF16)
            dh_acc[...] = jnp.zeros_like(dh_acc)

        da = _mm_nt(dd_scr[...], wd_ref[...].reshape(fc, d)).astype(BF16)
        u = u_ref[...]
        _, silu, dsilu = _swiglu_bf16(g_ref[...], u)
        du = da * silu
        dg = da * u * dsilu
        dg_ref[...] = dg
        du_ref[...] = du
        dh_acc[...] += _mm(dg, wg_ref[...]) + _mm(du, wu_ref[...])

        @pl.when(f == n_f - 1)
        def _():
            xv = x_ref[...]
            dxv, dp = _rms_bwd(xv, _rms_r(xv), pg_ref[...], dh_acc[...])
            dpg_ref[...] += dp
            dx_ref[...] = dxo_ref[...] + dxv

    row = pl.BlockSpec((ts, d), lambda i, f: (i, 0))
    vec = pl.BlockSpec((1, d), lambda i, f: (0, 0))
    act = pl.BlockSpec((None, ts, fc), lambda i, f: (f, i, 0))
    return _pcall(
        body, (dxo, dmid, x, pre_g, post_g, g_s, u_s, wgu_t, wgu_t, wd), name=name, grid=(n_t, n_f),
        in_specs=[row, row, row, vec, vec, act, act] + _ffn_wspecs(d, fc, "tf"),
        out_specs=[row, row, act, act, vec, vec],
        out_shape=[SDS((s, d), F32), SDS((s, d), BF16), SDS((n_f, s, fc), BF16), SDS((n_f, s, fc), BF16),
                   SDS((1, d), F32), SDS((1, d), F32)],
        scratch_shapes=[pltpu.VMEM((ts, d), BF16), pltpu.VMEM((ts, d), F32)], carry=carry)


def _ffn_bwd_w(h, dd, g_s, u_s, dg, du, name, carry=()):
    s, d = h.shape
    n_f, _, fc = g_s.shape
    ts = 2 * _time_tile(s)
    n_t = s // ts

    def body(h_ref, dd_ref, g_ref, u_ref, dg_ref, du_ref, wg_ref, wu_ref, wd_ref, acc_g, acc_u, acc_d):
        i = pl.program_id(1)

        @pl.when(i == 0)
        def _():
            acc_g[...] = jnp.zeros_like(acc_g)
            acc_u[...] = jnp.zeros_like(acc_u)
            acc_d[...] = jnp.zeros_like(acc_d)

        a = _swiglu_bf16(g_ref[...], u_ref[...])[0]
        hv = h_ref[...]
        acc_g[...] += _mm_tn(dg_ref[...], hv)
        acc_u[...] += _mm_tn(du_ref[...], hv)
        acc_d[...] += _mm_tn(a, dd_ref[...])

        @pl.when(i == n_t - 1)
        def _():
            wg_ref[...] = acc_g[...].astype(BF16)
            wu_ref[...] = acc_u[...].astype(BF16)
            wd_ref[...] = acc_d[...].astype(BF16)

    row = pl.BlockSpec((ts, d), lambda f, i: (i, 0))
    act = pl.BlockSpec((None, ts, fc), lambda f, i: (f, i, 0))
    out = pl.BlockSpec((None, fc, d), lambda f, i: (f, 0, 0))
    return _pcall(
        body, (h, dd, g_s, u_s, dg, du), name=name, grid=(n_f, n_t),
        in_specs=[row, row, act, act, act, act], out_specs=[out, out, out],
        out_shape=[SDS((n_f, fc, d), BF16)] * 3,
        scratch_shapes=[pltpu.VMEM((fc, d), F32)] * 3, carry=carry)


def _ffn_bwd_w_send(h, dd, g_s, u_s, dg, du, recv_gu, recv_d, layer, name, carry=()):
    s, d = h.shape
    n_f, _, fc = g_s.shape
    ts = _time_tile(s)
    n_t = s // ts
    half = fc // 2

    def chunk_of(step):
        return (step + 2 * lax.axis_index("x") + lax.axis_index("y")) % n_f

    def body(h_ref, dd_ref, g_ref, u_ref, dg_ref, du_ref, _rgu_in, _rd_in, rgu_ref, rd_ref,
             acc_g, acc_u, acc_d, st_g, st_u, st_d, pair_gu, pair_d, zeros,
             send_sems, recv_sems, local_sems, pair_sems, zero_sems):
        f, i = pl.program_id(0), pl.program_id(1)
        x, y, c_me, me = _my_pos()
        sibling = me ^ 1

        @pl.when(i == 0)
        def _():
            acc_g[...] = jnp.zeros_like(acc_g)
            acc_u[...] = jnp.zeros_like(acc_u)
            acc_d[...] = jnp.zeros_like(acc_d)

        def zero_fills():
            res = []
            for n_k, k in enumerate((2, 4, 6)):
                other = _flip(k)
                slot = _slot((other[0], other[1], 1 - c_me))
                res += [pltpu.make_async_copy(zeros, rgu_ref.at[slot, layer, pl.ds(0, half)], zero_sems.at[n_k, 0]),
                        pltpu.make_async_copy(zeros, rgu_ref.at[slot, layer, pl.ds(half, half)], zero_sems.at[n_k, 1]),
                        pltpu.make_async_copy(zeros, rd_ref.at[slot, layer], zero_sems.at[n_k, 2])]
            return res

        @pl.when((f == 0) & (i == 0))
        def _():
            zeros[...] = jnp.zeros_like(zeros)
            for cp in zero_fills():
                cp.start()

        a = _swiglu_bf16(g_ref[...], u_ref[...])[0]
        hv = h_ref[...]
        acc_g[...] += _mm_tn(dg_ref[...], hv)
        acc_u[...] += _mm_tn(du_ref[...], hv)
        acc_d[...] += _mm_tn(a, dd_ref[...])

        def messages(fs):
            c = chunk_of(fs)
            lo, hi = pl.ds(0, half), pl.ds(half, half)
            return [(st_g.at[fs], pair_gu.at[fs // 2, 0], rgu_ref, 0, c, 0),
                    (st_u.at[fs], pair_gu.at[fs // 2, 1], rgu_ref, 0, c + n_f, 1),
                    (st_d.at[fs, lo], pair_d.at[fs], rd_ref, 1, 2 * c, 2),
                    (st_d.at[fs, hi], pair_d.at[fs], rd_ref, 1, 2 * c + 1, 3)]

        def roles(p):
            same_chip = (p >> 1) == (me >> 1)
            same_c = (p & 1) == c_me
            return p == me, p == sibling, (~same_chip) & same_c, (~same_chip) & (~same_c)

        def to_owner(fs, msg, src_dev):
            src, _, buf, row, p, j = msg
            return pltpu.make_async_remote_copy(
                src_ref=src, dst_ref=buf.at[src_dev, layer], send_sem=send_sems.at[fs, j],
                recv_sem=recv_sems.at[row, src_dev], device_id=_dev(p), device_id_type=MESH)

        def to_pair(fs, msg):
            src, pair, _, _, _, j = msg
            return pltpu.make_async_remote_copy(
                src_ref=src, dst_ref=pair, send_sem=send_sems.at[fs, j], recv_sem=pair_sems.at[fs, j],
                device_id=_dev(sibling), device_id_type=MESH)

        def local(fs, msg):
            src, _, buf, _, p, j = msg
            return pltpu.make_async_copy(src, buf.at[p, layer], local_sems.at[fs, j])

        for fs in range(n_f):
            @pl.when((f == fs) & (i == n_t - 1))
            def _():
                st_g[fs] = acc_g[...].astype(BF16)
                st_u[fs] = acc_u[...].astype(BF16)
                st_d[fs] = acc_d[...].astype(BF16)
                msgs = messages(fs)
                for msg in msgs:
                    mine, sib, _, hand_over = roles(msg[4])

                    @pl.when(mine)
                    def _():
                        local(fs, msg).start()

                    @pl.when(sib)
                    def _():
                        to_owner(fs, msg, me).start()

                    @pl.when(hand_over)
                    def _():
                        to_pair(fs, msg).start()
                for msg in msgs:
                    @pl.when(roles(msg[4])[2])
                    def _():
                        src, pair = msg[0], msg[1]
                        to_pair(fs, msg).wait_recv()
                        src[...] = (src[...].astype(F32) + pair[...].astype(F32)).astype(BF16)
                        to_owner(fs, msg, me).start()

        @pl.when((f == n_f - 1) & (i == n_t - 1))
        def _():
            for fs in range(n_f):
                for msg in messages(fs):
                    mine = roles(msg[4])[0]

                    @pl.when(mine)
                    def _():
                        local(fs, msg).wait()

                    @pl.when(~mine)
                    def _():
                        to_owner(fs, msg, me).wait_send()
            for k in (1, 2, 4, 6):
                src_dev = _slot(_flip(k))
                to_owner(0, messages(0)[0], src_dev).wait_recv()
                to_owner(0, messages(0)[2], src_dev).wait_recv()
            for cp in zero_fills():
                cp.wait()

    row = pl.BlockSpec((ts, d), lambda f, i: (i, 0))
    act = pl.BlockSpec((None, ts, fc), lambda f, i: (chunk_of(f), i, 0))
    return _pcall(
        body, (h, dd, g_s, u_s, dg, du, recv_gu, recv_d), name=name, grid=(n_f, n_t),
        in_specs=[row, row, act, act, act, act, ANY, ANY], out_specs=[ANY, ANY],
        out_shape=[SDS(recv_gu.shape, recv_gu.dtype), SDS(recv_d.shape, recv_d.dtype)],
        scratch_shapes=[pltpu.VMEM((fc, d), F32)] * 3 + [pltpu.VMEM((n_f, fc, d), BF16)] * 3
        + [pltpu.VMEM((n_f // 2, 2, fc, d), BF16), pltpu.VMEM((n_f, half, d), BF16), pltpu.VMEM((half, d), BF16)]
        + [pltpu.SemaphoreType.DMA((n_f, 4)), pltpu.SemaphoreType.DMA((2, N_DEV)), pltpu.SemaphoreType.DMA((n_f, 4)),
           pltpu.SemaphoreType.DMA((n_f, 4)), pltpu.SemaphoreType.DMA((3, 3))],
        carry=carry, body_aliases={6: 0, 7: 1})


_PROJ_WIDTHS = (W_A, W_A, W_B, KV_W, KV_W, 2 * W_C)


def _mix_in_fwd(x, pre_g, w_in_t, name, carry=()):
    s, d = x.shape
    ts = _time_tile(s)

    def body(x_ref, pg_ref, w_ref, hn_ref, *outs):
        xv = x_ref[...]
        hn = (xv * _rms_r(xv) * pg_ref[...]).astype(BF16)
        hn_ref[...] = hn
        proj = _mm_nt(hn, w_ref[...])
        off = 0
        for o_ref, w in zip(outs, _PROJ_WIDTHS):
            o_ref[...] = proj[:, off:off + w]
            off += w

    row = lambda w: pl.BlockSpec((ts, w), lambda i: (i, 0))
    return _pcall(
        body, (x, pre_g, w_in_t), name=name, grid=(s // ts,),
        in_specs=[row(d), pl.BlockSpec((1, d), lambda i: (0, 0)), pl.BlockSpec((D_IN_PROJ, d), lambda i: (0, 0))],
        out_specs=[row(d)] + [row(w) for w in _PROJ_WIDTHS],
        out_shape=[SDS((s, d), BF16)] + [SDS((s, w), F32) for w in _PROJ_WIDTHS], carry=carry)


def _mix_in_bwd(dres, x, pre_g, hn, w_in_t, dlx, dlg, dq, dk, dk_up, dv, dv_up, dglu, name, carry=()):
    s, d = x.shape
    ts = _time_tile(s)
    n_t = s // ts

    def body(dres_ref, x_ref, pg_ref, hn_ref, w_ref, dlx_ref, dlg_ref, dq_ref, dk_ref, dkn_ref,
             dv_ref, dvn_ref, dglu_ref, dx_ref, dw_ref, dpg_ref, acc):
        i = pl.program_id(0)

        @pl.when(i == 0)
        def _():
            acc[...] = jnp.zeros_like(acc)
            dpg_ref[...] = jnp.zeros_like(dpg_ref)

        def with_next(cur_ref, nxt_ref):
            nxt = jnp.where(i < n_t - 1, nxt_ref[...], 0.0)
            if ts == BLK:
                return cur_ref[...] + nxt
            return jnp.concatenate([cur_ref[:ts - BLK, :], cur_ref[ts - BLK:, :] + nxt], axis=0)

        dproj = jnp.concatenate([dlx_ref[...], dlg_ref[...], dq_ref[...], with_next(dk_ref, dkn_ref),
                                 with_next(dv_ref, dvn_ref), dglu_ref[...]], axis=1).astype(BF16)
        dhn = _mm(dproj, w_ref[...])
        acc[...] += _mm_tn(dproj, hn_ref[...])
        xv = x_ref[...]
        dxv, dp = _rms_bwd(xv, _rms_r(xv), pg_ref[...], dhn)
        dpg_ref[...] += dp
        dx_ref[...] = dres_ref[...] + dxv

        @pl.when(i == n_t - 1)
        def _():
            dw_ref[...] = acc[...].astype(BF16)

    row = lambda w: pl.BlockSpec((ts, w), lambda i: (i, 0))
    nxt = pl.BlockSpec((BLK, KV_W), lambda i: (jnp.minimum(i + 1, n_t - 1), 0))
    vec = pl.BlockSpec((1, d), lambda i: (0, 0))
    full = pl.BlockSpec((D_IN_PROJ, d), lambda i: (0, 0))
    return _pcall(
        body, (dres, x, pre_g, hn, w_in_t, dlx, dlg, dq, dk, dk_up, dv, dv_up, dglu), name=name, grid=(n_t,),
        in_specs=[row(d), row(d), vec, row(d), full, row(W_A), row(W_A), row(W_B), row(KV_W), nxt,
                  row(KV_W), nxt, row(2 * W_C)],
        out_specs=[row(d), full, vec],
        out_shape=[SDS((s, d), F32), SDS((D_IN_PROJ, d), BF16), SDS((1, d), F32)],
        scratch_shapes=[pltpu.VMEM((D_IN_PROJ, d), F32)], carry=carry)


def _lru_gates(xc, lru_p):
    cw_ref, cb_ref, wa_ref, ba_ref, wx_ref, bx_ref, lam_ref = lru_p
    c = cb_ref[...]
    for j in range(LRU_K):
        c = c + cw_ref[j:j + 1, :] * _shift_down(xc, LRU_K - 1 - j)[LRU_HALO:, :]
    r = _sigmoid(_mm(c, wa_ref[...]) + ba_ref[...])
    ig = _sigmoid(_mm(c, wx_ref[...]) + bx_ref[...])
    sp = _softplus(-lam_ref[...])
    log_a = -LRU_C * r * sp
    a = jnp.exp(log_a)
    m = jnp.sqrt(_neg_expm1(2.0 * log_a))
    return c, r, ig, sp, a, m


def _lru_pspecs():
    small = lambda r: pl.BlockSpec((r, W_A), lambda i: (0, 0))
    return [small(LRU_K), small(1), small(W_A), small(1), small(W_A), small(1), small(1)]


def _lru_fwd(lx, lg, lru_p, name, carry=()):
    s = lx.shape[0]
    ts = _time_tile(s)
    n8 = ts // LRU_HALO

    def body(lx_ref, lxp_ref, lg_ref, *rest):
        lru_p, (ya_ref, h_ref, hcarry) = rest[:7], rest[7:]
        i = pl.program_id(0)
        prev = jnp.where(i > 0, lxp_ref[...], 0.0)
        xc = jnp.concatenate([prev, lx_ref[...]], axis=0)
        c, r, ig, sp, a, m = _lru_gates(xc, lru_p)
        acc_a, acc_b = a, m * (ig * c)
        t = lax.broadcasted_iota(jnp.int32, a.shape, 0)
        k = 1
        while k < ts:
            keep = t >= k
            acc_b = jnp.where(keep, acc_a * _shift_down(acc_b, k) + acc_b, acc_b)
            acc_a = jnp.where(keep, acc_a * _shift_down(acc_a, k), acc_a)
            k *= 2
        h0 = jnp.where(i > 0, hcarry[...], 0.0)
        h = acc_b + acc_a * h0
        hcarry[...] = h[ts - 1:ts, :]
        h_ref[...] = h
        ya_ref[...] = _gelu(lg_ref[...])[0] * h

    row = pl.BlockSpec((ts, W_A), lambda i: (i, 0))
    prev8 = pl.BlockSpec((LRU_HALO, W_A), lambda i: (jnp.maximum(i * n8 - 1, 0), 0))
    return _pcall(
        body, (lx, lx, lg, *lru_p), name=name, grid=(s // ts,),
        in_specs=[row, prev8, row] + _lru_pspecs(), out_specs=[row, row],
        out_shape=[SDS((s, W_A), F32), SDS((s, W_A), F32)],
        scratch_shapes=[pltpu.VMEM((1, W_A), F32)], carry=carry)


def _lru_bwd(dya, lx, lg, h_s, lru_p, name, carry=()):
    s = lx.shape[0]
    ts = _time_tile(s)
    n_t = s // ts
    n8 = ts // LRU_HALO

    def body(dya_ref, lx_ref, lxp_ref, lg_ref, h_ref, hp_ref, *rest):
        lru_p = rest[:7]
        (dlx_ref, dlg_ref, dcw_ref, dcb_ref, dwa_ref, dba_ref, dwx_ref, dbx_ref, dlam_ref,
         carry_a, carry_l, carry_dc) = rest[7:]
        cw_ref, _, wa_ref, _, wx_ref, _, lam_ref = lru_p
        i = pl.program_id(0)
        first_tile = i == n_t - 1
        last_tile = i == 0

        @pl.when(i == 0)
        def _():
            for ref in (dcw_ref, dcb_ref, dwa_ref, dba_ref, dwx_ref, dbx_ref, dlam_ref):
                ref[...] = jnp.zeros_like(ref)

        prev = jnp.where(first_tile, 0.0, lxp_ref[...])
        xc = jnp.concatenate([prev, lx_ref[...]], axis=0)
        c, r, ig, sp, a, m = _lru_gates(xc, lru_p)
        h = h_ref[...]
        hcat = jnp.concatenate([jnp.where(first_tile, 0.0, hp_ref[...]), h], axis=0)
        h_m1 = _shift_down(hcat, 1)[LRU_HALO:, :]
        lg = lg_ref[...]
        ge, th = _gelu(lg)
        dya = dya_ref[...]
        dlg_ref[...] = dya * h * _dgelu(lg, th)
        dh = dya * ge
        t = lax.broadcasted_iota(jnp.int32, a.shape, 0)
        a_next = jnp.where(t < ts - 1, _shift_up(a, 1), jnp.where(last_tile, 0.0, carry_a[...]))
        acc_a, acc_b = a_next, dh
        k = 1
        while k < ts:
            keep = t < ts - k
            acc_b = jnp.where(keep, acc_a * _shift_up(acc_b, k) + acc_b, acc_b)
            acc_a = jnp.where(keep, acc_a * _shift_up(acc_a, k), acc_a)
            k *= 2
        lam_beyond = jnp.where(last_tile, 0.0, carry_l[...])
        lmb = acc_b + acc_a * lam_beyond
        carry_a[...] = a[0:1, :]
        carry_l[...] = lmb[0:1, :]
        gi = ig * c
        dgi = lmb * m
        dla = lmb * h_m1 * a - (lmb * gi) * (a * a) / m
        dr = dla * (-LRU_C * sp)
        dsp = jnp.sum(dla * (-LRU_C * r), axis=0, keepdims=True)
        dlam_ref[...] += -dsp * _sigmoid(-lam_ref[...])
        dra = dr * r * (1.0 - r)
        dia = dgi * c * ig * (1.0 - ig)
        dc = dgi * ig + _mm_nt(dra, wa_ref[...]) + _mm_nt(dia, wx_ref[...])
        dwa_ref[...] += _mm_tn(c, dra)
        dwx_ref[...] += _mm_tn(c, dia)
        dba_ref[...] += jnp.sum(dra, axis=0, keepdims=True)
        dbx_ref[...] += jnp.sum(dia, axis=0, keepdims=True)
        dcb_ref[...] += jnp.sum(dc, axis=0, keepdims=True)
        dcc = jnp.concatenate([dc, jnp.where(last_tile, 0.0, carry_dc[...])], axis=0)
        carry_dc[...] = dc[0:LRU_HALO, :]
        dlx = jnp.zeros_like(dc)
        for j in range(LRU_K):
            sh = LRU_K - 1 - j
            dcw_ref[j:j + 1, :] += jnp.sum(dc * _shift_down(xc, sh)[LRU_HALO:, :], axis=0, keepdims=True)
            dlx = dlx + cw_ref[j:j + 1, :] * _shift_up(dcc, sh)[:ts, :]
        dlx_ref[...] = dlx

    row = pl.BlockSpec((ts, W_A), lambda i: (n_t - 1 - i, 0))
    prev8 = pl.BlockSpec((LRU_HALO, W_A), lambda i: (jnp.maximum((n_t - 1 - i) * n8 - 1, 0), 0))
    small = lambda r: pl.BlockSpec((r, W_A), lambda i: (0, 0))
    return _pcall(
        body, (dya, lx, lx, lg, h_s, h_s, *lru_p), name=name, grid=(n_t,),
        in_specs=[row, row, prev8, row, row, prev8] + _lru_pspecs(),
        out_specs=[row, row, small(LRU_K), small(1), small(W_A), small(1), small(W_A), small(1), small(1)],
        out_shape=[SDS((s, W_A), F32), SDS((s, W_A), F32), SDS((LRU_K, W_A), F32), SDS((1, W_A), F32),
                   SDS((W_A, W_A), F32), SDS((1, W_A), F32), SDS((W_A, W_A), F32), SDS((1, W_A), F32),
                   SDS((1, W_A), F32)],
        scratch_shapes=[pltpu.VMEM((1, W_A), F32), pltpu.VMEM((1, W_A), F32), pltpu.VMEM((LRU_HALO, W_A), F32)],
        carry=carry)


_ATT_ROWS = N_Q_HEADS * BLK
_GRP_ROWS = Q_PER_KV * BLK


def _attn_stack(ref, rows, g):
    return jnp.concatenate([ref[rows, h * HEAD_DIM:(h + 1) * HEAD_DIM]
                            for h in range(g * Q_PER_KV, (g + 1) * Q_PER_KV)], axis=0)


def _attn_unstack(parts):
    return jnp.concatenate([p[j * BLK:(j + 1) * BLK, :] for p in parts for j in range(Q_PER_KV)], axis=1)


def _grp(x, g):
    return x[:, g * _GRP_ROWS:(g + 1) * _GRP_ROWS]


def _attn_block(q_ref, k_ref, kp_ref, v_ref, vp_ref, sink_row, i, b):
    rows, prev = slice(b * BLK, (b + 1) * BLK), slice((b - 1) * BLK, b * BLK)
    qs, kcs, kps, vcs, vps = [], [], [], [], []
    for g in range(N_KV_HEADS):
        cols = slice(g * HEAD_DIM, (g + 1) * HEAD_DIM)
        qs.append(_attn_stack(q_ref, rows, g))
        kcs.append(k_ref[rows, cols])
        vcs.append(v_ref[rows, cols])
        kps.append(kp_ref[:, cols] if b == 0 else k_ref[prev, cols])
        vps.append(vp_ref[:, cols] if b == 0 else v_ref[prev, cols])
    scale = 1.0 / math.sqrt(HEAD_DIM)
    sc = jnp.concatenate([_mm_nt(kcs[g], qs[g]) for g in range(N_KV_HEADS)], axis=1) * scale
    sp = jnp.concatenate([_mm_nt(kps[g], qs[g]) for g in range(N_KV_HEADS)], axis=1) * scale
    kj = lax.broadcasted_iota(jnp.int32, (BLK, _ATT_ROWS), 0)
    qi = lax.broadcasted_iota(jnp.int32, (BLK, _ATT_ROWS), 1) & (BLK - 1)
    sc = jnp.where(kj <= qi, sc, NEG_BIG)
    sp = jnp.where((kj > qi) if b > 0 else ((kj > qi) & (i > 0)), sp, NEG_BIG)
    m = jnp.maximum(jnp.maximum(jnp.max(sc, axis=0, keepdims=True), jnp.max(sp, axis=0, keepdims=True)), sink_row)
    pc = jnp.exp(sc - m)
    pp = jnp.exp(sp - m)
    es = jnp.exp(sink_row - m)
    inv = 1.0 / (jnp.sum(pc, axis=0, keepdims=True) + jnp.sum(pp, axis=0, keepdims=True) + es)
    return qs, kcs, kps, vcs, vps, pc * inv, pp * inv, es * inv


def _attn_specs(s, ts):
    bpt = ts // BLK
    tile = lambda w: pl.BlockSpec((ts, w), lambda i: (i, 0))
    prv = pl.BlockSpec((BLK, KV_W), lambda i: (jnp.maximum(i * bpt - 1, 0), 0))
    sink = pl.BlockSpec((1, _ATT_ROWS), lambda i: (0, 0))
    return bpt, tile, prv, sink


def _attn_fwd(q, k, v, sink_row, name, carry=()):
    s = q.shape[0]
    ts = _time_tile(s)
    bpt, tile, prv, sink = _attn_specs(s, ts)

    def body(q_ref, k_ref, kp_ref, v_ref, vp_ref, sk_ref, y_ref):
        i = pl.program_id(0)
        for b in range(bpt):
            _, _, _, vcs, vps, pc, pp, _ = _attn_block(q_ref, k_ref, kp_ref, v_ref, vp_ref, sk_ref[...], i, b)
            outs = [_mm_tn(_grp(pc, g), vcs[g]) + _mm_tn(_grp(pp, g), vps[g]) for g in range(N_KV_HEADS)]
            y_ref[b * BLK:(b + 1) * BLK, :] = _attn_unstack(outs)

    return _pcall(
        body, (q, k, k, v, v, sink_row), name=name, grid=(s // ts,),
        in_specs=[tile(W_B), tile(KV_W), prv, tile(KV_W), prv, sink],
        out_specs=[tile(W_B)], out_shape=[SDS((s, W_B), F32)], carry=carry)


def _attn_bwd(dy, q, k, v, sinks, name, carry=()):
    s = q.shape[0]
    ts = _time_tile(s)
    n_t = s // ts
    bpt, tile, prv, sink = _attn_specs(s, ts)

    def body(dy_ref, q_ref, k_ref, kp_ref, v_ref, vp_ref, sk_ref, dq_ref, dk_ref, dv_ref, dku_ref, dvu_ref, dsk_ref):
        i = pl.program_id(0)

        @pl.when(i == 0)
        def _():
            dsk_ref[...] = jnp.zeros_like(dsk_ref)

        scale = 1.0 / math.sqrt(HEAD_DIM)
        groups = range(N_KV_HEADS)
        head_row = lax.broadcasted_iota(jnp.int32, (N_Q_HEADS, BLK), 0)
        dsk = jnp.zeros((N_Q_HEADS, BLK), F32)
        dk_blocks, dv_blocks = [], []
        for b in range(bpt):
            rows = slice(b * BLK, (b + 1) * BLK)
            qs, kcs, kps, vcs, vps, pc, pp, ps = _attn_block(q_ref, k_ref, kp_ref, v_ref, vp_ref, sk_ref[...], i, b)
            dos = [_attn_stack(dy_ref, rows, g) for g in groups]
            dpc = jnp.concatenate([_mm_nt(vcs[g], dos[g]) for g in groups], axis=1)
            dpp = jnp.concatenate([_mm_nt(vps[g], dos[g]) for g in groups], axis=1)
            delta = jnp.sum(pc * dpc, axis=0, keepdims=True) + jnp.sum(pp * dpp, axis=0, keepdims=True)
            dsc = pc * (dpc - delta) * scale
            dsp = pp * (dpp - delta) * scale
            dq_ref[rows, :] = _attn_unstack([_mm_tn(_grp(dsc, g), kcs[g]) + _mm_tn(_grp(dsp, g), kps[g])
                                             for g in groups])
            dk_blocks.append(jnp.concatenate([_mm(_grp(dsc, g), qs[g]) for g in groups], axis=1))
            dv_blocks.append(jnp.concatenate([_mm(_grp(pc, g), dos[g]) for g in groups], axis=1))
            dkp = jnp.concatenate([_mm(_grp(dsp, g), qs[g]) for g in groups], axis=1)
            dvp = jnp.concatenate([_mm(_grp(pp, g), dos[g]) for g in groups], axis=1)
            if b == 0:
                dku_ref[...] = dkp
                dvu_ref[...] = dvp
            else:
                dk_blocks[b - 1] = dk_blocks[b - 1] + dkp
                dv_blocks[b - 1] = dv_blocks[b - 1] + dvp
            dsink = -ps * delta
            for h in range(N_Q_HEADS):
                dsk = dsk + jnp.where(head_row == h, jnp.sum(dsink[:, h * BLK:(h + 1) * BLK], axis=1, keepdims=True), 0.0)
        for b in range(bpt):
            dk_ref[b * BLK:(b + 1) * BLK, :] = dk_blocks[b]
            dv_ref[b * BLK:(b + 1) * BLK, :] = dv_blocks[b]
        dsk_ref[...] += dsk

    up = pl.BlockSpec((BLK, KV_W), lambda i: (i, 0))
    return _pcall(
        body, (dy, q, k, k, v, v, sinks), name=name, grid=(n_t,),
        in_specs=[tile(W_B), tile(W_B), tile(KV_W), prv, tile(KV_W), prv, sink],
        out_specs=[tile(W_B), tile(KV_W), tile(KV_W), up, up, pl.BlockSpec((N_Q_HEADS, BLK), lambda i: (0, 0))],
        out_shape=[SDS((s, W_B), F32), SDS((s, KV_W), F32), SDS((s, KV_W), F32), SDS((n_t * BLK, KV_W), F32),
                   SDS((n_t * BLK, KV_W), F32), SDS((N_Q_HEADS, BLK), F32)], carry=carry)


def _cc_recompute(glu_ref, glup_ref, cw_ref, cb_ref, first_tile):
    prev = jnp.where(first_tile, 0.0, glup_ref[...])
    ge = jnp.concatenate([prev, glu_ref[...]], axis=0)
    y0 = ge[:, :W_C] * _sigmoid(ge[:, W_C:])
    y1 = cb_ref[...]
    for j in range(CC_K):
        y1 = y1 + cw_ref[j:j + 1, :] * _shift_down(y0, CC_K - 1 - j)[CC_HALO:, :]
    return y0, y1


def _ln_stats(y1):
    mu = jnp.mean(y1, axis=-1, keepdims=True)
    xc = y1 - mu
    rstd = lax.rsqrt(jnp.mean(xc * xc, axis=-1, keepdims=True) + LN_EPS)
    return xc * rstd, rstd


def _cc_specs(s, ts):
    n32 = ts // CC_HALO
    row = lambda w: pl.BlockSpec((ts, w), lambda i: (i, 0))
    prev = pl.BlockSpec((CC_HALO, 2 * W_C), lambda i: (jnp.maximum(i * n32 - 1, 0), 0))
    small = lambda r: pl.BlockSpec((r, W_C), lambda i: (0, 0))
    return row, prev, small


def _cc_fwd(glu, cw, cb, lng, lnb, name, carry=()):
    s = glu.shape[0]
    ts = _time_tile(s)
    row, prev, small = _cc_specs(s, ts)

    def body(glu_ref, glup_ref, cw_ref, cb_ref, lng_ref, lnb_ref, y_ref):
        _, y1 = _cc_recompute(glu_ref, glup_ref, cw_ref, cb_ref, pl.program_id(0) == 0)
        xhat, _ = _ln_stats(y1)
        z = xhat * lng_ref[...] + lnb_ref[...]
        y_ref[...] = z * _sigmoid(z)

    return _pcall(
        body, (glu, glu, cw, cb, lng, lnb), name=name, grid=(s // ts,),
        in_specs=[row(2 * W_C), prev, small(CC_HALO), small(1), small(1), small(1)],
        out_specs=[row(W_C)], out_shape=[SDS((s, W_C), F32)], carry=carry)


def _cc_bwd_conv(dy, glu, cw, cb, lng, lnb, name, carry=()):
    s = glu.shape[0]
    ts = _time_tile(s)
    row, prev, small = _cc_specs(s, ts)

    def body(dy_ref, glu_ref, glup_ref, cw_ref, cb_ref, lng_ref, lnb_ref, dy1_ref, dcw_ref, dcb_ref, dlng_ref, dlnb_ref):
        i = pl.program_id(0)

        @pl.when(i == 0)
        def _():
            for ref in (dcw_ref, dcb_ref, dlng_ref, dlnb_ref):
                ref[...] = jnp.zeros_like(ref)

        y0, y1 = _cc_recompute(glu_ref, glup_ref, cw_ref, cb_ref, i == 0)
        xhat, rstd = _ln_stats(y1)
        z = xhat * lng_ref[...] + lnb_ref[...]
        dz = dy_ref[...] * _dsilu(z, _sigmoid(z))
        dlng_ref[...] += jnp.sum(dz * xhat, axis=0, keepdims=True)
        dlnb_ref[...] += jnp.sum(dz, axis=0, keepdims=True)
        dxh = dz * lng_ref[...]
        dy1 = rstd * (dxh - jnp.mean(dxh, axis=-1, keepdims=True) - xhat * jnp.mean(dxh * xhat, axis=-1, keepdims=True))
        dy1_ref[...] = dy1
        dcb_ref[...] += jnp.sum(dy1, axis=0, keepdims=True)
        for j in range(CC_K):
            dcw_ref[j:j + 1, :] += jnp.sum(dy1 * _shift_down(y0, CC_K - 1 - j)[CC_HALO:, :], axis=0, keepdims=True)

    return _pcall(
        body, (dy, glu, glu, cw, cb, lng, lnb), name=name, grid=(s // ts,),
        in_specs=[row(W_C), row(2 * W_C), prev, small(CC_HALO), small(1), small(1), small(1)],
        out_specs=[row(W_C), small(CC_HALO), small(1), small(1), small(1)],
        out_shape=[SDS((s, W_C), F32), SDS((CC_HALO, W_C), F32)] + [SDS((1, W_C), F32)] * 3, carry=carry)


def _cc_bwd_glu(dy1, glu, cw, name, carry=()):
    s = glu.shape[0]
    ts = _time_tile(s)
    n_t = s // ts
    n32 = ts // CC_HALO

    def body(dy1_ref, dyn_ref, glu_ref, cw_ref, dglu_ref):
        i = pl.program_id(0)
        dcat = jnp.concatenate([dy1_ref[...], jnp.where(i < n_t - 1, dyn_ref[...], 0.0)], axis=0)
        dy0 = jnp.zeros((ts, W_C), F32)
        for j in range(CC_K):
            dy0 = dy0 + cw_ref[j:j + 1, :] * _shift_up(dcat, CC_K - 1 - j)[:ts, :]
        a = glu_ref[:, :W_C]
        sg = _sigmoid(glu_ref[:, W_C:])
        dglu_ref[...] = jnp.concatenate([dy0 * sg, dy0 * a * sg * (1.0 - sg)], axis=1)

    row = lambda w: pl.BlockSpec((ts, w), lambda i: (i, 0))
    nxt = pl.BlockSpec((CC_HALO, W_C), lambda i: (jnp.minimum((i + 1) * n32, s // CC_HALO - 1), 0))
    return _pcall(
        body, (dy1, dy1, glu, cw), name=name, grid=(n_t,),
        in_specs=[row(W_C), nxt, row(2 * W_C), pl.BlockSpec((CC_HALO, W_C), lambda i: (0, 0))],
        out_specs=[row(2 * W_C)], out_shape=[SDS((s, 2 * W_C), F32)], carry=carry)


_MIX_OFFS = ((0, W_A), (W_A, W_A + W_B), (W_A + W_B, W_A + W_B + W_C))


def _mix_out_fwd(x, ya, yb, yc, group_g, w_out, post_g, name, carry=()):
    s, d = x.shape
    ts = _time_tile(s)
    dm = w_out.shape[0]

    def body(x_ref, ya_ref, yb_ref, yc_ref, gg_ref, w_ref, qg_ref, xo_ref, o_ref):
        parts = []
        for y_ref, (lo, hi) in zip((ya_ref, yb_ref, yc_ref), _MIX_OFFS):
            yv = y_ref[...]
            parts.append(yv * _rms_r(yv) * gg_ref[:, lo:hi])
        o = _mm(jnp.concatenate(parts, axis=1), w_ref[...])
        o_ref[...] = o
        xo_ref[...] = x_ref[...] + o * _rms_r(o) * qg_ref[...]

    row = lambda w: pl.BlockSpec((ts, w), lambda i: (i, 0))
    return _pcall(
        body, (x, ya, yb, yc, group_g, w_out, post_g), name=name, grid=(s // ts,),
        in_specs=[row(d), row(W_A), row(W_B), row(W_C), pl.BlockSpec((1, dm), lambda i: (0, 0)),
                  pl.BlockSpec((dm, d), lambda i: (0, 0)), pl.BlockSpec((1, d), lambda i: (0, 0))],
        out_specs=[row(d), row(d)], out_shape=[SDS((s, d), F32), SDS((s, d), F32)], carry=carry)


def _mix_out_bwd(dxo, o, ya, yb, yc, group_g, w_out, post_g, name, carry=()):
    s, d = o.shape
    ts = _time_tile(s)
    n_t = s // ts
    dm = w_out.shape[0]

    def body(dxo_ref, o_ref, ya_ref, yb_ref, yc_ref, gg_ref, w_ref, qg_ref,
             dya_ref, dyb_ref, dyc_ref, dw_ref, dqg_ref, dgg_ref, acc):
        i = pl.program_id(0)

        @pl.when(i == 0)
        def _():
            acc[...] = jnp.zeros_like(acc)
            dqg_ref[...] = jnp.zeros_like(dqg_ref)
            dgg_ref[...] = jnp.zeros_like(dgg_ref)

        ov = o_ref[...]
        do, dq = _rms_bwd(ov, _rms_r(ov), qg_ref[...], dxo_ref[...])
        dqg_ref[...] += dq
        do = do.astype(BF16)
        dyn = _mm_nt(do, w_ref[...])
        parts, dggs = [], []
        for y_ref, dy_ref, (lo, hi) in zip((ya_ref, yb_ref, yc_ref), (dya_ref, dyb_ref, dyc_ref), _MIX_OFFS):
            yv = y_ref[...]
            r = _rms_r(yv)
            gg = gg_ref[:, lo:hi]
            parts.append(yv * r * gg)
            dyv, dg = _rms_bwd(yv, r, gg, dyn[:, lo:hi])
            dy_ref[...] = dyv
            dggs.append(dg)
        dgg_ref[...] += jnp.concatenate(dggs, axis=1)
        acc[...] += _mm_tn(jnp.concatenate(parts, axis=1), do)

        @pl.when(i == n_t - 1)
        def _():
            dw_ref[...] = acc[...].astype(BF16)

    row = lambda w: pl.BlockSpec((ts, w), lambda i: (i, 0))
    full = pl.BlockSpec((dm, d), lambda i: (0, 0))
    return _pcall(
        body, (dxo, o, ya, yb, yc, group_g, w_out, post_g), name=name, grid=(n_t,),
        in_specs=[row(d), row(d), row(W_A), row(W_B), row(W_C), pl.BlockSpec((1, dm), lambda i: (0, 0)), full,
                  pl.BlockSpec((1, d), lambda i: (0, 0))],
        out_specs=[row(W_A), row(W_B), row(W_C), full, pl.BlockSpec((1, d), lambda i: (0, 0)),
                   pl.BlockSpec((1, dm), lambda i: (0, 0))],
        out_shape=[SDS((s, W_A), F32), SDS((s, W_B), F32), SDS((s, W_C), F32), SDS((dm, d), BF16),
                   SDS((1, d), F32), SDS((1, dm), F32)],
        scratch_shapes=[pltpu.VMEM((dm, d), F32)], carry=carry)


def _loss_head(y, target, name):
    s, d = y.shape
    ts = _time_tile(s)

    def body(y_ref, t_ref, loss_ref, dy_ref):
        @pl.when(pl.program_id(0) == 0)
        def _():
            loss_ref[...] = jnp.zeros_like(loss_ref)

        err = y_ref[...] - t_ref[...]
        dy_ref[...] = err * (1.0 / d)
        per_tok = jnp.mean(err * err, axis=-1, keepdims=True)
        loss_ref[...] += 0.5 * jnp.sum(per_tok, axis=0, keepdims=True)

    row = pl.BlockSpec((ts, d), lambda i: (i, 0))
    return _pcall(body, (y, target), name=name, grid=(s // ts,), in_specs=[row, row],
                  out_specs=[pl.BlockSpec((1, BLK), lambda i: (0, 0)), row],
                  out_shape=[SDS((1, BLK), F32), SDS((s, d), F32)])[0]


def _adamw_math(w, g, m, v):
    m = ADAM_B1 * m + (1.0 - ADAM_B1) * g
    v = ADAM_B2 * v + (1.0 - ADAM_B2) * (g * g)
    m_hat = m / (1.0 - ADAM_B1 ** ADAM_STEP)
    v_hat = v / (1.0 - ADAM_B2 ** ADAM_STEP)
    delta = -ADAM_LR * (m_hat / (jnp.sqrt(v_hat) + ADAM_EPS) + ADAM_WD * w)
    return delta, m, v


def _row_tile(rows, cap=256):
    best = None
    for t in range(16, min(rows, cap) + 1, 16):
        if rows % t == 0:
            best = t
    return best if best is not None else rows


def _reduce_adamw(recv, w, m, v, name):
    n_l, r, c = w.shape
    tr = _row_tile(r)

    def body(recv_ref, w_ref, m_ref, v_ref, g_ref, d_ref, nm_ref, nv_ref):
        g = recv_ref[0].astype(F32)
        for p in range(1, N_DEV):
            g = g + recv_ref[p].astype(F32)
        g_ref[...] = g
        d_ref[...], nm_ref[...], nv_ref[...] = _adamw_math(w_ref[...], g, m_ref[...], v_ref[...])

    blk = pl.BlockSpec((None, tr, c), lambda l, i: (l, i, 0))
    return _pcall(
        body, (recv, w, m, v), name=name, grid=(n_l, r // tr),
        in_specs=[pl.BlockSpec((N_DEV, None, tr, c), lambda l, i: (0, l, i, 0)), blk, blk, blk],
        out_specs=[blk] * 4, out_shape=[SDS(w.shape, F32)] * 4)[0]


def _reduce_adamw_small(parts, w, m, v, name):
    def body(p_ref, w_ref, m_ref, v_ref, g_ref, d_ref, nm_ref, nv_ref):
        g = p_ref[0]
        for p in range(1, N_DEV):
            g = g + p_ref[p]
        g_ref[...] = g
        d_ref[...], nm_ref[...], nv_ref[...] = _adamw_math(w_ref[...], g, m_ref[...], v_ref[...])

    vm = pl.BlockSpec(memory_space=pltpu.VMEM)
    return pl.pallas_call(body, name=name, in_specs=[vm] * 4, out_specs=[vm] * 4, out_shape=[SDS(w.shape, F32)] * 4,
                          compiler_params=pltpu.CompilerParams(vmem_limit_bytes=VMEM_LIMIT))(parts, w, m, v)


def _rows_of(shape):
    return -(-math.prod(shape) // (8 * BLK)) * 8


def _pack(arrs):
    rows = []
    for a in arrs:
        n, r = math.prod(a.shape), _rows_of(a.shape)
        if n % BLK == 0:
            part = a.reshape(n // BLK, BLK)
            rows.append(part if n // BLK == r else jnp.pad(part, ((0, r - n // BLK), (0, 0))))
        else:
            rows.append(jnp.pad(a.reshape(-1), (0, r * BLK - n)).reshape(r, BLK))
    return jnp.concatenate(rows, axis=0)


def _unpack(packed, shapes):
    out, row = [], 0
    for shp in shapes:
        n, r = math.prod(shp), _rows_of(shp)
        if n % BLK == 0:
            out.append(packed[row:row + n // BLK].reshape(shp))
        else:
            out.append(packed[row:row + r].reshape(-1)[:n].reshape(shp))
        row += r
    return out


def _block_diag(w):
    nb, bw, _ = w.shape
    eye = jnp.eye(nb, dtype=w.dtype)
    return (eye[:, None, :, None] * w[:, :, None, :]).reshape(nb * bw, nb * bw)


def _diag_blocks(wd, nb):
    bw = wd.shape[0] // nb
    return jnp.stack([wd[b * bw:(b + 1) * bw, b * bw:(b + 1) * bw] for b in range(nb)])


WEIGHT_NAMES = ['ffn1_pre_g', 'ffn1_w_gu', 'ffn1_w_down', 'ffn1_post_g', 'mix_pre_g', 'w_in', 'lru_conv_w', 'lru_conv_b',
                'lru_w_a', 'lru_b_a', 'lru_w_x', 'lru_b_x', 'lru_lambda', 'attn_sinks', 'conv_w', 'conv_b', 'conv_ln_g',
                'conv_ln_b', 'group_g', 'w_out', 'mix_post_g', 'ffn2_pre_g', 'ffn2_w_gu', 'ffn2_w_down', 'ffn2_post_g']
BIG = ('ffn1_w_gu', 'ffn1_w_down', 'w_in', 'w_out', 'ffn2_w_gu', 'ffn2_w_down')
TRANSPOSED = ('ffn1_w_gu', 'ffn2_w_gu', 'w_in')
SMALL = tuple(k for k in WEIGHT_NAMES if k not in BIG)
CHANNEL_SHARDED = ('lru_conv_w', 'conv_w')


def _step(x, target, w, m, v):
    n_l = w['ffn1_pre_g'].shape[0]
    assert n_l == 2, "the exchange schedule below is laid out for two layers"
    s, d = x.shape[1], x.shape[2]
    x = x.reshape(s, d)
    target = target.reshape(s, d)
    me = _my_pos()[3]
    tview = lambda t, k: jnp.swapaxes(t[k], 1, 2) if k in TRANSPOSED else t[k]
    wb = {k: tview(w, k).astype(BF16) for k in BIG}
    vec = lambda name, l: w[name][l][None, :]

    conv_shard = _pack([w['lru_conv_w'], w['conv_w']])
    g0 = _all_gather([(wb['ffn1_w_gu'], 0), (wb['ffn1_w_down'], 0), (wb['w_in'], 0), (wb['w_out'], 0),
                      (conv_shard, None)], "all_gather_first")
    wts = [dict(), dict()]
    wts[0]['ffn1_w_gu'], wts[0]['ffn1_w_down'], wts[0]['w_in'], wts[0]['w_out'], conv_g = g0
    ch = W_A // N_DEV
    conv_parts = [_unpack(conv_g[p], [(n_l, LRU_K, ch), (n_l, CC_K, ch)]) for p in range(N_DEV)]
    lru_cw = jnp.concatenate([cp[0] for cp in conv_parts], axis=-1)
    cc_cw = jnp.concatenate([cp[1] for cp in conv_parts], axis=-1)
    cc_cw = jnp.pad(cc_cw, ((0, 0), (0, CC_HALO - CC_K), (0, 0)))

    fc = wb['ffn1_w_gu'].shape[1]
    cut1, cut2 = (fc * 4 // 11 + 15) // 16 * 16, (fc * 27 // 44 + 15) // 16 * 16
    gather_plan = {
        ('ffn1', 0): [('A', 'f2_0', ('ffn2_w_gu', 'ffn2_w_down'), 0)],
        ('mix_in', 0): [('B', 'f2_0'), ('A', 'g1_1a', ('ffn1_w_gu',), 1, (0, cut1))],
        ('lru', 0): [('A', 'g1_1b', ('ffn1_w_gu',), 1, (cut1, cut2 - cut1), 'g1_1a')],
        ('attn', 0): [('A', 'g1_1', ('ffn1_w_gu',), 1, (cut2, fc - cut2), 'g1_1b')],
        ('cconv', 0): [('B', 'g1_1')],
        ('ffn2', 0): [('D', None, ('ffn1_w_down',), 1), ('A', 'wi_1', ('w_in',), 1), ('A', 'wo_1', ('w_out',), 1)],
        ('ffn1', 1): [('A', 'f2_1', ('ffn2_w_gu', 'ffn2_w_down'), 1), ('B', 'wi_1'), ('B', 'wo_1')],
        ('mix_in', 1): [('B', 'f2_1')],
    }
    pend = {}

    def fwd(kernel_name, l, fn, *args):
        plan = gather_plan.get((kernel_name, l), [])
        carry = []
        for st in plan:
            if st[0] == 'B':
                carry.append(_gather_b(pend[st[1]][2]))
            else:
                rows = st[4] if len(st) > 4 else None
                into = pend.pop(st[5])[2] if len(st) > 5 else [None] * len(st[2])
                carry.append(_gather_a([(wb[k], st[3], rows, buf) for k, buf in zip(st[2], into)],
                                       two_level=st[0] == 'A'))
        outs, ex = fn(*args, f"{kernel_name}_fwd_l{l}", carry)
        for st, bufs in zip(plan, ex):
            if st[0] == 'A':
                pend[st[1]] = (st[2], st[3], bufs)
            else:
                names, wl = (st[2], st[3]) if st[0] == 'D' else pend.pop(st[1])[:2]
                for k, b in zip(names, bufs):
                    wts[wl][k] = b
        return outs

    saved = []
    h = x
    for l in range(n_l):
        sv = {'x0': h}
        lw = wts[l]
        x1, sv['h1'], sv['g1'], sv['u1'], sv['d1'] = fwd(
            'ffn1', l, _ffn_fwd, h, vec('ffn1_pre_g', l), vec('ffn1_post_g', l), lw['ffn1_w_gu'], lw['ffn1_w_down'])
        sv['x1'] = x1
        sv['hn'], lx, lg, q, k, vv, glu = fwd('mix_in', l, _mix_in_fwd, x1, vec('mix_pre_g', l),
                                              lw['w_in'].reshape(D_IN_PROJ, d))
        sv.update(lx=lx, lg=lg, q=q, k=k, v=vv, glu=glu)
        lru_p = (lru_cw[l], vec('lru_conv_b', l), _block_diag(w['lru_w_a'][l]).astype(BF16), vec('lru_b_a', l),
                 _block_diag(w['lru_w_x'][l]).astype(BF16), vec('lru_b_x', l), vec('lru_lambda', l))
        cc_p = (cc_cw[l], vec('conv_b', l), vec('conv_ln_g', l), vec('conv_ln_b', l))
        sv.update(lru_p=lru_p, cc_p=cc_p)
        sv['ya'], sv['hs'] = fwd('lru', l, _lru_fwd, lx, lg, lru_p)
        sv['sink_row'] = jnp.repeat(w['attn_sinks'][l], BLK)[None, :]
        (sv['yb'],) = fwd('attn', l, _attn_fwd, q, k, vv, sv['sink_row'])
        (sv['yc'],) = fwd('cconv', l, _cc_fwd, glu, *cc_p)
        x2, sv['o'] = fwd('mix_out', l, _mix_out_fwd, x1, sv['ya'], sv['yb'], sv['yc'], vec('group_g', l),
                          lw['w_out'].reshape(-1, d), vec('mix_post_g', l))
        sv['x2'] = x2
        h, sv['h2'], sv['g2'], sv['u2'], sv['d2'] = fwd(
            'ffn2', l, _ffn_fwd, x2, vec('ffn2_pre_g', l), vec('ffn2_post_g', l), lw['ffn2_w_gu'], lw['ffn2_w_down'])
        saved.append(sv)

    loss_row, dh = _loss_head(h, target, "loss_head")

    recv = {k: None for k in BIG}
    ready = {}
    small = [dict() for _ in range(n_l)]

    c_even, c_odd = tuple(range(0, N_DEV, 2)), tuple(range(1, N_DEV, 2))

    def exchange(keys):
        return _grad_x([(ready[key[:2]], key[1], recv[key[0]]) + tuple(key[2:]) for key in keys], n_l)

    def received(keys, bufs):
        for key, b in zip(keys, bufs):
            recv[key[0]] = b

    def run(fn, *args, keys=()):
        outs, ex = fn(*args, carry=[exchange(keys)] if keys else [])
        if keys:
            received(keys, ex[0])
        return outs

    for l in reversed(range(n_l)):
        sv, sg, lw = saved[l], small[l], wts[l]
        keys = [] if l == n_l - 1 else [('ffn1_w_gu', l + 1)]
        dx2, dd, dg, du, sg['ffn2_pre_g'], sg['ffn2_post_g'] = run(
            _ffn_bwd_act, dh, sv['d2'], sv['x2'], vec('ffn2_pre_g', l), vec('ffn2_post_g', l), sv['g2'], sv['u2'],
            lw['ffn2_w_gu'], lw['ffn2_w_down'], f"ffn2_bwd_act_l{l}", keys=keys)
        keys = [] if l == n_l - 1 else [('ffn1_w_down', l + 1), ('w_in', l + 1, c_odd)]
        dwg, dwu, dwd = run(_ffn_bwd_w, sv['h2'], dd, sv['g2'], sv['u2'], dg, du, f"ffn2_bwd_w_l{l}", keys=keys)
        ready[('ffn2_w_gu', l)] = [dwg, dwu]
        ready[('ffn2_w_down', l)] = [dwd.reshape(N_DEV, -1, d)]
        dya, dyb, dyc, dw_out, sg['mix_post_g'], sg['group_g'] = run(
            _mix_out_bwd, dx2, sv['o'], sv['ya'], sv['yb'], sv['yc'], vec('group_g', l), lw['w_out'].reshape(-1, d),
            vec('mix_post_g', l), f"mix_out_bwd_l{l}")
        ready[('w_out', l)] = [dw_out.reshape(N_DEV, -1, d)]
        (dlx, dlg, sg['lru_conv_w'], sg['lru_conv_b'], dwa, sg['lru_b_a'], dwx, sg['lru_b_x'],
         sg['lru_lambda']) = run(_lru_bwd, dya, sv['lx'], sv['lg'], sv['hs'], sv['lru_p'], f"lru_bwd_l{l}")
        sg['lru_w_a'] = _diag_blocks(dwa, A_BLOCKS)
        sg['lru_w_x'] = _diag_blocks(dwx, A_BLOCKS)
        dq, dk, dv, dk_up, dv_up, dsk = run(_attn_bwd, dyb, sv['q'], sv['k'], sv['v'], sv['sink_row'],
                                            f"attn_bwd_l{l}", keys=[('ffn2_w_down', l, c_even)] if l == 0 else [])
        sg['attn_sinks'] = dsk[:, 0]
        dy1, dcw, sg['conv_b'], sg['conv_ln_g'], sg['conv_ln_b'] = run(
            _cc_bwd_conv, dyc, sv['glu'], *sv['cc_p'], f"cconv_bwd_conv_l{l}", keys=[('w_out', l)] if l == 0 else [])
        sg['conv_w'] = dcw[:CC_K]
        (dglu,) = run(_cc_bwd_glu, dy1, sv['glu'], sv['cc_p'][0], f"cconv_bwd_glu_l{l}")
        dx1, dw_in, sg['mix_pre_g'] = run(
            _mix_in_bwd, dx2, sv['x1'], vec('mix_pre_g', l), sv['hn'], lw['w_in'].reshape(D_IN_PROJ, d),
            dlx, dlg, dq, dk, dk_up, dv, dv_up, dglu, f"mix_in_bwd_l{l}",
            keys=[('ffn2_w_down', l, c_odd)] if l == 0 else [('w_out', l)])
        ready[('w_in', l)] = [dw_in.reshape(N_DEV, -1, d)]
        dh, dd, dg, du, sg['ffn1_pre_g'], sg['ffn1_post_g'] = run(
            _ffn_bwd_act, dx1, sv['d1'], sv['x0'], vec('ffn1_pre_g', l), vec('ffn1_post_g', l), sv['g1'], sv['u1'],
            lw['ffn1_w_gu'], lw['ffn1_w_down'], f"ffn1_bwd_act_l{l}",
            keys=[('ffn2_w_gu', l), ('w_in', l)] if l == 0 else [('ffn2_w_gu', l)])
        if l > 0:
            dwg, dwu, dwd = run(_ffn_bwd_w, sv['h1'], dd, sv['g1'], sv['u1'], dg, du, f"ffn1_bwd_w_l{l}",
                                keys=[('ffn2_w_down', l), ('w_in', l, c_even)])
            ready[('ffn1_w_gu', l)] = [dwg, dwu]
            ready[('ffn1_w_down', l)] = [dwd.reshape(N_DEV, -1, d)]
        else:
            part = _pack([jnp.stack([small[j][k] for j in range(n_l)]) for k in SMALL] + [loss_row])
            (recv['ffn1_w_gu'], recv['ffn1_w_down']), ex = _ffn_bwd_w_send(
                sv['h1'], dd, sv['g1'], sv['u1'], dg, du, recv['ffn1_w_gu'], recv['ffn1_w_down'], 0, "ffn1_bwd_w_send_l0",
                [_gather_a([(part, None)], two_level=False)])
            small_parts = ex[0][0]
    grad_x = dh.reshape(1, s, d)

    out = {}
    for k in BIG:
        res = _reduce_adamw(recv[k], tview(w, k), tview(m, k), tview(v, k), f"reduce_adamw_{k}")
        out[k] = [jnp.swapaxes(r, 1, 2) for r in res] if k in TRANSPOSED else res

    small_shapes = [(n_l,) + tuple(small[0][k].shape) for k in SMALL]

    def widen(t, k):
        if k not in CHANNEL_SHARDED:
            return t.reshape((n_l,) + tuple(small[0][k].shape))
        full = jnp.zeros((n_l,) + tuple(small[0][k].shape), F32)
        return lax.dynamic_update_slice_in_dim(full, t, me * ch, axis=2)

    no_w = jnp.zeros(loss_row.shape, F32)
    packed = [_pack([widen(src[k], k) for k in SMALL] + [no_w]) for src in (w, m, v)]
    res = _reduce_adamw_small(small_parts, *packed, "reduce_adamw_small")
    loss = _unpack(res[0], small_shapes + [loss_row.shape])[-1][0, 0]
    for k, g, dlt, nm, nv in zip(SMALL, *[_unpack(r, small_shapes) for r in res]):
        vals = [g, dlt, nm, nv]
        if k in CHANNEL_SHARDED:
            vals = [lax.dynamic_slice_in_dim(t, me * ch, ch, axis=2) for t in vals]
        out[k] = [t.reshape(w[k].shape) for t in vals]

    return (loss, grad_x, *[out[k][0] for k in WEIGHT_NAMES], *[out[k][1] for k in WEIGHT_NAMES],
            *[out[k][2] for k in WEIGHT_NAMES], *[out[k][3] for k in WEIGHT_NAMES])


def kernel(x, ffn1_pre_g, ffn1_w_gu, ffn1_w_down, ffn1_post_g, mix_pre_g, w_in, lru_conv_w, lru_conv_b, lru_w_a, lru_b_a, lru_w_x, lru_b_x, lru_lambda, attn_sinks, conv_w, conv_b, conv_ln_g, conv_ln_b, group_g, w_out, mix_post_g, ffn2_pre_g, ffn2_w_gu, ffn2_w_down, ffn2_post_g, loss_target, m_ffn1_pre_g, m_ffn1_w_gu, m_ffn1_w_down, m_ffn1_post_g, m_mix_pre_g, m_w_in, m_lru_conv_w, m_lru_conv_b, m_lru_w_a, m_lru_b_a, m_lru_w_x, m_lru_b_x, m_lru_lambda, m_attn_sinks, m_conv_w, m_conv_b, m_conv_ln_g, m_conv_ln_b, m_group_g, m_w_out, m_mix_post_g, m_ffn2_pre_g, m_ffn2_w_gu, m_ffn2_w_down, m_ffn2_post_g, v_ffn1_pre_g, v_ffn1_w_gu, v_ffn1_w_down, v_ffn1_post_g, v_mix_pre_g, v_w_in, v_lru_conv_w, v_lru_conv_b, v_lru_w_a, v_lru_b_a, v_lru_w_x, v_lru_b_x, v_lru_lambda, v_attn_sinks, v_conv_w, v_conv_b, v_conv_ln_g, v_conv_ln_b, v_group_g, v_w_out, v_mix_post_g, v_ffn2_pre_g, v_ffn2_w_gu, v_ffn2_w_down, v_ffn2_post_g):
    args = (ffn1_pre_g, ffn1_w_gu, ffn1_w_down, ffn1_post_g, mix_pre_g, w_in, lru_conv_w, lru_conv_b, lru_w_a, lru_b_a, lru_w_x, lru_b_x, lru_lambda, attn_sinks, conv_w, conv_b, conv_ln_g, conv_ln_b, group_g, w_out, mix_post_g, ffn2_pre_g, ffn2_w_gu, ffn2_w_down, ffn2_post_g)
    ms = (m_ffn1_pre_g, m_ffn1_w_gu, m_ffn1_w_down, m_ffn1_post_g, m_mix_pre_g, m_w_in, m_lru_conv_w, m_lru_conv_b, m_lru_w_a, m_lru_b_a, m_lru_w_x, m_lru_b_x, m_lru_lambda, m_attn_sinks, m_conv_w, m_conv_b, m_conv_ln_g, m_conv_ln_b, m_group_g, m_w_out, m_mix_post_g, m_ffn2_pre_g, m_ffn2_w_gu, m_ffn2_w_down, m_ffn2_post_g)
    vs = (v_ffn1_pre_g, v_ffn1_w_gu, v_ffn1_w_down, v_ffn1_post_g, v_mix_pre_g, v_w_in, v_lru_conv_w, v_lru_conv_b, v_lru_w_a, v_lru_b_a, v_lru_w_x, v_lru_b_x, v_lru_lambda, v_attn_sinks, v_conv_w, v_conv_b, v_conv_ln_g, v_conv_ln_b, v_group_g, v_w_out, v_mix_post_g, v_ffn2_pre_g, v_ffn2_w_gu, v_ffn2_w_down, v_ffn2_post_g)
    return _step(x, loss_target, dict(zip(WEIGHT_NAMES, args)), dict(zip(WEIGHT_NAMES, ms)), dict(zip(WEIGHT_NAMES, vs)))
```

```python
import functools
import math
import operator

import jax
import jax.numpy as jnp
from jax import lax
from jax.experimental import pallas as pl
from jax.experimental.pallas import tpu as pltpu

F32 = jnp.float32
BF16 = jnp.bfloat16
N_DEV = 8
AXES = ("x", "y", "c")
MESH = pl.DeviceIdType.MESH

NORM_EPS = 1e-6
LN_EPS = 1e-5
NEG_BIG = -1e30
W_A = 256
W_B = 512
W_C = 256
HEAD_DIM = 64
N_Q_HEADS = 8
N_KV_HEADS = 2
Q_PER_KV = N_Q_HEADS // N_KV_HEADS
KV_W = N_KV_HEADS * HEAD_DIM
BLK = 128
LRU_K = 4
LRU_C = 8.0
A_BLOCKS = 4
CC_K = 31
CC_HALO = 32
LRU_HALO = 8
D_IN_PROJ = 2 * W_A + W_B + 2 * KV_W + 2 * W_C
ADAM_LR = 0.001
ADAM_B1 = 0.9
ADAM_B2 = 0.999
ADAM_EPS = 1e-08
ADAM_WD = 0.01
ADAM_STEP = 10
VMEM_LIMIT = 56 * 1024 * 1024

SDS = jax.ShapeDtypeStruct
ANY = pl.BlockSpec(memory_space=pl.ANY)


def _time_tile(s):
    return max(BLK, s // 8)


class _Exchange:
    def __init__(self, inputs, out_shapes, aliases, sem_shapes, start, wait):
        self.inputs, self.out_shapes, self.aliases, self.sem_shapes = inputs, out_shapes, aliases, sem_shapes
        self.start, self.wait = start, wait


def _my_pos():
    x, y, c = (lax.axis_index(a) for a in AXES)
    return x, y, c, 4 * x + 2 * y + c


def _flip(k):
    x, y, c, _ = _my_pos()
    return (1 - x if k & 4 else x, 1 - y if k & 2 else y, 1 - c if k & 1 else c)


def _slot(dev):
    return 4 * dev[0] + 2 * dev[1] + dev[2]


def _dev(p):
    return (p >> 2, (p >> 1) & 1, p & 1)


def _gather_a(items, two_level):
    rels = (1, 2, 4, 6) if two_level else tuple(range(1, N_DEV))
    items = [tuple(it) + (None,) * (4 - len(it)) for it in items]
    n = len(items)
    with_buf = [a for a in range(n) if items[a][3] is not None]

    def rows_of(ref, a):
        return ref if items[a][2] is None else ref.at[pl.ds(*items[a][2])]

    def src_of(ins, a):
        return rows_of(ins[a] if items[a][1] is None else ins[a].at[items[a][1]], a)

    def dst_of(outs, a, slot):
        return rows_of(outs[a].at[slot], a)

    def shape_of(a):
        arr, l = items[a][:2]
        return arr.shape if l is None else arr.shape[1:]

    def copies(ins, outs, sems, a):
        send, recv, _ = sems
        me = _my_pos()[3]
        return [(k, pltpu.make_async_remote_copy(
            src_ref=src_of(ins, a), dst_ref=dst_of(outs, a, me), send_sem=send.at[a, k], recv_sem=recv.at[a, k],
            device_id=_flip(k), device_id_type=MESH)) for k in rels]

    def local(ins, outs, sems, a):
        return pltpu.make_async_copy(src_of(ins, a), dst_of(outs, a, _my_pos()[3]), sems[2].at[a])

    def start(ins, outs, sems):
        for a in range(n):
            local(ins, outs, sems, a).start()
            for _, cp in copies(ins, outs, sems, a):
                cp.start()

    def wait(ins, outs, sems):
        send, recv, _ = sems
        for a in range(n):
            for k, cp in copies(ins, outs, sems, a):
                pltpu.make_async_remote_copy(
                    src_ref=src_of(ins, a), dst_ref=dst_of(outs, a, _slot(_flip(k))), send_sem=send.at[a, k],
                    recv_sem=recv.at[a, k], device_id=_flip(k), device_id_type=MESH).wait_recv()
                cp.wait_send()
            local(ins, outs, sems, a).wait()

    return _Exchange([it[0] for it in items] + [items[a][3] for a in with_buf],
                     [SDS((N_DEV,) + shape_of(a), items[a][0].dtype) for a in range(n)],
                     {n + j: a for j, a in enumerate(with_buf)},
                     [pltpu.SemaphoreType.DMA((n, N_DEV)), pltpu.SemaphoreType.DMA((n, N_DEV)),
                      pltpu.SemaphoreType.DMA((n,))], start, wait)


def _gather_b(bufs):
    n = len(bufs)

    def copies(ins, outs, sems, a, c_of_block):
        send, recv = sems
        x, y, c, _ = _my_pos()
        res = []
        for k in (2, 4, 6):
            chip = _flip(k)
            blk = _slot((chip[0], chip[1], c if c_of_block == "mine" else 1 - c))
            res.append(pltpu.make_async_remote_copy(
                src_ref=ins[a].at[blk], dst_ref=outs[a].at[blk], send_sem=send.at[a, k], recv_sem=recv.at[a, k],
                device_id=_flip(1), device_id_type=MESH))
        return res

    def start(ins, outs, sems):
        for a in range(n):
            for cp in copies(ins, outs, sems, a, "mine"):
                cp.start()

    def wait(ins, outs, sems):
        for a in range(n):
            for cp in copies(ins, outs, sems, a, "sibling"):
                cp.wait_recv()
            for cp in copies(ins, outs, sems, a, "mine"):
                cp.wait_send()

    return _Exchange(list(bufs), [SDS(b.shape, b.dtype) for b in bufs], {a: a for a in range(n)},
                     [pltpu.SemaphoreType.DMA((n, N_DEV)), pltpu.SemaphoreType.DMA((n, N_DEV))], start, wait)


def _grad_x(items, n_l):
    items = [tuple(it) + (None,) * (4 - len(it)) for it in items]
    n = len(items)
    owners = [tuple(range(N_DEV)) if it[3] is None else tuple(it[3]) for it in items]
    inputs, first_in, aliases, out_shapes = [], [], {}, []
    for a, (arrs, l, recv, _) in enumerate(items):
        first_in.append(len(inputs))
        inputs += list(arrs)
        assert sum(arr.shape[0] for arr in arrs) == N_DEV
        if recv is not None:
            aliases[len(inputs)] = a
            inputs.append(recv)
        out_shapes.append(SDS((N_DEV, n_l) + arrs[0].shape[1:], arrs[0].dtype))

    def slab(ins, a, p):
        off = 0
        for j, arr in enumerate(items[a][0]):
            if p < off + arr.shape[0]:
                return ins[first_in[a] + j].at[p - off]
            off += arr.shape[0]
        raise AssertionError

    def rdma(ins, outs, sems, a, p, src_dev):
        send, recv, _ = sems
        return pltpu.make_async_remote_copy(
            src_ref=slab(ins, a, p), dst_ref=outs[a].at[src_dev, items[a][1]], send_sem=send.at[a, p],
            recv_sem=recv.at[a, src_dev], device_id=_dev(p), device_id_type=MESH)

    def local(ins, outs, sems, a, p):
        return pltpu.make_async_copy(slab(ins, a, p), outs[a].at[p, items[a][1]], sems[2].at[a])

    def start(ins, outs, sems):
        me = _my_pos()[3]
        for p in range(N_DEV):
            mine = [a for a in range(n) if p in owners[a]]

            @pl.when(me != p)
            def _():
                for a in mine:
                    rdma(ins, outs, sems, a, p, me).start()

            @pl.when(me == p)
            def _():
                for a in mine:
                    local(ins, outs, sems, a, p).start()

    def wait(ins, outs, sems):
        me = _my_pos()[3]
        for a in range(n):
            i_own = functools.reduce(operator.or_, [me == q for q in owners[a]])
            for p in range(N_DEV):
                @pl.when((me != p) & i_own)
                def _():
                    rdma(ins, outs, sems, a, p, p).wait_recv()

                if p in owners[a]:
                    @pl.when(me != p)
                    def _():
                        rdma(ins, outs, sems, a, p, p).wait_send()

                    @pl.when(me == p)
                    def _():
                        local(ins, outs, sems, a, p).wait()

    return _Exchange(inputs, out_shapes, aliases,
                     [pltpu.SemaphoreType.DMA((n, N_DEV)), pltpu.SemaphoreType.DMA((n, N_DEV)),
                      pltpu.SemaphoreType.DMA((n,))], start, wait)


def _pcall(body, args, *, name, grid, in_specs, out_specs, out_shape, scratch_shapes=(), carry=(), body_aliases=None):
    n_in, n_out, n_scr = len(in_specs), len(out_specs), len(scratch_shapes)
    c_in = [len(e.inputs) for e in carry]
    c_out = [len(e.out_shapes) for e in carry]
    c_sem = [len(e.sem_shapes) for e in carry]
    aliases = dict(body_aliases or {})
    for j, e in enumerate(carry):
        for i_loc, o_loc in e.aliases.items():
            aliases[n_in + sum(c_in[:j]) + i_loc] = n_out + sum(c_out[:j]) + o_loc

    def wrapped(*refs):
        def take(counts, pos):
            groups = []
            for cnt in counts:
                groups.append(refs[pos:pos + cnt])
                pos += cnt
            return groups, pos

        (ins,), pos = take([n_in], 0)
        cins, pos = take(c_in, pos)
        (outs,), pos = take([n_out], pos)
        couts, pos = take(c_out, pos)
        (scr,), pos = take([n_scr], pos)
        csems, pos = take(c_sem, pos)
        if carry:
            ids = [pl.program_id(k) for k in range(len(grid))]
            first = functools.reduce(operator.and_, [i == 0 for i in ids])
            last = functools.reduce(operator.and_, [i == g - 1 for i, g in zip(ids, grid)])

            @pl.when(first)
            def _():
                for e, ci, co, cs in zip(carry, cins, couts, csems):
                    e.start(ci, co, cs)

        body(*ins, *outs, *scr)
        if carry:
            @pl.when(last)
            def _():
                for e, ci, co, cs in zip(carry, cins, couts, csems):
                    e.wait(ci, co, cs)

    res = pl.pallas_call(
        wrapped, name=name, grid=grid,
        in_specs=list(in_specs) + [ANY] * sum(c_in),
        out_specs=list(out_specs) + [ANY] * sum(c_out),
        out_shape=list(out_shape) + [s for e in carry for s in e.out_shapes],
        scratch_shapes=list(scratch_shapes) + [s for e in carry for s in e.sem_shapes],
        input_output_aliases=aliases,
        compiler_params=pltpu.CompilerParams(dimension_semantics=("arbitrary",) * len(grid),
                                             vmem_limit_bytes=VMEM_LIMIT),
    )(*args, *[a for e in carry for a in e.inputs])
    outs, pos, extra = list(res[:n_out]), n_out, []
    for cnt in c_out:
        extra.append(list(res[pos:pos + cnt]))
        pos += cnt
    return outs, extra


def _all_gather(items, name):
    n = len(items)
    shape_of = lambda a: items[a][0].shape if items[a][1] is None else items[a][0].shape[1:]

    def body(*refs):
        ins, outs, (send_sems, recv_sems, local_sems) = refs[:n], refs[n:2 * n], refs[2 * n:]
        x, y, c, me = _my_pos()
        src_of = lambda a: ins[a] if items[a][1] is None else ins[a].at[items[a][1]]

        def copy(a, k, block, to, src=None):
            dst = outs[a].at[_slot(block)]
            return pltpu.make_async_remote_copy(
                src_ref=dst if src is None else src, dst_ref=dst,
                send_sem=send_sems.at[a, k], recv_sem=recv_sems.at[a, k], device_id=to, device_id_type=MESH)

        mine = [pltpu.make_async_copy(src_of(a), outs[a].at[me], local_sems.at[a]) for a in range(n)]
        for cp in mine:
            cp.start()
        first = [copy(a, k, (x, y, c), _flip(k), src=src_of(a)) for a in range(n) for k in (1, 2, 4, 6)]
        for cp in first:
            cp.start()
        passed = []
        for k in (2, 4, 6):
            for a in range(n):
                copy(a, k, _flip(k), (x, y, c)).wait_recv()
                fwd = copy(a, k + 1, _flip(k), _flip(1))
                fwd.start()
                passed.append(fwd)
        for a in range(n):
            copy(a, 1, _flip(1), (x, y, c)).wait_recv()
            for k in (2, 4, 6):
                copy(a, k + 1, _flip(k + 1), (x, y, c)).wait_recv()
        for cp in first + passed:
            cp.wait_send()
        for cp in mine:
            cp.wait()

    return pl.pallas_call(
        body, name=name,
        in_specs=[ANY] * n, out_specs=[ANY] * n,
        out_shape=[SDS((N_DEV,) + shape_of(a), items[a][0].dtype) for a in range(n)],
        scratch_shapes=[pltpu.SemaphoreType.DMA((n, N_DEV)), pltpu.SemaphoreType.DMA((n, N_DEV)),
                        pltpu.SemaphoreType.DMA((n,))],
    )(*[it[0] for it in items])


def _mm(a, b):
    return jnp.dot(a.astype(BF16), b.astype(BF16), preferred_element_type=F32)


def _mm_nt(a, b):
    return lax.dot_general(a.astype(BF16), b.astype(BF16), (((1,), (1,)), ((), ())), preferred_element_type=F32)


def _mm_tn(a, b):
    return lax.dot_general(a.astype(BF16), b.astype(BF16), (((0,), (0,)), ((), ())), preferred_element_type=F32)


def _rms_r(x):
    return lax.rsqrt(jnp.mean(x * x, axis=-1, keepdims=True) + NORM_EPS)


def _rms_bwd(x, r, g, dy):
    gy = dy * g
    dx = r * (gy - x * (r * r) * jnp.mean(gy * x, axis=-1, keepdims=True))
    dg = jnp.sum(dy * x * r, axis=0, keepdims=True)
    return dx, dg


def _sigmoid(x):
    return 1.0 / (1.0 + jnp.exp(-x))


def _sigmoid_t(x):
    return 0.5 * jnp.tanh(0.5 * x) + 0.5


def _dsilu(z, sz):
    return sz * (1.0 + z * (1.0 - sz))


def _swiglu_bf16(g, u):
    sg = 0.5 * jnp.tanh(0.5 * g) + 0.5
    silu = g * sg
    return silu * u, silu, sg + silu * (1.0 - sg)


_GELU_C = math.sqrt(2.0 / math.pi)


def _gelu(x):
    t = jnp.tanh(_GELU_C * (x + 0.044715 * x * x * x))
    return 0.5 * x * (1.0 + t), t


def _dgelu(x, t):
    return 0.5 * (1.0 + t) + 0.5 * x * (1.0 - t * t) * _GELU_C * (1.0 + 3.0 * 0.044715 * x * x)


def _log1p(e):
    return jnp.where(e < 1e-2, e * (1.0 - e * (0.5 - e * (1.0 / 3.0))), jnp.log(1.0 + e))


def _softplus(x):
    return jnp.maximum(x, 0.0) + _log1p(jnp.exp(-jnp.abs(x)))


def _neg_expm1(x):
    small = -x * (1.0 + x * (0.5 + x * (1.0 / 6.0) * (1.0 + x * 0.25)))
    return jnp.where(x > -1e-2, small, 1.0 - jnp.exp(x))


def _shift_down(x, s):
    return x if s == 0 else pltpu.roll(x, s, 0)


def _shift_up(x, s):
    return x if s == 0 else pltpu.roll(x, x.shape[0] - s, 0)


def _ffn_wspecs(d, fc, order):
    f_of = (lambda i, f: f) if order == "tf" else (lambda f, i: f)
    n_f = N_DEV // 2
    return [pl.BlockSpec((None, fc, d), lambda *g: (f_of(*g), 0, 0)),
            pl.BlockSpec((None, fc, d), lambda *g: (f_of(*g) + n_f, 0, 0)),
            pl.BlockSpec((2, fc // 2, d), lambda *g: (f_of(*g), 0, 0))]


def _ffn_fwd(x, pre_g, post_g, wgu_t, wd, name, carry=()):
    s, d = x.shape
    fc = wgu_t.shape[1]
    ts = 2 * _time_tile(s)
    n_t, n_f = s // ts, N_DEV // 2

    def body(x_ref, pg_ref, qg_ref, wg_ref, wu_ref, wd_ref, xo_ref, h_ref, g_ref, u_ref, d_ref, h_scr, acc):
        f = pl.program_id(1)

        @pl.when(f == 0)
        def _():
            xv = x_ref[...]
            hv = (xv * _rms_r(xv) * pg_ref[...]).astype(BF16)
            h_scr[...] = hv
            h_ref[...] = hv
            acc[...] = jnp.zeros_like(acc)

        hv = h_scr[...]
        g = _mm_nt(hv, wg_ref[...])
        u = _mm_nt(hv, wu_ref[...])
        g = g.astype(BF16)
        u = u.astype(BF16)
        g_ref[...] = g
        u_ref[...] = u
        acc[...] += jnp.dot(_swiglu_bf16(g, u)[0], wd_ref[...].reshape(fc, d), preferred_element_type=F32)

        @pl.when(f == n_f - 1)
        def _():
            dv = acc[...]
            d_ref[...] = dv.astype(BF16)
            xo_ref[...] = x_ref[...] + 0.5 * (dv * _rms_r(dv) * qg_ref[...])

    row = pl.BlockSpec((ts, d), lambda i, f: (i, 0))
    vec = pl.BlockSpec((1, d), lambda i, f: (0, 0))
    act = pl.BlockSpec((None, ts, fc), lambda i, f: (f, i, 0))
    return _pcall(
        body, (x, pre_g, post_g, wgu_t, wgu_t, wd), name=name, grid=(n_t, n_f),
        in_specs=[row, vec, vec] + _ffn_wspecs(d, fc, "tf"),
        out_specs=[row, row, act, act, row],
        out_shape=[SDS((s, d), F32), SDS((s, d), BF16), SDS((n_f, s, fc), BF16), SDS((n_f, s, fc), BF16),
                   SDS((s, d), BF16)],
        scratch_shapes=[pltpu.VMEM((ts, d), BF16), pltpu.VMEM((ts, d), F32)], carry=carry)


def _ffn_bwd_act(dxo, dmid, x, pre_g, post_g, g_s, u_s, wgu_t, wd, name, carry=()):
    s, d = x.shape
    fc = wgu_t.shape[1]
    ts = _time_tile(s)
    n_t, n_f = s // ts, N_DEV // 2

    def body(dxo_ref, dm_ref, x_ref, pg_ref, qg_ref, g_ref, u_ref, wg_ref, wu_ref, wd_ref,
             dx_ref, dd_ref, dg_ref, du_ref, dpg_ref, dqg_ref, dd_scr, dh_acc):
        i, f = pl.program_id(0), pl.program_id(1)

        @pl.when((i == 0) & (f == 0))
        def _():
            dpg_ref[...] = jnp.zeros_like(dpg_ref)
            dqg_ref[...] = jnp.zeros_like(dqg_ref)

        @pl.when(f == 0)
        def _():
            dv = dm_ref[...].astype(F32)
            ddv, dq = _rms_bwd(dv, _rms_r(dv), qg_ref[...], 0.5 * dxo_ref[...])
            dqg_ref[...] += dq
            dd_scr[...] = ddv.astype(BF16)
            dd_ref[...] = ddv.astype(BF16)
            dh_acc[...] = jnp.zeros_like(dh_acc)

        da = _mm_nt(dd_scr[...], wd_ref[...].reshape(fc, d)).astype(BF16)
        u = u_ref[...]
        _, silu, dsilu = _swiglu_bf16(g_ref[...], u)
        du = da * silu
        dg = da * u * dsilu
        dg_ref[...] = dg
        du_ref[...] = du
        dh_acc[...] += _mm(dg, wg_ref[...]) + _mm(du, wu_ref[...])

        @pl.when(f == n_f - 1)
        def _():
            xv = x_ref[...]
            dxv, dp = _rms_bwd(xv, _rms_r(xv), pg_ref[...], dh_acc[...])
            dpg_ref[...] += dp
            dx_ref[...] = dxo_ref[...] + dxv

    row = pl.BlockSpec((ts, d), lambda i, f: (i, 0))
    vec = pl.BlockSpec((1, d), lambda i, f: (0, 0))
    act = pl.BlockSpec((None, ts, fc), lambda i, f: (f, i, 0))
    return _pcall(
        body, (dxo, dmid, x, pre_g, post_g, g_s, u_s, wgu_t, wgu_t, wd), name=name, grid=(n_t, n_f),
        in_specs=[row, row, row, vec, vec, act, act] + _ffn_wspecs(d, fc, "tf"),
        out_specs=[row, row, act, act, vec, vec],
        out_shape=[SDS((s, d), F32), SDS((s, d), BF16), SDS((n_f, s, fc), BF16), SDS((n_f, s, fc), BF16),
                   SDS((1, d), F32), SDS((1, d), F32)],
        scratch_shapes=[pltpu.VMEM((ts, d), BF16), pltpu.VMEM((ts, d), F32)], carry=carry)


def _ffn_bwd_w(h, dd, g_s, u_s, dg, du, name, carry=()):
    s, d = h.shape
    n_f, _, fc = g_s.shape
    ts = 2 * _time_tile(s)
    n_t = s // ts

    def body(h_ref, dd_ref, g_ref, u_ref, dg_ref, du_ref, wg_ref, wu_ref, wd_ref, acc_g, acc_u, acc_d):
        i = pl.program_id(1)

        @pl.when(i == 0)
        def _():
            acc_g[...] = jnp.zeros_like(acc_g)
            acc_u[...] = jnp.zeros_like(acc_u)
            acc_d[...] = jnp.zeros_like(acc_d)

        a = _swiglu_bf16(g_ref[...], u_ref[...])[0]
        hv = h_ref[...]
        acc_g[...] += _mm_tn(dg_ref[...], hv)
        acc_u[...] += _mm_tn(du_ref[...], hv)
        acc_d[...] += _mm_tn(a, dd_ref[...])

        @pl.when(i == n_t - 1)
        def _():
            wg_ref[...] = acc_g[...].astype(BF16)
            wu_ref[...] = acc_u[...].astype(BF16)
            wd_ref[...] = acc_d[...].astype(BF16)

    row = pl.BlockSpec((ts, d), lambda f, i: (i, 0))
    act = pl.BlockSpec((None, ts, fc), lambda f, i: (f, i, 0))
    out = pl.BlockSpec((None, fc, d), lambda f, i: (f, 0, 0))
    return _pcall(
        body, (h, dd, g_s, u_s, dg, du), name=name, grid=(n_f, n_t),
        in_specs=[row, row, act, act, act, act], out_specs=[out, out, out],
        out_shape=[SDS((n_f, fc, d), BF16)] * 3,
        scratch_shapes=[pltpu.VMEM((fc, d), F32)] * 3, carry=carry)


def _ffn_bwd_w_send(h, dd, g_s, u_s, dg, du, recv_gu, recv_d, layer, name, carry=()):
    s, d = h.shape
    n_f, _, fc = g_s.shape
    ts = _time_tile(s)
    n_t = s // ts
    half = fc // 2

    def chunk_of(step):
        return (step + 2 * lax.axis_index("x") + lax.axis_index("y")) % n_f

    def body(h_ref, dd_ref, g_ref, u_ref, dg_ref, du_ref, _rgu_in, _rd_in, rgu_ref, rd_ref,
             acc_g, acc_u, acc_d, st_g, st_u, st_d, pair_gu, pair_d, zeros,
             send_sems, recv_sems, local_sems, pair_sems, zero_sems):
        f, i = pl.program_id(0), pl.program_id(1)
        x, y, c_me, me = _my_pos()
        sibling = me ^ 1

        @pl.when(i == 0)
        def _():
            acc_g[...] = jnp.zeros_like(acc_g)
            acc_u[...] = jnp.zeros_like(acc_u)
            acc_d[...] = jnp.zeros_like(acc_d)

        def zero_fills():
            res = []
            for n_k, k in enumerate((2, 4, 6)):
                other = _flip(k)
                slot = _slot((other[0], other[1], 1 - c_me))
                res += [pltpu.make_async_copy(zeros, rgu_ref.at[slot, layer, pl.ds(0, half)], zero_sems.at[n_k, 0]),
                        pltpu.make_async_copy(zeros, rgu_ref.at[slot, layer, pl.ds(half, half)], zero_sems.at[n_k, 1]),
                        pltpu.make_async_copy(zeros, rd_ref.at[slot, layer], zero_sems.at[n_k, 2])]
            return res

        @pl.when((f == 0) & (i == 0))
        def _():
            zeros[...] = jnp.zeros_like(zeros)
            for cp in zero_fills():
                cp.start()

        a = _swiglu_bf16(g_ref[...], u_ref[...])[0]
        hv = h_ref[...]
        acc_g[...] += _mm_tn(dg_ref[...], hv)
        acc_u[...] += _mm_tn(du_ref[...], hv)
        acc_d[...] += _mm_tn(a, dd_ref[...])

        def messages(fs):
            c = chunk_of(fs)
            lo, hi = pl.ds(0, half), pl.ds(half, half)
            return [(st_g.at[fs], pair_gu.at[fs // 2, 0], rgu_ref, 0, c, 0),
                    (st_u.at[fs], pair_gu.at[fs // 2, 1], rgu_ref, 0, c + n_f, 1),
                    (st_d.at[fs, lo], pair_d.at[fs], rd_ref, 1, 2 * c, 2),
                    (st_d.at[fs, hi], pair_d.at[fs], rd_ref, 1, 2 * c + 1, 3)]

        def roles(p):
            same_chip = (p >> 1) == (me >> 1)
            same_c = (p & 1) == c_me
            return p == me, p == sibling, (~same_chip) & same_c, (~same_chip) & (~same_c)

        def to_owner(fs, msg, src_dev):
            src, _, buf, row, p, j = msg
            return pltpu.make_async_remote_copy(
                src_ref=src, dst_ref=buf.at[src_dev, layer], send_sem=send_sems.at[fs, j],
                recv_sem=recv_sems.at[row, src_dev], device_id=_dev(p), device_id_type=MESH)

        def to_pair(fs, msg):
            src, pair, _, _, _, j = msg
            return pltpu.make_async_remote_copy(
                src_ref=src, dst_ref=pair, send_sem=send_sems.at[fs, j], recv_sem=pair_sems.at[fs, j],
                device_id=_dev(sibling), device_id_type=MESH)

        def local(fs, msg):
            src, _, buf, _, p, j = msg
            return pltpu.make_async_copy(src, buf.at[p, layer], local_sems.at[fs, j])

        for fs in range(n_f):
            @pl.when((f == fs) & (i == n_t - 1))
            def _():
                st_g[fs] = acc_g[...].astype(BF16)
                st_u[fs] = acc_u[...].astype(BF16)
                st_d[fs] = acc_d[...].astype(BF16)
                msgs = messages(fs)
                for msg in msgs:
                    mine, sib, _, hand_over = roles(msg[4])

                    @pl.when(mine)
                    def _():
                        local(fs, msg).start()

                    @pl.when(sib)
                    def _():
                        to_owner(fs, msg, me).start()

                    @pl.when(hand_over)
                    def _():
                        to_pair(fs, msg).start()
                for msg in msgs:
                    @pl.when(roles(msg[4])[2])
                    def _():
                        src, pair = msg[0], msg[1]
                        to_pair(fs, msg).wait_recv()
                        src[...] = (src[...].astype(F32) + pair[...].astype(F32)).astype(BF16)
                        to_owner(fs, msg, me).start()

        @pl.when((f == n_f - 1) & (i == n_t - 1))
        def _():
            for fs in range(n_f):
                for msg in messages(fs):
                    mine = roles(msg[4])[0]

                    @pl.when(mine)
                    def _():
                        local(fs, msg).wait()

                    @pl.when(~mine)
                    def _():
                        to_owner(fs, msg, me).wait_send()
            for k in (1, 2, 4, 6):
                src_dev = _slot(_flip(k))
                to_owner(0, messages(0)[0], src_dev).wait_recv()
                to_owner(0, messages(0)[2], src_dev).wait_recv()
            for cp in zero_fills():
                cp.wait()

    row = pl.BlockSpec((ts, d), lambda f, i: (i, 0))
    act = pl.BlockSpec((None, ts, fc), lambda f, i: (chunk_of(f), i, 0))
    return _pcall(
        body, (h, dd, g_s, u_s, dg, du, recv_gu, recv_d), name=name, grid=(n_f, n_t),
        in_specs=[row, row, act, act, act, act, ANY, ANY], out_specs=[ANY, ANY],
        out_shape=[SDS(recv_gu.shape, recv_gu.dtype), SDS(recv_d.shape, recv_d.dtype)],
        scratch_shapes=[pltpu.VMEM((fc, d), F32)] * 3 + [pltpu.VMEM((n_f, fc, d), BF16)] * 3
        + [pltpu.VMEM((n_f // 2, 2, fc, d), BF16), pltpu.VMEM((n_f, half, d), BF16), pltpu.VMEM((half, d), BF16)]
        + [pltpu.SemaphoreType.DMA((n_f, 4)), pltpu.SemaphoreType.DMA((2, N_DEV)), pltpu.SemaphoreType.DMA((n_f, 4)),
           pltpu.SemaphoreType.DMA((n_f, 4)), pltpu.SemaphoreType.DMA((3, 3))],
        carry=carry, body_aliases={6: 0, 7: 1})


_PROJ_WIDTHS = (W_A, W_A, W_B, KV_W, KV_W, 2 * W_C)


def _mix_in_fwd(x, pre_g, w_in_t, name, carry=()):
    s, d = x.shape
    ts = 2 * _time_tile(s)

    def body(x_ref, pg_ref, w_ref, hn_ref, *outs):
        xv = x_ref[...]
        hn = (xv * _rms_r(xv) * pg_ref[...]).astype(BF16)
        hn_ref[...] = hn
        proj = _mm_nt(hn, w_ref[...])
        off = 0
        for o_ref, w in zip(outs, _PROJ_WIDTHS):
            o_ref[...] = proj[:, off:off + w]
            off += w

    row = lambda w: pl.BlockSpec((ts, w), lambda i: (i, 0))
    return _pcall(
        body, (x, pre_g, w_in_t), name=name, grid=(s // ts,),
        in_specs=[row(d), pl.BlockSpec((1, d), lambda i: (0, 0)), pl.BlockSpec((D_IN_PROJ, d), lambda i: (0, 0))],
        out_specs=[row(d)] + [row(w) for w in _PROJ_WIDTHS],
        out_shape=[SDS((s, d), BF16)] + [SDS((s, w), F32) for w in _PROJ_WIDTHS], carry=carry)


def _mix_in_bwd(dres, x, pre_g, hn, w_in_t, dlx, dlg, dq, dk, dk_up, dv, dv_up, dglu, name, carry=()):
    s, d = x.shape
    ts = _time_tile(s)
    n_t = s // ts

    def body(dres_ref, x_ref, pg_ref, hn_ref, w_ref, dlx_ref, dlg_ref, dq_ref, dk_ref, dkn_ref,
             dv_ref, dvn_ref, dglu_ref, dx_ref, dw_ref, dpg_ref, acc):
        i = pl.program_id(0)

        @pl.when(i == 0)
        def _():
            acc[...] = jnp.zeros_like(acc)
            dpg_ref[...] = jnp.zeros_like(dpg_ref)

        def with_next(cur_ref, nxt_ref):
            nxt = jnp.where(i < n_t - 1, nxt_ref[...], 0.0)
            if ts == BLK:
                return cur_ref[...] + nxt
            return jnp.concatenate([cur_ref[:ts - BLK, :], cur_ref[ts - BLK:, :] + nxt], axis=0)

        dproj = jnp.concatenate([dlx_ref[...], dlg_ref[...], dq_ref[...], with_next(dk_ref, dkn_ref),
                                 with_next(dv_ref, dvn_ref), dglu_ref[...]], axis=1).astype(BF16)
        dhn = _mm(dproj, w_ref[...])
        acc[...] += _mm_tn(dproj, hn_ref[...])
        xv = x_ref[...]
        dxv, dp = _rms_bwd(xv, _rms_r(xv), pg_ref[...], dhn)
        dpg_ref[...] += dp
        dx_ref[...] = dres_ref[...] + dxv

        @pl.when(i == n_t - 1)
        def _():
            dw_ref[...] = acc[...].astype(BF16)

    row = lambda w: pl.BlockSpec((ts, w), lambda i: (i, 0))
    nxt = pl.BlockSpec((BLK, KV_W), lambda i: (jnp.minimum(i + 1, n_t - 1), 0))
    vec = pl.BlockSpec((1, d), lambda i: (0, 0))
    full = pl.BlockSpec((D_IN_PROJ, d), lambda i: (0, 0))
    return _pcall(
        body, (dres, x, pre_g, hn, w_in_t, dlx, dlg, dq, dk, dk_up, dv, dv_up, dglu), name=name, grid=(n_t,),
        in_specs=[row(d), row(d), vec, row(d), full, row(W_A), row(W_A), row(W_B), row(KV_W), nxt,
                  row(KV_W), nxt, row(2 * W_C)],
        out_specs=[row(d), full, vec],
        out_shape=[SDS((s, d), F32), SDS((D_IN_PROJ, d), BF16), SDS((1, d), F32)],
        scratch_shapes=[pltpu.VMEM((D_IN_PROJ, d), F32)], carry=carry)


def _lru_gates(xc, lru_p):
    cw_ref, cb_ref, wa_ref, ba_ref, wx_ref, bx_ref, lam_ref = lru_p
    c = cb_ref[...]
    for j in range(LRU_K):
        c = c + cw_ref[j:j + 1, :] * _shift_down(xc, LRU_K - 1 - j)[LRU_HALO:, :]
    r = _sigmoid(_mm(c, wa_ref[...]) + ba_ref[...])
    ig = _sigmoid(_mm(c, wx_ref[...]) + bx_ref[...])
    sp = _softplus(-lam_ref[...])
    log_a = -LRU_C * r * sp
    a = jnp.exp(log_a)
    m = jnp.sqrt(_neg_expm1(2.0 * log_a))
    return c, r, ig, sp, a, m


def _lru_pspecs():
    small = lambda r: pl.BlockSpec((r, W_A), lambda i: (0, 0))
    return [small(LRU_K), small(1), small(W_A), small(1), small(W_A), small(1), small(1)]


def _lru_fwd(lx, lg, lru_p, name, carry=()):
    s = lx.shape[0]
    ts = _time_tile(s)
    n8 = ts // LRU_HALO

    def body(lx_ref, lxp_ref, lg_ref, *rest):
        lru_p, (ya_ref, h_ref, hcarry) = rest[:7], rest[7:]
        i = pl.program_id(0)
        prev = jnp.where(i > 0, lxp_ref[...], 0.0)
        xc = jnp.concatenate([prev, lx_ref[...]], axis=0)
        c, r, ig, sp, a, m = _lru_gates(xc, lru_p)
        acc_a, acc_b = a, m * (ig * c)
        t = lax.broadcasted_iota(jnp.int32, a.shape, 0)
        k = 1
        while k < ts:
            keep = t >= k
            acc_b = jnp.where(keep, acc_a * _shift_down(acc_b, k) + acc_b, acc_b)
            acc_a = jnp.where(keep, acc_a * _shift_down(acc_a, k), acc_a)
            k *= 2
        h0 = jnp.where(i > 0, hcarry[...], 0.0)
        h = acc_b + acc_a * h0
        hcarry[...] = h[ts - 1:ts, :]
        h_ref[...] = h
        ya_ref[...] = _gelu(lg_ref[...])[0] * h

    row = pl.BlockSpec((ts, W_A), lambda i: (i, 0))
    prev8 = pl.BlockSpec((LRU_HALO, W_A), lambda i: (jnp.maximum(i * n8 - 1, 0), 0))
    return _pcall(
        body, (lx, lx, lg, *lru_p), name=name, grid=(s // ts,),
        in_specs=[row, prev8, row] + _lru_pspecs(), out_specs=[row, row],
        out_shape=[SDS((s, W_A), F32), SDS((s, W_A), F32)],
        scratch_shapes=[pltpu.VMEM((1, W_A), F32)], carry=carry)


def _lru_bwd(dya, lx, lg, h_s, lru_p, name, carry=()):
    s = lx.shape[0]
    ts = _time_tile(s)
    n_t = s // ts
    n8 = ts // LRU_HALO

    def body(dya_ref, lx_ref, lxp_ref, lg_ref, h_ref, hp_ref, *rest):
        lru_p = rest[:7]
        (dlx_ref, dlg_ref, dcw_ref, dcb_ref, dwa_ref, dba_ref, dwx_ref, dbx_ref, dlam_ref,
         carry_a, carry_l, carry_dc) = rest[7:]
        cw_ref, _, wa_ref, _, wx_ref, _, lam_ref = lru_p
        i = pl.program_id(0)
        first_tile = i == n_t - 1
        last_tile = i == 0

        @pl.when(i == 0)
        def _():
            for ref in (dcw_ref, dcb_ref, dwa_ref, dba_ref, dwx_ref, dbx_ref, dlam_ref):
                ref[...] = jnp.zeros_like(ref)

        prev = jnp.where(first_tile, 0.0, lxp_ref[...])
        xc = jnp.concatenate([prev, lx_ref[...]], axis=0)
        c, r, ig, sp, a, m = _lru_gates(xc, lru_p)
        h = h_ref[...]
        hcat = jnp.concatenate([jnp.where(first_tile, 0.0, hp_ref[...]), h], axis=0)
        h_m1 = _shift_down(hcat, 1)[LRU_HALO:, :]
        lg = lg_ref[...]
        ge, th = _gelu(lg)
        dya = dya_ref[...]
        dlg_ref[...] = dya * h * _dgelu(lg, th)
        dh = dya * ge
        t = lax.broadcasted_iota(jnp.int32, a.shape, 0)
        a_next = jnp.where(t < ts - 1, _shift_up(a, 1), jnp.where(last_tile, 0.0, carry_a[...]))
        acc_a, acc_b = a_next, dh
        k = 1
        while k < ts:
            keep = t < ts - k
            acc_b = jnp.where(keep, acc_a * _shift_up(acc_b, k) + acc_b, acc_b)
            acc_a = jnp.where(keep, acc_a * _shift_up(acc_a, k), acc_a)
            k *= 2
        lam_beyond = jnp.where(last_tile, 0.0, carry_l[...])
        lmb = acc_b + acc_a * lam_beyond
        carry_a[...] = a[0:1, :]
        carry_l[...] = lmb[0:1, :]
        gi = ig * c
        dgi = lmb * m
        dla = lmb * h_m1 * a - (lmb * gi) * (a * a) / m
        dr = dla * (-LRU_C * sp)
        dsp = jnp.sum(dla * (-LRU_C * r), axis=0, keepdims=True)
        dlam_ref[...] += -dsp * _sigmoid(-lam_ref[...])
        dra = dr * r * (1.0 - r)
        dia = dgi * c * ig * (1.0 - ig)
        dc = dgi * ig + _mm_nt(dra, wa_ref[...]) + _mm_nt(dia, wx_ref[...])
        dwa_ref[...] += _mm_tn(c, dra)
        dwx_ref[...] += _mm_tn(c, dia)
        dba_ref[...] += jnp.sum(dra, axis=0, keepdims=True)
        dbx_ref[...] += jnp.sum(dia, axis=0, keepdims=True)
        dcb_ref[...] += jnp.sum(dc, axis=0, keepdims=True)
        dcc = jnp.concatenate([dc, jnp.where(last_tile, 0.0, carry_dc[...])], axis=0)
        carry_dc[...] = dc[0:LRU_HALO, :]
        dlx = jnp.zeros_like(dc)
        for j in range(LRU_K):
            sh = LRU_K - 1 - j
            dcw_ref[j:j + 1, :] += jnp.sum(dc * _shift_down(xc, sh)[LRU_HALO:, :], axis=0, keepdims=True)
            dlx = dlx + cw_ref[j:j + 1, :] * _shift_up(dcc, sh)[:ts, :]
        dlx_ref[...] = dlx

    row = pl.BlockSpec((ts, W_A), lambda i: (n_t - 1 - i, 0))
    prev8 = pl.BlockSpec((LRU_HALO, W_A), lambda i: (jnp.maximum((n_t - 1 - i) * n8 - 1, 0), 0))
    small = lambda r: pl.BlockSpec((r, W_A), lambda i: (0, 0))
    return _pcall(
        body, (dya, lx, lx, lg, h_s, h_s, *lru_p), name=name, grid=(n_t,),
        in_specs=[row, row, prev8, row, row, prev8] + _lru_pspecs(),
        out_specs=[row, row, small(LRU_K), small(1), small(W_A), small(1), small(W_A), small(1), small(1)],
        out_shape=[SDS((s, W_A), F32), SDS((s, W_A), F32), SDS((LRU_K, W_A), F32), SDS((1, W_A), F32),
                   SDS((W_A, W_A), F32), SDS((1, W_A), F32), SDS((W_A, W_A), F32), SDS((1, W_A), F32),
                   SDS((1, W_A), F32)],
        scratch_shapes=[pltpu.VMEM((1, W_A), F32), pltpu.VMEM((1, W_A), F32), pltpu.VMEM((LRU_HALO, W_A), F32)],
        carry=carry)


_ATT_ROWS = N_Q_HEADS * BLK
_GRP_ROWS = Q_PER_KV * BLK


def _attn_stack(ref, rows, g):
    return jnp.concatenate([ref[rows, h * HEAD_DIM:(h + 1) * HEAD_DIM]
                            for h in range(g * Q_PER_KV, (g + 1) * Q_PER_KV)], axis=0)


def _attn_unstack(parts):
    return jnp.concatenate([p[j * BLK:(j + 1) * BLK, :] for p in parts for j in range(Q_PER_KV)], axis=1)


def _grp(x, g):
    return x[:, g * _GRP_ROWS:(g + 1) * _GRP_ROWS]


def _attn_block(q_ref, k_ref, kp_ref, v_ref, vp_ref, sink_row, i, b):
    rows, prev = slice(b * BLK, (b + 1) * BLK), slice((b - 1) * BLK, b * BLK)
    qs, kcs, kps, vcs, vps = [], [], [], [], []
    for g in range(N_KV_HEADS):
        cols = slice(g * HEAD_DIM, (g + 1) * HEAD_DIM)
        qs.append(_attn_stack(q_ref, rows, g))
        kcs.append(k_ref[rows, cols])
        vcs.append(v_ref[rows, cols])
        kps.append(kp_ref[:, cols] if b == 0 else k_ref[prev, cols])
        vps.append(vp_ref[:, cols] if b == 0 else v_ref[prev, cols])
    scale = 1.0 / math.sqrt(HEAD_DIM)
    sc = jnp.concatenate([_mm_nt(kcs[g], qs[g]) for g in range(N_KV_HEADS)], axis=1) * scale
    sp = jnp.concatenate([_mm_nt(kps[g], qs[g]) for g in range(N_KV_HEADS)], axis=1) * scale
    kj = lax.broadcasted_iota(jnp.int32, (BLK, _ATT_ROWS), 0)
    qi = lax.broadcasted_iota(jnp.int32, (BLK, _ATT_ROWS), 1) & (BLK - 1)
    sc = jnp.where(kj <= qi, sc, NEG_BIG)
    sp = jnp.where((kj > qi) if b > 0 else ((kj > qi) & (i > 0)), sp, NEG_BIG)
    m = jnp.maximum(jnp.maximum(jnp.max(sc, axis=0, keepdims=True), jnp.max(sp, axis=0, keepdims=True)), sink_row)
    pc = jnp.exp(sc - m)
    pp = jnp.exp(sp - m)
    es = jnp.exp(sink_row - m)
    inv = 1.0 / (jnp.sum(pc, axis=0, keepdims=True) + jnp.sum(pp, axis=0, keepdims=True) + es)
    return qs, kcs, kps, vcs, vps, pc * inv, pp * inv, es * inv


def _attn_specs(s, ts):
    bpt = ts // BLK
    tile = lambda w: pl.BlockSpec((ts, w), lambda i: (i, 0))
    prv = pl.BlockSpec((BLK, KV_W), lambda i: (jnp.maximum(i * bpt - 1, 0), 0))
    sink = pl.BlockSpec((1, _ATT_ROWS), lambda i: (0, 0))
    return bpt, tile, prv, sink


def _attn_fwd(q, k, v, sink_row, name, carry=()):
    s = q.shape[0]
    ts = _time_tile(s)
    bpt, tile, prv, sink = _attn_specs(s, ts)

    def body(q_ref, k_ref, kp_ref, v_ref, vp_ref, sk_ref, y_ref):
        i = pl.program_id(0)
        for b in range(bpt):
            _, _, _, vcs, vps, pc, pp, _ = _attn_block(q_ref, k_ref, kp_ref, v_ref, vp_ref, sk_ref[...], i, b)
            outs = [_mm_tn(_grp(pc, g), vcs[g]) + _mm_tn(_grp(pp, g), vps[g]) for g in range(N_KV_HEADS)]
            y_ref[b * BLK:(b + 1) * BLK, :] = _attn_unstack(outs)

    return _pcall(
        body, (q, k, k, v, v, sink_row), name=name, grid=(s // ts,),
        in_specs=[tile(W_B), tile(KV_W), prv, tile(KV_W), prv, sink],
        out_specs=[tile(W_B)], out_shape=[SDS((s, W_B), F32)], carry=carry)


def _attn_bwd(dy, q, k, v, sinks, name, carry=()):
    s = q.shape[0]
    ts = _time_tile(s)
    n_t = s // ts
    bpt, tile, prv, sink = _attn_specs(s, ts)

    def body(dy_ref, q_ref, k_ref, kp_ref, v_ref, vp_ref, sk_ref, dq_ref, dk_ref, dv_ref, dku_ref, dvu_ref, dsk_ref):
        i = pl.program_id(0)

        @pl.when(i == 0)
        def _():
            dsk_ref[...] = jnp.zeros_like(dsk_ref)

        scale = 1.0 / math.sqrt(HEAD_DIM)
        groups = range(N_KV_HEADS)
        head_row = lax.broadcasted_iota(jnp.int32, (N_Q_HEADS, BLK), 0)
        dsk = jnp.zeros((N_Q_HEADS, BLK), F32)
        dk_blocks, dv_blocks = [], []
        for b in range(bpt):
            rows = slice(b * BLK, (b + 1) * BLK)
            qs, kcs, kps, vcs, vps, pc, pp, ps = _attn_block(q_ref, k_ref, kp_ref, v_ref, vp_ref, sk_ref[...], i, b)
            dos = [_attn_stack(dy_ref, rows, g) for g in groups]
            dpc = jnp.concatenate([_mm_nt(vcs[g], dos[g]) for g in groups], axis=1)
            dpp = jnp.concatenate([_mm_nt(vps[g], dos[g]) for g in groups], axis=1)
            delta = jnp.sum(pc * dpc, axis=0, keepdims=True) + jnp.sum(pp * dpp, axis=0, keepdims=True)
            dsc = pc * (dpc - delta) * scale
            dsp = pp * (dpp - delta) * scale
            dq_ref[rows, :] = _attn_unstack([_mm_tn(_grp(dsc, g), kcs[g]) + _mm_tn(_grp(dsp, g), kps[g])
                                             for g in groups])
            dk_blocks.append(jnp.concatenate([_mm(_grp(dsc, g), qs[g]) for g in groups], axis=1))
            dv_blocks.append(jnp.concatenate([_mm(_grp(pc, g), dos[g]) for g in groups], axis=1))
            dkp = jnp.concatenate([_mm(_grp(dsp, g), qs[g]) for g in groups], axis=1)
            dvp = jnp.concatenate([_mm(_grp(pp, g), dos[g]) for g in groups], axis=1)
            if b == 0:
                dku_ref[...] = dkp
                dvu_ref[...] = dvp
            else:
                dk_blocks[b - 1] = dk_blocks[b - 1] + dkp
                dv_blocks[b - 1] = dv_blocks[b - 1] + dvp
            dsink = -ps * delta
            for h in range(N_Q_HEADS):
                dsk = dsk + jnp.where(head_row == h, jnp.sum(dsink[:, h * BLK:(h + 1) * BLK], axis=1, keepdims=True), 0.0)
        for b in range(bpt):
            dk_ref[b * BLK:(b + 1) * BLK, :] = dk_blocks[b]
            dv_ref[b * BLK:(b + 1) * BLK, :] = dv_blocks[b]
        dsk_ref[...] += dsk

    up = pl.BlockSpec((BLK, KV_W), lambda i: (i, 0))
    return _pcall(
        body, (dy, q, k, k, v, v, sinks), name=name, grid=(n_t,),
        in_specs=[tile(W_B), tile(W_B), tile(KV_W), prv, tile(KV_W), prv, sink],
        out_specs=[tile(W_B), tile(KV_W), tile(KV_W), up, up, pl.BlockSpec((N_Q_HEADS, BLK), lambda i: (0, 0))],
        out_shape=[SDS((s, W_B), F32), SDS((s, KV_W), F32), SDS((s, KV_W), F32), SDS((n_t * BLK, KV_W), F32),
                   SDS((n_t * BLK, KV_W), F32), SDS((N_Q_HEADS, BLK), F32)], carry=carry)


def _cc_recompute(glu_ref, glup_ref, cw_ref, cb_ref, first_tile):
    prev = jnp.where(first_tile, 0.0, glup_ref[...])
    ge = jnp.concatenate([prev, glu_ref[...]], axis=0)
    y0 = ge[:, :W_C] * _sigmoid_t(ge[:, W_C:])
    y1 = cb_ref[...]
    for j in range(CC_K):
        y1 = y1 + cw_ref[j:j + 1, :] * _shift_down(y0, CC_K - 1 - j)[CC_HALO:, :]
    return y0, y1


def _ln_stats(y1):
    mu = jnp.mean(y1, axis=-1, keepdims=True)
    xc = y1 - mu
    rstd = lax.rsqrt(jnp.mean(xc * xc, axis=-1, keepdims=True) + LN_EPS)
    return xc * rstd, rstd


def _cc_specs(s, ts):
    n32 = ts // CC_HALO
    row = lambda w: pl.BlockSpec((ts, w), lambda i: (i, 0))
    prev = pl.BlockSpec((CC_HALO, 2 * W_C), lambda i: (jnp.maximum(i * n32 - 1, 0), 0))
    small = lambda r: pl.BlockSpec((r, W_C), lambda i: (0, 0))
    return row, prev, small


def _cc_fwd(glu, cw, cb, lng, lnb, name, carry=()):
    s = glu.shape[0]
    ts = _time_tile(s)
    row, prev, small = _cc_specs(s, ts)

    def body(glu_ref, glup_ref, cw_ref, cb_ref, lng_ref, lnb_ref, y_ref):
        _, y1 = _cc_recompute(glu_ref, glup_ref, cw_ref, cb_ref, pl.program_id(0) == 0)
        xhat, _ = _ln_stats(y1)
        z = xhat * lng_ref[...] + lnb_ref[...]
        y_ref[...] = z * _sigmoid_t(z)

    return _pcall(
        body, (glu, glu, cw, cb, lng, lnb), name=name, grid=(s // ts,),
        in_specs=[row(2 * W_C), prev, small(CC_HALO), small(1), small(1), small(1)],
        out_specs=[row(W_C)], out_shape=[SDS((s, W_C), F32)], carry=carry)


def _cc_bwd_conv(dy, glu, cw, cb, lng, lnb, name, carry=()):
    s = glu.shape[0]
    ts = _time_tile(s)
    row, prev, small = _cc_specs(s, ts)

    def body(dy_ref, glu_ref, glup_ref, cw_ref, cb_ref, lng_ref, lnb_ref, dy1_ref, dcw_ref, dcb_ref, dlng_ref, dlnb_ref):
        i = pl.program_id(0)

        @pl.when(i == 0)
        def _():
            for ref in (dcw_ref, dcb_ref, dlng_ref, dlnb_ref):
                ref[...] = jnp.zeros_like(ref)

        y0, y1 = _cc_recompute(glu_ref, glup_ref, cw_ref, cb_ref, i == 0)
        xhat, rstd = _ln_stats(y1)
        z = xhat * lng_ref[...] + lnb_ref[...]
        dz = dy_ref[...] * _dsilu(z, _sigmoid_t(z))
        dlng_ref[...] += jnp.sum(dz * xhat, axis=0, keepdims=True)
        dlnb_ref[...] += jnp.sum(dz, axis=0, keepdims=True)
        dxh = dz * lng_ref[...]
        dy1 = rstd * (dxh - jnp.mean(dxh, axis=-1, keepdims=True) - xhat * jnp.mean(dxh * xhat, axis=-1, keepdims=True))
        dy1_ref[...] = dy1
        dcb_ref[...] += jnp.sum(dy1, axis=0, keepdims=True)
        for j in range(CC_K):
            dcw_ref[j:j + 1, :] += jnp.sum(dy1 * _shift_down(y0, CC_K - 1 - j)[CC_HALO:, :], axis=0, keepdims=True)

    return _pcall(
        body, (dy, glu, glu, cw, cb, lng, lnb), name=name, grid=(s // ts,),
        in_specs=[row(W_C), row(2 * W_C), prev, small(CC_HALO), small(1), small(1), small(1)],
        out_specs=[row(W_C), small(CC_HALO), small(1), small(1), small(1)],
        out_shape=[SDS((s, W_C), F32), SDS((CC_HALO, W_C), F32)] + [SDS((1, W_C), F32)] * 3, carry=carry)


def _cc_bwd_glu(dy1, glu, cw, name, carry=()):
    s = glu.shape[0]
    ts = _time_tile(s)
    n_t = s // ts
    n32 = ts // CC_HALO

    def body(dy1_ref, dyn_ref, glu_ref, cw_ref, dglu_ref):
        i = pl.program_id(0)
        dcat = jnp.concatenate([dy1_ref[...], jnp.where(i < n_t - 1, dyn_ref[...], 0.0)], axis=0)
        dy0 = jnp.zeros((ts, W_C), F32)
        for j in range(CC_K):
            dy0 = dy0 + cw_ref[j:j + 1, :] * _shift_up(dcat, CC_K - 1 - j)[:ts, :]
        a = glu_ref[:, :W_C]
        sg = _sigmoid_t(glu_ref[:, W_C:])
        dglu_ref[...] = jnp.concatenate([dy0 * sg, dy0 * a * sg * (1.0 - sg)], axis=1)

    row = lambda w: pl.BlockSpec((ts, w), lambda i: (i, 0))
    nxt = pl.BlockSpec((CC_HALO, W_C), lambda i: (jnp.minimum((i + 1) * n32, s // CC_HALO - 1), 0))
    return _pcall(
        body, (dy1, dy1, glu, cw), name=name, grid=(n_t,),
        in_specs=[row(W_C), nxt, row(2 * W_C), pl.BlockSpec((CC_HALO, W_C), lambda i: (0, 0))],
        out_specs=[row(2 * W_C)], out_shape=[SDS((s, 2 * W_C), F32)], carry=carry)


_MIX_OFFS = ((0, W_A), (W_A, W_A + W_B), (W_A + W_B, W_A + W_B + W_C))


def _mix_out_fwd(x, ya, yb, yc, group_g, w_out, post_g, name, carry=()):
    s, d = x.shape
    ts = 2 * _time_tile(s)
    dm = w_out.shape[0]

    def body(x_ref, ya_ref, yb_ref, yc_ref, gg_ref, w_ref, qg_ref, xo_ref, o_ref):
        parts = []
        for y_ref, (lo, hi) in zip((ya_ref, yb_ref, yc_ref), _MIX_OFFS):
            yv = y_ref[...]
            parts.append(yv * _rms_r(yv) * gg_ref[:, lo:hi])
        o = _mm(jnp.concatenate(parts, axis=1), w_ref[...])
        o_ref[...] = o
        xo_ref[...] = x_ref[...] + o * _rms_r(o) * qg_ref[...]

    row = lambda w: pl.BlockSpec((ts, w), lambda i: (i, 0))
    return _pcall(
        body, (x, ya, yb, yc, group_g, w_out, post_g), name=name, grid=(s // ts,),
        in_specs=[row(d), row(W_A), row(W_B), row(W_C), pl.BlockSpec((1, dm), lambda i: (0, 0)),
                  pl.BlockSpec((dm, d), lambda i: (0, 0)), pl.BlockSpec((1, d), lambda i: (0, 0))],
        out_specs=[row(d), row(d)], out_shape=[SDS((s, d), F32), SDS((s, d), F32)], carry=carry)


def _mix_out_bwd(dxo, o, ya, yb, yc, group_g, w_out, post_g, name, carry=()):
    s, d = o.shape
    ts = _time_tile(s)
    n_t = s // ts
    dm = w_out.shape[0]

    def body(dxo_ref, o_ref, ya_ref, yb_ref, yc_ref, gg_ref, w_ref, qg_ref,
             dya_ref, dyb_ref, dyc_ref, dw_ref, dqg_ref, dgg_ref, acc):
        i = pl.program_id(0)

        @pl.when(i == 0)
        def _():
            acc[...] = jnp.zeros_like(acc)
            dqg_ref[...] = jnp.zeros_like(dqg_ref)
            dgg_ref[...] = jnp.zeros_like(dgg_ref)

        ov = o_ref[...]
        do, dq = _rms_bwd(ov, _rms_r(ov), qg_ref[...], dxo_ref[...])
        dqg_ref[...] += dq
        do = do.astype(BF16)
        dyn = _mm_nt(do, w_ref[...])
        parts, dggs = [], []
        for y_ref, dy_ref, (lo, hi) in zip((ya_ref, yb_ref, yc_ref), (dya_ref, dyb_ref, dyc_ref), _MIX_OFFS):
            yv = y_ref[...]
            r = _rms_r(yv)
            gg = gg_ref[:, lo:hi]
            parts.append(yv * r * gg)
            dyv, dg = _rms_bwd(yv, r, gg, dyn[:, lo:hi])
            dy_ref[...] = dyv
            dggs.append(dg)
        dgg_ref[...] += jnp.concatenate(dggs, axis=1)
        acc[...] += _mm_tn(jnp.concatenate(parts, axis=1), do)

        @pl.when(i == n_t - 1)
        def _():
            dw_ref[...] = acc[...].astype(BF16)

    row = lambda w: pl.BlockSpec((ts, w), lambda i: (i, 0))
    full = pl.BlockSpec((dm, d), lambda i: (0, 0))
    return _pcall(
        body, (dxo, o, ya, yb, yc, group_g, w_out, post_g), name=name, grid=(n_t,),
        in_specs=[row(d), row(d), row(W_A), row(W_B), row(W_C), pl.BlockSpec((1, dm), lambda i: (0, 0)), full,
                  pl.BlockSpec((1, d), lambda i: (0, 0))],
        out_specs=[row(W_A), row(W_B), row(W_C), full, pl.BlockSpec((1, d), lambda i: (0, 0)),
                   pl.BlockSpec((1, dm), lambda i: (0, 0))],
        out_shape=[SDS((s, W_A), F32), SDS((s, W_B), F32), SDS((s, W_C), F32), SDS((dm, d), BF16),
                   SDS((1, d), F32), SDS((1, dm), F32)],
        scratch_shapes=[pltpu.VMEM((dm, d), F32)], carry=carry)


def _loss_head(y, target, name):
    s, d = y.shape
    ts = _time_tile(s)

    def body(y_ref, t_ref, loss_ref, dy_ref):
        @pl.when(pl.program_id(0) == 0)
        def _():
            loss_ref[...] = jnp.zeros_like(loss_ref)

        err = y_ref[...] - t_ref[...]
        dy_ref[...] = err * (1.0 / d)
        per_tok = jnp.mean(err * err, axis=-1, keepdims=True)
        loss_ref[...] += 0.5 * jnp.sum(per_tok, axis=0, keepdims=True)

    row = pl.BlockSpec((ts, d), lambda i: (i, 0))
    return _pcall(body, (y, target), name=name, grid=(s // ts,), in_specs=[row, row],
                  out_specs=[pl.BlockSpec((1, BLK), lambda i: (0, 0)), row],
                  out_shape=[SDS((1, BLK), F32), SDS((s, d), F32)])[0]


def _adamw_math(w, g, m, v):
    m = ADAM_B1 * m + (1.0 - ADAM_B1) * g
    v = ADAM_B2 * v + (1.0 - ADAM_B2) * (g * g)
    m_hat = m / (1.0 - ADAM_B1 ** ADAM_STEP)
    v_hat = v / (1.0 - ADAM_B2 ** ADAM_STEP)
    delta = -ADAM_LR * (m_hat / (jnp.sqrt(v_hat) + ADAM_EPS) + ADAM_WD * w)
    return delta, m, v


def _row_tile(rows, cap=256):
    best = None
    for t in range(16, min(rows, cap) + 1, 16):
        if rows % t == 0:
            best = t
    return best if best is not None else rows


def _reduce_adamw(recv, w, m, v, name):
    n_l, r, c = w.shape
    tr = _row_tile(r)

    def body(recv_ref, w_ref, m_ref, v_ref, g_ref, d_ref, nm_ref, nv_ref):
        g = recv_ref[0].astype(F32)
        for p in range(1, N_DEV):
            g = g + recv_ref[p].astype(F32)
        g_ref[...] = g
        d_ref[...], nm_ref[...], nv_ref[...] = _adamw_math(w_ref[...], g, m_ref[...], v_ref[...])

    blk = pl.BlockSpec((None, tr, c), lambda l, i: (l, i, 0))
    return _pcall(
        body, (recv, w, m, v), name=name, grid=(n_l, r // tr),
        in_specs=[pl.BlockSpec((N_DEV, None, tr, c), lambda l, i: (0, l, i, 0)), blk, blk, blk],
        out_specs=[blk] * 4, out_shape=[SDS(w.shape, F32)] * 4)[0]


def _reduce_adamw_small(parts, w, m, v, name):
    def body(p_ref, w_ref, m_ref, v_ref, g_ref, d_ref, nm_ref, nv_ref):
        g = p_ref[0]
        for p in range(1, N_DEV):
            g = g + p_ref[p]
        g_ref[...] = g
        d_ref[...], nm_ref[...], nv_ref[...] = _adamw_math(w_ref[...], g, m_ref[...], v_ref[...])

    vm = pl.BlockSpec(memory_space=pltpu.VMEM)
    return pl.pallas_call(body, name=name, in_specs=[vm] * 4, out_specs=[vm] * 4, out_shape=[SDS(w.shape, F32)] * 4,
                          compiler_params=pltpu.CompilerParams(vmem_limit_bytes=VMEM_LIMIT))(parts, w, m, v)


def _rows_of(shape):
    return -(-math.prod(shape) // (8 * BLK)) * 8


def _pack(arrs):
    rows = []
    for a in arrs:
        n, r = math.prod(a.shape), _rows_of(a.shape)
        if n % BLK == 0:
            part = a.reshape(n // BLK, BLK)
            rows.append(part if n // BLK == r else jnp.pad(part, ((0, r - n // BLK), (0, 0))))
        else:
            rows.append(jnp.pad(a.reshape(-1), (0, r * BLK - n)).reshape(r, BLK))
    return jnp.concatenate(rows, axis=0)


def _unpack(packed, shapes):
    out, row = [], 0
    for shp in shapes:
        n, r = math.prod(shp), _rows_of(shp)
        if n % BLK == 0:
            out.append(packed[row:row + n // BLK].reshape(shp))
        else:
            out.append(packed[row:row + r].reshape(-1)[:n].reshape(shp))
        row += r
    return out


def _block_diag(w):
    nb, bw, _ = w.shape
    eye = jnp.eye(nb, dtype=w.dtype)
    return (eye[:, None, :, None] * w[:, :, None, :]).reshape(nb * bw, nb * bw)


def _diag_blocks(wd, nb):
    bw = wd.shape[0] // nb
    return jnp.stack([wd[b * bw:(b + 1) * bw, b * bw:(b + 1) * bw] for b in range(nb)])


WEIGHT_NAMES = ['ffn1_pre_g', 'ffn1_w_gu', 'ffn1_w_down', 'ffn1_post_g', 'mix_pre_g', 'w_in', 'lru_conv_w', 'lru_conv_b',
                'lru_w_a', 'lru_b_a', 'lru_w_x', 'lru_b_x', 'lru_lambda', 'attn_sinks', 'conv_w', 'conv_b', 'conv_ln_g',
                'conv_ln_b', 'group_g', 'w_out', 'mix_post_g', 'ffn2_pre_g', 'ffn2_w_gu', 'ffn2_w_down', 'ffn2_post_g']
BIG = ('ffn1_w_gu', 'ffn1_w_down', 'w_in', 'w_out', 'ffn2_w_gu', 'ffn2_w_down')
TRANSPOSED = ('ffn1_w_gu', 'ffn2_w_gu', 'w_in')
SMALL = tuple(k for k in WEIGHT_NAMES if k not in BIG)
CHANNEL_SHARDED = ('lru_conv_w', 'conv_w')


def _step(x, target, w, m, v):
    n_l = w['ffn1_pre_g'].shape[0]
    assert n_l == 2, "the exchange schedule below is laid out for two layers"
    s, d = x.shape[1], x.shape[2]
    x = x.reshape(s, d)
    target = target.reshape(s, d)
    me = _my_pos()[3]
    tview = lambda t, k: jnp.swapaxes(t[k], 1, 2) if k in TRANSPOSED else t[k]
    wb = {k: tview(w, k).astype(BF16) for k in BIG}
    vec = lambda name, l: w[name][l][None, :]

    conv_shard = _pack([w['lru_conv_w'], w['conv_w']])
    g0 = _all_gather([(wb['ffn1_w_gu'], 0), (wb['ffn1_w_down'], 0), (wb['w_in'], 0), (wb['w_out'], 0),
                      (conv_shard, None)], "all_gather_first")
    wts = [dict(), dict()]
    wts[0]['ffn1_w_gu'], wts[0]['ffn1_w_down'], wts[0]['w_in'], wts[0]['w_out'], conv_g = g0
    ch = W_A // N_DEV
    conv_parts = [_unpack(conv_g[p], [(n_l, LRU_K, ch), (n_l, CC_K, ch)]) for p in range(N_DEV)]
    lru_cw = jnp.concatenate([cp[0] for cp in conv_parts], axis=-1)
    cc_cw = jnp.concatenate([cp[1] for cp in conv_parts], axis=-1)
    cc_cw = jnp.pad(cc_cw, ((0, 0), (0, CC_HALO - CC_K), (0, 0)))

    fc = wb['ffn1_w_gu'].shape[1]
    cut1, cut2 = (fc * 4 // 11 + 15) // 16 * 16, (fc * 27 // 44 + 15) // 16 * 16
    gather_plan = {
        ('ffn1', 0): [('A', 'f2_0', ('ffn2_w_gu', 'ffn2_w_down'), 0)],
        ('mix_in', 0): [('B', 'f2_0'), ('A', 'g1_1a', ('ffn1_w_gu',), 1, (0, cut1))],
        ('lru', 0): [('A', 'g1_1b', ('ffn1_w_gu',), 1, (cut1, cut2 - cut1), 'g1_1a')],
        ('attn', 0): [('A', 'g1_1', ('ffn1_w_gu',), 1, (cut2, fc - cut2), 'g1_1b')],
        ('cconv', 0): [('B', 'g1_1')],
        ('ffn2', 0): [('D', None, ('ffn1_w_down',), 1), ('A', 'wi_1', ('w_in',), 1), ('A', 'wo_1', ('w_out',), 1)],
        ('ffn1', 1): [('A', 'f2_1', ('ffn2_w_gu', 'ffn2_w_down'), 1), ('B', 'wi_1'), ('B', 'wo_1')],
        ('mix_in', 1): [('B', 'f2_1')],
    }
    pend = {}

    def fwd(kernel_name, l, fn, *args):
        plan = gather_plan.get((kernel_name, l), [])
        carry = []
        for st in plan:
            if st[0] == 'B':
                carry.append(_gather_b(pend[st[1]][2]))
            else:
                rows = st[4] if len(st) > 4 else None
                into = pend.pop(st[5])[2] if len(st) > 5 else [None] * len(st[2])
                carry.append(_gather_a([(wb[k], st[3], rows, buf) for k, buf in zip(st[2], into)],
                                       two_level=st[0] == 'A'))
        outs, ex = fn(*args, f"{kernel_name}_fwd_l{l}", carry)
        for st, bufs in zip(plan, ex):
            if st[0] == 'A':
                pend[st[1]] = (st[2], st[3], bufs)
            else:
                names, wl = (st[2], st[3]) if st[0] == 'D' else pend.pop(st[1])[:2]
                for k, b in zip(names, bufs):
                    wts[wl][k] = b
        return outs

    saved = []
    h = x
    for l in range(n_l):
        sv = {'x0': h}
        lw = wts[l]
        x1, sv['h1'], sv['g1'], sv['u1'], sv['d1'] = fwd(
            'ffn1', l, _ffn_fwd, h, vec('ffn1_pre_g', l), vec('ffn1_post_g', l), lw['ffn1_w_gu'], lw['ffn1_w_down'])
        sv['x1'] = x1
        sv['hn'], lx, lg, q, k, vv, glu = fwd('mix_in', l, _mix_in_fwd, x1, vec('mix_pre_g', l),
                                              lw['w_in'].reshape(D_IN_PROJ, d))
        sv.update(lx=lx, lg=lg, q=q, k=k, v=vv, glu=glu)
        lru_p = (lru_cw[l], vec('lru_conv_b', l), _block_diag(w['lru_w_a'][l]).astype(BF16), vec('lru_b_a', l),
                 _block_diag(w['lru_w_x'][l]).astype(BF16), vec('lru_b_x', l), vec('lru_lambda', l))
        cc_p = (cc_cw[l], vec('conv_b', l), vec('conv_ln_g', l), vec('conv_ln_b', l))
        sv.update(lru_p=lru_p, cc_p=cc_p)
        sv['ya'], sv['hs'] = fwd('lru', l, _lru_fwd, lx, lg, lru_p)
        sv['sink_row'] = jnp.repeat(w['attn_sinks'][l], BLK)[None, :]
        (sv['yb'],) = fwd('attn', l, _attn_fwd, q, k, vv, sv['sink_row'])
        (sv['yc'],) = fwd('cconv', l, _cc_fwd, glu, *cc_p)
        x2, sv['o'] = fwd('mix_out', l, _mix_out_fwd, x1, sv['ya'], sv['yb'], sv['yc'], vec('group_g', l),
                          lw['w_out'].reshape(-1, d), vec('mix_post_g', l))
        sv['x2'] = x2
        h, sv['h2'], sv['g2'], sv['u2'], sv['d2'] = fwd(
            'ffn2', l, _ffn_fwd, x2, vec('ffn2_pre_g', l), vec('ffn2_post_g', l), lw['ffn2_w_gu'], lw['ffn2_w_down'])
        saved.append(sv)

    loss_row, dh = _loss_head(h, target, "loss_head")

    recv = {k: None for k in BIG}
    ready = {}
    small = [dict() for _ in range(n_l)]

    c_even, c_odd = tuple(range(0, N_DEV, 2)), tuple(range(1, N_DEV, 2))

    def exchange(keys):
        return _grad_x([(ready[key[:2]], key[1], recv[key[0]]) + tuple(key[2:]) for key in keys], n_l)

    def received(keys, bufs):
        for key, b in zip(keys, bufs):
            recv[key[0]] = b

    def run(fn, *args, keys=()):
        outs, ex = fn(*args, carry=[exchange(keys)] if keys else [])
        if keys:
            received(keys, ex[0])
        return outs

    for l in reversed(range(n_l)):
        sv, sg, lw = saved[l], small[l], wts[l]
        keys = [] if l == n_l - 1 else [('ffn1_w_gu', l + 1)]
        dx2, dd, dg, du, sg['ffn2_pre_g'], sg['ffn2_post_g'] = run(
            _ffn_bwd_act, dh, sv['d2'], sv['x2'], vec('ffn2_pre_g', l), vec('ffn2_post_g', l), sv['g2'], sv['u2'],
            lw['ffn2_w_gu'], lw['ffn2_w_down'], f"ffn2_bwd_act_l{l}", keys=keys)
        keys = [] if l == n_l - 1 else [('ffn1_w_down', l + 1), ('w_in', l + 1, c_odd)]
        dwg, dwu, dwd = run(_ffn_bwd_w, sv['h2'], dd, sv['g2'], sv['u2'], dg, du, f"ffn2_bwd_w_l{l}", keys=keys)
        ready[('ffn2_w_gu', l)] = [dwg, dwu]
        ready[('ffn2_w_down', l)] = [dwd.reshape(N_DEV, -1, d)]
        dya, dyb, dyc, dw_out, sg['mix_post_g'], sg['group_g'] = run(
            _mix_out_bwd, dx2, sv['o'], sv['ya'], sv['yb'], sv['yc'], vec('group_g', l), lw['w_out'].reshape(-1, d),
            vec('mix_post_g', l), f"mix_out_bwd_l{l}")
        ready[('w_out', l)] = [dw_out.reshape(N_DEV, -1, d)]
        (dlx, dlg, sg['lru_conv_w'], sg['lru_conv_b'], dwa, sg['lru_b_a'], dwx, sg['lru_b_x'],
         sg['lru_lambda']) = run(_lru_bwd, dya, sv['lx'], sv['lg'], sv['hs'], sv['lru_p'], f"lru_bwd_l{l}")
        sg['lru_w_a'] = _diag_blocks(dwa, A_BLOCKS)
        sg['lru_w_x'] = _diag_blocks(dwx, A_BLOCKS)
        dq, dk, dv, dk_up, dv_up, dsk = run(_attn_bwd, dyb, sv['q'], sv['k'], sv['v'], sv['sink_row'],
                                            f"attn_bwd_l{l}", keys=[('ffn2_w_down', l, c_even)] if l == 0 else [])
        sg['attn_sinks'] = dsk[:, 0]
        dy1, dcw, sg['conv_b'], sg['conv_ln_g'], sg['conv_ln_b'] = run(
            _cc_bwd_conv, dyc, sv['glu'], *sv['cc_p'], f"cconv_bwd_conv_l{l}", keys=[('w_out', l)] if l == 0 else [])
        sg['conv_w'] = dcw[:CC_K]
        (dglu,) = run(_cc_bwd_glu, dy1, sv['glu'], sv['cc_p'][0], f"cconv_bwd_glu_l{l}")
        dx1, dw_in, sg['mix_pre_g'] = run(
            _mix_in_bwd, dx2, sv['x1'], vec('mix_pre_g', l), sv['hn'], lw['w_in'].reshape(D_IN_PROJ, d),
            dlx, dlg, dq, dk, dk_up, dv, dv_up, dglu, f"mix_in_bwd_l{l}",
            keys=[('ffn2_w_down', l, c_odd)] if l == 0 else [('w_out', l)])
        ready[('w_in', l)] = [dw_in.reshape(N_DEV, -1, d)]
        dh, dd, dg, du, sg['ffn1_pre_g'], sg['ffn1_post_g'] = run(
            _ffn_bwd_act, dx1, sv['d1'], sv['x0'], vec('ffn1_pre_g', l), vec('ffn1_post_g', l), sv['g1'], sv['u1'],
            lw['ffn1_w_gu'], lw['ffn1_w_down'], f"ffn1_bwd_act_l{l}",
            keys=[('ffn2_w_gu', l), ('w_in', l)] if l == 0 else [('ffn2_w_gu', l)])
        if l > 0:
            dwg, dwu, dwd = run(_ffn_bwd_w, sv['h1'], dd, sv['g1'], sv['u1'], dg, du, f"ffn1_bwd_w_l{l}",
                                keys=[('ffn2_w_down', l), ('w_in', l, c_even)])
            ready[('ffn1_w_gu', l)] = [dwg, dwu]
            ready[('ffn1_w_down', l)] = [dwd.reshape(N_DEV, -1, d)]
        else:
            part = _pack([jnp.stack([small[j][k] for j in range(n_l)]) for k in SMALL] + [loss_row])
            (recv['ffn1_w_gu'], recv['ffn1_w_down']), ex = _ffn_bwd_w_send(
                sv['h1'], dd, sv['g1'], sv['u1'], dg, du, recv['ffn1_w_gu'], recv['ffn1_w_down'], 0, "ffn1_bwd_w_send_l0",
                [_gather_a([(part, None)], two_level=False)])
            small_parts = ex[0][0]
    grad_x = dh.reshape(1, s, d)

    out = {}
    for k in BIG:
        res = _reduce_adamw(recv[k], tview(w, k), tview(m, k), tview(v, k), f"reduce_adamw_{k}")
        out[k] = [jnp.swapaxes(r, 1, 2) for r in res] if k in TRANSPOSED else res

    small_shapes = [(n_l,) + tuple(small[0][k].shape) for k in SMALL]

    def widen(t, k):
        if k not in CHANNEL_SHARDED:
            return t.reshape((n_l,) + tuple(small[0][k].shape))
        full = jnp.zeros((n_l,) + tuple(small[0][k].shape), F32)
        return lax.dynamic_update_slice_in_dim(full, t, me * ch, axis=2)

    no_w = jnp.zeros(loss_row.shape, F32)
    packed = [_pack([widen(src[k], k) for k in SMALL] + [no_w]) for src in (w, m, v)]
    res = _reduce_adamw_small(small_parts, *packed, "reduce_adamw_small")
    loss = _unpack(res[0], small_shapes + [loss_row.shape])[-1][0, 0]
    for k, g, dlt, nm, nv in zip(SMALL, *[_unpack(r, small_shapes) for r in res]):
        vals = [g, dlt, nm, nv]
        if k in CHANNEL_SHARDED:
            vals = [lax.dynamic_slice_in_dim(t, me * ch, ch, axis=2) for t in vals]
        out[k] = [t.reshape(w[k].shape) for t in vals]

    return (loss, grad_x, *[out[k][0] for k in WEIGHT_NAMES], *[out[k][1] for k in WEIGHT_NAMES],
            *[out[k][2] for k in WEIGHT_NAMES], *[out[k][3] for k in WEIGHT_NAMES])


def kernel(x, ffn1_pre_g, ffn1_w_gu, ffn1_w_down, ffn1_post_g, mix_pre_g, w_in, lru_conv_w, lru_conv_b, lru_w_a, lru_b_a, lru_w_x, lru_b_x, lru_lambda, attn_sinks, conv_w, conv_b, conv_ln_g, conv_ln_b, group_g, w_out, mix_post_g, ffn2_pre_g, ffn2_w_gu, ffn2_w_down, ffn2_post_g, loss_target, m_ffn1_pre_g, m_ffn1_w_gu, m_ffn1_w_down, m_ffn1_post_g, m_mix_pre_g, m_w_in, m_lru_conv_w, m_lru_conv_b, m_lru_w_a, m_lru_b_a, m_lru_w_x, m_lru_b_x, m_lru_lambda, m_attn_sinks, m_conv_w, m_conv_b, m_conv_ln_g, m_conv_ln_b, m_group_g, m_w_out, m_mix_post_g, m_ffn2_pre_g, m_ffn2_w_gu, m_ffn2_w_down, m_ffn2_post_g, v_ffn1_pre_g, v_ffn1_w_gu, v_ffn1_w_down, v_ffn1_post_g, v_mix_pre_g, v_w_in, v_lru_conv_w, v_lru_conv_b, v_lru_w_a, v_lru_b_a, v_lru_w_x, v_lru_b_x, v_lru_lambda, v_attn_sinks, v_conv_w, v_conv_b, v_conv_ln_g, v_conv_ln_b, v_group_g, v_w_out, v_mix_post_g, v_ffn2_pre_g, v_ffn2_w_gu, v_ffn2_w_down, v_ffn2_post_g):
    args = (ffn1_pre_g, ffn1_w_gu, ffn1_w_down, ffn1_post_g, mix_pre_g, w_in, lru_conv_w, lru_conv_b, lru_w_a, lru_b_a, lru_w_x, lru_b_x, lru_lambda, attn_sinks, conv_w, conv_b, conv_ln_g, conv_ln_b, group_g, w_out, mix_post_g, ffn2_pre_g, ffn2_w_gu, ffn2_w_down, ffn2_post_g)
    ms = (m_ffn1_pre_g, m_ffn1_w_gu, m_ffn1_w_down, m_ffn1_post_g, m_mix_pre_g, m_w_in, m_lru_conv_w, m_lru_conv_b, m_lru_w_a, m_lru_b_a, m_lru_w_x, m_lru_b_x, m_lru_lambda, m_attn_sinks, m_conv_w, m_conv_b, m_conv_ln_g, m_conv_ln_b, m_group_g, m_w_out, m_mix_post_g, m_ffn2_pre_g, m_ffn2_w_gu, m_ffn2_w_down, m_ffn2_post_g)
    vs = (v_ffn1_pre_g, v_ffn1_w_gu, v_ffn1_w_down, v_ffn1_post_g, v_mix_pre_g, v_w_in, v_lru_conv_w, v_lru_conv_b, v_lru_w_a, v_lru_b_a, v_lru_w_x, v_lru_b_x, v_lru_lambda, v_attn_sinks, v_conv_w, v_conv_b, v_conv_ln_g, v_conv_ln_b, v_group_g, v_w_out, v_mix_post_g, v_ffn2_pre_g, v_ffn2_w_gu, v_ffn2_w_down, v_ffn2_post_g)
    return _step(x, loss_target, dict(zip(WEIGHT_NAMES, args)), dict(zip(WEIGHT_NAMES, ms)), dict(zip(WEIGHT_NAMES, vs)))
```

```python
import functools
import math
import operator

import jax
import jax.numpy as jnp
from jax import lax
from jax.experimental import pallas as pl
from jax.experimental.pallas import tpu as pltpu

F32 = jnp.float32
BF16 = jnp.bfloat16
N_DEV = 8
AXES = ("x", "y", "c")
MESH = pl.DeviceIdType.MESH

NORM_EPS = 1e-6
LN_EPS = 1e-5
NEG_BIG = -1e30
W_A = 256
W_B = 512
W_C = 256
HEAD_DIM = 64
N_Q_HEADS = 8
N_KV_HEADS = 2
Q_PER_KV = N_Q_HEADS // N_KV_HEADS
KV_W = N_KV_HEADS * HEAD_DIM
BLK = 128
LRU_K = 4
LRU_C = 8.0
A_BLOCKS = 4
CC_K = 31
CC_HALO = 32
LRU_HALO = 8
D_IN_PROJ = 2 * W_A + W_B + 2 * KV_W + 2 * W_C
ADAM_LR = 0.001
ADAM_B1 = 0.9
ADAM_B2 = 0.999
ADAM_EPS = 1e-08
ADAM_WD = 0.01
ADAM_STEP = 10
VMEM_LIMIT = 56 * 1024 * 1024

SDS = jax.ShapeDtypeStruct
ANY = pl.BlockSpec(memory_space=pl.ANY)


def _time_tile(s):
    return max(BLK, s // 8)


class _Exchange:
    def __init__(self, inputs, out_shapes, aliases, sem_shapes, start, wait):
        self.inputs, self.out_shapes, self.aliases, self.sem_shapes = inputs, out_shapes, aliases, sem_shapes
        self.start, self.wait = start, wait


def _my_pos():
    x, y, c = (lax.axis_index(a) for a in AXES)
    return x, y, c, 4 * x + 2 * y + c


def _flip(k):
    x, y, c, _ = _my_pos()
    return (1 - x if k & 4 else x, 1 - y if k & 2 else y, 1 - c if k & 1 else c)


def _slot(dev):
    return 4 * dev[0] + 2 * dev[1] + dev[2]


def _dev(p):
    return (p >> 2, (p >> 1) & 1, p & 1)


def _gather_a(items, two_level):
    rels = (1, 2, 4, 6) if two_level else tuple(range(1, N_DEV))
    items = [tuple(it) + (None,) * (4 - len(it)) for it in items]
    n = len(items)
    with_buf = [a for a in range(n) if items[a][3] is not None]

    def rows_of(ref, a):
        return ref if items[a][2] is None else ref.at[pl.ds(*items[a][2])]

    def src_of(ins, a):
        return rows_of(ins[a] if items[a][1] is None else ins[a].at[items[a][1]], a)

    def dst_of(outs, a, slot):
        return rows_of(outs[a].at[slot], a)

    def shape_of(a):
        arr, l = items[a][:2]
        return arr.shape if l is None else arr.shape[1:]

    def copies(ins, outs, sems, a):
        send, recv, _ = sems
        me = _my_pos()[3]
        return [(k, pltpu.make_async_remote_copy(
            src_ref=src_of(ins, a), dst_ref=dst_of(outs, a, me), send_sem=send.at[a, k], recv_sem=recv.at[a, k],
            device_id=_flip(k), device_id_type=MESH)) for k in rels]

    def local(ins, outs, sems, a):
        return pltpu.make_async_copy(src_of(ins, a), dst_of(outs, a, _my_pos()[3]), sems[2].at[a])

    def start(ins, outs, sems):
        for a in range(n):
            local(ins, outs, sems, a).start()
            for _, cp in copies(ins, outs, sems, a):
                cp.start()

    def wait(ins, outs, sems):
        send, recv, _ = sems
        for a in range(n):
            for k, cp in copies(ins, outs, sems, a):
                pltpu.make_async_remote_copy(
                    src_ref=src_of(ins, a), dst_ref=dst_of(outs, a, _slot(_flip(k))), send_sem=send.at[a, k],
                    recv_sem=recv.at[a, k], device_id=_flip(k), device_id_type=MESH).wait_recv()
                cp.wait_send()
            local(ins, outs, sems, a).wait()

    return _Exchange([it[0] for it in items] + [items[a][3] for a in with_buf],
                     [SDS((N_DEV,) + shape_of(a), items[a][0].dtype) for a in range(n)],
                     {n + j: a for j, a in enumerate(with_buf)},
                     [pltpu.SemaphoreType.DMA((n, N_DEV)), pltpu.SemaphoreType.DMA((n, N_DEV)),
                      pltpu.SemaphoreType.DMA((n,))], start, wait)


def _gather_b(bufs):
    n = len(bufs)

    def copies(ins, outs, sems, a, c_of_block):
        send, recv = sems
        x, y, c, _ = _my_pos()
        res = []
        for k in (2, 4, 6):
            chip = _flip(k)
            blk = _slot((chip[0], chip[1], c if c_of_block == "mine" else 1 - c))
            res.append(pltpu.make_async_remote_copy(
                src_ref=ins[a].at[blk], dst_ref=outs[a].at[blk], send_sem=send.at[a, k], recv_sem=recv.at[a, k],
                device_id=_flip(1), device_id_type=MESH))
        return res

    def start(ins, outs, sems):
        for a in range(n):
            for cp in copies(ins, outs, sems, a, "mine"):
                cp.start()

    def wait(ins, outs, sems):
        for a in range(n):
            for cp in copies(ins, outs, sems, a, "sibling"):
                cp.wait_recv()
            for cp in copies(ins, outs, sems, a, "mine"):
                cp.wait_send()

    return _Exchange(list(bufs), [SDS(b.shape, b.dtype) for b in bufs], {a: a for a in range(n)},
                     [pltpu.SemaphoreType.DMA((n, N_DEV)), pltpu.SemaphoreType.DMA((n, N_DEV))], start, wait)


def _grad_x(items, n_l):
    items = [tuple(it) + (None,) * (4 - len(it)) for it in items]
    n = len(items)
    owners = [tuple(range(N_DEV)) if it[3] is None else tuple(it[3]) for it in items]
    inputs, first_in, aliases, out_shapes = [], [], {}, []
    for a, (arrs, l, recv, _) in enumerate(items):
        first_in.append(len(inputs))
        inputs += list(arrs)
        assert sum(arr.shape[0] for arr in arrs) == N_DEV
        if recv is not None:
            aliases[len(inputs)] = a
            inputs.append(recv)
        out_shapes.append(SDS((N_DEV, n_l) + arrs[0].shape[1:], arrs[0].dtype))

    def slab(ins, a, p):
        off = 0
        for j, arr in enumerate(items[a][0]):
            if p < off + arr.shape[0]:
                return ins[first_in[a] + j].at[p - off]
            off += arr.shape[0]
        raise AssertionError

    def rdma(ins, outs, sems, a, p, src_dev):
        send, recv, _ = sems
        return pltpu.make_async_remote_copy(
            src_ref=slab(ins, a, p), dst_ref=outs[a].at[src_dev, items[a][1]], send_sem=send.at[a, p],
            recv_sem=recv.at[a, src_dev], device_id=_dev(p), device_id_type=MESH)

    def local(ins, outs, sems, a, p):
        return pltpu.make_async_copy(slab(ins, a, p), outs[a].at[p, items[a][1]], sems[2].at[a])

    def start(ins, outs, sems):
        me = _my_pos()[3]
        for p in range(N_DEV):
            mine = [a for a in range(n) if p in owners[a]]

            @pl.when(me != p)
            def _():
                for a in mine:
                    rdma(ins, outs, sems, a, p, me).start()

            @pl.when(me == p)
            def _():
                for a in mine:
                    local(ins, outs, sems, a, p).start()

    def wait(ins, outs, sems):
        me = _my_pos()[3]
        for a in range(n):
            i_own = functools.reduce(operator.or_, [me == q for q in owners[a]])
            for p in range(N_DEV):
                @pl.when((me != p) & i_own)
                def _():
                    rdma(ins, outs, sems, a, p, p).wait_recv()

                if p in owners[a]:
                    @pl.when(me != p)
                    def _():
                        rdma(ins, outs, sems, a, p, p).wait_send()

                    @pl.when(me == p)
                    def _():
                        local(ins, outs, sems, a, p).wait()

    return _Exchange(inputs, out_shapes, aliases,
                     [pltpu.SemaphoreType.DMA((n, N_DEV)), pltpu.SemaphoreType.DMA((n, N_DEV)),
                      pltpu.SemaphoreType.DMA((n,))], start, wait)


def _pcall(body, args, *, name, grid, in_specs, out_specs, out_shape, scratch_shapes=(), carry=(), body_aliases=None):
    n_in, n_out, n_scr = len(in_specs), len(out_specs), len(scratch_shapes)
    c_in = [len(e.inputs) for e in carry]
    c_out = [len(e.out_shapes) for e in carry]
    c_sem = [len(e.sem_shapes) for e in carry]
    aliases = dict(body_aliases or {})
    for j, e in enumerate(carry):
        for i_loc, o_loc in e.aliases.items():
            aliases[n_in + sum(c_in[:j]) + i_loc] = n_out + sum(c_out[:j]) + o_loc

    def wrapped(*refs):
        def take(counts, pos):
            groups = []
            for cnt in counts:
                groups.append(refs[pos:pos + cnt])
                pos += cnt
            return groups, pos

        (ins,), pos = take([n_in], 0)
        cins, pos = take(c_in, pos)
        (outs,), pos = take([n_out], pos)
        couts, pos = take(c_out, pos)
        (scr,), pos = take([n_scr], pos)
        csems, pos = take(c_sem, pos)
        if carry:
            ids = [pl.program_id(k) for k in range(len(grid))]
            first = functools.reduce(operator.and_, [i == 0 for i in ids])
            last = functools.reduce(operator.and_, [i == g - 1 for i, g in zip(ids, grid)])

            @pl.when(first)
            def _():
                for e, ci, co, cs in zip(carry, cins, couts, csems):
                    e.start(ci, co, cs)

        body(*ins, *outs, *scr)
        if carry:
            @pl.when(last)
            def _():
                for e, ci, co, cs in zip(carry, cins, couts, csems):
                    e.wait(ci, co, cs)

    res = pl.pallas_call(
        wrapped, name=name, grid=grid,
        in_specs=list(in_specs) + [ANY] * sum(c_in),
        out_specs=list(out_specs) + [ANY] * sum(c_out),
        out_shape=list(out_shape) + [s for e in carry for s in e.out_shapes],
        scratch_shapes=list(scratch_shapes) + [s for e in carry for s in e.sem_shapes],
        input_output_aliases=aliases,
        compiler_params=pltpu.CompilerParams(dimension_semantics=("arbitrary",) * len(grid),
                                             vmem_limit_bytes=VMEM_LIMIT),
    )(*args, *[a for e in carry for a in e.inputs])
    outs, pos, extra = list(res[:n_out]), n_out, []
    for cnt in c_out:
        extra.append(list(res[pos:pos + cnt]))
        pos += cnt
    return outs, extra


def _all_gather(items, name):
    n = len(items)
    shape_of = lambda a: items[a][0].shape if items[a][1] is None else items[a][0].shape[1:]

    def body(*refs):
        ins, outs, (send_sems, recv_sems, local_sems) = refs[:n], refs[n:2 * n], refs[2 * n:]
        x, y, c, me = _my_pos()
        src_of = lambda a: ins[a] if items[a][1] is None else ins[a].at[items[a][1]]

        def copy(a, k, block, to, src=None):
            dst = outs[a].at[_slot(block)]
            return pltpu.make_async_remote_copy(
                src_ref=dst if src is None else src, dst_ref=dst,
                send_sem=send_sems.at[a, k], recv_sem=recv_sems.at[a, k], device_id=to, device_id_type=MESH)

        mine = [pltpu.make_async_copy(src_of(a), outs[a].at[me], local_sems.at[a]) for a in range(n)]
        for cp in mine:
            cp.start()
        first = [copy(a, k, (x, y, c), _flip(k), src=src_of(a)) for a in range(n) for k in (1, 2, 4, 6)]
        for cp in first:
            cp.start()
        passed = []
        for k in (2, 4, 6):
            for a in range(n):
                copy(a, k, _flip(k), (x, y, c)).wait_recv()
                fwd = copy(a, k + 1, _flip(k), _flip(1))
                fwd.start()
                passed.append(fwd)
        for a in range(n):
            copy(a, 1, _flip(1), (x, y, c)).wait_recv()
            for k in (2, 4, 6):
                copy(a, k + 1, _flip(k + 1), (x, y, c)).wait_recv()
        for cp in first + passed:
            cp.wait_send()
        for cp in mine:
            cp.wait()

    return pl.pallas_call(
        body, name=name,
        in_specs=[ANY] * n, out_specs=[ANY] * n,
        out_shape=[SDS((N_DEV,) + shape_of(a), items[a][0].dtype) for a in range(n)],
        scratch_shapes=[pltpu.SemaphoreType.DMA((n, N_DEV)), pltpu.SemaphoreType.DMA((n, N_DEV)),
                        pltpu.SemaphoreType.DMA((n,))],
    )(*[it[0] for it in items])


def _mm(a, b):
    return jnp.dot(a.astype(BF16), b.astype(BF16), preferred_element_type=F32)


def _mm_nt(a, b):
    return lax.dot_general(a.astype(BF16), b.astype(BF16), (((1,), (1,)), ((), ())), preferred_element_type=F32)


def _mm_tn(a, b):
    return lax.dot_general(a.astype(BF16), b.astype(BF16), (((0,), (0,)), ((), ())), preferred_element_type=F32)


def _rms_r(x):
    return lax.rsqrt(jnp.mean(x * x, axis=-1, keepdims=True) + NORM_EPS)


def _rms_bwd(x, r, g, dy):
    gy = dy * g
    dx = r * (gy - x * (r * r) * jnp.mean(gy * x, axis=-1, keepdims=True))
    dg = jnp.sum(dy * x * r, axis=0, keepdims=True)
    return dx, dg


def _sigmoid(x):
    return 1.0 / (1.0 + jnp.exp(-x))


def _sigmoid_t(x):
    return 0.5 * jnp.tanh(0.5 * x) + 0.5


def _dsilu(z, sz):
    return sz * (1.0 + z * (1.0 - sz))


def _swiglu_bf16(g, u):
    sg = 0.5 * jnp.tanh(0.5 * g) + 0.5
    silu = g * sg
    return silu * u, silu, sg + silu * (1.0 - sg)


_GELU_C = math.sqrt(2.0 / math.pi)


def _gelu(x):
    t = jnp.tanh(_GELU_C * (x + 0.044715 * x * x * x))
    return 0.5 * x * (1.0 + t), t


def _dgelu(x, t):
    return 0.5 * (1.0 + t) + 0.5 * x * (1.0 - t * t) * _GELU_C * (1.0 + 3.0 * 0.044715 * x * x)


def _log1p(e):
    return jnp.where(e < 1e-2, e * (1.0 - e * (0.5 - e * (1.0 / 3.0))), jnp.log(1.0 + e))


def _softplus(x):
    return jnp.maximum(x, 0.0) + _log1p(jnp.exp(-jnp.abs(x)))


def _neg_expm1(x):
    small = -x * (1.0 + x * (0.5 + x * (1.0 / 6.0) * (1.0 + x * 0.25)))
    return jnp.where(x > -1e-2, small, 1.0 - jnp.exp(x))


def _shift_down(x, s):
    return x if s == 0 else pltpu.roll(x, s, 0)


def _shift_up(x, s):
    return x if s == 0 else pltpu.roll(x, x.shape[0] - s, 0)


def _ffn_wspecs(d, fc, order):
    f_of = (lambda i, f: f) if order == "tf" else (lambda f, i: f)
    n_f = N_DEV // 2
    return [pl.BlockSpec((None, fc, d), lambda *g: (f_of(*g), 0, 0)),
            pl.BlockSpec((None, fc, d), lambda *g: (f_of(*g) + n_f, 0, 0)),
            pl.BlockSpec((2, fc // 2, d), lambda *g: (f_of(*g), 0, 0))]


def _ffn_fwd(x, pre_g, post_g, wgu_t, wd, name, carry=()):
    s, d = x.shape
    fc = wgu_t.shape[1]
    ts = 2 * _time_tile(s)
    n_t, n_f = s // ts, N_DEV // 2

    def body(x_ref, pg_ref, qg_ref, wg_ref, wu_ref, wd_ref, xo_ref, h_ref, g_ref, u_ref, d_ref, h_scr, acc):
        f = pl.program_id(1)

        @pl.when(f == 0)
        def _():
            xv = x_ref[...]
            hv = (xv * _rms_r(xv) * pg_ref[...]).astype(BF16)
            h_scr[...] = hv
            h_ref[...] = hv
            acc[...] = jnp.zeros_like(acc)

        hv = h_scr[...]
        g = _mm_nt(hv, wg_ref[...])
        u = _mm_nt(hv, wu_ref[...])
        g = g.astype(BF16)
        u = u.astype(BF16)
        g_ref[...] = g
        u_ref[...] = u
        acc[...] += jnp.dot(_swiglu_bf16(g, u)[0], wd_ref[...].reshape(fc, d), preferred_element_type=F32)

        @pl.when(f == n_f - 1)
        def _():
            dv = acc[...]
            d_ref[...] = dv.astype(BF16)
            xo_ref[...] = x_ref[...] + 0.5 * (dv * _rms_r(dv) * qg_ref[...])

    row = pl.BlockSpec((ts, d), lambda i, f: (i, 0))
    vec = pl.BlockSpec((1, d), lambda i, f: (0, 0))
    act = pl.BlockSpec((None, ts, fc), lambda i, f: (f, i, 0))
    return _pcall(
        body, (x, pre_g, post_g, wgu_t, wgu_t, wd), name=name, grid=(n_t, n_f),
        in_specs=[row, vec, vec] + _ffn_wspecs(d, fc, "tf"),
        out_specs=[row, row, act, act, row],
        out_shape=[SDS((s, d), F32), SDS((s, d), BF16), SDS((n_f, s, fc), BF16), SDS((n_f, s, fc), BF16),
                   SDS((s, d), BF16)],
        scratch_shapes=[pltpu.VMEM((ts, d), BF16), pltpu.VMEM((ts, d), F32)], carry=carry)


def _ffn_bwd_act(dxo, dmid, x, pre_g, post_g, g_s, u_s, wgu_t, wd, name, carry=(), loss_target=None):
    s, d = x.shape
    fc = wgu_t.shape[1]
    ts = _time_tile(s)
    n_t, n_f = s // ts, N_DEV // 2
    with_loss = loss_target is not None

    def body(*refs):
        t_ref, refs = (refs[0], refs[1:]) if with_loss else (None, refs)
        (dxo_ref, dm_ref, x_ref, pg_ref, qg_ref, g_ref, u_ref, wg_ref, wu_ref, wd_ref,
         dx_ref, dd_ref, dg_ref, du_ref, dpg_ref, dqg_ref) = refs[:16]
        loss_ref = refs[16] if with_loss else None
        dd_scr, dh_acc = refs[-2:]
        i, f = pl.program_id(0), pl.program_id(1)

        def incoming():
            return (dxo_ref[...] - t_ref[...]) * (1.0 / d) if with_loss else dxo_ref[...]

        @pl.when((i == 0) & (f == 0))
        def _():
            dpg_ref[...] = jnp.zeros_like(dpg_ref)
            dqg_ref[...] = jnp.zeros_like(dqg_ref)
            if with_loss:
                loss_ref[...] = jnp.zeros_like(loss_ref)

        @pl.when(f == 0)
        def _():
            if with_loss:
                err = dxo_ref[...] - t_ref[...]
                loss_ref[...] += 0.5 * jnp.sum(jnp.mean(err * err, axis=-1, keepdims=True), axis=0, keepdims=True)
            dv = dm_ref[...].astype(F32)
            ddv, dq = _rms_bwd(dv, _rms_r(dv), qg_ref[...], 0.5 * incoming())
            dqg_ref[...] += dq
            dd_scr[...] = ddv.astype(BF16)
            dd_ref[...] = ddv.astype(BF16)
            dh_acc[...] = jnp.zeros_like(dh_acc)

        da = _mm_nt(dd_scr[...], wd_ref[...].reshape(fc, d)).astype(BF16)
        u = u_ref[...]
        _, silu, dsilu = _swiglu_bf16(g_ref[...], u)
        du = da * silu
        dg = da * u * dsilu
        dg_ref[...] = dg
        du_ref[...] = du
        dh_acc[...] += _mm(dg, wg_ref[...]) + _mm(du, wu_ref[...])

        @pl.when(f == n_f - 1)
        def _():
            xv = x_ref[...]
            dxv, dp = _rms_bwd(xv, _rms_r(xv), pg_ref[...], dh_acc[...])
            dpg_ref[...] += dp
            dx_ref[...] = incoming() + dxv

    row = pl.BlockSpec((ts, d), lambda i, f: (i, 0))
    vec = pl.BlockSpec((1, d), lambda i, f: (0, 0))
    act = pl.BlockSpec((None, ts, fc), lambda i, f: (f, i, 0))
    lead = [loss_target] if with_loss else []
    return _pcall(
        body, (*lead, dxo, dmid, x, pre_g, post_g, g_s, u_s, wgu_t, wgu_t, wd), name=name, grid=(n_t, n_f),
        in_specs=[row] * len(lead) + [row, row, row, vec, vec, act, act] + _ffn_wspecs(d, fc, "tf"),
        out_specs=[row, row, act, act, vec, vec] + [pl.BlockSpec((1, BLK), lambda i, f: (0, 0))] * len(lead),
        out_shape=[SDS((s, d), F32), SDS((s, d), BF16), SDS((n_f, s, fc), BF16), SDS((n_f, s, fc), BF16),
                   SDS((1, d), F32), SDS((1, d), F32)] + [SDS((1, BLK), F32)] * len(lead),
        scratch_shapes=[pltpu.VMEM((ts, d), BF16), pltpu.VMEM((ts, d), F32)], carry=carry)


def _ffn_bwd_w(h, dd, g_s, u_s, dg, du, name, carry=()):
    s, d = h.shape
    n_f, _, fc = g_s.shape
    ts = 2 * _time_tile(s)
    n_t = s // ts

    def body(h_ref, dd_ref, g_ref, u_ref, dg_ref, du_ref, wg_ref, wu_ref, wd_ref, acc_g, acc_u, acc_d):
        i = pl.program_id(1)

        @pl.when(i == 0)
        def _():
            acc_g[...] = jnp.zeros_like(acc_g)
            acc_u[...] = jnp.zeros_like(acc_u)
            acc_d[...] = jnp.zeros_like(acc_d)

        a = _swiglu_bf16(g_ref[...], u_ref[...])[0]
        hv = h_ref[...]
        acc_g[...] += _mm_tn(dg_ref[...], hv)
        acc_u[...] += _mm_tn(du_ref[...], hv)
        acc_d[...] += _mm_tn(a, dd_ref[...])

        @pl.when(i == n_t - 1)
        def _():
            wg_ref[...] = acc_g[...].astype(BF16)
            wu_ref[...] = acc_u[...].astype(BF16)
            wd_ref[...] = acc_d[...].astype(BF16)

    row = pl.BlockSpec((ts, d), lambda f, i: (i, 0))
    act = pl.BlockSpec((None, ts, fc), lambda f, i: (f, i, 0))
    out = pl.BlockSpec((None, fc, d), lambda f, i: (f, 0, 0))
    return _pcall(
        body, (h, dd, g_s, u_s, dg, du), name=name, grid=(n_f, n_t),
        in_specs=[row, row, act, act, act, act], out_specs=[out, out, out],
        out_shape=[SDS((n_f, fc, d), BF16)] * 3,
        scratch_shapes=[pltpu.VMEM((fc, d), F32)] * 3, carry=carry)


def _ffn_bwd_w_send(h, dd, g_s, u_s, dg, du, recv_gu, recv_d, layer, name, carry=()):
    s, d = h.shape
    n_f, _, fc = g_s.shape
    ts = _time_tile(s)
    n_t = s // ts
    half = fc // 2

    def chunk_of(step):
        return (step + 2 * lax.axis_index("x") + lax.axis_index("y")) % n_f

    def body(h_ref, dd_ref, g_ref, u_ref, dg_ref, du_ref, _rgu_in, _rd_in, rgu_ref, rd_ref,
             acc_g, acc_u, acc_d, st_g, st_u, st_d, pair_gu, pair_d, zeros,
             send_sems, recv_sems, local_sems, pair_sems, zero_sems):
        f, i = pl.program_id(0), pl.program_id(1)
        x, y, c_me, me = _my_pos()
        sibling = me ^ 1

        @pl.when(i == 0)
        def _():
            acc_g[...] = jnp.zeros_like(acc_g)
            acc_u[...] = jnp.zeros_like(acc_u)
            acc_d[...] = jnp.zeros_like(acc_d)

        def zero_fills():
            res = []
            for n_k, k in enumerate((2, 4, 6)):
                other = _flip(k)
                slot = _slot((other[0], other[1], 1 - c_me))
                res += [pltpu.make_async_copy(zeros, rgu_ref.at[slot, layer, pl.ds(0, half)], zero_sems.at[n_k, 0]),
                        pltpu.make_async_copy(zeros, rgu_ref.at[slot, layer, pl.ds(half, half)], zero_sems.at[n_k, 1]),
                        pltpu.make_async_copy(zeros, rd_ref.at[slot, layer], zero_sems.at[n_k, 2])]
            return res

        @pl.when((f == 0) & (i == 0))
        def _():
            zeros[...] = jnp.zeros_like(zeros)
            for cp in zero_fills():
                cp.start()

        a = _swiglu_bf16(g_ref[...], u_ref[...])[0]
        hv = h_ref[...]
        acc_g[...] += _mm_tn(dg_ref[...], hv)
        acc_u[...] += _mm_tn(du_ref[...], hv)
        acc_d[...] += _mm_tn(a, dd_ref[...])

        def messages(fs):
            c = chunk_of(fs)
            lo, hi = pl.ds(0, half), pl.ds(half, half)
            return [(st_g.at[fs], pair_gu.at[fs // 2, 0], rgu_ref, 0, c, 0),
                    (st_u.at[fs], pair_gu.at[fs // 2, 1], rgu_ref, 0, c + n_f, 1),
                    (st_d.at[fs, lo], pair_d.at[fs], rd_ref, 1, 2 * c, 2),
                    (st_d.at[fs, hi], pair_d.at[fs], rd_ref, 1, 2 * c + 1, 3)]

        def roles(p):
            same_chip = (p >> 1) == (me >> 1)
            same_c = (p & 1) == c_me
            return p == me, p == sibling, (~same_chip) & same_c, (~same_chip) & (~same_c)

        def to_owner(fs, msg, src_dev):
            src, _, buf, row, p, j = msg
            return pltpu.make_async_remote_copy(
                src_ref=src, dst_ref=buf.at[src_dev, layer], send_sem=send_sems.at[fs, j],
                recv_sem=recv_sems.at[row, src_dev], device_id=_dev(p), device_id_type=MESH)

        def to_pair(fs, msg):
            src, pair, _, _, _, j = msg
            return pltpu.make_async_remote_copy(
                src_ref=src, dst_ref=pair, send_sem=send_sems.at[fs, j], recv_sem=pair_sems.at[fs, j],
                device_id=_dev(sibling), device_id_type=MESH)

        def local(fs, msg):
            src, _, buf, _, p, j = msg
            return pltpu.make_async_copy(src, buf.at[p, layer], local_sems.at[fs, j])

        for fs in range(n_f):
            @pl.when((f == fs) & (i == n_t - 1))
            def _():
                st_g[fs] = acc_g[...].astype(BF16)
                st_u[fs] = acc_u[...].astype(BF16)
                st_d[fs] = acc_d[...].astype(BF16)
                msgs = messages(fs)
                for msg in msgs:
                    mine, sib, _, hand_over = roles(msg[4])

                    @pl.when(mine)
                    def _():
                        local(fs, msg).start()

                    @pl.when(sib)
                    def _():
                        to_owner(fs, msg, me).start()

                    @pl.when(hand_over)
                    def _():
                        to_pair(fs, msg).start()
                for msg in msgs:
                    @pl.when(roles(msg[4])[2])
                    def _():
                        src, pair = msg[0], msg[1]
                        to_pair(fs, msg).wait_recv()
                        src[...] = (src[...].astype(F32) + pair[...].astype(F32)).astype(BF16)
                        to_owner(fs, msg, me).start()

        @pl.when((f == n_f - 1) & (i == n_t - 1))
        def _():
            for fs in range(n_f):
                for msg in messages(fs):
                    mine = roles(msg[4])[0]

                    @pl.when(mine)
                    def _():
                        local(fs, msg).wait()

                    @pl.when(~mine)
                    def _():
                        to_owner(fs, msg, me).wait_send()
            for k in (1, 2, 4, 6):
                src_dev = _slot(_flip(k))
                to_owner(0, messages(0)[0], src_dev).wait_recv()
                to_owner(0, messages(0)[2], src_dev).wait_recv()
            for cp in zero_fills():
                cp.wait()

    row = pl.BlockSpec((ts, d), lambda f, i: (i, 0))
    act = pl.BlockSpec((None, ts, fc), lambda f, i: (chunk_of(f), i, 0))
    return _pcall(
        body, (h, dd, g_s, u_s, dg, du, recv_gu, recv_d), name=name, grid=(n_f, n_t),
        in_specs=[row, row, act, act, act, act, ANY, ANY], out_specs=[ANY, ANY],
        out_shape=[SDS(recv_gu.shape, recv_gu.dtype), SDS(recv_d.shape, recv_d.dtype)],
        scratch_shapes=[pltpu.VMEM((fc, d), F32)] * 3 + [pltpu.VMEM((n_f, fc, d), BF16)] * 3
        + [pltpu.VMEM((n_f // 2, 2, fc, d), BF16), pltpu.VMEM((n_f, half, d), BF16), pltpu.VMEM((half, d), BF16)]
        + [pltpu.SemaphoreType.DMA((n_f, 4)), pltpu.SemaphoreType.DMA((2, N_DEV)), pltpu.SemaphoreType.DMA((n_f, 4)),
           pltpu.SemaphoreType.DMA((n_f, 4)), pltpu.SemaphoreType.DMA((3, 3))],
        carry=carry, body_aliases={6: 0, 7: 1})


_PROJ_WIDTHS = (W_A, W_A, W_B, KV_W, KV_W, 2 * W_C)


def _mix_in_fwd(x, pre_g, w_in_t, name, carry=()):
    s, d = x.shape
    ts = 2 * _time_tile(s)

    def body(x_ref, pg_ref, w_ref, hn_ref, *outs):
        xv = x_ref[...]
        hn = (xv * _rms_r(xv) * pg_ref[...]).astype(BF16)
        hn_ref[...] = hn
        proj = _mm_nt(hn, w_ref[...])
        off = 0
        for o_ref, w in zip(outs, _PROJ_WIDTHS):
            o_ref[...] = proj[:, off:off + w]
            off += w

    row = lambda w: pl.BlockSpec((ts, w), lambda i: (i, 0))
    return _pcall(
        body, (x, pre_g, w_in_t), name=name, grid=(s // ts,),
        in_specs=[row(d), pl.BlockSpec((1, d), lambda i: (0, 0)), pl.BlockSpec((D_IN_PROJ, d), lambda i: (0, 0))],
        out_specs=[row(d)] + [row(w) for w in _PROJ_WIDTHS],
        out_shape=[SDS((s, d), BF16)] + [SDS((s, w), F32) for w in _PROJ_WIDTHS], carry=carry)


def _mix_in_bwd(dres, x, pre_g, hn, w_in_t, dlx, dlg, dq, dk, dk_up, dv, dv_up, dglu, name, carry=()):
    s, d = x.shape
    ts = _time_tile(s)
    n_t = s // ts

    def body(dres_ref, x_ref, pg_ref, hn_ref, w_ref, dlx_ref, dlg_ref, dq_ref, dk_ref, dkn_ref,
             dv_ref, dvn_ref, dglu_ref, dx_ref, dw_ref, dpg_ref, acc):
        i = pl.program_id(0)

        @pl.when(i == 0)
        def _():
            acc[...] = jnp.zeros_like(acc)
            dpg_ref[...] = jnp.zeros_like(dpg_ref)

        def with_next(cur_ref, nxt_ref):
            nxt = jnp.where(i < n_t - 1, nxt_ref[...], 0.0)
            if ts == BLK:
                return cur_ref[...] + nxt
            return jnp.concatenate([cur_ref[:ts - BLK, :], cur_ref[ts - BLK:, :] + nxt], axis=0)

        dproj = jnp.concatenate([dlx_ref[...], dlg_ref[...], dq_ref[...], with_next(dk_ref, dkn_ref),
                                 with_next(dv_ref, dvn_ref), dglu_ref[...]], axis=1).astype(BF16)
        dhn = _mm(dproj, w_ref[...])
        acc[...] += _mm_tn(dproj, hn_ref[...])
        xv = x_ref[...]
        dxv, dp = _rms_bwd(xv, _rms_r(xv), pg_ref[...], dhn)
        dpg_ref[...] += dp
        dx_ref[...] = dres_ref[...] + dxv

        @pl.when(i == n_t - 1)
        def _():
            dw_ref[...] = acc[...].astype(BF16)

    row = lambda w: pl.BlockSpec((ts, w), lambda i: (i, 0))
    nxt = pl.BlockSpec((BLK, KV_W), lambda i: (jnp.minimum(i + 1, n_t - 1), 0))
    vec = pl.BlockSpec((1, d), lambda i: (0, 0))
    full = pl.BlockSpec((D_IN_PROJ, d), lambda i: (0, 0))
    return _pcall(
        body, (dres, x, pre_g, hn, w_in_t, dlx, dlg, dq, dk, dk_up, dv, dv_up, dglu), name=name, grid=(n_t,),
        in_specs=[row(d), row(d), vec, row(d), full, row(W_A), row(W_A), row(W_B), row(KV_W), nxt,
                  row(KV_W), nxt, row(2 * W_C)],
        out_specs=[row(d), full, vec],
        out_shape=[SDS((s, d), F32), SDS((D_IN_PROJ, d), BF16), SDS((1, d), F32)],
        scratch_shapes=[pltpu.VMEM((D_IN_PROJ, d), F32)], carry=carry)


def _lru_gates(xc, lru_p):
    cw_ref, cb_ref, wa_ref, ba_ref, wx_ref, bx_ref, lam_ref = lru_p
    c = cb_ref[...]
    for j in range(LRU_K):
        c = c + cw_ref[j:j + 1, :] * _shift_down(xc, LRU_K - 1 - j)[LRU_HALO:, :]
    r = _sigmoid(_mm(c, wa_ref[...]) + ba_ref[...])
    ig = _sigmoid(_mm(c, wx_ref[...]) + bx_ref[...])
    sp = _softplus(-lam_ref[...])
    log_a = -LRU_C * r * sp
    a = jnp.exp(log_a)
    m = jnp.sqrt(_neg_expm1(2.0 * log_a))
    return c, r, ig, sp, a, m


def _lru_pspecs():
    small = lambda r: pl.BlockSpec((r, W_A), lambda i: (0, 0))
    return [small(LRU_K), small(1), small(W_A), small(1), small(W_A), small(1), small(1)]


def _lru_fwd(lx, lg, lru_p, name, carry=()):
    s = lx.shape[0]
    ts = _time_tile(s)
    n8 = ts // LRU_HALO

    def body(lx_ref, lxp_ref, lg_ref, *rest):
        lru_p, (ya_ref, h_ref, hcarry) = rest[:7], rest[7:]
        i = pl.program_id(0)
        prev = jnp.where(i > 0, lxp_ref[...], 0.0)
        xc = jnp.concatenate([prev, lx_ref[...]], axis=0)
        c, r, ig, sp, a, m = _lru_gates(xc, lru_p)
        acc_a, acc_b = a, m * (ig * c)
        t = lax.broadcasted_iota(jnp.int32, a.shape, 0)
        k = 1
        while k < ts:
            keep = t >= k
            acc_b = jnp.where(keep, acc_a * _shift_down(acc_b, k) + acc_b, acc_b)
            acc_a = jnp.where(keep, acc_a * _shift_down(acc_a, k), acc_a)
            k *= 2
        h0 = jnp.where(i > 0, hcarry[...], 0.0)
        h = acc_b + acc_a * h0
        hcarry[...] = h[ts - 1:ts, :]
        h_ref[...] = h
        ya_ref[...] = _gelu(lg_ref[...])[0] * h

    row = pl.BlockSpec((ts, W_A), lambda i: (i, 0))
    prev8 = pl.BlockSpec((LRU_HALO, W_A), lambda i: (jnp.maximum(i * n8 - 1, 0), 0))
    return _pcall(
        body, (lx, lx, lg, *lru_p), name=name, grid=(s // ts,),
        in_specs=[row, prev8, row] + _lru_pspecs(), out_specs=[row, row],
        out_shape=[SDS((s, W_A), F32), SDS((s, W_A), F32)],
        scratch_shapes=[pltpu.VMEM((1, W_A), F32)], carry=carry)


def _lru_bwd(dya, lx, lg, h_s, lru_p, name, carry=()):
    s = lx.shape[0]
    ts = _time_tile(s)
    n_t = s // ts
    n8 = ts // LRU_HALO

    def body(dya_ref, lx_ref, lxp_ref, lg_ref, h_ref, hp_ref, *rest):
        lru_p = rest[:7]
        (dlx_ref, dlg_ref, dcw_ref, dcb_ref, dwa_ref, dba_ref, dwx_ref, dbx_ref, dlam_ref,
         carry_a, carry_l, carry_dc) = rest[7:]
        cw_ref, _, wa_ref, _, wx_ref, _, lam_ref = lru_p
        i = pl.program_id(0)
        first_tile = i == n_t - 1
        last_tile = i == 0

        @pl.when(i == 0)
        def _():
            for ref in (dcw_ref, dcb_ref, dwa_ref, dba_ref, dwx_ref, dbx_ref, dlam_ref):
                ref[...] = jnp.zeros_like(ref)

        prev = jnp.where(first_tile, 0.0, lxp_ref[...])
        xc = jnp.concatenate([prev, lx_ref[...]], axis=0)
        c, r, ig, sp, a, m = _lru_gates(xc, lru_p)
        h = h_ref[...]
        hcat = jnp.concatenate([jnp.where(first_tile, 0.0, hp_ref[...]), h], axis=0)
        h_m1 = _shift_down(hcat, 1)[LRU_HALO:, :]
        lg = lg_ref[...]
        ge, th = _gelu(lg)
        dya = dya_ref[...]
        dlg_ref[...] = dya * h * _dgelu(lg, th)
        dh = dya * ge
        t = lax.broadcasted_iota(jnp.int32, a.shape, 0)
        a_next = jnp.where(t < ts - 1, _shift_up(a, 1), jnp.where(last_tile, 0.0, carry_a[...]))
        acc_a, acc_b = a_next, dh
        k = 1
        while k < ts:
            keep = t < ts - k
            acc_b = jnp.where(keep, acc_a * _shift_up(acc_b, k) + acc_b, acc_b)
            acc_a = jnp.where(keep, acc_a * _shift_up(acc_a, k), acc_a)
            k *= 2
        lam_beyond = jnp.where(last_tile, 0.0, carry_l[...])
        lmb = acc_b + acc_a * lam_beyond
        carry_a[...] = a[0:1, :]
        carry_l[...] = lmb[0:1, :]
        gi = ig * c
        dgi = lmb * m
        dla = lmb * h_m1 * a - (lmb * gi) * (a * a) / m
        dr = dla * (-LRU_C * sp)
        dsp = jnp.sum(dla * (-LRU_C * r), axis=0, keepdims=True)
        dlam_ref[...] += -dsp * _sigmoid(-lam_ref[...])
        dra = dr * r * (1.0 - r)
        dia = dgi * c * ig * (1.0 - ig)
        dc = dgi * ig + _mm_nt(dra, wa_ref[...]) + _mm_nt(dia, wx_ref[...])
        dwa_ref[...] += _mm_tn(c, dra)
        dwx_ref[...] += _mm_tn(c, dia)
        dba_ref[...] += jnp.sum(dra, axis=0, keepdims=True)
        dbx_ref[...] += jnp.sum(dia, axis=0, keepdims=True)
        dcb_ref[...] += jnp.sum(dc, axis=0, keepdims=True)
        dcc = jnp.concatenate([dc, jnp.where(last_tile, 0.0, carry_dc[...])], axis=0)
        carry_dc[...] = dc[0:LRU_HALO, :]
        dlx = jnp.zeros_like(dc)
        for j in range(LRU_K):
            sh = LRU_K - 1 - j
            dcw_ref[j:j + 1, :] += jnp.sum(dc * _shift_down(xc, sh)[LRU_HALO:, :], axis=0, keepdims=True)
            dlx = dlx + cw_ref[j:j + 1, :] * _shift_up(dcc, sh)[:ts, :]
        dlx_ref[...] = dlx

    row = pl.BlockSpec((ts, W_A), lambda i: (n_t - 1 - i, 0))
    prev8 = pl.BlockSpec((LRU_HALO, W_A), lambda i: (jnp.maximum((n_t - 1 - i) * n8 - 1, 0), 0))
    small = lambda r: pl.BlockSpec((r, W_A), lambda i: (0, 0))
    return _pcall(
        body, (dya, lx, lx, lg, h_s, h_s, *lru_p), name=name, grid=(n_t,),
        in_specs=[row, row, prev8, row, row, prev8] + _lru_pspecs(),
        out_specs=[row, row, small(LRU_K), small(1), small(W_A), small(1), small(W_A), small(1), small(1)],
        out_shape=[SDS((s, W_A), F32), SDS((s, W_A), F32), SDS((LRU_K, W_A), F32), SDS((1, W_A), F32),
                   SDS((W_A, W_A), F32), SDS((1, W_A), F32), SDS((W_A, W_A), F32), SDS((1, W_A), F32),
                   SDS((1, W_A), F32)],
        scratch_shapes=[pltpu.VMEM((1, W_A), F32), pltpu.VMEM((1, W_A), F32), pltpu.VMEM((LRU_HALO, W_A), F32)],
        carry=carry)


_ATT_ROWS = N_Q_HEADS * BLK
_GRP_ROWS = Q_PER_KV * BLK


def _attn_stack(ref, rows, g):
    return jnp.concatenate([ref[rows, h * HEAD_DIM:(h + 1) * HEAD_DIM]
                            for h in range(g * Q_PER_KV, (g + 1) * Q_PER_KV)], axis=0)


def _attn_unstack(parts):
    return jnp.concatenate([p[j * BLK:(j + 1) * BLK, :] for p in parts for j in range(Q_PER_KV)], axis=1)


def _grp(x, g):
    return x[:, g * _GRP_ROWS:(g + 1) * _GRP_ROWS]


def _attn_block(q_ref, k_ref, kp_ref, v_ref, vp_ref, sink_row, i, b):
    rows, prev = slice(b * BLK, (b + 1) * BLK), slice((b - 1) * BLK, b * BLK)
    qs, kcs, kps, vcs, vps = [], [], [], [], []
    for g in range(N_KV_HEADS):
        cols = slice(g * HEAD_DIM, (g + 1) * HEAD_DIM)
        qs.append(_attn_stack(q_ref, rows, g))
        kcs.append(k_ref[rows, cols])
        vcs.append(v_ref[rows, cols])
        kps.append(kp_ref[:, cols] if b == 0 else k_ref[prev, cols])
        vps.append(vp_ref[:, cols] if b == 0 else v_ref[prev, cols])
    scale = 1.0 / math.sqrt(HEAD_DIM)
    sc = jnp.concatenate([_mm_nt(kcs[g], qs[g]) for g in range(N_KV_HEADS)], axis=1) * scale
    sp = jnp.concatenate([_mm_nt(kps[g], qs[g]) for g in range(N_KV_HEADS)], axis=1) * scale
    kj = lax.broadcasted_iota(jnp.int32, (BLK, _ATT_ROWS), 0)
    qi = lax.broadcasted_iota(jnp.int32, (BLK, _ATT_ROWS), 1) & (BLK - 1)
    sc = jnp.where(kj <= qi, sc, NEG_BIG)
    sp = jnp.where((kj > qi) if b > 0 else ((kj > qi) & (i > 0)), sp, NEG_BIG)
    m = jnp.maximum(jnp.maximum(jnp.max(sc, axis=0, keepdims=True), jnp.max(sp, axis=0, keepdims=True)), sink_row)
    pc = jnp.exp(sc - m)
    pp = jnp.exp(sp - m)
    es = jnp.exp(sink_row - m)
    inv = 1.0 / (jnp.sum(pc, axis=0, keepdims=True) + jnp.sum(pp, axis=0, keepdims=True) + es)
    return qs, kcs, kps, vcs, vps, pc * inv, pp * inv, es * inv


def _attn_specs(s, ts):
    bpt = ts // BLK
    tile = lambda w: pl.BlockSpec((ts, w), lambda i: (i, 0))
    prv = pl.BlockSpec((BLK, KV_W), lambda i: (jnp.maximum(i * bpt - 1, 0), 0))
    sink = pl.BlockSpec((1, _ATT_ROWS), lambda i: (0, 0))
    return bpt, tile, prv, sink


def _attn_fwd(q, k, v, sink_row, name, carry=()):
    s = q.shape[0]
    ts = _time_tile(s)
    bpt, tile, prv, sink = _attn_specs(s, ts)

    def body(q_ref, k_ref, kp_ref, v_ref, vp_ref, sk_ref, y_ref):
        i = pl.program_id(0)
        for b in range(bpt):
            _, _, _, vcs, vps, pc, pp, _ = _attn_block(q_ref, k_ref, kp_ref, v_ref, vp_ref, sk_ref[...], i, b)
            outs = [_mm_tn(_grp(pc, g), vcs[g]) + _mm_tn(_grp(pp, g), vps[g]) for g in range(N_KV_HEADS)]
            y_ref[b * BLK:(b + 1) * BLK, :] = _attn_unstack(outs)

    return _pcall(
        body, (q, k, k, v, v, sink_row), name=name, grid=(s // ts,),
        in_specs=[tile(W_B), tile(KV_W), prv, tile(KV_W), prv, sink],
        out_specs=[tile(W_B)], out_shape=[SDS((s, W_B), F32)], carry=carry)


def _attn_bwd(dy, q, k, v, sinks, name, carry=()):
    s = q.shape[0]
    ts = _time_tile(s)
    n_t = s // ts
    bpt, tile, prv, sink = _attn_specs(s, ts)

    def body(dy_ref, q_ref, k_ref, kp_ref, v_ref, vp_ref, sk_ref, dq_ref, dk_ref, dv_ref, dku_ref, dvu_ref, dsk_ref):
        i = pl.program_id(0)

        @pl.when(i == 0)
        def _():
            dsk_ref[...] = jnp.zeros_like(dsk_ref)

        scale = 1.0 / math.sqrt(HEAD_DIM)
        groups = range(N_KV_HEADS)
        head_row = lax.broadcasted_iota(jnp.int32, (N_Q_HEADS, BLK), 0)
        dsk = jnp.zeros((N_Q_HEADS, BLK), F32)
        dk_blocks, dv_blocks = [], []
        for b in range(bpt):
            rows = slice(b * BLK, (b + 1) * BLK)
            qs, kcs, kps, vcs, vps, pc, pp, ps = _attn_block(q_ref, k_ref, kp_ref, v_ref, vp_ref, sk_ref[...], i, b)
            dos = [_attn_stack(dy_ref, rows, g) for g in groups]
            dpc = jnp.concatenate([_mm_nt(vcs[g], dos[g]) for g in groups], axis=1)
            dpp = jnp.concatenate([_mm_nt(vps[g], dos[g]) for g in groups], axis=1)
            delta = jnp.sum(pc * dpc, axis=0, keepdims=True) + jnp.sum(pp * dpp, axis=0, keepdims=True)
            dsc = pc * (dpc - delta) * scale
            dsp = pp * (dpp - delta) * scale
            dq_ref[rows, :] = _attn_unstack([_mm_tn(_grp(dsc, g), kcs[g]) + _mm_tn(_grp(dsp, g), kps[g])
                                             for g in groups])
            dk_blocks.append(jnp.concatenate([_mm(_grp(dsc, g), qs[g]) for g in groups], axis=1))
            dv_blocks.append(jnp.concatenate([_mm(_grp(pc, g), dos[g]) for g in groups], axis=1))
            dkp = jnp.concatenate([_mm(_grp(dsp, g), qs[g]) for g in groups], axis=1)
            dvp = jnp.concatenate([_mm(_grp(pp, g), dos[g]) for g in groups], axis=1)
            if b == 0:
                dku_ref[...] = dkp
                dvu_ref[...] = dvp
            else:
                dk_blocks[b - 1] = dk_blocks[b - 1] + dkp
                dv_blocks[b - 1] = dv_blocks[b - 1] + dvp
            dsink = -ps * delta
            for h in range(N_Q_HEADS):
                dsk = dsk + jnp.where(head_row == h, jnp.sum(dsink[:, h * BLK:(h + 1) * BLK], axis=1, keepdims=True), 0.0)
        for b in range(bpt):
            dk_ref[b * BLK:(b + 1) * BLK, :] = dk_blocks[b]
            dv_ref[b * BLK:(b + 1) * BLK, :] = dv_blocks[b]
        dsk_ref[...] += dsk

    up = pl.BlockSpec((BLK, KV_W), lambda i: (i, 0))
    return _pcall(
        body, (dy, q, k, k, v, v, sinks), name=name, grid=(n_t,),
        in_specs=[tile(W_B), tile(W_B), tile(KV_W), prv, tile(KV_W), prv, sink],
        out_specs=[tile(W_B), tile(KV_W), tile(KV_W), up, up, pl.BlockSpec((N_Q_HEADS, BLK), lambda i: (0, 0))],
        out_shape=[SDS((s, W_B), F32), SDS((s, KV_W), F32), SDS((s, KV_W), F32), SDS((n_t * BLK, KV_W), F32),
                   SDS((n_t * BLK, KV_W), F32), SDS((N_Q_HEADS, BLK), F32)], carry=carry)


def _cc_recompute(glu_ref, glup_ref, cw_ref, cb_ref, first_tile):
    prev = jnp.where(first_tile, 0.0, glup_ref[...])
    ge = jnp.concatenate([prev, glu_ref[...]], axis=0)
    y0 = ge[:, :W_C] * _sigmoid_t(ge[:, W_C:])
    y1 = cb_ref[...]
    for j in range(CC_K):
        y1 = y1 + cw_ref[j:j + 1, :] * _shift_down(y0, CC_K - 1 - j)[CC_HALO:, :]
    return y0, y1


def _ln_stats(y1):
    mu = jnp.mean(y1, axis=-1, keepdims=True)
    xc = y1 - mu
    rstd = lax.rsqrt(jnp.mean(xc * xc, axis=-1, keepdims=True) + LN_EPS)
    return xc * rstd, rstd


def _cc_specs(s, ts):
    n32 = ts // CC_HALO
    row = lambda w: pl.BlockSpec((ts, w), lambda i: (i, 0))
    prev = pl.BlockSpec((CC_HALO, 2 * W_C), lambda i: (jnp.maximum(i * n32 - 1, 0), 0))
    small = lambda r: pl.BlockSpec((r, W_C), lambda i: (0, 0))
    return row, prev, small


def _cc_fwd(glu, cw, cb, lng, lnb, name, carry=()):
    s = glu.shape[0]
    ts = _time_tile(s)
    row, prev, small = _cc_specs(s, ts)

    def body(glu_ref, glup_ref, cw_ref, cb_ref, lng_ref, lnb_ref, y_ref):
        _, y1 = _cc_recompute(glu_ref, glup_ref, cw_ref, cb_ref, pl.program_id(0) == 0)
        xhat, _ = _ln_stats(y1)
        z = xhat * lng_ref[...] + lnb_ref[...]
        y_ref[...] = z * _sigmoid_t(z)

    return _pcall(
        body, (glu, glu, cw, cb, lng, lnb), name=name, grid=(s // ts,),
        in_specs=[row(2 * W_C), prev, small(CC_HALO), small(1), small(1), small(1)],
        out_specs=[row(W_C)], out_shape=[SDS((s, W_C), F32)], carry=carry)


def _cc_bwd_conv(dy, glu, cw, cb, lng, lnb, name, carry=()):
    s = glu.shape[0]
    ts = _time_tile(s)
    row, prev, small = _cc_specs(s, ts)

    def body(dy_ref, glu_ref, glup_ref, cw_ref, cb_ref, lng_ref, lnb_ref, dy1_ref, dcw_ref, dcb_ref, dlng_ref, dlnb_ref):
        i = pl.program_id(0)

        @pl.when(i == 0)
        def _():
            for ref in (dcw_ref, dcb_ref, dlng_ref, dlnb_ref):
                ref[...] = jnp.zeros_like(ref)

        y0, y1 = _cc_recompute(glu_ref, glup_ref, cw_ref, cb_ref, i == 0)
        xhat, rstd = _ln_stats(y1)
        z = xhat * lng_ref[...] + lnb_ref[...]
        dz = dy_ref[...] * _dsilu(z, _sigmoid_t(z))
        dlng_ref[...] += jnp.sum(dz * xhat, axis=0, keepdims=True)
        dlnb_ref[...] += jnp.sum(dz, axis=0, keepdims=True)
        dxh = dz * lng_ref[...]
        dy1 = rstd * (dxh - jnp.mean(dxh, axis=-1, keepdims=True) - xhat * jnp.mean(dxh * xhat, axis=-1, keepdims=True))
        dy1_ref[...] = dy1
        dcb_ref[...] += jnp.sum(dy1, axis=0, keepdims=True)
        for j in range(CC_K):
            dcw_ref[j:j + 1, :] += jnp.sum(dy1 * _shift_down(y0, CC_K - 1 - j)[CC_HALO:, :], axis=0, keepdims=True)

    return _pcall(
        body, (dy, glu, glu, cw, cb, lng, lnb), name=name, grid=(s // ts,),
        in_specs=[row(W_C), row(2 * W_C), prev, small(CC_HALO), small(1), small(1), small(1)],
        out_specs=[row(W_C), small(CC_HALO), small(1), small(1), small(1)],
        out_shape=[SDS((s, W_C), F32), SDS((CC_HALO, W_C), F32)] + [SDS((1, W_C), F32)] * 3, carry=carry)


def _cc_bwd_glu(dy1, glu, cw, name, carry=()):
    s = glu.shape[0]
    ts = _time_tile(s)
    n_t = s // ts
    n32 = ts // CC_HALO

    def body(dy1_ref, dyn_ref, glu_ref, cw_ref, dglu_ref):
        i = pl.program_id(0)
        dcat = jnp.concatenate([dy1_ref[...], jnp.where(i < n_t - 1, dyn_ref[...], 0.0)], axis=0)
        dy0 = jnp.zeros((ts, W_C), F32)
        for j in range(CC_K):
            dy0 = dy0 + cw_ref[j:j + 1, :] * _shift_up(dcat, CC_K - 1 - j)[:ts, :]
        a = glu_ref[:, :W_C]
        sg = _sigmoid_t(glu_ref[:, W_C:])
        dglu_ref[...] = jnp.concatenate([dy0 * sg, dy0 * a * sg * (1.0 - sg)], axis=1)

    row = lambda w: pl.BlockSpec((ts, w), lambda i: (i, 0))
    nxt = pl.BlockSpec((CC_HALO, W_C), lambda i: (jnp.minimum((i + 1) * n32, s // CC_HALO - 1), 0))
    return _pcall(
        body, (dy1, dy1, glu, cw), name=name, grid=(n_t,),
        in_specs=[row(W_C), nxt, row(2 * W_C), pl.BlockSpec((CC_HALO, W_C), lambda i: (0, 0))],
        out_specs=[row(2 * W_C)], out_shape=[SDS((s, 2 * W_C), F32)], carry=carry)


_MIX_OFFS = ((0, W_A), (W_A, W_A + W_B), (W_A + W_B, W_A + W_B + W_C))


def _mix_out_fwd(x, ya, yb, yc, group_g, w_out, post_g, name, carry=()):
    s, d = x.shape
    ts = 2 * _time_tile(s)
    dm = w_out.shape[0]

    def body(x_ref, ya_ref, yb_ref, yc_ref, gg_ref, w_ref, qg_ref, xo_ref, o_ref):
        parts = []
        for y_ref, (lo, hi) in zip((ya_ref, yb_ref, yc_ref), _MIX_OFFS):
            yv = y_ref[...]
            parts.append(yv * _rms_r(yv) * gg_ref[:, lo:hi])
        o = _mm(jnp.concatenate(parts, axis=1), w_ref[...])
        o_ref[...] = o
        xo_ref[...] = x_ref[...] + o * _rms_r(o) * qg_ref[...]

    row = lambda w: pl.BlockSpec((ts, w), lambda i: (i, 0))
    return _pcall(
        body, (x, ya, yb, yc, group_g, w_out, post_g), name=name, grid=(s // ts,),
        in_specs=[row(d), row(W_A), row(W_B), row(W_C), pl.BlockSpec((1, dm), lambda i: (0, 0)),
                  pl.BlockSpec((dm, d), lambda i: (0, 0)), pl.BlockSpec((1, d), lambda i: (0, 0))],
        out_specs=[row(d), row(d)], out_shape=[SDS((s, d), F32), SDS((s, d), F32)], carry=carry)


def _mix_out_bwd(dxo, o, ya, yb, yc, group_g, w_out, post_g, name, carry=()):
    s, d = o.shape
    ts = _time_tile(s)
    n_t = s // ts
    dm = w_out.shape[0]

    def body(dxo_ref, o_ref, ya_ref, yb_ref, yc_ref, gg_ref, w_ref, qg_ref,
             dya_ref, dyb_ref, dyc_ref, dw_ref, dqg_ref, dgg_ref, acc):
        i = pl.program_id(0)

        @pl.when(i == 0)
        def _():
            acc[...] = jnp.zeros_like(acc)
            dqg_ref[...] = jnp.zeros_like(dqg_ref)
            dgg_ref[...] = jnp.zeros_like(dgg_ref)

        ov = o_ref[...]
        do, dq = _rms_bwd(ov, _rms_r(ov), qg_ref[...], dxo_ref[...])
        dqg_ref[...] += dq
        do = do.astype(BF16)
        dyn = _mm_nt(do, w_ref[...])
        parts, dggs = [], []
        for y_ref, dy_ref, (lo, hi) in zip((ya_ref, yb_ref, yc_ref), (dya_ref, dyb_ref, dyc_ref), _MIX_OFFS):
            yv = y_ref[...]
            r = _rms_r(yv)
            gg = gg_ref[:, lo:hi]
            parts.append(yv * r * gg)
            dyv, dg = _rms_bwd(yv, r, gg, dyn[:, lo:hi])
            dy_ref[...] = dyv
            dggs.append(dg)
        dgg_ref[...] += jnp.concatenate(dggs, axis=1)
        acc[...] += _mm_tn(jnp.concatenate(parts, axis=1), do)

        @pl.when(i == n_t - 1)
        def _():
            dw_ref[...] = acc[...].astype(BF16)

    row = lambda w: pl.BlockSpec((ts, w), lambda i: (i, 0))
    full = pl.BlockSpec((dm, d), lambda i: (0, 0))
    return _pcall(
        body, (dxo, o, ya, yb, yc, group_g, w_out, post_g), name=name, grid=(n_t,),
        in_specs=[row(d), row(d), row(W_A), row(W_B), row(W_C), pl.BlockSpec((1, dm), lambda i: (0, 0)), full,
                  pl.BlockSpec((1, d), lambda i: (0, 0))],
        out_specs=[row(W_A), row(W_B), row(W_C), full, pl.BlockSpec((1, d), lambda i: (0, 0)),
                   pl.BlockSpec((1, dm), lambda i: (0, 0))],
        out_shape=[SDS((s, W_A), F32), SDS((s, W_B), F32), SDS((s, W_C), F32), SDS((dm, d), BF16),
                   SDS((1, d), F32), SDS((1, dm), F32)],
        scratch_shapes=[pltpu.VMEM((dm, d), F32)], carry=carry)


def _adamw_math(w, g, m, v):
    m = ADAM_B1 * m + (1.0 - ADAM_B1) * g
    v = ADAM_B2 * v + (1.0 - ADAM_B2) * (g * g)
    m_hat = m / (1.0 - ADAM_B1 ** ADAM_STEP)
    v_hat = v / (1.0 - ADAM_B2 ** ADAM_STEP)
    delta = -ADAM_LR * (m_hat / (jnp.sqrt(v_hat) + ADAM_EPS) + ADAM_WD * w)
    return delta, m, v


def _row_tile(rows, cap=256):
    best = None
    for t in range(16, min(rows, cap) + 1, 16):
        if rows % t == 0:
            best = t
    return best if best is not None else rows


def _reduce_adamw(recv, w, m, v, name):
    n_l, r, c = w.shape
    tr = _row_tile(r)

    def body(recv_ref, w_ref, m_ref, v_ref, g_ref, d_ref, nm_ref, nv_ref):
        g = recv_ref[0].astype(F32)
        for p in range(1, N_DEV):
            g = g + recv_ref[p].astype(F32)
        g_ref[...] = g
        d_ref[...], nm_ref[...], nv_ref[...] = _adamw_math(w_ref[...], g, m_ref[...], v_ref[...])

    blk = pl.BlockSpec((None, tr, c), lambda l, i: (l, i, 0))
    return _pcall(
        body, (recv, w, m, v), name=name, grid=(n_l, r // tr),
        in_specs=[pl.BlockSpec((N_DEV, None, tr, c), lambda l, i: (0, l, i, 0)), blk, blk, blk],
        out_specs=[blk] * 4, out_shape=[SDS(w.shape, F32)] * 4)[0]


def _reduce_adamw_small(parts, w, m, v, name):
    def body(p_ref, w_ref, m_ref, v_ref, g_ref, d_ref, nm_ref, nv_ref):
        g = p_ref[0]
        for p in range(1, N_DEV):
            g = g + p_ref[p]
        g_ref[...] = g
        d_ref[...], nm_ref[...], nv_ref[...] = _adamw_math(w_ref[...], g, m_ref[...], v_ref[...])

    vm = pl.BlockSpec(memory_space=pltpu.VMEM)
    return pl.pallas_call(body, name=name, in_specs=[vm] * 4, out_specs=[vm] * 4, out_shape=[SDS(w.shape, F32)] * 4,
                          compiler_params=pltpu.CompilerParams(vmem_limit_bytes=VMEM_LIMIT))(parts, w, m, v)


def _rows_of(shape):
    return -(-math.prod(shape) // (8 * BLK)) * 8


def _pack(arrs):
    rows = []
    for a in arrs:
        n, r = math.prod(a.shape), _rows_of(a.shape)
        if n % BLK == 0:
            part = a.reshape(n // BLK, BLK)
            rows.append(part if n // BLK == r else jnp.pad(part, ((0, r - n // BLK), (0, 0))))
        else:
            rows.append(jnp.pad(a.reshape(-1), (0, r * BLK - n)).reshape(r, BLK))
    return jnp.concatenate(rows, axis=0)


def _unpack(packed, shapes):
    out, row = [], 0
    for shp in shapes:
        n, r = math.prod(shp), _rows_of(shp)
        if n % BLK == 0:
            out.append(packed[row:row + n // BLK].reshape(shp))
        else:
            out.append(packed[row:row + r].reshape(-1)[:n].reshape(shp))
        row += r
    return out


def _block_diag(w):
    nb, bw, _ = w.shape
    eye = jnp.eye(nb, dtype=w.dtype)
    return (eye[:, None, :, None] * w[:, :, None, :]).reshape(nb * bw, nb * bw)


def _diag_blocks(wd, nb):
    bw = wd.shape[0] // nb
    return jnp.stack([wd[b * bw:(b + 1) * bw, b * bw:(b + 1) * bw] for b in range(nb)])


WEIGHT_NAMES = ['ffn1_pre_g', 'ffn1_w_gu', 'ffn1_w_down', 'ffn1_post_g', 'mix_pre_g', 'w_in', 'lru_conv_w', 'lru_conv_b',
                'lru_w_a', 'lru_b_a', 'lru_w_x', 'lru_b_x', 'lru_lambda', 'attn_sinks', 'conv_w', 'conv_b', 'conv_ln_g',
                'conv_ln_b', 'group_g', 'w_out', 'mix_post_g', 'ffn2_pre_g', 'ffn2_w_gu', 'ffn2_w_down', 'ffn2_post_g']
BIG = ('ffn1_w_gu', 'ffn1_w_down', 'w_in', 'w_out', 'ffn2_w_gu', 'ffn2_w_down')
TRANSPOSED = ('ffn1_w_gu', 'ffn2_w_gu', 'w_in')
SMALL = tuple(k for k in WEIGHT_NAMES if k not in BIG)
CHANNEL_SHARDED = ('lru_conv_w', 'conv_w')


def _step(x, target, w, m, v):
    n_l = w['ffn1_pre_g'].shape[0]
    assert n_l == 2, "the exchange schedule below is laid out for two layers"
    s, d = x.shape[1], x.shape[2]
    x = x.reshape(s, d)
    target = target.reshape(s, d)
    me = _my_pos()[3]
    tview = lambda t, k: jnp.swapaxes(t[k], 1, 2) if k in TRANSPOSED else t[k]
    wb = {k: tview(w, k).astype(BF16) for k in BIG}
    vec = lambda name, l: w[name][l][None, :]

    conv_shard = _pack([w['lru_conv_w'], w['conv_w']])
    g0 = _all_gather([(wb['ffn1_w_gu'], 0), (wb['ffn1_w_down'], 0), (wb['w_in'], 0), (wb['w_out'], 0),
                      (conv_shard, None)], "all_gather_first")
    wts = [dict(), dict()]
    wts[0]['ffn1_w_gu'], wts[0]['ffn1_w_down'], wts[0]['w_in'], wts[0]['w_out'], conv_g = g0
    ch = W_A // N_DEV
    conv_parts = [_unpack(conv_g[p], [(n_l, LRU_K, ch), (n_l, CC_K, ch)]) for p in range(N_DEV)]
    lru_cw = jnp.concatenate([cp[0] for cp in conv_parts], axis=-1)
    cc_cw = jnp.concatenate([cp[1] for cp in conv_parts], axis=-1)
    cc_cw = jnp.pad(cc_cw, ((0, 0), (0, CC_HALO - CC_K), (0, 0)))

    fc = wb['ffn1_w_gu'].shape[1]
    cut1, cut2 = (fc * 4 // 11 + 15) // 16 * 16, (fc * 27 // 44 + 15) // 16 * 16
    gather_plan = {
        ('ffn1', 0): [('A', 'f2_0', ('ffn2_w_gu', 'ffn2_w_down'), 0)],
        ('mix_in', 0): [('B', 'f2_0'), ('A', 'g1_1a', ('ffn1_w_gu',), 1, (0, cut1))],
        ('lru', 0): [('A', 'g1_1b', ('ffn1_w_gu',), 1, (cut1, cut2 - cut1), 'g1_1a')],
        ('attn', 0): [('A', 'g1_1', ('ffn1_w_gu',), 1, (cut2, fc - cut2), 'g1_1b')],
        ('cconv', 0): [('B', 'g1_1')],
        ('ffn2', 0): [('D', None, ('ffn1_w_down',), 1), ('A', 'wi_1', ('w_in',), 1), ('A', 'wo_1', ('w_out',), 1)],
        ('ffn1', 1): [('A', 'f2_1', ('ffn2_w_gu', 'ffn2_w_down'), 1), ('B', 'wi_1'), ('B', 'wo_1')],
        ('mix_in', 1): [('B', 'f2_1')],
    }
    pend = {}

    def fwd(kernel_name, l, fn, *args):
        plan = gather_plan.get((kernel_name, l), [])
        carry = []
        for st in plan:
            if st[0] == 'B':
                carry.append(_gather_b(pend[st[1]][2]))
            else:
                rows = st[4] if len(st) > 4 else None
                into = pend.pop(st[5])[2] if len(st) > 5 else [None] * len(st[2])
                carry.append(_gather_a([(wb[k], st[3], rows, buf) for k, buf in zip(st[2], into)],
                                       two_level=st[0] == 'A'))
        outs, ex = fn(*args, f"{kernel_name}_fwd_l{l}", carry)
        for st, bufs in zip(plan, ex):
            if st[0] == 'A':
                pend[st[1]] = (st[2], st[3], bufs)
            else:
                names, wl = (st[2], st[3]) if st[0] == 'D' else pend.pop(st[1])[:2]
                for k, b in zip(names, bufs):
                    wts[wl][k] = b
        return outs

    saved = []
    h = x
    for l in range(n_l):
        sv = {'x0': h}
        lw = wts[l]
        x1, sv['h1'], sv['g1'], sv['u1'], sv['d1'] = fwd(
            'ffn1', l, _ffn_fwd, h, vec('ffn1_pre_g', l), vec('ffn1_post_g', l), lw['ffn1_w_gu'], lw['ffn1_w_down'])
        sv['x1'] = x1
        sv['hn'], lx, lg, q, k, vv, glu = fwd('mix_in', l, _mix_in_fwd, x1, vec('mix_pre_g', l),
                                              lw['w_in'].reshape(D_IN_PROJ, d))
        sv.update(lx=lx, lg=lg, q=q, k=k, v=vv, glu=glu)
        lru_p = (lru_cw[l], vec('lru_conv_b', l), _block_diag(w['lru_w_a'][l]).astype(BF16), vec('lru_b_a', l),
                 _block_diag(w['lru_w_x'][l]).astype(BF16), vec('lru_b_x', l), vec('lru_lambda', l))
        cc_p = (cc_cw[l], vec('conv_b', l), vec('conv_ln_g', l), vec('conv_ln_b', l))
        sv.update(lru_p=lru_p, cc_p=cc_p)
        sv['ya'], sv['hs'] = fwd('lru', l, _lru_fwd, lx, lg, lru_p)
        sv['sink_row'] = jnp.repeat(w['attn_sinks'][l], BLK)[None, :]
        (sv['yb'],) = fwd('attn', l, _attn_fwd, q, k, vv, sv['sink_row'])
        (sv['yc'],) = fwd('cconv', l, _cc_fwd, glu, *cc_p)
        x2, sv['o'] = fwd('mix_out', l, _mix_out_fwd, x1, sv['ya'], sv['yb'], sv['yc'], vec('group_g', l),
                          lw['w_out'].reshape(-1, d), vec('mix_post_g', l))
        sv['x2'] = x2
        h, sv['h2'], sv['g2'], sv['u2'], sv['d2'] = fwd(
            'ffn2', l, _ffn_fwd, x2, vec('ffn2_pre_g', l), vec('ffn2_post_g', l), lw['ffn2_w_gu'], lw['ffn2_w_down'])
        saved.append(sv)

    dh = h

    recv = {k: None for k in BIG}
    ready = {}
    small = [dict() for _ in range(n_l)]

    c_even, c_odd = tuple(range(0, N_DEV, 2)), tuple(range(1, N_DEV, 2))

    def exchange(keys):
        return _grad_x([(ready[key[:2]], key[1], recv[key[0]]) + tuple(key[2:]) for key in keys], n_l)

    def received(keys, bufs):
        for key, b in zip(keys, bufs):
            recv[key[0]] = b

    def run(fn, *args, keys=(), **kw):
        outs, ex = fn(*args, carry=[exchange(keys)] if keys else [], **kw)
        if keys:
            received(keys, ex[0])
        return outs

    for l in reversed(range(n_l)):
        sv, sg, lw = saved[l], small[l], wts[l]
        keys = [] if l == n_l - 1 else [('ffn1_w_gu', l + 1)]
        dx2, dd, dg, du, sg['ffn2_pre_g'], sg['ffn2_post_g'], *loss_rows = run(
            _ffn_bwd_act, dh, sv['d2'], sv['x2'], vec('ffn2_pre_g', l), vec('ffn2_post_g', l), sv['g2'], sv['u2'],
            lw['ffn2_w_gu'], lw['ffn2_w_down'], f"ffn2_bwd_act_l{l}", keys=keys,
            loss_target=target if l == n_l - 1 else None)
        if loss_rows:
            loss_row = loss_rows[0]
        keys = [] if l == n_l - 1 else [('ffn1_w_down', l + 1), ('w_in', l + 1, c_odd)]
        dwg, dwu, dwd = run(_ffn_bwd_w, sv['h2'], dd, sv['g2'], sv['u2'], dg, du, f"ffn2_bwd_w_l{l}", keys=keys)
        ready[('ffn2_w_gu', l)] = [dwg, dwu]
        ready[('ffn2_w_down', l)] = [dwd.reshape(N_DEV, -1, d)]
        dya, dyb, dyc, dw_out, sg['mix_post_g'], sg['group_g'] = run(
            _mix_out_bwd, dx2, sv['o'], sv['ya'], sv['yb'], sv['yc'], vec('group_g', l), lw['w_out'].reshape(-1, d),
            vec('mix_post_g', l), f"mix_out_bwd_l{l}")
        ready[('w_out', l)] = [dw_out.reshape(N_DEV, -1, d)]
        (dlx, dlg, sg['lru_conv_w'], sg['lru_conv_b'], dwa, sg['lru_b_a'], dwx, sg['lru_b_x'],
         sg['lru_lambda']) = run(_lru_bwd, dya, sv['lx'], sv['lg'], sv['hs'], sv['lru_p'], f"lru_bwd_l{l}")
        sg['lru_w_a'] = _diag_blocks(dwa, A_BLOCKS)
        sg['lru_w_x'] = _diag_blocks(dwx, A_BLOCKS)
        dq, dk, dv, dk_up, dv_up, dsk = run(_attn_bwd, dyb, sv['q'], sv['k'], sv['v'], sv['sink_row'],
                                            f"attn_bwd_l{l}", keys=[('ffn2_w_down', l, c_even)] if l == 0 else [])
        sg['attn_sinks'] = dsk[:, 0]
        dy1, dcw, sg['conv_b'], sg['conv_ln_g'], sg['conv_ln_b'] = run(
            _cc_bwd_conv, dyc, sv['glu'], *sv['cc_p'], f"cconv_bwd_conv_l{l}", keys=[('w_out', l)] if l == 0 else [])
        sg['conv_w'] = dcw[:CC_K]
        (dglu,) = run(_cc_bwd_glu, dy1, sv['glu'], sv['cc_p'][0], f"cconv_bwd_glu_l{l}")
        dx1, dw_in, sg['mix_pre_g'] = run(
            _mix_in_bwd, dx2, sv['x1'], vec('mix_pre_g', l), sv['hn'], lw['w_in'].reshape(D_IN_PROJ, d),
            dlx, dlg, dq, dk, dk_up, dv, dv_up, dglu, f"mix_in_bwd_l{l}",
            keys=[('ffn2_w_down', l, c_odd)] if l == 0 else [('w_out', l)])
        ready[('w_in', l)] = [dw_in.reshape(N_DEV, -1, d)]
        dh, dd, dg, du, sg['ffn1_pre_g'], sg['ffn1_post_g'] = run(
            _ffn_bwd_act, dx1, sv['d1'], sv['x0'], vec('ffn1_pre_g', l), vec('ffn1_post_g', l), sv['g1'], sv['u1'],
            lw['ffn1_w_gu'], lw['ffn1_w_down'], f"ffn1_bwd_act_l{l}",
            keys=[('ffn2_w_gu', l), ('w_in', l)] if l == 0 else [('ffn2_w_gu', l)])
        if l > 0:
            dwg, dwu, dwd = run(_ffn_bwd_w, sv['h1'], dd, sv['g1'], sv['u1'], dg, du, f"ffn1_bwd_w_l{l}",
                                keys=[('ffn2_w_down', l), ('w_in', l, c_even)])
            ready[('ffn1_w_gu', l)] = [dwg, dwu]
            ready[('ffn1_w_down', l)] = [dwd.reshape(N_DEV, -1, d)]
        else:
            part = _pack([jnp.stack([small[j][k] for j in range(n_l)]) for k in SMALL] + [loss_row])
            (recv['ffn1_w_gu'], recv['ffn1_w_down']), ex = _ffn_bwd_w_send(
                sv['h1'], dd, sv['g1'], sv['u1'], dg, du, recv['ffn1_w_gu'], recv['ffn1_w_down'], 0, "ffn1_bwd_w_send_l0",
                [_gather_a([(part, None)], two_level=False)])
            small_parts = ex[0][0]
    grad_x = dh.reshape(1, s, d)

    out = {}
    for k in BIG:
        res = _reduce_adamw(recv[k], tview(w, k), tview(m, k), tview(v, k), f"reduce_adamw_{k}")
        out[k] = [jnp.swapaxes(r, 1, 2) for r in res] if k in TRANSPOSED else res

    small_shapes = [(n_l,) + tuple(small[0][k].shape) for k in SMALL]

    def widen(t, k):
        if k not in CHANNEL_SHARDED:
            return t.reshape((n_l,) + tuple(small[0][k].shape))
        full = jnp.zeros((n_l,) + tuple(small[0][k].shape), F32)
        return lax.dynamic_update_slice_in_dim(full, t, me * ch, axis=2)

    no_w = jnp.zeros(loss_row.shape, F32)
    packed = [_pack([widen(src[k], k) for k in SMALL] + [no_w]) for src in (w, m, v)]
    res = _reduce_adamw_small(small_parts, *packed, "reduce_adamw_small")
    loss = _unpack(res[0], small_shapes + [loss_row.shape])[-1][0, 0]
    for k, g, dlt, nm, nv in zip(SMALL, *[_unpack(r, small_shapes) for r in res]):
        vals = [g, dlt, nm, nv]
        if k in CHANNEL_SHARDED:
            vals = [lax.dynamic_slice_in_dim(t, me * ch, ch, axis=2) for t in vals]
        out[k] = [t.reshape(w[k].shape) for t in vals]

    return (loss, grad_x, *[out[k][0] for k in WEIGHT_NAMES], *[out[k][1] for k in WEIGHT_NAMES],
            *[out[k][2] for k in WEIGHT_NAMES], *[out[k][3] for k in WEIGHT_NAMES])


def kernel(x, ffn1_pre_g, ffn1_w_gu, ffn1_w_down, ffn1_post_g, mix_pre_g, w_in, lru_conv_w, lru_conv_b, lru_w_a, lru_b_a, lru_w_x, lru_b_x, lru_lambda, attn_sinks, conv_w, conv_b, conv_ln_g, conv_ln_b, group_g, w_out, mix_post_g, ffn2_pre_g, ffn2_w_gu, ffn2_w_down, ffn2_post_g, loss_target, m_ffn1_pre_g, m_ffn1_w_gu, m_ffn1_w_down, m_ffn1_post_g, m_mix_pre_g, m_w_in, m_lru_conv_w, m_lru_conv_b, m_lru_w_a, m_lru_b_a, m_lru_w_x, m_lru_b_x, m_lru_lambda, m_attn_sinks, m_conv_w, m_conv_b, m_conv_ln_g, m_conv_ln_b, m_group_g, m_w_out, m_mix_post_g, m_ffn2_pre_g, m_ffn2_w_gu, m_ffn2_w_down, m_ffn2_post_g, v_ffn1_pre_g, v_ffn1_w_gu, v_ffn1_w_down, v_ffn1_post_g, v_mix_pre_g, v_w_in, v_lru_conv_w, v_lru_conv_b, v_lru_w_a, v_lru_b_a, v_lru_w_x, v_lru_b_x, v_lru_lambda, v_attn_sinks, v_conv_w, v_conv_b, v_conv_ln_g, v_conv_ln_b, v_group_g, v_w_out, v_mix_post_g, v_ffn2_pre_g, v_ffn2_w_gu, v_ffn2_w_down, v_ffn2_post_g):
    args = (ffn1_pre_g, ffn1_w_gu, ffn1_w_down, ffn1_post_g, mix_pre_g, w_in, lru_conv_w, lru_conv_b, lru_w_a, lru_b_a, lru_w_x, lru_b_x, lru_lambda, attn_sinks, conv_w, conv_b, conv_ln_g, conv_ln_b, group_g, w_out, mix_post_g, ffn2_pre_g, ffn2_w_gu, ffn2_w_down, ffn2_post_g)
    ms = (m_ffn1_pre_g, m_ffn1_w_gu, m_ffn1_w_down, m_ffn1_post_g, m_mix_pre_g, m_w_in, m_lru_conv_w, m_lru_conv_b, m_lru_w_a, m_lru_b_a, m_lru_w_x, m_lru_b_x, m_lru_lambda, m_attn_sinks, m_conv_w, m_conv_b, m_conv_ln_g, m_conv_ln_b, m_group_g, m_w_out, m_mix_post_g, m_ffn2_pre_g, m_ffn2_w_gu, m_ffn2_w_down, m_ffn2_post_g)
    vs = (v_ffn1_pre_g, v_ffn1_w_gu, v_ffn1_w_down, v_ffn1_post_g, v_mix_pre_g, v_w_in, v_lru_conv_w, v_lru_conv_b, v_lru_w_a, v_lru_b_a, v_lru_w_x, v_lru_b_x, v_lru_lambda, v_attn_sinks, v_conv_w, v_conv_b, v_conv_ln_g, v_conv_ln_b, v_group_g, v_w_out, v_mix_post_g, v_ffn2_pre_g, v_ffn2_w_gu, v_ffn2_w_down, v_ffn2_post_g)
    return _step(x, loss_target, dict(zip(WEIGHT_NAMES, args)), dict(zip(WEIGHT_NAMES, ms)), dict(zip(WEIGHT_NAMES, vs)))
```

```python
import functools
import math
import operator

import jax
import jax.numpy as jnp
from jax import lax
from jax.experimental import pallas as pl
from jax.experimental.pallas import tpu as pltpu

F32 = jnp.float32
BF16 = jnp.bfloat16
N_DEV = 8
AXES = ("x", "y", "c")
MESH = pl.DeviceIdType.MESH

NORM_EPS = 1e-6
LN_EPS = 1e-5
NEG_BIG = -1e30
W_A = 256
W_B = 512
W_C = 256
HEAD_DIM = 64
N_Q_HEADS = 8
N_KV_HEADS = 2
Q_PER_KV = N_Q_HEADS // N_KV_HEADS
KV_W = N_KV_HEADS * HEAD_DIM
BLK = 128
LRU_K = 4
LRU_C = 8.0
A_BLOCKS = 4
CC_K = 31
CC_HALO = 32
LRU_HALO = 8
D_IN_PROJ = 2 * W_A + W_B + 2 * KV_W + 2 * W_C
ADAM_LR = 0.001
ADAM_B1 = 0.9
ADAM_B2 = 0.999
ADAM_EPS = 1e-08
ADAM_WD = 0.01
ADAM_STEP = 10
VMEM_LIMIT = 56 * 1024 * 1024

SDS = jax.ShapeDtypeStruct
ANY = pl.BlockSpec(memory_space=pl.ANY)


def _time_tile(s):
    return max(BLK, s // 8)


class _Exchange:
    def __init__(self, inputs, out_shapes, aliases, sem_shapes, start, wait):
        self.inputs, self.out_shapes, self.aliases, self.sem_shapes = inputs, out_shapes, aliases, sem_shapes
        self.start, self.wait = start, wait


def _my_pos():
    x, y, c = (lax.axis_index(a) for a in AXES)
    return x, y, c, 4 * x + 2 * y + c


def _flip(k):
    x, y, c, _ = _my_pos()
    return (1 - x if k & 4 else x, 1 - y if k & 2 else y, 1 - c if k & 1 else c)


def _slot(dev):
    return 4 * dev[0] + 2 * dev[1] + dev[2]


def _dev(p):
    return (p >> 2, (p >> 1) & 1, p & 1)


def _gather_a(items, two_level):
    rels = (1, 2, 4, 6) if two_level else tuple(range(1, N_DEV))
    items = [tuple(it) + (None,) * (4 - len(it)) for it in items]
    n = len(items)
    with_buf = [a for a in range(n) if items[a][3] is not None]

    def rows_of(ref, a):
        return ref if items[a][2] is None else ref.at[pl.ds(*items[a][2])]

    def src_of(ins, a):
        return rows_of(ins[a] if items[a][1] is None else ins[a].at[items[a][1]], a)

    def dst_of(outs, a, slot):
        return rows_of(outs[a].at[slot], a)

    def shape_of(a):
        arr, l = items[a][:2]
        return arr.shape if l is None else arr.shape[1:]

    def copies(ins, outs, sems, a):
        send, recv, _ = sems
        me = _my_pos()[3]
        return [(k, pltpu.make_async_remote_copy(
            src_ref=src_of(ins, a), dst_ref=dst_of(outs, a, me), send_sem=send.at[a, k], recv_sem=recv.at[a, k],
            device_id=_flip(k), device_id_type=MESH)) for k in rels]

    def local(ins, outs, sems, a):
        return pltpu.make_async_copy(src_of(ins, a), dst_of(outs, a, _my_pos()[3]), sems[2].at[a])

    def start(ins, outs, sems):
        for a in range(n):
            local(ins, outs, sems, a).start()
            for _, cp in copies(ins, outs, sems, a):
                cp.start()

    def wait(ins, outs, sems):
        send, recv, _ = sems
        for a in range(n):
            for k, cp in copies(ins, outs, sems, a):
                pltpu.make_async_remote_copy(
                    src_ref=src_of(ins, a), dst_ref=dst_of(outs, a, _slot(_flip(k))), send_sem=send.at[a, k],
                    recv_sem=recv.at[a, k], device_id=_flip(k), device_id_type=MESH).wait_recv()
                cp.wait_send()
            local(ins, outs, sems, a).wait()

    return _Exchange([it[0] for it in items] + [items[a][3] for a in with_buf],
                     [SDS((N_DEV,) + shape_of(a), items[a][0].dtype) for a in range(n)],
                     {n + j: a for j, a in enumerate(with_buf)},
                     [pltpu.SemaphoreType.DMA((n, N_DEV)), pltpu.SemaphoreType.DMA((n, N_DEV)),
                      pltpu.SemaphoreType.DMA((n,))], start, wait)


def _gather_b(bufs):
    n = len(bufs)

    def copies(ins, outs, sems, a, c_of_block):
        send, recv = sems
        x, y, c, _ = _my_pos()
        res = []
        for k in (2, 4, 6):
            chip = _flip(k)
            blk = _slot((chip[0], chip[1], c if c_of_block == "mine" else 1 - c))
            res.append(pltpu.make_async_remote_copy(
                src_ref=ins[a].at[blk], dst_ref=outs[a].at[blk], send_sem=send.at[a, k], recv_sem=recv.at[a, k],
                device_id=_flip(1), device_id_type=MESH))
        return res

    def start(ins, outs, sems):
        for a in range(n):
            for cp in copies(ins, outs, sems, a, "mine"):
                cp.start()

    def wait(ins, outs, sems):
        for a in range(n):
            for cp in copies(ins, outs, sems, a, "sibling"):
                cp.wait_recv()
            for cp in copies(ins, outs, sems, a, "mine"):
                cp.wait_send()

    return _Exchange(list(bufs), [SDS(b.shape, b.dtype) for b in bufs], {a: a for a in range(n)},
                     [pltpu.SemaphoreType.DMA((n, N_DEV)), pltpu.SemaphoreType.DMA((n, N_DEV))], start, wait)


def _grad_x(items, n_l):
    items = [tuple(it) + (None,) * (4 - len(it)) for it in items]
    n = len(items)
    owners = [tuple(range(N_DEV)) if it[3] is None else tuple(it[3]) for it in items]
    inputs, first_in, aliases, out_shapes = [], [], {}, []
    for a, (arrs, l, recv, _) in enumerate(items):
        first_in.append(len(inputs))
        inputs += list(arrs)
        assert sum(arr.shape[0] for arr in arrs) == N_DEV
        if recv is not None:
            aliases[len(inputs)] = a
            inputs.append(recv)
        out_shapes.append(SDS((N_DEV, n_l) + arrs[0].shape[1:], arrs[0].dtype))

    def slab(ins, a, p):
        off = 0
        for j, arr in enumerate(items[a][0]):
            if p < off + arr.shape[0]:
                return ins[first_in[a] + j].at[p - off]
            off += arr.shape[0]
        raise AssertionError

    def rdma(ins, outs, sems, a, p, src_dev):
        send, recv, _ = sems
        return pltpu.make_async_remote_copy(
            src_ref=slab(ins, a, p), dst_ref=outs[a].at[src_dev, items[a][1]], send_sem=send.at[a, p],
            recv_sem=recv.at[a, src_dev], device_id=_dev(p), device_id_type=MESH)

    def local(ins, outs, sems, a, p):
        return pltpu.make_async_copy(slab(ins, a, p), outs[a].at[p, items[a][1]], sems[2].at[a])

    def start(ins, outs, sems):
        me = _my_pos()[3]
        for p in range(N_DEV):
            mine = [a for a in range(n) if p in owners[a]]

            @pl.when(me != p)
            def _():
                for a in mine:
                    rdma(ins, outs, sems, a, p, me).start()

            @pl.when(me == p)
            def _():
                for a in mine:
                    local(ins, outs, sems, a, p).start()

    def wait(ins, outs, sems):
        me = _my_pos()[3]
        for a in range(n):
            i_own = functools.reduce(operator.or_, [me == q for q in owners[a]])
            for p in range(N_DEV):
                @pl.when((me != p) & i_own)
                def _():
                    rdma(ins, outs, sems, a, p, p).wait_recv()

                if p in owners[a]:
                    @pl.when(me != p)
                    def _():
                        rdma(ins, outs, sems, a, p, p).wait_send()

                    @pl.when(me == p)
                    def _():
                        local(ins, outs, sems, a, p).wait()

    return _Exchange(inputs, out_shapes, aliases,
                     [pltpu.SemaphoreType.DMA((n, N_DEV)), pltpu.SemaphoreType.DMA((n, N_DEV)),
                      pltpu.SemaphoreType.DMA((n,))], start, wait)


def _pcall(body, args, *, name, grid, in_specs, out_specs, out_shape, scratch_shapes=(), carry=(), body_aliases=None):
    n_in, n_out, n_scr = len(in_specs), len(out_specs), len(scratch_shapes)
    c_in = [len(e.inputs) for e in carry]
    c_out = [len(e.out_shapes) for e in carry]
    c_sem = [len(e.sem_shapes) for e in carry]
    aliases = dict(body_aliases or {})
    for j, e in enumerate(carry):
        for i_loc, o_loc in e.aliases.items():
            aliases[n_in + sum(c_in[:j]) + i_loc] = n_out + sum(c_out[:j]) + o_loc

    def wrapped(*refs):
        def take(counts, pos):
            groups = []
            for cnt in counts:
                groups.append(refs[pos:pos + cnt])
                pos += cnt
            return groups, pos

        (ins,), pos = take([n_in], 0)
        cins, pos = take(c_in, pos)
        (outs,), pos = take([n_out], pos)
        couts, pos = take(c_out, pos)
        (scr,), pos = take([n_scr], pos)
        csems, pos = take(c_sem, pos)
        if carry:
            ids = [pl.program_id(k) for k in range(len(grid))]
            first = functools.reduce(operator.and_, [i == 0 for i in ids])
            last = functools.reduce(operator.and_, [i == g - 1 for i, g in zip(ids, grid)])

            @pl.when(first)
            def _():
                for e, ci, co, cs in zip(carry, cins, couts, csems):
                    e.start(ci, co, cs)

        body(*ins, *outs, *scr)
        if carry:
            @pl.when(last)
            def _():
                for e, ci, co, cs in zip(carry, cins, couts, csems):
                    e.wait(ci, co, cs)

    res = pl.pallas_call(
        wrapped, name=name, grid=grid,
        in_specs=list(in_specs) + [ANY] * sum(c_in),
        out_specs=list(out_specs) + [ANY] * sum(c_out),
        out_shape=list(out_shape) + [s for e in carry for s in e.out_shapes],
        scratch_shapes=list(scratch_shapes) + [s for e in carry for s in e.sem_shapes],
        input_output_aliases=aliases,
        compiler_params=pltpu.CompilerParams(dimension_semantics=("arbitrary",) * len(grid),
                                             vmem_limit_bytes=VMEM_LIMIT),
    )(*args, *[a for e in carry for a in e.inputs])
    outs, pos, extra = list(res[:n_out]), n_out, []
    for cnt in c_out:
        extra.append(list(res[pos:pos + cnt]))
        pos += cnt
    return outs, extra


def _all_gather(items, name):
    n = len(items)
    shape_of = lambda a: items[a][0].shape if items[a][1] is None else items[a][0].shape[1:]

    def body(*refs):
        ins, outs, (send_sems, recv_sems, local_sems) = refs[:n], refs[n:2 * n], refs[2 * n:]
        x, y, c, me = _my_pos()
        src_of = lambda a: ins[a] if items[a][1] is None else ins[a].at[items[a][1]]

        def copy(a, k, block, to, src=None):
            dst = outs[a].at[_slot(block)]
            return pltpu.make_async_remote_copy(
                src_ref=dst if src is None else src, dst_ref=dst,
                send_sem=send_sems.at[a, k], recv_sem=recv_sems.at[a, k], device_id=to, device_id_type=MESH)

        mine = [pltpu.make_async_copy(src_of(a), outs[a].at[me], local_sems.at[a]) for a in range(n)]
        for cp in mine:
            cp.start()
        first = [copy(a, k, (x, y, c), _flip(k), src=src_of(a)) for a in range(n) for k in (1, 2, 4, 6)]
        for cp in first:
            cp.start()
        passed = []
        for k in (2, 4, 6):
            for a in range(n):
                copy(a, k, _flip(k), (x, y, c)).wait_recv()
                fwd = copy(a, k + 1, _flip(k), _flip(1))
                fwd.start()
                passed.append(fwd)
        for a in range(n):
            copy(a, 1, _flip(1), (x, y, c)).wait_recv()
            for k in (2, 4, 6):
                copy(a, k + 1, _flip(k + 1), (x, y, c)).wait_recv()
        for cp in first + passed:
            cp.wait_send()
        for cp in mine:
            cp.wait()

    return pl.pallas_call(
        body, name=name,
        in_specs=[ANY] * n, out_specs=[ANY] * n,
        out_shape=[SDS((N_DEV,) + shape_of(a), items[a][0].dtype) for a in range(n)],
        scratch_shapes=[pltpu.SemaphoreType.DMA((n, N_DEV)), pltpu.SemaphoreType.DMA((n, N_DEV)),
                        pltpu.SemaphoreType.DMA((n,))],
    )(*[it[0] for it in items])


def _mm(a, b):
    return jnp.dot(a.astype(BF16), b.astype(BF16), preferred_element_type=F32)


def _mm_nt(a, b):
    return lax.dot_general(a.astype(BF16), b.astype(BF16), (((1,), (1,)), ((), ())), preferred_element_type=F32)


def _mm_tn(a, b):
    return lax.dot_general(a.astype(BF16), b.astype(BF16), (((0,), (0,)), ((), ())), preferred_element_type=F32)


def _rms_r(x):
    return lax.rsqrt(jnp.mean(x * x, axis=-1, keepdims=True) + NORM_EPS)


def _rms_bwd(x, r, g, dy):
    gy = dy * g
    dx = r * (gy - x * (r * r) * jnp.mean(gy * x, axis=-1, keepdims=True))
    dg = jnp.sum(dy * x * r, axis=0, keepdims=True)
    return dx, dg


def _sigmoid(x):
    return 1.0 / (1.0 + jnp.exp(-x))


def _sigmoid_t(x):
    return 0.5 * jnp.tanh(0.5 * x) + 0.5


def _dsilu(z, sz):
    return sz * (1.0 + z * (1.0 - sz))


def _swiglu_bf16(g, u):
    sg = 0.5 * jnp.tanh(0.5 * g) + 0.5
    silu = g * sg
    return silu * u, silu, sg + silu * (1.0 - sg)


_GELU_C = math.sqrt(2.0 / math.pi)


def _gelu(x):
    t = jnp.tanh(_GELU_C * (x + 0.044715 * x * x * x))
    return 0.5 * x * (1.0 + t), t


def _dgelu(x, t):
    return 0.5 * (1.0 + t) + 0.5 * x * (1.0 - t * t) * _GELU_C * (1.0 + 3.0 * 0.044715 * x * x)


def _log1p(e):
    return jnp.where(e < 1e-2, e * (1.0 - e * (0.5 - e * (1.0 / 3.0))), jnp.log(1.0 + e))


def _softplus(x):
    return jnp.maximum(x, 0.0) + _log1p(jnp.exp(-jnp.abs(x)))


def _neg_expm1(x):
    small = -x * (1.0 + x * (0.5 + x * (1.0 / 6.0) * (1.0 + x * 0.25)))
    return jnp.where(x > -1e-2, small, 1.0 - jnp.exp(x))


def _shift_down(x, s):
    return x if s == 0 else pltpu.roll(x, s, 0)


def _shift_up(x, s):
    return x if s == 0 else pltpu.roll(x, x.shape[0] - s, 0)


def _ffn_wspecs(d, fc, order):
    f_of = (lambda i, f: f) if order == "tf" else (lambda f, i: f)
    n_f = N_DEV // 2
    return [pl.BlockSpec((None, fc, d), lambda *g: (f_of(*g), 0, 0)),
            pl.BlockSpec((None, fc, d), lambda *g: (f_of(*g) + n_f, 0, 0)),
            pl.BlockSpec((2, fc // 2, d), lambda *g: (f_of(*g), 0, 0))]


def _ffn_fwd(x, pre_g, post_g, wgu_t, wd, name, carry=()):
    s, d = x.shape
    fc = wgu_t.shape[1]
    ts = 2 * _time_tile(s)
    n_t, n_f = s // ts, N_DEV // 2

    def body(x_ref, pg_ref, qg_ref, wg_ref, wu_ref, wd_ref, xo_ref, h_ref, g_ref, u_ref, d_ref, h_scr, acc):
        f = pl.program_id(1)

        @pl.when(f == 0)
        def _():
            xv = x_ref[...]
            hv = (xv * _rms_r(xv) * pg_ref[...]).astype(BF16)
            h_scr[...] = hv
            h_ref[...] = hv
            acc[...] = jnp.zeros_like(acc)

        hv = h_scr[...]
        g = _mm_nt(hv, wg_ref[...])
        u = _mm_nt(hv, wu_ref[...])
        g = g.astype(BF16)
        u = u.astype(BF16)
        g_ref[...] = g
        u_ref[...] = u
        acc[...] += jnp.dot(_swiglu_bf16(g, u)[0], wd_ref[...].reshape(fc, d), preferred_element_type=F32)

        @pl.when(f == n_f - 1)
        def _():
            dv = acc[...]
            d_ref[...] = dv.astype(BF16)
            xo_ref[...] = x_ref[...] + 0.5 * (dv * _rms_r(dv) * qg_ref[...])

    row = pl.BlockSpec((ts, d), lambda i, f: (i, 0))
    vec = pl.BlockSpec((1, d), lambda i, f: (0, 0))
    act = pl.BlockSpec((None, ts, fc), lambda i, f: (f, i, 0))
    return _pcall(
        body, (x, pre_g, post_g, wgu_t, wgu_t, wd), name=name, grid=(n_t, n_f),
        in_specs=[row, vec, vec] + _ffn_wspecs(d, fc, "tf"),
        out_specs=[row, row, act, act, row],
        out_shape=[SDS((s, d), F32), SDS((s, d), BF16), SDS((n_f, s, fc), BF16), SDS((n_f, s, fc), BF16),
                   SDS((s, d), BF16)],
        scratch_shapes=[pltpu.VMEM((ts, d), BF16), pltpu.VMEM((ts, d), F32)], carry=carry)


def _ffn_bwd_act(dxo, dmid, x, pre_g, post_g, g_s, u_s, wgu_t, wd, name, carry=(), loss_target=None):
    s, d = x.shape
    fc = wgu_t.shape[1]
    ts = _time_tile(s)
    n_t, n_f = s // ts, N_DEV // 2
    with_loss = loss_target is not None

    def body(*refs):
        t_ref, refs = (refs[0], refs[1:]) if with_loss else (None, refs)
        (dxo_ref, dm_ref, x_ref, pg_ref, qg_ref, g_ref, u_ref, wg_ref, wu_ref, wd_ref,
         dx_ref, dd_ref, dg_ref, du_ref, dpg_ref, dqg_ref) = refs[:16]
        loss_ref = refs[16] if with_loss else None
        dd_scr, dh_acc = refs[-2:]
        i, f = pl.program_id(0), pl.program_id(1)

        def incoming():
            return (dxo_ref[...] - t_ref[...]) * (1.0 / d) if with_loss else dxo_ref[...]

        @pl.when((i == 0) & (f == 0))
        def _():
            dpg_ref[...] = jnp.zeros_like(dpg_ref)
            dqg_ref[...] = jnp.zeros_like(dqg_ref)
            if with_loss:
                loss_ref[...] = jnp.zeros_like(loss_ref)

        @pl.when(f == 0)
        def _():
            if with_loss:
                err = dxo_ref[...] - t_ref[...]
                loss_ref[...] += 0.5 * jnp.sum(jnp.mean(err * err, axis=-1, keepdims=True), axis=0, keepdims=True)
            dv = dm_ref[...].astype(F32)
            ddv, dq = _rms_bwd(dv, _rms_r(dv), qg_ref[...], 0.5 * incoming())
            dqg_ref[...] += dq
            dd_scr[...] = ddv.astype(BF16)
            dd_ref[...] = ddv.astype(BF16)
            dh_acc[...] = jnp.zeros_like(dh_acc)

        da = _mm_nt(dd_scr[...], wd_ref[...].reshape(fc, d)).astype(BF16)
        u = u_ref[...]
        _, silu, dsilu = _swiglu_bf16(g_ref[...], u)
        du = da * silu
        dg = da * u * dsilu
        dg_ref[...] = dg
        du_ref[...] = du
        dh_acc[...] += _mm(dg, wg_ref[...]) + _mm(du, wu_ref[...])

        @pl.when(f == n_f - 1)
        def _():
            xv = x_ref[...]
            dxv, dp = _rms_bwd(xv, _rms_r(xv), pg_ref[...], dh_acc[...])
            dpg_ref[...] += dp
            dx_ref[...] = incoming() + dxv

    row = pl.BlockSpec((ts, d), lambda i, f: (i, 0))
    vec = pl.BlockSpec((1, d), lambda i, f: (0, 0))
    act = pl.BlockSpec((None, ts, fc), lambda i, f: (f, i, 0))
    lead = [loss_target] if with_loss else []
    return _pcall(
        body, (*lead, dxo, dmid, x, pre_g, post_g, g_s, u_s, wgu_t, wgu_t, wd), name=name, grid=(n_t, n_f),
        in_specs=[row] * len(lead) + [row, row, row, vec, vec, act, act] + _ffn_wspecs(d, fc, "tf"),
        out_specs=[row, row, act, act, vec, vec] + [pl.BlockSpec((1, BLK), lambda i, f: (0, 0))] * len(lead),
        out_shape=[SDS((s, d), F32), SDS((s, d), BF16), SDS((n_f, s, fc), BF16), SDS((n_f, s, fc), BF16),
                   SDS((1, d), F32), SDS((1, d), F32)] + [SDS((1, BLK), F32)] * len(lead),
        scratch_shapes=[pltpu.VMEM((ts, d), BF16), pltpu.VMEM((ts, d), F32)], carry=carry)


def _ffn_bwd_w(h, dd, g_s, u_s, dg, du, name, carry=()):
    s, d = h.shape
    n_f, _, fc = g_s.shape
    ts = 2 * _time_tile(s)
    n_t = s // ts

    def body(h_ref, dd_ref, g_ref, u_ref, dg_ref, du_ref, wg_ref, wu_ref, wd_ref, acc_g, acc_u, acc_d):
        i = pl.program_id(1)

        @pl.when(i == 0)
        def _():
            acc_g[...] = jnp.zeros_like(acc_g)
            acc_u[...] = jnp.zeros_like(acc_u)
            acc_d[...] = jnp.zeros_like(acc_d)

        a = _swiglu_bf16(g_ref[...], u_ref[...])[0]
        hv = h_ref[...]
        acc_g[...] += _mm_tn(dg_ref[...], hv)
        acc_u[...] += _mm_tn(du_ref[...], hv)
        acc_d[...] += _mm_tn(a, dd_ref[...])

        @pl.when(i == n_t - 1)
        def _():
            wg_ref[...] = acc_g[...].astype(BF16)
            wu_ref[...] = acc_u[...].astype(BF16)
            wd_ref[...] = acc_d[...].astype(BF16)

    row = pl.BlockSpec((ts, d), lambda f, i: (i, 0))
    act = pl.BlockSpec((None, ts, fc), lambda f, i: (f, i, 0))
    out = pl.BlockSpec((None, fc, d), lambda f, i: (f, 0, 0))
    return _pcall(
        body, (h, dd, g_s, u_s, dg, du), name=name, grid=(n_f, n_t),
        in_specs=[row, row, act, act, act, act], out_specs=[out, out, out],
        out_shape=[SDS((n_f, fc, d), BF16)] * 3,
        scratch_shapes=[pltpu.VMEM((fc, d), F32)] * 3, carry=carry)


def _ffn_bwd_w_send(h, dd, g_s, u_s, dg, du, recv_gu, recv_d, layer, name, carry=()):
    s, d = h.shape
    n_f, _, fc = g_s.shape
    ts = _time_tile(s)
    n_t = s // ts
    half = fc // 2

    def chunk_of(step):
        return (step + 2 * lax.axis_index("x") + lax.axis_index("y")) % n_f

    def body(h_ref, dd_ref, g_ref, u_ref, dg_ref, du_ref, _rgu_in, _rd_in, rgu_ref, rd_ref,
             acc_g, acc_u, acc_d, st_g, st_u, st_d, pair_gu, pair_d, zeros,
             send_sems, recv_sems, local_sems, pair_sems, zero_sems):
        f, i = pl.program_id(0), pl.program_id(1)
        x, y, c_me, me = _my_pos()
        sibling = me ^ 1

        @pl.when(i == 0)
        def _():
            acc_g[...] = jnp.zeros_like(acc_g)
            acc_u[...] = jnp.zeros_like(acc_u)
            acc_d[...] = jnp.zeros_like(acc_d)

        def zero_fills():
            res = []
            for n_k, k in enumerate((2, 4, 6)):
                other = _flip(k)
                slot = _slot((other[0], other[1], 1 - c_me))
                res += [pltpu.make_async_copy(zeros, rgu_ref.at[slot, layer, pl.ds(0, half)], zero_sems.at[n_k, 0]),
                        pltpu.make_async_copy(zeros, rgu_ref.at[slot, layer, pl.ds(half, half)], zero_sems.at[n_k, 1]),
                        pltpu.make_async_copy(zeros, rd_ref.at[slot, layer], zero_sems.at[n_k, 2])]
            return res

        @pl.when((f == 0) & (i == 0))
        def _():
            zeros[...] = jnp.zeros_like(zeros)
            for cp in zero_fills():
                cp.start()

        a = _swiglu_bf16(g_ref[...], u_ref[...])[0]
        hv = h_ref[...]
        acc_g[...] += _mm_tn(dg_ref[...], hv)
        acc_u[...] += _mm_tn(du_ref[...], hv)
        acc_d[...] += _mm_tn(a, dd_ref[...])

        def messages(fs):
            c = chunk_of(fs)
            lo, hi = pl.ds(0, half), pl.ds(half, half)
            return [(st_g.at[fs], pair_gu.at[fs // 2, 0], rgu_ref, 0, c, 0),
                    (st_u.at[fs], pair_gu.at[fs // 2, 1], rgu_ref, 0, c + n_f, 1),
                    (st_d.at[fs, lo], pair_d.at[fs], rd_ref, 1, 2 * c, 2),
                    (st_d.at[fs, hi], pair_d.at[fs], rd_ref, 1, 2 * c + 1, 3)]

        def roles(p):
            same_chip = (p >> 1) == (me >> 1)
            same_c = (p & 1) == c_me
            return p == me, p == sibling, (~same_chip) & same_c, (~same_chip) & (~same_c)

        def to_owner(fs, msg, src_dev):
            src, _, buf, row, p, j = msg
            return pltpu.make_async_remote_copy(
                src_ref=src, dst_ref=buf.at[src_dev, layer], send_sem=send_sems.at[fs, j],
                recv_sem=recv_sems.at[row, src_dev], device_id=_dev(p), device_id_type=MESH)

        def to_pair(fs, msg):
            src, pair, _, _, _, j = msg
            return pltpu.make_async_remote_copy(
                src_ref=src, dst_ref=pair, send_sem=send_sems.at[fs, j], recv_sem=pair_sems.at[fs, j],
                device_id=_dev(sibling), device_id_type=MESH)

        def local(fs, msg):
            src, _, buf, _, p, j = msg
            return pltpu.make_async_copy(src, buf.at[p, layer], local_sems.at[fs, j])

        for fs in range(n_f):
            @pl.when((f == fs) & (i == n_t - 1))
            def _():
                st_g[fs] = acc_g[...].astype(BF16)
                st_u[fs] = acc_u[...].astype(BF16)
                st_d[fs] = acc_d[...].astype(BF16)
                msgs = messages(fs)
                for msg in msgs:
                    mine, sib, _, hand_over = roles(msg[4])

                    @pl.when(mine)
                    def _():
                        local(fs, msg).start()

                    @pl.when(sib)
                    def _():
                        to_owner(fs, msg, me).start()

                    @pl.when(hand_over)
                    def _():
                        to_pair(fs, msg).start()
                for msg in msgs:
                    @pl.when(roles(msg[4])[2])
                    def _():
                        src, pair = msg[0], msg[1]
                        to_pair(fs, msg).wait_recv()
                        src[...] = (src[...].astype(F32) + pair[...].astype(F32)).astype(BF16)
                        to_owner(fs, msg, me).start()

        @pl.when((f == n_f - 1) & (i == n_t - 1))
        def _():
            for fs in range(n_f):
                for msg in messages(fs):
                    mine = roles(msg[4])[0]

                    @pl.when(mine)
                    def _():
                        local(fs, msg).wait()

                    @pl.when(~mine)
                    def _():
                        to_owner(fs, msg, me).wait_send()
            for k in (1, 2, 4, 6):
                src_dev = _slot(_flip(k))
                to_owner(0, messages(0)[0], src_dev).wait_recv()
                to_owner(0, messages(0)[2], src_dev).wait_recv()
            for cp in zero_fills():
                cp.wait()

    row = pl.BlockSpec((ts, d), lambda f, i: (i, 0))
    act = pl.BlockSpec((None, ts, fc), lambda f, i: (chunk_of(f), i, 0))
    return _pcall(
        body, (h, dd, g_s, u_s, dg, du, recv_gu, recv_d), name=name, grid=(n_f, n_t),
        in_specs=[row, row, act, act, act, act, ANY, ANY], out_specs=[ANY, ANY],
        out_shape=[SDS(recv_gu.shape, recv_gu.dtype), SDS(recv_d.shape, recv_d.dtype)],
        scratch_shapes=[pltpu.VMEM((fc, d), F32)] * 3 + [pltpu.VMEM((n_f, fc, d), BF16)] * 3
        + [pltpu.VMEM((n_f // 2, 2, fc, d), BF16), pltpu.VMEM((n_f, half, d), BF16), pltpu.VMEM((half, d), BF16)]
        + [pltpu.SemaphoreType.DMA((n_f, 4)), pltpu.SemaphoreType.DMA((2, N_DEV)), pltpu.SemaphoreType.DMA((n_f, 4)),
           pltpu.SemaphoreType.DMA((n_f, 4)), pltpu.SemaphoreType.DMA((3, 3))],
        carry=carry, body_aliases={6: 0, 7: 1})


_PROJ_WIDTHS = (W_A, W_A, W_B, KV_W, KV_W, 2 * W_C)


def _mix_in_fwd(x, pre_g, w_in_t, name, carry=()):
    s, d = x.shape
    ts = 2 * _time_tile(s)

    def body(x_ref, pg_ref, w_ref, hn_ref, *outs):
        xv = x_ref[...]
        hn = (xv * _rms_r(xv) * pg_ref[...]).astype(BF16)
        hn_ref[...] = hn
        proj = _mm_nt(hn, w_ref[...])
        off = 0
        for o_ref, w in zip(outs, _PROJ_WIDTHS):
            o_ref[...] = proj[:, off:off + w]
            off += w

    row = lambda w: pl.BlockSpec((ts, w), lambda i: (i, 0))
    return _pcall(
        body, (x, pre_g, w_in_t), name=name, grid=(s // ts,),
        in_specs=[row(d), pl.BlockSpec((1, d), lambda i: (0, 0)), pl.BlockSpec((D_IN_PROJ, d), lambda i: (0, 0))],
        out_specs=[row(d)] + [row(w) for w in _PROJ_WIDTHS],
        out_shape=[SDS((s, d), BF16)] + [SDS((s, w), F32) for w in _PROJ_WIDTHS], carry=carry)


def _mix_in_bwd(dres, x, pre_g, hn, w_in_t, dlx, dlg, dq, dk, dk_up, dv, dv_up, dglu, name, carry=()):
    s, d = x.shape
    ts = _time_tile(s)
    n_t = s // ts

    def body(dres_ref, x_ref, pg_ref, hn_ref, w_ref, dlx_ref, dlg_ref, dq_ref, dk_ref, dkn_ref,
             dv_ref, dvn_ref, dglu_ref, dx_ref, dw_ref, dpg_ref, acc):
        i = pl.program_id(0)

        @pl.when(i == 0)
        def _():
            acc[...] = jnp.zeros_like(acc)
            dpg_ref[...] = jnp.zeros_like(dpg_ref)

        def with_next(cur_ref, nxt_ref):
            nxt = jnp.where(i < n_t - 1, nxt_ref[...], 0.0)
            if ts == BLK:
                return cur_ref[...] + nxt
            return jnp.concatenate([cur_ref[:ts - BLK, :], cur_ref[ts - BLK:, :] + nxt], axis=0)

        dproj = jnp.concatenate([dlx_ref[...], dlg_ref[...], dq_ref[...], with_next(dk_ref, dkn_ref),
                                 with_next(dv_ref, dvn_ref), dglu_ref[...]], axis=1).astype(BF16)
        dhn = _mm(dproj, w_ref[...])
        acc[...] += _mm_tn(dproj, hn_ref[...])
        xv = x_ref[...]
        dxv, dp = _rms_bwd(xv, _rms_r(xv), pg_ref[...], dhn)
        dpg_ref[...] += dp
        dx_ref[...] = dres_ref[...] + dxv

        @pl.when(i == n_t - 1)
        def _():
            dw_ref[...] = acc[...].astype(BF16)

    row = lambda w: pl.BlockSpec((ts, w), lambda i: (i, 0))
    nxt = pl.BlockSpec((BLK, KV_W), lambda i: (jnp.minimum(i + 1, n_t - 1), 0))
    vec = pl.BlockSpec((1, d), lambda i: (0, 0))
    full = pl.BlockSpec((D_IN_PROJ, d), lambda i: (0, 0))
    return _pcall(
        body, (dres, x, pre_g, hn, w_in_t, dlx, dlg, dq, dk, dk_up, dv, dv_up, dglu), name=name, grid=(n_t,),
        in_specs=[row(d), row(d), vec, row(d), full, row(W_A), row(W_A), row(W_B), row(KV_W), nxt,
                  row(KV_W), nxt, row(2 * W_C)],
        out_specs=[row(d), full, vec],
        out_shape=[SDS((s, d), F32), SDS((D_IN_PROJ, d), BF16), SDS((1, d), F32)],
        scratch_shapes=[pltpu.VMEM((D_IN_PROJ, d), F32)], carry=carry)


def _lru_gates(xc, lru_p):
    cw_ref, cb_ref, wa_ref, ba_ref, wx_ref, bx_ref, lam_ref = lru_p
    c = cb_ref[...]
    for j in range(LRU_K):
        c = c + cw_ref[j:j + 1, :] * _shift_down(xc, LRU_K - 1 - j)[LRU_HALO:, :]
    r = _sigmoid(_mm(c, wa_ref[...]) + ba_ref[...])
    ig = _sigmoid(_mm(c, wx_ref[...]) + bx_ref[...])
    sp = _softplus(-lam_ref[...])
    log_a = -LRU_C * r * sp
    a = jnp.exp(log_a)
    m = jnp.sqrt(_neg_expm1(2.0 * log_a))
    return c, r, ig, sp, a, m


def _lru_pspecs():
    small = lambda r: pl.BlockSpec((r, W_A), lambda i: (0, 0))
    return [small(LRU_K), small(1), small(W_A), small(1), small(W_A), small(1), small(1)]


def _lru_fwd(lx, lg, lru_p, name, carry=()):
    s = lx.shape[0]
    ts = _time_tile(s)
    n8 = ts // LRU_HALO

    def body(lx_ref, lxp_ref, lg_ref, *rest):
        lru_p, (ya_ref, h_ref, hcarry) = rest[:7], rest[7:]
        i = pl.program_id(0)
        prev = jnp.where(i > 0, lxp_ref[...], 0.0)
        xc = jnp.concatenate([prev, lx_ref[...]], axis=0)
        c, r, ig, sp, a, m = _lru_gates(xc, lru_p)
        acc_a, acc_b = a, m * (ig * c)
        t = lax.broadcasted_iota(jnp.int32, a.shape, 0)
        k = 1
        while k < ts:
            keep = t >= k
            acc_b = jnp.where(keep, acc_a * _shift_down(acc_b, k) + acc_b, acc_b)
            acc_a = jnp.where(keep, acc_a * _shift_down(acc_a, k), acc_a)
            k *= 2
        h0 = jnp.where(i > 0, hcarry[...], 0.0)
        h = acc_b + acc_a * h0
        hcarry[...] = h[ts - 1:ts, :]
        h_ref[...] = h
        ya_ref[...] = _gelu(lg_ref[...])[0] * h

    row = pl.BlockSpec((ts, W_A), lambda i: (i, 0))
    prev8 = pl.BlockSpec((LRU_HALO, W_A), lambda i: (jnp.maximum(i * n8 - 1, 0), 0))
    return _pcall(
        body, (lx, lx, lg, *lru_p), name=name, grid=(s // ts,),
        in_specs=[row, prev8, row] + _lru_pspecs(), out_specs=[row, row],
        out_shape=[SDS((s, W_A), F32), SDS((s, W_A), F32)],
        scratch_shapes=[pltpu.VMEM((1, W_A), F32)], carry=carry)


def _lru_bwd(dya, lx, lg, h_s, lru_p, name, carry=()):
    s = lx.shape[0]
    ts = _time_tile(s)
    n_t = s // ts
    n8 = ts // LRU_HALO

    def body(dya_ref, lx_ref, lxp_ref, lg_ref, h_ref, hp_ref, *rest):
        lru_p = rest[:7]
        (dlx_ref, dlg_ref, dcw_ref, dcb_ref, dwa_ref, dba_ref, dwx_ref, dbx_ref, dlam_ref,
         carry_a, carry_l, carry_dc) = rest[7:]
        cw_ref, _, wa_ref, _, wx_ref, _, lam_ref = lru_p
        i = pl.program_id(0)
        first_tile = i == n_t - 1
        last_tile = i == 0

        @pl.when(i == 0)
        def _():
            for ref in (dcw_ref, dcb_ref, dwa_ref, dba_ref, dwx_ref, dbx_ref, dlam_ref):
                ref[...] = jnp.zeros_like(ref)

        prev = jnp.where(first_tile, 0.0, lxp_ref[...])
        xc = jnp.concatenate([prev, lx_ref[...]], axis=0)
        c, r, ig, sp, a, m = _lru_gates(xc, lru_p)
        h = h_ref[...]
        hcat = jnp.concatenate([jnp.where(first_tile, 0.0, hp_ref[...]), h], axis=0)
        h_m1 = _shift_down(hcat, 1)[LRU_HALO:, :]
        lg = lg_ref[...]
        ge, th = _gelu(lg)
        dya = dya_ref[...]
        dlg_ref[...] = dya * h * _dgelu(lg, th)
        dh = dya * ge
        t = lax.broadcasted_iota(jnp.int32, a.shape, 0)
        a_next = jnp.where(t < ts - 1, _shift_up(a, 1), jnp.where(last_tile, 0.0, carry_a[...]))
        acc_a, acc_b = a_next, dh
        k = 1
        while k < ts:
            keep = t < ts - k
            acc_b = jnp.where(keep, acc_a * _shift_up(acc_b, k) + acc_b, acc_b)
            acc_a = jnp.where(keep, acc_a * _shift_up(acc_a, k), acc_a)
            k *= 2
        lam_beyond = jnp.where(last_tile, 0.0, carry_l[...])
        lmb = acc_b + acc_a * lam_beyond
        carry_a[...] = a[0:1, :]
        carry_l[...] = lmb[0:1, :]
        gi = ig * c
        dgi = lmb * m
        dla = lmb * h_m1 * a - (lmb * gi) * (a * a) / m
        dr = dla * (-LRU_C * sp)
        dsp = jnp.sum(dla * (-LRU_C * r), axis=0, keepdims=True)
        dlam_ref[...] += -dsp * _sigmoid(-lam_ref[...])
        dra = dr * r * (1.0 - r)
        dia = dgi * c * ig * (1.0 - ig)
        dc = dgi * ig + _mm_nt(dra, wa_ref[...]) + _mm_nt(dia, wx_ref[...])
        dwa_ref[...] += _mm_tn(c, dra)
        dwx_ref[...] += _mm_tn(c, dia)
        dba_ref[...] += jnp.sum(dra, axis=0, keepdims=True)
        dbx_ref[...] += jnp.sum(dia, axis=0, keepdims=True)
        dcb_ref[...] += jnp.sum(dc, axis=0, keepdims=True)
        dcc = jnp.concatenate([dc, jnp.where(last_tile, 0.0, carry_dc[...])], axis=0)
        carry_dc[...] = dc[0:LRU_HALO, :]
        dlx = jnp.zeros_like(dc)
        for j in range(LRU_K):
            sh = LRU_K - 1 - j
            dcw_ref[j:j + 1, :] += jnp.sum(dc * _shift_down(xc, sh)[LRU_HALO:, :], axis=0, keepdims=True)
            dlx = dlx + cw_ref[j:j + 1, :] * _shift_up(dcc, sh)[:ts, :]
        dlx_ref[...] = dlx

    row = pl.BlockSpec((ts, W_A), lambda i: (n_t - 1 - i, 0))
    prev8 = pl.BlockSpec((LRU_HALO, W_A), lambda i: (jnp.maximum((n_t - 1 - i) * n8 - 1, 0), 0))
    small = lambda r: pl.BlockSpec((r, W_A), lambda i: (0, 0))
    return _pcall(
        body, (dya, lx, lx, lg, h_s, h_s, *lru_p), name=name, grid=(n_t,),
        in_specs=[row, row, prev8, row, row, prev8] + _lru_pspecs(),
        out_specs=[row, row, small(LRU_K), small(1), small(W_A), small(1), small(W_A), small(1), small(1)],
        out_shape=[SDS((s, W_A), F32), SDS((s, W_A), F32), SDS((LRU_K, W_A), F32), SDS((1, W_A), F32),
                   SDS((W_A, W_A), F32), SDS((1, W_A), F32), SDS((W_A, W_A), F32), SDS((1, W_A), F32),
                   SDS((1, W_A), F32)],
        scratch_shapes=[pltpu.VMEM((1, W_A), F32), pltpu.VMEM((1, W_A), F32), pltpu.VMEM((LRU_HALO, W_A), F32)],
        carry=carry)


_ATT_ROWS = N_Q_HEADS * BLK
_GRP_ROWS = Q_PER_KV * BLK


def _attn_stack(ref, rows, g):
    return jnp.concatenate([ref[rows, h * HEAD_DIM:(h + 1) * HEAD_DIM]
                            for h in range(g * Q_PER_KV, (g + 1) * Q_PER_KV)], axis=0)


def _attn_unstack(parts):
    return jnp.concatenate([p[j * BLK:(j + 1) * BLK, :] for p in parts for j in range(Q_PER_KV)], axis=1)


def _grp(x, g):
    return x[:, g * _GRP_ROWS:(g + 1) * _GRP_ROWS]


def _attn_block(q_ref, k_ref, kp_ref, v_ref, vp_ref, sink_row, i, b):
    rows, prev = slice(b * BLK, (b + 1) * BLK), slice((b - 1) * BLK, b * BLK)
    qs, kcs, kps, vcs, vps = [], [], [], [], []
    for g in range(N_KV_HEADS):
        cols = slice(g * HEAD_DIM, (g + 1) * HEAD_DIM)
        qs.append(_attn_stack(q_ref, rows, g))
        kcs.append(k_ref[rows, cols])
        vcs.append(v_ref[rows, cols])
        kps.append(kp_ref[:, cols] if b == 0 else k_ref[prev, cols])
        vps.append(vp_ref[:, cols] if b == 0 else v_ref[prev, cols])
    scale = 1.0 / math.sqrt(HEAD_DIM)
    sc = jnp.concatenate([_mm_nt(kcs[g], qs[g]) for g in range(N_KV_HEADS)], axis=1) * scale
    sp = jnp.concatenate([_mm_nt(kps[g], qs[g]) for g in range(N_KV_HEADS)], axis=1) * scale
    kj = lax.broadcasted_iota(jnp.int32, (BLK, _ATT_ROWS), 0)
    qi = lax.broadcasted_iota(jnp.int32, (BLK, _ATT_ROWS), 1) & (BLK - 1)
    sc = jnp.where(kj <= qi, sc, NEG_BIG)
    sp = jnp.where((kj > qi) if b > 0 else ((kj > qi) & (i > 0)), sp, NEG_BIG)
    m = jnp.maximum(jnp.maximum(jnp.max(sc, axis=0, keepdims=True), jnp.max(sp, axis=0, keepdims=True)), sink_row)
    pc = jnp.exp(sc - m)
    pp = jnp.exp(sp - m)
    es = jnp.exp(sink_row - m)
    inv = 1.0 / (jnp.sum(pc, axis=0, keepdims=True) + jnp.sum(pp, axis=0, keepdims=True) + es)
    return qs, kcs, kps, vcs, vps, pc * inv, pp * inv, es * inv


def _attn_specs(s, ts):
    bpt = ts // BLK
    tile = lambda w: pl.BlockSpec((ts, w), lambda i: (i, 0))
    prv = pl.BlockSpec((BLK, KV_W), lambda i: (jnp.maximum(i * bpt - 1, 0), 0))
    sink = pl.BlockSpec((1, _ATT_ROWS), lambda i: (0, 0))
    return bpt, tile, prv, sink


def _attn_fwd(q, k, v, sink_row, name, carry=()):
    s = q.shape[0]
    ts = _time_tile(s)
    bpt, tile, prv, sink = _attn_specs(s, ts)

    def body(q_ref, k_ref, kp_ref, v_ref, vp_ref, sk_ref, y_ref):
        i = pl.program_id(0)
        for b in range(bpt):
            _, _, _, vcs, vps, pc, pp, _ = _attn_block(q_ref, k_ref, kp_ref, v_ref, vp_ref, sk_ref[...], i, b)
            outs = [_mm_tn(_grp(pc, g), vcs[g]) + _mm_tn(_grp(pp, g), vps[g]) for g in range(N_KV_HEADS)]
            y_ref[b * BLK:(b + 1) * BLK, :] = _attn_unstack(outs)

    return _pcall(
        body, (q, k, k, v, v, sink_row), name=name, grid=(s // ts,),
        in_specs=[tile(W_B), tile(KV_W), prv, tile(KV_W), prv, sink],
        out_specs=[tile(W_B)], out_shape=[SDS((s, W_B), F32)], carry=carry)


def _attn_bwd(dy, q, k, v, sinks, name, carry=()):
    s = q.shape[0]
    ts = _time_tile(s)
    n_t = s // ts
    bpt, tile, prv, sink = _attn_specs(s, ts)

    def body(dy_ref, q_ref, k_ref, kp_ref, v_ref, vp_ref, sk_ref, dq_ref, dk_ref, dv_ref, dku_ref, dvu_ref, dsk_ref):
        i = pl.program_id(0)

        @pl.when(i == 0)
        def _():
            dsk_ref[...] = jnp.zeros_like(dsk_ref)

        scale = 1.0 / math.sqrt(HEAD_DIM)
        groups = range(N_KV_HEADS)
        head_row = lax.broadcasted_iota(jnp.int32, (N_Q_HEADS, BLK), 0)
        dsk = jnp.zeros((N_Q_HEADS, BLK), F32)
        dk_blocks, dv_blocks = [], []
        for b in range(bpt):
            rows = slice(b * BLK, (b + 1) * BLK)
            qs, kcs, kps, vcs, vps, pc, pp, ps = _attn_block(q_ref, k_ref, kp_ref, v_ref, vp_ref, sk_ref[...], i, b)
            dos = [_attn_stack(dy_ref, rows, g) for g in groups]
            dpc = jnp.concatenate([_mm_nt(vcs[g], dos[g]) for g in groups], axis=1)
            dpp = jnp.concatenate([_mm_nt(vps[g], dos[g]) for g in groups], axis=1)
            delta = jnp.sum(pc * dpc, axis=0, keepdims=True) + jnp.sum(pp * dpp, axis=0, keepdims=True)
            dsc = pc * (dpc - delta) * scale
            dsp = pp * (dpp - delta) * scale
            dq_ref[rows, :] = _attn_unstack([_mm_tn(_grp(dsc, g), kcs[g]) + _mm_tn(_grp(dsp, g), kps[g])
                                             for g in groups])
            dk_blocks.append(jnp.concatenate([_mm(_grp(dsc, g), qs[g]) for g in groups], axis=1))
            dv_blocks.append(jnp.concatenate([_mm(_grp(pc, g), dos[g]) for g in groups], axis=1))
            dkp = jnp.concatenate([_mm(_grp(dsp, g), qs[g]) for g in groups], axis=1)
            dvp = jnp.concatenate([_mm(_grp(pp, g), dos[g]) for g in groups], axis=1)
            if b == 0:
                dku_ref[...] = dkp
                dvu_ref[...] = dvp
            else:
                dk_blocks[b - 1] = dk_blocks[b - 1] + dkp
                dv_blocks[b - 1] = dv_blocks[b - 1] + dvp
            dsink = -ps * delta
            for h in range(N_Q_HEADS):
                dsk = dsk + jnp.where(head_row == h, jnp.sum(dsink[:, h * BLK:(h + 1) * BLK], axis=1, keepdims=True), 0.0)
        for b in range(bpt):
            dk_ref[b * BLK:(b + 1) * BLK, :] = dk_blocks[b]
            dv_ref[b * BLK:(b + 1) * BLK, :] = dv_blocks[b]
        dsk_ref[...] += dsk

    up = pl.BlockSpec((BLK, KV_W), lambda i: (i, 0))
    return _pcall(
        body, (dy, q, k, k, v, v, sinks), name=name, grid=(n_t,),
        in_specs=[tile(W_B), tile(W_B), tile(KV_W), prv, tile(KV_W), prv, sink],
        out_specs=[tile(W_B), tile(KV_W), tile(KV_W), up, up, pl.BlockSpec((N_Q_HEADS, BLK), lambda i: (0, 0))],
        out_shape=[SDS((s, W_B), F32), SDS((s, KV_W), F32), SDS((s, KV_W), F32), SDS((n_t * BLK, KV_W), F32),
                   SDS((n_t * BLK, KV_W), F32), SDS((N_Q_HEADS, BLK), F32)], carry=carry)


def _cc_recompute(glu_ref, glup_ref, cw_ref, cb_ref, first_tile):
    prev = jnp.where(first_tile, 0.0, glup_ref[...])
    ge = jnp.concatenate([prev, glu_ref[...]], axis=0)
    y0 = ge[:, :W_C] * _sigmoid_t(ge[:, W_C:])
    y1 = cb_ref[...]
    for j in range(CC_K):
        y1 = y1 + cw_ref[j:j + 1, :] * _shift_down(y0, CC_K - 1 - j)[CC_HALO:, :]
    return y0, y1


def _ln_stats(y1):
    mu = jnp.mean(y1, axis=-1, keepdims=True)
    xc = y1 - mu
    rstd = lax.rsqrt(jnp.mean(xc * xc, axis=-1, keepdims=True) + LN_EPS)
    return xc * rstd, rstd


def _cc_specs(s, ts):
    n32 = ts // CC_HALO
    row = lambda w: pl.BlockSpec((ts, w), lambda i: (i, 0))
    prev = pl.BlockSpec((CC_HALO, 2 * W_C), lambda i: (jnp.maximum(i * n32 - 1, 0), 0))
    small = lambda r: pl.BlockSpec((r, W_C), lambda i: (0, 0))
    return row, prev, small


def _cc_fwd(glu, cw, cb, lng, lnb, name, carry=()):
    s = glu.shape[0]
    ts = _time_tile(s)
    row, prev, small = _cc_specs(s, ts)

    def body(glu_ref, glup_ref, cw_ref, cb_ref, lng_ref, lnb_ref, y_ref):
        _, y1 = _cc_recompute(glu_ref, glup_ref, cw_ref, cb_ref, pl.program_id(0) == 0)
        xhat, _ = _ln_stats(y1)
        z = xhat * lng_ref[...] + lnb_ref[...]
        y_ref[...] = z * _sigmoid_t(z)

    return _pcall(
        body, (glu, glu, cw, cb, lng, lnb), name=name, grid=(s // ts,),
        in_specs=[row(2 * W_C), prev, small(CC_HALO), small(1), small(1), small(1)],
        out_specs=[row(W_C)], out_shape=[SDS((s, W_C), F32)], carry=carry)


def _cc_bwd_conv(dy, glu, cw, cb, lng, lnb, name, carry=()):
    s = glu.shape[0]
    ts = _time_tile(s)
    row, prev, small = _cc_specs(s, ts)

    def body(dy_ref, glu_ref, glup_ref, cw_ref, cb_ref, lng_ref, lnb_ref, dy1_ref, dcw_ref, dcb_ref, dlng_ref, dlnb_ref):
        i = pl.program_id(0)

        @pl.when(i == 0)
        def _():
            for ref in (dcw_ref, dcb_ref, dlng_ref, dlnb_ref):
                ref[...] = jnp.zeros_like(ref)

        y0, y1 = _cc_recompute(glu_ref, glup_ref, cw_ref, cb_ref, i == 0)
        xhat, rstd = _ln_stats(y1)
        z = xhat * lng_ref[...] + lnb_ref[...]
        dz = dy_ref[...] * _dsilu(z, _sigmoid_t(z))
        dlng_ref[...] += jnp.sum(dz * xhat, axis=0, keepdims=True)
        dlnb_ref[...] += jnp.sum(dz, axis=0, keepdims=True)
        dxh = dz * lng_ref[...]
        dy1 = rstd * (dxh - jnp.mean(dxh, axis=-1, keepdims=True) - xhat * jnp.mean(dxh * xhat, axis=-1, keepdims=True))
        dy1_ref[...] = dy1
        dcb_ref[...] += jnp.sum(dy1, axis=0, keepdims=True)
        for j in range(CC_K):
            dcw_ref[j:j + 1, :] += jnp.sum(dy1 * _shift_down(y0, CC_K - 1 - j)[CC_HALO:, :], axis=0, keepdims=True)

    return _pcall(
        body, (dy, glu, glu, cw, cb, lng, lnb), name=name, grid=(s // ts,),
        in_specs=[row(W_C), row(2 * W_C), prev, small(CC_HALO), small(1), small(1), small(1)],
        out_specs=[row(W_C), small(CC_HALO), small(1), small(1), small(1)],
        out_shape=[SDS((s, W_C), F32), SDS((CC_HALO, W_C), F32)] + [SDS((1, W_C), F32)] * 3, carry=carry)


def _cc_bwd_glu(dy1, glu, cw, name, carry=()):
    s = glu.shape[0]
    ts = _time_tile(s)
    n_t = s // ts
    n32 = ts // CC_HALO

    def body(dy1_ref, dyn_ref, glu_ref, cw_ref, dglu_ref):
        i = pl.program_id(0)
        dcat = jnp.concatenate([dy1_ref[...], jnp.where(i < n_t - 1, dyn_ref[...], 0.0)], axis=0)
        dy0 = jnp.zeros((ts, W_C), F32)
        for j in range(CC_K):
            dy0 = dy0 + cw_ref[j:j + 1, :] * _shift_up(dcat, CC_K - 1 - j)[:ts, :]
        a = glu_ref[:, :W_C]
        sg = _sigmoid_t(glu_ref[:, W_C:])
        dglu_ref[...] = jnp.concatenate([dy0 * sg, dy0 * a * sg * (1.0 - sg)], axis=1)

    row = lambda w: pl.BlockSpec((ts, w), lambda i: (i, 0))
    nxt = pl.BlockSpec((CC_HALO, W_C), lambda i: (jnp.minimum((i + 1) * n32, s // CC_HALO - 1), 0))
    return _pcall(
        body, (dy1, dy1, glu, cw), name=name, grid=(n_t,),
        in_specs=[row(W_C), nxt, row(2 * W_C), pl.BlockSpec((CC_HALO, W_C), lambda i: (0, 0))],
        out_specs=[row(2 * W_C)], out_shape=[SDS((s, 2 * W_C), F32)], carry=carry)


_MIX_OFFS = ((0, W_A), (W_A, W_A + W_B), (W_A + W_B, W_A + W_B + W_C))


def _mix_out_fwd(x, ya, yb, yc, group_g, w_out, post_g, name, carry=()):
    s, d = x.shape
    ts = 2 * _time_tile(s)
    dm = w_out.shape[0]

    def body(x_ref, ya_ref, yb_ref, yc_ref, gg_ref, w_ref, qg_ref, xo_ref, o_ref):
        parts = []
        for y_ref, (lo, hi) in zip((ya_ref, yb_ref, yc_ref), _MIX_OFFS):
            yv = y_ref[...]
            parts.append(yv * _rms_r(yv) * gg_ref[:, lo:hi])
        o = _mm(jnp.concatenate(parts, axis=1), w_ref[...])
        o_ref[...] = o
        xo_ref[...] = x_ref[...] + o * _rms_r(o) * qg_ref[...]

    row = lambda w: pl.BlockSpec((ts, w), lambda i: (i, 0))
    return _pcall(
        body, (x, ya, yb, yc, group_g, w_out, post_g), name=name, grid=(s // ts,),
        in_specs=[row(d), row(W_A), row(W_B), row(W_C), pl.BlockSpec((1, dm), lambda i: (0, 0)),
                  pl.BlockSpec((dm, d), lambda i: (0, 0)), pl.BlockSpec((1, d), lambda i: (0, 0))],
        out_specs=[row(d), row(d)], out_shape=[SDS((s, d), F32), SDS((s, d), F32)], carry=carry)


def _mix_out_bwd(dxo, o, ya, yb, yc, group_g, w_out, post_g, name, carry=()):
    s, d = o.shape
    ts = _time_tile(s)
    n_t = s // ts
    dm = w_out.shape[0]

    def body(dxo_ref, o_ref, ya_ref, yb_ref, yc_ref, gg_ref, w_ref, qg_ref,
             dya_ref, dyb_ref, dyc_ref, dw_ref, dqg_ref, dgg_ref, acc):
        i = pl.program_id(0)

        @pl.when(i == 0)
        def _():
            acc[...] = jnp.zeros_like(acc)
            dqg_ref[...] = jnp.zeros_like(dqg_ref)
            dgg_ref[...] = jnp.zeros_like(dgg_ref)

        ov = o_ref[...]
        do, dq = _rms_bwd(ov, _rms_r(ov), qg_ref[...], dxo_ref[...])
        dqg_ref[...] += dq
        do = do.astype(BF16)
        dyn = _mm_nt(do, w_ref[...])
        parts, dggs = [], []
        for y_ref, dy_ref, (lo, hi) in zip((ya_ref, yb_ref, yc_ref), (dya_ref, dyb_ref, dyc_ref), _MIX_OFFS):
            yv = y_ref[...]
            r = _rms_r(yv)
            gg = gg_ref[:, lo:hi]
            parts.append(yv * r * gg)
            dyv, dg = _rms_bwd(yv, r, gg, dyn[:, lo:hi])
            dy_ref[...] = dyv
            dggs.append(dg)
        dgg_ref[...] += jnp.concatenate(dggs, axis=1)
        acc[...] += _mm_tn(jnp.concatenate(parts, axis=1), do)

        @pl.when(i == n_t - 1)
        def _():
            dw_ref[...] = acc[...].astype(BF16)

    row = lambda w: pl.BlockSpec((ts, w), lambda i: (i, 0))
    full = pl.BlockSpec((dm, d), lambda i: (0, 0))
    return _pcall(
        body, (dxo, o, ya, yb, yc, group_g, w_out, post_g), name=name, grid=(n_t,),
        in_specs=[row(d), row(d), row(W_A), row(W_B), row(W_C), pl.BlockSpec((1, dm), lambda i: (0, 0)), full,
                  pl.BlockSpec((1, d), lambda i: (0, 0))],
        out_specs=[row(W_A), row(W_B), row(W_C), full, pl.BlockSpec((1, d), lambda i: (0, 0)),
                   pl.BlockSpec((1, dm), lambda i: (0, 0))],
        out_shape=[SDS((s, W_A), F32), SDS((s, W_B), F32), SDS((s, W_C), F32), SDS((dm, d), BF16),
                   SDS((1, d), F32), SDS((1, dm), F32)],
        scratch_shapes=[pltpu.VMEM((dm, d), F32)], carry=carry)


def _adamw_math(w, g, m, v):
    m = ADAM_B1 * m + (1.0 - ADAM_B1) * g
    v = ADAM_B2 * v + (1.0 - ADAM_B2) * (g * g)
    m_hat = m / (1.0 - ADAM_B1 ** ADAM_STEP)
    v_hat = v / (1.0 - ADAM_B2 ** ADAM_STEP)
    delta = -ADAM_LR * (m_hat / (jnp.sqrt(v_hat) + ADAM_EPS) + ADAM_WD * w)
    return delta, m, v


def _row_tile(rows, cap=512):
    best = None
    for t in range(16, min(rows, cap) + 1, 16):
        if rows % t == 0:
            best = t
    return best if best is not None else rows


def _reduce_adamw(recv, w, m, v, name):
    n_l, r, c = w.shape
    tr = _row_tile(r)

    def body(recv_ref, w_ref, m_ref, v_ref, g_ref, d_ref, nm_ref, nv_ref):
        g = recv_ref[0].astype(F32)
        for p in range(1, N_DEV):
            g = g + recv_ref[p].astype(F32)
        g_ref[...] = g
        d_ref[...], nm_ref[...], nv_ref[...] = _adamw_math(w_ref[...], g, m_ref[...], v_ref[...])

    blk = pl.BlockSpec((None, tr, c), lambda l, i: (l, i, 0))
    return _pcall(
        body, (recv, w, m, v), name=name, grid=(n_l, r // tr),
        in_specs=[pl.BlockSpec((N_DEV, None, tr, c), lambda l, i: (0, l, i, 0)), blk, blk, blk],
        out_specs=[blk] * 4, out_shape=[SDS(w.shape, F32)] * 4)[0]


def _reduce_adamw_small(parts, w, m, v, name):
    def body(p_ref, w_ref, m_ref, v_ref, g_ref, d_ref, nm_ref, nv_ref):
        g = p_ref[0]
        for p in range(1, N_DEV):
            g = g + p_ref[p]
        g_ref[...] = g
        d_ref[...], nm_ref[...], nv_ref[...] = _adamw_math(w_ref[...], g, m_ref[...], v_ref[...])

    vm = pl.BlockSpec(memory_space=pltpu.VMEM)
    return pl.pallas_call(body, name=name, in_specs=[vm] * 4, out_specs=[vm] * 4, out_shape=[SDS(w.shape, F32)] * 4,
                          compiler_params=pltpu.CompilerParams(vmem_limit_bytes=VMEM_LIMIT))(parts, w, m, v)


def _rows_of(shape):
    return -(-math.prod(shape) // (8 * BLK)) * 8


def _pack(arrs):
    rows = []
    for a in arrs:
        n, r = math.prod(a.shape), _rows_of(a.shape)
        if n % BLK == 0:
            part = a.reshape(n // BLK, BLK)
            rows.append(part if n // BLK == r else jnp.pad(part, ((0, r - n // BLK), (0, 0))))
        else:
            rows.append(jnp.pad(a.reshape(-1), (0, r * BLK - n)).reshape(r, BLK))
    return jnp.concatenate(rows, axis=0)


def _unpack(packed, shapes):
    out, row = [], 0
    for shp in shapes:
        n, r = math.prod(shp), _rows_of(shp)
        if n % BLK == 0:
            out.append(packed[row:row + n // BLK].reshape(shp))
        else:
            out.append(packed[row:row + r].reshape(-1)[:n].reshape(shp))
        row += r
    return out


def _block_diag(w):
    nb, bw, _ = w.shape
    eye = jnp.eye(nb, dtype=w.dtype)
    return (eye[:, None, :, None] * w[:, :, None, :]).reshape(nb * bw, nb * bw)


def _diag_blocks(wd, nb):
    bw = wd.shape[0] // nb
    return jnp.stack([wd[b * bw:(b + 1) * bw, b * bw:(b + 1) * bw] for b in range(nb)])


WEIGHT_NAMES = ['ffn1_pre_g', 'ffn1_w_gu', 'ffn1_w_down', 'ffn1_post_g', 'mix_pre_g', 'w_in', 'lru_conv_w', 'lru_conv_b',
                'lru_w_a', 'lru_b_a', 'lru_w_x', 'lru_b_x', 'lru_lambda', 'attn_sinks', 'conv_w', 'conv_b', 'conv_ln_g',
                'conv_ln_b', 'group_g', 'w_out', 'mix_post_g', 'ffn2_pre_g', 'ffn2_w_gu', 'ffn2_w_down', 'ffn2_post_g']
BIG = ('ffn1_w_gu', 'ffn1_w_down', 'w_in', 'w_out', 'ffn2_w_gu', 'ffn2_w_down')
TRANSPOSED = ('ffn1_w_gu', 'ffn2_w_gu', 'w_in')
SMALL = tuple(k for k in WEIGHT_NAMES if k not in BIG)
CHANNEL_SHARDED = ('lru_conv_w', 'conv_w')


def _step(x, target, w, m, v):
    n_l = w['ffn1_pre_g'].shape[0]
    assert n_l == 2, "the exchange schedule below is laid out for two layers"
    s, d = x.shape[1], x.shape[2]
    x = x.reshape(s, d)
    target = target.reshape(s, d)
    me = _my_pos()[3]
    tview = lambda t, k: jnp.swapaxes(t[k], 1, 2) if k in TRANSPOSED else t[k]
    wb = {k: tview(w, k).astype(BF16) for k in BIG}
    vec = lambda name, l: w[name][l][None, :]

    conv_shard = _pack([w['lru_conv_w'], w['conv_w']])
    g0 = _all_gather([(wb['ffn1_w_gu'], 0), (wb['ffn1_w_down'], 0), (wb['w_in'], 0), (wb['w_out'], 0),
                      (conv_shard, None)], "all_gather_first")
    wts = [dict(), dict()]
    wts[0]['ffn1_w_gu'], wts[0]['ffn1_w_down'], wts[0]['w_in'], wts[0]['w_out'], conv_g = g0
    ch = W_A // N_DEV
    conv_parts = [_unpack(conv_g[p], [(n_l, LRU_K, ch), (n_l, CC_K, ch)]) for p in range(N_DEV)]
    lru_cw = jnp.concatenate([cp[0] for cp in conv_parts], axis=-1)
    cc_cw = jnp.concatenate([cp[1] for cp in conv_parts], axis=-1)
    cc_cw = jnp.pad(cc_cw, ((0, 0), (0, CC_HALO - CC_K), (0, 0)))

    fc = wb['ffn1_w_gu'].shape[1]
    cut1, cut2 = (fc * 4 // 11 + 15) // 16 * 16, (fc * 27 // 44 + 15) // 16 * 16
    gather_plan = {
        ('ffn1', 0): [('A', 'f2_0', ('ffn2_w_gu', 'ffn2_w_down'), 0)],
        ('mix_in', 0): [('B', 'f2_0'), ('A', 'g1_1a', ('ffn1_w_gu',), 1, (0, cut1))],
        ('lru', 0): [('A', 'g1_1b', ('ffn1_w_gu',), 1, (cut1, cut2 - cut1), 'g1_1a')],
        ('attn', 0): [('A', 'g1_1', ('ffn1_w_gu',), 1, (cut2, fc - cut2), 'g1_1b')],
        ('cconv', 0): [('B', 'g1_1')],
        ('ffn2', 0): [('D', None, ('ffn1_w_down',), 1), ('A', 'wi_1', ('w_in',), 1), ('A', 'wo_1', ('w_out',), 1)],
        ('ffn1', 1): [('A', 'f2_1', ('ffn2_w_gu', 'ffn2_w_down'), 1), ('B', 'wi_1'), ('B', 'wo_1')],
        ('mix_in', 1): [('B', 'f2_1')],
    }
    pend = {}

    def fwd(kernel_name, l, fn, *args):
        plan = gather_plan.get((kernel_name, l), [])
        carry = []
        for st in plan:
            if st[0] == 'B':
                carry.append(_gather_b(pend[st[1]][2]))
            else:
                rows = st[4] if len(st) > 4 else None
                into = pend.pop(st[5])[2] if len(st) > 5 else [None] * len(st[2])
                carry.append(_gather_a([(wb[k], st[3], rows, buf) for k, buf in zip(st[2], into)],
                                       two_level=st[0] == 'A'))
        outs, ex = fn(*args, f"{kernel_name}_fwd_l{l}", carry)
        for st, bufs in zip(plan, ex):
            if st[0] == 'A':
                pend[st[1]] = (st[2], st[3], bufs)
            else:
                names, wl = (st[2], st[3]) if st[0] == 'D' else pend.pop(st[1])[:2]
                for k, b in zip(names, bufs):
                    wts[wl][k] = b
        return outs

    saved = []
    h = x
    for l in range(n_l):
        sv = {'x0': h}
        lw = wts[l]
        x1, sv['h1'], sv['g1'], sv['u1'], sv['d1'] = fwd(
            'ffn1', l, _ffn_fwd, h, vec('ffn1_pre_g', l), vec('ffn1_post_g', l), lw['ffn1_w_gu'], lw['ffn1_w_down'])
        sv['x1'] = x1
        sv['hn'], lx, lg, q, k, vv, glu = fwd('mix_in', l, _mix_in_fwd, x1, vec('mix_pre_g', l),
                                              lw['w_in'].reshape(D_IN_PROJ, d))
        sv.update(lx=lx, lg=lg, q=q, k=k, v=vv, glu=glu)
        lru_p = (lru_cw[l], vec('lru_conv_b', l), _block_diag(w['lru_w_a'][l]).astype(BF16), vec('lru_b_a', l),
                 _block_diag(w['lru_w_x'][l]).astype(BF16), vec('lru_b_x', l), vec('lru_lambda', l))
        cc_p = (cc_cw[l], vec('conv_b', l), vec('conv_ln_g', l), vec('conv_ln_b', l))
        sv.update(lru_p=lru_p, cc_p=cc_p)
        sv['ya'], sv['hs'] = fwd('lru', l, _lru_fwd, lx, lg, lru_p)
        sv['sink_row'] = jnp.repeat(w['attn_sinks'][l], BLK)[None, :]
        (sv['yb'],) = fwd('attn', l, _attn_fwd, q, k, vv, sv['sink_row'])
        (sv['yc'],) = fwd('cconv', l, _cc_fwd, glu, *cc_p)
        x2, sv['o'] = fwd('mix_out', l, _mix_out_fwd, x1, sv['ya'], sv['yb'], sv['yc'], vec('group_g', l),
                          lw['w_out'].reshape(-1, d), vec('mix_post_g', l))
        sv['x2'] = x2
        h, sv['h2'], sv['g2'], sv['u2'], sv['d2'] = fwd(
            'ffn2', l, _ffn_fwd, x2, vec('ffn2_pre_g', l), vec('ffn2_post_g', l), lw['ffn2_w_gu'], lw['ffn2_w_down'])
        saved.append(sv)

    dh = h

    recv = {k: None for k in BIG}
    ready = {}
    small = [dict() for _ in range(n_l)]

    c_even, c_odd = tuple(range(0, N_DEV, 2)), tuple(range(1, N_DEV, 2))

    def exchange(keys):
        return _grad_x([(ready[key[:2]], key[1], recv[key[0]]) + tuple(key[2:]) for key in keys], n_l)

    def received(keys, bufs):
        for key, b in zip(keys, bufs):
            recv[key[0]] = b

    def run(fn, *args, keys=(), **kw):
        outs, ex = fn(*args, carry=[exchange(keys)] if keys else [], **kw)
        if keys:
            received(keys, ex[0])
        return outs

    for l in reversed(range(n_l)):
        sv, sg, lw = saved[l], small[l], wts[l]
        keys = [] if l == n_l - 1 else [('ffn1_w_gu', l + 1)]
        dx2, dd, dg, du, sg['ffn2_pre_g'], sg['ffn2_post_g'], *loss_rows = run(
            _ffn_bwd_act, dh, sv['d2'], sv['x2'], vec('ffn2_pre_g', l), vec('ffn2_post_g', l), sv['g2'], sv['u2'],
            lw['ffn2_w_gu'], lw['ffn2_w_down'], f"ffn2_bwd_act_l{l}", keys=keys,
            loss_target=target if l == n_l - 1 else None)
        if loss_rows:
            loss_row = loss_rows[0]
        keys = [] if l == n_l - 1 else [('ffn1_w_down', l + 1), ('w_in', l + 1, c_odd)]
        dwg, dwu, dwd = run(_ffn_bwd_w, sv['h2'], dd, sv['g2'], sv['u2'], dg, du, f"ffn2_bwd_w_l{l}", keys=keys)
        ready[('ffn2_w_gu', l)] = [dwg, dwu]
        ready[('ffn2_w_down', l)] = [dwd.reshape(N_DEV, -1, d)]
        dya, dyb, dyc, dw_out, sg['mix_post_g'], sg['group_g'] = run(
            _mix_out_bwd, dx2, sv['o'], sv['ya'], sv['yb'], sv['yc'], vec('group_g', l), lw['w_out'].reshape(-1, d),
            vec('mix_post_g', l), f"mix_out_bwd_l{l}")
        ready[('w_out', l)] = [dw_out.reshape(N_DEV, -1, d)]
        (dlx, dlg, sg['lru_conv_w'], sg['lru_conv_b'], dwa, sg['lru_b_a'], dwx, sg['lru_b_x'],
         sg['lru_lambda']) = run(_lru_bwd, dya, sv['lx'], sv['lg'], sv['hs'], sv['lru_p'], f"lru_bwd_l{l}")
        sg['lru_w_a'] = _diag_blocks(dwa, A_BLOCKS)
        sg['lru_w_x'] = _diag_blocks(dwx, A_BLOCKS)
        dq, dk, dv, dk_up, dv_up, dsk = run(_attn_bwd, dyb, sv['q'], sv['k'], sv['v'], sv['sink_row'],
                                            f"attn_bwd_l{l}", keys=[('ffn2_w_down', l, c_even)] if l == 0 else [])
        sg['attn_sinks'] = dsk[:, 0]
        dy1, dcw, sg['conv_b'], sg['conv_ln_g'], sg['conv_ln_b'] = run(
            _cc_bwd_conv, dyc, sv['glu'], *sv['cc_p'], f"cconv_bwd_conv_l{l}", keys=[('w_out', l)] if l == 0 else [])
        sg['conv_w'] = dcw[:CC_K]
        (dglu,) = run(_cc_bwd_glu, dy1, sv['glu'], sv['cc_p'][0], f"cconv_bwd_glu_l{l}")
        dx1, dw_in, sg['mix_pre_g'] = run(
            _mix_in_bwd, dx2, sv['x1'], vec('mix_pre_g', l), sv['hn'], lw['w_in'].reshape(D_IN_PROJ, d),
            dlx, dlg, dq, dk, dk_up, dv, dv_up, dglu, f"mix_in_bwd_l{l}",
            keys=[('ffn2_w_down', l, c_odd)] if l == 0 else [('w_out', l)])
        ready[('w_in', l)] = [dw_in.reshape(N_DEV, -1, d)]
        dh, dd, dg, du, sg['ffn1_pre_g'], sg['ffn1_post_g'] = run(
            _ffn_bwd_act, dx1, sv['d1'], sv['x0'], vec('ffn1_pre_g', l), vec('ffn1_post_g', l), sv['g1'], sv['u1'],
            lw['ffn1_w_gu'], lw['ffn1_w_down'], f"ffn1_bwd_act_l{l}",
            keys=[('ffn2_w_gu', l), ('w_in', l)] if l == 0 else [('ffn2_w_gu', l)])
        if l > 0:
            dwg, dwu, dwd = run(_ffn_bwd_w, sv['h1'], dd, sv['g1'], sv['u1'], dg, du, f"ffn1_bwd_w_l{l}",
                                keys=[('ffn2_w_down', l), ('w_in', l, c_even)])
            ready[('ffn1_w_gu', l)] = [dwg, dwu]
            ready[('ffn1_w_down', l)] = [dwd.reshape(N_DEV, -1, d)]
        else:
            part = _pack([jnp.stack([small[j][k] for j in range(n_l)]) for k in SMALL] + [loss_row])
            (recv['ffn1_w_gu'], recv['ffn1_w_down']), ex = _ffn_bwd_w_send(
                sv['h1'], dd, sv['g1'], sv['u1'], dg, du, recv['ffn1_w_gu'], recv['ffn1_w_down'], 0, "ffn1_bwd_w_send_l0",
                [_gather_a([(part, None)], two_level=False)])
            small_parts = ex[0][0]
    grad_x = dh.reshape(1, s, d)

    out = {}
    for k in BIG:
        res = _reduce_adamw(recv[k], tview(w, k), tview(m, k), tview(v, k), f"reduce_adamw_{k}")
        out[k] = [jnp.swapaxes(r, 1, 2) for r in res] if k in TRANSPOSED else res

    small_shapes = [(n_l,) + tuple(small[0][k].shape) for k in SMALL]

    def widen(t, k):
        if k not in CHANNEL_SHARDED:
            return t.reshape((n_l,) + tuple(small[0][k].shape))
        full = jnp.zeros((n_l,) + tuple(small[0][k].shape), F32)
        return lax.dynamic_update_slice_in_dim(full, t, me * ch, axis=2)

    no_w = jnp.zeros(loss_row.shape, F32)
    packed = [_pack([widen(src[k], k) for k in SMALL] + [no_w]) for src in (w, m, v)]
    res = _reduce_adamw_small(small_parts, *packed, "reduce_adamw_small")
    loss = _unpack(res[0], small_shapes + [loss_row.shape])[-1][0, 0]
    for k, g, dlt, nm, nv in zip(SMALL, *[_unpack(r, small_shapes) for r in res]):
        vals = [g, dlt, nm, nv]
        if k in CHANNEL_SHARDED:
            vals = [lax.dynamic_slice_in_dim(t, me * ch, ch, axis=2) for t in vals]
        out[k] = [t.reshape(w[k].shape) for t in vals]

    return (loss, grad_x, *[out[k][0] for k in WEIGHT_NAMES], *[out[k][1] for k in WEIGHT_NAMES],
            *[out[k][2] for k in WEIGHT_NAMES], *[out[k][3] for k in WEIGHT_NAMES])


def kernel(x, ffn1_pre_g, ffn1_w_gu, ffn1_w_down, ffn1_post_g, mix_pre_g, w_in, lru_conv_w, lru_conv_b, lru_w_a, lru_b_a, lru_w_x, lru_b_x, lru_lambda, attn_sinks, conv_w, conv_b, conv_ln_g, conv_ln_b, group_g, w_out, mix_post_g, ffn2_pre_g, ffn2_w_gu, ffn2_w_down, ffn2_post_g, loss_target, m_ffn1_pre_g, m_ffn1_w_gu, m_ffn1_w_down, m_ffn1_post_g, m_mix_pre_g, m_w_in, m_lru_conv_w, m_lru_conv_b, m_lru_w_a, m_lru_b_a, m_lru_w_x, m_lru_b_x, m_lru_lambda, m_attn_sinks, m_conv_w, m_conv_b, m_conv_ln_g, m_conv_ln_b, m_group_g, m_w_out, m_mix_post_g, m_ffn2_pre_g, m_ffn2_w_gu, m_ffn2_w_down, m_ffn2_post_g, v_ffn1_pre_g, v_ffn1_w_gu, v_ffn1_w_down, v_ffn1_post_g, v_mix_pre_g, v_w_in, v_lru_conv_w, v_lru_conv_b, v_lru_w_a, v_lru_b_a, v_lru_w_x, v_lru_b_x, v_lru_lambda, v_attn_sinks, v_conv_w, v_conv_b, v_conv_ln_g, v_conv_ln_b, v_group_g, v_w_out, v_mix_post_g, v_ffn2_pre_g, v_ffn2_w_gu, v_ffn2_w_down, v_ffn2_post_g):
    args = (ffn1_pre_g, ffn1_w_gu, ffn1_w_down, ffn1_post_g, mix_pre_g, w_in, lru_conv_w, lru_conv_b, lru_w_a, lru_b_a, lru_w_x, lru_b_x, lru_lambda, attn_sinks, conv_w, conv_b, conv_ln_g, conv_ln_b, group_g, w_out, mix_post_g, ffn2_pre_g, ffn2_w_gu, ffn2_w_down, ffn2_post_g)
    ms = (m_ffn1_pre_g, m_ffn1_w_gu, m_ffn1_w_down, m_ffn1_post_g, m_mix_pre_g, m_w_in, m_lru_conv_w, m_lru_conv_b, m_lru_w_a, m_lru_b_a, m_lru_w_x, m_lru_b_x, m_lru_lambda, m_attn_sinks, m_conv_w, m_conv_b, m_conv_ln_g, m_conv_ln_b, m_group_g, m_w_out, m_mix_post_g, m_ffn2_pre_g, m_ffn2_w_gu, m_ffn2_w_down, m_ffn2_post_g)
    vs = (v_ffn1_pre_g, v_ffn1_w_gu, v_ffn1_w_down, v_ffn1_post_g, v_mix_pre_g, v_w_in, v_lru_conv_w, v_lru_conv_b, v_lru_w_a, v_lru_b_a, v_lru_w_x, v_lru_b_x, v_lru_lambda, v_attn_sinks, v_conv_w, v_conv_b, v_conv_ln_g, v_conv_ln_b, v_group_g, v_w_out, v_mix_post_g, v_ffn2_pre_g, v_ffn2_w_gu, v_ffn2_w_down, v_ffn2_post_g)
    return _step(x, loss_target, dict(zip(WEIGHT_NAMES, args)), dict(zip(WEIGHT_NAMES, ms)), dict(zip(WEIGHT_NAMES, vs)))
```

```python
import functools
import math
import operator

import jax
import jax.numpy as jnp
from jax import lax
from jax.experimental import pallas as pl
from jax.experimental.pallas import tpu as pltpu

F32 = jnp.float32
BF16 = jnp.bfloat16
N_DEV = 8
AXES = ("x", "y", "c")
MESH = pl.DeviceIdType.MESH

NORM_EPS = 1e-6
LN_EPS = 1e-5
NEG_BIG = -1e30
W_A = 256
W_B = 512
W_C = 256
HEAD_DIM = 64
N_Q_HEADS = 8
N_KV_HEADS = 2
Q_PER_KV = N_Q_HEADS // N_KV_HEADS
KV_W = N_KV_HEADS * HEAD_DIM
BLK = 128
LRU_K = 4
LRU_C = 8.0
A_BLOCKS = 4
CC_K = 31
CC_HALO = 32
LRU_HALO = 8
D_IN_PROJ = 2 * W_A + W_B + 2 * KV_W + 2 * W_C
ADAM_LR = 0.001
ADAM_B1 = 0.9
ADAM_B2 = 0.999
ADAM_EPS = 1e-08
ADAM_WD = 0.01
ADAM_STEP = 10
VMEM_LIMIT = 56 * 1024 * 1024

SDS = jax.ShapeDtypeStruct
ANY = pl.BlockSpec(memory_space=pl.ANY)


def _time_tile(s):
    return max(BLK, s // 8)


class _Exchange:
    def __init__(self, inputs, out_shapes, aliases, sem_shapes, start, wait):
        self.inputs, self.out_shapes, self.aliases, self.sem_shapes = inputs, out_shapes, aliases, sem_shapes
        self.start, self.wait = start, wait


def _my_pos():
    x, y, c = (lax.axis_index(a) for a in AXES)
    return x, y, c, 4 * x + 2 * y + c


def _flip(k):
    x, y, c, _ = _my_pos()
    return (1 - x if k & 4 else x, 1 - y if k & 2 else y, 1 - c if k & 1 else c)


def _slot(dev):
    return 4 * dev[0] + 2 * dev[1] + dev[2]


def _dev(p):
    return (p >> 2, (p >> 1) & 1, p & 1)


def _gather_a(items, two_level):
    rels = (1, 2, 4, 6) if two_level else tuple(range(1, N_DEV))
    items = [tuple(it) + (None,) * (4 - len(it)) for it in items]
    n = len(items)
    with_buf = [a for a in range(n) if items[a][3] is not None]

    def rows_of(ref, a):
        return ref if items[a][2] is None else ref.at[pl.ds(*items[a][2])]

    def src_of(ins, a):
        return rows_of(ins[a] if items[a][1] is None else ins[a].at[items[a][1]], a)

    def dst_of(outs, a, slot):
        return rows_of(outs[a].at[slot], a)

    def shape_of(a):
        arr, l = items[a][:2]
        return arr.shape if l is None else arr.shape[1:]

    def copies(ins, outs, sems, a):
        send, recv, _ = sems
        me = _my_pos()[3]
        return [(k, pltpu.make_async_remote_copy(
            src_ref=src_of(ins, a), dst_ref=dst_of(outs, a, me), send_sem=send.at[a, k], recv_sem=recv.at[a, k],
            device_id=_flip(k), device_id_type=MESH)) for k in rels]

    def local(ins, outs, sems, a):
        return pltpu.make_async_copy(src_of(ins, a), dst_of(outs, a, _my_pos()[3]), sems[2].at[a])

    def start(ins, outs, sems):
        for a in range(n):
            local(ins, outs, sems, a).start()
            for _, cp in copies(ins, outs, sems, a):
                cp.start()

    def wait(ins, outs, sems):
        send, recv, _ = sems
        for a in range(n):
            for k, cp in copies(ins, outs, sems, a):
                pltpu.make_async_remote_copy(
                    src_ref=src_of(ins, a), dst_ref=dst_of(outs, a, _slot(_flip(k))), send_sem=send.at[a, k],
                    recv_sem=recv.at[a, k], device_id=_flip(k), device_id_type=MESH).wait_recv()
                cp.wait_send()
            local(ins, outs, sems, a).wait()

    return _Exchange([it[0] for it in items] + [items[a][3] for a in with_buf],
                     [SDS((N_DEV,) + shape_of(a), items[a][0].dtype) for a in range(n)],
                     {n + j: a for j, a in enumerate(with_buf)},
                     [pltpu.SemaphoreType.DMA((n, N_DEV)), pltpu.SemaphoreType.DMA((n, N_DEV)),
                      pltpu.SemaphoreType.DMA((n,))], start, wait)


def _gather_b(bufs):
    n = len(bufs)

    def copies(ins, outs, sems, a, c_of_block):
        send, recv = sems
        x, y, c, _ = _my_pos()
        res = []
        for k in (2, 4, 6):
            chip = _flip(k)
            blk = _slot((chip[0], chip[1], c if c_of_block == "mine" else 1 - c))
            res.append(pltpu.make_async_remote_copy(
                src_ref=ins[a].at[blk], dst_ref=outs[a].at[blk], send_sem=send.at[a, k], recv_sem=recv.at[a, k],
                device_id=_flip(1), device_id_type=MESH))
        return res

    def start(ins, outs, sems):
        for a in range(n):
            for cp in copies(ins, outs, sems, a, "mine"):
                cp.start()

    def wait(ins, outs, sems):
        for a in range(n):
            for cp in copies(ins, outs, sems, a, "sibling"):
                cp.wait_recv()
            for cp in copies(ins, outs, sems, a, "mine"):
                cp.wait_send()

    return _Exchange(list(bufs), [SDS(b.shape, b.dtype) for b in bufs], {a: a for a in range(n)},
                     [pltpu.SemaphoreType.DMA((n, N_DEV)), pltpu.SemaphoreType.DMA((n, N_DEV))], start, wait)


def _grad_x(items, n_l):
    items = [tuple(it) + (None,) * (4 - len(it)) for it in items]
    n = len(items)
    owners = [tuple(range(N_DEV)) if it[3] is None else tuple(it[3]) for it in items]
    inputs, first_in, aliases, out_shapes = [], [], {}, []
    for a, (arrs, l, recv, _) in enumerate(items):
        first_in.append(len(inputs))
        inputs += list(arrs)
        assert sum(arr.shape[0] for arr in arrs) == N_DEV
        if recv is not None:
            aliases[len(inputs)] = a
            inputs.append(recv)
        out_shapes.append(SDS((N_DEV, n_l) + arrs[0].shape[1:], arrs[0].dtype))

    def slab(ins, a, p):
        off = 0
        for j, arr in enumerate(items[a][0]):
            if p < off + arr.shape[0]:
                return ins[first_in[a] + j].at[p - off]
            off += arr.shape[0]
        raise AssertionError

    def rdma(ins, outs, sems, a, p, src_dev):
        send, recv, _ = sems
        return pltpu.make_async_remote_copy(
            src_ref=slab(ins, a, p), dst_ref=outs[a].at[src_dev, items[a][1]], send_sem=send.at[a, p],
            recv_sem=recv.at[a, src_dev], device_id=_dev(p), device_id_type=MESH)

    def local(ins, outs, sems, a, p):
        return pltpu.make_async_copy(slab(ins, a, p), outs[a].at[p, items[a][1]], sems[2].at[a])

    def start(ins, outs, sems):
        me = _my_pos()[3]
        for p in range(N_DEV):
            mine = [a for a in range(n) if p in owners[a]]

            @pl.when(me != p)
            def _():
                for a in mine:
                    rdma(ins, outs, sems, a, p, me).start()

            @pl.when(me == p)
            def _():
                for a in mine:
                    local(ins, outs, sems, a, p).start()

    def wait(ins, outs, sems):
        me = _my_pos()[3]
        for a in range(n):
            i_own = functools.reduce(operator.or_, [me == q for q in owners[a]])
            for p in range(N_DEV):
                @pl.when((me != p) & i_own)
                def _():
                    rdma(ins, outs, sems, a, p, p).wait_recv()

                if p in owners[a]:
                    @pl.when(me != p)
                    def _():
                        rdma(ins, outs, sems, a, p, p).wait_send()

                    @pl.when(me == p)
                    def _():
                        local(ins, outs, sems, a, p).wait()

    return _Exchange(inputs, out_shapes, aliases,
                     [pltpu.SemaphoreType.DMA((n, N_DEV)), pltpu.SemaphoreType.DMA((n, N_DEV)),
                      pltpu.SemaphoreType.DMA((n,))], start, wait)


def _pcall(body, args, *, name, grid, in_specs, out_specs, out_shape, scratch_shapes=(), carry=(), body_aliases=None):
    n_in, n_out, n_scr = len(in_specs), len(out_specs), len(scratch_shapes)
    c_in = [len(e.inputs) for e in carry]
    c_out = [len(e.out_shapes) for e in carry]
    c_sem = [len(e.sem_shapes) for e in carry]
    aliases = dict(body_aliases or {})
    for j, e in enumerate(carry):
        for i_loc, o_loc in e.aliases.items():
            aliases[n_in + sum(c_in[:j]) + i_loc] = n_out + sum(c_out[:j]) + o_loc

    def wrapped(*refs):
        def take(counts, pos):
            groups = []
            for cnt in counts:
                groups.append(refs[pos:pos + cnt])
                pos += cnt
            return groups, pos

        (ins,), pos = take([n_in], 0)
        cins, pos = take(c_in, pos)
        (outs,), pos = take([n_out], pos)
        couts, pos = take(c_out, pos)
        (scr,), pos = take([n_scr], pos)
        csems, pos = take(c_sem, pos)
        if carry:
            ids = [pl.program_id(k) for k in range(len(grid))]
            first = functools.reduce(operator.and_, [i == 0 for i in ids])
            last = functools.reduce(operator.and_, [i == g - 1 for i, g in zip(ids, grid)])

            @pl.when(first)
            def _():
                for e, ci, co, cs in zip(carry, cins, couts, csems):
                    e.start(ci, co, cs)

        body(*ins, *outs, *scr)
        if carry:
            @pl.when(last)
            def _():
                for e, ci, co, cs in zip(carry, cins, couts, csems):
                    e.wait(ci, co, cs)

    res = pl.pallas_call(
        wrapped, name=name, grid=grid,
        in_specs=list(in_specs) + [ANY] * sum(c_in),
        out_specs=list(out_specs) + [ANY] * sum(c_out),
        out_shape=list(out_shape) + [s for e in carry for s in e.out_shapes],
        scratch_shapes=list(scratch_shapes) + [s for e in carry for s in e.sem_shapes],
        input_output_aliases=aliases,
        compiler_params=pltpu.CompilerParams(dimension_semantics=("arbitrary",) * len(grid),
                                             vmem_limit_bytes=VMEM_LIMIT),
    )(*args, *[a for e in carry for a in e.inputs])
    outs, pos, extra = list(res[:n_out]), n_out, []
    for cnt in c_out:
        extra.append(list(res[pos:pos + cnt]))
        pos += cnt
    return outs, extra


def _all_gather(items, name):
    n = len(items)
    shape_of = lambda a: items[a][0].shape if items[a][1] is None else items[a][0].shape[1:]

    def body(*refs):
        ins, outs, (send_sems, recv_sems, local_sems) = refs[:n], refs[n:2 * n], refs[2 * n:]
        x, y, c, me = _my_pos()
        src_of = lambda a: ins[a] if items[a][1] is None else ins[a].at[items[a][1]]

        def copy(a, k, block, to, src=None):
            dst = outs[a].at[_slot(block)]
            return pltpu.make_async_remote_copy(
                src_ref=dst if src is None else src, dst_ref=dst,
                send_sem=send_sems.at[a, k], recv_sem=recv_sems.at[a, k], device_id=to, device_id_type=MESH)

        mine = [pltpu.make_async_copy(src_of(a), outs[a].at[me], local_sems.at[a]) for a in range(n)]
        for cp in mine:
            cp.start()
        first = [copy(a, k, (x, y, c), _flip(k), src=src_of(a)) for a in range(n) for k in (1, 2, 4, 6)]
        for cp in first:
            cp.start()
        passed = []
        for k in (2, 4, 6):
            for a in range(n):
                copy(a, k, _flip(k), (x, y, c)).wait_recv()
                fwd = copy(a, k + 1, _flip(k), _flip(1))
                fwd.start()
                passed.append(fwd)
        for a in range(n):
            copy(a, 1, _flip(1), (x, y, c)).wait_recv()
            for k in (2, 4, 6):
                copy(a, k + 1, _flip(k + 1), (x, y, c)).wait_recv()
        for cp in first + passed:
            cp.wait_send()
        for cp in mine:
            cp.wait()

    return pl.pallas_call(
        body, name=name,
        in_specs=[ANY] * n, out_specs=[ANY] * n,
        out_shape=[SDS((N_DEV,) + shape_of(a), items[a][0].dtype) for a in range(n)],
        scratch_shapes=[pltpu.SemaphoreType.DMA((n, N_DEV)), pltpu.SemaphoreType.DMA((n, N_DEV)),
                        pltpu.SemaphoreType.DMA((n,))],
    )(*[it[0] for it in items])


def _mm(a, b):
    return jnp.dot(a.astype(BF16), b.astype(BF16), preferred_element_type=F32)


def _mm_nt(a, b):
    return lax.dot_general(a.astype(BF16), b.astype(BF16), (((1,), (1,)), ((), ())), preferred_element_type=F32)


def _mm_tn(a, b):
    return lax.dot_general(a.astype(BF16), b.astype(BF16), (((0,), (0,)), ((), ())), preferred_element_type=F32)


def _rms_r(x):
    return lax.rsqrt(jnp.mean(x * x, axis=-1, keepdims=True) + NORM_EPS)


def _rms_bwd(x, r, g, dy):
    gy = dy * g
    dx = r * (gy - x * (r * r) * jnp.mean(gy * x, axis=-1, keepdims=True))
    dg = jnp.sum(dy * x * r, axis=0, keepdims=True)
    return dx, dg


def _sigmoid(x):
    return 1.0 / (1.0 + jnp.exp(-x))


def _sigmoid_t(x):
    return 0.5 * jnp.tanh(0.5 * x) + 0.5


def _dsilu(z, sz):
    return sz * (1.0 + z * (1.0 - sz))


def _swiglu_bf16(g, u):
    sg = 0.5 * jnp.tanh(0.5 * g) + 0.5
    silu = g * sg
    return silu * u, silu, sg + silu * (1.0 - sg)


_GELU_C = math.sqrt(2.0 / math.pi)


def _gelu(x):
    t = jnp.tanh(_GELU_C * (x + 0.044715 * x * x * x))
    return 0.5 * x * (1.0 + t), t


def _dgelu(x, t):
    return 0.5 * (1.0 + t) + 0.5 * x * (1.0 - t * t) * _GELU_C * (1.0 + 3.0 * 0.044715 * x * x)


def _log1p(e):
    return jnp.where(e < 1e-2, e * (1.0 - e * (0.5 - e * (1.0 / 3.0))), jnp.log(1.0 + e))


def _softplus(x):
    return jnp.maximum(x, 0.0) + _log1p(jnp.exp(-jnp.abs(x)))


def _neg_expm1(x):
    small = -x * (1.0 + x * (0.5 + x * (1.0 / 6.0) * (1.0 + x * 0.25)))
    return jnp.where(x > -1e-2, small, 1.0 - jnp.exp(x))


def _shift_down(x, s):
    return x if s == 0 else pltpu.roll(x, s, 0)


def _shift_up(x, s):
    return x if s == 0 else pltpu.roll(x, x.shape[0] - s, 0)


def _ffn_wspecs(d, fc, order):
    f_of = (lambda i, f: f) if order == "tf" else (lambda f, i: f)
    n_f = N_DEV // 2
    return [pl.BlockSpec((None, fc, d), lambda *g: (f_of(*g), 0, 0)),
            pl.BlockSpec((None, fc, d), lambda *g: (f_of(*g) + n_f, 0, 0)),
            pl.BlockSpec((2, fc // 2, d), lambda *g: (f_of(*g), 0, 0))]


def _ffn_fwd(x, pre_g, post_g, wgu_t, wd, name, carry=()):
    s, d = x.shape
    fc = wgu_t.shape[1]
    ts = 2 * _time_tile(s)
    n_t, n_f = s // ts, N_DEV // 2

    def body(x_ref, pg_ref, qg_ref, wg_ref, wu_ref, wd_ref, xo_ref, h_ref, g_ref, u_ref, d_ref, h_scr, acc):
        f = pl.program_id(1)

        @pl.when(f == 0)
        def _():
            xv = x_ref[...]
            hv = (xv * _rms_r(xv) * pg_ref[...]).astype(BF16)
            h_scr[...] = hv
            h_ref[...] = hv
            acc[...] = jnp.zeros_like(acc)

        hv = h_scr[...]
        g = _mm_nt(hv, wg_ref[...])
        u = _mm_nt(hv, wu_ref[...])
        g = g.astype(BF16)
        u = u.astype(BF16)
        g_ref[...] = g
        u_ref[...] = u
        acc[...] += jnp.dot(_swiglu_bf16(g, u)[0], wd_ref[...].reshape(fc, d), preferred_element_type=F32)

        @pl.when(f == n_f - 1)
        def _():
            dv = acc[...]
            d_ref[...] = dv.astype(BF16)
            xo_ref[...] = x_ref[...] + 0.5 * (dv * _rms_r(dv) * qg_ref[...])

    row = pl.BlockSpec((ts, d), lambda i, f: (i, 0))
    vec = pl.BlockSpec((1, d), lambda i, f: (0, 0))
    act = pl.BlockSpec((None, ts, fc), lambda i, f: (f, i, 0))
    return _pcall(
        body, (x, pre_g, post_g, wgu_t, wgu_t, wd), name=name, grid=(n_t, n_f),
        in_specs=[row, vec, vec] + _ffn_wspecs(d, fc, "tf"),
        out_specs=[row, row, act, act, row],
        out_shape=[SDS((s, d), F32), SDS((s, d), BF16), SDS((n_f, s, fc), BF16), SDS((n_f, s, fc), BF16),
                   SDS((s, d), BF16)],
        scratch_shapes=[pltpu.VMEM((ts, d), BF16), pltpu.VMEM((ts, d), F32)], carry=carry)


def _ffn_bwd_act(dxo, dmid, x, pre_g, post_g, g_s, u_s, wgu_t, wd, name, carry=(), loss_target=None):
    s, d = x.shape
    fc = wgu_t.shape[1]
    ts = _time_tile(s)
    n_t, n_f = s // ts, N_DEV // 2
    with_loss = loss_target is not None

    def body(*refs):
        t_ref, refs = (refs[0], refs[1:]) if with_loss else (None, refs)
        (dxo_ref, dm_ref, x_ref, pg_ref, qg_ref, g_ref, u_ref, wg_ref, wu_ref, wd_ref,
         dx_ref, dd_ref, dg_ref, du_ref, dpg_ref, dqg_ref) = refs[:16]
        loss_ref = refs[16] if with_loss else None
        dd_scr, dh_acc = refs[-2:]
        i, f = pl.program_id(0), pl.program_id(1)

        def incoming():
            return (dxo_ref[...] - t_ref[...]) * (1.0 / d) if with_loss else dxo_ref[...]

        @pl.when((i == 0) & (f == 0))
        def _():
            dpg_ref[...] = jnp.zeros_like(dpg_ref)
            dqg_ref[...] = jnp.zeros_like(dqg_ref)
            if with_loss:
                loss_ref[...] = jnp.zeros_like(loss_ref)

        @pl.when(f == 0)
        def _():
            if with_loss:
                err = dxo_ref[...] - t_ref[...]
                loss_ref[...] += 0.5 * jnp.sum(jnp.mean(err * err, axis=-1, keepdims=True), axis=0, keepdims=True)
            dv = dm_ref[...].astype(F32)
            ddv, dq = _rms_bwd(dv, _rms_r(dv), qg_ref[...], 0.5 * incoming())
            dqg_ref[...] += dq
            dd_scr[...] = ddv.astype(BF16)
            dd_ref[...] = ddv.astype(BF16)
            dh_acc[...] = jnp.zeros_like(dh_acc)

        da = _mm_nt(dd_scr[...], wd_ref[...].reshape(fc, d)).astype(BF16)
        u = u_ref[...]
        _, silu, dsilu = _swiglu_bf16(g_ref[...], u)
        du = da * silu
        dg = da * u * dsilu
        dg_ref[...] = dg
        du_ref[...] = du
        dh_acc[...] += _mm(dg, wg_ref[...]) + _mm(du, wu_ref[...])

        @pl.when(f == n_f - 1)
        def _():
            xv = x_ref[...]
            dxv, dp = _rms_bwd(xv, _rms_r(xv), pg_ref[...], dh_acc[...])
            dpg_ref[...] += dp
            dx_ref[...] = incoming() + dxv

    row = pl.BlockSpec((ts, d), lambda i, f: (i, 0))
    vec = pl.BlockSpec((1, d), lambda i, f: (0, 0))
    act = pl.BlockSpec((None, ts, fc), lambda i, f: (f, i, 0))
    lead = [loss_target] if with_loss else []
    return _pcall(
        body, (*lead, dxo, dmid, x, pre_g, post_g, g_s, u_s, wgu_t, wgu_t, wd), name=name, grid=(n_t, n_f),
        in_specs=[row] * len(lead) + [row, row, row, vec, vec, act, act] + _ffn_wspecs(d, fc, "tf"),
        out_specs=[row, row, act, act, vec, vec] + [pl.BlockSpec((1, BLK), lambda i, f: (0, 0))] * len(lead),
        out_shape=[SDS((s, d), F32), SDS((s, d), BF16), SDS((n_f, s, fc), BF16), SDS((n_f, s, fc), BF16),
                   SDS((1, d), F32), SDS((1, d), F32)] + [SDS((1, BLK), F32)] * len(lead),
        scratch_shapes=[pltpu.VMEM((ts, d), BF16), pltpu.VMEM((ts, d), F32)], carry=carry)


def _ffn_bwd_w(h, dd, g_s, u_s, dg, du, name, carry=()):
    s, d = h.shape
    n_f, _, fc = g_s.shape
    ts = 2 * _time_tile(s)
    n_t = s // ts

    def body(h_ref, dd_ref, g_ref, u_ref, dg_ref, du_ref, wg_ref, wu_ref, wd_ref, acc_g, acc_u, acc_d):
        i = pl.program_id(1)

        @pl.when(i == 0)
        def _():
            acc_g[...] = jnp.zeros_like(acc_g)
            acc_u[...] = jnp.zeros_like(acc_u)
            acc_d[...] = jnp.zeros_like(acc_d)

        a = _swiglu_bf16(g_ref[...], u_ref[...])[0]
        hv = h_ref[...]
        acc_g[...] += _mm_tn(dg_ref[...], hv)
        acc_u[...] += _mm_tn(du_ref[...], hv)
        acc_d[...] += _mm_tn(a, dd_ref[...])

        @pl.when(i == n_t - 1)
        def _():
            wg_ref[...] = acc_g[...].astype(BF16)
            wu_ref[...] = acc_u[...].astype(BF16)
            wd_ref[...] = acc_d[...].astype(BF16)

    row = pl.BlockSpec((ts, d), lambda f, i: (i, 0))
    act = pl.BlockSpec((None, ts, fc), lambda f, i: (f, i, 0))
    out = pl.BlockSpec((None, fc, d), lambda f, i: (f, 0, 0))
    return _pcall(
        body, (h, dd, g_s, u_s, dg, du), name=name, grid=(n_f, n_t),
        in_specs=[row, row, act, act, act, act], out_specs=[out, out, out],
        out_shape=[SDS((n_f, fc, d), BF16)] * 3,
        scratch_shapes=[pltpu.VMEM((fc, d), F32)] * 3, carry=carry)


def _ffn_bwd_w_send(h, dd, g_s, u_s, dg, du, recv_gu, recv_d, layer, name, carry=()):
    s, d = h.shape
    n_f, _, fc = g_s.shape
    ts = _time_tile(s)
    n_t = s // ts
    half = fc // 2

    def chunk_of(step):
        return (step + 2 * lax.axis_index("x") + lax.axis_index("y")) % n_f

    def body(h_ref, dd_ref, g_ref, u_ref, dg_ref, du_ref, _rgu_in, _rd_in, rgu_ref, rd_ref,
             acc_g, acc_u, acc_d, st_g, st_u, st_d, pair_gu, pair_d, zeros,
             send_sems, recv_sems, local_sems, pair_sems, zero_sems):
        f, i = pl.program_id(0), pl.program_id(1)
        x, y, c_me, me = _my_pos()
        sibling = me ^ 1

        @pl.when(i == 0)
        def _():
            acc_g[...] = jnp.zeros_like(acc_g)
            acc_u[...] = jnp.zeros_like(acc_u)
            acc_d[...] = jnp.zeros_like(acc_d)

        def zero_fills():
            res = []
            for n_k, k in enumerate((2, 4, 6)):
                other = _flip(k)
                slot = _slot((other[0], other[1], 1 - c_me))
                res += [pltpu.make_async_copy(zeros, rgu_ref.at[slot, layer, pl.ds(0, half)], zero_sems.at[n_k, 0]),
                        pltpu.make_async_copy(zeros, rgu_ref.at[slot, layer, pl.ds(half, half)], zero_sems.at[n_k, 1]),
                        pltpu.make_async_copy(zeros, rd_ref.at[slot, layer], zero_sems.at[n_k, 2])]
            return res

        @pl.when((f == 0) & (i == 0))
        def _():
            zeros[...] = jnp.zeros_like(zeros)
            for cp in zero_fills():
                cp.start()

        a = _swiglu_bf16(g_ref[...], u_ref[...])[0]
        hv = h_ref[...]
        acc_g[...] += _mm_tn(dg_ref[...], hv)
        acc_u[...] += _mm_tn(du_ref[...], hv)
        acc_d[...] += _mm_tn(a, dd_ref[...])

        def messages(fs):
            c = chunk_of(fs)
            lo, hi = pl.ds(0, half), pl.ds(half, half)
            return [(st_g.at[fs], pair_gu.at[fs // 2, 0], rgu_ref, 0, c, 0),
                    (st_u.at[fs], pair_gu.at[fs // 2, 1], rgu_ref, 0, c + n_f, 1),
                    (st_d.at[fs, lo], pair_d.at[fs], rd_ref, 1, 2 * c, 2),
                    (st_d.at[fs, hi], pair_d.at[fs], rd_ref, 1, 2 * c + 1, 3)]

        def roles(p):
            same_chip = (p >> 1) == (me >> 1)
            same_c = (p & 1) == c_me
            return p == me, p == sibling, (~same_chip) & same_c, (~same_chip) & (~same_c)

        def to_owner(fs, msg, src_dev):
            src, _, buf, row, p, j = msg
            return pltpu.make_async_remote_copy(
                src_ref=src, dst_ref=buf.at[src_dev, layer], send_sem=send_sems.at[fs, j],
                recv_sem=recv_sems.at[row, src_dev], device_id=_dev(p), device_id_type=MESH)

        def to_pair(fs, msg):
            src, pair, _, _, _, j = msg
            return pltpu.make_async_remote_copy(
                src_ref=src, dst_ref=pair, send_sem=send_sems.at[fs, j], recv_sem=pair_sems.at[fs, j],
                device_id=_dev(sibling), device_id_type=MESH)

        def local(fs, msg):
            src, _, buf, _, p, j = msg
            return pltpu.make_async_copy(src, buf.at[p, layer], local_sems.at[fs, j])

        for fs in range(n_f):
            @pl.when((f == fs) & (i == n_t - 1))
            def _():
                st_g[fs] = acc_g[...].astype(BF16)
                st_u[fs] = acc_u[...].astype(BF16)
                st_d[fs] = acc_d[...].astype(BF16)
                msgs = messages(fs)
                for msg in msgs:
                    mine, sib, _, hand_over = roles(msg[4])

                    @pl.when(mine)
                    def _():
                        local(fs, msg).start()

                    @pl.when(sib)
                    def _():
                        to_owner(fs, msg, me).start()

                    @pl.when(hand_over)
                    def _():
                        to_pair(fs, msg).start()
                for msg in msgs:
                    @pl.when(roles(msg[4])[2])
                    def _():
                        src, pair = msg[0], msg[1]
                        to_pair(fs, msg).wait_recv()
                        src[...] = (src[...].astype(F32) + pair[...].astype(F32)).astype(BF16)
                        to_owner(fs, msg, me).start()

        @pl.when((f == n_f - 1) & (i == n_t - 1))
        def _():
            for fs in range(n_f):
                for msg in messages(fs):
                    mine = roles(msg[4])[0]

                    @pl.when(mine)
                    def _():
                        local(fs, msg).wait()

                    @pl.when(~mine)
                    def _():
                        to_owner(fs, msg, me).wait_send()
            for k in (1, 2, 4, 6):
                src_dev = _slot(_flip(k))
                to_owner(0, messages(0)[0], src_dev).wait_recv()
                to_owner(0, messages(0)[2], src_dev).wait_recv()
            for cp in zero_fills():
                cp.wait()

    row = pl.BlockSpec((ts, d), lambda f, i: (i, 0))
    act = pl.BlockSpec((None, ts, fc), lambda f, i: (chunk_of(f), i, 0))
    return _pcall(
        body, (h, dd, g_s, u_s, dg, du, recv_gu, recv_d), name=name, grid=(n_f, n_t),
        in_specs=[row, row, act, act, act, act, ANY, ANY], out_specs=[ANY, ANY],
        out_shape=[SDS(recv_gu.shape, recv_gu.dtype), SDS(recv_d.shape, recv_d.dtype)],
        scratch_shapes=[pltpu.VMEM((fc, d), F32)] * 3 + [pltpu.VMEM((n_f, fc, d), BF16)] * 3
        + [pltpu.VMEM((n_f // 2, 2, fc, d), BF16), pltpu.VMEM((n_f, half, d), BF16), pltpu.VMEM((half, d), BF16)]
        + [pltpu.SemaphoreType.DMA((n_f, 4)), pltpu.SemaphoreType.DMA((2, N_DEV)), pltpu.SemaphoreType.DMA((n_f, 4)),
           pltpu.SemaphoreType.DMA((n_f, 4)), pltpu.SemaphoreType.DMA((3, 3))],
        carry=carry, body_aliases={6: 0, 7: 1})


_PROJ_WIDTHS = (W_A, W_A, W_B, KV_W, KV_W, 2 * W_C)


def _mix_in_fwd(x, pre_g, w_in_t, name, carry=()):
    s, d = x.shape
    ts = 2 * _time_tile(s)

    def body(x_ref, pg_ref, w_ref, hn_ref, *outs):
        xv = x_ref[...]
        hn = (xv * _rms_r(xv) * pg_ref[...]).astype(BF16)
        hn_ref[...] = hn
        proj = _mm_nt(hn, w_ref[...])
        off = 0
        for o_ref, w in zip(outs, _PROJ_WIDTHS):
            o_ref[...] = proj[:, off:off + w]
            off += w

    row = lambda w: pl.BlockSpec((ts, w), lambda i: (i, 0))
    return _pcall(
        body, (x, pre_g, w_in_t), name=name, grid=(s // ts,),
        in_specs=[row(d), pl.BlockSpec((1, d), lambda i: (0, 0)), pl.BlockSpec((D_IN_PROJ, d), lambda i: (0, 0))],
        out_specs=[row(d)] + [row(w) for w in _PROJ_WIDTHS],
        out_shape=[SDS((s, d), BF16)] + [SDS((s, w), F32) for w in _PROJ_WIDTHS], carry=carry)


def _mix_in_bwd(dres, x, pre_g, hn, w_in_t, dlx, dlg, dq, dk, dk_up, dv, dv_up, dglu, name, carry=()):
    s, d = x.shape
    ts = _time_tile(s)
    n_t = s // ts

    def body(dres_ref, x_ref, pg_ref, hn_ref, w_ref, dlx_ref, dlg_ref, dq_ref, dk_ref, dkn_ref,
             dv_ref, dvn_ref, dglu_ref, dx_ref, dw_ref, dpg_ref, acc):
        i = pl.program_id(0)

        @pl.when(i == 0)
        def _():
            acc[...] = jnp.zeros_like(acc)
            dpg_ref[...] = jnp.zeros_like(dpg_ref)

        def with_next(cur_ref, nxt_ref):
            nxt = jnp.where(i < n_t - 1, nxt_ref[...], 0.0)
            if ts == BLK:
                return cur_ref[...] + nxt
            return jnp.concatenate([cur_ref[:ts - BLK, :], cur_ref[ts - BLK:, :] + nxt], axis=0)

        dproj = jnp.concatenate([dlx_ref[...], dlg_ref[...], dq_ref[...], with_next(dk_ref, dkn_ref),
                                 with_next(dv_ref, dvn_ref), dglu_ref[...]], axis=1).astype(BF16)
        dhn = _mm(dproj, w_ref[...])
        acc[...] += _mm_tn(dproj, hn_ref[...])
        xv = x_ref[...]
        dxv, dp = _rms_bwd(xv, _rms_r(xv), pg_ref[...], dhn)
        dpg_ref[...] += dp
        dx_ref[...] = dres_ref[...] + dxv

        @pl.when(i == n_t - 1)
        def _():
            dw_ref[...] = acc[...].astype(BF16)

    row = lambda w: pl.BlockSpec((ts, w), lambda i: (i, 0))
    nxt = pl.BlockSpec((BLK, KV_W), lambda i: (jnp.minimum(i + 1, n_t - 1), 0))
    vec = pl.BlockSpec((1, d), lambda i: (0, 0))
    full = pl.BlockSpec((D_IN_PROJ, d), lambda i: (0, 0))
    return _pcall(
        body, (dres, x, pre_g, hn, w_in_t, dlx, dlg, dq, dk, dk_up, dv, dv_up, dglu), name=name, grid=(n_t,),
        in_specs=[row(d), row(d), vec, row(d), full, row(W_A), row(W_A), row(W_B), row(KV_W), nxt,
                  row(KV_W), nxt, row(2 * W_C)],
        out_specs=[row(d), full, vec],
        out_shape=[SDS((s, d), F32), SDS((D_IN_PROJ, d), BF16), SDS((1, d), F32)],
        scratch_shapes=[pltpu.VMEM((D_IN_PROJ, d), F32)], carry=carry)


def _lru_gates(xc, lru_p):
    cw_ref, cb_ref, wa_ref, ba_ref, wx_ref, bx_ref, lam_ref = lru_p
    c = cb_ref[...]
    for j in range(LRU_K):
        c = c + cw_ref[j:j + 1, :] * _shift_down(xc, LRU_K - 1 - j)[LRU_HALO:, :]
    r = _sigmoid(_mm(c, wa_ref[...]) + ba_ref[...])
    ig = _sigmoid(_mm(c, wx_ref[...]) + bx_ref[...])
    sp = _softplus(-lam_ref[...])
    log_a = -LRU_C * r * sp
    a = jnp.exp(log_a)
    m = jnp.sqrt(_neg_expm1(2.0 * log_a))
    return c, r, ig, sp, a, m


def _lru_pspecs():
    small = lambda r: pl.BlockSpec((r, W_A), lambda i: (0, 0))
    return [small(LRU_K), small(1), small(W_A), small(1), small(W_A), small(1), small(1)]


def _lru_fwd(lx, lg, lru_p, name, carry=()):
    s = lx.shape[0]
    ts = _time_tile(s)
    n8 = ts // LRU_HALO

    def body(lx_ref, lxp_ref, lg_ref, *rest):
        lru_p, (ya_ref, h_ref, hcarry) = rest[:7], rest[7:]
        i = pl.program_id(0)
        prev = jnp.where(i > 0, lxp_ref[...], 0.0)
        xc = jnp.concatenate([prev, lx_ref[...]], axis=0)
        c, r, ig, sp, a, m = _lru_gates(xc, lru_p)
        acc_a, acc_b = a, m * (ig * c)
        t = lax.broadcasted_iota(jnp.int32, a.shape, 0)
        k = 1
        while k < ts:
            keep = t >= k
            acc_b = jnp.where(keep, acc_a * _shift_down(acc_b, k) + acc_b, acc_b)
            acc_a = jnp.where(keep, acc_a * _shift_down(acc_a, k), acc_a)
            k *= 2
        h0 = jnp.where(i > 0, hcarry[...], 0.0)
        h = acc_b + acc_a * h0
        hcarry[...] = h[ts - 1:ts, :]
        h_ref[...] = h
        ya_ref[...] = _gelu(lg_ref[...])[0] * h

    row = pl.BlockSpec((ts, W_A), lambda i: (i, 0))
    prev8 = pl.BlockSpec((LRU_HALO, W_A), lambda i: (jnp.maximum(i * n8 - 1, 0), 0))
    return _pcall(
        body, (lx, lx, lg, *lru_p), name=name, grid=(s // ts,),
        in_specs=[row, prev8, row] + _lru_pspecs(), out_specs=[row, row],
        out_shape=[SDS((s, W_A), F32), SDS((s, W_A), F32)],
        scratch_shapes=[pltpu.VMEM((1, W_A), F32)], carry=carry)


def _lru_bwd(dya, lx, lg, h_s, lru_p, name, carry=()):
    s = lx.shape[0]
    ts = _time_tile(s)
    n_t = s // ts
    n8 = ts // LRU_HALO

    def body(dya_ref, lx_ref, lxp_ref, lg_ref, h_ref, hp_ref, *rest):
        lru_p = rest[:7]
        (dlx_ref, dlg_ref, dcw_ref, dcb_ref, dwa_ref, dba_ref, dwx_ref, dbx_ref, dlam_ref,
         carry_a, carry_l, carry_dc) = rest[7:]
        cw_ref, _, wa_ref, _, wx_ref, _, lam_ref = lru_p
        i = pl.program_id(0)
        first_tile = i == n_t - 1
        last_tile = i == 0

        @pl.when(i == 0)
        def _():
            for ref in (dcw_ref, dcb_ref, dwa_ref, dba_ref, dwx_ref, dbx_ref, dlam_ref):
                ref[...] = jnp.zeros_like(ref)

        prev = jnp.where(first_tile, 0.0, lxp_ref[...])
        xc = jnp.concatenate([prev, lx_ref[...]], axis=0)
        c, r, ig, sp, a, m = _lru_gates(xc, lru_p)
        h = h_ref[...]
        hcat = jnp.concatenate([jnp.where(first_tile, 0.0, hp_ref[...]), h], axis=0)
        h_m1 = _shift_down(hcat, 1)[LRU_HALO:, :]
        lg = lg_ref[...]
        ge, th = _gelu(lg)
        dya = dya_ref[...]
        dlg_ref[...] = dya * h * _dgelu(lg, th)
        dh = dya * ge
        t = lax.broadcasted_iota(jnp.int32, a.shape, 0)
        a_next = jnp.where(t < ts - 1, _shift_up(a, 1), jnp.where(last_tile, 0.0, carry_a[...]))
        acc_a, acc_b = a_next, dh
        k = 1
        while k < ts:
            keep = t < ts - k
            acc_b = jnp.where(keep, acc_a * _shift_up(acc_b, k) + acc_b, acc_b)
            acc_a = jnp.where(keep, acc_a * _shift_up(acc_a, k), acc_a)
            k *= 2
        lam_beyond = jnp.where(last_tile, 0.0, carry_l[...])
        lmb = acc_b + acc_a * lam_beyond
        carry_a[...] = a[0:1, :]
        carry_l[...] = lmb[0:1, :]
        gi = ig * c
        dgi = lmb * m
        dla = lmb * h_m1 * a - (lmb * gi) * (a * a) / m
        dr = dla * (-LRU_C * sp)
        dsp = jnp.sum(dla * (-LRU_C * r), axis=0, keepdims=True)
        dlam_ref[...] += -dsp * _sigmoid(-lam_ref[...])
        dra = dr * r * (1.0 - r)
        dia = dgi * c * ig * (1.0 - ig)
        dc = dgi * ig + _mm_nt(dra, wa_ref[...]) + _mm_nt(dia, wx_ref[...])
        dwa_ref[...] += _mm_tn(c, dra)
        dwx_ref[...] += _mm_tn(c, dia)
        dba_ref[...] += jnp.sum(dra, axis=0, keepdims=True)
        dbx_ref[...] += jnp.sum(dia, axis=0, keepdims=True)
        dcb_ref[...] += jnp.sum(dc, axis=0, keepdims=True)
        dcc = jnp.concatenate([dc, jnp.where(last_tile, 0.0, carry_dc[...])], axis=0)
        carry_dc[...] = dc[0:LRU_HALO, :]
        dlx = jnp.zeros_like(dc)
        for j in range(LRU_K):
            sh = LRU_K - 1 - j
            dcw_ref[j:j + 1, :] += jnp.sum(dc * _shift_down(xc, sh)[LRU_HALO:, :], axis=0, keepdims=True)
            dlx = dlx + cw_ref[j:j + 1, :] * _shift_up(dcc, sh)[:ts, :]
        dlx_ref[...] = dlx

    row = pl.BlockSpec((ts, W_A), lambda i: (n_t - 1 - i, 0))
    prev8 = pl.BlockSpec((LRU_HALO, W_A), lambda i: (jnp.maximum((n_t - 1 - i) * n8 - 1, 0), 0))
    small = lambda r: pl.BlockSpec((r, W_A), lambda i: (0, 0))
    return _pcall(
        body, (dya, lx, lx, lg, h_s, h_s, *lru_p), name=name, grid=(n_t,),
        in_specs=[row, row, prev8, row, row, prev8] + _lru_pspecs(),
        out_specs=[row, row, small(LRU_K), small(1), small(W_A), small(1), small(W_A), small(1), small(1)],
        out_shape=[SDS((s, W_A), F32), SDS((s, W_A), F32), SDS((LRU_K, W_A), F32), SDS((1, W_A), F32),
                   SDS((W_A, W_A), F32), SDS((1, W_A), F32), SDS((W_A, W_A), F32), SDS((1, W_A), F32),
                   SDS((1, W_A), F32)],
        scratch_shapes=[pltpu.VMEM((1, W_A), F32), pltpu.VMEM((1, W_A), F32), pltpu.VMEM((LRU_HALO, W_A), F32)],
        carry=carry)


_ATT_ROWS = N_Q_HEADS * BLK
_GRP_ROWS = Q_PER_KV * BLK


def _attn_stack(ref, rows, g):
    return jnp.concatenate([ref[rows, h * HEAD_DIM:(h + 1) * HEAD_DIM]
                            for h in range(g * Q_PER_KV, (g + 1) * Q_PER_KV)], axis=0)


def _attn_unstack(parts):
    return jnp.concatenate([p[j * BLK:(j + 1) * BLK, :] for p in parts for j in range(Q_PER_KV)], axis=1)


def _grp(x, g):
    return x[:, g * _GRP_ROWS:(g + 1) * _GRP_ROWS]


def _attn_block(q_ref, k_ref, kp_ref, v_ref, vp_ref, sink_row, i, b):
    rows, prev = slice(b * BLK, (b + 1) * BLK), slice((b - 1) * BLK, b * BLK)
    qs, kcs, kps, vcs, vps = [], [], [], [], []
    for g in range(N_KV_HEADS):
        cols = slice(g * HEAD_DIM, (g + 1) * HEAD_DIM)
        qs.append(_attn_stack(q_ref, rows, g))
        kcs.append(k_ref[rows, cols])
        vcs.append(v_ref[rows, cols])
        kps.append(kp_ref[:, cols] if b == 0 else k_ref[prev, cols])
        vps.append(vp_ref[:, cols] if b == 0 else v_ref[prev, cols])
    scale = 1.0 / math.sqrt(HEAD_DIM)
    sc = jnp.concatenate([_mm_nt(kcs[g], qs[g]) for g in range(N_KV_HEADS)], axis=1) * scale
    sp = jnp.concatenate([_mm_nt(kps[g], qs[g]) for g in range(N_KV_HEADS)], axis=1) * scale
    kj = lax.broadcasted_iota(jnp.int32, (BLK, _ATT_ROWS), 0)
    qi = lax.broadcasted_iota(jnp.int32, (BLK, _ATT_ROWS), 1) & (BLK - 1)
    sc = jnp.where(kj <= qi, sc, NEG_BIG)
    sp = jnp.where((kj > qi) if b > 0 else ((kj > qi) & (i > 0)), sp, NEG_BIG)
    m = jnp.maximum(jnp.maximum(jnp.max(sc, axis=0, keepdims=True), jnp.max(sp, axis=0, keepdims=True)), sink_row)
    pc = jnp.exp(sc - m)
    pp = jnp.exp(sp - m)
    es = jnp.exp(sink_row - m)
    inv = 1.0 / (jnp.sum(pc, axis=0, keepdims=True) + jnp.sum(pp, axis=0, keepdims=True) + es)
    return qs, kcs, kps, vcs, vps, pc * inv, pp * inv, es * inv


def _attn_specs(s, ts):
    bpt = ts // BLK
    tile = lambda w: pl.BlockSpec((ts, w), lambda i: (i, 0))
    prv = pl.BlockSpec((BLK, KV_W), lambda i: (jnp.maximum(i * bpt - 1, 0), 0))
    sink = pl.BlockSpec((1, _ATT_ROWS), lambda i: (0, 0))
    return bpt, tile, prv, sink


def _attn_fwd(q, k, v, sink_row, name, carry=()):
    s = q.shape[0]
    ts = _time_tile(s)
    bpt, tile, prv, sink = _attn_specs(s, ts)

    def body(q_ref, k_ref, kp_ref, v_ref, vp_ref, sk_ref, y_ref):
        i = pl.program_id(0)
        for b in range(bpt):
            _, _, _, vcs, vps, pc, pp, _ = _attn_block(q_ref, k_ref, kp_ref, v_ref, vp_ref, sk_ref[...], i, b)
            outs = [_mm_tn(_grp(pc, g), vcs[g]) + _mm_tn(_grp(pp, g), vps[g]) for g in range(N_KV_HEADS)]
            y_ref[b * BLK:(b + 1) * BLK, :] = _attn_unstack(outs)

    return _pcall(
        body, (q, k, k, v, v, sink_row), name=name, grid=(s // ts,),
        in_specs=[tile(W_B), tile(KV_W), prv, tile(KV_W), prv, sink],
        out_specs=[tile(W_B)], out_shape=[SDS((s, W_B), F32)], carry=carry)


def _attn_bwd(dy, q, k, v, sinks, name, carry=()):
    s = q.shape[0]
    ts = _time_tile(s)
    n_t = s // ts
    bpt, tile, prv, sink = _attn_specs(s, ts)

    def body(dy_ref, q_ref, k_ref, kp_ref, v_ref, vp_ref, sk_ref, dq_ref, dk_ref, dv_ref, dku_ref, dvu_ref, dsk_ref):
        i = pl.program_id(0)

        @pl.when(i == 0)
        def _():
            dsk_ref[...] = jnp.zeros_like(dsk_ref)

        scale = 1.0 / math.sqrt(HEAD_DIM)
        groups = range(N_KV_HEADS)
        head_row = lax.broadcasted_iota(jnp.int32, (N_Q_HEADS, BLK), 0)
        dsk = jnp.zeros((N_Q_HEADS, BLK), F32)
        dk_blocks, dv_blocks = [], []
        for b in range(bpt):
            rows = slice(b * BLK, (b + 1) * BLK)
            qs, kcs, kps, vcs, vps, pc, pp, ps = _attn_block(q_ref, k_ref, kp_ref, v_ref, vp_ref, sk_ref[...], i, b)
            dos = [_attn_stack(dy_ref, rows, g) for g in groups]
            dpc = jnp.concatenate([_mm_nt(vcs[g], dos[g]) for g in groups], axis=1)
            dpp = jnp.concatenate([_mm_nt(vps[g], dos[g]) for g in groups], axis=1)
            delta = jnp.sum(pc * dpc, axis=0, keepdims=True) + jnp.sum(pp * dpp, axis=0, keepdims=True)
            dsc = pc * (dpc - delta) * scale
            dsp = pp * (dpp - delta) * scale
            dq_ref[rows, :] = _attn_unstack([_mm_tn(_grp(dsc, g), kcs[g]) + _mm_tn(_grp(dsp, g), kps[g])
                                             for g in groups])
            dk_blocks.append(jnp.concatenate([_mm(_grp(dsc, g), qs[g]) for g in groups], axis=1))
            dv_blocks.append(jnp.concatenate([_mm(_grp(pc, g), dos[g]) for g in groups], axis=1))
            dkp = jnp.concatenate([_mm(_grp(dsp, g), qs[g]) for g in groups], axis=1)
            dvp = jnp.concatenate([_mm(_grp(pp, g), dos[g]) for g in groups], axis=1)
            if b == 0:
                dku_ref[...] = dkp
                dvu_ref[...] = dvp
            else:
                dk_blocks[b - 1] = dk_blocks[b - 1] + dkp
                dv_blocks[b - 1] = dv_blocks[b - 1] + dvp
            dsink = -ps * delta
            for h in range(N_Q_HEADS):
                dsk = dsk + jnp.where(head_row == h, jnp.sum(dsink[:, h * BLK:(h + 1) * BLK], axis=1, keepdims=True), 0.0)
        for b in range(bpt):
            dk_ref[b * BLK:(b + 1) * BLK, :] = dk_blocks[b]
            dv_ref[b * BLK:(b + 1) * BLK, :] = dv_blocks[b]
        dsk_ref[...] += dsk

    up = pl.BlockSpec((BLK, KV_W), lambda i: (i, 0))
    return _pcall(
        body, (dy, q, k, k, v, v, sinks), name=name, grid=(n_t,),
        in_specs=[tile(W_B), tile(W_B), tile(KV_W), prv, tile(KV_W), prv, sink],
        out_specs=[tile(W_B), tile(KV_W), tile(KV_W), up, up, pl.BlockSpec((N_Q_HEADS, BLK), lambda i: (0, 0))],
        out_shape=[SDS((s, W_B), F32), SDS((s, KV_W), F32), SDS((s, KV_W), F32), SDS((n_t * BLK, KV_W), F32),
                   SDS((n_t * BLK, KV_W), F32), SDS((N_Q_HEADS, BLK), F32)], carry=carry)


def _cc_recompute(glu_ref, glup_ref, cw_ref, cb_ref, first_tile):
    prev = jnp.where(first_tile, 0.0, glup_ref[...])
    ge = jnp.concatenate([prev, glu_ref[...]], axis=0)
    y0 = ge[:, :W_C] * _sigmoid_t(ge[:, W_C:])
    y1 = cb_ref[...]
    for j in range(CC_K):
        y1 = y1 + cw_ref[j:j + 1, :] * _shift_down(y0, CC_K - 1 - j)[CC_HALO:, :]
    return y0, y1


def _ln_stats(y1):
    mu = jnp.mean(y1, axis=-1, keepdims=True)
    xc = y1 - mu
    rstd = lax.rsqrt(jnp.mean(xc * xc, axis=-1, keepdims=True) + LN_EPS)
    return xc * rstd, rstd


def _cc_specs(s, ts):
    n32 = ts // CC_HALO
    row = lambda w: pl.BlockSpec((ts, w), lambda i: (i, 0))
    prev = pl.BlockSpec((CC_HALO, 2 * W_C), lambda i: (jnp.maximum(i * n32 - 1, 0), 0))
    small = lambda r: pl.BlockSpec((r, W_C), lambda i: (0, 0))
    return row, prev, small


def _cc_fwd(glu, cw, cb, lng, lnb, name, carry=()):
    s = glu.shape[0]
    ts = _time_tile(s)
    row, prev, small = _cc_specs(s, ts)

    def body(glu_ref, glup_ref, cw_ref, cb_ref, lng_ref, lnb_ref, y_ref):
        _, y1 = _cc_recompute(glu_ref, glup_ref, cw_ref, cb_ref, pl.program_id(0) == 0)
        xhat, _ = _ln_stats(y1)
        z = xhat * lng_ref[...] + lnb_ref[...]
        y_ref[...] = z * _sigmoid_t(z)

    return _pcall(
        body, (glu, glu, cw, cb, lng, lnb), name=name, grid=(s // ts,),
        in_specs=[row(2 * W_C), prev, small(CC_HALO), small(1), small(1), small(1)],
        out_specs=[row(W_C)], out_shape=[SDS((s, W_C), F32)], carry=carry)


def _cc_bwd_conv(dy, glu, cw, cb, lng, lnb, name, carry=()):
    s = glu.shape[0]
    ts = _time_tile(s)
    row, prev, small = _cc_specs(s, ts)

    def body(dy_ref, glu_ref, glup_ref, cw_ref, cb_ref, lng_ref, lnb_ref, dy1_ref, dcw_ref, dcb_ref, dlng_ref, dlnb_ref):
        i = pl.program_id(0)

        @pl.when(i == 0)
        def _():
            for ref in (dcw_ref, dcb_ref, dlng_ref, dlnb_ref):
                ref[...] = jnp.zeros_like(ref)

        y0, y1 = _cc_recompute(glu_ref, glup_ref, cw_ref, cb_ref, i == 0)
        xhat, rstd = _ln_stats(y1)
        z = xhat * lng_ref[...] + lnb_ref[...]
        dz = dy_ref[...] * _dsilu(z, _sigmoid_t(z))
        dlng_ref[...] += jnp.sum(dz * xhat, axis=0, keepdims=True)
        dlnb_ref[...] += jnp.sum(dz, axis=0, keepdims=True)
        dxh = dz * lng_ref[...]
        dy1 = rstd * (dxh - jnp.mean(dxh, axis=-1, keepdims=True) - xhat * jnp.mean(dxh * xhat, axis=-1, keepdims=True))
        dy1_ref[...] = dy1
        dcb_ref[...] += jnp.sum(dy1, axis=0, keepdims=True)
        for j in range(CC_K):
            dcw_ref[j:j + 1, :] += jnp.sum(dy1 * _shift_down(y0, CC_K - 1 - j)[CC_HALO:, :], axis=0, keepdims=True)

    return _pcall(
        body, (dy, glu, glu, cw, cb, lng, lnb), name=name, grid=(s // ts,),
        in_specs=[row(W_C), row(2 * W_C), prev, small(CC_HALO), small(1), small(1), small(1)],
        out_specs=[row(W_C), small(CC_HALO), small(1), small(1), small(1)],
        out_shape=[SDS((s, W_C), F32), SDS((CC_HALO, W_C), F32)] + [SDS((1, W_C), F32)] * 3, carry=carry)


def _cc_bwd_glu(dy1, glu, cw, name, carry=()):
    s = glu.shape[0]
    ts = _time_tile(s)
    n_t = s // ts
    n32 = ts // CC_HALO

    def body(dy1_ref, dyn_ref, glu_ref, cw_ref, dglu_ref):
        i = pl.program_id(0)
        dcat = jnp.concatenate([dy1_ref[...], jnp.where(i < n_t - 1, dyn_ref[...], 0.0)], axis=0)
        dy0 = jnp.zeros((ts, W_C), F32)
        for j in range(CC_K):
            dy0 = dy0 + cw_ref[j:j + 1, :] * _shift_up(dcat, CC_K - 1 - j)[:ts, :]
        a = glu_ref[:, :W_C]
        sg = _sigmoid_t(glu_ref[:, W_C:])
        dglu_ref[...] = jnp.concatenate([dy0 * sg, dy0 * a * sg * (1.0 - sg)], axis=1)

    row = lambda w: pl.BlockSpec((ts, w), lambda i: (i, 0))
    nxt = pl.BlockSpec((CC_HALO, W_C), lambda i: (jnp.minimum((i + 1) * n32, s // CC_HALO - 1), 0))
    return _pcall(
        body, (dy1, dy1, glu, cw), name=name, grid=(n_t,),
        in_specs=[row(W_C), nxt, row(2 * W_C), pl.BlockSpec((CC_HALO, W_C), lambda i: (0, 0))],
        out_specs=[row(2 * W_C)], out_shape=[SDS((s, 2 * W_C), F32)], carry=carry)


_MIX_OFFS = ((0, W_A), (W_A, W_A + W_B), (W_A + W_B, W_A + W_B + W_C))


def _mix_out_fwd(x, ya, yb, yc, group_g, w_out, post_g, name, carry=()):
    s, d = x.shape
    ts = 2 * _time_tile(s)
    dm = w_out.shape[0]

    def body(x_ref, ya_ref, yb_ref, yc_ref, gg_ref, w_ref, qg_ref, xo_ref, o_ref):
        parts = []
        for y_ref, (lo, hi) in zip((ya_ref, yb_ref, yc_ref), _MIX_OFFS):
            yv = y_ref[...]
            parts.append(yv * _rms_r(yv) * gg_ref[:, lo:hi])
        o = _mm(jnp.concatenate(parts, axis=1), w_ref[...])
        o_ref[...] = o
        xo_ref[...] = x_ref[...] + o * _rms_r(o) * qg_ref[...]

    row = lambda w: pl.BlockSpec((ts, w), lambda i: (i, 0))
    return _pcall(
        body, (x, ya, yb, yc, group_g, w_out, post_g), name=name, grid=(s // ts,),
        in_specs=[row(d), row(W_A), row(W_B), row(W_C), pl.BlockSpec((1, dm), lambda i: (0, 0)),
                  pl.BlockSpec((dm, d), lambda i: (0, 0)), pl.BlockSpec((1, d), lambda i: (0, 0))],
        out_specs=[row(d), row(d)], out_shape=[SDS((s, d), F32), SDS((s, d), F32)], carry=carry)


def _mix_out_bwd(dxo, o, ya, yb, yc, group_g, w_out, post_g, name, carry=()):
    s, d = o.shape
    ts = _time_tile(s)
    n_t = s // ts
    dm = w_out.shape[0]

    def body(dxo_ref, o_ref, ya_ref, yb_ref, yc_ref, gg_ref, w_ref, qg_ref,
             dya_ref, dyb_ref, dyc_ref, dw_ref, dqg_ref, dgg_ref, acc):
        i = pl.program_id(0)

        @pl.when(i == 0)
        def _():
            acc[...] = jnp.zeros_like(acc)
            dqg_ref[...] = jnp.zeros_like(dqg_ref)
            dgg_ref[...] = jnp.zeros_like(dgg_ref)

        ov = o_ref[...]
        do, dq = _rms_bwd(ov, _rms_r(ov), qg_ref[...], dxo_ref[...])
        dqg_ref[...] += dq
        do = do.astype(BF16)
        dyn = _mm_nt(do, w_ref[...])
        parts, dggs = [], []
        for y_ref, dy_ref, (lo, hi) in zip((ya_ref, yb_ref, yc_ref), (dya_ref, dyb_ref, dyc_ref), _MIX_OFFS):
            yv = y_ref[...]
            r = _rms_r(yv)
            gg = gg_ref[:, lo:hi]
            parts.append(yv * r * gg)
            dyv, dg = _rms_bwd(yv, r, gg, dyn[:, lo:hi])
            dy_ref[...] = dyv
            dggs.append(dg)
        dgg_ref[...] += jnp.concatenate(dggs, axis=1)
        acc[...] += _mm_tn(jnp.concatenate(parts, axis=1), do)

        @pl.when(i == n_t - 1)
        def _():
            dw_ref[...] = acc[...].astype(BF16)

    row = lambda w: pl.BlockSpec((ts, w), lambda i: (i, 0))
    full = pl.BlockSpec((dm, d), lambda i: (0, 0))
    return _pcall(
        body, (dxo, o, ya, yb, yc, group_g, w_out, post_g), name=name, grid=(n_t,),
        in_specs=[row(d), row(d), row(W_A), row(W_B), row(W_C), pl.BlockSpec((1, dm), lambda i: (0, 0)), full,
                  pl.BlockSpec((1, d), lambda i: (0, 0))],
        out_specs=[row(W_A), row(W_B), row(W_C), full, pl.BlockSpec((1, d), lambda i: (0, 0)),
                   pl.BlockSpec((1, dm), lambda i: (0, 0))],
        out_shape=[SDS((s, W_A), F32), SDS((s, W_B), F32), SDS((s, W_C), F32), SDS((dm, d), BF16),
                   SDS((1, d), F32), SDS((1, dm), F32)],
        scratch_shapes=[pltpu.VMEM((dm, d), F32)], carry=carry)


def _adamw_math(w, g, m, v):
    m = ADAM_B1 * m + (1.0 - ADAM_B1) * g
    v = ADAM_B2 * v + (1.0 - ADAM_B2) * (g * g)
    m_hat = m / (1.0 - ADAM_B1 ** ADAM_STEP)
    v_hat = v / (1.0 - ADAM_B2 ** ADAM_STEP)
    delta = -ADAM_LR * (m_hat / (jnp.sqrt(v_hat) + ADAM_EPS) + ADAM_WD * w)
    return delta, m, v


def _row_tile(rows, cap=512):
    best = None
    for t in range(16, min(rows, cap) + 1, 16):
        if rows % t == 0:
            best = t
    return best if best is not None else rows


def _reduce_adamw(recv, w, m, v, name):
    n_l, r, c = w.shape
    tr = _row_tile(r)

    def body(recv_ref, w_ref, m_ref, v_ref, g_ref, d_ref, nm_ref, nv_ref):
        g = recv_ref[0].astype(F32)
        for p in range(1, N_DEV):
            g = g + recv_ref[p].astype(F32)
        g_ref[...] = g
        d_ref[...], nm_ref[...], nv_ref[...] = _adamw_math(w_ref[...], g, m_ref[...], v_ref[...])

    blk = pl.BlockSpec((None, tr, c), lambda l, i: (l, i, 0))
    return _pcall(
        body, (recv, w, m, v), name=name, grid=(n_l, r // tr),
        in_specs=[pl.BlockSpec((N_DEV, None, tr, c), lambda l, i: (0, l, i, 0)), blk, blk, blk],
        out_specs=[blk] * 4, out_shape=[SDS(w.shape, F32)] * 4)[0]


def _reduce_adamw_small(parts, w, m, v, name):
    def body(p_ref, w_ref, m_ref, v_ref, g_ref, d_ref, nm_ref, nv_ref):
        g = p_ref[0]
        for p in range(1, N_DEV):
            g = g + p_ref[p]
        g_ref[...] = g
        d_ref[...], nm_ref[...], nv_ref[...] = _adamw_math(w_ref[...], g, m_ref[...], v_ref[...])

    vm = pl.BlockSpec(memory_space=pltpu.VMEM)
    return pl.pallas_call(body, name=name, in_specs=[vm] * 4, out_specs=[vm] * 4, out_shape=[SDS(w.shape, F32)] * 4,
                          compiler_params=pltpu.CompilerParams(vmem_limit_bytes=VMEM_LIMIT))(parts, w, m, v)


def _rows_of(shape):
    return -(-math.prod(shape) // (8 * BLK)) * 8


def _pack(arrs):
    rows = []
    for a in arrs:
        n, r = math.prod(a.shape), _rows_of(a.shape)
        if n % BLK == 0:
            part = a.reshape(n // BLK, BLK)
            rows.append(part if n // BLK == r else jnp.pad(part, ((0, r - n // BLK), (0, 0))))
        else:
            rows.append(jnp.pad(a.reshape(-1), (0, r * BLK - n)).reshape(r, BLK))
    return jnp.concatenate(rows, axis=0)


def _unpack(packed, shapes):
    out, row = [], 0
    for shp in shapes:
        n, r = math.prod(shp), _rows_of(shp)
        if n % BLK == 0:
            out.append(packed[row:row + n // BLK].reshape(shp))
        else:
            out.append(packed[row:row + r].reshape(-1)[:n].reshape(shp))
        row += r
    return out


def _block_diag(w):
    nb, bw, _ = w.shape
    eye = jnp.eye(nb, dtype=w.dtype)
    return (eye[:, None, :, None] * w[:, :, None, :]).reshape(nb * bw, nb * bw)


def _diag_blocks(wd, nb):
    bw = wd.shape[0] // nb
    return jnp.stack([wd[b * bw:(b + 1) * bw, b * bw:(b + 1) * bw] for b in range(nb)])


WEIGHT_NAMES = ['ffn1_pre_g', 'ffn1_w_gu', 'ffn1_w_down', 'ffn1_post_g', 'mix_pre_g', 'w_in', 'lru_conv_w', 'lru_conv_b',
                'lru_w_a', 'lru_b_a', 'lru_w_x', 'lru_b_x', 'lru_lambda', 'attn_sinks', 'conv_w', 'conv_b', 'conv_ln_g',
                'conv_ln_b', 'group_g', 'w_out', 'mix_post_g', 'ffn2_pre_g', 'ffn2_w_gu', 'ffn2_w_down', 'ffn2_post_g']
BIG = ('ffn1_w_gu', 'ffn1_w_down', 'w_in', 'w_out', 'ffn2_w_gu', 'ffn2_w_down')
TRANSPOSED = ('ffn1_w_gu', 'ffn2_w_gu', 'w_in')
SMALL = tuple(k for k in WEIGHT_NAMES if k not in BIG)
CHANNEL_SHARDED = ('lru_conv_w', 'conv_w')


def _step(x, target, w, m, v):
    n_l = w['ffn1_pre_g'].shape[0]
    assert n_l == 2, "the exchange schedule below is laid out for two layers"
    s, d = x.shape[1], x.shape[2]
    x = x.reshape(s, d)
    target = target.reshape(s, d)
    me = _my_pos()[3]
    tview = lambda t, k: jnp.swapaxes(t[k], 1, 2) if k in TRANSPOSED else t[k]
    wb = {k: tview(w, k).astype(BF16) for k in BIG}
    vec = lambda name, l: w[name][l][None, :]

    conv_shard = _pack([w['lru_conv_w'], w['conv_w']])
    g0 = _all_gather([(wb['ffn1_w_gu'], 0), (wb['ffn1_w_down'], 0), (wb['w_in'], 0), (wb['w_out'], 0),
                      (conv_shard, None)], "all_gather_first")
    wts = [dict(), dict()]
    wts[0]['ffn1_w_gu'], wts[0]['ffn1_w_down'], wts[0]['w_in'], wts[0]['w_out'], conv_g = g0
    ch = W_A // N_DEV
    conv_parts = [_unpack(conv_g[p], [(n_l, LRU_K, ch), (n_l, CC_K, ch)]) for p in range(N_DEV)]
    lru_cw = jnp.concatenate([cp[0] for cp in conv_parts], axis=-1)
    cc_cw = jnp.concatenate([cp[1] for cp in conv_parts], axis=-1)
    cc_cw = jnp.pad(cc_cw, ((0, 0), (0, CC_HALO - CC_K), (0, 0)))

    fc = wb['ffn1_w_gu'].shape[1]
    cut1, cut2 = (fc * 4 // 11 + 15) // 16 * 16, (fc * 27 // 44 + 15) // 16 * 16
    gather_plan = {
        ('ffn1', 0): [('A', 'f2_0', ('ffn2_w_gu', 'ffn2_w_down'), 0)],
        ('mix_in', 0): [('B', 'f2_0'), ('A', 'g1_1a', ('ffn1_w_gu',), 1, (0, cut1))],
        ('lru', 0): [('A', 'g1_1b', ('ffn1_w_gu',), 1, (cut1, cut2 - cut1), 'g1_1a')],
        ('attn', 0): [('A', 'g1_1', ('ffn1_w_gu',), 1, (cut2, fc - cut2), 'g1_1b')],
        ('mix_out', 0): [('B', 'g1_1')],
        ('ffn2', 0): [('D', None, ('ffn1_w_down',), 1), ('A', 'wi_1', ('w_in',), 1), ('A', 'wo_1', ('w_out',), 1)],
        ('ffn1', 1): [('A', 'f2_1', ('ffn2_w_gu', 'ffn2_w_down'), 1), ('B', 'wi_1'), ('B', 'wo_1')],
        ('mix_in', 1): [('B', 'f2_1')],
    }
    pend = {}

    def fwd(kernel_name, l, fn, *args):
        plan = gather_plan.get((kernel_name, l), [])
        carry = []
        for st in plan:
            if st[0] == 'B':
                carry.append(_gather_b(pend[st[1]][2]))
            else:
                rows = st[4] if len(st) > 4 else None
                into = pend.pop(st[5])[2] if len(st) > 5 else [None] * len(st[2])
                carry.append(_gather_a([(wb[k], st[3], rows, buf) for k, buf in zip(st[2], into)],
                                       two_level=st[0] == 'A'))
        outs, ex = fn(*args, f"{kernel_name}_fwd_l{l}", carry)
        for st, bufs in zip(plan, ex):
            if st[0] == 'A':
                pend[st[1]] = (st[2], st[3], bufs)
            else:
                names, wl = (st[2], st[3]) if st[0] == 'D' else pend.pop(st[1])[:2]
                for k, b in zip(names, bufs):
                    wts[wl][k] = b
        return outs

    saved = []
    h = x
    for l in range(n_l):
        sv = {'x0': h}
        lw = wts[l]
        x1, sv['h1'], sv['g1'], sv['u1'], sv['d1'] = fwd(
            'ffn1', l, _ffn_fwd, h, vec('ffn1_pre_g', l), vec('ffn1_post_g', l), lw['ffn1_w_gu'], lw['ffn1_w_down'])
        sv['x1'] = x1
        sv['hn'], lx, lg, q, k, vv, glu = fwd('mix_in', l, _mix_in_fwd, x1, vec('mix_pre_g', l),
                                              lw['w_in'].reshape(D_IN_PROJ, d))
        sv.update(lx=lx, lg=lg, q=q, k=k, v=vv, glu=glu)
        lru_p = (lru_cw[l], vec('lru_conv_b', l), _block_diag(w['lru_w_a'][l]).astype(BF16), vec('lru_b_a', l),
                 _block_diag(w['lru_w_x'][l]).astype(BF16), vec('lru_b_x', l), vec('lru_lambda', l))
        cc_p = (cc_cw[l], vec('conv_b', l), vec('conv_ln_g', l), vec('conv_ln_b', l))
        sv.update(lru_p=lru_p, cc_p=cc_p)
        sv['ya'], sv['hs'] = fwd('lru', l, _lru_fwd, lx, lg, lru_p)
        sv['sink_row'] = jnp.repeat(w['attn_sinks'][l], BLK)[None, :]
        (sv['yb'],) = fwd('attn', l, _attn_fwd, q, k, vv, sv['sink_row'])
        (sv['yc'],) = fwd('cconv', l, _cc_fwd, glu, *cc_p)
        x2, sv['o'] = fwd('mix_out', l, _mix_out_fwd, x1, sv['ya'], sv['yb'], sv['yc'], vec('group_g', l),
                          lw['w_out'].reshape(-1, d), vec('mix_post_g', l))
        sv['x2'] = x2
        h, sv['h2'], sv['g2'], sv['u2'], sv['d2'] = fwd(
            'ffn2', l, _ffn_fwd, x2, vec('ffn2_pre_g', l), vec('ffn2_post_g', l), lw['ffn2_w_gu'], lw['ffn2_w_down'])
        saved.append(sv)

    dh = h

    recv = {k: None for k in BIG}
    ready = {}
    small = [dict() for _ in range(n_l)]

    c_even, c_odd = tuple(range(0, N_DEV, 2)), tuple(range(1, N_DEV, 2))

    def exchange(keys):
        return _grad_x([(ready[key[:2]], key[1], recv[key[0]]) + tuple(key[2:]) for key in keys], n_l)

    def received(keys, bufs):
        for key, b in zip(keys, bufs):
            recv[key[0]] = b

    def run(fn, *args, keys=(), **kw):
        outs, ex = fn(*args, carry=[exchange(keys)] if keys else [], **kw)
        if keys:
            received(keys, ex[0])
        return outs

    for l in reversed(range(n_l)):
        sv, sg, lw = saved[l], small[l], wts[l]
        keys = [] if l == n_l - 1 else [('ffn1_w_gu', l + 1)]
        dx2, dd, dg, du, sg['ffn2_pre_g'], sg['ffn2_post_g'], *loss_rows = run(
            _ffn_bwd_act, dh, sv['d2'], sv['x2'], vec('ffn2_pre_g', l), vec('ffn2_post_g', l), sv['g2'], sv['u2'],
            lw['ffn2_w_gu'], lw['ffn2_w_down'], f"ffn2_bwd_act_l{l}", keys=keys,
            loss_target=target if l == n_l - 1 else None)
        if loss_rows:
            loss_row = loss_rows[0]
        keys = [] if l == n_l - 1 else [('ffn1_w_down', l + 1), ('w_in', l + 1, c_odd)]
        dwg, dwu, dwd = run(_ffn_bwd_w, sv['h2'], dd, sv['g2'], sv['u2'], dg, du, f"ffn2_bwd_w_l{l}", keys=keys)
        ready[('ffn2_w_gu', l)] = [dwg, dwu]
        ready[('ffn2_w_down', l)] = [dwd.reshape(N_DEV, -1, d)]
        dya, dyb, dyc, dw_out, sg['mix_post_g'], sg['group_g'] = run(
            _mix_out_bwd, dx2, sv['o'], sv['ya'], sv['yb'], sv['yc'], vec('group_g', l), lw['w_out'].reshape(-1, d),
            vec('mix_post_g', l), f"mix_out_bwd_l{l}")
        ready[('w_out', l)] = [dw_out.reshape(N_DEV, -1, d)]
        (dlx, dlg, sg['lru_conv_w'], sg['lru_conv_b'], dwa, sg['lru_b_a'], dwx, sg['lru_b_x'],
         sg['lru_lambda']) = run(_lru_bwd, dya, sv['lx'], sv['lg'], sv['hs'], sv['lru_p'], f"lru_bwd_l{l}")
        sg['lru_w_a'] = _diag_blocks(dwa, A_BLOCKS)
        sg['lru_w_x'] = _diag_blocks(dwx, A_BLOCKS)
        dq, dk, dv, dk_up, dv_up, dsk = run(_attn_bwd, dyb, sv['q'], sv['k'], sv['v'], sv['sink_row'],
                                            f"attn_bwd_l{l}", keys=[('ffn2_w_down', l, c_even)] if l == 0 else [])
        sg['attn_sinks'] = dsk[:, 0]
        dy1, dcw, sg['conv_b'], sg['conv_ln_g'], sg['conv_ln_b'] = run(
            _cc_bwd_conv, dyc, sv['glu'], *sv['cc_p'], f"cconv_bwd_conv_l{l}", keys=[('w_out', l)] if l == 0 else [])
        sg['conv_w'] = dcw[:CC_K]
        (dglu,) = run(_cc_bwd_glu, dy1, sv['glu'], sv['cc_p'][0], f"cconv_bwd_glu_l{l}")
        dx1, dw_in, sg['mix_pre_g'] = run(
            _mix_in_bwd, dx2, sv['x1'], vec('mix_pre_g', l), sv['hn'], lw['w_in'].reshape(D_IN_PROJ, d),
            dlx, dlg, dq, dk, dk_up, dv, dv_up, dglu, f"mix_in_bwd_l{l}",
            keys=[('ffn2_w_down', l, c_odd)] if l == 0 else [('w_out', l)])
        ready[('w_in', l)] = [dw_in.reshape(N_DEV, -1, d)]
        dh, dd, dg, du, sg['ffn1_pre_g'], sg['ffn1_post_g'] = run(
            _ffn_bwd_act, dx1, sv['d1'], sv['x0'], vec('ffn1_pre_g', l), vec('ffn1_post_g', l), sv['g1'], sv['u1'],
            lw['ffn1_w_gu'], lw['ffn1_w_down'], f"ffn1_bwd_act_l{l}",
            keys=[('ffn2_w_gu', l), ('w_in', l)] if l == 0 else [('ffn2_w_gu', l)])
        if l > 0:
            dwg, dwu, dwd = run(_ffn_bwd_w, sv['h1'], dd, sv['g1'], sv['u1'], dg, du, f"ffn1_bwd_w_l{l}",
                                keys=[('ffn2_w_down', l), ('w_in', l, c_even)])
            ready[('ffn1_w_gu', l)] = [dwg, dwu]
            ready[('ffn1_w_down', l)] = [dwd.reshape(N_DEV, -1, d)]
        else:
            part = _pack([jnp.stack([small[j][k] for j in range(n_l)]) for k in SMALL] + [loss_row])
            (recv['ffn1_w_gu'], recv['ffn1_w_down']), ex = _ffn_bwd_w_send(
                sv['h1'], dd, sv['g1'], sv['u1'], dg, du, recv['ffn1_w_gu'], recv['ffn1_w_down'], 0, "ffn1_bwd_w_send_l0",
                [_gather_a([(part, None)], two_level=False)])
            small_parts = ex[0][0]
    grad_x = dh.reshape(1, s, d)

    out = {}
    for k in BIG:
        res = _reduce_adamw(recv[k], tview(w, k), tview(m, k), tview(v, k), f"reduce_adamw_{k}")
        out[k] = [jnp.swapaxes(r, 1, 2) for r in res] if k in TRANSPOSED else res

    small_shapes = [(n_l,) + tuple(small[0][k].shape) for k in SMALL]

    def widen(t, k):
        if k not in CHANNEL_SHARDED:
            return t.reshape((n_l,) + tuple(small[0][k].shape))
        full = jnp.zeros((n_l,) + tuple(small[0][k].shape), F32)
        return lax.dynamic_update_slice_in_dim(full, t, me * ch, axis=2)

    no_w = jnp.zeros(loss_row.shape, F32)
    packed = [_pack([widen(src[k], k) for k in SMALL] + [no_w]) for src in (w, m, v)]
    res = _reduce_adamw_small(small_parts, *packed, "reduce_adamw_small")
    loss = _unpack(res[0], small_shapes + [loss_row.shape])[-1][0, 0]
    for k, g, dlt, nm, nv in zip(SMALL, *[_unpack(r, small_shapes) for r in res]):
        vals = [g, dlt, nm, nv]
        if k in CHANNEL_SHARDED:
            vals = [lax.dynamic_slice_in_dim(t, me * ch, ch, axis=2) for t in vals]
        out[k] = [t.reshape(w[k].shape) for t in vals]

    return (loss, grad_x, *[out[k][0] for k in WEIGHT_NAMES], *[out[k][1] for k in WEIGHT_NAMES],
            *[out[k][2] for k in WEIGHT_NAMES], *[out[k][3] for k in WEIGHT_NAMES])


def kernel(x, ffn1_pre_g, ffn1_w_gu, ffn1_w_down, ffn1_post_g, mix_pre_g, w_in, lru_conv_w, lru_conv_b, lru_w_a, lru_b_a, lru_w_x, lru_b_x, lru_lambda, attn_sinks, conv_w, conv_b, conv_ln_g, conv_ln_b, group_g, w_out, mix_post_g, ffn2_pre_g, ffn2_w_gu, ffn2_w_down, ffn2_post_g, loss_target, m_ffn1_pre_g, m_ffn1_w_gu, m_ffn1_w_down, m_ffn1_post_g, m_mix_pre_g, m_w_in, m_lru_conv_w, m_lru_conv_b, m_lru_w_a, m_lru_b_a, m_lru_w_x, m_lru_b_x, m_lru_lambda, m_attn_sinks, m_conv_w, m_conv_b, m_conv_ln_g, m_conv_ln_b, m_group_g, m_w_out, m_mix_post_g, m_ffn2_pre_g, m_ffn2_w_gu, m_ffn2_w_down, m_ffn2_post_g, v_ffn1_pre_g, v_ffn1_w_gu, v_ffn1_w_down, v_ffn1_post_g, v_mix_pre_g, v_w_in, v_lru_conv_w, v_lru_conv_b, v_lru_w_a, v_lru_b_a, v_lru_w_x, v_lru_b_x, v_lru_lambda, v_attn_sinks, v_conv_w, v_conv_b, v_conv_ln_g, v_conv_ln_b, v_group_g, v_w_out, v_mix_post_g, v_ffn2_pre_g, v_ffn2_w_gu, v_ffn2_w_down, v_ffn2_post_g):
    args = (ffn1_pre_g, ffn1_w_gu, ffn1_w_down, ffn1_post_g, mix_pre_g, w_in, lru_conv_w, lru_conv_b, lru_w_a, lru_b_a, lru_w_x, lru_b_x, lru_lambda, attn_sinks, conv_w, conv_b, conv_ln_g, conv_ln_b, group_g, w_out, mix_post_g, ffn2_pre_g, ffn2_w_gu, ffn2_w_down, ffn2_post_g)
    ms = (m_ffn1_pre_g, m_ffn1_w_gu, m_ffn1_w_down, m_ffn1_post_g, m_mix_pre_g, m_w_in, m_lru_conv_w, m_lru_conv_b, m_lru_w_a, m_lru_b_a, m_lru_w_x, m_lru_b_x, m_lru_lambda, m_attn_sinks, m_conv_w, m_conv_b, m_conv_ln_g, m_conv_ln_b, m_group_g, m_w_out, m_mix_post_g, m_ffn2_pre_g, m_ffn2_w_gu, m_ffn2_w_down, m_ffn2_post_g)
    vs = (v_ffn1_pre_g, v_ffn1_w_gu, v_ffn1_w_down, v_ffn1_post_g, v_mix_pre_g, v_w_in, v_lru_conv_w, v_lru_conv_b, v_lru_w_a, v_lru_b_a, v_lru_w_x, v_lru_b_x, v_lru_lambda, v_attn_sinks, v_conv_w, v_conv_b, v_conv_ln_g, v_conv_ln_b, v_group_g, v_w_out, v_mix_post_g, v_ffn2_pre_g, v_ffn2_w_gu, v_ffn2_w_down, v_ffn2_post_g)
    return _step(x, loss_target, dict(zip(WEIGHT_NAMES, args)), dict(zip(WEIGHT_NAMES, ms)), dict(zip(WEIGHT_NAMES, vs)))
```

```python
import functools
import math
import operator

import jax
import jax.numpy as jnp
from jax import lax
from jax.experimental import pallas as pl
from jax.experimental.pallas import tpu as pltpu

F32 = jnp.float32
BF16 = jnp.bfloat16
N_DEV = 8
AXES = ("x", "y", "c")
MESH = pl.DeviceIdType.MESH

NORM_EPS = 1e-6
LN_EPS = 1e-5
NEG_BIG = -1e30
W_A = 256
W_B = 512
W_C = 256
HEAD_DIM = 64
N_Q_HEADS = 8
N_KV_HEADS = 2
Q_PER_KV = N_Q_HEADS // N_KV_HEADS
KV_W = N_KV_HEADS * HEAD_DIM
BLK = 128
LRU_K = 4
LRU_C = 8.0
A_BLOCKS = 4
CC_K = 31
CC_HALO = 32
LRU_HALO = 8
D_IN_PROJ = 2 * W_A + W_B + 2 * KV_W + 2 * W_C
ADAM_LR = 0.001
ADAM_B1 = 0.9
ADAM_B2 = 0.999
ADAM_EPS = 1e-08
ADAM_WD = 0.01
ADAM_STEP = 10
VMEM_LIMIT = 56 * 1024 * 1024

SDS = jax.ShapeDtypeStruct
ANY = pl.BlockSpec(memory_space=pl.ANY)


def _time_tile(s):
    return max(BLK, s // 8)


class _Exchange:
    def __init__(self, inputs, out_shapes, aliases, sem_shapes, start, wait):
        self.inputs, self.out_shapes, self.aliases, self.sem_shapes = inputs, out_shapes, aliases, sem_shapes
        self.start, self.wait = start, wait


def _my_pos():
    x, y, c = (lax.axis_index(a) for a in AXES)
    return x, y, c, 4 * x + 2 * y + c


def _flip(k):
    x, y, c, _ = _my_pos()
    return (1 - x if k & 4 else x, 1 - y if k & 2 else y, 1 - c if k & 1 else c)


def _slot(dev):
    return 4 * dev[0] + 2 * dev[1] + dev[2]


def _dev(p):
    return (p >> 2, (p >> 1) & 1, p & 1)


def _gather_a(items, two_level):
    rels = (1, 2, 4, 6) if two_level else tuple(range(1, N_DEV))
    items = [tuple(it) + (None,) * (4 - len(it)) for it in items]
    n = len(items)
    with_buf = [a for a in range(n) if items[a][3] is not None]

    def rows_of(ref, a):
        return ref if items[a][2] is None else ref.at[pl.ds(*items[a][2])]

    def src_of(ins, a):
        return rows_of(ins[a] if items[a][1] is None else ins[a].at[items[a][1]], a)

    def dst_of(outs, a, slot):
        return rows_of(outs[a].at[slot], a)

    def shape_of(a):
        arr, l = items[a][:2]
        return arr.shape if l is None else arr.shape[1:]

    def copies(ins, outs, sems, a):
        send, recv, _ = sems
        me = _my_pos()[3]
        return [(k, pltpu.make_async_remote_copy(
            src_ref=src_of(ins, a), dst_ref=dst_of(outs, a, me), send_sem=send.at[a, k], recv_sem=recv.at[a, k],
            device_id=_flip(k), device_id_type=MESH)) for k in rels]

    def local(ins, outs, sems, a):
        return pltpu.make_async_copy(src_of(ins, a), dst_of(outs, a, _my_pos()[3]), sems[2].at[a])

    def start(ins, outs, sems):
        for a in range(n):
            local(ins, outs, sems, a).start()
            for _, cp in copies(ins, outs, sems, a):
                cp.start()

    def wait(ins, outs, sems):
        send, recv, _ = sems
        for a in range(n):
            for k, cp in copies(ins, outs, sems, a):
                pltpu.make_async_remote_copy(
                    src_ref=src_of(ins, a), dst_ref=dst_of(outs, a, _slot(_flip(k))), send_sem=send.at[a, k],
                    recv_sem=recv.at[a, k], device_id=_flip(k), device_id_type=MESH).wait_recv()
                cp.wait_send()
            local(ins, outs, sems, a).wait()

    return _Exchange([it[0] for it in items] + [items[a][3] for a in with_buf],
                     [SDS((N_DEV,) + shape_of(a), items[a][0].dtype) for a in range(n)],
                     {n + j: a for j, a in enumerate(with_buf)},
                     [pltpu.SemaphoreType.DMA((n, N_DEV)), pltpu.SemaphoreType.DMA((n, N_DEV)),
                      pltpu.SemaphoreType.DMA((n,))], start, wait)


def _gather_b(bufs):
    n = len(bufs)

    def copies(ins, outs, sems, a, c_of_block):
        send, recv = sems
        x, y, c, _ = _my_pos()
        res = []
        for k in (2, 4, 6):
            chip = _flip(k)
            blk = _slot((chip[0], chip[1], c if c_of_block == "mine" else 1 - c))
            res.append(pltpu.make_async_remote_copy(
                src_ref=ins[a].at[blk], dst_ref=outs[a].at[blk], send_sem=send.at[a, k], recv_sem=recv.at[a, k],
                device_id=_flip(1), device_id_type=MESH))
        return res

    def start(ins, outs, sems):
        for a in range(n):
            for cp in copies(ins, outs, sems, a, "mine"):
                cp.start()

    def wait(ins, outs, sems):
        for a in range(n):
            for cp in copies(ins, outs, sems, a, "sibling"):
                cp.wait_recv()
            for cp in copies(ins, outs, sems, a, "mine"):
                cp.wait_send()

    return _Exchange(list(bufs), [SDS(b.shape, b.dtype) for b in bufs], {a: a for a in range(n)},
                     [pltpu.SemaphoreType.DMA((n, N_DEV)), pltpu.SemaphoreType.DMA((n, N_DEV))], start, wait)


def _grad_x(items, n_l):
    items = [tuple(it) + (None,) * (4 - len(it)) for it in items]
    n = len(items)
    owners = [tuple(range(N_DEV)) if it[3] is None else tuple(it[3]) for it in items]
    inputs, first_in, aliases, out_shapes = [], [], {}, []
    for a, (arrs, l, recv, _) in enumerate(items):
        first_in.append(len(inputs))
        inputs += list(arrs)
        assert sum(arr.shape[0] for arr in arrs) == N_DEV
        if recv is not None:
            aliases[len(inputs)] = a
            inputs.append(recv)
        out_shapes.append(SDS((N_DEV, n_l) + arrs[0].shape[1:], arrs[0].dtype))

    def slab(ins, a, p):
        off = 0
        for j, arr in enumerate(items[a][0]):
            if p < off + arr.shape[0]:
                return ins[first_in[a] + j].at[p - off]
            off += arr.shape[0]
        raise AssertionError

    def rdma(ins, outs, sems, a, p, src_dev):
        send, recv, _ = sems
        return pltpu.make_async_remote_copy(
            src_ref=slab(ins, a, p), dst_ref=outs[a].at[src_dev, items[a][1]], send_sem=send.at[a, p],
            recv_sem=recv.at[a, src_dev], device_id=_dev(p), device_id_type=MESH)

    def local(ins, outs, sems, a, p):
        return pltpu.make_async_copy(slab(ins, a, p), outs[a].at[p, items[a][1]], sems[2].at[a])

    def start(ins, outs, sems):
        me = _my_pos()[3]
        for p in range(N_DEV):
            mine = [a for a in range(n) if p in owners[a]]

            @pl.when(me != p)
            def _():
                for a in mine:
                    rdma(ins, outs, sems, a, p, me).start()

            @pl.when(me == p)
            def _():
                for a in mine:
                    local(ins, outs, sems, a, p).start()

    def wait(ins, outs, sems):
        me = _my_pos()[3]
        for a in range(n):
            i_own = functools.reduce(operator.or_, [me == q for q in owners[a]])
            for p in range(N_DEV):
                @pl.when((me != p) & i_own)
                def _():
                    rdma(ins, outs, sems, a, p, p).wait_recv()

                if p in owners[a]:
                    @pl.when(me != p)
                    def _():
                        rdma(ins, outs, sems, a, p, p).wait_send()

                    @pl.when(me == p)
                    def _():
                        local(ins, outs, sems, a, p).wait()

    return _Exchange(inputs, out_shapes, aliases,
                     [pltpu.SemaphoreType.DMA((n, N_DEV)), pltpu.SemaphoreType.DMA((n, N_DEV)),
                      pltpu.SemaphoreType.DMA((n,))], start, wait)


def _pcall(body, args, *, name, grid, in_specs, out_specs, out_shape, scratch_shapes=(), carry=(), body_aliases=None):
    n_in, n_out, n_scr = len(in_specs), len(out_specs), len(scratch_shapes)
    c_in = [len(e.inputs) for e in carry]
    c_out = [len(e.out_shapes) for e in carry]
    c_sem = [len(e.sem_shapes) for e in carry]
    aliases = dict(body_aliases or {})
    for j, e in enumerate(carry):
        for i_loc, o_loc in e.aliases.items():
            aliases[n_in + sum(c_in[:j]) + i_loc] = n_out + sum(c_out[:j]) + o_loc

    def wrapped(*refs):
        def take(counts, pos):
            groups = []
            for cnt in counts:
                groups.append(refs[pos:pos + cnt])
                pos += cnt
            return groups, pos

        (ins,), pos = take([n_in], 0)
        cins, pos = take(c_in, pos)
        (outs,), pos = take([n_out], pos)
        couts, pos = take(c_out, pos)
        (scr,), pos = take([n_scr], pos)
        csems, pos = take(c_sem, pos)
        if carry:
            ids = [pl.program_id(k) for k in range(len(grid))]
            first = functools.reduce(operator.and_, [i == 0 for i in ids])
            last = functools.reduce(operator.and_, [i == g - 1 for i, g in zip(ids, grid)])

            @pl.when(first)
            def _():
                for e, ci, co, cs in zip(carry, cins, couts, csems):
                    e.start(ci, co, cs)

        body(*ins, *outs, *scr)
        if carry:
            @pl.when(last)
            def _():
                for e, ci, co, cs in zip(carry, cins, couts, csems):
                    e.wait(ci, co, cs)

    res = pl.pallas_call(
        wrapped, name=name, grid=grid,
        in_specs=list(in_specs) + [ANY] * sum(c_in),
        out_specs=list(out_specs) + [ANY] * sum(c_out),
        out_shape=list(out_shape) + [s for e in carry for s in e.out_shapes],
        scratch_shapes=list(scratch_shapes) + [s for e in carry for s in e.sem_shapes],
        input_output_aliases=aliases,
        compiler_params=pltpu.CompilerParams(dimension_semantics=("arbitrary",) * len(grid),
                                             vmem_limit_bytes=VMEM_LIMIT),
    )(*args, *[a for e in carry for a in e.inputs])
    outs, pos, extra = list(res[:n_out]), n_out, []
    for cnt in c_out:
        extra.append(list(res[pos:pos + cnt]))
        pos += cnt
    return outs, extra


def _all_gather(items, name):
    n = len(items)
    shape_of = lambda a: items[a][0].shape if items[a][1] is None else items[a][0].shape[1:]

    def body(*refs):
        ins, outs, (send_sems, recv_sems, local_sems) = refs[:n], refs[n:2 * n], refs[2 * n:]
        x, y, c, me = _my_pos()
        src_of = lambda a: ins[a] if items[a][1] is None else ins[a].at[items[a][1]]

        def copy(a, k, block, to, src=None):
            dst = outs[a].at[_slot(block)]
            return pltpu.make_async_remote_copy(
                src_ref=dst if src is None else src, dst_ref=dst,
                send_sem=send_sems.at[a, k], recv_sem=recv_sems.at[a, k], device_id=to, device_id_type=MESH)

        mine = [pltpu.make_async_copy(src_of(a), outs[a].at[me], local_sems.at[a]) for a in range(n)]
        for cp in mine:
            cp.start()
        first = [copy(a, k, (x, y, c), _flip(k), src=src_of(a)) for a in range(n) for k in (1, 2, 4, 6)]
        for cp in first:
            cp.start()
        passed = []
        for k in (2, 4, 6):
            for a in range(n):
                copy(a, k, _flip(k), (x, y, c)).wait_recv()
                fwd = copy(a, k + 1, _flip(k), _flip(1))
                fwd.start()
                passed.append(fwd)
        for a in range(n):
            copy(a, 1, _flip(1), (x, y, c)).wait_recv()
            for k in (2, 4, 6):
                copy(a, k + 1, _flip(k + 1), (x, y, c)).wait_recv()
        for cp in first + passed:
            cp.wait_send()
        for cp in mine:
            cp.wait()

    return pl.pallas_call(
        body, name=name,
        in_specs=[ANY] * n, out_specs=[ANY] * n,
        out_shape=[SDS((N_DEV,) + shape_of(a), items[a][0].dtype) for a in range(n)],
        scratch_shapes=[pltpu.SemaphoreType.DMA((n, N_DEV)), pltpu.SemaphoreType.DMA((n, N_DEV)),
                        pltpu.SemaphoreType.DMA((n,))],
    )(*[it[0] for it in items])


def _mm(a, b):
    return jnp.dot(a.astype(BF16), b.astype(BF16), preferred_element_type=F32)


def _mm_nt(a, b):
    return lax.dot_general(a.astype(BF16), b.astype(BF16), (((1,), (1,)), ((), ())), preferred_element_type=F32)


def _mm_tn(a, b):
    return lax.dot_general(a.astype(BF16), b.astype(BF16), (((0,), (0,)), ((), ())), preferred_element_type=F32)


def _rms_r(x):
    return lax.rsqrt(jnp.mean(x * x, axis=-1, keepdims=True) + NORM_EPS)


def _rms_bwd(x, r, g, dy):
    gy = dy * g
    dx = r * (gy - x * (r * r) * jnp.mean(gy * x, axis=-1, keepdims=True))
    dg = jnp.sum(dy * x * r, axis=0, keepdims=True)
    return dx, dg


def _sigmoid(x):
    return 1.0 / (1.0 + jnp.exp(-x))


def _sigmoid_t(x):
    return 0.5 * jnp.tanh(0.5 * x) + 0.5


def _dsilu(z, sz):
    return sz * (1.0 + z * (1.0 - sz))


def _swiglu_bf16(g, u):
    sg = 0.5 * jnp.tanh(0.5 * g) + 0.5
    silu = g * sg
    return silu * u, silu, sg + silu * (1.0 - sg)


_GELU_C = math.sqrt(2.0 / math.pi)


def _gelu(x):
    t = jnp.tanh(_GELU_C * (x + 0.044715 * x * x * x))
    return 0.5 * x * (1.0 + t), t


def _dgelu(x, t):
    return 0.5 * (1.0 + t) + 0.5 * x * (1.0 - t * t) * _GELU_C * (1.0 + 3.0 * 0.044715 * x * x)


def _log1p(e):
    return jnp.where(e < 1e-2, e * (1.0 - e * (0.5 - e * (1.0 / 3.0))), jnp.log(1.0 + e))


def _softplus(x):
    return jnp.maximum(x, 0.0) + _log1p(jnp.exp(-jnp.abs(x)))


def _neg_expm1(x):
    small = -x * (1.0 + x * (0.5 + x * (1.0 / 6.0) * (1.0 + x * 0.25)))
    return jnp.where(x > -1e-2, small, 1.0 - jnp.exp(x))


def _shift_down(x, s):
    return x if s == 0 else pltpu.roll(x, s, 0)


def _shift_up(x, s):
    return x if s == 0 else pltpu.roll(x, x.shape[0] - s, 0)


def _ffn_wspecs(d, fc, order):
    f_of = (lambda i, f: f) if order == "tf" else (lambda f, i: f)
    n_f = N_DEV // 2
    return [pl.BlockSpec((None, fc, d), lambda *g: (f_of(*g), 0, 0)),
            pl.BlockSpec((None, fc, d), lambda *g: (f_of(*g) + n_f, 0, 0)),
            pl.BlockSpec((2, fc // 2, d), lambda *g: (f_of(*g), 0, 0))]


def _ffn_fwd(x, pre_g, post_g, wgu_t, wd, name, carry=()):
    s, d = x.shape
    fc = wgu_t.shape[1]
    ts = 2 * _time_tile(s)
    n_t, n_f = s // ts, N_DEV // 2

    def body(x_ref, pg_ref, qg_ref, wg_ref, wu_ref, wd_ref, xo_ref, h_ref, g_ref, u_ref, d_ref, h_scr, acc):
        f = pl.program_id(1)

        @pl.when(f == 0)
        def _():
            xv = x_ref[...]
            hv = (xv * _rms_r(xv) * pg_ref[...]).astype(BF16)
            h_scr[...] = hv
            h_ref[...] = hv
            acc[...] = jnp.zeros_like(acc)

        hv = h_scr[...]
        g = _mm_nt(hv, wg_ref[...])
        u = _mm_nt(hv, wu_ref[...])
        g = g.astype(BF16)
        u = u.astype(BF16)
        g_ref[...] = g
        u_ref[...] = u
        acc[...] += jnp.dot(_swiglu_bf16(g, u)[0], wd_ref[...].reshape(fc, d), preferred_element_type=F32)

        @pl.when(f == n_f - 1)
        def _():
            dv = acc[...]
            d_ref[...] = dv.astype(BF16)
            xo_ref[...] = x_ref[...] + 0.5 * (dv * _rms_r(dv) * qg_ref[...])

    row = pl.BlockSpec((ts, d), lambda i, f: (i, 0))
    vec = pl.BlockSpec((1, d), lambda i, f: (0, 0))
    act = pl.BlockSpec((None, ts, fc), lambda i, f: (f, i, 0))
    return _pcall(
        body, (x, pre_g, post_g, wgu_t, wgu_t, wd), name=name, grid=(n_t, n_f),
        in_specs=[row, vec, vec] + _ffn_wspecs(d, fc, "tf"),
        out_specs=[row, row, act, act, row],
        out_shape=[SDS((s, d), F32), SDS((s, d), BF16), SDS((n_f, s, fc), BF16), SDS((n_f, s, fc), BF16),
                   SDS((s, d), BF16)],
        scratch_shapes=[pltpu.VMEM((ts, d), BF16), pltpu.VMEM((ts, d), F32)], carry=carry)


def _ffn_bwd_act(dxo, dmid, x, pre_g, post_g, g_s, u_s, wgu_t, wd, name, carry=(), loss_target=None):
    s, d = x.shape
    fc = wgu_t.shape[1]
    ts = _time_tile(s)
    n_t, n_f = s // ts, N_DEV // 2
    with_loss = loss_target is not None

    def body(*refs):
        t_ref, refs = (refs[0], refs[1:]) if with_loss else (None, refs)
        (dxo_ref, dm_ref, x_ref, pg_ref, qg_ref, g_ref, u_ref, wg_ref, wu_ref, wd_ref,
         dx_ref, dd_ref, dg_ref, du_ref, dpg_ref, dqg_ref) = refs[:16]
        loss_ref = refs[16] if with_loss else None
        dd_scr, dh_acc = refs[-2:]
        i, f = pl.program_id(0), pl.program_id(1)

        def incoming():
            return (dxo_ref[...] - t_ref[...]) * (1.0 / d) if with_loss else dxo_ref[...]

        @pl.when((i == 0) & (f == 0))
        def _():
            dpg_ref[...] = jnp.zeros_like(dpg_ref)
            dqg_ref[...] = jnp.zeros_like(dqg_ref)
            if with_loss:
                loss_ref[...] = jnp.zeros_like(loss_ref)

        @pl.when(f == 0)
        def _():
            if with_loss:
                err = dxo_ref[...] - t_ref[...]
                loss_ref[...] += 0.5 * jnp.sum(jnp.mean(err * err, axis=-1, keepdims=True), axis=0, keepdims=True)
            dv = dm_ref[...].astype(F32)
            ddv, dq = _rms_bwd(dv, _rms_r(dv), qg_ref[...], 0.5 * incoming())
            dqg_ref[...] += dq
            dd_scr[...] = ddv.astype(BF16)
            dd_ref[...] = ddv.astype(BF16)
            dh_acc[...] = jnp.zeros_like(dh_acc)

        da = _mm_nt(dd_scr[...], wd_ref[...].reshape(fc, d)).astype(BF16)
        u = u_ref[...]
        _, silu, dsilu = _swiglu_bf16(g_ref[...], u)
        du = da * silu
        dg = da * u * dsilu
        dg_ref[...] = dg
        du_ref[...] = du
        dh_acc[...] += _mm(dg, wg_ref[...]) + _mm(du, wu_ref[...])

        @pl.when(f == n_f - 1)
        def _():
            xv = x_ref[...]
            dxv, dp = _rms_bwd(xv, _rms_r(xv), pg_ref[...], dh_acc[...])
            dpg_ref[...] += dp
            dx_ref[...] = incoming() + dxv

    row = pl.BlockSpec((ts, d), lambda i, f: (i, 0))
    vec = pl.BlockSpec((1, d), lambda i, f: (0, 0))
    act = pl.BlockSpec((None, ts, fc), lambda i, f: (f, i, 0))
    lead = [loss_target] if with_loss else []
    return _pcall(
        body, (*lead, dxo, dmid, x, pre_g, post_g, g_s, u_s, wgu_t, wgu_t, wd), name=name, grid=(n_t, n_f),
        in_specs=[row] * len(lead) + [row, row, row, vec, vec, act, act] + _ffn_wspecs(d, fc, "tf"),
        out_specs=[row, row, act, act, vec, vec] + [pl.BlockSpec((1, BLK), lambda i, f: (0, 0))] * len(lead),
        out_shape=[SDS((s, d), F32), SDS((s, d), BF16), SDS((n_f, s, fc), BF16), SDS((n_f, s, fc), BF16),
                   SDS((1, d), F32), SDS((1, d), F32)] + [SDS((1, BLK), F32)] * len(lead),
        scratch_shapes=[pltpu.VMEM((ts, d), BF16), pltpu.VMEM((ts, d), F32)], carry=carry)


def _ffn_bwd_w(h, dd, g_s, u_s, dg, du, name, carry=()):
    s, d = h.shape
    n_f, _, fc = g_s.shape
    ts = 2 * _time_tile(s)
    n_t = s // ts

    def body(h_ref, dd_ref, g_ref, u_ref, dg_ref, du_ref, wg_ref, wu_ref, wd_ref, acc_g, acc_u, acc_d):
        i = pl.program_id(1)

        @pl.when(i == 0)
        def _():
            acc_g[...] = jnp.zeros_like(acc_g)
            acc_u[...] = jnp.zeros_like(acc_u)
            acc_d[...] = jnp.zeros_like(acc_d)

        a = _swiglu_bf16(g_ref[...], u_ref[...])[0]
        hv = h_ref[...]
        acc_g[...] += _mm_tn(dg_ref[...], hv)
        acc_u[...] += _mm_tn(du_ref[...], hv)
        acc_d[...] += _mm_tn(a, dd_ref[...])

        @pl.when(i == n_t - 1)
        def _():
            wg_ref[...] = acc_g[...].astype(BF16)
            wu_ref[...] = acc_u[...].astype(BF16)
            wd_ref[...] = acc_d[...].astype(BF16)

    row = pl.BlockSpec((ts, d), lambda f, i: (i, 0))
    act = pl.BlockSpec((None, ts, fc), lambda f, i: (f, i, 0))
    out = pl.BlockSpec((None, fc, d), lambda f, i: (f, 0, 0))
    return _pcall(
        body, (h, dd, g_s, u_s, dg, du), name=name, grid=(n_f, n_t),
        in_specs=[row, row, act, act, act, act], out_specs=[out, out, out],
        out_shape=[SDS((n_f, fc, d), BF16)] * 3,
        scratch_shapes=[pltpu.VMEM((fc, d), F32)] * 3, carry=carry)


def _ffn_bwd_w_send(h, dd, g_s, u_s, dg, du, recv_gu, recv_d, layer, name, carry=()):
    s, d = h.shape
    n_f, _, fc = g_s.shape
    ts = _time_tile(s)
    n_t = s // ts
    half = fc // 2

    def chunk_of(step):
        return (step + 2 * lax.axis_index("x") + lax.axis_index("y")) % n_f

    def body(h_ref, dd_ref, g_ref, u_ref, dg_ref, du_ref, _rgu_in, _rd_in, rgu_ref, rd_ref,
             acc_g, acc_u, acc_d, st_g, st_u, st_d, pair_gu, pair_d, zeros,
             send_sems, recv_sems, local_sems, pair_sems, zero_sems):
        f, i = pl.program_id(0), pl.program_id(1)
        x, y, c_me, me = _my_pos()
        sibling = me ^ 1

        @pl.when(i == 0)
        def _():
            acc_g[...] = jnp.zeros_like(acc_g)
            acc_u[...] = jnp.zeros_like(acc_u)
            acc_d[...] = jnp.zeros_like(acc_d)

        def zero_fills():
            res = []
            for n_k, k in enumerate((2, 4, 6)):
                other = _flip(k)
                slot = _slot((other[0], other[1], 1 - c_me))
                res += [pltpu.make_async_copy(zeros, rgu_ref.at[slot, layer, pl.ds(0, half)], zero_sems.at[n_k, 0]),
                        pltpu.make_async_copy(zeros, rgu_ref.at[slot, layer, pl.ds(half, half)], zero_sems.at[n_k, 1]),
                        pltpu.make_async_copy(zeros, rd_ref.at[slot, layer], zero_sems.at[n_k, 2])]
            return res

        @pl.when((f == 0) & (i == 0))
        def _():
            zeros[...] = jnp.zeros_like(zeros)
            for cp in zero_fills():
                cp.start()

        a = _swiglu_bf16(g_ref[...], u_ref[...])[0]
        hv = h_ref[...]
        acc_g[...] += _mm_tn(dg_ref[...], hv)
        acc_u[...] += _mm_tn(du_ref[...], hv)
        acc_d[...] += _mm_tn(a, dd_ref[...])

        def messages(fs):
            c = chunk_of(fs)
            lo, hi = pl.ds(0, half), pl.ds(half, half)
            return [(st_g.at[fs], pair_gu.at[fs // 2, 0], rgu_ref, 0, c, 0),
                    (st_u.at[fs], pair_gu.at[fs // 2, 1], rgu_ref, 0, c + n_f, 1),
                    (st_d.at[fs, lo], pair_d.at[fs], rd_ref, 1, 2 * c, 2),
                    (st_d.at[fs, hi], pair_d.at[fs], rd_ref, 1, 2 * c + 1, 3)]

        def roles(p):
            same_chip = (p >> 1) == (me >> 1)
            same_c = (p & 1) == c_me
            return p == me, p == sibling, (~same_chip) & same_c, (~same_chip) & (~same_c)

        def to_owner(fs, msg, src_dev):
            src, _, buf, row, p, j = msg
            return pltpu.make_async_remote_copy(
                src_ref=src, dst_ref=buf.at[src_dev, layer], send_sem=send_sems.at[fs, j],
                recv_sem=recv_sems.at[row, src_dev], device_id=_dev(p), device_id_type=MESH)

        def to_pair(fs, msg):
            src, pair, _, _, _, j = msg
            return pltpu.make_async_remote_copy(
                src_ref=src, dst_ref=pair, send_sem=send_sems.at[fs, j], recv_sem=pair_sems.at[fs, j],
                device_id=_dev(sibling), device_id_type=MESH)

        def local(fs, msg):
            src, _, buf, _, p, j = msg
            return pltpu.make_async_copy(src, buf.at[p, layer], local_sems.at[fs, j])

        for fs in range(n_f):
            @pl.when((f == fs) & (i == n_t - 1))
            def _():
                st_g[fs] = acc_g[...].astype(BF16)
                st_u[fs] = acc_u[...].astype(BF16)
                st_d[fs] = acc_d[...].astype(BF16)
                msgs = messages(fs)
                for msg in msgs:
                    mine, sib, _, hand_over = roles(msg[4])

                    @pl.when(mine)
                    def _():
                        local(fs, msg).start()

                    @pl.when(sib)
                    def _():
                        to_owner(fs, msg, me).start()

                    @pl.when(hand_over)
                    def _():
                        to_pair(fs, msg).start()
                for msg in msgs:
                    @pl.when(roles(msg[4])[2])
                    def _():
                        src, pair = msg[0], msg[1]
                        to_pair(fs, msg).wait_recv()
                        src[...] = (src[...].astype(F32) + pair[...].astype(F32)).astype(BF16)
                        to_owner(fs, msg, me).start()

        @pl.when((f == n_f - 1) & (i == n_t - 1))
        def _():
            for fs in range(n_f):
                for msg in messages(fs):
                    mine = roles(msg[4])[0]

                    @pl.when(mine)
                    def _():
                        local(fs, msg).wait()

                    @pl.when(~mine)
                    def _():
                        to_owner(fs, msg, me).wait_send()
            for k in (1, 2, 4, 6):
                src_dev = _slot(_flip(k))
                to_owner(0, messages(0)[0], src_dev).wait_recv()
                to_owner(0, messages(0)[2], src_dev).wait_recv()
            for cp in zero_fills():
                cp.wait()

    row = pl.BlockSpec((ts, d), lambda f, i: (i, 0))
    act = pl.BlockSpec((None, ts, fc), lambda f, i: (chunk_of(f), i, 0))
    return _pcall(
        body, (h, dd, g_s, u_s, dg, du, recv_gu, recv_d), name=name, grid=(n_f, n_t),
        in_specs=[row, row, act, act, act, act, ANY, ANY], out_specs=[ANY, ANY],
        out_shape=[SDS(recv_gu.shape, recv_gu.dtype), SDS(recv_d.shape, recv_d.dtype)],
        scratch_shapes=[pltpu.VMEM((fc, d), F32)] * 3 + [pltpu.VMEM((n_f, fc, d), BF16)] * 3
        + [pltpu.VMEM((n_f // 2, 2, fc, d), BF16), pltpu.VMEM((n_f, half, d), BF16), pltpu.VMEM((half, d), BF16)]
        + [pltpu.SemaphoreType.DMA((n_f, 4)), pltpu.SemaphoreType.DMA((2, N_DEV)), pltpu.SemaphoreType.DMA((n_f, 4)),
           pltpu.SemaphoreType.DMA((n_f, 4)), pltpu.SemaphoreType.DMA((3, 3))],
        carry=carry, body_aliases={6: 0, 7: 1})


_PROJ_WIDTHS = (W_A, W_A, W_B, KV_W, KV_W, 2 * W_C)


def _mix_in_fwd(x, pre_g, w_in_t, name, carry=()):
    s, d = x.shape
    ts = 2 * _time_tile(s)

    def body(x_ref, pg_ref, w_ref, hn_ref, *outs):
        xv = x_ref[...]
        hn = (xv * _rms_r(xv) * pg_ref[...]).astype(BF16)
        hn_ref[...] = hn
        proj = _mm_nt(hn, w_ref[...])
        off = 0
        for o_ref, w in zip(outs, _PROJ_WIDTHS):
            o_ref[...] = proj[:, off:off + w]
            off += w

    row = lambda w: pl.BlockSpec((ts, w), lambda i: (i, 0))
    return _pcall(
        body, (x, pre_g, w_in_t), name=name, grid=(s // ts,),
        in_specs=[row(d), pl.BlockSpec((1, d), lambda i: (0, 0)), pl.BlockSpec((D_IN_PROJ, d), lambda i: (0, 0))],
        out_specs=[row(d)] + [row(w) for w in _PROJ_WIDTHS],
        out_shape=[SDS((s, d), BF16)] + [SDS((s, w), F32) for w in _PROJ_WIDTHS], carry=carry)


def _mix_in_bwd(dres, x, pre_g, hn, w_in_t, dlx, dlg, dq, dk, dk_up, dv, dv_up, dglu, name, carry=()):
    s, d = x.shape
    ts = _time_tile(s)
    n_t = s // ts

    def body(dres_ref, x_ref, pg_ref, hn_ref, w_ref, dlx_ref, dlg_ref, dq_ref, dk_ref, dkn_ref,
             dv_ref, dvn_ref, dglu_ref, dx_ref, dw_ref, dpg_ref, acc):
        i = pl.program_id(0)

        @pl.when(i == 0)
        def _():
            acc[...] = jnp.zeros_like(acc)
            dpg_ref[...] = jnp.zeros_like(dpg_ref)

        def with_next(cur_ref, nxt_ref):
            nxt = jnp.where(i < n_t - 1, nxt_ref[...], 0.0)
            if ts == BLK:
                return cur_ref[...] + nxt
            return jnp.concatenate([cur_ref[:ts - BLK, :], cur_ref[ts - BLK:, :] + nxt], axis=0)

        dproj = jnp.concatenate([dlx_ref[...], dlg_ref[...], dq_ref[...], with_next(dk_ref, dkn_ref),
                                 with_next(dv_ref, dvn_ref), dglu_ref[...]], axis=1).astype(BF16)
        dhn = _mm(dproj, w_ref[...])
        acc[...] += _mm_tn(dproj, hn_ref[...])
        xv = x_ref[...]
        dxv, dp = _rms_bwd(xv, _rms_r(xv), pg_ref[...], dhn)
        dpg_ref[...] += dp
        dx_ref[...] = dres_ref[...] + dxv

        @pl.when(i == n_t - 1)
        def _():
            dw_ref[...] = acc[...].astype(BF16)

    row = lambda w: pl.BlockSpec((ts, w), lambda i: (i, 0))
    nxt = pl.BlockSpec((BLK, KV_W), lambda i: (jnp.minimum(i + 1, n_t - 1), 0))
    vec = pl.BlockSpec((1, d), lambda i: (0, 0))
    full = pl.BlockSpec((D_IN_PROJ, d), lambda i: (0, 0))
    return _pcall(
        body, (dres, x, pre_g, hn, w_in_t, dlx, dlg, dq, dk, dk_up, dv, dv_up, dglu), name=name, grid=(n_t,),
        in_specs=[row(d), row(d), vec, row(d), full, row(W_A), row(W_A), row(W_B), row(KV_W), nxt,
                  row(KV_W), nxt, row(2 * W_C)],
        out_specs=[row(d), full, vec],
        out_shape=[SDS((s, d), F32), SDS((D_IN_PROJ, d), BF16), SDS((1, d), F32)],
        scratch_shapes=[pltpu.VMEM((D_IN_PROJ, d), F32)], carry=carry)


def _lru_gates(xc, lru_p):
    cw_ref, cb_ref, wa_ref, ba_ref, wx_ref, bx_ref, lam_ref = lru_p
    c = cb_ref[...]
    for j in range(LRU_K):
        c = c + cw_ref[j:j + 1, :] * _shift_down(xc, LRU_K - 1 - j)[LRU_HALO:, :]
    r = _sigmoid(_mm(c, wa_ref[...]) + ba_ref[...])
    ig = _sigmoid(_mm(c, wx_ref[...]) + bx_ref[...])
    sp = _softplus(-lam_ref[...])
    log_a = -LRU_C * r * sp
    a = jnp.exp(log_a)
    m = jnp.sqrt(_neg_expm1(2.0 * log_a))
    return c, r, ig, sp, a, m


def _lru_pspecs():
    small = lambda r: pl.BlockSpec((r, W_A), lambda i: (0, 0))
    return [small(LRU_K), small(1), small(W_A), small(1), small(W_A), small(1), small(1)]


def _lru_fwd(lx, lg, lru_p, name, carry=()):
    s = lx.shape[0]
    ts = _time_tile(s)
    n8 = ts // LRU_HALO

    def body(lx_ref, lxp_ref, lg_ref, *rest):
        lru_p, (ya_ref, h_ref, hcarry) = rest[:7], rest[7:]
        i = pl.program_id(0)
        prev = jnp.where(i > 0, lxp_ref[...], 0.0)
        xc = jnp.concatenate([prev, lx_ref[...]], axis=0)
        c, r, ig, sp, a, m = _lru_gates(xc, lru_p)
        acc_a, acc_b = a, m * (ig * c)
        t = lax.broadcasted_iota(jnp.int32, a.shape, 0)
        k = 1
        while k < ts:
            keep = t >= k
            acc_b = jnp.where(keep, acc_a * _shift_down(acc_b, k) + acc_b, acc_b)
            acc_a = jnp.where(keep, acc_a * _shift_down(acc_a, k), acc_a)
            k *= 2
        h0 = jnp.where(i > 0, hcarry[...], 0.0)
        h = acc_b + acc_a * h0
        hcarry[...] = h[ts - 1:ts, :]
        h_ref[...] = h
        ya_ref[...] = _gelu(lg_ref[...])[0] * h

    row = pl.BlockSpec((ts, W_A), lambda i: (i, 0))
    prev8 = pl.BlockSpec((LRU_HALO, W_A), lambda i: (jnp.maximum(i * n8 - 1, 0), 0))
    return _pcall(
        body, (lx, lx, lg, *lru_p), name=name, grid=(s // ts,),
        in_specs=[row, prev8, row] + _lru_pspecs(), out_specs=[row, row],
        out_shape=[SDS((s, W_A), F32), SDS((s, W_A), F32)],
        scratch_shapes=[pltpu.VMEM((1, W_A), F32)], carry=carry)


def _lru_bwd(dya, lx, lg, h_s, lru_p, name, carry=()):
    s = lx.shape[0]
    ts = _time_tile(s)
    n_t = s // ts
    n8 = ts // LRU_HALO

    def body(dya_ref, lx_ref, lxp_ref, lg_ref, h_ref, hp_ref, *rest):
        lru_p = rest[:7]
        (dlx_ref, dlg_ref, dcw_ref, dcb_ref, dwa_ref, dba_ref, dwx_ref, dbx_ref, dlam_ref,
         carry_a, carry_l, carry_dc) = rest[7:]
        cw_ref, _, wa_ref, _, wx_ref, _, lam_ref = lru_p
        i = pl.program_id(0)
        first_tile = i == n_t - 1
        last_tile = i == 0

        @pl.when(i == 0)
        def _():
            for ref in (dcw_ref, dcb_ref, dwa_ref, dba_ref, dwx_ref, dbx_ref, dlam_ref):
                ref[...] = jnp.zeros_like(ref)

        prev = jnp.where(first_tile, 0.0, lxp_ref[...])
        xc = jnp.concatenate([prev, lx_ref[...]], axis=0)
        c, r, ig, sp, a, m = _lru_gates(xc, lru_p)
        h = h_ref[...]
        hcat = jnp.concatenate([jnp.where(first_tile, 0.0, hp_ref[...]), h], axis=0)
        h_m1 = _shift_down(hcat, 1)[LRU_HALO:, :]
        lg = lg_ref[...]
        ge, th = _gelu(lg)
        dya = dya_ref[...]
        dlg_ref[...] = dya * h * _dgelu(lg, th)
        dh = dya * ge
        t = lax.broadcasted_iota(jnp.int32, a.shape, 0)
        a_next = jnp.where(t < ts - 1, _shift_up(a, 1), jnp.where(last_tile, 0.0, carry_a[...]))
        acc_a, acc_b = a_next, dh
        k = 1
        while k < ts:
            keep = t < ts - k
            acc_b = jnp.where(keep, acc_a * _shift_up(acc_b, k) + acc_b, acc_b)
            acc_a = jnp.where(keep, acc_a * _shift_up(acc_a, k), acc_a)
            k *= 2
        lam_beyond = jnp.where(last_tile, 0.0, carry_l[...])
        lmb = acc_b + acc_a * lam_beyond
        carry_a[...] = a[0:1, :]
        carry_l[...] = lmb[0:1, :]
        gi = ig * c
        dgi = lmb * m
        dla = lmb * h_m1 * a - (lmb * gi) * (a * a) / m
        dr = dla * (-LRU_C * sp)
        dsp = jnp.sum(dla * (-LRU_C * r), axis=0, keepdims=True)
        dlam_ref[...] += -dsp * _sigmoid(-lam_ref[...])
        dra = dr * r * (1.0 - r)
        dia = dgi * c * ig * (1.0 - ig)
        dc = dgi * ig + _mm_nt(dra, wa_ref[...]) + _mm_nt(dia, wx_ref[...])
        dwa_ref[...] += _mm_tn(c, dra)
        dwx_ref[...] += _mm_tn(c, dia)
        dba_ref[...] += jnp.sum(dra, axis=0, keepdims=True)
        dbx_ref[...] += jnp.sum(dia, axis=0, keepdims=True)
        dcb_ref[...] += jnp.sum(dc, axis=0, keepdims=True)
        dcc = jnp.concatenate([dc, jnp.where(last_tile, 0.0, carry_dc[...])], axis=0)
        carry_dc[...] = dc[0:LRU_HALO, :]
        dlx = jnp.zeros_like(dc)
        for j in range(LRU_K):
            sh = LRU_K - 1 - j
            dcw_ref[j:j + 1, :] += jnp.sum(dc * _shift_down(xc, sh)[LRU_HALO:, :], axis=0, keepdims=True)
            dlx = dlx + cw_ref[j:j + 1, :] * _shift_up(dcc, sh)[:ts, :]
        dlx_ref[...] = dlx

    row = pl.BlockSpec((ts, W_A), lambda i: (n_t - 1 - i, 0))
    prev8 = pl.BlockSpec((LRU_HALO, W_A), lambda i: (jnp.maximum((n_t - 1 - i) * n8 - 1, 0), 0))
    small = lambda r: pl.BlockSpec((r, W_A), lambda i: (0, 0))
    return _pcall(
        body, (dya, lx, lx, lg, h_s, h_s, *lru_p), name=name, grid=(n_t,),
        in_specs=[row, row, prev8, row, row, prev8] + _lru_pspecs(),
        out_specs=[row, row, small(LRU_K), small(1), small(W_A), small(1), small(W_A), small(1), small(1)],
        out_shape=[SDS((s, W_A), F32), SDS((s, W_A), F32), SDS((LRU_K, W_A), F32), SDS((1, W_A), F32),
                   SDS((W_A, W_A), F32), SDS((1, W_A), F32), SDS((W_A, W_A), F32), SDS((1, W_A), F32),
                   SDS((1, W_A), F32)],
        scratch_shapes=[pltpu.VMEM((1, W_A), F32), pltpu.VMEM((1, W_A), F32), pltpu.VMEM((LRU_HALO, W_A), F32)],
        carry=carry)


_ATT_ROWS = N_Q_HEADS * BLK
_GRP_ROWS = Q_PER_KV * BLK


def _attn_stack(ref, rows, g):
    return jnp.concatenate([ref[rows, h * HEAD_DIM:(h + 1) * HEAD_DIM]
                            for h in range(g * Q_PER_KV, (g + 1) * Q_PER_KV)], axis=0)


def _attn_unstack(parts):
    return jnp.concatenate([p[j * BLK:(j + 1) * BLK, :] for p in parts for j in range(Q_PER_KV)], axis=1)


def _grp(x, g):
    return x[:, g * _GRP_ROWS:(g + 1) * _GRP_ROWS]


def _attn_block(q_ref, k_ref, kp_ref, v_ref, vp_ref, sink_row, i, b):
    rows, prev = slice(b * BLK, (b + 1) * BLK), slice((b - 1) * BLK, b * BLK)
    qs, kcs, kps, vcs, vps = [], [], [], [], []
    for g in range(N_KV_HEADS):
        cols = slice(g * HEAD_DIM, (g + 1) * HEAD_DIM)
        qs.append(_attn_stack(q_ref, rows, g))
        kcs.append(k_ref[rows, cols])
        vcs.append(v_ref[rows, cols])
        kps.append(kp_ref[:, cols] if b == 0 else k_ref[prev, cols])
        vps.append(vp_ref[:, cols] if b == 0 else v_ref[prev, cols])
    scale = 1.0 / math.sqrt(HEAD_DIM)
    sc = jnp.concatenate([_mm_nt(kcs[g], qs[g]) for g in range(N_KV_HEADS)], axis=1) * scale
    sp = jnp.concatenate([_mm_nt(kps[g], qs[g]) for g in range(N_KV_HEADS)], axis=1) * scale
    kj = lax.broadcasted_iota(jnp.int32, (BLK, _ATT_ROWS), 0)
    qi = lax.broadcasted_iota(jnp.int32, (BLK, _ATT_ROWS), 1) & (BLK - 1)
    sc = jnp.where(kj <= qi, sc, NEG_BIG)
    sp = jnp.where((kj > qi) if b > 0 else ((kj > qi) & (i > 0)), sp, NEG_BIG)
    m = jnp.maximum(jnp.maximum(jnp.max(sc, axis=0, keepdims=True), jnp.max(sp, axis=0, keepdims=True)), sink_row)
    pc = jnp.exp(sc - m)
    pp = jnp.exp(sp - m)
    es = jnp.exp(sink_row - m)
    inv = 1.0 / (jnp.sum(pc, axis=0, keepdims=True) + jnp.sum(pp, axis=0, keepdims=True) + es)
    return qs, kcs, kps, vcs, vps, pc * inv, pp * inv, es * inv


def _attn_specs(s, ts):
    bpt = ts // BLK
    tile = lambda w: pl.BlockSpec((ts, w), lambda i: (i, 0))
    prv = pl.BlockSpec((BLK, KV_W), lambda i: (jnp.maximum(i * bpt - 1, 0), 0))
    sink = pl.BlockSpec((1, _ATT_ROWS), lambda i: (0, 0))
    return bpt, tile, prv, sink


def _attn_fwd(q, k, v, sink_row, name, carry=()):
    s = q.shape[0]
    ts = _time_tile(s)
    bpt, tile, prv, sink = _attn_specs(s, ts)

    def body(q_ref, k_ref, kp_ref, v_ref, vp_ref, sk_ref, y_ref):
        i = pl.program_id(0)
        for b in range(bpt):
            _, _, _, vcs, vps, pc, pp, _ = _attn_block(q_ref, k_ref, kp_ref, v_ref, vp_ref, sk_ref[...], i, b)
            outs = [_mm_tn(_grp(pc, g), vcs[g]) + _mm_tn(_grp(pp, g), vps[g]) for g in range(N_KV_HEADS)]
            y_ref[b * BLK:(b + 1) * BLK, :] = _attn_unstack(outs)

    return _pcall(
        body, (q, k, k, v, v, sink_row), name=name, grid=(s // ts,),
        in_specs=[tile(W_B), tile(KV_W), prv, tile(KV_W), prv, sink],
        out_specs=[tile(W_B)], out_shape=[SDS((s, W_B), F32)], carry=carry)


def _attn_bwd(dy, q, k, v, sinks, name, carry=()):
    s = q.shape[0]
    ts = _time_tile(s)
    n_t = s // ts
    bpt, tile, prv, sink = _attn_specs(s, ts)

    def body(dy_ref, q_ref, k_ref, kp_ref, v_ref, vp_ref, sk_ref, dq_ref, dk_ref, dv_ref, dku_ref, dvu_ref, dsk_ref):
        i = pl.program_id(0)

        @pl.when(i == 0)
        def _():
            dsk_ref[...] = jnp.zeros_like(dsk_ref)

        scale = 1.0 / math.sqrt(HEAD_DIM)
        groups = range(N_KV_HEADS)
        head_row = lax.broadcasted_iota(jnp.int32, (N_Q_HEADS, BLK), 0)
        dsk = jnp.zeros((N_Q_HEADS, BLK), F32)
        dk_blocks, dv_blocks = [], []
        for b in range(bpt):
            rows = slice(b * BLK, (b + 1) * BLK)
            qs, kcs, kps, vcs, vps, pc, pp, ps = _attn_block(q_ref, k_ref, kp_ref, v_ref, vp_ref, sk_ref[...], i, b)
            dos = [_attn_stack(dy_ref, rows, g) for g in groups]
            dpc = jnp.concatenate([_mm_nt(vcs[g], dos[g]) for g in groups], axis=1)
            dpp = jnp.concatenate([_mm_nt(vps[g], dos[g]) for g in groups], axis=1)
            delta = jnp.sum(pc * dpc, axis=0, keepdims=True) + jnp.sum(pp * dpp, axis=0, keepdims=True)
            dsc = pc * (dpc - delta) * scale
            dsp = pp * (dpp - delta) * scale
            dq_ref[rows, :] = _attn_unstack([_mm_tn(_grp(dsc, g), kcs[g]) + _mm_tn(_grp(dsp, g), kps[g])
                                             for g in groups])
            dk_blocks.append(jnp.concatenate([_mm(_grp(dsc, g), qs[g]) for g in groups], axis=1))
            dv_blocks.append(jnp.concatenate([_mm(_grp(pc, g), dos[g]) for g in groups], axis=1))
            dkp = jnp.concatenate([_mm(_grp(dsp, g), qs[g]) for g in groups], axis=1)
            dvp = jnp.concatenate([_mm(_grp(pp, g), dos[g]) for g in groups], axis=1)
            if b == 0:
                dku_ref[...] = dkp
                dvu_ref[...] = dvp
            else:
                dk_blocks[b - 1] = dk_blocks[b - 1] + dkp
                dv_blocks[b - 1] = dv_blocks[b - 1] + dvp
            dsink = -ps * delta
            for h in range(N_Q_HEADS):
                dsk = dsk + jnp.where(head_row == h, jnp.sum(dsink[:, h * BLK:(h + 1) * BLK], axis=1, keepdims=True), 0.0)
        for b in range(bpt):
            dk_ref[b * BLK:(b + 1) * BLK, :] = dk_blocks[b]
            dv_ref[b * BLK:(b + 1) * BLK, :] = dv_blocks[b]
        dsk_ref[...] += dsk

    up = pl.BlockSpec((BLK, KV_W), lambda i: (i, 0))
    return _pcall(
        body, (dy, q, k, k, v, v, sinks), name=name, grid=(n_t,),
        in_specs=[tile(W_B), tile(W_B), tile(KV_W), prv, tile(KV_W), prv, sink],
        out_specs=[tile(W_B), tile(KV_W), tile(KV_W), up, up, pl.BlockSpec((N_Q_HEADS, BLK), lambda i: (0, 0))],
        out_shape=[SDS((s, W_B), F32), SDS((s, KV_W), F32), SDS((s, KV_W), F32), SDS((n_t * BLK, KV_W), F32),
                   SDS((n_t * BLK, KV_W), F32), SDS((N_Q_HEADS, BLK), F32)], carry=carry)


def _cc_recompute(glu_ref, glup_ref, cw_ref, cb_ref, first_tile):
    prev = jnp.where(first_tile, 0.0, glup_ref[...])
    ge = jnp.concatenate([prev, glu_ref[...]], axis=0)
    y0 = ge[:, :W_C] * _sigmoid_t(ge[:, W_C:])
    y1 = cb_ref[...]
    for j in range(CC_K):
        y1 = y1 + cw_ref[j:j + 1, :] * _shift_down(y0, CC_K - 1 - j)[CC_HALO:, :]
    return y0, y1


def _ln_stats(y1):
    mu = jnp.mean(y1, axis=-1, keepdims=True)
    xc = y1 - mu
    rstd = lax.rsqrt(jnp.mean(xc * xc, axis=-1, keepdims=True) + LN_EPS)
    return xc * rstd, rstd


def _cc_specs(s, ts):
    n32 = ts // CC_HALO
    row = lambda w: pl.BlockSpec((ts, w), lambda i: (i, 0))
    prev = pl.BlockSpec((CC_HALO, 2 * W_C), lambda i: (jnp.maximum(i * n32 - 1, 0), 0))
    small = lambda r: pl.BlockSpec((r, W_C), lambda i: (0, 0))
    return row, prev, small


def _cc_fwd(glu, cw, cb, lng, lnb, name, carry=()):
    s = glu.shape[0]
    ts = _time_tile(s)
    row, prev, small = _cc_specs(s, ts)

    def body(glu_ref, glup_ref, cw_ref, cb_ref, lng_ref, lnb_ref, y_ref):
        _, y1 = _cc_recompute(glu_ref, glup_ref, cw_ref, cb_ref, pl.program_id(0) == 0)
        xhat, _ = _ln_stats(y1)
        z = xhat * lng_ref[...] + lnb_ref[...]
        y_ref[...] = z * _sigmoid_t(z)

    return _pcall(
        body, (glu, glu, cw, cb, lng, lnb), name=name, grid=(s // ts,),
        in_specs=[row(2 * W_C), prev, small(CC_HALO), small(1), small(1), small(1)],
        out_specs=[row(W_C)], out_shape=[SDS((s, W_C), F32)], carry=carry)


def _cc_bwd_conv(dy, glu, cw, cb, lng, lnb, name, carry=()):
    s = glu.shape[0]
    ts = _time_tile(s)
    row, prev, small = _cc_specs(s, ts)

    def body(dy_ref, glu_ref, glup_ref, cw_ref, cb_ref, lng_ref, lnb_ref, dy1_ref, dcw_ref, dcb_ref, dlng_ref, dlnb_ref):
        i = pl.program_id(0)

        @pl.when(i == 0)
        def _():
            for ref in (dcw_ref, dcb_ref, dlng_ref, dlnb_ref):
                ref[...] = jnp.zeros_like(ref)

        y0, y1 = _cc_recompute(glu_ref, glup_ref, cw_ref, cb_ref, i == 0)
        xhat, rstd = _ln_stats(y1)
        z = xhat * lng_ref[...] + lnb_ref[...]
        dz = dy_ref[...] * _dsilu(z, _sigmoid_t(z))
        dlng_ref[...] += jnp.sum(dz * xhat, axis=0, keepdims=True)
        dlnb_ref[...] += jnp.sum(dz, axis=0, keepdims=True)
        dxh = dz * lng_ref[...]
        dy1 = rstd * (dxh - jnp.mean(dxh, axis=-1, keepdims=True) - xhat * jnp.mean(dxh * xhat, axis=-1, keepdims=True))
        dy1_ref[...] = dy1
        dcb_ref[...] += jnp.sum(dy1, axis=0, keepdims=True)
        for j in range(CC_K):
            dcw_ref[j:j + 1, :] += jnp.sum(dy1 * _shift_down(y0, CC_K - 1 - j)[CC_HALO:, :], axis=0, keepdims=True)

    return _pcall(
        body, (dy, glu, glu, cw, cb, lng, lnb), name=name, grid=(s // ts,),
        in_specs=[row(W_C), row(2 * W_C), prev, small(CC_HALO), small(1), small(1), small(1)],
        out_specs=[row(W_C), small(CC_HALO), small(1), small(1), small(1)],
        out_shape=[SDS((s, W_C), F32), SDS((CC_HALO, W_C), F32)] + [SDS((1, W_C), F32)] * 3, carry=carry)


def _cc_bwd_glu(dy1, glu, cw, name, carry=()):
    s = glu.shape[0]
    ts = _time_tile(s)
    n_t = s // ts
    n32 = ts // CC_HALO

    def body(dy1_ref, dyn_ref, glu_ref, cw_ref, dglu_ref):
        i = pl.program_id(0)
        dcat = jnp.concatenate([dy1_ref[...], jnp.where(i < n_t - 1, dyn_ref[...], 0.0)], axis=0)
        dy0 = jnp.zeros((ts, W_C), F32)
        for j in range(CC_K):
            dy0 = dy0 + cw_ref[j:j + 1, :] * _shift_up(dcat, CC_K - 1 - j)[:ts, :]
        a = glu_ref[:, :W_C]
        sg = _sigmoid_t(glu_ref[:, W_C:])
        dglu_ref[...] = jnp.concatenate([dy0 * sg, dy0 * a * sg * (1.0 - sg)], axis=1)

    row = lambda w: pl.BlockSpec((ts, w), lambda i: (i, 0))
    nxt = pl.BlockSpec((CC_HALO, W_C), lambda i: (jnp.minimum((i + 1) * n32, s // CC_HALO - 1), 0))
    return _pcall(
        body, (dy1, dy1, glu, cw), name=name, grid=(n_t,),
        in_specs=[row(W_C), nxt, row(2 * W_C), pl.BlockSpec((CC_HALO, W_C), lambda i: (0, 0))],
        out_specs=[row(2 * W_C)], out_shape=[SDS((s, 2 * W_C), F32)], carry=carry)


_MIX_OFFS = ((0, W_A), (W_A, W_A + W_B), (W_A + W_B, W_A + W_B + W_C))


def _mix_out_fwd(x, ya, yb, yc, group_g, w_out, post_g, name, carry=()):
    s, d = x.shape
    ts = 2 * _time_tile(s)
    dm = w_out.shape[0]

    def body(x_ref, ya_ref, yb_ref, yc_ref, gg_ref, w_ref, qg_ref, xo_ref, o_ref):
        parts = []
        for y_ref, (lo, hi) in zip((ya_ref, yb_ref, yc_ref), _MIX_OFFS):
            yv = y_ref[...]
            parts.append(yv * _rms_r(yv) * gg_ref[:, lo:hi])
        o = _mm(jnp.concatenate(parts, axis=1), w_ref[...])
        o_ref[...] = o
        xo_ref[...] = x_ref[...] + o * _rms_r(o) * qg_ref[...]

    row = lambda w: pl.BlockSpec((ts, w), lambda i: (i, 0))
    return _pcall(
        body, (x, ya, yb, yc, group_g, w_out, post_g), name=name, grid=(s // ts,),
        in_specs=[row(d), row(W_A), row(W_B), row(W_C), pl.BlockSpec((1, dm), lambda i: (0, 0)),
                  pl.BlockSpec((dm, d), lambda i: (0, 0)), pl.BlockSpec((1, d), lambda i: (0, 0))],
        out_specs=[row(d), row(d)], out_shape=[SDS((s, d), F32), SDS((s, d), F32)], carry=carry)


def _mix_out_bwd(dxo, o, ya, yb, yc, group_g, w_out, post_g, name, carry=()):
    s, d = o.shape
    ts = 2 * _time_tile(s)
    n_t = s // ts
    dm = w_out.shape[0]

    def body(dxo_ref, o_ref, ya_ref, yb_ref, yc_ref, gg_ref, w_ref, qg_ref,
             dya_ref, dyb_ref, dyc_ref, dw_ref, dqg_ref, dgg_ref, acc):
        i = pl.program_id(0)

        @pl.when(i == 0)
        def _():
            acc[...] = jnp.zeros_like(acc)
            dqg_ref[...] = jnp.zeros_like(dqg_ref)
            dgg_ref[...] = jnp.zeros_like(dgg_ref)

        ov = o_ref[...]
        do, dq = _rms_bwd(ov, _rms_r(ov), qg_ref[...], dxo_ref[...])
        dqg_ref[...] += dq
        do = do.astype(BF16)
        dyn = _mm_nt(do, w_ref[...])
        parts, dggs = [], []
        for y_ref, dy_ref, (lo, hi) in zip((ya_ref, yb_ref, yc_ref), (dya_ref, dyb_ref, dyc_ref), _MIX_OFFS):
            yv = y_ref[...]
            r = _rms_r(yv)
            gg = gg_ref[:, lo:hi]
            parts.append(yv * r * gg)
            dyv, dg = _rms_bwd(yv, r, gg, dyn[:, lo:hi])
            dy_ref[...] = dyv
            dggs.append(dg)
        dgg_ref[...] += jnp.concatenate(dggs, axis=1)
        acc[...] += _mm_tn(jnp.concatenate(parts, axis=1), do)

        @pl.when(i == n_t - 1)
        def _():
            dw_ref[...] = acc[...].astype(BF16)

    row = lambda w: pl.BlockSpec((ts, w), lambda i: (i, 0))
    full = pl.BlockSpec((dm, d), lambda i: (0, 0))
    held = pl.BlockSpec((dm, d), lambda i: (0, 0), pipeline_mode=pl.Buffered(1))
    return _pcall(
        body, (dxo, o, ya, yb, yc, group_g, w_out, post_g), name=name, grid=(n_t,),
        in_specs=[row(d), row(d), row(W_A), row(W_B), row(W_C), pl.BlockSpec((1, dm), lambda i: (0, 0)), held,
                  pl.BlockSpec((1, d), lambda i: (0, 0))],
        out_specs=[row(W_A), row(W_B), row(W_C), full, pl.BlockSpec((1, d), lambda i: (0, 0)),
                   pl.BlockSpec((1, dm), lambda i: (0, 0))],
        out_shape=[SDS((s, W_A), F32), SDS((s, W_B), F32), SDS((s, W_C), F32), SDS((dm, d), BF16),
                   SDS((1, d), F32), SDS((1, dm), F32)],
        scratch_shapes=[pltpu.VMEM((dm, d), F32)], carry=carry)


def _adamw_math(w, g, m, v):
    m = ADAM_B1 * m + (1.0 - ADAM_B1) * g
    v = ADAM_B2 * v + (1.0 - ADAM_B2) * (g * g)
    m_hat = m / (1.0 - ADAM_B1 ** ADAM_STEP)
    v_hat = v / (1.0 - ADAM_B2 ** ADAM_STEP)
    delta = -ADAM_LR * (m_hat / (jnp.sqrt(v_hat) + ADAM_EPS) + ADAM_WD * w)
    return delta, m, v


def _row_tile(rows, cap=512):
    best = None
    for t in range(16, min(rows, cap) + 1, 16):
        if rows % t == 0:
            best = t
    return best if best is not None else rows


def _reduce_adamw(recv, w, m, v, name):
    n_l, r, c = w.shape
    tr = _row_tile(r)

    def body(recv_ref, w_ref, m_ref, v_ref, g_ref, d_ref, nm_ref, nv_ref):
        g = recv_ref[0].astype(F32)
        for p in range(1, N_DEV):
            g = g + recv_ref[p].astype(F32)
        g_ref[...] = g
        d_ref[...], nm_ref[...], nv_ref[...] = _adamw_math(w_ref[...], g, m_ref[...], v_ref[...])

    blk = pl.BlockSpec((None, tr, c), lambda l, i: (l, i, 0))
    return _pcall(
        body, (recv, w, m, v), name=name, grid=(n_l, r // tr),
        in_specs=[pl.BlockSpec((N_DEV, None, tr, c), lambda l, i: (0, l, i, 0)), blk, blk, blk],
        out_specs=[blk] * 4, out_shape=[SDS(w.shape, F32)] * 4)[0]


def _reduce_adamw_small(parts, w, m, v, name):
    def body(p_ref, w_ref, m_ref, v_ref, g_ref, d_ref, nm_ref, nv_ref):
        g = p_ref[0]
        for p in range(1, N_DEV):
            g = g + p_ref[p]
        g_ref[...] = g
        d_ref[...], nm_ref[...], nv_ref[...] = _adamw_math(w_ref[...], g, m_ref[...], v_ref[...])

    vm = pl.BlockSpec(memory_space=pltpu.VMEM)
    return pl.pallas_call(body, name=name, in_specs=[vm] * 4, out_specs=[vm] * 4, out_shape=[SDS(w.shape, F32)] * 4,
                          compiler_params=pltpu.CompilerParams(vmem_limit_bytes=VMEM_LIMIT))(parts, w, m, v)


def _rows_of(shape):
    return -(-math.prod(shape) // (8 * BLK)) * 8


def _pack(arrs):
    rows = []
    for a in arrs:
        n, r = math.prod(a.shape), _rows_of(a.shape)
        if n % BLK == 0:
            part = a.reshape(n // BLK, BLK)
            rows.append(part if n // BLK == r else jnp.pad(part, ((0, r - n // BLK), (0, 0))))
        else:
            rows.append(jnp.pad(a.reshape(-1), (0, r * BLK - n)).reshape(r, BLK))
    return jnp.concatenate(rows, axis=0)


def _unpack(packed, shapes):
    out, row = [], 0
    for shp in shapes:
        n, r = math.prod(shp), _rows_of(shp)
        if n % BLK == 0:
            out.append(packed[row:row + n // BLK].reshape(shp))
        else:
            out.append(packed[row:row + r].reshape(-1)[:n].reshape(shp))
        row += r
    return out


def _block_diag(w):
    nb, bw, _ = w.shape
    eye = jnp.eye(nb, dtype=w.dtype)
    return (eye[:, None, :, None] * w[:, :, None, :]).reshape(nb * bw, nb * bw)


def _diag_blocks(wd, nb):
    bw = wd.shape[0] // nb
    return jnp.stack([wd[b * bw:(b + 1) * bw, b * bw:(b + 1) * bw] for b in range(nb)])


WEIGHT_NAMES = ['ffn1_pre_g', 'ffn1_w_gu', 'ffn1_w_down', 'ffn1_post_g', 'mix_pre_g', 'w_in', 'lru_conv_w', 'lru_conv_b',
                'lru_w_a', 'lru_b_a', 'lru_w_x', 'lru_b_x', 'lru_lambda', 'attn_sinks', 'conv_w', 'conv_b', 'conv_ln_g',
                'conv_ln_b', 'group_g', 'w_out', 'mix_post_g', 'ffn2_pre_g', 'ffn2_w_gu', 'ffn2_w_down', 'ffn2_post_g']
BIG = ('ffn1_w_gu', 'ffn1_w_down', 'w_in', 'w_out', 'ffn2_w_gu', 'ffn2_w_down')
TRANSPOSED = ('ffn1_w_gu', 'ffn2_w_gu', 'w_in')
SMALL = tuple(k for k in WEIGHT_NAMES if k not in BIG)
CHANNEL_SHARDED = ('lru_conv_w', 'conv_w')


def _step(x, target, w, m, v):
    n_l = w['ffn1_pre_g'].shape[0]
    assert n_l == 2, "the exchange schedule below is laid out for two layers"
    s, d = x.shape[1], x.shape[2]
    x = x.reshape(s, d)
    target = target.reshape(s, d)
    me = _my_pos()[3]
    tview = lambda t, k: jnp.swapaxes(t[k], 1, 2) if k in TRANSPOSED else t[k]
    wb = {k: tview(w, k).astype(BF16) for k in BIG}
    vec = lambda name, l: w[name][l][None, :]

    conv_shard = _pack([w['lru_conv_w'], w['conv_w']])
    g0 = _all_gather([(wb['ffn1_w_gu'], 0), (wb['ffn1_w_down'], 0), (wb['w_in'], 0), (wb['w_out'], 0),
                      (conv_shard, None)], "all_gather_first")
    wts = [dict(), dict()]
    wts[0]['ffn1_w_gu'], wts[0]['ffn1_w_down'], wts[0]['w_in'], wts[0]['w_out'], conv_g = g0
    ch = W_A // N_DEV
    conv_parts = [_unpack(conv_g[p], [(n_l, LRU_K, ch), (n_l, CC_K, ch)]) for p in range(N_DEV)]
    lru_cw = jnp.concatenate([cp[0] for cp in conv_parts], axis=-1)
    cc_cw = jnp.concatenate([cp[1] for cp in conv_parts], axis=-1)
    cc_cw = jnp.pad(cc_cw, ((0, 0), (0, CC_HALO - CC_K), (0, 0)))

    fc = wb['ffn1_w_gu'].shape[1]
    cut1, cut2 = (fc * 4 // 11 + 15) // 16 * 16, (fc * 27 // 44 + 15) // 16 * 16
    gather_plan = {
        ('ffn1', 0): [('A', 'f2_0', ('ffn2_w_gu', 'ffn2_w_down'), 0)],
        ('mix_in', 0): [('B', 'f2_0'), ('A', 'g1_1a', ('ffn1_w_gu',), 1, (0, cut1))],
        ('lru', 0): [('A', 'g1_1b', ('ffn1_w_gu',), 1, (cut1, cut2 - cut1), 'g1_1a')],
        ('attn', 0): [('A', 'g1_1', ('ffn1_w_gu',), 1, (cut2, fc - cut2), 'g1_1b')],
        ('cconv', 0): [('B', 'g1_1')],
        ('ffn2', 0): [('D', None, ('ffn1_w_down',), 1), ('A', 'wi_1', ('w_in',), 1), ('A', 'wo_1', ('w_out',), 1)],
        ('ffn1', 1): [('A', 'f2_1', ('ffn2_w_gu', 'ffn2_w_down'), 1), ('B', 'wi_1'), ('B', 'wo_1')],
        ('mix_in', 1): [('B', 'f2_1')],
    }
    pend = {}

    def fwd(kernel_name, l, fn, *args):
        plan = gather_plan.get((kernel_name, l), [])
        carry = []
        for st in plan:
            if st[0] == 'B':
                carry.append(_gather_b(pend[st[1]][2]))
            else:
                rows = st[4] if len(st) > 4 else None
                into = pend.pop(st[5])[2] if len(st) > 5 else [None] * len(st[2])
                carry.append(_gather_a([(wb[k], st[3], rows, buf) for k, buf in zip(st[2], into)],
                                       two_level=st[0] == 'A'))
        outs, ex = fn(*args, f"{kernel_name}_fwd_l{l}", carry)
        for st, bufs in zip(plan, ex):
            if st[0] == 'A':
                pend[st[1]] = (st[2], st[3], bufs)
            else:
                names, wl = (st[2], st[3]) if st[0] == 'D' else pend.pop(st[1])[:2]
                for k, b in zip(names, bufs):
                    wts[wl][k] = b
        return outs

    saved = []
    h = x
    for l in range(n_l):
        sv = {'x0': h}
        lw = wts[l]
        x1, sv['h1'], sv['g1'], sv['u1'], sv['d1'] = fwd(
            'ffn1', l, _ffn_fwd, h, vec('ffn1_pre_g', l), vec('ffn1_post_g', l), lw['ffn1_w_gu'], lw['ffn1_w_down'])
        sv['x1'] = x1
        sv['hn'], lx, lg, q, k, vv, glu = fwd('mix_in', l, _mix_in_fwd, x1, vec('mix_pre_g', l),
                                              lw['w_in'].reshape(D_IN_PROJ, d))
        sv.update(lx=lx, lg=lg, q=q, k=k, v=vv, glu=glu)
        lru_p = (lru_cw[l], vec('lru_conv_b', l), _block_diag(w['lru_w_a'][l]).astype(BF16), vec('lru_b_a', l),
                 _block_diag(w['lru_w_x'][l]).astype(BF16), vec('lru_b_x', l), vec('lru_lambda', l))
        cc_p = (cc_cw[l], vec('conv_b', l), vec('conv_ln_g', l), vec('conv_ln_b', l))
        sv.update(lru_p=lru_p, cc_p=cc_p)
        sv['ya'], sv['hs'] = fwd('lru', l, _lru_fwd, lx, lg, lru_p)
        sv['sink_row'] = jnp.repeat(w['attn_sinks'][l], BLK)[None, :]
        (sv['yb'],) = fwd('attn', l, _attn_fwd, q, k, vv, sv['sink_row'])
        (sv['yc'],) = fwd('cconv', l, _cc_fwd, glu, *cc_p)
        x2, sv['o'] = fwd('mix_out', l, _mix_out_fwd, x1, sv['ya'], sv['yb'], sv['yc'], vec('group_g', l),
                          lw['w_out'].reshape(-1, d), vec('mix_post_g', l))
        sv['x2'] = x2
        h, sv['h2'], sv['g2'], sv['u2'], sv['d2'] = fwd(
            'ffn2', l, _ffn_fwd, x2, vec('ffn2_pre_g', l), vec('ffn2_post_g', l), lw['ffn2_w_gu'], lw['ffn2_w_down'])
        saved.append(sv)

    dh = h

    recv = {k: None for k in BIG}
    ready = {}
    small = [dict() for _ in range(n_l)]

    c_even, c_odd = tuple(range(0, N_DEV, 2)), tuple(range(1, N_DEV, 2))

    def exchange(keys):
        return _grad_x([(ready[key[:2]], key[1], recv[key[0]]) + tuple(key[2:]) for key in keys], n_l)

    def received(keys, bufs):
        for key, b in zip(keys, bufs):
            recv[key[0]] = b

    def run(fn, *args, keys=(), **kw):
        outs, ex = fn(*args, carry=[exchange(keys)] if keys else [], **kw)
        if keys:
            received(keys, ex[0])
        return outs

    for l in reversed(range(n_l)):
        sv, sg, lw = saved[l], small[l], wts[l]
        keys = [] if l == n_l - 1 else [('ffn1_w_gu', l + 1)]
        dx2, dd, dg, du, sg['ffn2_pre_g'], sg['ffn2_post_g'], *loss_rows = run(
            _ffn_bwd_act, dh, sv['d2'], sv['x2'], vec('ffn2_pre_g', l), vec('ffn2_post_g', l), sv['g2'], sv['u2'],
            lw['ffn2_w_gu'], lw['ffn2_w_down'], f"ffn2_bwd_act_l{l}", keys=keys,
            loss_target=target if l == n_l - 1 else None)
        if loss_rows:
            loss_row = loss_rows[0]
        keys = [] if l == n_l - 1 else [('ffn1_w_down', l + 1), ('w_in', l + 1, c_odd)]
        dwg, dwu, dwd = run(_ffn_bwd_w, sv['h2'], dd, sv['g2'], sv['u2'], dg, du, f"ffn2_bwd_w_l{l}", keys=keys)
        ready[('ffn2_w_gu', l)] = [dwg, dwu]
        ready[('ffn2_w_down', l)] = [dwd.reshape(N_DEV, -1, d)]
        dya, dyb, dyc, dw_out, sg['mix_post_g'], sg['group_g'] = run(
            _mix_out_bwd, dx2, sv['o'], sv['ya'], sv['yb'], sv['yc'], vec('group_g', l), lw['w_out'].reshape(-1, d),
            vec('mix_post_g', l), f"mix_out_bwd_l{l}")
        ready[('w_out', l)] = [dw_out.reshape(N_DEV, -1, d)]
        (dlx, dlg, sg['lru_conv_w'], sg['lru_conv_b'], dwa, sg['lru_b_a'], dwx, sg['lru_b_x'],
         sg['lru_lambda']) = run(_lru_bwd, dya, sv['lx'], sv['lg'], sv['hs'], sv['lru_p'], f"lru_bwd_l{l}")
        sg['lru_w_a'] = _diag_blocks(dwa, A_BLOCKS)
        sg['lru_w_x'] = _diag_blocks(dwx, A_BLOCKS)
        dq, dk, dv, dk_up, dv_up, dsk = run(_attn_bwd, dyb, sv['q'], sv['k'], sv['v'], sv['sink_row'],
                                            f"attn_bwd_l{l}", keys=[('ffn2_w_down', l, c_even)] if l == 0 else [])
        sg['attn_sinks'] = dsk[:, 0]
        dy1, dcw, sg['conv_b'], sg['conv_ln_g'], sg['conv_ln_b'] = run(
            _cc_bwd_conv, dyc, sv['glu'], *sv['cc_p'], f"cconv_bwd_conv_l{l}", keys=[('w_out', l)] if l == 0 else [])
        sg['conv_w'] = dcw[:CC_K]
        (dglu,) = run(_cc_bwd_glu, dy1, sv['glu'], sv['cc_p'][0], f"cconv_bwd_glu_l{l}")
        dx1, dw_in, sg['mix_pre_g'] = run(
            _mix_in_bwd, dx2, sv['x1'], vec('mix_pre_g', l), sv['hn'], lw['w_in'].reshape(D_IN_PROJ, d),
            dlx, dlg, dq, dk, dk_up, dv, dv_up, dglu, f"mix_in_bwd_l{l}",
            keys=[('ffn2_w_down', l, c_odd)] if l == 0 else [('w_out', l)])
        ready[('w_in', l)] = [dw_in.reshape(N_DEV, -1, d)]
        dh, dd, dg, du, sg['ffn1_pre_g'], sg['ffn1_post_g'] = run(
            _ffn_bwd_act, dx1, sv['d1'], sv['x0'], vec('ffn1_pre_g', l), vec('ffn1_post_g', l), sv['g1'], sv['u1'],
            lw['ffn1_w_gu'], lw['ffn1_w_down'], f"ffn1_bwd_act_l{l}",
            keys=[('ffn2_w_gu', l), ('w_in', l)] if l == 0 else [('ffn2_w_gu', l)])
        if l > 0:
            dwg, dwu, dwd = run(_ffn_bwd_w, sv['h1'], dd, sv['g1'], sv['u1'], dg, du, f"ffn1_bwd_w_l{l}",
                                keys=[('ffn2_w_down', l), ('w_in', l, c_even)])
            ready[('ffn1_w_gu', l)] = [dwg, dwu]
            ready[('ffn1_w_down', l)] = [dwd.reshape(N_DEV, -1, d)]
        else:
            part = _pack([jnp.stack([small[j][k] for j in range(n_l)]) for k in SMALL] + [loss_row])
            (recv['ffn1_w_gu'], recv['ffn1_w_down']), ex = _ffn_bwd_w_send(
                sv['h1'], dd, sv['g1'], sv['u1'], dg, du, recv['ffn1_w_gu'], recv['ffn1_w_down'], 0, "ffn1_bwd_w_send_l0",
                [_gather_a([(part, None)], two_level=False)])
            small_parts = ex[0][0]
    grad_x = dh.reshape(1, s, d)

    out = {}
    for k in BIG:
        res = _reduce_adamw(recv[k], tview(w, k), tview(m, k), tview(v, k), f"reduce_adamw_{k}")
        out[k] = [jnp.swapaxes(r, 1, 2) for r in res] if k in TRANSPOSED else res

    small_shapes = [(n_l,) + tuple(small[0][k].shape) for k in SMALL]

    def widen(t, k):
        if k not in CHANNEL_SHARDED:
            return t.reshape((n_l,) + tuple(small[0][k].shape))
        full = jnp.zeros((n_l,) + tuple(small[0][k].shape), F32)
        return lax.dynamic_update_slice_in_dim(full, t, me * ch, axis=2)

    no_w = jnp.zeros(loss_row.shape, F32)
    packed = [_pack([widen(src[k], k) for k in SMALL] + [no_w]) for src in (w, m, v)]
    res = _reduce_adamw_small(small_parts, *packed, "reduce_adamw_small")
    loss = _unpack(res[0], small_shapes + [loss_row.shape])[-1][0, 0]
    for k, g, dlt, nm, nv in zip(SMALL, *[_unpack(r, small_shapes) for r in res]):
        vals = [g, dlt, nm, nv]
        if k in CHANNEL_SHARDED:
            vals = [lax.dynamic_slice_in_dim(t, me * ch, ch, axis=2) for t in vals]
        out[k] = [t.reshape(w[k].shape) for t in vals]

    return (loss, grad_x, *[out[k][0] for k in WEIGHT_NAMES], *[out[k][1] for k in WEIGHT_NAMES],
            *[out[k][2] for k in WEIGHT_NAMES], *[out[k][3] for k in WEIGHT_NAMES])


def kernel(x, ffn1_pre_g, ffn1_w_gu, ffn1_w_down, ffn1_post_g, mix_pre_g, w_in, lru_conv_w, lru_conv_b, lru_w_a, lru_b_a, lru_w_x, lru_b_x, lru_lambda, attn_sinks, conv_w, conv_b, conv_ln_g, conv_ln_b, group_g, w_out, mix_post_g, ffn2_pre_g, ffn2_w_gu, ffn2_w_down, ffn2_post_g, loss_target, m_ffn1_pre_g, m_ffn1_w_gu, m_ffn1_w_down, m_ffn1_post_g, m_mix_pre_g, m_w_in, m_lru_conv_w, m_lru_conv_b, m_lru_w_a, m_lru_b_a, m_lru_w_x, m_lru_b_x, m_lru_lambda, m_attn_sinks, m_conv_w, m_conv_b, m_conv_ln_g, m_conv_ln_b, m_group_g, m_w_out, m_mix_post_g, m_ffn2_pre_g, m_ffn2_w_gu, m_ffn2_w_down, m_ffn2_post_g, v_ffn1_pre_g, v_ffn1_w_gu, v_ffn1_w_down, v_ffn1_post_g, v_mix_pre_g, v_w_in, v_lru_conv_w, v_lru_conv_b, v_lru_w_a, v_lru_b_a, v_lru_w_x, v_lru_b_x, v_lru_lambda, v_attn_sinks, v_conv_w, v_conv_b, v_conv_ln_g, v_conv_ln_b, v_group_g, v_w_out, v_mix_post_g, v_ffn2_pre_g, v_ffn2_w_gu, v_ffn2_w_down, v_ffn2_post_g):
    args = (ffn1_pre_g, ffn1_w_gu, ffn1_w_down, ffn1_post_g, mix_pre_g, w_in, lru_conv_w, lru_conv_b, lru_w_a, lru_b_a, lru_w_x, lru_b_x, lru_lambda, attn_sinks, conv_w, conv_b, conv_ln_g, conv_ln_b, group_g, w_out, mix_post_g, ffn2_pre_g, ffn2_w_gu, ffn2_w_down, ffn2_post_g)
    ms = (m_ffn1_pre_g, m_ffn1_w_gu, m_ffn1_w_down, m_ffn1_post_g, m_mix_pre_g, m_w_in, m_lru_conv_w, m_lru_conv_b, m_lru_w_a, m_lru_b_a, m_lru_w_x, m_lru_b_x, m_lru_lambda, m_attn_sinks, m_conv_w, m_conv_b, m_conv_ln_g, m_conv_ln_b, m_group_g, m_w_out, m_mix_post_g, m_ffn2_pre_g, m_ffn2_w_gu, m_ffn2_w_down, m_ffn2_post_g)
    vs = (v_ffn1_pre_g, v_ffn1_w_gu, v_ffn1_w_down, v_ffn1_post_g, v_mix_pre_g, v_w_in, v_lru_conv_w, v_lru_conv_b, v_lru_w_a, v_lru_b_a, v_lru_w_x, v_lru_b_x, v_lru_lambda, v_attn_sinks, v_conv_w, v_conv_b, v_conv_ln_g, v_conv_ln_b, v_group_g, v_w_out, v_mix_post_g, v_ffn2_pre_g, v_ffn2_w_gu, v_ffn2_w_down, v_ffn2_post_g)
    return _step(x, loss_target, dict(zip(WEIGHT_NAMES, args)), dict(zip(WEIGHT_NAMES, ms)), dict(zip(WEIGHT_NAMES, vs)))
```
